```python
import math
import jax, jax.numpy as jnp
from jax import lax
import numpy as np

D_MODEL = 2048
BATCH = 8
SEQ = 2048
DEPTH = 2

N_MIXERS = 2
CHUNK = 128
A_WIDTH = D_MODEL
A_GROUPS = 8
A_HEAD = A_WIDTH // A_GROUPS
B_WIDTH = D_MODEL
B_WINDOWS = (2, 4, 8, 16)
B_GROUPS = len(B_WINDOWS)
B_HEAD = B_WIDTH // B_GROUPS
D_FF = 4 * D_MODEL
N_A = (DEPTH + 1) // 2
N_B = DEPTH // 2
EPS = 1e-6

kernel_name = "hybrid_chunked_gmlp_multiscale_pool"


def rmsnorm(x, g):
    x32 = x.astype(jnp.float32)
    y = x32 * lax.rsqrt(jnp.mean(x32 * x32, axis=-1, keepdims=True) + EPS)
    return (y * g.astype(jnp.float32)).astype(x.dtype)


def layernorm(x, g, b):
    x32 = x.astype(jnp.float32)
    mu = jnp.mean(x32, axis=-1, keepdims=True)
    xc = x32 - mu
    y = xc * lax.rsqrt(jnp.mean(xc * xc, axis=-1, keepdims=True) + EPS)
    return (y * g.astype(jnp.float32) + b.astype(jnp.float32)).astype(x.dtype)


def chunked_gmlp_mixer(h, w_in, ln_g, ln_b, w_s, b_s, w_out):
    bsz, seq, _ = h.shape
    n_chunks = seq // CHUNK
    z = jax.nn.gelu(h @ w_in, approximate=False)
    u, v = jnp.split(z, 2, axis=-1)
    v = layernorm(v, ln_g, ln_b)
    v = v.reshape(bsz, n_chunks, CHUNK, A_GROUPS, A_HEAD)
    causal = jnp.tril(jnp.ones((CHUNK, CHUNK), dtype=w_s.dtype))
    w_masked = w_s * causal[None]
    s = jnp.einsum('gts,bcsgd->bctgd', w_masked, v) + jnp.transpose(b_s)[None, None, :, :, None]
    gated = u * s.reshape(bsz, seq, A_WIDTH)
    return gated @ w_out


def causal_window_mean(v, w):
    seq = v.shape[1]
    v32 = v.astype(jnp.float32)
    c = jnp.cumsum(v32, axis=1)
    c_prev = jnp.pad(c, ((0, 0), (w, 0), (0, 0)))[:, :seq]
    count = jnp.minimum(jnp.arange(1, seq + 1), w).astype(jnp.float32)
    return ((c - c_prev) / count[None, :, None]).astype(v.dtype)


def multiscale_pool_mixer(h, w_in, w_grp, scale, w_out):
    bsz, seq, _ = h.shape
    v = (h @ w_in).reshape(bsz, seq, B_GROUPS, B_HEAD)
    pooled = jnp.stack(
        [causal_window_mean(v[:, :, g], w) - v[:, :, g] for g, w in enumerate(B_WINDOWS)],
        axis=2)
    mixed = jnp.einsum('bsgc,gcd->bsgd', pooled, w_grp).reshape(bsz, seq, B_WIDTH)
    return (mixed * scale) @ w_out


def sqrelu_mlp(h, w1, w2):
    a = jax.nn.relu(h @ w1)
    return (a * a) @ w2


def _fwd_setup_inputs(seed: int = 0) -> dict:
    key = jax.random.key(seed)
    ks = jax.random.split(key, 20)
    f32 = jnp.float32

    def nrm(k, shape, scale):
        return jax.random.normal(k, shape, f32) * scale

    x = jax.random.normal(ks[0], (BATCH, SEQ, D_MODEL), f32)
    a_w_in = nrm(ks[1], (N_A, D_MODEL, 2 * A_WIDTH), D_MODEL ** -0.5)
    a_ln_g = 1.0 + nrm(ks[2], (N_A, A_WIDTH), 0.02)
    a_ln_b = nrm(ks[3], (N_A, A_WIDTH), 0.02)
    a_w_s = nrm(ks[4], (N_A, A_GROUPS, CHUNK, CHUNK), 0.5 * CHUNK ** -0.5)
    a_b_s = 1.0 + nrm(ks[5], (N_A, A_GROUPS, CHUNK), 0.02)
    a_w_out = nrm(ks[6], (N_A, A_WIDTH, D_MODEL), A_WIDTH ** -0.5)
    b_w_in = nrm(ks[7], (N_B, D_MODEL, B_WIDTH), D_MODEL ** -0.5)
    b_w_grp = nrm(ks[8], (N_B, B_GROUPS, B_HEAD, B_HEAD), B_HEAD ** -0.5)
    b_scale = 0.5 + nrm(ks[9], (N_B, B_WIDTH), 0.02)
    b_w_out = nrm(ks[10], (N_B, B_WIDTH, D_MODEL), B_WIDTH ** -0.5)
    norm_mix = 1.0 + nrm(ks[11], (DEPTH, D_MODEL), 0.02)
    norm_mlp = 1.0 + nrm(ks[12], (DEPTH, D_MODEL), 0.02)
    mlp_w1 = nrm(ks[13], (DEPTH, D_MODEL, D_FF), D_MODEL ** -0.5)
    mlp_w2 = nrm(ks[14], (DEPTH, D_FF, D_MODEL), D_FF ** -0.5)
    final_norm = 1.0 + nrm(ks[15], (D_MODEL,), 0.02)
    return {"x": x,
            "a_w_in": a_w_in, "a_ln_g": a_ln_g, "a_ln_b": a_ln_b, "a_w_s": a_w_s,
            "a_b_s": a_b_s, "a_w_out": a_w_out,
            "b_w_in": b_w_in, "b_w_grp": b_w_grp, "b_scale": b_scale, "b_w_out": b_w_out,
            "norm_mix": norm_mix, "norm_mlp": norm_mlp, "mlp_w1": mlp_w1, "mlp_w2": mlp_w2,
            "final_norm": final_norm}


def _fwd_reference(x, a_w_in, a_ln_g, a_ln_b, a_w_s, a_b_s, a_w_out,
              b_w_in, b_w_grp, b_scale, b_w_out,
              norm_mix, norm_mlp, mlp_w1, mlp_w2, final_norm):
    h = x
    for i in range(DEPTH):
        hn = rmsnorm(h, norm_mix[i])
        if i % N_MIXERS == 0:
            j = i // N_MIXERS
            mix = chunked_gmlp_mixer(hn, a_w_in[j], a_ln_g[j], a_ln_b[j],
                                     a_w_s[j], a_b_s[j], a_w_out[j])
        else:
            j = i // N_MIXERS
            mix = multiscale_pool_mixer(hn, b_w_in[j], b_w_grp[j], b_scale[j], b_w_out[j])
        h = h + mix
        h = h + sqrelu_mlp(rmsnorm(h, norm_mlp[i]), mlp_w1[i], mlp_w2[i])
    return rmsnorm(h, final_norm)


import jax as _jax
import jax.numpy as _jnp

TWIN_FORMAT = 'train_step'
FWD_PARAMS = ['x', 'a_w_in', 'a_ln_g', 'a_ln_b', 'a_w_s', 'a_b_s', 'a_w_out', 'b_w_in', 'b_w_grp', 'b_scale', 'b_w_out', 'norm_mix', 'norm_mlp', 'mlp_w1', 'mlp_w2', 'final_norm']
TWIN_WEIGHTS = ['a_w_in', 'a_ln_g', 'a_ln_b', 'a_w_s', 'a_b_s', 'a_w_out', 'b_w_in', 'b_w_grp', 'b_scale', 'b_w_out', 'norm_mix', 'norm_mlp', 'mlp_w1', 'mlp_w2', 'final_norm']
TWIN_DIFF_INPUT = 'x'
TWIN_INPUTS = ['x', 'a_w_in', 'a_ln_g', 'a_ln_b', 'a_w_s', 'a_b_s', 'a_w_out', 'b_w_in', 'b_w_grp', 'b_scale', 'b_w_out', 'norm_mix', 'norm_mlp', 'mlp_w1', 'mlp_w2', 'final_norm', 'loss_target', 'm_a_w_in', 'm_a_ln_g', 'm_a_ln_b', 'm_a_w_s', 'm_a_b_s', 'm_a_w_out', 'm_b_w_in', 'm_b_w_grp', 'm_b_scale', 'm_b_w_out', 'm_norm_mix', 'm_norm_mlp', 'm_mlp_w1', 'm_mlp_w2', 'm_final_norm', 'v_a_w_in', 'v_a_ln_g', 'v_a_ln_b', 'v_a_w_s', 'v_a_b_s', 'v_a_w_out', 'v_b_w_in', 'v_b_w_grp', 'v_b_scale', 'v_b_w_out', 'v_norm_mix', 'v_norm_mlp', 'v_mlp_w1', 'v_mlp_w2', 'v_final_norm']
TWIN_OUTPUTS = ['loss', 'grad_x', 'grad_a_w_in', 'grad_a_ln_g', 'grad_a_ln_b', 'grad_a_w_s', 'grad_a_b_s', 'grad_a_w_out', 'grad_b_w_in', 'grad_b_w_grp', 'grad_b_scale', 'grad_b_w_out', 'grad_norm_mix', 'grad_norm_mlp', 'grad_mlp_w1', 'grad_mlp_w2', 'grad_final_norm', 'delta_a_w_in', 'delta_a_ln_g', 'delta_a_ln_b', 'delta_a_w_s', 'delta_a_b_s', 'delta_a_w_out', 'delta_b_w_in', 'delta_b_w_grp', 'delta_b_scale', 'delta_b_w_out', 'delta_norm_mix', 'delta_norm_mlp', 'delta_mlp_w1', 'delta_mlp_w2', 'delta_final_norm', 'new_m_a_w_in', 'new_m_a_ln_g', 'new_m_a_ln_b', 'new_m_a_w_s', 'new_m_a_b_s', 'new_m_a_w_out', 'new_m_b_w_in', 'new_m_b_w_grp', 'new_m_b_scale', 'new_m_b_w_out', 'new_m_norm_mix', 'new_m_norm_mlp', 'new_m_mlp_w1', 'new_m_mlp_w2', 'new_m_final_norm', 'new_v_a_w_in', 'new_v_a_ln_g', 'new_v_a_ln_b', 'new_v_a_w_s', 'new_v_a_b_s', 'new_v_a_w_out', 'new_v_b_w_in', 'new_v_b_w_grp', 'new_v_b_scale', 'new_v_b_w_out', 'new_v_norm_mix', 'new_v_norm_mlp', 'new_v_mlp_w1', 'new_v_mlp_w2', 'new_v_final_norm']
TWIN_LEAF_KINDS = {'loss': 'loss', 'grad_x': 'grad_x', 'grad_a_w_in': 'grad_w', 'grad_a_ln_g': 'grad_w', 'grad_a_ln_b': 'grad_w', 'grad_a_w_s': 'grad_w', 'grad_a_b_s': 'grad_w', 'grad_a_w_out': 'grad_w', 'grad_b_w_in': 'grad_w', 'grad_b_w_grp': 'grad_w', 'grad_b_scale': 'grad_w', 'grad_b_w_out': 'grad_w', 'grad_norm_mix': 'grad_w', 'grad_norm_mlp': 'grad_w', 'grad_mlp_w1': 'grad_w', 'grad_mlp_w2': 'grad_w', 'grad_final_norm': 'grad_w', 'delta_a_w_in': 'delta_w', 'delta_a_ln_g': 'delta_w', 'delta_a_ln_b': 'delta_w', 'delta_a_w_s': 'delta_w', 'delta_a_b_s': 'delta_w', 'delta_a_w_out': 'delta_w', 'delta_b_w_in': 'delta_w', 'delta_b_w_grp': 'delta_w', 'delta_b_scale': 'delta_w', 'delta_b_w_out': 'delta_w', 'delta_norm_mix': 'delta_w', 'delta_norm_mlp': 'delta_w', 'delta_mlp_w1': 'delta_w', 'delta_mlp_w2': 'delta_w', 'delta_final_norm': 'delta_w', 'new_m_a_w_in': 'new_m', 'new_m_a_ln_g': 'new_m', 'new_m_a_ln_b': 'new_m', 'new_m_a_w_s': 'new_m', 'new_m_a_b_s': 'new_m', 'new_m_a_w_out': 'new_m', 'new_m_b_w_in': 'new_m', 'new_m_b_w_grp': 'new_m', 'new_m_b_scale': 'new_m', 'new_m_b_w_out': 'new_m', 'new_m_norm_mix': 'new_m', 'new_m_norm_mlp': 'new_m', 'new_m_mlp_w1': 'new_m', 'new_m_mlp_w2': 'new_m', 'new_m_final_norm': 'new_m', 'new_v_a_w_in': 'new_v', 'new_v_a_ln_g': 'new_v', 'new_v_a_ln_b': 'new_v', 'new_v_a_w_s': 'new_v', 'new_v_a_b_s': 'new_v', 'new_v_a_w_out': 'new_v', 'new_v_b_w_in': 'new_v', 'new_v_b_w_grp': 'new_v', 'new_v_b_scale': 'new_v', 'new_v_b_w_out': 'new_v', 'new_v_norm_mix': 'new_v', 'new_v_norm_mlp': 'new_v', 'new_v_mlp_w1': 'new_v', 'new_v_mlp_w2': 'new_v', 'new_v_final_norm': 'new_v'}


def _forward(args):
    return _fwd_reference(*[args[k] for k in FWD_PARAMS])


def _output_shape():
    out = _jax.eval_shape(lambda: _forward(_fwd_setup_inputs(0)))
    return out.shape, out.dtype

N_MICROBATCH = 1
ADAM_LR = 0.001
ADAM_B1 = 0.9
ADAM_B2 = 0.999
ADAM_EPS = 1e-08
ADAM_WD = 0.01
ADAM_STEP = 10
PER_EXAMPLE_BATCH_AXIS = {'x': 0, 'loss_target': 0}
SHARED_INPUTS = []
_WEIGHT_DTYPES = {'a_w_in': _jnp.float32, 'a_ln_g': _jnp.float32, 'a_ln_b': _jnp.float32, 'a_w_s': _jnp.float32, 'a_b_s': _jnp.float32, 'a_w_out': _jnp.float32, 'b_w_in': _jnp.float32, 'b_w_grp': _jnp.float32, 'b_scale': _jnp.float32, 'b_w_out': _jnp.float32, 'norm_mix': _jnp.float32, 'norm_mlp': _jnp.float32, 'mlp_w1': _jnp.float32, 'mlp_w2': _jnp.float32, 'final_norm': _jnp.float32}
MOMENT_SCALE = {'a_w_in': 3.170082e-02, 'a_ln_g': 1.387411e-02, 'a_ln_b': 1.350465e-02, 'a_w_s': 3.818399e-02, 'a_b_s': 5.583567e-02, 'a_w_out': 4.341207e-02, 'b_w_in': 1.528111e-02, 'b_w_grp': 1.526080e-02, 'b_scale': 3.085415e-02, 'b_w_out': 1.528508e-02, 'norm_mix': 3.414293e-02, 'norm_mlp': 5.040600e-02, 'mlp_w1': 2.452402e-02, 'mlp_w2': 4.769127e-02, 'final_norm': 8.154864e+00}


def _to_microbatches(a, axis):
    t = _jnp.moveaxis(a, axis, 0)
    t = t.reshape((N_MICROBATCH, t.shape[0] // N_MICROBATCH) + t.shape[1:])
    return _jnp.moveaxis(t, 1, axis + 1)


def setup_inputs(seed: int = 0) -> dict:
    inp = _fwd_setup_inputs(seed)
    key = _jax.random.fold_in(_jax.random.key(seed), 7919)
    shape, _ = _output_shape()
    out = dict(inp)
    out["loss_target"] = _jax.random.normal(_jax.random.fold_in(key, 0), shape, _jnp.float32)
    for i, name in enumerate(TWIN_WEIGHTS):
        w = inp[name].astype(_jnp.float32)
        if MOMENT_SCALE is None:
            s = _jnp.sqrt(_jnp.mean(_jnp.square(w)) + 1e-30)
        else:
            s = MOMENT_SCALE[name]
        km, kv = _jax.random.split(_jax.random.fold_in(key, i + 1))
        out[name] = w
        out["m_" + name] = s * _jax.random.normal(km, w.shape, _jnp.float32)
        out["v_" + name] = (s * s) * _jax.random.uniform(kv, w.shape, _jnp.float32, 0.5, 1.5)
    if N_MICROBATCH > 1:
        for name, axis in PER_EXAMPLE_BATCH_AXIS.items():
            out[name] = _to_microbatches(out[name], axis)
    return {'x': out['x'], 'a_w_in': out['a_w_in'], 'a_ln_g': out['a_ln_g'], 'a_ln_b': out['a_ln_b'], 'a_w_s': out['a_w_s'], 'a_b_s': out['a_b_s'], 'a_w_out': out['a_w_out'], 'b_w_in': out['b_w_in'], 'b_w_grp': out['b_w_grp'], 'b_scale': out['b_scale'], 'b_w_out': out['b_w_out'], 'norm_mix': out['norm_mix'], 'norm_mlp': out['norm_mlp'], 'mlp_w1': out['mlp_w1'], 'mlp_w2': out['mlp_w2'], 'final_norm': out['final_norm'], 'loss_target': out['loss_target'], 'm_a_w_in': out['m_a_w_in'], 'm_a_ln_g': out['m_a_ln_g'], 'm_a_ln_b': out['m_a_ln_b'], 'm_a_w_s': out['m_a_w_s'], 'm_a_b_s': out['m_a_b_s'], 'm_a_w_out': out['m_a_w_out'], 'm_b_w_in': out['m_b_w_in'], 'm_b_w_grp': out['m_b_w_grp'], 'm_b_scale': out['m_b_scale'], 'm_b_w_out': out['m_b_w_out'], 'm_norm_mix': out['m_norm_mix'], 'm_norm_mlp': out['m_norm_mlp'], 'm_mlp_w1': out['m_mlp_w1'], 'm_mlp_w2': out['m_mlp_w2'], 'm_final_norm': out['m_final_norm'], 'v_a_w_in': out['v_a_w_in'], 'v_a_ln_g': out['v_a_ln_g'], 'v_a_ln_b': out['v_a_ln_b'], 'v_a_w_s': out['v_a_w_s'], 'v_a_b_s': out['v_a_b_s'], 'v_a_w_out': out['v_a_w_out'], 'v_b_w_in': out['v_b_w_in'], 'v_b_w_grp': out['v_b_w_grp'], 'v_b_scale': out['v_b_scale'], 'v_b_w_out': out['v_b_w_out'], 'v_norm_mix': out['v_norm_mix'], 'v_norm_mlp': out['v_norm_mlp'], 'v_mlp_w1': out['v_mlp_w1'], 'v_mlp_w2': out['v_mlp_w2'], 'v_final_norm': out['v_final_norm']}


def _loss(weights, diff, rest, loss_target):
    with _jax.named_scope("forward"):
        args = {**rest, TWIN_DIFF_INPUT: diff, **{k: w.astype(_WEIGHT_DTYPES[k]) for k, w in weights.items()}}
        y = _forward(args)
    with _jax.named_scope("loss_head"):
        err = _jnp.square(y.astype(_jnp.float32) - loss_target)
        return 0.5 * _jnp.sum(_jnp.mean(err, axis=-1)) if err.ndim else 0.5 * err


def _adamw(w, g, m, v):
    m = ADAM_B1 * m + (1.0 - ADAM_B1) * g
    v = ADAM_B2 * v + (1.0 - ADAM_B2) * _jnp.square(g)
    m_hat = m / (1.0 - ADAM_B1 ** ADAM_STEP)
    v_hat = v / (1.0 - ADAM_B2 ** ADAM_STEP)
    delta = -ADAM_LR * (m_hat / (_jnp.sqrt(v_hat) + ADAM_EPS) + ADAM_WD * w)
    return delta, m, v


def reference(x, a_w_in, a_ln_g, a_ln_b, a_w_s, a_b_s, a_w_out, b_w_in, b_w_grp, b_scale, b_w_out, norm_mix, norm_mlp, mlp_w1, mlp_w2, final_norm, loss_target, m_a_w_in, m_a_ln_g, m_a_ln_b, m_a_w_s, m_a_b_s, m_a_w_out, m_b_w_in, m_b_w_grp, m_b_scale, m_b_w_out, m_norm_mix, m_norm_mlp, m_mlp_w1, m_mlp_w2, m_final_norm, v_a_w_in, v_a_ln_g, v_a_ln_b, v_a_w_s, v_a_b_s, v_a_w_out, v_b_w_in, v_b_w_grp, v_b_scale, v_b_w_out, v_norm_mix, v_norm_mlp, v_mlp_w1, v_mlp_w2, v_final_norm):
    given = dict(x=x, a_w_in=a_w_in, a_ln_g=a_ln_g, a_ln_b=a_ln_b, a_w_s=a_w_s, a_b_s=a_b_s, a_w_out=a_w_out, b_w_in=b_w_in, b_w_grp=b_w_grp, b_scale=b_scale, b_w_out=b_w_out, norm_mix=norm_mix, norm_mlp=norm_mlp, mlp_w1=mlp_w1, mlp_w2=mlp_w2, final_norm=final_norm, loss_target=loss_target, m_a_w_in=m_a_w_in, m_a_ln_g=m_a_ln_g, m_a_ln_b=m_a_ln_b, m_a_w_s=m_a_w_s, m_a_b_s=m_a_b_s, m_a_w_out=m_a_w_out, m_b_w_in=m_b_w_in, m_b_w_grp=m_b_w_grp, m_b_scale=m_b_scale, m_b_w_out=m_b_w_out, m_norm_mix=m_norm_mix, m_norm_mlp=m_norm_mlp, m_mlp_w1=m_mlp_w1, m_mlp_w2=m_mlp_w2, m_final_norm=m_final_norm, v_a_w_in=v_a_w_in, v_a_ln_g=v_a_ln_g, v_a_ln_b=v_a_ln_b, v_a_w_s=v_a_w_s, v_a_b_s=v_a_b_s, v_a_w_out=v_a_w_out, v_b_w_in=v_b_w_in, v_b_w_grp=v_b_w_grp, v_b_scale=v_b_scale, v_b_w_out=v_b_w_out, v_norm_mix=v_norm_mix, v_norm_mlp=v_norm_mlp, v_mlp_w1=v_mlp_w1, v_mlp_w2=v_mlp_w2, v_final_norm=v_final_norm)
    weights = {n: given[n] for n in TWIN_WEIGHTS}
    shared = {n: given[n] for n in SHARED_INPUTS}
    per_example = {n: given[n] for n in ['x']}
    grad_fn = _jax.value_and_grad(_loss, argnums=(0, 1))

    def one_microbatch(ex, loss_target):
        ex = dict(ex)
        diff = ex.pop(TWIN_DIFF_INPUT)
        return grad_fn(weights, diff, {**shared, **ex}, loss_target)

    if N_MICROBATCH == 1:
        loss, (grad_w, grad_x) = one_microbatch(per_example, given["loss_target"])
    else:
        def body(carry, xs):
            loss_sum, grad_sum = carry
            l_k, (gw_k, gx_k) = one_microbatch(xs[0], xs[1])
            with _jax.named_scope("update"):
                return (loss_sum + l_k, _jax.tree.map(_jnp.add, grad_sum, gw_k)), gx_k

        init = (_jnp.zeros((), _jnp.float32), _jax.tree.map(_jnp.zeros_like, weights))
        (loss, grad_w), grad_x = _jax.lax.scan(body, init, (per_example, given["loss_target"]))
    with _jax.named_scope("update"):
        delta_w, new_m, new_v = {}, {}, {}
        for n in TWIN_WEIGHTS:
            delta_w[n], new_m[n], new_v[n] = _adamw(weights[n], grad_w[n], given["m_" + n], given["v_" + n])
    return (loss, grad_x, *[grad_w[n] for n in TWIN_WEIGHTS], *[delta_w[n] for n in TWIN_WEIGHTS],
            *[new_m[n] for n in TWIN_WEIGHTS], *[new_v[n] for n in TWIN_WEIGHTS])
```

```python
import math

import jax
import jax.numpy as jnp
from jax import lax
from jax.experimental import pallas as pl
from jax.experimental.pallas import tpu as pltpu

F32 = jnp.float32
BF16 = jnp.bfloat16
MESH = pl.DeviceIdType.MESH

EPS = 1e-6
B_WINDOWS = (2, 4, 8, 16)
ADAM_LR = 0.001
ADAM_B1 = 0.9
ADAM_B2 = 0.999
ADAM_EPS = 1e-08
ADAM_WD = 0.01
ADAM_STEP = 10

N_CHIPS = 4
N_DEV = 8
LANES = 128
PACK_ELEMS = 8 * LANES
VMEM_LIMIT = 56 * 1024 * 1024
ROW_TILE = 256
MM_TM, MM_TN, MM_TK = 1024, 1024, 512


def _tile(dim, pref):
    t = min(dim, pref)
    while dim % t:
        t //= 2
    return t


def _params(*sem):
    return pltpu.CompilerParams(dimension_semantics=sem, vmem_limit_bytes=VMEM_LIMIT)


class _W4:
    def __init__(self, arr, col_sharded):
        self.arr = arr
        self.nj, self.nl, self.r, self.c = arr.shape
        self.col = col_sharded
        self.rows = self.r if col_sharded else self.nj * self.r
        self.cols = self.nj * self.c if col_sharded else self.c

    def tile_rows(self, pref):
        return _tile(self.r, pref)

    def tile_cols(self, pref):
        return _tile(self.c, pref)

    def index(self, layer, rb, cb, tr, tc):
        if self.col:
            n = self.c // tc
            return (cb // n, layer, rb, cb % n)
        n = self.r // tr
        return (rb // n, layer, rb % n, cb)


def _mm_aw(name, a, w, *, layer=0, groups=1, transpose_w=False, extras=(), out_dtypes=(F32,), epilogue=None):
    s, ka_total = a.shape
    kdim, ndim = (w.cols, w.rows) if transpose_w else (w.rows, w.cols)
    assert ka_total == groups * kdim, (name, a.shape, kdim, groups)
    tm = _tile(s, MM_TM)
    if transpose_w:
        tn, tk = w.tile_rows(MM_TN), w.tile_cols(MM_TK)
    else:
        tk, tn = w.tile_rows(MM_TK), w.tile_cols(MM_TN)
    nk, nn = kdim // tk, ndim // tn

    def lay(g):
        return g if groups > 1 else layer

    a_spec = pl.BlockSpec((tm, tk), lambda g, i, n, k: (i, g * nk + k))
    if transpose_w:
        w_spec = pl.BlockSpec((None, None, tn, tk), lambda g, i, n, k: w.index(lay(g), n, k, tn, tk))
    else:
        w_spec = pl.BlockSpec((None, None, tk, tn), lambda g, i, n, k: w.index(lay(g), k, n, tk, tn))
    ex_specs = []
    for e in extras:
        assert e.shape[1] == groups * ndim and e.shape[0] in (1, s), (name, e.shape)
        if e.shape[0] == 1:
            ex_specs.append(pl.BlockSpec((1, tn), lambda g, i, n, k: (0, g * nn + n)))
        else:
            ex_specs.append(pl.BlockSpec((tm, tn), lambda g, i, n, k: (i, g * nn + n)))
    out_spec = pl.BlockSpec((tm, tn), lambda g, i, n, k: (i, g * nn + n))
    n_ex, n_out = len(extras), len(out_dtypes)

    def body(a_ref, w_ref, *rest):
        ex, outs, acc = rest[:n_ex], rest[n_ex:n_ex + n_out], rest[-1]
        k = pl.program_id(3)

        @pl.when(k == 0)
        def _():
            acc[...] = jnp.zeros_like(acc)

        av = a_ref[...]
        if av.dtype != BF16:
            av = av.astype(BF16)
        if transpose_w:
            acc[...] += lax.dot_general(av, w_ref[...], (((1,), (1,)), ((), ())), preferred_element_type=F32)
        else:
            acc[...] += jnp.dot(av, w_ref[...], preferred_element_type=F32)

        @pl.when(k == nk - 1)
        def _():
            vals = (acc[...],) if epilogue is None else epilogue(acc[...], *[e[...] for e in ex])
            for o, v in zip(outs, vals):
                o[...] = v.astype(o.dtype)

    outs = pl.pallas_call(
        body, name=name, grid=(groups, s // tm, nn, nk),
        in_specs=[a_spec, w_spec, *ex_specs], out_specs=[out_spec] * n_out,
        out_shape=[jax.ShapeDtypeStruct((s, groups * ndim), dt) for dt in out_dtypes],
        scratch_shapes=[pltpu.VMEM((tm, tn), F32)],
        compiler_params=_params("parallel", "parallel", "parallel", "arbitrary"),
    )(a, w.arr, *extras)
    return outs[0] if n_out == 1 else outs


def _mm_dw(name, a, b, like, *, layer=0, groups=1, prev=None):
    s, ka_total = a.shape
    rows, cols = ka_total // groups, b.shape[1] // groups
    assert (rows, cols) == (like.rows, like.cols) and b.shape[0] == s, (name, a.shape, b.shape)
    tm, tn, tk = like.tile_rows(MM_TM), like.tile_cols(MM_TN), _tile(s, MM_TK)
    nr, nc, nk = rows // tm, cols // tn, s // tk

    def lay(g):
        return g if groups > 1 else layer

    in_specs = [pl.BlockSpec((tk, tm), lambda g, i, n, k: (k, g * nr + i)),
                pl.BlockSpec((tk, tn), lambda g, i, n, k: (k, g * nc + n))]
    operands = [a, b]
    aliases = {}
    if prev is not None:
        in_specs.append(pl.BlockSpec(memory_space=pl.ANY))
        operands.append(prev)
        aliases = {2: 0}

    def body(a_ref, b_ref, *rest):
        o_ref, acc = rest[-2], rest[-1]
        k = pl.program_id(3)

        @pl.when(k == 0)
        def _():
            acc[...] = jnp.zeros_like(acc)

        av, bv = a_ref[...], b_ref[...]
        if av.dtype != BF16:
            av = av.astype(BF16)
        if bv.dtype != BF16:
            bv = bv.astype(BF16)
        acc[...] += lax.dot_general(av, bv, (((0,), (0,)), ((), ())), preferred_element_type=F32)

        @pl.when(k == nk - 1)
        def _():
            o_ref[...] = acc[...]

    return pl.pallas_call(
        body, name=name, grid=(groups, nr, nc, nk), in_specs=in_specs,
        out_specs=pl.BlockSpec((None, None, tm, tn), lambda g, i, n, k: like.index(lay(g), i, n, tm, tn)),
        out_shape=jax.ShapeDtypeStruct((like.nj, like.nl, like.r, like.c), F32),
        scratch_shapes=[pltpu.VMEM((tm, tn), F32)], input_output_aliases=aliases,
        compiler_params=_params("parallel", "parallel", "parallel", "arbitrary"),
    )(*operands)


def _row_spec(tr, d):
    return pl.BlockSpec((tr, d), lambda i: (i, 0))


def _vec_spec(d):
    return pl.BlockSpec((1, d), lambda i: (0, 0))


def _rms_fwd(name, x, g):
    s, d = x.shape
    tr = _tile(s, ROW_TILE)

    def body(x_ref, g_ref, o_ref):
        xv = x_ref[...]
        r = lax.rsqrt(jnp.mean(xv * xv, axis=-1, keepdims=True) + EPS)
        o_ref[...] = (xv * r * g_ref[...]).astype(BF16)

    return pl.pallas_call(
        body, name=name, grid=(s // tr,), in_specs=[_row_spec(tr, d), _vec_spec(d)],
        out_specs=_row_spec(tr, d), out_shape=jax.ShapeDtypeStruct((s, d), BF16),
        compiler_params=_params("parallel"),
    )(x, g)


def _rms_bwd(name, x, g, dhn, dres):
    s, d = x.shape
    tr = _tile(s, ROW_TILE)

    def body(x_ref, g_ref, dhn_ref, dres_ref, dx_ref, dg_ref):
        @pl.when(pl.program_id(0) == 0)
        def _():
            dg_ref[...] = jnp.zeros_like(dg_ref)

        xv = x_ref[...]
        r = lax.rsqrt(jnp.mean(xv * xv, axis=-1, keepdims=True) + EPS)
        xh = xv * r
        dy = dhn_ref[...]
        dg_ref[...] += jnp.sum(dy * xh, axis=0, keepdims=True)
        dxh = dy * g_ref[...]
        dx_ref[...] = dres_ref[...] + r * (dxh - xh * jnp.mean(dxh * xh, axis=-1, keepdims=True))

    return pl.pallas_call(
        body, name=name, grid=(s // tr,),
        in_specs=[_row_spec(tr, d), _vec_spec(d), _row_spec(tr, d), _row_spec(tr, d)],
        out_specs=[_row_spec(tr, d), _vec_spec(d)],
        out_shape=[jax.ShapeDtypeStruct((s, d), F32), jax.ShapeDtypeStruct((1, d), F32)],
        compiler_params=_params("arbitrary"),
    )(x, g, dhn, dres)


def _final(name, h, g, tgt):
    s, d = h.shape
    tr = _tile(s, ROW_TILE)

    def body(h_ref, g_ref, t_ref, dh_ref, dg_ref, loss_ref):
        @pl.when(pl.program_id(0) == 0)
        def _():
            dg_ref[...] = jnp.zeros_like(dg_ref)
            loss_ref[...] = jnp.zeros_like(loss_ref)

        hv = h_ref[...]
        r = lax.rsqrt(jnp.mean(hv * hv, axis=-1, keepdims=True) + EPS)
        xh = hv * r
        gv = g_ref[...]
        err = xh * gv - t_ref[...]
        part = 0.5 * jnp.sum(jnp.mean(err * err, axis=-1, keepdims=True), axis=0, keepdims=True)
        loss_ref[...] += jnp.broadcast_to(part, loss_ref.shape)
        dy = err * (1.0 / d)
        dg_ref[...] += jnp.sum(dy * xh, axis=0, keepdims=True)
        dxh = dy * gv
        dh_ref[...] = r * (dxh - xh * jnp.mean(dxh * xh, axis=-1, keepdims=True))

    return pl.pallas_call(
        body, name=name, grid=(s // tr,),
        in_specs=[_row_spec(tr, d), _vec_spec(d), _row_spec(tr, d)],
        out_specs=[_row_spec(tr, d), _vec_spec(d), _vec_spec(LANES)],
        out_shape=[jax.ShapeDtypeStruct((s, d), F32), jax.ShapeDtypeStruct((1, d), F32),
                   jax.ShapeDtypeStruct((1, LANES), F32)],
        compiler_params=_params("arbitrary"),
    )(h, g, tgt)


_SQRT_HALF = 1.0 / math.sqrt(2.0)
_INV_SQRT_2PI = 1.0 / math.sqrt(2.0 * math.pi)


def _gelu(x):
    return x * (lax.erf(x * _SQRT_HALF) + 1.0) * 0.5


def _gelu_grad(x):
    return 0.5 * (lax.erf(x * _SQRT_HALF) + 1.0) + x * jnp.exp(-0.5 * x * x) * _INV_SQRT_2PI


def _causal(chunk):
    t = lax.broadcasted_iota(jnp.int32, (chunk, chunk), 0)
    sidx = lax.broadcasted_iota(jnp.int32, (chunk, chunk), 1)
    return sidx <= t


def _layernorm_parts(v, g, b):
    mu = jnp.mean(v, axis=-1, keepdims=True)
    vc = v - mu
    rs = lax.rsqrt(jnp.mean(vc * vc, axis=-1, keepdims=True) + EPS)
    vhat = vc * rs
    return vhat, rs, vhat * g + b


def _gate_fwd(name, zpre, ln_g, ln_b, w_s, b_col):
    s, aw2 = zpre.shape
    aw = aw2 // 2
    ng, chunk, _ = w_s.shape
    dh = aw // ng

    def body(z_ref, g_ref, b_ref, ws_ref, bc_ref, o_ref):
        u = _gelu(z_ref[:, :aw])
        v = _gelu(z_ref[:, aw:])
        _, _, vln = _layernorm_parts(v, g_ref[...], b_ref[...])
        mask = _causal(chunk)
        for gi in range(ng):
            sl = slice(gi * dh, (gi + 1) * dh)
            wm = jnp.where(mask, ws_ref[gi], 0.0).astype(BF16)
            sg = jnp.dot(wm, vln[:, sl].astype(BF16), preferred_element_type=F32) + bc_ref[:, gi:gi + 1]
            o_ref[:, sl] = (u[:, sl] * sg).astype(BF16)

    return pl.pallas_call(
        body, name=name, grid=(s // chunk,),
        in_specs=[_row_spec(chunk, aw2), _vec_spec(aw), _vec_spec(aw),
                  pl.BlockSpec((ng, chunk, chunk), lambda i: (0, 0, 0)),
                  pl.BlockSpec((chunk, ng), lambda i: (0, 0))],
        out_specs=_row_spec(chunk, aw), out_shape=jax.ShapeDtypeStruct((s, aw), BF16),
        compiler_params=_params("parallel"),
    )(zpre, ln_g, ln_b, w_s, b_col)


def _gate_bwd(name, zpre, dgated, ln_g, ln_b, w_s, b_col):
    s, aw2 = zpre.shape
    aw = aw2 // 2
    ng, chunk, _ = w_s.shape
    dh = aw // ng

    def body(z_ref, dgt_ref, g_ref, b_ref, ws_ref, bc_ref, dz_ref, dws_ref, dbc_ref, dlg_ref, dlb_ref,
             du_scr, dvln_scr):
        @pl.when(pl.program_id(0) == 0)
        def _():
            dws_ref[...] = jnp.zeros_like(dws_ref)
            dbc_ref[...] = jnp.zeros_like(dbc_ref)
            dlg_ref[...] = jnp.zeros_like(dlg_ref)
            dlb_ref[...] = jnp.zeros_like(dlb_ref)

        zu = z_ref[:, :aw]
        zv = z_ref[:, aw:]
        u = _gelu(zu)
        lg = g_ref[...]
        vhat, rs, vln = _layernorm_parts(_gelu(zv), lg, b_ref[...])
        mask = _causal(chunk)
        for gi in range(ng):
            sl = slice(gi * dh, (gi + 1) * dh)
            wm = jnp.where(mask, ws_ref[gi], 0.0).astype(BF16)
            vg = vln[:, sl].astype(BF16)
            sg = jnp.dot(wm, vg, preferred_element_type=F32) + bc_ref[:, gi:gi + 1]
            dgt = dgt_ref[:, sl]
            du_scr[:, sl] = dgt * sg
            ds = dgt * u[:, sl]
            dbc_ref[:, gi:gi + 1] += jnp.sum(ds, axis=-1, keepdims=True)
            dsb = ds.astype(BF16)
            dwm = lax.dot_general(dsb, vg, (((1,), (1,)), ((), ())), preferred_element_type=F32)
            dws_ref[gi] += jnp.where(mask, dwm, 0.0)
            dvln_scr[:, sl] = lax.dot_general(wm, dsb, (((0,), (0,)), ((), ())), preferred_element_type=F32)
        dvln = dvln_scr[...]
        dlb_ref[...] += jnp.sum(dvln, axis=0, keepdims=True)
        dlg_ref[...] += jnp.sum(dvln * vhat, axis=0, keepdims=True)
        dvh = dvln * lg
        dv = rs * (dvh - jnp.mean(dvh, axis=-1, keepdims=True)
                   - vhat * jnp.mean(dvh * vhat, axis=-1, keepdims=True))
        dz_ref[:, :aw] = (du_scr[...] * _gelu_grad(zu)).astype(BF16)
        dz_ref[:, aw:] = (dv * _gelu_grad(zv)).astype(BF16)

    return pl.pallas_call(
        body, name=name, grid=(s // chunk,),
        in_specs=[_row_spec(chunk, aw2), _row_spec(chunk, aw), _vec_spec(aw), _vec_spec(aw),
                  pl.BlockSpec((ng, chunk, chunk), lambda i: (0, 0, 0)),
                  pl.BlockSpec((chunk, ng), lambda i: (0, 0))],
        out_specs=[_row_spec(chunk, aw2), pl.BlockSpec((ng, chunk, chunk), lambda i: (0, 0, 0)),
                   pl.BlockSpec((chunk, ng), lambda i: (0, 0)), _vec_spec(aw), _vec_spec(aw)],
        out_shape=[jax.ShapeDtypeStruct((s, aw2), BF16), jax.ShapeDtypeStruct((ng, chunk, chunk), F32),
                   jax.ShapeDtypeStruct((chunk, ng), F32), jax.ShapeDtypeStruct((1, aw), F32),
                   jax.ShapeDtypeStruct((1, aw), F32)],
        scratch_shapes=[pltpu.VMEM((chunk, aw), F32), pltpu.VMEM((chunk, aw), F32)],
        compiler_params=_params("arbitrary"),
    )(zpre, dgated, ln_g, ln_b, w_s, b_col)


def _pool_select(g, parts):
    out = parts[-1]
    for gi in range(len(parts) - 2, -1, -1):
        out = jnp.where(g == gi, parts[gi], out)
    return out


def _pool_specs(s, bw):
    head = bw // len(B_WINDOWS)
    tc = _tile(head, 256)
    nb = head // tc
    return tc, (len(B_WINDOWS), nb), pl.BlockSpec((s, tc), lambda g, j: (0, g * nb + j))


def _pool_window(g, t):
    w = _pool_select(g, [jnp.full(t.shape, wi, jnp.int32) for wi in B_WINDOWS])
    return jnp.minimum(t + 1, w).astype(F32)


def _pool_fwd(name, vb):
    assert B_WINDOWS == (2, 4, 8, 16)
    s, bw = vb.shape
    tc, grid, spec = _pool_specs(s, bw)

    def body(v_ref, o_ref):
        g = pl.program_id(0)
        v = v_ref[...]
        t = lax.broadcasted_iota(jnp.int32, (s, tc), 0)

        def down(x, k):
            return jnp.where(t >= k, pltpu.roll(x, k, 0), 0.0)

        sums, cur, k = [], v, 1
        for _ in B_WINDOWS:
            cur = cur + down(cur, k)
            sums.append(cur)
            k *= 2
        o_ref[...] = (_pool_select(g, sums) / _pool_window(g, t) - v).astype(BF16)

    return pl.pallas_call(
        body, name=name, grid=grid, in_specs=[spec], out_specs=spec,
        out_shape=jax.ShapeDtypeStruct((s, bw), BF16), compiler_params=_params("parallel", "parallel"),
    )(vb)


def _pool_bwd(name, dpooled):
    s, bw = dpooled.shape
    tc, grid, spec = _pool_specs(s, bw)

    def body(d_ref, o_ref):
        g = pl.program_id(0)
        dp = d_ref[...]
        t = lax.broadcasted_iota(jnp.int32, (s, tc), 0)

        def up(x, k):
            return jnp.where(t < s - k, pltpu.roll(x, s - k, 0), 0.0)

        sums, cur, k = [], dp / _pool_window(g, t), 1
        for _ in B_WINDOWS:
            cur = cur + up(cur, k)
            sums.append(cur)
            k *= 2
        o_ref[...] = (_pool_select(g, sums) - dp).astype(BF16)

    return pl.pallas_call(
        body, name=name, grid=grid, in_specs=[spec], out_specs=spec,
        out_shape=jax.ShapeDtypeStruct((s, bw), BF16), compiler_params=_params("parallel", "parallel"),
    )(dpooled)


def _scale_bwd(name, dms, mixed, scale):
    s, bw = dms.shape
    tr = _tile(s, ROW_TILE)

    def body(d_ref, m_ref, sc_ref, o_ref, ds_ref):
        @pl.when(pl.program_id(0) == 0)
        def _():
            ds_ref[...] = jnp.zeros_like(ds_ref)

        dv = d_ref[...]
        ds_ref[...] += jnp.sum(dv * m_ref[...], axis=0, keepdims=True)
        o_ref[...] = (dv * sc_ref[...]).astype(BF16)

    return pl.pallas_call(
        body, name=name, grid=(s // tr,), in_specs=[_row_spec(tr, bw), _row_spec(tr, bw), _vec_spec(bw)],
        out_specs=[_row_spec(tr, bw), _vec_spec(bw)],
        out_shape=[jax.ShapeDtypeStruct((s, bw), BF16), jax.ShapeDtypeStruct((1, bw), F32)],
        compiler_params=_params("arbitrary"),
    )(dms, mixed, scale)


def _adam_update(w, g, m, v):
    m = ADAM_B1 * m + (1.0 - ADAM_B1) * g
    v = ADAM_B2 * v + (1.0 - ADAM_B2) * (g * g)
    m_hat = m / (1.0 - ADAM_B1 ** ADAM_STEP)
    v_hat = v / (1.0 - ADAM_B2 ** ADAM_STEP)
    delta = -ADAM_LR * (m_hat / (jnp.sqrt(v_hat) + ADAM_EPS) + ADAM_WD * w)
    return delta, m, v


def _adam_shard(name, w, m, v, parts):
    nl, r, c = w.shape
    nj = parts.shape[0]
    tr = _tile(r, 128)
    spec = pl.BlockSpec((None, tr, c), lambda l, i: (l, i, 0))

    def body(w_ref, m_ref, v_ref, p_ref, g_ref, d_ref, nm_ref, nv_ref):
        g = p_ref[0].astype(F32)
        for j in range(1, nj):
            g = g + p_ref[j].astype(F32)
        delta, nm, nv = _adam_update(w_ref[...], g, m_ref[...], v_ref[...])
        g_ref[...] = g
        d_ref[...] = delta
        nm_ref[...] = nm
        nv_ref[...] = nv

    return pl.pallas_call(
        body, name=name, grid=(nl, r // tr),
        in_specs=[spec, spec, spec, pl.BlockSpec((nj, None, tr, c), lambda l, i: (0, l, i, 0))],
        out_specs=[spec] * 4, out_shape=[jax.ShapeDtypeStruct(w.shape, F32)] * 4,
        compiler_params=_params("parallel", "parallel"),
    )(w, m, v, parts)


def _adam_packed(name, w, g, m, v):
    rows, lanes = w.shape
    tr = _tile(rows, 512)
    spec = pl.BlockSpec((tr, lanes), lambda i: (i, 0))

    def body(w_ref, g_ref, m_ref, v_ref, d_ref, nm_ref, nv_ref):
        delta, nm, nv = _adam_update(w_ref[...], g_ref[...], m_ref[...], v_ref[...])
        d_ref[...] = delta
        nm_ref[...] = nm
        nv_ref[...] = nv

    return pl.pallas_call(
        body, name=name, grid=(rows // tr,), in_specs=[spec] * 4, out_specs=[spec] * 3,
        out_shape=[jax.ShapeDtypeStruct(w.shape, F32)] * 3, compiler_params=_params("parallel"),
    )(w, g, m, v)


_ANY = pl.BlockSpec(memory_space=pl.ANY)


def _place():
    x, y, c = lax.axis_index("x"), lax.axis_index("y"), lax.axis_index("c")
    chips = [(1 - x, y), (x, 1 - y), (1 - x, 1 - y)]
    return x, y, c, chips


def _remote(src, dst, send_sem, recv_sem, device):
    return pltpu.make_async_remote_copy(src_ref=src, dst_ref=dst, send_sem=send_sem, recv_sem=recv_sem,
                                        device_id=device, device_id_type=MESH)


def _half(ref_rows, cc):
    h = ref_rows // 2
    return pl.ds(cc * h, h)


def _allgather(name, shards, split):
    n = len(shards)

    def body(*refs):
        ins, outs = refs[:n], refs[n:2 * n]
        send, recv, fsend, frecv, lsem = refs[2 * n:]
        x, y, c, chips = _place()
        jme = 2 * x + y
        sib = (x, y, 1 - c)

        def block(a, j, cc):
            if split[a]:
                return outs[a].at[j, :, _half(ins[a].shape[1], cc), :]
            return outs[a].at[j]

        def first(a, qi):
            qx, qy = chips[qi]
            src = ins[a].at[:, _half(ins[a].shape[1], c), :] if split[a] else ins[a]
            return _remote(src, block(a, jme, c), send.at[a, qi], recv.at[a, qi], (qx, qy, c))

        def arrival(a, qi):
            qx, qy = chips[qi]
            blk = block(a, 2 * qx + qy, c)
            return _remote(blk, blk, send.at[a, qi], recv.at[a, qi], (qx, qy, c))

        def relay(a, qi, cc):
            qx, qy = chips[qi]
            blk = block(a, 2 * qx + qy, cc)
            return _remote(blk, blk, fsend.at[a, qi], frecv.at[a, qi], sib)

        local = [pltpu.make_async_copy(ins[a], outs[a].at[jme], lsem.at[a]) for a in range(n)]
        for cp in local:
            cp.start()
        firsts = [first(a, qi) for a in range(n) for qi in range(3)]
        for cp in firsts:
            cp.start()
        relays = []
        for a in range(n):
            for qi in range(3):
                arrival(a, qi).wait_recv()
                if split[a]:
                    relays.append(relay(a, qi, c))
                    relays[-1].start()
        for a in range(n):
            if split[a]:
                for qi in range(3):
                    relay(a, qi, 1 - c).wait_recv()
        for cp in firsts + relays:
            cp.wait_send()
        for cp in local:
            cp.wait()

    return pl.pallas_call(
        body, name=name, in_specs=[_ANY] * n, out_specs=[_ANY] * n,
        out_shape=[jax.ShapeDtypeStruct((N_CHIPS, *sh.shape), sh.dtype) for sh in shards],
        scratch_shapes=[pltpu.SemaphoreType.DMA((n, 3))] * 4 + [pltpu.SemaphoreType.DMA((n,))],
    )(*shards)


def _pair_exchange(name, grads):
    n = len(grads)

    def body(*refs):
        ins, outs = refs[:n], refs[n:2 * n]
        send, recv = refs[2 * n:]
        x, y, c, _ = _place()
        copies = [_remote(ins[a].at[:, :, _half(ins[a].shape[2], 1 - c), :], outs[a], send.at[a], recv.at[a],
                          (x, y, 1 - c)) for a in range(n)]
        for cp in copies:
            cp.start()
        for cp in copies:
            cp.wait()

    return pl.pallas_call(
        body, name=name, in_specs=[_ANY] * n, out_specs=[_ANY] * n,
        out_shape=[jax.ShapeDtypeStruct((g.shape[0], g.shape[1], g.shape[2] // 2, g.shape[3]), g.dtype)
                   for g in grads],
        scratch_shapes=[pltpu.SemaphoreType.DMA((n,))] * 2,
    )(*grads)


def _pair_sum(name, grad, other, core):
    nj, nl, r, c = grad.shape
    h = r // 2
    tr = _tile(h, 256)
    nb = h // tr

    def body(core_ref, g_ref, o_ref, q_ref):
        q_ref[...] = (g_ref[...] + o_ref[...]).astype(BF16)

    blk = (None, None, tr, c)
    return pl.pallas_call(
        body, name=name,
        grid_spec=pltpu.PrefetchScalarGridSpec(
            num_scalar_prefetch=1, grid=(nj, nl, nb),
            in_specs=[pl.BlockSpec(blk, lambda j, l, i, core_ref: (j, l, core_ref[0] * nb + i, 0)),
                      pl.BlockSpec(blk, lambda j, l, i, core_ref: (j, l, i, 0))],
            out_specs=pl.BlockSpec(blk, lambda j, l, i, core_ref: (j, l, i, 0))),
        out_shape=jax.ShapeDtypeStruct((nj, nl, h, c), BF16),
        compiler_params=_params("parallel", "parallel", "parallel"),
    )(core, grad, other)


def _scatter(name, sums):
    n = len(sums)

    def body(*refs):
        ins, outs = refs[:n], refs[n:2 * n]
        send, recv, fsend, frecv, lsem = refs[2 * n:]
        x, y, c, chips = _place()
        jme = 2 * x + y
        sib = (x, y, 1 - c)
        origin = [2 * qx + qy for qx, qy in chips] + [jme]

        def block(a, j, cc):
            return outs[a].at[j, :, _half(outs[a].shape[2], cc), :]

        def first(a, qi):
            qx, qy = chips[qi]
            return _remote(ins[a].at[2 * qx + qy], block(a, jme, c), send.at[a, qi], recv.at[a, qi], (qx, qy, c))

        def arrival(a, qi):
            qx, qy = chips[qi]
            blk = block(a, 2 * qx + qy, c)
            return _remote(blk, blk, send.at[a, qi], recv.at[a, qi], (qx, qy, c))

        def relay(a, k, cc, src=None):
            blk = block(a, origin[k], cc)
            return _remote(blk if src is None else src, blk, fsend.at[a, k], frecv.at[a, k], sib)

        local = [pltpu.make_async_copy(ins[a].at[jme], block(a, jme, c), lsem.at[a]) for a in range(n)]
        for cp in local:
            cp.start()
        firsts = [first(a, qi) for a in range(n) for qi in range(3)]
        for cp in firsts:
            cp.start()
        relays = [relay(a, 3, c, src=ins[a].at[jme]) for a in range(n)]
        for cp in relays:
            cp.start()
        for a in range(n):
            for qi in range(3):
                arrival(a, qi).wait_recv()
                relays.append(relay(a, qi, c))
                relays[-1].start()
        for a in range(n):
            for k in range(4):
                relay(a, k, 1 - c).wait_recv()
        for cp in firsts + relays:
            cp.wait_send()
        for cp in local:
            cp.wait()

    return pl.pallas_call(
        body, name=name, in_specs=[_ANY] * n, out_specs=[_ANY] * n,
        out_shape=[jax.ShapeDtypeStruct((q.shape[0], q.shape[1], 2 * q.shape[2], q.shape[3]), q.dtype)
                   for q in sums],
        scratch_shapes=[pltpu.SemaphoreType.DMA((n, 3))] * 2 + [pltpu.SemaphoreType.DMA((n, 4))] * 2
        + [pltpu.SemaphoreType.DMA((n,))],
    )(*sums)


def _allreduce_small(name, packed):
    rows, lanes = packed.shape

    def body(x_ref, o_ref, all_ref, send, recv, lsem):
        x, y, c, chips = _place()
        me, sib = (x, y, c), (x, y, 1 - c)

        def slot(px, py, pc):
            return all_ref.at[pl.ds((4 * px + 2 * py + pc) * rows, rows), :]

        def copy(k, blk, to, src=None):
            return _remote(slot(*blk) if src is None else src, slot(*blk), send.at[k], recv.at[k], to)

        mine = pltpu.make_async_copy(x_ref, slot(*me), lsem)
        mine.start()
        first = [copy(0, me, sib, src=x_ref)] + [copy(1 + j, me, (*chip, c), src=x_ref) for j, chip in enumerate(chips)]
        for cp in first:
            cp.start()
        passed = [copy(4 + j, (*chip, c), sib) for j, chip in enumerate(chips)]
        for j, chip in enumerate(chips):
            copy(1 + j, (*chip, c), me).wait_recv()
            passed[j].start()
        copy(0, sib, me).wait_recv()
        for j, chip in enumerate(chips):
            copy(4 + j, (*chip, 1 - c), me).wait_recv()
        for cp in first + passed:
            cp.wait_send()
        mine.wait()
        total = all_ref[pl.ds(0, rows), :]
        for d in range(1, N_DEV):
            total = total + all_ref[pl.ds(d * rows, rows), :]
        o_ref[...] = total

    vmem = pl.BlockSpec(memory_space=pltpu.VMEM)
    return pl.pallas_call(
        body, name=name, in_specs=[vmem], out_specs=vmem, out_shape=jax.ShapeDtypeStruct(packed.shape, F32),
        scratch_shapes=[pltpu.VMEM((N_DEV * rows, lanes), F32), pltpu.SemaphoreType.DMA((7,)),
                        pltpu.SemaphoreType.DMA((7,)), pltpu.SemaphoreType.DMA],
        compiler_params=pltpu.CompilerParams(vmem_limit_bytes=VMEM_LIMIT),
    )(packed)


def _pack(parts):
    rows = []
    for p in parts:
        flat = p.reshape(-1)
        pad = (-flat.shape[0]) % PACK_ELEMS
        rows.append(jnp.pad(flat, (0, pad)).reshape(-1, LANES))
    return jnp.concatenate(rows, axis=0)


def _unpack(packed, shapes):
    out, row = [], 0
    for sh in shapes:
        size = math.prod(sh)
        nrows = -(-size // PACK_ELEMS) * (PACK_ELEMS // LANES)
        out.append(packed[row:row + nrows].reshape(-1)[:size].reshape(sh))
        row += nrows
    return out


def kernel(x, a_w_in, a_ln_g, a_ln_b, a_w_s, a_b_s, a_w_out, b_w_in, b_w_grp, b_scale, b_w_out, norm_mix, norm_mlp, mlp_w1, mlp_w2, final_norm, loss_target, m_a_w_in, m_a_ln_g, m_a_ln_b, m_a_w_s, m_a_b_s, m_a_w_out, m_b_w_in, m_b_w_grp, m_b_scale, m_b_w_out, m_norm_mix, m_norm_mlp, m_mlp_w1, m_mlp_w2, m_final_norm, v_a_w_in, v_a_ln_g, v_a_ln_b, v_a_w_s, v_a_b_s, v_a_w_out, v_b_w_in, v_b_w_grp, v_b_scale, v_b_w_out, v_norm_mix, v_norm_mlp, v_mlp_w1, v_mlp_w2, v_final_norm):
    xi, yi, ci = lax.axis_index("x"), lax.axis_index("y"), lax.axis_index("c")
    chip = 2 * xi + yi
    x2, tgt = x[0], loss_target[0]
    bw = b_scale.shape[1] * N_CHIPS

    big = dict(a_w_in=a_w_in, a_w_out=a_w_out, b_w_in=b_w_in, b_w_grp=b_w_grp[0], b_w_out=b_w_out,
               mlp_w1=mlp_w1, mlp_w2=mlp_w2)
    names = list(big)
    gathered = _allgather("gather_weights", [big[k].astype(BF16) for k in names] + [b_scale.reshape(1, 1, -1)],
                          [True] * len(names) + [False])
    col_sharded = dict(a_w_in=True, a_w_out=False, b_w_in=False, b_w_grp=False, b_w_out=False,
                       mlp_w1=True, mlp_w2=False)
    W = {k: _W4(gathered[i], col_sharded[k]) for i, k in enumerate(names)}
    scale_full = gathered[-1].reshape(1, bw)
    b_col = a_b_s[0].T

    def residual(acc, res):
        return (res + acc,)

    def sq_relu(acc):
        act = jnp.maximum(acc, 0.0)
        return act, act * act

    def mlp_fwd(tag, h, layer):
        hn = _rms_fwd(f"mlp{tag}_norm", h, norm_mlp[layer:layer + 1])
        act, act_sq = _mm_aw(f"mlp{tag}_up", hn, W["mlp_w1"], layer=layer, out_dtypes=(F32, BF16), epilogue=sq_relu)
        out = _mm_aw(f"mlp{tag}_down", act_sq, W["mlp_w2"], layer=layer, extras=(h,), epilogue=residual)
        return out, (h, hn, act, act_sq)

    hn0 = _rms_fwd("mix_a_norm", x2, norm_mix[0:1])
    zpre = _mm_aw("mix_a_in", hn0, W["a_w_in"])
    gated = _gate_fwd("mix_a_gate", zpre, a_ln_g, a_ln_b, a_w_s[0], b_col)
    h1 = _mm_aw("mix_a_out", gated, W["a_w_out"], extras=(x2,), epilogue=residual)
    h2, mlp0 = mlp_fwd("0", h1, 0)
    hn2 = _rms_fwd("mix_b_norm", h2, norm_mix[1:2])
    vb = _mm_aw("mix_b_in", hn2, W["b_w_in"])
    pooled = _pool_fwd("mix_b_pool", vb)
    mixed, ms = _mm_aw("mix_b_grp", pooled, W["b_w_grp"], groups=len(B_WINDOWS), extras=(scale_full,),
                       out_dtypes=(F32, BF16), epilogue=lambda acc, sc: (acc, acc * sc))
    h3 = _mm_aw("mix_b_out", ms, W["b_w_out"], extras=(h2,), epilogue=residual)
    h4, mlp1 = mlp_fwd("1", h3, 1)
    dh4, d_final, loss_part = _final("loss_head", h4, final_norm.reshape(1, -1), tgt)

    def mlp_bwd(tag, dh, saved, layer, g_w1, g_w2):
        h, hn, act, act_sq = saved
        dpre = _mm_aw(f"mlp{tag}_down_dx", dh, W["mlp_w2"], layer=layer, transpose_w=True, extras=(act,),
                      out_dtypes=(BF16,), epilogue=lambda acc, a: (acc * (2.0 * a),))
        g_w2 = _mm_dw(f"mlp{tag}_down_dw", act_sq, dh, W["mlp_w2"], layer=layer, prev=g_w2)
        dhn = _mm_aw(f"mlp{tag}_up_dx", dpre, W["mlp_w1"], layer=layer, transpose_w=True)
        g_w1 = _mm_dw(f"mlp{tag}_up_dw", hn, dpre, W["mlp_w1"], layer=layer, prev=g_w1)
        dh_in, d_norm = _rms_bwd(f"mlp{tag}_norm_bwd", h, norm_mlp[layer:layer + 1], dhn, dh)
        return dh_in, d_norm, g_w1, g_w2

    dh3, d_norm_mlp1, g_w1, g_w2 = mlp_bwd("1", dh4, mlp1, 1, None, None)
    dms = _mm_aw("mix_b_out_dx", dh3, W["b_w_out"], transpose_w=True)
    g_b_out = _mm_dw("mix_b_out_dw", ms, dh3, W["b_w_out"])
    dmixed, d_scale = _scale_bwd("mix_b_scale_bwd", dms, mixed, scale_full)
    dpooled = _mm_aw("mix_b_grp_dx", dmixed, W["b_w_grp"], groups=len(B_WINDOWS), transpose_w=True)
    g_b_grp = _mm_dw("mix_b_grp_dw", pooled, dmixed, W["b_w_grp"], groups=len(B_WINDOWS))
    dvb = _pool_bwd("mix_b_pool_bwd", dpooled)
    dhn2 = _mm_aw("mix_b_in_dx", dvb, W["b_w_in"], transpose_w=True)
    g_b_in = _mm_dw("mix_b_in_dw", hn2, dvb, W["b_w_in"])
    dh2, d_norm_mix1 = _rms_bwd("mix_b_norm_bwd", h2, norm_mix[1:2], dhn2, dh3)
    dh1, d_norm_mlp0, g_w1, g_w2 = mlp_bwd("0", dh2, mlp0, 0, g_w1, g_w2)
    dgated = _mm_aw("mix_a_out_dx", dh1, W["a_w_out"], transpose_w=True)
    g_a_out = _mm_dw("mix_a_out_dw", gated, dh1, W["a_w_out"])
    dzpre, d_w_s, d_b_col, d_ln_g, d_ln_b = _gate_bwd("mix_a_gate_bwd", zpre, dgated, a_ln_g, a_ln_b, a_w_s[0], b_col)
    dhn0 = _mm_aw("mix_a_in_dx", dzpre, W["a_w_in"], transpose_w=True)
    g_a_in = _mm_dw("mix_a_in_dw", hn0, dzpre, W["a_w_in"])
    dx, d_norm_mix0 = _rms_bwd("mix_a_norm_bwd", x2, norm_mix[0:1], dhn0, dh1)

    grads = dict(a_w_in=g_a_in, a_w_out=g_a_out, b_w_in=g_b_in, b_w_grp=g_b_grp, b_w_out=g_b_out,
                 mlp_w1=g_w1, mlp_w2=g_w2)
    theirs = _pair_exchange("grad_pair_exchange", [grads[k] for k in names])
    core = ci.astype(jnp.int32).reshape(1)
    sums = [_pair_sum(f"grad_pair_sum_{k}", grads[k], theirs[i], core) for i, k in enumerate(names)]
    parts = _scatter("grad_scatter", sums)
    moments = dict(a_w_in=(m_a_w_in, v_a_w_in), a_w_out=(m_a_w_out, v_a_w_out), b_w_in=(m_b_w_in, v_b_w_in),
                   b_w_grp=(m_b_w_grp, v_b_w_grp), b_w_out=(m_b_w_out, v_b_w_out),
                   mlp_w1=(m_mlp_w1, v_mlp_w1), mlp_w2=(m_mlp_w2, v_mlp_w2))
    weights = dict(a_w_in=a_w_in, a_w_out=a_w_out, b_w_in=b_w_in, b_w_grp=b_w_grp, b_w_out=b_w_out,
                   mlp_w1=mlp_w1, mlp_w2=mlp_w2)
    grad_out, delta_out, m_out, v_out = {}, {}, {}, {}
    for i, k in enumerate(names):
        shard_shape = parts[i].shape[1:]
        res = _adam_shard(f"adam_{k}", weights[k].reshape(shard_shape), moments[k][0].reshape(shard_shape),
                          moments[k][1].reshape(shard_shape), parts[i])
        grad_out[k], delta_out[k], m_out[k], v_out[k] = [r.reshape(weights[k].shape) for r in res]

    small = dict(a_ln_g=(a_ln_g, m_a_ln_g, v_a_ln_g), a_ln_b=(a_ln_b, m_a_ln_b, v_a_ln_b),
                 a_w_s=(a_w_s, m_a_w_s, v_a_w_s), a_b_s=(a_b_s, m_a_b_s, v_a_b_s),
                 b_scale=(b_scale, m_b_scale, v_b_scale), norm_mix=(norm_mix, m_norm_mix, v_norm_mix),
                 norm_mlp=(norm_mlp, m_norm_mlp, v_norm_mlp), final_norm=(final_norm, m_final_norm, v_final_norm))
    small_names = list(small)
    local = dict(a_ln_g=d_ln_g, a_ln_b=d_ln_b, a_w_s=d_w_s[None], a_b_s=d_b_col.T[None], b_scale=d_scale,
                 norm_mix=jnp.concatenate([d_norm_mix0, d_norm_mix1], axis=0),
                 norm_mlp=jnp.concatenate([d_norm_mlp0, d_norm_mlp1], axis=0), final_norm=d_final.reshape(-1))
    reduced = _allreduce_small("small_grad_allreduce", _pack([local[k] for k in small_names]))
    small_grads = dict(zip(small_names, _unpack(reduced, [local[k].shape for k in small_names])))
    shard_w = b_scale.shape[1]
    small_grads["b_scale"] = lax.dynamic_slice_in_dim(small_grads["b_scale"], chip * shard_w, shard_w, axis=1)
    small_grads = {k: small_grads[k].reshape(small[k][0].shape) for k in small_names}
    packed = [_pack([small[k][i] for k in small_names]) for i in range(3)]
    res = _adam_packed("adam_small", packed[0], _pack([small_grads[k] for k in small_names]), packed[1], packed[2])
    shapes = [small[k][0].shape for k in small_names]
    for k, d, nm, nv in zip(small_names, *[_unpack(r, shapes) for r in res]):
        grad_out[k], delta_out[k], m_out[k], v_out[k] = small_grads[k], d, nm, nv

    loss = lax.psum(loss_part[0, 0], ("x", "y", "c"))
    order = ["a_w_in", "a_ln_g", "a_ln_b", "a_w_s", "a_b_s", "a_w_out", "b_w_in", "b_w_grp", "b_scale", "b_w_out",
             "norm_mix", "norm_mlp", "mlp_w1", "mlp_w2", "final_norm"]
    return (loss, dx[None], *[grad_out[k] for k in order], *[delta_out[k] for k in order],
            *[m_out[k] for k in order], *[v_out[k] for k in order])
```

```python
import math

import jax
import jax.numpy as jnp
from jax import lax
from jax.experimental import pallas as pl
from jax.experimental.pallas import tpu as pltpu

F32 = jnp.float32
BF16 = jnp.bfloat16
MESH = pl.DeviceIdType.MESH

EPS = 1e-6
B_WINDOWS = (2, 4, 8, 16)
ADAM_LR = 0.001
ADAM_B1 = 0.9
ADAM_B2 = 0.999
ADAM_EPS = 1e-08
ADAM_WD = 0.01
ADAM_STEP = 10

N_CHIPS = 4
N_DEV = 8
LANES = 128
PACK_ELEMS = 8 * LANES
VMEM_LIMIT = 56 * 1024 * 1024
ROW_TILE = 256
MM_TM, MM_TN, MM_TK = 1024, 1024, 512


def _tile(dim, pref):
    t = min(dim, pref)
    while dim % t:
        t //= 2
    return t


def _params(*sem):
    return pltpu.CompilerParams(dimension_semantics=sem, vmem_limit_bytes=VMEM_LIMIT)


class _W4:
    def __init__(self, arr, col_sharded, shape=None):
        self.arr = arr
        self.nj, self.nl, self.r, self.c = arr.shape if shape is None else shape
        self.col = col_sharded
        self.rows = self.r if col_sharded else self.nj * self.r
        self.cols = self.nj * self.c if col_sharded else self.c

    def tile_rows(self, pref):
        return _tile(self.r, pref)

    def tile_cols(self, pref):
        return _tile(self.c, pref)

    def index(self, layer, rb, cb, tr, tc):
        if self.col:
            n = self.c // tc
            return (cb // n, layer, rb, cb % n)
        n = self.r // tr
        return (rb // n, layer, rb % n, cb)


def _mm_aw(name, a, w, *, layer=0, groups=1, transpose_w=False, extras=(), out_dtypes=(F32,), epilogue=None):
    s, ka_total = a.shape
    kdim, ndim = (w.cols, w.rows) if transpose_w else (w.rows, w.cols)
    assert ka_total == groups * kdim, (name, a.shape, kdim, groups)
    tm = _tile(s, MM_TM)
    if transpose_w:
        tn, tk = w.tile_rows(MM_TN), w.tile_cols(MM_TK)
    else:
        tk, tn = w.tile_rows(MM_TK), w.tile_cols(MM_TN)
    nk, nn = kdim // tk, ndim // tn

    def lay(g):
        return g if groups > 1 else layer

    a_spec = pl.BlockSpec((tm, tk), lambda g, i, n, k: (i, g * nk + k))
    if transpose_w:
        w_spec = pl.BlockSpec((None, None, tn, tk), lambda g, i, n, k: w.index(lay(g), n, k, tn, tk))
    else:
        w_spec = pl.BlockSpec((None, None, tk, tn), lambda g, i, n, k: w.index(lay(g), k, n, tk, tn))
    ex_specs = []
    for e in extras:
        assert e.shape[1] == groups * ndim and e.shape[0] in (1, s), (name, e.shape)
        if e.shape[0] == 1:
            ex_specs.append(pl.BlockSpec((1, tn), lambda g, i, n, k: (0, g * nn + n)))
        else:
            ex_specs.append(pl.BlockSpec((tm, tn), lambda g, i, n, k: (i, g * nn + n)))
    out_spec = pl.BlockSpec((tm, tn), lambda g, i, n, k: (i, g * nn + n))
    n_ex, n_out = len(extras), len(out_dtypes)

    def body(a_ref, w_ref, *rest):
        ex, outs, acc = rest[:n_ex], rest[n_ex:n_ex + n_out], rest[-1]
        k = pl.program_id(3)

        @pl.when(k == 0)
        def _():
            acc[...] = jnp.zeros_like(acc)

        av = a_ref[...]
        if av.dtype != BF16:
            av = av.astype(BF16)
        if transpose_w:
            acc[...] += lax.dot_general(av, w_ref[...], (((1,), (1,)), ((), ())), preferred_element_type=F32)
        else:
            acc[...] += jnp.dot(av, w_ref[...], preferred_element_type=F32)

        @pl.when(k == nk - 1)
        def _():
            vals = (acc[...],) if epilogue is None else epilogue(acc[...], *[e[...] for e in ex])
            for o, v in zip(outs, vals):
                o[...] = v.astype(o.dtype)

    outs = pl.pallas_call(
        body, name=name, grid=(groups, s // tm, nn, nk),
        in_specs=[a_spec, w_spec, *ex_specs], out_specs=[out_spec] * n_out,
        out_shape=[jax.ShapeDtypeStruct((s, groups * ndim), dt) for dt in out_dtypes],
        scratch_shapes=[pltpu.VMEM((tm, tn), F32)],
        compiler_params=_params("parallel", "parallel", "parallel", "arbitrary"),
    )(a, w.arr, *extras)
    return outs[0] if n_out == 1 else outs


def _mm_dw(name, a, b, like, *, layer=0, groups=1, prev=None):
    s, ka_total = a.shape
    rows, cols = ka_total // groups, b.shape[1] // groups
    assert (rows, cols) == (like.rows, like.cols) and b.shape[0] == s, (name, a.shape, b.shape)
    tm, tn, tk = like.tile_rows(MM_TM), like.tile_cols(MM_TN), _tile(s, MM_TK)
    nr, nc, nk = rows // tm, cols // tn, s // tk

    def lay(g):
        return g if groups > 1 else layer

    in_specs = [pl.BlockSpec((tk, tm), lambda g, i, n, k: (k, g * nr + i)),
                pl.BlockSpec((tk, tn), lambda g, i, n, k: (k, g * nc + n))]
    operands = [a, b]
    aliases = {}
    if prev is not None:
        in_specs.append(pl.BlockSpec(memory_space=pl.ANY))
        operands.append(prev)
        aliases = {2: 0}

    def body(a_ref, b_ref, *rest):
        o_ref, acc = rest[-2], rest[-1]
        k = pl.program_id(3)

        @pl.when(k == 0)
        def _():
            acc[...] = jnp.zeros_like(acc)

        av, bv = a_ref[...], b_ref[...]
        if av.dtype != BF16:
            av = av.astype(BF16)
        if bv.dtype != BF16:
            bv = bv.astype(BF16)
        acc[...] += lax.dot_general(av, bv, (((0,), (0,)), ((), ())), preferred_element_type=F32)

        @pl.when(k == nk - 1)
        def _():
            o_ref[...] = acc[...]

    return pl.pallas_call(
        body, name=name, grid=(groups, nr, nc, nk), in_specs=in_specs,
        out_specs=pl.BlockSpec((None, None, tm, tn), lambda g, i, n, k: like.index(lay(g), i, n, tm, tn)),
        out_shape=jax.ShapeDtypeStruct((like.nj, like.nl, like.r, like.c), F32),
        scratch_shapes=[pltpu.VMEM((tm, tn), F32)], input_output_aliases=aliases,
        compiler_params=_params("parallel", "parallel", "parallel", "arbitrary"),
    )(*operands)


def _row_spec(tr, d):
    return pl.BlockSpec((tr, d), lambda i: (i, 0))


def _vec_spec(d):
    return pl.BlockSpec((1, d), lambda i: (0, 0))


def _rms_fwd(name, x, g):
    s, d = x.shape
    tr = _tile(s, ROW_TILE)

    def body(x_ref, g_ref, o_ref):
        xv = x_ref[...]
        r = lax.rsqrt(jnp.mean(xv * xv, axis=-1, keepdims=True) + EPS)
        o_ref[...] = (xv * r * g_ref[...]).astype(BF16)

    return pl.pallas_call(
        body, name=name, grid=(s // tr,), in_specs=[_row_spec(tr, d), _vec_spec(d)],
        out_specs=_row_spec(tr, d), out_shape=jax.ShapeDtypeStruct((s, d), BF16),
        compiler_params=_params("parallel"),
    )(x, g)


def _rms_bwd(name, x, g, dhn, dres):
    s, d = x.shape
    tr = _tile(s, ROW_TILE)

    def body(x_ref, g_ref, dhn_ref, dres_ref, dx_ref, dg_ref):
        @pl.when(pl.program_id(0) == 0)
        def _():
            dg_ref[...] = jnp.zeros_like(dg_ref)

        xv = x_ref[...]
        r = lax.rsqrt(jnp.mean(xv * xv, axis=-1, keepdims=True) + EPS)
        xh = xv * r
        dy = dhn_ref[...]
        dg_ref[...] += jnp.sum(dy * xh, axis=0, keepdims=True)
        dxh = dy * g_ref[...]
        dx_ref[...] = dres_ref[...] + r * (dxh - xh * jnp.mean(dxh * xh, axis=-1, keepdims=True))

    return pl.pallas_call(
        body, name=name, grid=(s // tr,),
        in_specs=[_row_spec(tr, d), _vec_spec(d), _row_spec(tr, d), _row_spec(tr, d)],
        out_specs=[_row_spec(tr, d), _vec_spec(d)],
        out_shape=[jax.ShapeDtypeStruct((s, d), F32), jax.ShapeDtypeStruct((1, d), F32)],
        compiler_params=_params("arbitrary"),
    )(x, g, dhn, dres)


def _final(name, h, g, tgt):
    s, d = h.shape
    tr = _tile(s, ROW_TILE)

    def body(h_ref, g_ref, t_ref, dh_ref, dg_ref, loss_ref):
        @pl.when(pl.program_id(0) == 0)
        def _():
            dg_ref[...] = jnp.zeros_like(dg_ref)
            loss_ref[...] = jnp.zeros_like(loss_ref)

        hv = h_ref[...]
        r = lax.rsqrt(jnp.mean(hv * hv, axis=-1, keepdims=True) + EPS)
        xh = hv * r
        gv = g_ref[...]
        err = xh * gv - t_ref[...]
        part = 0.5 * jnp.sum(jnp.mean(err * err, axis=-1, keepdims=True), axis=0, keepdims=True)
        loss_ref[...] += jnp.broadcast_to(part, loss_ref.shape)
        dy = err * (1.0 / d)
        dg_ref[...] += jnp.sum(dy * xh, axis=0, keepdims=True)
        dxh = dy * gv
        dh_ref[...] = r * (dxh - xh * jnp.mean(dxh * xh, axis=-1, keepdims=True))

    return pl.pallas_call(
        body, name=name, grid=(s // tr,),
        in_specs=[_row_spec(tr, d), _vec_spec(d), _row_spec(tr, d)],
        out_specs=[_row_spec(tr, d), _vec_spec(d), _vec_spec(LANES)],
        out_shape=[jax.ShapeDtypeStruct((s, d), F32), jax.ShapeDtypeStruct((1, d), F32),
                   jax.ShapeDtypeStruct((1, LANES), F32)],
        compiler_params=_params("arbitrary"),
    )(h, g, tgt)


_SQRT_HALF = 1.0 / math.sqrt(2.0)
_INV_SQRT_2PI = 1.0 / math.sqrt(2.0 * math.pi)


def _gelu(x):
    return x * (lax.erf(x * _SQRT_HALF) + 1.0) * 0.5


def _gelu_grad(x):
    return 0.5 * (lax.erf(x * _SQRT_HALF) + 1.0) + x * jnp.exp(-0.5 * x * x) * _INV_SQRT_2PI


def _causal(chunk):
    t = lax.broadcasted_iota(jnp.int32, (chunk, chunk), 0)
    sidx = lax.broadcasted_iota(jnp.int32, (chunk, chunk), 1)
    return sidx <= t


def _layernorm_parts(v, g, b):
    mu = jnp.mean(v, axis=-1, keepdims=True)
    vc = v - mu
    rs = lax.rsqrt(jnp.mean(vc * vc, axis=-1, keepdims=True) + EPS)
    vhat = vc * rs
    return vhat, rs, vhat * g + b


def _gate_fwd(name, zpre, ln_g, ln_b, w_s, b_col):
    s, aw2 = zpre.shape
    aw = aw2 // 2
    ng, chunk, _ = w_s.shape
    dh = aw // ng

    def body(z_ref, g_ref, b_ref, ws_ref, bc_ref, o_ref):
        u = _gelu(z_ref[:, :aw])
        v = _gelu(z_ref[:, aw:])
        _, _, vln = _layernorm_parts(v, g_ref[...], b_ref[...])
        mask = _causal(chunk)
        for gi in range(ng):
            sl = slice(gi * dh, (gi + 1) * dh)
            wm = jnp.where(mask, ws_ref[gi], 0.0).astype(BF16)
            sg = jnp.dot(wm, vln[:, sl].astype(BF16), preferred_element_type=F32) + bc_ref[:, gi:gi + 1]
            o_ref[:, sl] = (u[:, sl] * sg).astype(BF16)

    return pl.pallas_call(
        body, name=name, grid=(s // chunk,),
        in_specs=[_row_spec(chunk, aw2), _vec_spec(aw), _vec_spec(aw),
                  pl.BlockSpec((ng, chunk, chunk), lambda i: (0, 0, 0)),
                  pl.BlockSpec((chunk, ng), lambda i: (0, 0))],
        out_specs=_row_spec(chunk, aw), out_shape=jax.ShapeDtypeStruct((s, aw), BF16),
        compiler_params=_params("parallel"),
    )(zpre, ln_g, ln_b, w_s, b_col)


def _gate_bwd(name, zpre, dgated, ln_g, ln_b, w_s, b_col):
    s, aw2 = zpre.shape
    aw = aw2 // 2
    ng, chunk, _ = w_s.shape
    dh = aw // ng

    def body(z_ref, dgt_ref, g_ref, b_ref, ws_ref, bc_ref, dz_ref, dws_ref, dbc_ref, dlg_ref, dlb_ref,
             du_scr, dvln_scr):
        @pl.when(pl.program_id(0) == 0)
        def _():
            dws_ref[...] = jnp.zeros_like(dws_ref)
            dbc_ref[...] = jnp.zeros_like(dbc_ref)
            dlg_ref[...] = jnp.zeros_like(dlg_ref)
            dlb_ref[...] = jnp.zeros_like(dlb_ref)

        zu = z_ref[:, :aw]
        zv = z_ref[:, aw:]
        u = _gelu(zu)
        lg = g_ref[...]
        vhat, rs, vln = _layernorm_parts(_gelu(zv), lg, b_ref[...])
        mask = _causal(chunk)
        for gi in range(ng):
            sl = slice(gi * dh, (gi + 1) * dh)
            wm = jnp.where(mask, ws_ref[gi], 0.0).astype(BF16)
            vg = vln[:, sl].astype(BF16)
            sg = jnp.dot(wm, vg, preferred_element_type=F32) + bc_ref[:, gi:gi + 1]
            dgt = dgt_ref[:, sl]
            du_scr[:, sl] = dgt * sg
            ds = dgt * u[:, sl]
            dbc_ref[:, gi:gi + 1] += jnp.sum(ds, axis=-1, keepdims=True)
            dsb = ds.astype(BF16)
            dwm = lax.dot_general(dsb, vg, (((1,), (1,)), ((), ())), preferred_element_type=F32)
            dws_ref[gi] += jnp.where(mask, dwm, 0.0)
            dvln_scr[:, sl] = lax.dot_general(wm, dsb, (((0,), (0,)), ((), ())), preferred_element_type=F32)
        dvln = dvln_scr[...]
        dlb_ref[...] += jnp.sum(dvln, axis=0, keepdims=True)
        dlg_ref[...] += jnp.sum(dvln * vhat, axis=0, keepdims=True)
        dvh = dvln * lg
        dv = rs * (dvh - jnp.mean(dvh, axis=-1, keepdims=True)
                   - vhat * jnp.mean(dvh * vhat, axis=-1, keepdims=True))
        dz_ref[:, :aw] = (du_scr[...] * _gelu_grad(zu)).astype(BF16)
        dz_ref[:, aw:] = (dv * _gelu_grad(zv)).astype(BF16)

    return pl.pallas_call(
        body, name=name, grid=(s // chunk,),
        in_specs=[_row_spec(chunk, aw2), _row_spec(chunk, aw), _vec_spec(aw), _vec_spec(aw),
                  pl.BlockSpec((ng, chunk, chunk), lambda i: (0, 0, 0)),
                  pl.BlockSpec((chunk, ng), lambda i: (0, 0))],
        out_specs=[_row_spec(chunk, aw2), pl.BlockSpec((ng, chunk, chunk), lambda i: (0, 0, 0)),
                   pl.BlockSpec((chunk, ng), lambda i: (0, 0)), _vec_spec(aw), _vec_spec(aw)],
        out_shape=[jax.ShapeDtypeStruct((s, aw2), BF16), jax.ShapeDtypeStruct((ng, chunk, chunk), F32),
                   jax.ShapeDtypeStruct((chunk, ng), F32), jax.ShapeDtypeStruct((1, aw), F32),
                   jax.ShapeDtypeStruct((1, aw), F32)],
        scratch_shapes=[pltpu.VMEM((chunk, aw), F32), pltpu.VMEM((chunk, aw), F32)],
        compiler_params=_params("arbitrary"),
    )(zpre, dgated, ln_g, ln_b, w_s, b_col)


def _pool_select(g, parts):
    out = parts[-1]
    for gi in range(len(parts) - 2, -1, -1):
        out = jnp.where(g == gi, parts[gi], out)
    return out


def _pool_specs(s, bw):
    head = bw // len(B_WINDOWS)
    tc = _tile(head, 256)
    nb = head // tc
    return tc, (len(B_WINDOWS), nb), pl.BlockSpec((s, tc), lambda g, j: (0, g * nb + j))


def _pool_window(g, t):
    w = _pool_select(g, [jnp.full(t.shape, wi, jnp.int32) for wi in B_WINDOWS])
    return jnp.minimum(t + 1, w).astype(F32)


def _pool_fwd(name, vb):
    assert B_WINDOWS == (2, 4, 8, 16)
    s, bw = vb.shape
    tc, grid, spec = _pool_specs(s, bw)

    def body(v_ref, o_ref):
        g = pl.program_id(0)
        v = v_ref[...]
        t = lax.broadcasted_iota(jnp.int32, (s, tc), 0)

        def down(x, k):
            return jnp.where(t >= k, pltpu.roll(x, k, 0), 0.0)

        sums, cur, k = [], v, 1
        for _ in B_WINDOWS:
            cur = cur + down(cur, k)
            sums.append(cur)
            k *= 2
        o_ref[...] = (_pool_select(g, sums) / _pool_window(g, t) - v).astype(BF16)

    return pl.pallas_call(
        body, name=name, grid=grid, in_specs=[spec], out_specs=spec,
        out_shape=jax.ShapeDtypeStruct((s, bw), BF16), compiler_params=_params("parallel", "parallel"),
    )(vb)


def _pool_bwd(name, dpooled):
    s, bw = dpooled.shape
    tc, grid, spec = _pool_specs(s, bw)

    def body(d_ref, o_ref):
        g = pl.program_id(0)
        dp = d_ref[...]
        t = lax.broadcasted_iota(jnp.int32, (s, tc), 0)

        def up(x, k):
            return jnp.where(t < s - k, pltpu.roll(x, s - k, 0), 0.0)

        sums, cur, k = [], dp / _pool_window(g, t), 1
        for _ in B_WINDOWS:
            cur = cur + up(cur, k)
            sums.append(cur)
            k *= 2
        o_ref[...] = (_pool_select(g, sums) - dp).astype(BF16)

    return pl.pallas_call(
        body, name=name, grid=grid, in_specs=[spec], out_specs=spec,
        out_shape=jax.ShapeDtypeStruct((s, bw), BF16), compiler_params=_params("parallel", "parallel"),
    )(dpooled)


def _scale_bwd(name, dms, mixed, scale):
    s, bw = dms.shape
    tr = _tile(s, ROW_TILE)

    def body(d_ref, m_ref, sc_ref, o_ref, ds_ref):
        @pl.when(pl.program_id(0) == 0)
        def _():
            ds_ref[...] = jnp.zeros_like(ds_ref)

        dv = d_ref[...]
        ds_ref[...] += jnp.sum(dv * m_ref[...], axis=0, keepdims=True)
        o_ref[...] = (dv * sc_ref[...]).astype(BF16)

    return pl.pallas_call(
        body, name=name, grid=(s // tr,), in_specs=[_row_spec(tr, bw), _row_spec(tr, bw), _vec_spec(bw)],
        out_specs=[_row_spec(tr, bw), _vec_spec(bw)],
        out_shape=[jax.ShapeDtypeStruct((s, bw), BF16), jax.ShapeDtypeStruct((1, bw), F32)],
        compiler_params=_params("arbitrary"),
    )(dms, mixed, scale)


def _adam_update(w, g, m, v):
    m = ADAM_B1 * m + (1.0 - ADAM_B1) * g
    v = ADAM_B2 * v + (1.0 - ADAM_B2) * (g * g)
    m_hat = m / (1.0 - ADAM_B1 ** ADAM_STEP)
    v_hat = v / (1.0 - ADAM_B2 ** ADAM_STEP)
    delta = -ADAM_LR * (m_hat / (jnp.sqrt(v_hat) + ADAM_EPS) + ADAM_WD * w)
    return delta, m, v


def _adam_shard(name, w, m, v, parts):
    nl, r, c = w.shape
    nj = parts.shape[0]
    tr = _tile(r, 128)
    spec = pl.BlockSpec((None, tr, c), lambda l, i: (l, i, 0))

    def body(w_ref, m_ref, v_ref, p_ref, g_ref, d_ref, nm_ref, nv_ref):
        g = p_ref[0].astype(F32)
        for j in range(1, nj):
            g = g + p_ref[j].astype(F32)
        delta, nm, nv = _adam_update(w_ref[...], g, m_ref[...], v_ref[...])
        g_ref[...] = g
        d_ref[...] = delta
        nm_ref[...] = nm
        nv_ref[...] = nv

    return pl.pallas_call(
        body, name=name, grid=(nl, r // tr),
        in_specs=[spec, spec, spec, pl.BlockSpec((nj, None, tr, c), lambda l, i: (0, l, i, 0))],
        out_specs=[spec] * 4, out_shape=[jax.ShapeDtypeStruct(w.shape, F32)] * 4,
        compiler_params=_params("parallel", "parallel"),
    )(w, m, v, parts)


def _adam_packed(name, w, g, m, v):
    rows, lanes = w.shape
    tr = _tile(rows, 512)
    spec = pl.BlockSpec((tr, lanes), lambda i: (i, 0))

    def body(w_ref, g_ref, m_ref, v_ref, d_ref, nm_ref, nv_ref):
        delta, nm, nv = _adam_update(w_ref[...], g_ref[...], m_ref[...], v_ref[...])
        d_ref[...] = delta
        nm_ref[...] = nm
        nv_ref[...] = nv

    return pl.pallas_call(
        body, name=name, grid=(rows // tr,), in_specs=[spec] * 4, out_specs=[spec] * 3,
        out_shape=[jax.ShapeDtypeStruct(w.shape, F32)] * 3, compiler_params=_params("parallel"),
    )(w, g, m, v)


_ANY = pl.BlockSpec(memory_space=pl.ANY)


def _place():
    x, y, c = lax.axis_index("x"), lax.axis_index("y"), lax.axis_index("c")
    chips = [(1 - x, y), (x, 1 - y), (1 - x, 1 - y)]
    return x, y, c, chips


def _remote(src, dst, send_sem, recv_sem, device):
    return pltpu.make_async_remote_copy(src_ref=src, dst_ref=dst, send_sem=send_sem, recv_sem=recv_sem,
                                        device_id=device, device_id_type=MESH)


def _half(ref_rows, cc):
    h = ref_rows // 2
    return pl.ds(cc * h, h)


_HBM = pl.BlockSpec(memory_space=pltpu.HBM)
_SEM = pl.BlockSpec(memory_space=pltpu.SEMAPHORE)
_EFFECT = pltpu.SideEffectType.DATAFLOW_SIDE_EFFECTING


def _in_hbm(arr):
    return pltpu.with_memory_space_constraint(arr, pltpu.HBM)


def _gather_block(land, shard_rows, split, j, cc):
    if split:
        return land.at[j, :, _half(shard_rows, cc), :]
    return land.at[j]


def _gather_copy(src, land, split, send_sem, recv_sem, x, y, c, chip):
    qx, qy = chip
    rows = src.shape[1]
    part = src.at[:, _half(rows, c), :] if split else src
    return _remote(part, _gather_block(land, rows, split, 2 * x + y, c), send_sem, recv_sem, (qx, qy, c))


def _gather_start(name, shards, split):
    n = len(shards)

    def body(*refs):
        srcs, lands, send, recv = refs[:n], refs[n:2 * n], refs[2 * n:3 * n], refs[3 * n:4 * n]
        token = refs[-1]
        x, y, c, chips = _place()
        for a in range(n):
            for qi in range(3):
                _gather_copy(srcs[a], lands[a], split[a], send[a], recv[a], x, y, c, chips[qi]).start()
        token[...] = jnp.zeros_like(token)

    lands = [lax.empty((N_CHIPS, *sh.shape), sh.dtype) for sh in shards]
    outs = pl.pallas_call(
        body, name=name, in_specs=[_HBM] * (2 * n),
        out_specs=[_SEM] * (2 * n) + [_HBM] * (2 * n) + [pl.BlockSpec(memory_space=pltpu.VMEM)],
        out_shape=[pltpu.SemaphoreType.DMA(())] * (2 * n)
        + [pltpu.HBM(sh.shape, sh.dtype) for sh in shards] + [pltpu.HBM(ld.shape, ld.dtype) for ld in lands]
        + [jax.ShapeDtypeStruct((8, LANES), F32)],
        input_output_aliases={i: 2 * n + i for i in range(2 * n)},
        compiler_params=pltpu.CompilerParams(has_side_effects=_EFFECT),
    )(*[_in_hbm(s) for s in shards], *[_in_hbm(ld) for ld in lands])
    return list(outs[:n]), list(outs[n:2 * n]), list(outs[2 * n:3 * n]), list(outs[3 * n:4 * n]), outs[-1]


def _gather_wait(name, split, send, recv, shards, lands, after):
    n = len(shards)

    def body(*refs):
        lds, snd, rcv = refs[n:2 * n], refs[2 * n:3 * n], refs[3 * n:4 * n]
        x, y, c, _ = _place()
        for k in range(n):
            three = lds[k].at[pl.ds(0, 3)]
            if split[k]:
                three = three.at[:, :, _half(lds[k].shape[2], c), :]
            all_three = _remote(three, three, snd[k], rcv[k], (x, y, c))
            all_three.wait_send()
            all_three.wait_recv()

    outs = pl.pallas_call(
        body, name=name, in_specs=[_HBM] * (2 * n) + [_SEM] * (2 * n) + [_ANY], out_specs=[_HBM] * (2 * n),
        out_shape=[pltpu.HBM(s.shape, s.dtype) for s in shards] + [pltpu.HBM(ld.shape, ld.dtype) for ld in lands],
        input_output_aliases={i: i for i in range(2 * n)},
        compiler_params=pltpu.CompilerParams(has_side_effects=_EFFECT),
    )(*shards, *lands, *send, *recv, after)
    return list(outs[:n]), list(outs[n:])


def _gather_finish(name, shards, lands, split):
    n = len(shards)

    def body(*refs):
        srcs, outs = refs[:n], refs[2 * n:3 * n]
        fsend, frecv, lsem = refs[3 * n:]
        x, y, c, chips = _place()
        sib = (x, y, 1 - c)

        def relay(a, qi, cc):
            qx, qy = chips[qi]
            blk = _gather_block(outs[a], srcs[a].shape[1], True, 2 * qx + qy, cc)
            return _remote(blk, blk, fsend.at[a, qi], frecv.at[a, qi], sib)

        local = [pltpu.make_async_copy(srcs[a], outs[a].at[2 * x + y], lsem.at[a]) for a in range(n)]
        for cp in local:
            cp.start()
        relays = [relay(a, qi, c) for a in range(n) if split[a] for qi in range(3)]
        for cp in relays:
            cp.start()
        for a in range(n):
            if split[a]:
                for qi in range(3):
                    relay(a, qi, 1 - c).wait_recv()
        for cp in relays:
            cp.wait_send()
        for cp in local:
            cp.wait()

    outs = pl.pallas_call(
        body, name=name, in_specs=[_ANY] * (2 * n), out_specs=[_ANY] * n,
        out_shape=[jax.ShapeDtypeStruct(ld.shape, ld.dtype) for ld in lands],
        input_output_aliases={n + i: i for i in range(n)},
        scratch_shapes=[pltpu.SemaphoreType.DMA((n, 3))] * 2 + [pltpu.SemaphoreType.DMA((n,))],
    )(*shards, *lands)
    return list(outs)


def _pair_exchange(name, grads):
    n = len(grads)

    def body(*refs):
        ins, outs = refs[:n], refs[n:2 * n]
        send, recv = refs[2 * n:]
        x, y, c, _ = _place()
        copies = [_remote(ins[a].at[:, :, _half(ins[a].shape[2], 1 - c), :], outs[a], send.at[a], recv.at[a],
                          (x, y, 1 - c)) for a in range(n)]
        for cp in copies:
            cp.start()
        for cp in copies:
            cp.wait()

    return pl.pallas_call(
        body, name=name, in_specs=[_ANY] * n, out_specs=[_ANY] * n,
        out_shape=[jax.ShapeDtypeStruct((g.shape[0], g.shape[1], g.shape[2] // 2, g.shape[3]), g.dtype)
                   for g in grads],
        scratch_shapes=[pltpu.SemaphoreType.DMA((n,))] * 2,
    )(*grads)


def _pair_sum(name, grad, other, core):
    nj, nl, r, c = grad.shape
    h = r // 2
    tr = _tile(h, 256)
    nb = h // tr

    def body(core_ref, g_ref, o_ref, q_ref):
        q_ref[...] = (g_ref[...] + o_ref[...]).astype(BF16)

    blk = (None, None, tr, c)
    return pl.pallas_call(
        body, name=name,
        grid_spec=pltpu.PrefetchScalarGridSpec(
            num_scalar_prefetch=1, grid=(nj, nl, nb),
            in_specs=[pl.BlockSpec(blk, lambda j, l, i, core_ref: (j, l, core_ref[0] * nb + i, 0)),
                      pl.BlockSpec(blk, lambda j, l, i, core_ref: (j, l, i, 0))],
            out_specs=pl.BlockSpec(blk, lambda j, l, i, core_ref: (j, l, i, 0))),
        out_shape=jax.ShapeDtypeStruct((nj, nl, h, c), BF16),
        compiler_params=_params("parallel", "parallel", "parallel"),
    )(core, grad, other)


def _scatter(name, sums):
    n = len(sums)

    def body(*refs):
        ins, outs = refs[:n], refs[n:2 * n]
        send, recv, fsend, frecv, lsem = refs[2 * n:]
        x, y, c, chips = _place()
        jme = 2 * x + y
        sib = (x, y, 1 - c)
        origin = [2 * qx + qy for qx, qy in chips] + [jme]

        def block(a, j, cc):
            return outs[a].at[j, :, _half(outs[a].shape[2], cc), :]

        def first(a, qi):
            qx, qy = chips[qi]
            return _remote(ins[a].at[2 * qx + qy], block(a, jme, c), send.at[a, qi], recv.at[a, qi], (qx, qy, c))

        def arrival(a, qi):
            qx, qy = chips[qi]
            blk = block(a, 2 * qx + qy, c)
            return _remote(blk, blk, send.at[a, qi], recv.at[a, qi], (qx, qy, c))

        def relay(a, k, cc, src=None):
            blk = block(a, origin[k], cc)
            return _remote(blk if src is None else src, blk, fsend.at[a, k], frecv.at[a, k], sib)

        local = [pltpu.make_async_copy(ins[a].at[jme], block(a, jme, c), lsem.at[a]) for a in range(n)]
        for cp in local:
            cp.start()
        firsts = [first(a, qi) for a in range(n) for qi in range(3)]
        for cp in firsts:
            cp.start()
        relays = [relay(a, 3, c, src=ins[a].at[jme]) for a in range(n)]
        for cp in relays:
            cp.start()
        for a in range(n):
            for qi in range(3):
                arrival(a, qi).wait_recv()
                relays.append(relay(a, qi, c))
                relays[-1].start()
        for a in range(n):
            for k in range(4):
                relay(a, k, 1 - c).wait_recv()
        for cp in firsts + relays:
            cp.wait_send()
        for cp in local:
            cp.wait()

    return pl.pallas_call(
        body, name=name, in_specs=[_ANY] * n, out_specs=[_ANY] * n,
        out_shape=[jax.ShapeDtypeStruct((q.shape[0], q.shape[1], 2 * q.shape[2], q.shape[3]), q.dtype)
                   for q in sums],
        scratch_shapes=[pltpu.SemaphoreType.DMA((n, 3))] * 2 + [pltpu.SemaphoreType.DMA((n, 4))] * 2
        + [pltpu.SemaphoreType.DMA((n,))],
    )(*sums)


def _allreduce_small(name, packed):
    rows, lanes = packed.shape

    def body(x_ref, o_ref, all_ref, send, recv, lsem):
        x, y, c, chips = _place()
        me, sib = (x, y, c), (x, y, 1 - c)

        def slot(px, py, pc):
            return all_ref.at[pl.ds((4 * px + 2 * py + pc) * rows, rows), :]

        def copy(k, blk, to, src=None):
            return _remote(slot(*blk) if src is None else src, slot(*blk), send.at[k], recv.at[k], to)

        mine = pltpu.make_async_copy(x_ref, slot(*me), lsem)
        mine.start()
        first = [copy(0, me, sib, src=x_ref)] + [copy(1 + j, me, (*chip, c), src=x_ref) for j, chip in enumerate(chips)]
        for cp in first:
            cp.start()
        passed = [copy(4 + j, (*chip, c), sib) for j, chip in enumerate(chips)]
        for j, chip in enumerate(chips):
            copy(1 + j, (*chip, c), me).wait_recv()
            passed[j].start()
        copy(0, sib, me).wait_recv()
        for j, chip in enumerate(chips):
            copy(4 + j, (*chip, 1 - c), me).wait_recv()
        for cp in first + passed:
            cp.wait_send()
        mine.wait()
        total = all_ref[pl.ds(0, rows), :]
        for d in range(1, N_DEV):
            total = total + all_ref[pl.ds(d * rows, rows), :]
        o_ref[...] = total

    vmem = pl.BlockSpec(memory_space=pltpu.VMEM)
    return pl.pallas_call(
        body, name=name, in_specs=[vmem], out_specs=vmem, out_shape=jax.ShapeDtypeStruct(packed.shape, F32),
        scratch_shapes=[pltpu.VMEM((N_DEV * rows, lanes), F32), pltpu.SemaphoreType.DMA((7,)),
                        pltpu.SemaphoreType.DMA((7,)), pltpu.SemaphoreType.DMA],
        compiler_params=pltpu.CompilerParams(vmem_limit_bytes=VMEM_LIMIT),
    )(packed)


def _pack(parts):
    rows = []
    for p in parts:
        flat = p.reshape(-1)
        pad = (-flat.shape[0]) % PACK_ELEMS
        rows.append(jnp.pad(flat, (0, pad)).reshape(-1, LANES))
    return jnp.concatenate(rows, axis=0)


def _unpack(packed, shapes):
    out, row = [], 0
    for sh in shapes:
        size = math.prod(sh)
        nrows = -(-size // PACK_ELEMS) * (PACK_ELEMS // LANES)
        out.append(packed[row:row + nrows].reshape(-1)[:size].reshape(sh))
        row += nrows
    return out


def kernel(x, a_w_in, a_ln_g, a_ln_b, a_w_s, a_b_s, a_w_out, b_w_in, b_w_grp, b_scale, b_w_out, norm_mix, norm_mlp, mlp_w1, mlp_w2, final_norm, loss_target, m_a_w_in, m_a_ln_g, m_a_ln_b, m_a_w_s, m_a_b_s, m_a_w_out, m_b_w_in, m_b_w_grp, m_b_scale, m_b_w_out, m_norm_mix, m_norm_mlp, m_mlp_w1, m_mlp_w2, m_final_norm, v_a_w_in, v_a_ln_g, v_a_ln_b, v_a_w_s, v_a_b_s, v_a_w_out, v_b_w_in, v_b_w_grp, v_b_scale, v_b_w_out, v_norm_mix, v_norm_mlp, v_mlp_w1, v_mlp_w2, v_final_norm):
    xi, yi, ci = lax.axis_index("x"), lax.axis_index("y"), lax.axis_index("c")
    chip = 2 * xi + yi
    x2, tgt = x[0], loss_target[0]
    bw = b_scale.shape[1] * N_CHIPS

    units = dict(a_w_in=a_w_in, a_w_out=a_w_out, w1_0=mlp_w1[0:1], w2_0=mlp_w2[0:1], b_scale=b_scale.reshape(1, 1, -1),
                 b_w_in=b_w_in, b_w_grp=b_w_grp[0], b_w_out=b_w_out, w1_1=mlp_w1[1:2], w2_1=mlp_w2[1:2])
    unit_names = list(units)
    split = [k != "b_scale" for k in unit_names]
    send, recv, shards_t, lands_t, token = _gather_start(
        "gather_start", [units[k].astype(BF16) if s else units[k] for k, s in zip(unit_names, split)], split)
    col_sharded = dict(a_w_in=True, a_w_out=False, b_w_in=False, b_w_grp=False, b_w_out=False,
                       w1_0=True, w2_0=False, w1_1=True, w2_1=False)
    W = {}

    def arrive(keys, after):
        ids = [unit_names.index(k) for k in keys]
        sp = [split[i] for i in ids]
        sh, ld = _gather_wait(f"gather_wait_{keys[0]}", sp, [send[i] for i in ids], [recv[i] for i in ids],
                              [shards_t[i] for i in ids], [lands_t[i] for i in ids], after)
        for k, full in zip(keys, _gather_finish(f"gather_finish_{keys[0]}", sh, ld, sp)):
            W[k] = _W4(full, col_sharded[k]) if k in col_sharded else full

    b_col = a_b_s[0].T

    def residual(acc, res):
        return (res + acc,)

    def sq_relu(acc):
        act = jnp.maximum(acc, 0.0)
        return act, act * act

    def mlp_fwd(tag, h, layer):
        hn = _rms_fwd(f"mlp{tag}_norm", h, norm_mlp[layer:layer + 1])
        arrive([f"w1_{layer}"], hn)
        act, act_sq = _mm_aw(f"mlp{tag}_up", hn, W[f"w1_{layer}"], out_dtypes=(F32, BF16), epilogue=sq_relu)
        arrive([f"w2_{layer}"], act_sq)
        out = _mm_aw(f"mlp{tag}_down", act_sq, W[f"w2_{layer}"], extras=(h,), epilogue=residual)
        return out, (h, hn, act, act_sq)

    hn0 = _rms_fwd("mix_a_norm", x2, norm_mix[0:1])
    arrive(["a_w_in"], token)
    zpre = _mm_aw("mix_a_in", hn0, W["a_w_in"])
    gated = _gate_fwd("mix_a_gate", zpre, a_ln_g, a_ln_b, a_w_s[0], b_col)
    arrive(["a_w_out"], gated)
    h1 = _mm_aw("mix_a_out", gated, W["a_w_out"], extras=(x2,), epilogue=residual)
    h2, mlp0 = mlp_fwd("0", h1, 0)
    hn2 = _rms_fwd("mix_b_norm", h2, norm_mix[1:2])
    arrive(["b_scale", "b_w_in"], hn2)
    scale_full = W["b_scale"].reshape(1, bw)
    vb = _mm_aw("mix_b_in", hn2, W["b_w_in"])
    pooled = _pool_fwd("mix_b_pool", vb)
    arrive(["b_w_grp", "b_w_out"], pooled)
    mixed, ms = _mm_aw("mix_b_grp", pooled, W["b_w_grp"], groups=len(B_WINDOWS), extras=(scale_full,),
                       out_dtypes=(F32, BF16), epilogue=lambda acc, sc: (acc, acc * sc))
    h3 = _mm_aw("mix_b_out", ms, W["b_w_out"], extras=(h2,), epilogue=residual)
    h4, mlp1 = mlp_fwd("1", h3, 1)
    dh4, d_final, loss_part = _final("loss_head", h4, final_norm.reshape(1, -1), tgt)
    g1_like = _W4(None, True, shape=(N_CHIPS, 2, *W["w1_0"].arr.shape[2:]))
    g2_like = _W4(None, False, shape=(N_CHIPS, 2, *W["w2_0"].arr.shape[2:]))

    def mlp_bwd(tag, dh, saved, layer, g_w1, g_w2):
        h, hn, act, act_sq = saved
        dpre = _mm_aw(f"mlp{tag}_down_dx", dh, W[f"w2_{layer}"], transpose_w=True, extras=(act,),
                      out_dtypes=(BF16,), epilogue=lambda acc, a: (acc * (2.0 * a),))
        g_w2 = _mm_dw(f"mlp{tag}_down_dw", act_sq, dh, g2_like, layer=layer, prev=g_w2)
        dhn = _mm_aw(f"mlp{tag}_up_dx", dpre, W[f"w1_{layer}"], transpose_w=True)
        g_w1 = _mm_dw(f"mlp{tag}_up_dw", hn, dpre, g1_like, layer=layer, prev=g_w1)
        dh_in, d_norm = _rms_bwd(f"mlp{tag}_norm_bwd", h, norm_mlp[layer:layer + 1], dhn, dh)
        return dh_in, d_norm, g_w1, g_w2

    dh3, d_norm_mlp1, g_w1, g_w2 = mlp_bwd("1", dh4, mlp1, 1, None, None)
    dms = _mm_aw("mix_b_out_dx", dh3, W["b_w_out"], transpose_w=True)
    g_b_out = _mm_dw("mix_b_out_dw", ms, dh3, W["b_w_out"])
    dmixed, d_scale = _scale_bwd("mix_b_scale_bwd", dms, mixed, scale_full)
    dpooled = _mm_aw("mix_b_grp_dx", dmixed, W["b_w_grp"], groups=len(B_WINDOWS), transpose_w=True)
    g_b_grp = _mm_dw("mix_b_grp_dw", pooled, dmixed, W["b_w_grp"], groups=len(B_WINDOWS))
    dvb = _pool_bwd("mix_b_pool_bwd", dpooled)
    dhn2 = _mm_aw("mix_b_in_dx", dvb, W["b_w_in"], transpose_w=True)
    g_b_in = _mm_dw("mix_b_in_dw", hn2, dvb, W["b_w_in"])
    dh2, d_norm_mix1 = _rms_bwd("mix_b_norm_bwd", h2, norm_mix[1:2], dhn2, dh3)
    dh1, d_norm_mlp0, g_w1, g_w2 = mlp_bwd("0", dh2, mlp0, 0, g_w1, g_w2)
    dgated = _mm_aw("mix_a_out_dx", dh1, W["a_w_out"], transpose_w=True)
    g_a_out = _mm_dw("mix_a_out_dw", gated, dh1, W["a_w_out"])
    dzpre, d_w_s, d_b_col, d_ln_g, d_ln_b = _gate_bwd("mix_a_gate_bwd", zpre, dgated, a_ln_g, a_ln_b, a_w_s[0], b_col)
    dhn0 = _mm_aw("mix_a_in_dx", dzpre, W["a_w_in"], transpose_w=True)
    g_a_in = _mm_dw("mix_a_in_dw", hn0, dzpre, W["a_w_in"])
    dx, d_norm_mix0 = _rms_bwd("mix_a_norm_bwd", x2, norm_mix[0:1], dhn0, dh1)

    grads = dict(a_w_in=g_a_in, a_w_out=g_a_out, b_w_in=g_b_in, b_w_grp=g_b_grp, b_w_out=g_b_out,
                 mlp_w1=g_w1, mlp_w2=g_w2)
    names = list(grads)
    theirs = _pair_exchange("grad_pair_exchange", [grads[k] for k in names])
    core = ci.astype(jnp.int32).reshape(1)
    sums = [_pair_sum(f"grad_pair_sum_{k}", grads[k], theirs[i], core) for i, k in enumerate(names)]
    parts = _scatter("grad_scatter", sums)
    moments = dict(a_w_in=(m_a_w_in, v_a_w_in), a_w_out=(m_a_w_out, v_a_w_out), b_w_in=(m_b_w_in, v_b_w_in),
                   b_w_grp=(m_b_w_grp, v_b_w_grp), b_w_out=(m_b_w_out, v_b_w_out),
                   mlp_w1=(m_mlp_w1, v_mlp_w1), mlp_w2=(m_mlp_w2, v_mlp_w2))
    weights = dict(a_w_in=a_w_in, a_w_out=a_w_out, b_w_in=b_w_in, b_w_grp=b_w_grp, b_w_out=b_w_out,
                   mlp_w1=mlp_w1, mlp_w2=mlp_w2)
    grad_out, delta_out, m_out, v_out = {}, {}, {}, {}
    for i, k in enumerate(names):
        shard_shape = parts[i].shape[1:]
        res = _adam_shard(f"adam_{k}", weights[k].reshape(shard_shape), moments[k][0].reshape(shard_shape),
                          moments[k][1].reshape(shard_shape), parts[i])
        grad_out[k], delta_out[k], m_out[k], v_out[k] = [r.reshape(weights[k].shape) for r in res]

    small = dict(a_ln_g=(a_ln_g, m_a_ln_g, v_a_ln_g), a_ln_b=(a_ln_b, m_a_ln_b, v_a_ln_b),
                 a_w_s=(a_w_s, m_a_w_s, v_a_w_s), a_b_s=(a_b_s, m_a_b_s, v_a_b_s),
                 b_scale=(b_scale, m_b_scale, v_b_scale), norm_mix=(norm_mix, m_norm_mix, v_norm_mix),
                 norm_mlp=(norm_mlp, m_norm_mlp, v_norm_mlp), final_norm=(final_norm, m_final_norm, v_final_norm))
    small_names = list(small)
    local = dict(a_ln_g=d_ln_g, a_ln_b=d_ln_b, a_w_s=d_w_s[None], a_b_s=d_b_col.T[None], b_scale=d_scale,
                 norm_mix=jnp.concatenate([d_norm_mix0, d_norm_mix1], axis=0),
                 norm_mlp=jnp.concatenate([d_norm_mlp0, d_norm_mlp1], axis=0), final_norm=d_final.reshape(-1))
    reduced = _allreduce_small("small_grad_allreduce", _pack([local[k] for k in small_names]))
    small_grads = dict(zip(small_names, _unpack(reduced, [local[k].shape for k in small_names])))
    shard_w = b_scale.shape[1]
    small_grads["b_scale"] = lax.dynamic_slice_in_dim(small_grads["b_scale"], chip * shard_w, shard_w, axis=1)
    small_grads = {k: small_grads[k].reshape(small[k][0].shape) for k in small_names}
    packed = [_pack([small[k][i] for k in small_names]) for i in range(3)]
    res = _adam_packed("adam_small", packed[0], _pack([small_grads[k] for k in small_names]), packed[1], packed[2])
    shapes = [small[k][0].shape for k in small_names]
    for k, d, nm, nv in zip(small_names, *[_unpack(r, shapes) for r in res]):
        grad_out[k], delta_out[k], m_out[k], v_out[k] = small_grads[k], d, nm, nv

    loss = lax.psum(loss_part[0, 0], ("x", "y", "c"))
    order = ["a_w_in", "a_ln_g", "a_ln_b", "a_w_s", "a_b_s", "a_w_out", "b_w_in", "b_w_grp", "b_scale", "b_w_out",
             "norm_mix", "norm_mlp", "mlp_w1", "mlp_w2", "final_norm"]
    return (loss, dx[None], *[grad_out[k] for k in order], *[delta_out[k] for k in order],
            *[m_out[k] for k in order], *[v_out[k] for k in order])
```

```python
import math

import jax
import jax.numpy as jnp
from jax import lax
from jax.experimental import pallas as pl
from jax.experimental.pallas import tpu as pltpu

F32 = jnp.float32
BF16 = jnp.bfloat16
MESH = pl.DeviceIdType.MESH

EPS = 1e-6
B_WINDOWS = (2, 4, 8, 16)
ADAM_LR = 0.001
ADAM_B1 = 0.9
ADAM_B2 = 0.999
ADAM_EPS = 1e-08
ADAM_WD = 0.01
ADAM_STEP = 10

N_CHIPS = 4
N_DEV = 8
LANES = 128
PACK_ELEMS = 8 * LANES
VMEM_LIMIT = 56 * 1024 * 1024
ROW_TILE = 256
MM_TM, MM_TN, MM_TK = 1024, 1024, 512


def _tile(dim, pref):
    t = min(dim, pref)
    while dim % t:
        t //= 2
    return t


def _params(*sem):
    return pltpu.CompilerParams(dimension_semantics=sem, vmem_limit_bytes=VMEM_LIMIT)


class _W4:
    def __init__(self, arr, col_sharded, shape=None):
        self.arr = arr
        self.nj, self.nl, self.r, self.c = arr.shape if shape is None else shape
        self.col = col_sharded
        self.rows = self.r if col_sharded else self.nj * self.r
        self.cols = self.nj * self.c if col_sharded else self.c

    def tile_rows(self, pref):
        return _tile(self.r, pref)

    def tile_cols(self, pref):
        return _tile(self.c, pref)

    def index(self, layer, rb, cb, tr, tc):
        if self.col:
            n = self.c // tc
            return (cb // n, layer, rb, cb % n)
        n = self.r // tr
        return (rb // n, layer, rb % n, cb)


def _mm_aw(name, a, w, *, layer=0, groups=1, transpose_w=False, extras=(), out_dtypes=(F32,), epilogue=None):
    s, ka_total = a.shape
    kdim, ndim = (w.cols, w.rows) if transpose_w else (w.rows, w.cols)
    assert ka_total == groups * kdim, (name, a.shape, kdim, groups)
    tm = _tile(s, MM_TM)
    if transpose_w:
        tn, tk = w.tile_rows(MM_TN), w.tile_cols(MM_TK)
    else:
        tk, tn = w.tile_rows(MM_TK), w.tile_cols(MM_TN)
    nk, nn = kdim // tk, ndim // tn

    def lay(g):
        return g if groups > 1 else layer

    a_spec = pl.BlockSpec((tm, tk), lambda g, i, n, k: (i, g * nk + k))
    if transpose_w:
        w_spec = pl.BlockSpec((None, None, tn, tk), lambda g, i, n, k: w.index(lay(g), n, k, tn, tk))
    else:
        w_spec = pl.BlockSpec((None, None, tk, tn), lambda g, i, n, k: w.index(lay(g), k, n, tk, tn))
    ex_specs = []
    for e in extras:
        assert e.shape[1] == groups * ndim and e.shape[0] in (1, s), (name, e.shape)
        if e.shape[0] == 1:
            ex_specs.append(pl.BlockSpec((1, tn), lambda g, i, n, k: (0, g * nn + n)))
        else:
            ex_specs.append(pl.BlockSpec((tm, tn), lambda g, i, n, k: (i, g * nn + n)))
    out_spec = pl.BlockSpec((tm, tn), lambda g, i, n, k: (i, g * nn + n))
    n_ex, n_out = len(extras), len(out_dtypes)

    def body(a_ref, w_ref, *rest):
        ex, outs, acc = rest[:n_ex], rest[n_ex:n_ex + n_out], rest[-1]
        k = pl.program_id(3)

        @pl.when(k == 0)
        def _():
            acc[...] = jnp.zeros_like(acc)

        av = a_ref[...]
        if av.dtype != BF16:
            av = av.astype(BF16)
        if transpose_w:
            acc[...] += lax.dot_general(av, w_ref[...], (((1,), (1,)), ((), ())), preferred_element_type=F32)
        else:
            acc[...] += jnp.dot(av, w_ref[...], preferred_element_type=F32)

        @pl.when(k == nk - 1)
        def _():
            vals = (acc[...],) if epilogue is None else epilogue(acc[...], *[e[...] for e in ex])
            for o, v in zip(outs, vals):
                o[...] = v.astype(o.dtype)

    outs = pl.pallas_call(
        body, name=name, grid=(groups, s // tm, nn, nk),
        in_specs=[a_spec, w_spec, *ex_specs], out_specs=[out_spec] * n_out,
        out_shape=[jax.ShapeDtypeStruct((s, groups * ndim), dt) for dt in out_dtypes],
        scratch_shapes=[pltpu.VMEM((tm, tn), F32)],
        compiler_params=_params("parallel", "parallel", "parallel", "arbitrary"),
    )(a, w.arr, *extras)
    return outs[0] if n_out == 1 else outs


def _mm_dw(name, a, b, like, *, layer=0, groups=1, prev=None):
    s, ka_total = a.shape
    rows, cols = ka_total // groups, b.shape[1] // groups
    assert (rows, cols) == (like.rows, like.cols) and b.shape[0] == s, (name, a.shape, b.shape)
    tm, tn, tk = like.tile_rows(MM_TM), like.tile_cols(MM_TN), _tile(s, MM_TK)
    nr, nc, nk = rows // tm, cols // tn, s // tk

    def lay(g):
        return g if groups > 1 else layer

    in_specs = [pl.BlockSpec((tk, tm), lambda g, i, n, k: (k, g * nr + i)),
                pl.BlockSpec((tk, tn), lambda g, i, n, k: (k, g * nc + n))]
    operands = [a, b]
    aliases = {}
    if prev is not None:
        in_specs.append(pl.BlockSpec(memory_space=pl.ANY))
        operands.append(prev)
        aliases = {2: 0}

    def body(a_ref, b_ref, *rest):
        o_ref, acc = rest[-2], rest[-1]
        k = pl.program_id(3)

        @pl.when(k == 0)
        def _():
            acc[...] = jnp.zeros_like(acc)

        av, bv = a_ref[...], b_ref[...]
        if av.dtype != BF16:
            av = av.astype(BF16)
        if bv.dtype != BF16:
            bv = bv.astype(BF16)
        acc[...] += lax.dot_general(av, bv, (((0,), (0,)), ((), ())), preferred_element_type=F32)

        @pl.when(k == nk - 1)
        def _():
            o_ref[...] = acc[...]

    return pl.pallas_call(
        body, name=name, grid=(groups, nr, nc, nk), in_specs=in_specs,
        out_specs=pl.BlockSpec((None, None, tm, tn), lambda g, i, n, k: like.index(lay(g), i, n, tm, tn)),
        out_shape=jax.ShapeDtypeStruct((like.nj, like.nl, like.r, like.c), F32),
        scratch_shapes=[pltpu.VMEM((tm, tn), F32)], input_output_aliases=aliases,
        compiler_params=_params("parallel", "parallel", "parallel", "arbitrary"),
    )(*operands)


def _row_spec(tr, d):
    return pl.BlockSpec((tr, d), lambda i: (i, 0))


def _vec_spec(d):
    return pl.BlockSpec((1, d), lambda i: (0, 0))


def _rms_fwd(name, x, g):
    s, d = x.shape
    tr = _tile(s, ROW_TILE)

    def body(x_ref, g_ref, o_ref):
        xv = x_ref[...]
        r = lax.rsqrt(jnp.mean(xv * xv, axis=-1, keepdims=True) + EPS)
        o_ref[...] = (xv * r * g_ref[...]).astype(BF16)

    return pl.pallas_call(
        body, name=name, grid=(s // tr,), in_specs=[_row_spec(tr, d), _vec_spec(d)],
        out_specs=_row_spec(tr, d), out_shape=jax.ShapeDtypeStruct((s, d), BF16),
        compiler_params=_params("parallel"),
    )(x, g)


def _rms_bwd(name, x, g, dhn, dres):
    s, d = x.shape
    tr = _tile(s, ROW_TILE)

    def body(x_ref, g_ref, dhn_ref, dres_ref, dx_ref, dg_ref):
        @pl.when(pl.program_id(0) == 0)
        def _():
            dg_ref[...] = jnp.zeros_like(dg_ref)

        xv = x_ref[...]
        r = lax.rsqrt(jnp.mean(xv * xv, axis=-1, keepdims=True) + EPS)
        xh = xv * r
        dy = dhn_ref[...]
        dg_ref[...] += jnp.sum(dy * xh, axis=0, keepdims=True)
        dxh = dy * g_ref[...]
        dx_ref[...] = dres_ref[...] + r * (dxh - xh * jnp.mean(dxh * xh, axis=-1, keepdims=True))

    return pl.pallas_call(
        body, name=name, grid=(s // tr,),
        in_specs=[_row_spec(tr, d), _vec_spec(d), _row_spec(tr, d), _row_spec(tr, d)],
        out_specs=[_row_spec(tr, d), _vec_spec(d)],
        out_shape=[jax.ShapeDtypeStruct((s, d), F32), jax.ShapeDtypeStruct((1, d), F32)],
        compiler_params=_params("arbitrary"),
    )(x, g, dhn, dres)


def _final(name, h, g, tgt):
    s, d = h.shape
    tr = _tile(s, ROW_TILE)

    def body(h_ref, g_ref, t_ref, dh_ref, dg_ref, loss_ref):
        @pl.when(pl.program_id(0) == 0)
        def _():
            dg_ref[...] = jnp.zeros_like(dg_ref)
            loss_ref[...] = jnp.zeros_like(loss_ref)

        hv = h_ref[...]
        r = lax.rsqrt(jnp.mean(hv * hv, axis=-1, keepdims=True) + EPS)
        xh = hv * r
        gv = g_ref[...]
        err = xh * gv - t_ref[...]
        part = 0.5 * jnp.sum(jnp.mean(err * err, axis=-1, keepdims=True), axis=0, keepdims=True)
        loss_ref[...] += jnp.broadcast_to(part, loss_ref.shape)
        dy = err * (1.0 / d)
        dg_ref[...] += jnp.sum(dy * xh, axis=0, keepdims=True)
        dxh = dy * gv
        dh_ref[...] = r * (dxh - xh * jnp.mean(dxh * xh, axis=-1, keepdims=True))

    return pl.pallas_call(
        body, name=name, grid=(s // tr,),
        in_specs=[_row_spec(tr, d), _vec_spec(d), _row_spec(tr, d)],
        out_specs=[_row_spec(tr, d), _vec_spec(d), _vec_spec(LANES)],
        out_shape=[jax.ShapeDtypeStruct((s, d), F32), jax.ShapeDtypeStruct((1, d), F32),
                   jax.ShapeDtypeStruct((1, LANES), F32)],
        compiler_params=_params("arbitrary"),
    )(h, g, tgt)


_SQRT_HALF = 1.0 / math.sqrt(2.0)
_INV_SQRT_2PI = 1.0 / math.sqrt(2.0 * math.pi)


def _gelu(x):
    return x * (lax.erf(x * _SQRT_HALF) + 1.0) * 0.5


def _gelu_grad(x):
    return 0.5 * (lax.erf(x * _SQRT_HALF) + 1.0) + x * jnp.exp(-0.5 * x * x) * _INV_SQRT_2PI


def _causal(chunk):
    t = lax.broadcasted_iota(jnp.int32, (chunk, chunk), 0)
    sidx = lax.broadcasted_iota(jnp.int32, (chunk, chunk), 1)
    return sidx <= t


def _layernorm_parts(v, g, b):
    mu = jnp.mean(v, axis=-1, keepdims=True)
    vc = v - mu
    rs = lax.rsqrt(jnp.mean(vc * vc, axis=-1, keepdims=True) + EPS)
    vhat = vc * rs
    return vhat, rs, vhat * g + b


def _gate_fwd(name, zpre, ln_g, ln_b, w_s, b_col):
    s, aw2 = zpre.shape
    aw = aw2 // 2
    ng, chunk, _ = w_s.shape
    dh = aw // ng

    def body(z_ref, g_ref, b_ref, ws_ref, bc_ref, o_ref):
        u = _gelu(z_ref[:, :aw])
        v = _gelu(z_ref[:, aw:])
        _, _, vln = _layernorm_parts(v, g_ref[...], b_ref[...])
        mask = _causal(chunk)
        for gi in range(ng):
            sl = slice(gi * dh, (gi + 1) * dh)
            wm = jnp.where(mask, ws_ref[gi], 0.0).astype(BF16)
            sg = jnp.dot(wm, vln[:, sl].astype(BF16), preferred_element_type=F32) + bc_ref[:, gi:gi + 1]
            o_ref[:, sl] = (u[:, sl] * sg).astype(BF16)

    return pl.pallas_call(
        body, name=name, grid=(s // chunk,),
        in_specs=[_row_spec(chunk, aw2), _vec_spec(aw), _vec_spec(aw),
                  pl.BlockSpec((ng, chunk, chunk), lambda i: (0, 0, 0)),
                  pl.BlockSpec((chunk, ng), lambda i: (0, 0))],
        out_specs=_row_spec(chunk, aw), out_shape=jax.ShapeDtypeStruct((s, aw), BF16),
        compiler_params=_params("parallel"),
    )(zpre, ln_g, ln_b, w_s, b_col)


def _gate_bwd(name, zpre, dgated, ln_g, ln_b, w_s, b_col):
    s, aw2 = zpre.shape
    aw = aw2 // 2
    ng, chunk, _ = w_s.shape
    dh = aw // ng

    def body(z_ref, dgt_ref, g_ref, b_ref, ws_ref, bc_ref, dz_ref, dws_ref, dbc_ref, dlg_ref, dlb_ref,
             du_scr, dvln_scr):
        @pl.when(pl.program_id(0) == 0)
        def _():
            dws_ref[...] = jnp.zeros_like(dws_ref)
            dbc_ref[...] = jnp.zeros_like(dbc_ref)
            dlg_ref[...] = jnp.zeros_like(dlg_ref)
            dlb_ref[...] = jnp.zeros_like(dlb_ref)

        zu = z_ref[:, :aw]
        zv = z_ref[:, aw:]
        u = _gelu(zu)
        lg = g_ref[...]
        vhat, rs, vln = _layernorm_parts(_gelu(zv), lg, b_ref[...])
        mask = _causal(chunk)
        for gi in range(ng):
            sl = slice(gi * dh, (gi + 1) * dh)
            wm = jnp.where(mask, ws_ref[gi], 0.0).astype(BF16)
            vg = vln[:, sl].astype(BF16)
            sg = jnp.dot(wm, vg, preferred_element_type=F32) + bc_ref[:, gi:gi + 1]
            dgt = dgt_ref[:, sl]
            du_scr[:, sl] = dgt * sg
            ds = dgt * u[:, sl]
            dbc_ref[:, gi:gi + 1] += jnp.sum(ds, axis=-1, keepdims=True)
            dsb = ds.astype(BF16)
            dwm = lax.dot_general(dsb, vg, (((1,), (1,)), ((), ())), preferred_element_type=F32)
            dws_ref[gi] += jnp.where(mask, dwm, 0.0)
            dvln_scr[:, sl] = lax.dot_general(wm, dsb, (((0,), (0,)), ((), ())), preferred_element_type=F32)
        dvln = dvln_scr[...]
        dlb_ref[...] += jnp.sum(dvln, axis=0, keepdims=True)
        dlg_ref[...] += jnp.sum(dvln * vhat, axis=0, keepdims=True)
        dvh = dvln * lg
        dv = rs * (dvh - jnp.mean(dvh, axis=-1, keepdims=True)
                   - vhat * jnp.mean(dvh * vhat, axis=-1, keepdims=True))
        dz_ref[:, :aw] = (du_scr[...] * _gelu_grad(zu)).astype(BF16)
        dz_ref[:, aw:] = (dv * _gelu_grad(zv)).astype(BF16)

    return pl.pallas_call(
        body, name=name, grid=(s // chunk,),
        in_specs=[_row_spec(chunk, aw2), _row_spec(chunk, aw), _vec_spec(aw), _vec_spec(aw),
                  pl.BlockSpec((ng, chunk, chunk), lambda i: (0, 0, 0)),
                  pl.BlockSpec((chunk, ng), lambda i: (0, 0))],
        out_specs=[_row_spec(chunk, aw2), pl.BlockSpec((ng, chunk, chunk), lambda i: (0, 0, 0)),
                   pl.BlockSpec((chunk, ng), lambda i: (0, 0)), _vec_spec(aw), _vec_spec(aw)],
        out_shape=[jax.ShapeDtypeStruct((s, aw2), BF16), jax.ShapeDtypeStruct((ng, chunk, chunk), F32),
                   jax.ShapeDtypeStruct((chunk, ng), F32), jax.ShapeDtypeStruct((1, aw), F32),
                   jax.ShapeDtypeStruct((1, aw), F32)],
        scratch_shapes=[pltpu.VMEM((chunk, aw), F32), pltpu.VMEM((chunk, aw), F32)],
        compiler_params=_params("arbitrary"),
    )(zpre, dgated, ln_g, ln_b, w_s, b_col)


def _pool_select(g, parts):
    out = parts[-1]
    for gi in range(len(parts) - 2, -1, -1):
        out = jnp.where(g == gi, parts[gi], out)
    return out


def _pool_specs(s, bw):
    head = bw // len(B_WINDOWS)
    tc = _tile(head, 256)
    nb = head // tc
    return tc, (len(B_WINDOWS), nb), pl.BlockSpec((s, tc), lambda g, j: (0, g * nb + j))


def _pool_window(g, t):
    w = _pool_select(g, [jnp.full(t.shape, wi, jnp.int32) for wi in B_WINDOWS])
    return jnp.minimum(t + 1, w).astype(F32)


def _pool_fwd(name, vb):
    assert B_WINDOWS == (2, 4, 8, 16)
    s, bw = vb.shape
    tc, grid, spec = _pool_specs(s, bw)

    def body(v_ref, o_ref):
        g = pl.program_id(0)
        v = v_ref[...]
        t = lax.broadcasted_iota(jnp.int32, (s, tc), 0)

        def down(x, k):
            return jnp.where(t >= k, pltpu.roll(x, k, 0), 0.0)

        sums, cur, k = [], v, 1
        for _ in B_WINDOWS:
            cur = cur + down(cur, k)
            sums.append(cur)
            k *= 2
        o_ref[...] = (_pool_select(g, sums) / _pool_window(g, t) - v).astype(BF16)

    return pl.pallas_call(
        body, name=name, grid=grid, in_specs=[spec], out_specs=spec,
        out_shape=jax.ShapeDtypeStruct((s, bw), BF16), compiler_params=_params("parallel", "parallel"),
    )(vb)


def _pool_bwd(name, dpooled):
    s, bw = dpooled.shape
    tc, grid, spec = _pool_specs(s, bw)

    def body(d_ref, o_ref):
        g = pl.program_id(0)
        dp = d_ref[...]
        t = lax.broadcasted_iota(jnp.int32, (s, tc), 0)

        def up(x, k):
            return jnp.where(t < s - k, pltpu.roll(x, s - k, 0), 0.0)

        sums, cur, k = [], dp / _pool_window(g, t), 1
        for _ in B_WINDOWS:
            cur = cur + up(cur, k)
            sums.append(cur)
            k *= 2
        o_ref[...] = (_pool_select(g, sums) - dp).astype(BF16)

    return pl.pallas_call(
        body, name=name, grid=grid, in_specs=[spec], out_specs=spec,
        out_shape=jax.ShapeDtypeStruct((s, bw), BF16), compiler_params=_params("parallel", "parallel"),
    )(dpooled)


def _scale_bwd(name, dms, mixed, scale):
    s, bw = dms.shape
    tr = _tile(s, ROW_TILE)

    def body(d_ref, m_ref, sc_ref, o_ref, ds_ref):
        @pl.when(pl.program_id(0) == 0)
        def _():
            ds_ref[...] = jnp.zeros_like(ds_ref)

        dv = d_ref[...]
        ds_ref[...] += jnp.sum(dv * m_ref[...], axis=0, keepdims=True)
        o_ref[...] = (dv * sc_ref[...]).astype(BF16)

    return pl.pallas_call(
        body, name=name, grid=(s // tr,), in_specs=[_row_spec(tr, bw), _row_spec(tr, bw), _vec_spec(bw)],
        out_specs=[_row_spec(tr, bw), _vec_spec(bw)],
        out_shape=[jax.ShapeDtypeStruct((s, bw), BF16), jax.ShapeDtypeStruct((1, bw), F32)],
        compiler_params=_params("arbitrary"),
    )(dms, mixed, scale)


def _adam_update(w, g, m, v):
    m = ADAM_B1 * m + (1.0 - ADAM_B1) * g
    v = ADAM_B2 * v + (1.0 - ADAM_B2) * (g * g)
    m_hat = m / (1.0 - ADAM_B1 ** ADAM_STEP)
    v_hat = v / (1.0 - ADAM_B2 ** ADAM_STEP)
    delta = -ADAM_LR * (m_hat / (jnp.sqrt(v_hat) + ADAM_EPS) + ADAM_WD * w)
    return delta, m, v


def _adam_shard(name, w, m, v, parts):
    nl, r, c = w.shape
    nj = parts.shape[0]
    tr = _tile(r, 128)
    spec = pl.BlockSpec((None, tr, c), lambda l, i: (l, i, 0))

    def body(w_ref, m_ref, v_ref, p_ref, g_ref, d_ref, nm_ref, nv_ref):
        g = p_ref[0].astype(F32)
        for j in range(1, nj):
            g = g + p_ref[j].astype(F32)
        delta, nm, nv = _adam_update(w_ref[...], g, m_ref[...], v_ref[...])
        g_ref[...] = g
        d_ref[...] = delta
        nm_ref[...] = nm
        nv_ref[...] = nv

    return pl.pallas_call(
        body, name=name, grid=(nl, r // tr),
        in_specs=[spec, spec, spec, pl.BlockSpec((nj, None, tr, c), lambda l, i: (0, l, i, 0))],
        out_specs=[spec] * 4, out_shape=[jax.ShapeDtypeStruct(w.shape, F32)] * 4,
        compiler_params=_params("parallel", "parallel"),
    )(w, m, v, parts)


def _adam_packed(name, w, g, m, v):
    rows, lanes = w.shape
    tr = _tile(rows, 512)
    spec = pl.BlockSpec((tr, lanes), lambda i: (i, 0))

    def body(w_ref, g_ref, m_ref, v_ref, d_ref, nm_ref, nv_ref):
        delta, nm, nv = _adam_update(w_ref[...], g_ref[...], m_ref[...], v_ref[...])
        d_ref[...] = delta
        nm_ref[...] = nm
        nv_ref[...] = nv

    return pl.pallas_call(
        body, name=name, grid=(rows // tr,), in_specs=[spec] * 4, out_specs=[spec] * 3,
        out_shape=[jax.ShapeDtypeStruct(w.shape, F32)] * 3, compiler_params=_params("parallel"),
    )(w, g, m, v)


_ANY = pl.BlockSpec(memory_space=pl.ANY)


def _place():
    x, y, c = lax.axis_index("x"), lax.axis_index("y"), lax.axis_index("c")
    chips = [(1 - x, y), (x, 1 - y), (1 - x, 1 - y)]
    return x, y, c, chips


def _remote(src, dst, send_sem, recv_sem, device):
    return pltpu.make_async_remote_copy(src_ref=src, dst_ref=dst, send_sem=send_sem, recv_sem=recv_sem,
                                        device_id=device, device_id_type=MESH)


def _half(ref_rows, cc):
    h = ref_rows // 2
    return pl.ds(cc * h, h)


_HBM = pl.BlockSpec(memory_space=pltpu.HBM)
_SEM = pl.BlockSpec(memory_space=pltpu.SEMAPHORE)
_EFFECT = pltpu.SideEffectType.DATAFLOW_SIDE_EFFECTING


def _in_hbm(arr):
    return pltpu.with_memory_space_constraint(arr, pltpu.HBM)


def _gather_block(land, shard_rows, split, j, cc):
    if split:
        return land.at[j, :, _half(shard_rows, cc), :]
    return land.at[j]


def _gather_copy(src, land, split, send_sem, recv_sem, x, y, c, chip):
    qx, qy = chip
    rows = src.shape[1]
    part = src.at[:, _half(rows, c), :] if split else src
    return _remote(part, _gather_block(land, rows, split, 2 * x + y, c), send_sem, recv_sem, (qx, qy, c))


def _gather_start(name, shards, split):
    n = len(shards)

    def body(*refs):
        srcs, lands, send, recv = refs[:n], refs[n:2 * n], refs[2 * n:3 * n], refs[3 * n:4 * n]
        token = refs[-1]
        x, y, c, chips = _place()
        for a in range(n):
            for qi in range(3):
                _gather_copy(srcs[a], lands[a], split[a], send[a], recv[a], x, y, c, chips[qi]).start()
        token[...] = jnp.zeros_like(token)

    lands = [lax.empty((N_CHIPS, *sh.shape), sh.dtype) for sh in shards]
    outs = pl.pallas_call(
        body, name=name, in_specs=[_HBM] * (2 * n),
        out_specs=[_SEM] * (2 * n) + [_HBM] * (2 * n) + [pl.BlockSpec(memory_space=pltpu.VMEM)],
        out_shape=[pltpu.SemaphoreType.DMA(())] * (2 * n)
        + [pltpu.HBM(sh.shape, sh.dtype) for sh in shards] + [pltpu.HBM(ld.shape, ld.dtype) for ld in lands]
        + [jax.ShapeDtypeStruct((8, LANES), F32)],
        input_output_aliases={i: 2 * n + i for i in range(2 * n)},
        compiler_params=pltpu.CompilerParams(has_side_effects=_EFFECT),
    )(*[_in_hbm(s) for s in shards], *[_in_hbm(ld) for ld in lands])
    return list(outs[:n]), list(outs[n:2 * n]), list(outs[2 * n:3 * n]), list(outs[3 * n:4 * n]), outs[-1]


def _gather_wait(name, split, send, recv, shards, lands, after):
    n = len(shards)

    def body(*refs):
        lds, snd, rcv = refs[n:2 * n], refs[2 * n:3 * n], refs[3 * n:4 * n]
        x, y, c, _ = _place()
        for k in range(n):
            three = lds[k].at[pl.ds(0, 3)]
            if split[k]:
                three = three.at[:, :, _half(lds[k].shape[2], c), :]
            all_three = _remote(three, three, snd[k], rcv[k], (x, y, c))
            all_three.wait_send()
            all_three.wait_recv()

    outs = pl.pallas_call(
        body, name=name, in_specs=[_HBM] * (2 * n) + [_SEM] * (2 * n) + [_ANY], out_specs=[_HBM] * (2 * n),
        out_shape=[pltpu.HBM(s.shape, s.dtype) for s in shards] + [pltpu.HBM(ld.shape, ld.dtype) for ld in lands],
        input_output_aliases={i: i for i in range(2 * n)},
        compiler_params=pltpu.CompilerParams(has_side_effects=_EFFECT),
    )(*shards, *lands, *send, *recv, after)
    return list(outs[:n]), list(outs[n:])


def _fill_own(name, src, dest, dest_shape, place, src_slot=False, half_rows=False):
    nl, r, c = src.shape[-3:]
    tr = _tile(r, 512)
    nb = r // tr

    def src_map(l, i, p):
        return (p[0], l, i, 0) if src_slot else (l, i, 0)

    def dst_map(l, i, p):
        return (p[0], l, (p[1] * nb if half_rows else 0) + i, 0)

    def body(p_ref, s_ref, *rest):
        rest[-1][...] = s_ref[...]

    in_specs = [pl.BlockSpec((None, None, tr, c) if src_slot else (None, tr, c), src_map)]
    operands, aliases = [src], {}
    if dest is not None:
        in_specs.append(_ANY)
        operands.append(dest)
        aliases = {2: 0}
    return pl.pallas_call(
        body, name=name,
        grid_spec=pltpu.PrefetchScalarGridSpec(
            num_scalar_prefetch=1, grid=(nl, nb), in_specs=in_specs,
            out_specs=pl.BlockSpec((None, None, tr, c), dst_map)),
        out_shape=jax.ShapeDtypeStruct(dest_shape, src.dtype), input_output_aliases=aliases,
        compiler_params=_params("parallel", "parallel"),
    )(place, *operands)


def _gather_finish(name, lands):
    n = len(lands)

    def body(*refs):
        outs = refs[n:2 * n]
        fsend, frecv = refs[2 * n:]
        x, y, c, chips = _place()
        sib = (x, y, 1 - c)

        def relay(a, qi, cc):
            qx, qy = chips[qi]
            blk = _gather_block(outs[a], outs[a].shape[2], True, 2 * qx + qy, cc)
            return _remote(blk, blk, fsend.at[a, qi], frecv.at[a, qi], sib)

        relays = [relay(a, qi, c) for a in range(n) for qi in range(3)]
        for cp in relays:
            cp.start()
        for a in range(n):
            for qi in range(3):
                relay(a, qi, 1 - c).wait_recv()
        for cp in relays:
            cp.wait_send()

    outs = pl.pallas_call(
        body, name=name, in_specs=[_ANY] * n, out_specs=[_ANY] * n,
        out_shape=[jax.ShapeDtypeStruct(ld.shape, ld.dtype) for ld in lands],
        input_output_aliases={i: i for i in range(n)},
        scratch_shapes=[pltpu.SemaphoreType.DMA((n, 3))] * 2,
    )(*lands)
    return list(outs)


def _pair_exchange(name, grads):
    n = len(grads)

    def body(*refs):
        ins, outs = refs[:n], refs[n:2 * n]
        send, recv = refs[2 * n:]
        x, y, c, _ = _place()
        copies = [_remote(ins[a].at[:, :, _half(ins[a].shape[2], 1 - c), :], outs[a], send.at[a], recv.at[a],
                          (x, y, 1 - c)) for a in range(n)]
        for cp in copies:
            cp.start()
        for cp in copies:
            cp.wait()

    return pl.pallas_call(
        body, name=name, in_specs=[_ANY] * n, out_specs=[_ANY] * n,
        out_shape=[jax.ShapeDtypeStruct((g.shape[0], g.shape[1], g.shape[2] // 2, g.shape[3]), g.dtype)
                   for g in grads],
        scratch_shapes=[pltpu.SemaphoreType.DMA((n,))] * 2,
    )(*grads)


def _pair_sum(name, grad, other, core):
    nj, nl, r, c = grad.shape
    h = r // 2
    tr = _tile(h, 256)
    nb = h // tr

    def body(core_ref, g_ref, o_ref, q_ref):
        q_ref[...] = (g_ref[...] + o_ref[...]).astype(BF16)

    blk = (None, None, tr, c)
    return pl.pallas_call(
        body, name=name,
        grid_spec=pltpu.PrefetchScalarGridSpec(
            num_scalar_prefetch=1, grid=(nj, nl, nb),
            in_specs=[pl.BlockSpec(blk, lambda j, l, i, core_ref: (j, l, core_ref[0] * nb + i, 0)),
                      pl.BlockSpec(blk, lambda j, l, i, core_ref: (j, l, i, 0))],
            out_specs=pl.BlockSpec(blk, lambda j, l, i, core_ref: (j, l, i, 0))),
        out_shape=jax.ShapeDtypeStruct((nj, nl, h, c), BF16),
        compiler_params=_params("parallel", "parallel", "parallel"),
    )(core, grad, other)


def _scatter(name, sums, dests):
    n = len(sums)

    def body(*refs):
        ins, outs = refs[:n], refs[2 * n:3 * n]
        send, recv, fsend, frecv = refs[3 * n:]
        x, y, c, chips = _place()
        jme = 2 * x + y
        sib = (x, y, 1 - c)
        origin = [2 * qx + qy for qx, qy in chips] + [jme]

        def block(a, j, cc):
            return outs[a].at[j, :, _half(outs[a].shape[2], cc), :]

        def first(a, qi):
            qx, qy = chips[qi]
            return _remote(ins[a].at[2 * qx + qy], block(a, jme, c), send.at[a, qi], recv.at[a, qi], (qx, qy, c))

        def arrival(a, qi):
            qx, qy = chips[qi]
            blk = block(a, 2 * qx + qy, c)
            return _remote(blk, blk, send.at[a, qi], recv.at[a, qi], (qx, qy, c))

        def relay(a, k, cc, src=None):
            blk = block(a, origin[k], cc)
            return _remote(blk if src is None else src, blk, fsend.at[a, k], frecv.at[a, k], sib)

        firsts = [first(a, qi) for a in range(n) for qi in range(3)]
        for cp in firsts:
            cp.start()
        relays = [relay(a, 3, c, src=ins[a].at[jme]) for a in range(n)]
        for cp in relays:
            cp.start()
        for a in range(n):
            for qi in range(3):
                arrival(a, qi).wait_recv()
                relays.append(relay(a, qi, c))
                relays[-1].start()
        for a in range(n):
            for k in range(4):
                relay(a, k, 1 - c).wait_recv()
        for cp in firsts + relays:
            cp.wait_send()

    return pl.pallas_call(
        body, name=name, in_specs=[_ANY] * (2 * n), out_specs=[_ANY] * n,
        out_shape=[jax.ShapeDtypeStruct(d.shape, d.dtype) for d in dests],
        input_output_aliases={n + i: i for i in range(n)},
        scratch_shapes=[pltpu.SemaphoreType.DMA((n, 3))] * 2 + [pltpu.SemaphoreType.DMA((n, 4))] * 2,
    )(*sums, *dests)


def _allreduce_small(name, packed):
    rows, lanes = packed.shape

    def body(x_ref, o_ref, all_ref, send, recv, lsem):
        x, y, c, chips = _place()
        me, sib = (x, y, c), (x, y, 1 - c)

        def slot(px, py, pc):
            return all_ref.at[pl.ds((4 * px + 2 * py + pc) * rows, rows), :]

        def copy(k, blk, to, src=None):
            return _remote(slot(*blk) if src is None else src, slot(*blk), send.at[k], recv.at[k], to)

        mine = pltpu.make_async_copy(x_ref, slot(*me), lsem)
        mine.start()
        first = [copy(0, me, sib, src=x_ref)] + [copy(1 + j, me, (*chip, c), src=x_ref) for j, chip in enumerate(chips)]
        for cp in first:
            cp.start()
        passed = [copy(4 + j, (*chip, c), sib) for j, chip in enumerate(chips)]
        for j, chip in enumerate(chips):
            copy(1 + j, (*chip, c), me).wait_recv()
            passed[j].start()
        copy(0, sib, me).wait_recv()
        for j, chip in enumerate(chips):
            copy(4 + j, (*chip, 1 - c), me).wait_recv()
        for cp in first + passed:
            cp.wait_send()
        mine.wait()
        total = all_ref[pl.ds(0, rows), :]
        for d in range(1, N_DEV):
            total = total + all_ref[pl.ds(d * rows, rows), :]
        o_ref[...] = total

    vmem = pl.BlockSpec(memory_space=pltpu.VMEM)
    return pl.pallas_call(
        body, name=name, in_specs=[vmem], out_specs=vmem, out_shape=jax.ShapeDtypeStruct(packed.shape, F32),
        scratch_shapes=[pltpu.VMEM((N_DEV * rows, lanes), F32), pltpu.SemaphoreType.DMA((7,)),
                        pltpu.SemaphoreType.DMA((7,)), pltpu.SemaphoreType.DMA],
        compiler_params=pltpu.CompilerParams(vmem_limit_bytes=VMEM_LIMIT),
    )(packed)


def _pack(parts):
    rows = []
    for p in parts:
        flat = p.reshape(-1)
        pad = (-flat.shape[0]) % PACK_ELEMS
        rows.append(jnp.pad(flat, (0, pad)).reshape(-1, LANES))
    return jnp.concatenate(rows, axis=0)


def _unpack(packed, shapes):
    out, row = [], 0
    for sh in shapes:
        size = math.prod(sh)
        nrows = -(-size // PACK_ELEMS) * (PACK_ELEMS // LANES)
        out.append(packed[row:row + nrows].reshape(-1)[:size].reshape(sh))
        row += nrows
    return out


def kernel(x, a_w_in, a_ln_g, a_ln_b, a_w_s, a_b_s, a_w_out, b_w_in, b_w_grp, b_scale, b_w_out, norm_mix, norm_mlp, mlp_w1, mlp_w2, final_norm, loss_target, m_a_w_in, m_a_ln_g, m_a_ln_b, m_a_w_s, m_a_b_s, m_a_w_out, m_b_w_in, m_b_w_grp, m_b_scale, m_b_w_out, m_norm_mix, m_norm_mlp, m_mlp_w1, m_mlp_w2, m_final_norm, v_a_w_in, v_a_ln_g, v_a_ln_b, v_a_w_s, v_a_b_s, v_a_w_out, v_b_w_in, v_b_w_grp, v_b_scale, v_b_w_out, v_norm_mix, v_norm_mlp, v_mlp_w1, v_mlp_w2, v_final_norm):
    xi, yi, ci = lax.axis_index("x"), lax.axis_index("y"), lax.axis_index("c")
    chip = 2 * xi + yi
    place = jnp.stack([chip, ci]).astype(jnp.int32)
    x2, tgt = x[0], loss_target[0]
    bw = b_scale.shape[1] * N_CHIPS

    units = dict(a_w_in=a_w_in, a_w_out=a_w_out, w1_0=mlp_w1[0:1], w2_0=mlp_w2[0:1], b_scale=b_scale.reshape(1, 1, -1),
                 b_w_in=b_w_in, b_w_grp=b_w_grp[0], b_w_out=b_w_out, w1_1=mlp_w1[1:2], w2_1=mlp_w2[1:2])
    unit_names = list(units)
    split = [k != "b_scale" for k in unit_names]
    send, recv, shards_t, lands_t, token = _gather_start(
        "gather_start", [units[k].astype(BF16) if s else units[k] for k, s in zip(unit_names, split)], split)
    col_sharded = dict(a_w_in=True, a_w_out=False, b_w_in=False, b_w_grp=False, b_w_out=False,
                       w1_0=True, w2_0=False, w1_1=True, w2_1=False)
    W = {}

    def arrive(keys, after):
        ids = [unit_names.index(k) for k in keys]
        sp = [split[i] for i in ids]
        sh, ld = _gather_wait(f"gather_wait_{keys[0]}", sp, [send[i] for i in ids], [recv[i] for i in ids],
                              [shards_t[i] for i in ids], [lands_t[i] for i in ids], after)
        ld = [_fill_own(f"gather_own_{k}", s, l, l.shape, place) for k, s, l in zip(keys, sh, ld)]
        relayed = iter(_gather_finish(f"gather_finish_{keys[0]}", [l for l, s in zip(ld, sp) if s]))
        for k, l, s in zip(keys, ld, sp):
            full = next(relayed) if s else l
            W[k] = _W4(full, col_sharded[k]) if k in col_sharded else full

    b_col = a_b_s[0].T

    def residual(acc, res):
        return (res + acc,)

    def sq_relu(acc):
        act = jnp.maximum(acc, 0.0)
        return act, act * act

    def mlp_fwd(tag, h, layer):
        hn = _rms_fwd(f"mlp{tag}_norm", h, norm_mlp[layer:layer + 1])
        arrive([f"w1_{layer}"], hn)
        act, act_sq = _mm_aw(f"mlp{tag}_up", hn, W[f"w1_{layer}"], out_dtypes=(F32, BF16), epilogue=sq_relu)
        arrive([f"w2_{layer}"], act_sq)
        out = _mm_aw(f"mlp{tag}_down", act_sq, W[f"w2_{layer}"], extras=(h,), epilogue=residual)
        return out, (h, hn, act, act_sq)

    hn0 = _rms_fwd("mix_a_norm", x2, norm_mix[0:1])
    arrive(["a_w_in"], token)
    zpre = _mm_aw("mix_a_in", hn0, W["a_w_in"])
    gated = _gate_fwd("mix_a_gate", zpre, a_ln_g, a_ln_b, a_w_s[0], b_col)
    arrive(["a_w_out"], gated)
    h1 = _mm_aw("mix_a_out", gated, W["a_w_out"], extras=(x2,), epilogue=residual)
    h2, mlp0 = mlp_fwd("0", h1, 0)
    hn2 = _rms_fwd("mix_b_norm", h2, norm_mix[1:2])
    arrive(["b_scale", "b_w_in"], hn2)
    scale_full = W["b_scale"].reshape(1, bw)
    vb = _mm_aw("mix_b_in", hn2, W["b_w_in"])
    pooled = _pool_fwd("mix_b_pool", vb)
    arrive(["b_w_grp", "b_w_out"], pooled)
    mixed, ms = _mm_aw("mix_b_grp", pooled, W["b_w_grp"], groups=len(B_WINDOWS), extras=(scale_full,),
                       out_dtypes=(F32, BF16), epilogue=lambda acc, sc: (acc, acc * sc))
    h3 = _mm_aw("mix_b_out", ms, W["b_w_out"], extras=(h2,), epilogue=residual)
    h4, mlp1 = mlp_fwd("1", h3, 1)
    dh4, d_final, loss_part = _final("loss_head", h4, final_norm.reshape(1, -1), tgt)
    g1_like = _W4(None, True, shape=(N_CHIPS, 2, *W["w1_0"].arr.shape[2:]))
    g2_like = _W4(None, False, shape=(N_CHIPS, 2, *W["w2_0"].arr.shape[2:]))

    def mlp_bwd(tag, dh, saved, layer, g_w1, g_w2):
        h, hn, act, act_sq = saved
        dpre = _mm_aw(f"mlp{tag}_down_dx", dh, W[f"w2_{layer}"], transpose_w=True, extras=(act,),
                      out_dtypes=(BF16,), epilogue=lambda acc, a: (acc * (2.0 * a),))
        g_w2 = _mm_dw(f"mlp{tag}_down_dw", act_sq, dh, g2_like, layer=layer, prev=g_w2)
        dhn = _mm_aw(f"mlp{tag}_up_dx", dpre, W[f"w1_{layer}"], transpose_w=True)
        g_w1 = _mm_dw(f"mlp{tag}_up_dw", hn, dpre, g1_like, layer=layer, prev=g_w1)
        dh_in, d_norm = _rms_bwd(f"mlp{tag}_norm_bwd", h, norm_mlp[layer:layer + 1], dhn, dh)
        return dh_in, d_norm, g_w1, g_w2

    dh3, d_norm_mlp1, g_w1, g_w2 = mlp_bwd("1", dh4, mlp1, 1, None, None)
    dms = _mm_aw("mix_b_out_dx", dh3, W["b_w_out"], transpose_w=True)
    g_b_out = _mm_dw("mix_b_out_dw", ms, dh3, W["b_w_out"])
    dmixed, d_scale = _scale_bwd("mix_b_scale_bwd", dms, mixed, scale_full)
    dpooled = _mm_aw("mix_b_grp_dx", dmixed, W["b_w_grp"], groups=len(B_WINDOWS), transpose_w=True)
    g_b_grp = _mm_dw("mix_b_grp_dw", pooled, dmixed, W["b_w_grp"], groups=len(B_WINDOWS))
    dvb = _pool_bwd("mix_b_pool_bwd", dpooled)
    dhn2 = _mm_aw("mix_b_in_dx", dvb, W["b_w_in"], transpose_w=True)
    g_b_in = _mm_dw("mix_b_in_dw", hn2, dvb, W["b_w_in"])
    dh2, d_norm_mix1 = _rms_bwd("mix_b_norm_bwd", h2, norm_mix[1:2], dhn2, dh3)
    dh1, d_norm_mlp0, g_w1, g_w2 = mlp_bwd("0", dh2, mlp0, 0, g_w1, g_w2)
    dgated = _mm_aw("mix_a_out_dx", dh1, W["a_w_out"], transpose_w=True)
    g_a_out = _mm_dw("mix_a_out_dw", gated, dh1, W["a_w_out"])
    dzpre, d_w_s, d_b_col, d_ln_g, d_ln_b = _gate_bwd("mix_a_gate_bwd", zpre, dgated, a_ln_g, a_ln_b, a_w_s[0], b_col)
    dhn0 = _mm_aw("mix_a_in_dx", dzpre, W["a_w_in"], transpose_w=True)
    g_a_in = _mm_dw("mix_a_in_dw", hn0, dzpre, W["a_w_in"])
    dx, d_norm_mix0 = _rms_bwd("mix_a_norm_bwd", x2, norm_mix[0:1], dhn0, dh1)

    grads = dict(a_w_in=g_a_in, a_w_out=g_a_out, b_w_in=g_b_in, b_w_grp=g_b_grp, b_w_out=g_b_out,
                 mlp_w1=g_w1, mlp_w2=g_w2)
    names = list(grads)
    theirs = _pair_exchange("grad_pair_exchange", [grads[k] for k in names])
    core = ci.astype(jnp.int32).reshape(1)
    sums = [_pair_sum(f"grad_pair_sum_{k}", grads[k], theirs[i], core) for i, k in enumerate(names)]
    dests = [_fill_own(f"grad_own_{k}", q, None, (q.shape[0], q.shape[1], 2 * q.shape[2], q.shape[3]), place,
                       src_slot=True, half_rows=True) for k, q in zip(names, sums)]
    parts = _scatter("grad_scatter", sums, dests)
    moments = dict(a_w_in=(m_a_w_in, v_a_w_in), a_w_out=(m_a_w_out, v_a_w_out), b_w_in=(m_b_w_in, v_b_w_in),
                   b_w_grp=(m_b_w_grp, v_b_w_grp), b_w_out=(m_b_w_out, v_b_w_out),
                   mlp_w1=(m_mlp_w1, v_mlp_w1), mlp_w2=(m_mlp_w2, v_mlp_w2))
    weights = dict(a_w_in=a_w_in, a_w_out=a_w_out, b_w_in=b_w_in, b_w_grp=b_w_grp, b_w_out=b_w_out,
                   mlp_w1=mlp_w1, mlp_w2=mlp_w2)
    grad_out, delta_out, m_out, v_out = {}, {}, {}, {}
    for i, k in enumerate(names):
        shard_shape = parts[i].shape[1:]
        res = _adam_shard(f"adam_{k}", weights[k].reshape(shard_shape), moments[k][0].reshape(shard_shape),
                          moments[k][1].reshape(shard_shape), parts[i])
        grad_out[k], delta_out[k], m_out[k], v_out[k] = [r.reshape(weights[k].shape) for r in res]

    small = dict(a_ln_g=(a_ln_g, m_a_ln_g, v_a_ln_g), a_ln_b=(a_ln_b, m_a_ln_b, v_a_ln_b),
                 a_w_s=(a_w_s, m_a_w_s, v_a_w_s), a_b_s=(a_b_s, m_a_b_s, v_a_b_s),
                 b_scale=(b_scale, m_b_scale, v_b_scale), norm_mix=(norm_mix, m_norm_mix, v_norm_mix),
                 norm_mlp=(norm_mlp, m_norm_mlp, v_norm_mlp), final_norm=(final_norm, m_final_norm, v_final_norm))
    small_names = list(small)
    local = dict(a_ln_g=d_ln_g, a_ln_b=d_ln_b, a_w_s=d_w_s[None], a_b_s=d_b_col.T[None], b_scale=d_scale,
                 norm_mix=jnp.concatenate([d_norm_mix0, d_norm_mix1], axis=0),
                 norm_mlp=jnp.concatenate([d_norm_mlp0, d_norm_mlp1], axis=0), final_norm=d_final.reshape(-1))
    reduced = _allreduce_small("small_grad_allreduce", _pack([local[k] for k in small_names]))
    small_grads = dict(zip(small_names, _unpack(reduced, [local[k].shape for k in small_names])))
    shard_w = b_scale.shape[1]
    small_grads["b_scale"] = lax.dynamic_slice_in_dim(small_grads["b_scale"], chip * shard_w, shard_w, axis=1)
    small_grads = {k: small_grads[k].reshape(small[k][0].shape) for k in small_names}
    packed = [_pack([small[k][i] for k in small_names]) for i in range(3)]
    res = _adam_packed("adam_small", packed[0], _pack([small_grads[k] for k in small_names]), packed[1], packed[2])
    shapes = [small[k][0].shape for k in small_names]
    for k, d, nm, nv in zip(small_names, *[_unpack(r, shapes) for r in res]):
        grad_out[k], delta_out[k], m_out[k], v_out[k] = small_grads[k], d, nm, nv

    loss = lax.psum(loss_part[0, 0], ("x", "y", "c"))
    order = ["a_w_in", "a_ln_g", "a_ln_b", "a_w_s", "a_b_s", "a_w_out", "b_w_in", "b_w_grp", "b_scale", "b_w_out",
             "norm_mix", "norm_mlp", "mlp_w1", "mlp_w2", "final_norm"]
    return (loss, dx[None], *[grad_out[k] for k in order], *[delta_out[k] for k in order],
            *[m_out[k] for k in order], *[v_out[k] for k in order])
```

```python
import math

import jax
import jax.numpy as jnp
from jax import lax
from jax.experimental import pallas as pl
from jax.experimental.pallas import tpu as pltpu

F32 = jnp.float32
BF16 = jnp.bfloat16
MESH = pl.DeviceIdType.MESH

EPS = 1e-6
B_WINDOWS = (2, 4, 8, 16)
ADAM_LR = 0.001
ADAM_B1 = 0.9
ADAM_B2 = 0.999
ADAM_EPS = 1e-08
ADAM_WD = 0.01
ADAM_STEP = 10

N_CHIPS = 4
N_DEV = 8
LANES = 128
PACK_ELEMS = 8 * LANES
VMEM_LIMIT = 56 * 1024 * 1024
ROW_TILE = 256
MM_TM, MM_TN, MM_TK = 1024, 1024, 512


_ANY = pl.BlockSpec(memory_space=pl.ANY)


def _tile(dim, pref):
    t = min(dim, pref)
    while dim % t:
        t //= 2
    return t


def _params(*sem):
    return pltpu.CompilerParams(dimension_semantics=sem, vmem_limit_bytes=VMEM_LIMIT)


class _W4:
    def __init__(self, arr, col_sharded, shape=None):
        self.arr = arr
        self.nj, self.nl, self.r, self.c = arr.shape if shape is None else shape
        self.col = col_sharded
        self.rows = self.r if col_sharded else self.nj * self.r
        self.cols = self.nj * self.c if col_sharded else self.c

    def tile_rows(self, pref):
        return _tile(self.r, pref)

    def tile_cols(self, pref):
        return _tile(self.c, pref)

    def index(self, layer, rb, cb, tr, tc):
        if self.col:
            n = self.c // tc
            return (cb // n, layer, rb, cb % n)
        n = self.r // tr
        return (rb // n, layer, rb % n, cb)


def _mm_aw(name, a, w, *, layer=0, groups=1, transpose_w=False, extras=(), out_dtypes=(F32,), epilogue=None,
           deps=()):
    s, ka_total = a.shape
    kdim, ndim = (w.cols, w.rows) if transpose_w else (w.rows, w.cols)
    assert ka_total == groups * kdim, (name, a.shape, kdim, groups)
    tm = _tile(s, MM_TM)
    if transpose_w:
        tn, tk = w.tile_rows(MM_TN), w.tile_cols(MM_TK)
    else:
        tk, tn = w.tile_rows(MM_TK), w.tile_cols(MM_TN)
    nk, nn = kdim // tk, ndim // tn

    def lay(g):
        return g if groups > 1 else layer

    a_spec = pl.BlockSpec((tm, tk), lambda g, i, n, k: (i, g * nk + k))
    if transpose_w:
        w_spec = pl.BlockSpec((None, None, tn, tk), lambda g, i, n, k: w.index(lay(g), n, k, tn, tk))
    else:
        w_spec = pl.BlockSpec((None, None, tk, tn), lambda g, i, n, k: w.index(lay(g), k, n, tk, tn))
    ex_specs = []
    for e in extras:
        assert e.shape[1] == groups * ndim and e.shape[0] in (1, s), (name, e.shape)
        if e.shape[0] == 1:
            ex_specs.append(pl.BlockSpec((1, tn), lambda g, i, n, k: (0, g * nn + n)))
        else:
            ex_specs.append(pl.BlockSpec((tm, tn), lambda g, i, n, k: (i, g * nn + n)))
    out_spec = pl.BlockSpec((tm, tn), lambda g, i, n, k: (i, g * nn + n))
    n_ex, n_out, n_dep = len(extras), len(out_dtypes), len(deps)

    def body(a_ref, w_ref, *rest):
        ex, outs, acc = rest[:n_ex], rest[n_ex + n_dep:n_ex + n_dep + n_out], rest[-1]
        k = pl.program_id(3)

        @pl.when(k == 0)
        def _():
            acc[...] = jnp.zeros_like(acc)

        av = a_ref[...]
        if av.dtype != BF16:
            av = av.astype(BF16)
        if transpose_w:
            acc[...] += lax.dot_general(av, w_ref[...], (((1,), (1,)), ((), ())), preferred_element_type=F32)
        else:
            acc[...] += jnp.dot(av, w_ref[...], preferred_element_type=F32)

        @pl.when(k == nk - 1)
        def _():
            vals = (acc[...],) if epilogue is None else epilogue(acc[...], *[e[...] for e in ex])
            for o, v in zip(outs, vals):
                o[...] = v.astype(o.dtype)

    outs = pl.pallas_call(
        body, name=name, grid=(groups, s // tm, nn, nk),
        in_specs=[a_spec, w_spec, *ex_specs] + [_ANY] * n_dep, out_specs=[out_spec] * n_out,
        out_shape=[jax.ShapeDtypeStruct((s, groups * ndim), dt) for dt in out_dtypes],
        scratch_shapes=[pltpu.VMEM((tm, tn), F32)],
        compiler_params=_params("parallel", "parallel", "parallel", "arbitrary"),
    )(a, w.arr, *extras, *deps)
    return outs[0] if n_out == 1 else outs


def _mm_dw(name, a, b, like, *, layer=0, groups=1, deps=()):
    s, ka_total = a.shape
    rows, cols = ka_total // groups, b.shape[1] // groups
    assert (rows, cols) == (like.rows, like.cols) and b.shape[0] == s, (name, a.shape, b.shape)
    tm, tn, tk = like.tile_rows(MM_TM), like.tile_cols(MM_TN), _tile(s, MM_TK)
    nr, nc, nk = rows // tm, cols // tn, s // tk

    def lay(g):
        return g if groups > 1 else layer

    in_specs = [pl.BlockSpec((tk, tm), lambda g, i, n, k: (k, g * nr + i)),
                pl.BlockSpec((tk, tn), lambda g, i, n, k: (k, g * nc + n))]
    in_specs += [_ANY] * len(deps)

    def body(a_ref, b_ref, *rest):
        o_ref, acc = rest[-2], rest[-1]
        k = pl.program_id(3)

        @pl.when(k == 0)
        def _():
            acc[...] = jnp.zeros_like(acc)

        av, bv = a_ref[...], b_ref[...]
        if av.dtype != BF16:
            av = av.astype(BF16)
        if bv.dtype != BF16:
            bv = bv.astype(BF16)
        acc[...] += lax.dot_general(av, bv, (((0,), (0,)), ((), ())), preferred_element_type=F32)

        @pl.when(k == nk - 1)
        def _():
            o_ref[...] = acc[...]

    return pl.pallas_call(
        body, name=name, grid=(groups, nr, nc, nk), in_specs=in_specs,
        out_specs=pl.BlockSpec((None, None, tm, tn), lambda g, i, n, k: like.index(lay(g), i, n, tm, tn)),
        out_shape=jax.ShapeDtypeStruct((like.nj, like.nl, like.r, like.c), F32),
        scratch_shapes=[pltpu.VMEM((tm, tn), F32)],
        compiler_params=_params("parallel", "parallel", "parallel", "arbitrary"),
    )(a, b, *deps)


def _row_spec(tr, d):
    return pl.BlockSpec((tr, d), lambda i: (i, 0))


def _vec_spec(d):
    return pl.BlockSpec((1, d), lambda i: (0, 0))


def _rms_fwd(name, x, g):
    s, d = x.shape
    tr = _tile(s, ROW_TILE)

    def body(x_ref, g_ref, o_ref):
        xv = x_ref[...]
        r = lax.rsqrt(jnp.mean(xv * xv, axis=-1, keepdims=True) + EPS)
        o_ref[...] = (xv * r * g_ref[...]).astype(BF16)

    return pl.pallas_call(
        body, name=name, grid=(s // tr,), in_specs=[_row_spec(tr, d), _vec_spec(d)],
        out_specs=_row_spec(tr, d), out_shape=jax.ShapeDtypeStruct((s, d), BF16),
        compiler_params=_params("parallel"),
    )(x, g)


def _rms_bwd(name, x, g, dhn, dres):
    s, d = x.shape
    tr = _tile(s, ROW_TILE)

    def body(x_ref, g_ref, dhn_ref, dres_ref, dx_ref, dg_ref):
        @pl.when(pl.program_id(0) == 0)
        def _():
            dg_ref[...] = jnp.zeros_like(dg_ref)

        xv = x_ref[...]
        r = lax.rsqrt(jnp.mean(xv * xv, axis=-1, keepdims=True) + EPS)
        xh = xv * r
        dy = dhn_ref[...]
        dg_ref[...] += jnp.sum(dy * xh, axis=0, keepdims=True)
        dxh = dy * g_ref[...]
        dx_ref[...] = dres_ref[...] + r * (dxh - xh * jnp.mean(dxh * xh, axis=-1, keepdims=True))

    return pl.pallas_call(
        body, name=name, grid=(s // tr,),
        in_specs=[_row_spec(tr, d), _vec_spec(d), _row_spec(tr, d), _row_spec(tr, d)],
        out_specs=[_row_spec(tr, d), _vec_spec(d)],
        out_shape=[jax.ShapeDtypeStruct((s, d), F32), jax.ShapeDtypeStruct((1, d), F32)],
        compiler_params=_params("arbitrary"),
    )(x, g, dhn, dres)


def _final(name, h, g, tgt):
    s, d = h.shape
    tr = _tile(s, ROW_TILE)

    def body(h_ref, g_ref, t_ref, dh_ref, dg_ref, loss_ref):
        @pl.when(pl.program_id(0) == 0)
        def _():
            dg_ref[...] = jnp.zeros_like(dg_ref)
            loss_ref[...] = jnp.zeros_like(loss_ref)

        hv = h_ref[...]
        r = lax.rsqrt(jnp.mean(hv * hv, axis=-1, keepdims=True) + EPS)
        xh = hv * r
        gv = g_ref[...]
        err = xh * gv - t_ref[...]
        part = 0.5 * jnp.sum(jnp.mean(err * err, axis=-1, keepdims=True), axis=0, keepdims=True)
        loss_ref[...] += jnp.broadcast_to(part, loss_ref.shape)
        dy = err * (1.0 / d)
        dg_ref[...] += jnp.sum(dy * xh, axis=0, keepdims=True)
        dxh = dy * gv
        dh_ref[...] = r * (dxh - xh * jnp.mean(dxh * xh, axis=-1, keepdims=True))

    return pl.pallas_call(
        body, name=name, grid=(s // tr,),
        in_specs=[_row_spec(tr, d), _vec_spec(d), _row_spec(tr, d)],
        out_specs=[_row_spec(tr, d), _vec_spec(d), _vec_spec(LANES)],
        out_shape=[jax.ShapeDtypeStruct((s, d), F32), jax.ShapeDtypeStruct((1, d), F32),
                   jax.ShapeDtypeStruct((1, LANES), F32)],
        compiler_params=_params("arbitrary"),
    )(h, g, tgt)


_SQRT_HALF = 1.0 / math.sqrt(2.0)
_INV_SQRT_2PI = 1.0 / math.sqrt(2.0 * math.pi)


def _gelu(x):
    return x * (lax.erf(x * _SQRT_HALF) + 1.0) * 0.5


def _gelu_grad(x):
    return 0.5 * (lax.erf(x * _SQRT_HALF) + 1.0) + x * jnp.exp(-0.5 * x * x) * _INV_SQRT_2PI


def _causal(chunk):
    t = lax.broadcasted_iota(jnp.int32, (chunk, chunk), 0)
    sidx = lax.broadcasted_iota(jnp.int32, (chunk, chunk), 1)
    return sidx <= t


def _layernorm_parts(v, g, b):
    mu = jnp.mean(v, axis=-1, keepdims=True)
    vc = v - mu
    rs = lax.rsqrt(jnp.mean(vc * vc, axis=-1, keepdims=True) + EPS)
    vhat = vc * rs
    return vhat, rs, vhat * g + b


def _gate_fwd(name, zpre, ln_g, ln_b, w_s, b_col):
    s, aw2 = zpre.shape
    aw = aw2 // 2
    ng, chunk, _ = w_s.shape
    dh = aw // ng

    def body(z_ref, g_ref, b_ref, ws_ref, bc_ref, o_ref):
        u = _gelu(z_ref[:, :aw])
        v = _gelu(z_ref[:, aw:])
        _, _, vln = _layernorm_parts(v, g_ref[...], b_ref[...])
        mask = _causal(chunk)
        for gi in range(ng):
            sl = slice(gi * dh, (gi + 1) * dh)
            wm = jnp.where(mask, ws_ref[gi], 0.0).astype(BF16)
            sg = jnp.dot(wm, vln[:, sl].astype(BF16), preferred_element_type=F32) + bc_ref[:, gi:gi + 1]
            o_ref[:, sl] = (u[:, sl] * sg).astype(BF16)

    return pl.pallas_call(
        body, name=name, grid=(s // chunk,),
        in_specs=[_row_spec(chunk, aw2), _vec_spec(aw), _vec_spec(aw),
                  pl.BlockSpec((ng, chunk, chunk), lambda i: (0, 0, 0)),
                  pl.BlockSpec((chunk, ng), lambda i: (0, 0))],
        out_specs=_row_spec(chunk, aw), out_shape=jax.ShapeDtypeStruct((s, aw), BF16),
        compiler_params=_params("parallel"),
    )(zpre, ln_g, ln_b, w_s, b_col)


def _gate_bwd(name, zpre, dgated, ln_g, ln_b, w_s, b_col):
    s, aw2 = zpre.shape
    aw = aw2 // 2
    ng, chunk, _ = w_s.shape
    dh = aw // ng

    def body(z_ref, dgt_ref, g_ref, b_ref, ws_ref, bc_ref, dz_ref, dws_ref, dbc_ref, dlg_ref, dlb_ref,
             du_scr, dvln_scr):
        @pl.when(pl.program_id(0) == 0)
        def _():
            dws_ref[...] = jnp.zeros_like(dws_ref)
            dbc_ref[...] = jnp.zeros_like(dbc_ref)
            dlg_ref[...] = jnp.zeros_like(dlg_ref)
            dlb_ref[...] = jnp.zeros_like(dlb_ref)

        zu = z_ref[:, :aw]
        zv = z_ref[:, aw:]
        u = _gelu(zu)
        lg = g_ref[...]
        vhat, rs, vln = _layernorm_parts(_gelu(zv), lg, b_ref[...])
        mask = _causal(chunk)
        for gi in range(ng):
            sl = slice(gi * dh, (gi + 1) * dh)
            wm = jnp.where(mask, ws_ref[gi], 0.0).astype(BF16)
            vg = vln[:, sl].astype(BF16)
            sg = jnp.dot(wm, vg, preferred_element_type=F32) + bc_ref[:, gi:gi + 1]
            dgt = dgt_ref[:, sl]
            du_scr[:, sl] = dgt * sg
            ds = dgt * u[:, sl]
            dbc_ref[:, gi:gi + 1] += jnp.sum(ds, axis=-1, keepdims=True)
            dsb = ds.astype(BF16)
            dwm = lax.dot_general(dsb, vg, (((1,), (1,)), ((), ())), preferred_element_type=F32)
            dws_ref[gi] += jnp.where(mask, dwm, 0.0)
            dvln_scr[:, sl] = lax.dot_general(wm, dsb, (((0,), (0,)), ((), ())), preferred_element_type=F32)
        dvln = dvln_scr[...]
        dlb_ref[...] += jnp.sum(dvln, axis=0, keepdims=True)
        dlg_ref[...] += jnp.sum(dvln * vhat, axis=0, keepdims=True)
        dvh = dvln * lg
        dv = rs * (dvh - jnp.mean(dvh, axis=-1, keepdims=True)
                   - vhat * jnp.mean(dvh * vhat, axis=-1, keepdims=True))
        dz_ref[:, :aw] = (du_scr[...] * _gelu_grad(zu)).astype(BF16)
        dz_ref[:, aw:] = (dv * _gelu_grad(zv)).astype(BF16)

    return pl.pallas_call(
        body, name=name, grid=(s // chunk,),
        in_specs=[_row_spec(chunk, aw2), _row_spec(chunk, aw), _vec_spec(aw), _vec_spec(aw),
                  pl.BlockSpec((ng, chunk, chunk), lambda i: (0, 0, 0)),
                  pl.BlockSpec((chunk, ng), lambda i: (0, 0))],
        out_specs=[_row_spec(chunk, aw2), pl.BlockSpec((ng, chunk, chunk), lambda i: (0, 0, 0)),
                   pl.BlockSpec((chunk, ng), lambda i: (0, 0)), _vec_spec(aw), _vec_spec(aw)],
        out_shape=[jax.ShapeDtypeStruct((s, aw2), BF16), jax.ShapeDtypeStruct((ng, chunk, chunk), F32),
                   jax.ShapeDtypeStruct((chunk, ng), F32), jax.ShapeDtypeStruct((1, aw), F32),
                   jax.ShapeDtypeStruct((1, aw), F32)],
        scratch_shapes=[pltpu.VMEM((chunk, aw), F32), pltpu.VMEM((chunk, aw), F32)],
        compiler_params=_params("arbitrary"),
    )(zpre, dgated, ln_g, ln_b, w_s, b_col)


def _pool_select(g, parts):
    out = parts[-1]
    for gi in range(len(parts) - 2, -1, -1):
        out = jnp.where(g == gi, parts[gi], out)
    return out


def _pool_specs(s, bw):
    head = bw // len(B_WINDOWS)
    tc = _tile(head, 256)
    nb = head // tc
    return tc, (len(B_WINDOWS), nb), pl.BlockSpec((s, tc), lambda g, j: (0, g * nb + j))


def _pool_window(g, t):
    w = _pool_select(g, [jnp.full(t.shape, wi, jnp.int32) for wi in B_WINDOWS])
    return jnp.minimum(t + 1, w).astype(F32)


def _pool_fwd(name, vb):
    assert B_WINDOWS == (2, 4, 8, 16)
    s, bw = vb.shape
    tc, grid, spec = _pool_specs(s, bw)

    def body(v_ref, o_ref):
        g = pl.program_id(0)
        v = v_ref[...]
        t = lax.broadcasted_iota(jnp.int32, (s, tc), 0)

        def down(x, k):
            return jnp.where(t >= k, pltpu.roll(x, k, 0), 0.0)

        sums, cur, k = [], v, 1
        for _ in B_WINDOWS:
            cur = cur + down(cur, k)
            sums.append(cur)
            k *= 2
        o_ref[...] = (_pool_select(g, sums) / _pool_window(g, t) - v).astype(BF16)

    return pl.pallas_call(
        body, name=name, grid=grid, in_specs=[spec], out_specs=spec,
        out_shape=jax.ShapeDtypeStruct((s, bw), BF16), compiler_params=_params("parallel", "parallel"),
    )(vb)


def _pool_bwd(name, dpooled):
    s, bw = dpooled.shape
    tc, grid, spec = _pool_specs(s, bw)

    def body(d_ref, o_ref):
        g = pl.program_id(0)
        dp = d_ref[...]
        t = lax.broadcasted_iota(jnp.int32, (s, tc), 0)

        def up(x, k):
            return jnp.where(t < s - k, pltpu.roll(x, s - k, 0), 0.0)

        sums, cur, k = [], dp / _pool_window(g, t), 1
        for _ in B_WINDOWS:
            cur = cur + up(cur, k)
            sums.append(cur)
            k *= 2
        o_ref[...] = (_pool_select(g, sums) - dp).astype(BF16)

    return pl.pallas_call(
        body, name=name, grid=grid, in_specs=[spec], out_specs=spec,
        out_shape=jax.ShapeDtypeStruct((s, bw), BF16), compiler_params=_params("parallel", "parallel"),
    )(dpooled)


def _scale_bwd(name, dms, mixed, scale):
    s, bw = dms.shape
    tr = _tile(s, ROW_TILE)

    def body(d_ref, m_ref, sc_ref, o_ref, ds_ref):
        @pl.when(pl.program_id(0) == 0)
        def _():
            ds_ref[...] = jnp.zeros_like(ds_ref)

        dv = d_ref[...]
        ds_ref[...] += jnp.sum(dv * m_ref[...], axis=0, keepdims=True)
        o_ref[...] = (dv * sc_ref[...]).astype(BF16)

    return pl.pallas_call(
        body, name=name, grid=(s // tr,), in_specs=[_row_spec(tr, bw), _row_spec(tr, bw), _vec_spec(bw)],
        out_specs=[_row_spec(tr, bw), _vec_spec(bw)],
        out_shape=[jax.ShapeDtypeStruct((s, bw), BF16), jax.ShapeDtypeStruct((1, bw), F32)],
        compiler_params=_params("arbitrary"),
    )(dms, mixed, scale)


def _adam_update(w, g, m, v):
    m = ADAM_B1 * m + (1.0 - ADAM_B1) * g
    v = ADAM_B2 * v + (1.0 - ADAM_B2) * (g * g)
    m_hat = m / (1.0 - ADAM_B1 ** ADAM_STEP)
    v_hat = v / (1.0 - ADAM_B2 ** ADAM_STEP)
    delta = -ADAM_LR * (m_hat / (jnp.sqrt(v_hat) + ADAM_EPS) + ADAM_WD * w)
    return delta, m, v


def _adam_shard(name, w, m, v, parts, layer=0, prev=None):
    nl, r, c = w.shape
    nj, nlp = parts.shape[:2]
    tr = _tile(r, 128)
    spec = pl.BlockSpec((None, tr, c), lambda l, i: (layer + l, i, 0))

    def body(w_ref, m_ref, v_ref, p_ref, *rest):
        g_ref, d_ref, nm_ref, nv_ref = rest[-4:]
        g = p_ref[0].astype(F32)
        for j in range(1, nj):
            g = g + p_ref[j].astype(F32)
        delta, nm, nv = _adam_update(w_ref[...], g, m_ref[...], v_ref[...])
        g_ref[...] = g
        d_ref[...] = delta
        nm_ref[...] = nm
        nv_ref[...] = nv

    prev = () if prev is None else tuple(prev)
    return pl.pallas_call(
        body, name=name, grid=(nlp, r // tr),
        in_specs=[spec, spec, spec, pl.BlockSpec((nj, None, tr, c), lambda l, i: (0, l, i, 0))] + [_ANY] * len(prev),
        out_specs=[spec] * 4, out_shape=[jax.ShapeDtypeStruct(w.shape, F32)] * 4,
        input_output_aliases={4 + i: i for i in range(len(prev))},
        compiler_params=_params("parallel", "parallel"),
    )(w, m, v, parts, *prev)


def _adam_packed(name, w, g, m, v):
    rows, lanes = w.shape
    tr = _tile(rows, 512)
    spec = pl.BlockSpec((tr, lanes), lambda i: (i, 0))

    def body(w_ref, g_ref, m_ref, v_ref, d_ref, nm_ref, nv_ref):
        delta, nm, nv = _adam_update(w_ref[...], g_ref[...], m_ref[...], v_ref[...])
        d_ref[...] = delta
        nm_ref[...] = nm
        nv_ref[...] = nv

    return pl.pallas_call(
        body, name=name, grid=(rows // tr,), in_specs=[spec] * 4, out_specs=[spec] * 3,
        out_shape=[jax.ShapeDtypeStruct(w.shape, F32)] * 3, compiler_params=_params("parallel"),
    )(w, g, m, v)


def _place():
    x, y, c = lax.axis_index("x"), lax.axis_index("y"), lax.axis_index("c")
    chips = [(1 - x, y), (x, 1 - y), (1 - x, 1 - y)]
    return x, y, c, chips


def _remote(src, dst, send_sem, recv_sem, device):
    return pltpu.make_async_remote_copy(src_ref=src, dst_ref=dst, send_sem=send_sem, recv_sem=recv_sem,
                                        device_id=device, device_id_type=MESH)


def _half(ref_rows, cc):
    h = ref_rows // 2
    return pl.ds(cc * h, h)


_HBM = pl.BlockSpec(memory_space=pltpu.HBM)
_SEM = pl.BlockSpec(memory_space=pltpu.SEMAPHORE)
_EFFECT = pltpu.SideEffectType.DATAFLOW_SIDE_EFFECTING


def _in_hbm(arr):
    return pltpu.with_memory_space_constraint(arr, pltpu.HBM)


def _gather_block(land, shard_rows, split, j, cc):
    if split:
        return land.at[j, :, _half(shard_rows, cc), :]
    return land.at[j]


def _gather_copy(src, land, split, send_sem, recv_sem, x, y, c, chip):
    qx, qy = chip
    rows = src.shape[1]
    part = src.at[:, _half(rows, c), :] if split else src
    return _remote(part, _gather_block(land, rows, split, 2 * x + y, c), send_sem, recv_sem, (qx, qy, c))


def _gather_start(name, shards, split):
    n = len(shards)

    def body(*refs):
        srcs, lands, send, recv = refs[:n], refs[n:2 * n], refs[2 * n:3 * n], refs[3 * n:4 * n]
        token = refs[-1]
        x, y, c, chips = _place()
        for a in range(n):
            for qi in range(3):
                _gather_copy(srcs[a], lands[a], split[a], send[a], recv[a], x, y, c, chips[qi]).start()
        token[...] = jnp.zeros_like(token)

    lands = [lax.empty((N_CHIPS, *sh.shape), sh.dtype) for sh in shards]
    outs = pl.pallas_call(
        body, name=name, in_specs=[_HBM] * (2 * n),
        out_specs=[_SEM] * (2 * n) + [_HBM] * (2 * n) + [pl.BlockSpec(memory_space=pltpu.VMEM)],
        out_shape=[pltpu.SemaphoreType.DMA(())] * (2 * n)
        + [pltpu.HBM(sh.shape, sh.dtype) for sh in shards] + [pltpu.HBM(ld.shape, ld.dtype) for ld in lands]
        + [jax.ShapeDtypeStruct((8, LANES), F32)],
        input_output_aliases={i: 2 * n + i for i in range(2 * n)},
        compiler_params=pltpu.CompilerParams(has_side_effects=_EFFECT),
    )(*[_in_hbm(s) for s in shards], *[_in_hbm(ld) for ld in lands])
    return list(outs[:n]), list(outs[n:2 * n]), list(outs[2 * n:3 * n]), list(outs[3 * n:4 * n]), outs[-1]


def _gather_wait(name, split, send, recv, shards, lands, after):
    n = len(shards)

    def body(*refs):
        lds, snd, rcv = refs[n:2 * n], refs[2 * n:3 * n], refs[3 * n:4 * n]
        x, y, c, _ = _place()
        for k in range(n):
            three = lds[k].at[pl.ds(0, 3)]
            if split[k]:
                three = three.at[:, :, _half(lds[k].shape[2], c), :]
            all_three = _remote(three, three, snd[k], rcv[k], (x, y, c))
            all_three.wait_send()
            all_three.wait_recv()

    outs = pl.pallas_call(
        body, name=name, in_specs=[_HBM] * (2 * n) + [_SEM] * (2 * n) + [_ANY], out_specs=[_HBM] * (2 * n),
        out_shape=[pltpu.HBM(s.shape, s.dtype) for s in shards] + [pltpu.HBM(ld.shape, ld.dtype) for ld in lands],
        input_output_aliases={i: i for i in range(2 * n)},
        compiler_params=pltpu.CompilerParams(has_side_effects=_EFFECT),
    )(*shards, *lands, *send, *recv, after)
    return list(outs[:n]), list(outs[n:])


def _fill_own(name, src, dest, dest_shape, place, src_slot=False, half_rows=False):
    nl, r, c = src.shape[-3:]
    tr = _tile(r, 512)
    nb = r // tr

    def src_map(l, i, p):
        return (p[0], l, i, 0) if src_slot else (l, i, 0)

    def dst_map(l, i, p):
        return (p[0], l, (p[1] * nb if half_rows else 0) + i, 0)

    def body(p_ref, s_ref, *rest):
        rest[-1][...] = s_ref[...]

    in_specs = [pl.BlockSpec((None, None, tr, c) if src_slot else (None, tr, c), src_map)]
    operands, aliases = [src], {}
    if dest is not None:
        in_specs.append(_ANY)
        operands.append(dest)
        aliases = {2: 0}
    return pl.pallas_call(
        body, name=name,
        grid_spec=pltpu.PrefetchScalarGridSpec(
            num_scalar_prefetch=1, grid=(nl, nb), in_specs=in_specs,
            out_specs=pl.BlockSpec((None, None, tr, c), dst_map)),
        out_shape=jax.ShapeDtypeStruct(dest_shape, src.dtype), input_output_aliases=aliases,
        compiler_params=_params("parallel", "parallel"),
    )(place, *operands)


def _gather_finish(name, lands):
    n = len(lands)

    def body(*refs):
        outs = refs[n:2 * n]
        fsend, frecv = refs[2 * n:]
        x, y, c, chips = _place()
        sib = (x, y, 1 - c)

        def relay(a, qi, cc):
            qx, qy = chips[qi]
            blk = _gather_block(outs[a], outs[a].shape[2], True, 2 * qx + qy, cc)
            return _remote(blk, blk, fsend.at[a, qi], frecv.at[a, qi], sib)

        relays = [relay(a, qi, c) for a in range(n) for qi in range(3)]
        for cp in relays:
            cp.start()
        for a in range(n):
            for qi in range(3):
                relay(a, qi, 1 - c).wait_recv()
        for cp in relays:
            cp.wait_send()

    outs = pl.pallas_call(
        body, name=name, in_specs=[_ANY] * n, out_specs=[_ANY] * n,
        out_shape=[jax.ShapeDtypeStruct(ld.shape, ld.dtype) for ld in lands],
        input_output_aliases={i: i for i in range(n)},
        scratch_shapes=[pltpu.SemaphoreType.DMA((n, 3))] * 2,
    )(*lands)
    return list(outs)


def _split_start(name, srcs, lands, plan):
    n = len(srcs)

    def body(*refs):
        s, ld, send, recv, token = refs[:n], refs[n:2 * n], refs[2 * n:3 * n], refs[3 * n:4 * n], refs[-1]
        for a, copies in enumerate(plan(s, ld)):
            for src, dst, peer in copies:
                _remote(src, dst, send[a], recv[a], peer).start()
        token[...] = jnp.zeros_like(token)

    both = list(srcs) + list(lands)
    outs = pl.pallas_call(
        body, name=name, in_specs=[_HBM] * (2 * n),
        out_specs=[_SEM] * (2 * n) + [_HBM] * (2 * n) + [pl.BlockSpec(memory_space=pltpu.VMEM)],
        out_shape=[pltpu.SemaphoreType.DMA(())] * (2 * n) + [pltpu.HBM(b.shape, b.dtype) for b in both]
        + [jax.ShapeDtypeStruct((8, LANES), F32)],
        input_output_aliases={i: 2 * n + i for i in range(2 * n)},
        compiler_params=pltpu.CompilerParams(has_side_effects=_EFFECT),
    )(*[_in_hbm(b) for b in both])
    return list(outs[:n]), list(outs[n:2 * n]), list(outs[2 * n:3 * n]), list(outs[3 * n:4 * n]), outs[-1]


def _split_wait(name, srcs, lands, send, recv, after, whole):
    n = len(srcs)

    def body(*refs):
        ld, snd, rcv = refs[n:2 * n], refs[2 * n:3 * n], refs[3 * n:4 * n]
        x, y, c, _ = _place()
        for a, blk in enumerate(whole(ld)):
            every = _remote(blk, blk, snd[a], rcv[a], (x, y, c))
            every.wait_send()
            every.wait_recv()

    both = list(srcs) + list(lands)
    outs = pl.pallas_call(
        body, name=name, in_specs=[_HBM] * (2 * n) + [_SEM] * (2 * n) + [_ANY], out_specs=[_HBM] * (2 * n),
        out_shape=[pltpu.HBM(b.shape, b.dtype) for b in both],
        input_output_aliases={i: i for i in range(2 * n)},
        compiler_params=pltpu.CompilerParams(has_side_effects=_EFFECT),
    )(*both, *send, *recv, after)
    return list(outs[:n]), list(outs[n:])


def _pair_plan(srcs, lands):
    x, y, c, _ = _place()
    return [[(s.at[:, :, _half(s.shape[2], 1 - c), :], ld, (x, y, 1 - c))] for s, ld in zip(srcs, lands)]


def _pair_whole(lands):
    return list(lands)


def _scatter_plan(srcs, lands):
    x, y, c, chips = _place()
    return [[(s.at[2 * qx + qy], ld.at[2 * x + y, :, _half(ld.shape[2], c), :], (qx, qy, c)) for qx, qy in chips]
            for s, ld in zip(srcs, lands)]


def _scatter_whole(lands):
    _, _, c, _ = _place()
    return [ld.at[pl.ds(0, 3), :, _half(ld.shape[2], c), :] for ld in lands]


def _pair_sum(name, grad, other, place):
    nj, nl, r, c = grad.shape
    h = r // 2
    tr = _tile(h, 256)
    nb = h // tr

    def body(p_ref, g_ref, o_ref, q_ref, d_ref):
        q = (g_ref[...] + o_ref[...]).astype(BF16)
        q_ref[...] = q

        @pl.when(pl.program_id(2) == p_ref[0])
        def _():
            d_ref[...] = q

    blk = (None, None, tr, c)
    return pl.pallas_call(
        body, name=name,
        grid_spec=pltpu.PrefetchScalarGridSpec(
            num_scalar_prefetch=1, grid=(nl, nb, nj),
            in_specs=[pl.BlockSpec(blk, lambda l, i, j, p: (j, l, p[1] * nb + i, 0)),
                      pl.BlockSpec(blk, lambda l, i, j, p: (j, l, i, 0))],
            out_specs=[pl.BlockSpec(blk, lambda l, i, j, p: (j, l, i, 0)),
                       pl.BlockSpec(blk, lambda l, i, j, p: (p[0], l, p[1] * nb + i, 0))]),
        out_shape=[jax.ShapeDtypeStruct((nj, nl, h, c), BF16), jax.ShapeDtypeStruct((nj, nl, r, c), BF16)],
        compiler_params=_params("parallel", "parallel", "arbitrary"),
    )(place, grad, other)


def _scatter_finish(name, dests):
    n = len(dests)

    def body(*refs):
        outs = refs[n:2 * n]
        fsend, frecv = refs[2 * n:]
        x, y, c, _ = _place()
        sib = (x, y, 1 - c)

        def relay(a, j, cc):
            blk = outs[a].at[j, :, _half(outs[a].shape[2], cc), :]
            return _remote(blk, blk, fsend.at[a, j], frecv.at[a, j], sib)

        relays = [relay(a, j, c) for a in range(n) for j in range(N_CHIPS)]
        for cp in relays:
            cp.start()
        for a in range(n):
            for j in range(N_CHIPS):
                relay(a, j, 1 - c).wait_recv()
        for cp in relays:
            cp.wait_send()

    outs = pl.pallas_call(
        body, name=name, in_specs=[_ANY] * n, out_specs=[_ANY] * n,
        out_shape=[jax.ShapeDtypeStruct(d.shape, d.dtype) for d in dests],
        input_output_aliases={i: i for i in range(n)},
        scratch_shapes=[pltpu.SemaphoreType.DMA((n, N_CHIPS))] * 2,
    )(*dests)
    return list(outs)


def _allreduce_small(name, packed):
    rows, lanes = packed.shape

    def body(x_ref, o_ref, all_ref, send, recv, lsem):
        x, y, c, chips = _place()
        me, sib = (x, y, c), (x, y, 1 - c)

        def slot(px, py, pc):
            return all_ref.at[pl.ds((4 * px + 2 * py + pc) * rows, rows), :]

        def copy(k, blk, to, src=None):
            return _remote(slot(*blk) if src is None else src, slot(*blk), send.at[k], recv.at[k], to)

        mine = pltpu.make_async_copy(x_ref, slot(*me), lsem)
        mine.start()
        first = [copy(0, me, sib, src=x_ref)] + [copy(1 + j, me, (*chip, c), src=x_ref) for j, chip in enumerate(chips)]
        for cp in first:
            cp.start()
        passed = [copy(4 + j, (*chip, c), sib) for j, chip in enumerate(chips)]
        for j, chip in enumerate(chips):
            copy(1 + j, (*chip, c), me).wait_recv()
            passed[j].start()
        copy(0, sib, me).wait_recv()
        for j, chip in enumerate(chips):
            copy(4 + j, (*chip, 1 - c), me).wait_recv()
        for cp in first + passed:
            cp.wait_send()
        mine.wait()
        total = all_ref[pl.ds(0, rows), :]
        for d in range(1, N_DEV):
            total = total + all_ref[pl.ds(d * rows, rows), :]
        o_ref[...] = total

    vmem = pl.BlockSpec(memory_space=pltpu.VMEM)
    return pl.pallas_call(
        body, name=name, in_specs=[vmem], out_specs=vmem, out_shape=jax.ShapeDtypeStruct(packed.shape, F32),
        scratch_shapes=[pltpu.VMEM((N_DEV * rows, lanes), F32), pltpu.SemaphoreType.DMA((7,)),
                        pltpu.SemaphoreType.DMA((7,)), pltpu.SemaphoreType.DMA],
        compiler_params=pltpu.CompilerParams(vmem_limit_bytes=VMEM_LIMIT),
    )(packed)


def _pack(parts):
    rows = []
    for p in parts:
        flat = p.reshape(-1)
        pad = (-flat.shape[0]) % PACK_ELEMS
        rows.append(jnp.pad(flat, (0, pad)).reshape(-1, LANES))
    return jnp.concatenate(rows, axis=0)


def _unpack(packed, shapes):
    out, row = [], 0
    for sh in shapes:
        size = math.prod(sh)
        nrows = -(-size // PACK_ELEMS) * (PACK_ELEMS // LANES)
        out.append(packed[row:row + nrows].reshape(-1)[:size].reshape(sh))
        row += nrows
    return out


def kernel(x, a_w_in, a_ln_g, a_ln_b, a_w_s, a_b_s, a_w_out, b_w_in, b_w_grp, b_scale, b_w_out, norm_mix, norm_mlp, mlp_w1, mlp_w2, final_norm, loss_target, m_a_w_in, m_a_ln_g, m_a_ln_b, m_a_w_s, m_a_b_s, m_a_w_out, m_b_w_in, m_b_w_grp, m_b_scale, m_b_w_out, m_norm_mix, m_norm_mlp, m_mlp_w1, m_mlp_w2, m_final_norm, v_a_w_in, v_a_ln_g, v_a_ln_b, v_a_w_s, v_a_b_s, v_a_w_out, v_b_w_in, v_b_w_grp, v_b_scale, v_b_w_out, v_norm_mix, v_norm_mlp, v_mlp_w1, v_mlp_w2, v_final_norm):
    xi, yi, ci = lax.axis_index("x"), lax.axis_index("y"), lax.axis_index("c")
    chip = 2 * xi + yi
    place = jnp.stack([chip, ci]).astype(jnp.int32)
    x2, tgt = x[0], loss_target[0]
    bw = b_scale.shape[1] * N_CHIPS

    units = dict(a_w_in=a_w_in, a_w_out=a_w_out, w1_0=mlp_w1[0:1], w2_0=mlp_w2[0:1], b_scale=b_scale.reshape(1, 1, -1),
                 b_w_in=b_w_in, b_w_grp=b_w_grp[0], b_w_out=b_w_out, w1_1=mlp_w1[1:2], w2_1=mlp_w2[1:2])
    unit_names = list(units)
    split = [k != "b_scale" for k in unit_names]
    send, recv, shards_t, lands_t, token = _gather_start(
        "gather_start", [units[k].astype(BF16) if s else units[k] for k, s in zip(unit_names, split)], split)
    col_sharded = dict(a_w_in=True, a_w_out=False, b_w_in=False, b_w_grp=False, b_w_out=False,
                       w1_0=True, w2_0=False, w1_1=True, w2_1=False)
    W = {}

    def arrive(keys, after):
        ids = [unit_names.index(k) for k in keys]
        sp = [split[i] for i in ids]
        sh, ld = _gather_wait(f"gather_wait_{keys[0]}", sp, [send[i] for i in ids], [recv[i] for i in ids],
                              [shards_t[i] for i in ids], [lands_t[i] for i in ids], after)
        ld = [_fill_own(f"gather_own_{k}", s, l, l.shape, place) for k, s, l in zip(keys, sh, ld)]
        relayed = iter(_gather_finish(f"gather_finish_{keys[0]}", [l for l, s in zip(ld, sp) if s]))
        for k, l, s in zip(keys, ld, sp):
            full = next(relayed) if s else l
            W[k] = _W4(full, col_sharded[k]) if k in col_sharded else full

    b_col = a_b_s[0].T

    def residual(acc, res):
        return (res + acc,)

    def sq_relu(acc):
        act = jnp.maximum(acc, 0.0)
        return act, act * act

    def mlp_fwd(tag, h, layer):
        hn = _rms_fwd(f"mlp{tag}_norm", h, norm_mlp[layer:layer + 1])
        arrive([f"w1_{layer}"], hn)
        act, act_sq = _mm_aw(f"mlp{tag}_up", hn, W[f"w1_{layer}"], out_dtypes=(F32, BF16), epilogue=sq_relu)
        arrive([f"w2_{layer}"], act_sq)
        out = _mm_aw(f"mlp{tag}_down", act_sq, W[f"w2_{layer}"], extras=(h,), epilogue=residual)
        return out, (h, hn, act, act_sq)

    hn0 = _rms_fwd("mix_a_norm", x2, norm_mix[0:1])
    arrive(["a_w_in"], token)
    zpre = _mm_aw("mix_a_in", hn0, W["a_w_in"])
    gated = _gate_fwd("mix_a_gate", zpre, a_ln_g, a_ln_b, a_w_s[0], b_col)
    arrive(["a_w_out"], gated)
    h1 = _mm_aw("mix_a_out", gated, W["a_w_out"], extras=(x2,), epilogue=residual)
    h2, mlp0 = mlp_fwd("0", h1, 0)
    hn2 = _rms_fwd("mix_b_norm", h2, norm_mix[1:2])
    arrive(["b_scale", "b_w_in"], hn2)
    scale_full = W["b_scale"].reshape(1, bw)
    vb = _mm_aw("mix_b_in", hn2, W["b_w_in"])
    pooled = _pool_fwd("mix_b_pool", vb)
    arrive(["b_w_grp", "b_w_out"], pooled)
    mixed, ms = _mm_aw("mix_b_grp", pooled, W["b_w_grp"], groups=len(B_WINDOWS), extras=(scale_full,),
                       out_dtypes=(F32, BF16), epilogue=lambda acc, sc: (acc, acc * sc))
    h3 = _mm_aw("mix_b_out", ms, W["b_w_out"], extras=(h2,), epilogue=residual)
    h4, mlp1 = mlp_fwd("1", h3, 1)
    dh4, d_final, loss_part = _final("loss_head", h4, final_norm.reshape(1, -1), tgt)
    g1_like = _W4(None, True, shape=(N_CHIPS, 1, *W["w1_0"].arr.shape[2:]))
    g2_like = _W4(None, False, shape=(N_CHIPS, 1, *W["w2_0"].arr.shape[2:]))

    def exchange(tag, gs):
        zones = [lax.empty((g.shape[0], g.shape[1], g.shape[2] // 2, g.shape[3]), g.dtype) for g in gs]
        send, recv, srcs, zones, tok = _split_start(f"pair_start_{tag}", gs, zones, _pair_plan)
        return (tag, send, recv, srcs, zones), tok

    def reduce(state, after):
        tag, send, recv, srcs, zones = state
        srcs, zones = _split_wait(f"pair_wait_{tag}", srcs, zones, send, recv, after, _pair_whole)
        both = [_pair_sum(f"pair_sum_{tag}_{i}", g, o, place) for i, (g, o) in enumerate(zip(srcs, zones))]
        send, recv, sums, dests, tok = _split_start(f"scatter_start_{tag}", [b[0] for b in both],
                                                    [b[1] for b in both], _scatter_plan)
        return (tag, send, recv, sums, dests), tok

    def land(state, after):
        tag, send, recv, sums, dests = state
        _, dests = _split_wait(f"scatter_wait_{tag}", sums, dests, send, recv, after, _scatter_whole)
        return _scatter_finish(f"scatter_finish_{tag}", dests)

    def mlp_bwd(tag, dh, saved, layer, deps, pending=None):
        h, hn, act, act_sq = saved
        dpre = _mm_aw(f"mlp{tag}_down_dx", dh, W[f"w2_{layer}"], transpose_w=True, extras=(act,),
                      out_dtypes=(BF16,), epilogue=lambda acc, a: (acc * (2.0 * a),), deps=deps)
        scattering, dw_deps = None, ()
        if pending is not None:
            scattering, tok = reduce(pending, dpre)
            dw_deps = (tok,)
        g_w2 = _mm_dw(f"mlp{tag}_down_dw", act_sq, dh, g2_like, deps=dw_deps)
        pair_w2, tok = exchange(f"w2_{layer}", [g_w2])
        dhn = _mm_aw(f"mlp{tag}_up_dx", dpre, W[f"w1_{layer}"], transpose_w=True, deps=(tok,))
        g_w1 = _mm_dw(f"mlp{tag}_up_dw", hn, dpre, g1_like)
        pair_w1, tok1 = exchange(f"w1_{layer}", [g_w1])
        scat_w2, tok2 = reduce(pair_w2, dhn)
        dh_in, d_norm = _rms_bwd(f"mlp{tag}_norm_bwd", h, norm_mlp[layer:layer + 1], dhn, dh)
        return dh_in, d_norm, pair_w1, scat_w2, (tok1, tok2), scattering

    dh3, d_norm_mlp1, pair_w1_1, scat_w2_1, toks, _ = mlp_bwd("1", dh4, mlp1, 1, ())
    dms = _mm_aw("mix_b_out_dx", dh3, W["b_w_out"], transpose_w=True, deps=toks)
    g_b_out = _mm_dw("mix_b_out_dw", ms, dh3, W["b_w_out"])
    scat_w1_1, tok = reduce(pair_w1_1, dms)
    dmixed, d_scale = _scale_bwd("mix_b_scale_bwd", dms, mixed, scale_full)
    dpooled = _mm_aw("mix_b_grp_dx", dmixed, W["b_w_grp"], groups=len(B_WINDOWS), transpose_w=True, deps=(tok,))
    g_b_grp = _mm_dw("mix_b_grp_dw", pooled, dmixed, W["b_w_grp"], groups=len(B_WINDOWS))
    dvb = _pool_bwd("mix_b_pool_bwd", dpooled)
    dhn2 = _mm_aw("mix_b_in_dx", dvb, W["b_w_in"], transpose_w=True)
    g_b_in = _mm_dw("mix_b_in_dw", hn2, dvb, W["b_w_in"])
    pair_b, tok = exchange("b", [g_b_out, g_b_grp, g_b_in])
    dh2, d_norm_mix1 = _rms_bwd("mix_b_norm_bwd", h2, norm_mix[1:2], dhn2, dh3)
    dh1, d_norm_mlp0, pair_w1_0, scat_w2_0, toks, scat_b = mlp_bwd("0", dh2, mlp0, 0, (tok,), pending=pair_b)
    dgated = _mm_aw("mix_a_out_dx", dh1, W["a_w_out"], transpose_w=True, deps=toks)
    g_a_out = _mm_dw("mix_a_out_dw", gated, dh1, W["a_w_out"])
    scat_w1_0, tok = reduce(pair_w1_0, dgated)
    dzpre, d_w_s, d_b_col, d_ln_g, d_ln_b = _gate_bwd("mix_a_gate_bwd", zpre, dgated, a_ln_g, a_ln_b, a_w_s[0], b_col)
    dhn0 = _mm_aw("mix_a_in_dx", dzpre, W["a_w_in"], transpose_w=True, deps=(tok,))
    g_a_in = _mm_dw("mix_a_in_dw", hn0, dzpre, W["a_w_in"])
    pair_a, tok = exchange("a", [g_a_out, g_a_in])
    dx, d_norm_mix0 = _rms_bwd("mix_a_norm_bwd", x2, norm_mix[0:1], dhn0, dh1)
    scat_a, _ = reduce(pair_a, dx)

    moments = dict(a_w_in=(m_a_w_in, v_a_w_in), a_w_out=(m_a_w_out, v_a_w_out), b_w_in=(m_b_w_in, v_b_w_in),
                   b_w_grp=(m_b_w_grp, v_b_w_grp), b_w_out=(m_b_w_out, v_b_w_out),
                   mlp_w1=(m_mlp_w1, v_mlp_w1), mlp_w2=(m_mlp_w2, v_mlp_w2))
    weights = dict(a_w_in=a_w_in, a_w_out=a_w_out, b_w_in=b_w_in, b_w_grp=b_w_grp, b_w_out=b_w_out,
                   mlp_w1=mlp_w1, mlp_w2=mlp_w2)
    landing = [(scat_w2_1, [("mlp_w2", 1)]), (scat_w1_1, [("mlp_w1", 1)]),
               (scat_b, [("b_w_out", 0), ("b_w_grp", 0), ("b_w_in", 0)]),
               (scat_w2_0, [("mlp_w2", 0)]), (scat_w1_0, [("mlp_w1", 0)]), (scat_a, [("a_w_out", 0), ("a_w_in", 0)])]
    results, after = {}, dx
    for state, members in landing:
        for (k, layer), parts in zip(members, land(state, after)):
            shard_shape = (-1, *parts.shape[2:])
            results[k] = _adam_shard(f"adam_{k}_{layer}", weights[k].reshape(shard_shape),
                                     moments[k][0].reshape(shard_shape), moments[k][1].reshape(shard_shape),
                                     parts, layer=layer, prev=results.get(k))
            after = results[k][1]
    grad_out, delta_out, m_out, v_out = {}, {}, {}, {}
    for k, res in results.items():
        grad_out[k], delta_out[k], m_out[k], v_out[k] = [r.reshape(weights[k].shape) for r in res]

    small = dict(a_ln_g=(a_ln_g, m_a_ln_g, v_a_ln_g), a_ln_b=(a_ln_b, m_a_ln_b, v_a_ln_b),
                 a_w_s=(a_w_s, m_a_w_s, v_a_w_s), a_b_s=(a_b_s, m_a_b_s, v_a_b_s),
                 b_scale=(b_scale, m_b_scale, v_b_scale), norm_mix=(norm_mix, m_norm_mix, v_norm_mix),
                 norm_mlp=(norm_mlp, m_norm_mlp, v_norm_mlp), final_norm=(final_norm, m_final_norm, v_final_norm))
    small_names = list(small)
    local = dict(a_ln_g=d_ln_g, a_ln_b=d_ln_b, a_w_s=d_w_s[None], a_b_s=d_b_col.T[None], b_scale=d_scale,
                 norm_mix=jnp.concatenate([d_norm_mix0, d_norm_mix1], axis=0),
                 norm_mlp=jnp.concatenate([d_norm_mlp0, d_norm_mlp1], axis=0), final_norm=d_final.reshape(-1))
    reduced = _allreduce_small("small_grad_allreduce", _pack([local[k] for k in small_names]))
    small_grads = dict(zip(small_names, _unpack(reduced, [local[k].shape for k in small_names])))
    shard_w = b_scale.shape[1]
    small_grads["b_scale"] = lax.dynamic_slice_in_dim(small_grads["b_scale"], chip * shard_w, shard_w, axis=1)
    small_grads = {k: small_grads[k].reshape(small[k][0].shape) for k in small_names}
    packed = [_pack([small[k][i] for k in small_names]) for i in range(3)]
    res = _adam_packed("adam_small", packed[0], _pack([small_grads[k] for k in small_names]), packed[1], packed[2])
    shapes = [small[k][0].shape for k in small_names]
    for k, d, nm, nv in zip(small_names, *[_unpack(r, shapes) for r in res]):
        grad_out[k], delta_out[k], m_out[k], v_out[k] = small_grads[k], d, nm, nv

    loss = lax.psum(loss_part[0, 0], ("x", "y", "c"))
    order = ["a_w_in", "a_ln_g", "a_ln_b", "a_w_s", "a_b_s", "a_w_out", "b_w_in", "b_w_grp", "b_scale", "b_w_out",
             "norm_mix", "norm_mlp", "mlp_w1", "mlp_w2", "final_norm"]
    return (loss, dx[None], *[grad_out[k] for k in order], *[delta_out[k] for k in order],
            *[m_out[k] for k in order], *[v_out[k] for k in order])
```

```python
import math

import jax
import jax.numpy as jnp
from jax import lax
from jax.experimental import pallas as pl
from jax.experimental.pallas import tpu as pltpu

F32 = jnp.float32
BF16 = jnp.bfloat16
MESH = pl.DeviceIdType.MESH

EPS = 1e-6
B_WINDOWS = (2, 4, 8, 16)
ADAM_LR = 0.001
ADAM_B1 = 0.9
ADAM_B2 = 0.999
ADAM_EPS = 1e-08
ADAM_WD = 0.01
ADAM_STEP = 10

N_CHIPS = 4
N_DEV = 8
LANES = 128
PACK_ELEMS = 8 * LANES
VMEM_LIMIT = 56 * 1024 * 1024
ROW_TILE = 256
MM_TM, MM_TN, MM_TK = 1024, 1024, 2048


_ANY = pl.BlockSpec(memory_space=pl.ANY)


def _tile(dim, pref):
    t = min(dim, pref)
    while dim % t:
        t //= 2
    return t


def _params(*sem):
    return pltpu.CompilerParams(dimension_semantics=sem, vmem_limit_bytes=VMEM_LIMIT)


class _W4:
    def __init__(self, arr, col_sharded, shape=None):
        self.arr = arr
        self.nj, self.nl, self.r, self.c = arr.shape if shape is None else shape
        self.col = col_sharded
        self.rows = self.r if col_sharded else self.nj * self.r
        self.cols = self.nj * self.c if col_sharded else self.c

    def tile_rows(self, pref):
        return _tile(self.r, pref)

    def tile_cols(self, pref):
        return _tile(self.c, pref)

    def index(self, layer, rb, cb, tr, tc):
        if self.col:
            n = self.c // tc
            return (cb // n, layer, rb, cb % n)
        n = self.r // tr
        return (rb // n, layer, rb % n, cb)


def _mm_aw(name, a, w, *, layer=0, groups=1, transpose_w=False, extras=(), out_dtypes=(F32,), epilogue=None,
           deps=()):
    s, ka_total = a.shape
    kdim, ndim = (w.cols, w.rows) if transpose_w else (w.rows, w.cols)
    assert ka_total == groups * kdim, (name, a.shape, kdim, groups)
    span = groups == 1 and not transpose_w and not w.col and kdim <= MM_TK
    if groups > 1:
        tm = _tile(s, MM_TM)
        tn, tk = (w.tile_rows(MM_TN), w.tile_cols(512)) if transpose_w else (w.tile_cols(MM_TN), w.tile_rows(512))
    else:
        tk = kdim if span else (w.tile_cols(MM_TK) if transpose_w else w.tile_rows(MM_TK))
        tm, tn_pref = (_tile(s, 2048), 512) if tk == kdim else (_tile(s, MM_TM), MM_TN)
        tn = w.tile_rows(tn_pref) if transpose_w else w.tile_cols(tn_pref)
    nk, nn = kdim // tk, ndim // tn

    def lay(g):
        return g if groups > 1 else layer

    a_spec = pl.BlockSpec((tm, tk), lambda g, i, n, k: (i, g * nk + k))
    if span:
        w_spec = pl.BlockSpec((w.nj, None, w.r, tn), lambda g, i, n, k: (0, layer, 0, n))
    elif transpose_w:
        w_spec = pl.BlockSpec((None, None, tn, tk), lambda g, i, n, k: w.index(lay(g), n, k, tn, tk))
    else:
        w_spec = pl.BlockSpec((None, None, tk, tn), lambda g, i, n, k: w.index(lay(g), k, n, tk, tn))
    ex_specs = []
    for e in extras:
        assert e.shape[1] == groups * ndim and e.shape[0] in (1, s), (name, e.shape)
        if e.shape[0] == 1:
            ex_specs.append(pl.BlockSpec((1, tn), lambda g, i, n, k: (0, g * nn + n)))
        else:
            ex_specs.append(pl.BlockSpec((tm, tn), lambda g, i, n, k: (i, g * nn + n)))
    out_spec = pl.BlockSpec((tm, tn), lambda g, i, n, k: (i, g * nn + n))
    n_ex, n_out, n_dep = len(extras), len(out_dtypes), len(deps)

    def body(a_ref, w_ref, *rest):
        ex, outs = rest[:n_ex], rest[n_ex + n_dep:n_ex + n_dep + n_out]
        av = a_ref[...]
        if av.dtype != BF16:
            av = av.astype(BF16)
        wv = w_ref[...].reshape(tk, tn) if span else w_ref[...]
        if transpose_w:
            prod = lax.dot_general(av, wv, (((1,), (1,)), ((), ())), preferred_element_type=F32)
        else:
            prod = jnp.dot(av, wv, preferred_element_type=F32)

        def finish(total):
            vals = (total,) if epilogue is None else epilogue(total, *[e[...] for e in ex])
            for o, v in zip(outs, vals):
                o[...] = v.astype(o.dtype)

        if nk == 1:
            finish(prod)
            return
        acc, k = rest[-1], pl.program_id(3)

        @pl.when(k == 0)
        def _():
            acc[...] = prod

        @pl.when(k > 0)
        def _():
            acc[...] += prod

        @pl.when(k == nk - 1)
        def _():
            finish(acc[...])

    outs = pl.pallas_call(
        body, name=name, grid=(groups, s // tm, nn, nk),
        in_specs=[a_spec, w_spec, *ex_specs] + [_ANY] * n_dep, out_specs=[out_spec] * n_out,
        out_shape=[jax.ShapeDtypeStruct((s, groups * ndim), dt) for dt in out_dtypes],
        scratch_shapes=[pltpu.VMEM((tm, tn), F32)] if nk > 1 else [],
        compiler_params=_params("parallel", "parallel", "parallel", "arbitrary"),
    )(a, w.arr, *extras, *deps)
    return outs[0] if n_out == 1 else outs


def _mm_dw(name, a, b, like, *, layer=0, groups=1, deps=()):
    s, ka_total = a.shape
    rows, cols = ka_total // groups, b.shape[1] // groups
    assert (rows, cols) == (like.rows, like.cols) and b.shape[0] == s, (name, a.shape, b.shape)
    tm, tn, tk = like.tile_rows(MM_TM), like.tile_cols(MM_TN), _tile(s, MM_TK)
    nr, nc, nk = rows // tm, cols // tn, s // tk

    def lay(g):
        return g if groups > 1 else layer

    in_specs = [pl.BlockSpec((tk, tm), lambda g, n, i, k: (k, g * nr + i)),
                pl.BlockSpec((tk, tn), lambda g, n, i, k: (k, g * nc + n))]
    in_specs += [_ANY] * len(deps)

    def body(a_ref, b_ref, *rest):
        av, bv = a_ref[...], b_ref[...]
        if av.dtype != BF16:
            av = av.astype(BF16)
        if bv.dtype != BF16:
            bv = bv.astype(BF16)
        prod = lax.dot_general(av, bv, (((0,), (0,)), ((), ())), preferred_element_type=F32)
        if nk == 1:
            rest[-1][...] = prod
            return
        o_ref, acc, k = rest[-2], rest[-1], pl.program_id(3)

        @pl.when(k == 0)
        def _():
            acc[...] = prod

        @pl.when(k > 0)
        def _():
            acc[...] += prod

        @pl.when(k == nk - 1)
        def _():
            o_ref[...] = acc[...]

    return pl.pallas_call(
        body, name=name, grid=(groups, nc, nr, nk), in_specs=in_specs,
        out_specs=pl.BlockSpec((None, None, tm, tn), lambda g, n, i, k: like.index(lay(g), i, n, tm, tn)),
        out_shape=jax.ShapeDtypeStruct((like.nj, like.nl, like.r, like.c), F32),
        scratch_shapes=[pltpu.VMEM((tm, tn), F32)] if nk > 1 else [],
        compiler_params=_params("parallel", "parallel", "parallel", "arbitrary"),
    )(a, b, *deps)


def _row_spec(tr, d):
    return pl.BlockSpec((tr, d), lambda i: (i, 0))


def _vec_spec(d):
    return pl.BlockSpec((1, d), lambda i: (0, 0))


def _rms_fwd(name, x, g):
    s, d = x.shape
    tr = _tile(s, ROW_TILE)

    def body(x_ref, g_ref, o_ref):
        xv = x_ref[...]
        r = lax.rsqrt(jnp.mean(xv * xv, axis=-1, keepdims=True) + EPS)
        o_ref[...] = (xv * r * g_ref[...]).astype(BF16)

    return pl.pallas_call(
        body, name=name, grid=(s // tr,), in_specs=[_row_spec(tr, d), _vec_spec(d)],
        out_specs=_row_spec(tr, d), out_shape=jax.ShapeDtypeStruct((s, d), BF16),
        compiler_params=_params("parallel"),
    )(x, g)


def _rms_bwd(name, x, g, dhn, dres):
    s, d = x.shape
    tr = _tile(s, ROW_TILE)

    def body(x_ref, g_ref, dhn_ref, dres_ref, dx_ref, dxb_ref, dg_ref):
        @pl.when(pl.program_id(0) == 0)
        def _():
            dg_ref[...] = jnp.zeros_like(dg_ref)

        xv = x_ref[...]
        r = lax.rsqrt(jnp.mean(xv * xv, axis=-1, keepdims=True) + EPS)
        xh = xv * r
        dy = dhn_ref[...]
        dg_ref[...] += jnp.sum(dy * xh, axis=0, keepdims=True)
        dxh = dy * g_ref[...]
        dx = dres_ref[...] + r * (dxh - xh * jnp.mean(dxh * xh, axis=-1, keepdims=True))
        dx_ref[...] = dx
        dxb_ref[...] = dx.astype(BF16)

    return pl.pallas_call(
        body, name=name, grid=(s // tr,),
        in_specs=[_row_spec(tr, d), _vec_spec(d), _row_spec(tr, d), _row_spec(tr, d)],
        out_specs=[_row_spec(tr, d), _row_spec(tr, d), _vec_spec(d)],
        out_shape=[jax.ShapeDtypeStruct((s, d), F32), jax.ShapeDtypeStruct((s, d), BF16),
                   jax.ShapeDtypeStruct((1, d), F32)],
        compiler_params=_params("arbitrary"),
    )(x, g, dhn, dres)


def _final(name, h, g, tgt):
    s, d = h.shape
    tr = _tile(s, ROW_TILE)

    def body(h_ref, g_ref, t_ref, dh_ref, dhb_ref, dg_ref, loss_ref):
        @pl.when(pl.program_id(0) == 0)
        def _():
            dg_ref[...] = jnp.zeros_like(dg_ref)
            loss_ref[...] = jnp.zeros_like(loss_ref)

        hv = h_ref[...]
        r = lax.rsqrt(jnp.mean(hv * hv, axis=-1, keepdims=True) + EPS)
        xh = hv * r
        gv = g_ref[...]
        err = xh * gv - t_ref[...]
        part = 0.5 * jnp.sum(jnp.mean(err * err, axis=-1, keepdims=True), axis=0, keepdims=True)
        loss_ref[...] += jnp.broadcast_to(part, loss_ref.shape)
        dy = err * (1.0 / d)
        dg_ref[...] += jnp.sum(dy * xh, axis=0, keepdims=True)
        dxh = dy * gv
        dh = r * (dxh - xh * jnp.mean(dxh * xh, axis=-1, keepdims=True))
        dh_ref[...] = dh
        dhb_ref[...] = dh.astype(BF16)

    return pl.pallas_call(
        body, name=name, grid=(s // tr,),
        in_specs=[_row_spec(tr, d), _vec_spec(d), _row_spec(tr, d)],
        out_specs=[_row_spec(tr, d), _row_spec(tr, d), _vec_spec(d), _vec_spec(LANES)],
        out_shape=[jax.ShapeDtypeStruct((s, d), F32), jax.ShapeDtypeStruct((s, d), BF16),
                   jax.ShapeDtypeStruct((1, d), F32), jax.ShapeDtypeStruct((1, LANES), F32)],
        compiler_params=_params("arbitrary"),
    )(h, g, tgt)


_SQRT_HALF = 1.0 / math.sqrt(2.0)
_INV_SQRT_2PI = 1.0 / math.sqrt(2.0 * math.pi)


def _gelu(x):
    return x * (lax.erf(x * _SQRT_HALF) + 1.0) * 0.5


def _gelu_grad(x):
    return 0.5 * (lax.erf(x * _SQRT_HALF) + 1.0) + x * jnp.exp(-0.5 * x * x) * _INV_SQRT_2PI


def _causal(chunk):
    t = lax.broadcasted_iota(jnp.int32, (chunk, chunk), 0)
    sidx = lax.broadcasted_iota(jnp.int32, (chunk, chunk), 1)
    return sidx <= t


def _layernorm_parts(v, g, b):
    mu = jnp.mean(v, axis=-1, keepdims=True)
    vc = v - mu
    rs = lax.rsqrt(jnp.mean(vc * vc, axis=-1, keepdims=True) + EPS)
    vhat = vc * rs
    return vhat, rs, vhat * g + b


def _gate_fwd(name, zpre, ln_g, ln_b, w_s, b_col):
    s, aw2 = zpre.shape
    aw = aw2 // 2
    ng, chunk, _ = w_s.shape
    dh = aw // ng

    def body(z_ref, g_ref, b_ref, ws_ref, bc_ref, o_ref):
        u = _gelu(z_ref[:, :aw])
        v = _gelu(z_ref[:, aw:])
        _, _, vln = _layernorm_parts(v, g_ref[...], b_ref[...])
        mask = _causal(chunk)
        for gi in range(ng):
            sl = slice(gi * dh, (gi + 1) * dh)
            wm = jnp.where(mask, ws_ref[gi], 0.0).astype(BF16)
            sg = jnp.dot(wm, vln[:, sl].astype(BF16), preferred_element_type=F32) + bc_ref[:, gi:gi + 1]
            o_ref[:, sl] = (u[:, sl] * sg).astype(BF16)

    return pl.pallas_call(
        body, name=name, grid=(s // chunk,),
        in_specs=[_row_spec(chunk, aw2), _vec_spec(aw), _vec_spec(aw),
                  pl.BlockSpec((ng, chunk, chunk), lambda i: (0, 0, 0)),
                  pl.BlockSpec((chunk, ng), lambda i: (0, 0))],
        out_specs=_row_spec(chunk, aw), out_shape=jax.ShapeDtypeStruct((s, aw), BF16),
        compiler_params=_params("parallel"),
    )(zpre, ln_g, ln_b, w_s, b_col)


def _gate_bwd(name, zpre, dgated, ln_g, ln_b, w_s, b_col):
    s, aw2 = zpre.shape
    aw = aw2 // 2
    ng, chunk, _ = w_s.shape
    dh = aw // ng

    def body(z_ref, dgt_ref, g_ref, b_ref, ws_ref, bc_ref, dz_ref, dws_ref, dbc_ref, dlg_ref, dlb_ref,
             du_scr, dvln_scr):
        @pl.when(pl.program_id(0) == 0)
        def _():
            dws_ref[...] = jnp.zeros_like(dws_ref)
            dbc_ref[...] = jnp.zeros_like(dbc_ref)
            dlg_ref[...] = jnp.zeros_like(dlg_ref)
            dlb_ref[...] = jnp.zeros_like(dlb_ref)

        zu = z_ref[:, :aw]
        zv = z_ref[:, aw:]
        u = _gelu(zu)
        lg = g_ref[...]
        vhat, rs, vln = _layernorm_parts(_gelu(zv), lg, b_ref[...])
        mask = _causal(chunk)
        for gi in range(ng):
            sl = slice(gi * dh, (gi + 1) * dh)
            wm = jnp.where(mask, ws_ref[gi], 0.0).astype(BF16)
            vg = vln[:, sl].astype(BF16)
            sg = jnp.dot(wm, vg, preferred_element_type=F32) + bc_ref[:, gi:gi + 1]
            dgt = dgt_ref[:, sl]
            du_scr[:, sl] = dgt * sg
            ds = dgt * u[:, sl]
            dbc_ref[:, gi:gi + 1] += jnp.sum(ds, axis=-1, keepdims=True)
            dsb = ds.astype(BF16)
            dwm = lax.dot_general(dsb, vg, (((1,), (1,)), ((), ())), preferred_element_type=F32)
            dws_ref[gi] += jnp.where(mask, dwm, 0.0)
            dvln_scr[:, sl] = lax.dot_general(wm, dsb, (((0,), (0,)), ((), ())), preferred_element_type=F32)
        dvln = dvln_scr[...]
        dlb_ref[...] += jnp.sum(dvln, axis=0, keepdims=True)
        dlg_ref[...] += jnp.sum(dvln * vhat, axis=0, keepdims=True)
        dvh = dvln * lg
        dv = rs * (dvh - jnp.mean(dvh, axis=-1, keepdims=True)
                   - vhat * jnp.mean(dvh * vhat, axis=-1, keepdims=True))
        dz_ref[:, :aw] = (du_scr[...] * _gelu_grad(zu)).astype(BF16)
        dz_ref[:, aw:] = (dv * _gelu_grad(zv)).astype(BF16)

    return pl.pallas_call(
        body, name=name, grid=(s // chunk,),
        in_specs=[_row_spec(chunk, aw2), _row_spec(chunk, aw), _vec_spec(aw), _vec_spec(aw),
                  pl.BlockSpec((ng, chunk, chunk), lambda i: (0, 0, 0)),
                  pl.BlockSpec((chunk, ng), lambda i: (0, 0))],
        out_specs=[_row_spec(chunk, aw2), pl.BlockSpec((ng, chunk, chunk), lambda i: (0, 0, 0)),
                   pl.BlockSpec((chunk, ng), lambda i: (0, 0)), _vec_spec(aw), _vec_spec(aw)],
        out_shape=[jax.ShapeDtypeStruct((s, aw2), BF16), jax.ShapeDtypeStruct((ng, chunk, chunk), F32),
                   jax.ShapeDtypeStruct((chunk, ng), F32), jax.ShapeDtypeStruct((1, aw), F32),
                   jax.ShapeDtypeStruct((1, aw), F32)],
        scratch_shapes=[pltpu.VMEM((chunk, aw), F32), pltpu.VMEM((chunk, aw), F32)],
        compiler_params=_params("arbitrary"),
    )(zpre, dgated, ln_g, ln_b, w_s, b_col)


def _pool_select(g, parts):
    out = parts[-1]
    for gi in range(len(parts) - 2, -1, -1):
        out = jnp.where(g == gi, parts[gi], out)
    return out


def _pool_specs(s, bw):
    head = bw // len(B_WINDOWS)
    tc = _tile(head, 256)
    nb = head // tc
    return tc, (len(B_WINDOWS), nb), pl.BlockSpec((s, tc), lambda g, j: (0, g * nb + j))


def _pool_window(g, t):
    w = _pool_select(g, [jnp.full(t.shape, wi, jnp.int32) for wi in B_WINDOWS])
    return jnp.minimum(t + 1, w).astype(F32)


def _pool_fwd(name, vb):
    assert B_WINDOWS == (2, 4, 8, 16)
    s, bw = vb.shape
    tc, grid, spec = _pool_specs(s, bw)

    def body(v_ref, o_ref):
        g = pl.program_id(0)
        v = v_ref[...]
        t = lax.broadcasted_iota(jnp.int32, (s, tc), 0)

        def down(x, k):
            return jnp.where(t >= k, pltpu.roll(x, k, 0), 0.0)

        sums, cur, k = [], v, 1
        for _ in B_WINDOWS:
            cur = cur + down(cur, k)
            sums.append(cur)
            k *= 2
        o_ref[...] = (_pool_select(g, sums) / _pool_window(g, t) - v).astype(BF16)

    return pl.pallas_call(
        body, name=name, grid=grid, in_specs=[spec], out_specs=spec,
        out_shape=jax.ShapeDtypeStruct((s, bw), BF16), compiler_params=_params("parallel", "parallel"),
    )(vb)


def _pool_bwd(name, dpooled):
    s, bw = dpooled.shape
    tc, grid, spec = _pool_specs(s, bw)

    def body(d_ref, o_ref):
        g = pl.program_id(0)
        dp = d_ref[...]
        t = lax.broadcasted_iota(jnp.int32, (s, tc), 0)

        def up(x, k):
            return jnp.where(t < s - k, pltpu.roll(x, s - k, 0), 0.0)

        sums, cur, k = [], dp / _pool_window(g, t), 1
        for _ in B_WINDOWS:
            cur = cur + up(cur, k)
            sums.append(cur)
            k *= 2
        o_ref[...] = (_pool_select(g, sums) - dp).astype(BF16)

    return pl.pallas_call(
        body, name=name, grid=grid, in_specs=[spec], out_specs=spec,
        out_shape=jax.ShapeDtypeStruct((s, bw), BF16), compiler_params=_params("parallel", "parallel"),
    )(dpooled)


def _scale_bwd(name, dms, mixed, scale):
    s, bw = dms.shape
    tr = _tile(s, ROW_TILE)

    def body(d_ref, m_ref, sc_ref, o_ref, ds_ref):
        @pl.when(pl.program_id(0) == 0)
        def _():
            ds_ref[...] = jnp.zeros_like(ds_ref)

        dv = d_ref[...]
        ds_ref[...] += jnp.sum(dv * m_ref[...], axis=0, keepdims=True)
        o_ref[...] = (dv * sc_ref[...]).astype(BF16)

    return pl.pallas_call(
        body, name=name, grid=(s // tr,), in_specs=[_row_spec(tr, bw), _row_spec(tr, bw), _vec_spec(bw)],
        out_specs=[_row_spec(tr, bw), _vec_spec(bw)],
        out_shape=[jax.ShapeDtypeStruct((s, bw), BF16), jax.ShapeDtypeStruct((1, bw), F32)],
        compiler_params=_params("arbitrary"),
    )(dms, mixed, scale)


def _adam_update(w, g, m, v):
    m = ADAM_B1 * m + (1.0 - ADAM_B1) * g
    v = ADAM_B2 * v + (1.0 - ADAM_B2) * (g * g)
    m_hat = m / (1.0 - ADAM_B1 ** ADAM_STEP)
    v_hat = v / (1.0 - ADAM_B2 ** ADAM_STEP)
    delta = -ADAM_LR * (m_hat / (jnp.sqrt(v_hat) + ADAM_EPS) + ADAM_WD * w)
    return delta, m, v


def _adam_shard(name, w, m, v, parts, layer=0, prev=None):
    nl, r, c = w.shape
    nj, nlp = parts.shape[:2]
    tr = _tile(r, 128)
    spec = pl.BlockSpec((None, tr, c), lambda l, i: (layer + l, i, 0))

    def body(w_ref, m_ref, v_ref, p_ref, *rest):
        g_ref, d_ref, nm_ref, nv_ref = rest[-4:]
        g = p_ref[0].astype(F32)
        for j in range(1, nj):
            g = g + p_ref[j].astype(F32)
        delta, nm, nv = _adam_update(w_ref[...], g, m_ref[...], v_ref[...])
        g_ref[...] = g
        d_ref[...] = delta
        nm_ref[...] = nm
        nv_ref[...] = nv

    prev = () if prev is None else tuple(prev)
    return pl.pallas_call(
        body, name=name, grid=(nlp, r // tr),
        in_specs=[spec, spec, spec, pl.BlockSpec((nj, None, tr, c), lambda l, i: (0, l, i, 0))] + [_ANY] * len(prev),
        out_specs=[spec] * 4, out_shape=[jax.ShapeDtypeStruct(w.shape, F32)] * 4,
        input_output_aliases={4 + i: i for i in range(len(prev))},
        compiler_params=_params("parallel", "parallel"),
    )(w, m, v, parts, *prev)


def _adam_packed(name, w, g, m, v):
    rows, lanes = w.shape
    tr = _tile(rows, 512)
    spec = pl.BlockSpec((tr, lanes), lambda i: (i, 0))

    def body(w_ref, g_ref, m_ref, v_ref, d_ref, nm_ref, nv_ref):
        delta, nm, nv = _adam_update(w_ref[...], g_ref[...], m_ref[...], v_ref[...])
        d_ref[...] = delta
        nm_ref[...] = nm
        nv_ref[...] = nv

    return pl.pallas_call(
        body, name=name, grid=(rows // tr,), in_specs=[spec] * 4, out_specs=[spec] * 3,
        out_shape=[jax.ShapeDtypeStruct(w.shape, F32)] * 3, compiler_params=_params("parallel"),
    )(w, g, m, v)


def _place():
    x, y, c = lax.axis_index("x"), lax.axis_index("y"), lax.axis_index("c")
    chips = [(1 - x, y), (x, 1 - y), (1 - x, 1 - y)]
    return x, y, c, chips


def _remote(src, dst, send_sem, recv_sem, device):
    return pltpu.make_async_remote_copy(src_ref=src, dst_ref=dst, send_sem=send_sem, recv_sem=recv_sem,
                                        device_id=device, device_id_type=MESH)


def _half(ref_rows, cc):
    h = ref_rows // 2
    return pl.ds(cc * h, h)


_HBM = pl.BlockSpec(memory_space=pltpu.HBM)
_SEM = pl.BlockSpec(memory_space=pltpu.SEMAPHORE)
_EFFECT = pltpu.SideEffectType.DATAFLOW_SIDE_EFFECTING


def _in_hbm(arr):
    return pltpu.with_memory_space_constraint(arr, pltpu.HBM)


def _gather_block(land, shard_rows, split, j, cc):
    if split:
        return land.at[j, :, _half(shard_rows, cc), :]
    return land.at[j]


def _gather_copy(src, land, split, send_sem, recv_sem, x, y, c, chip):
    qx, qy = chip
    rows = src.shape[1]
    part = src.at[:, _half(rows, c), :] if split else src
    return _remote(part, _gather_block(land, rows, split, 2 * x + y, c), send_sem, recv_sem, (qx, qy, c))


def _gather_start(name, shards, split):
    n = len(shards)

    def body(*refs):
        srcs, lands, send, recv = refs[:n], refs[n:2 * n], refs[2 * n:3 * n], refs[3 * n:4 * n]
        token = refs[-1]
        x, y, c, chips = _place()
        for a in range(n):
            for qi in range(3):
                _gather_copy(srcs[a], lands[a], split[a], send[a], recv[a], x, y, c, chips[qi]).start()
        token[...] = jnp.zeros_like(token)

    lands = [lax.empty((N_CHIPS, *sh.shape), sh.dtype) for sh in shards]
    outs = pl.pallas_call(
        body, name=name, in_specs=[_HBM] * (2 * n),
        out_specs=[_SEM] * (2 * n) + [_HBM] * (2 * n) + [pl.BlockSpec(memory_space=pltpu.VMEM)],
        out_shape=[pltpu.SemaphoreType.DMA(())] * (2 * n)
        + [pltpu.HBM(sh.shape, sh.dtype) for sh in shards] + [pltpu.HBM(ld.shape, ld.dtype) for ld in lands]
        + [jax.ShapeDtypeStruct((8, LANES), F32)],
        input_output_aliases={i: 2 * n + i for i in range(2 * n)},
        compiler_params=pltpu.CompilerParams(has_side_effects=_EFFECT),
    )(*[_in_hbm(s) for s in shards], *[_in_hbm(ld) for ld in lands])
    return list(outs[:n]), list(outs[n:2 * n]), list(outs[2 * n:3 * n]), list(outs[3 * n:4 * n]), outs[-1]


def _gather_wait(name, split, send, recv, shards, lands, after):
    n = len(shards)

    def body(*refs):
        lds, snd, rcv = refs[n:2 * n], refs[2 * n:3 * n], refs[3 * n:4 * n]
        x, y, c, _ = _place()
        for k in range(n):
            three = lds[k].at[pl.ds(0, 3)]
            if split[k]:
                three = three.at[:, :, _half(lds[k].shape[2], c), :]
            all_three = _remote(three, three, snd[k], rcv[k], (x, y, c))
            all_three.wait_send()
            all_three.wait_recv()

    outs = pl.pallas_call(
        body, name=name, in_specs=[_HBM] * (2 * n) + [_SEM] * (2 * n) + [_ANY], out_specs=[_HBM] * (2 * n),
        out_shape=[pltpu.HBM(s.shape, s.dtype) for s in shards] + [pltpu.HBM(ld.shape, ld.dtype) for ld in lands],
        input_output_aliases={i: i for i in range(2 * n)},
        compiler_params=pltpu.CompilerParams(has_side_effects=_EFFECT),
    )(*shards, *lands, *send, *recv, after)
    return list(outs[:n]), list(outs[n:])


def _fill_own(name, src, dest, dest_shape, place, src_slot=False, half_rows=False):
    nl, r, c = src.shape[-3:]
    tr = _tile(r, 512)
    nb = r // tr

    def src_map(l, i, p):
        return (p[0], l, i, 0) if src_slot else (l, i, 0)

    def dst_map(l, i, p):
        return (p[0], l, (p[1] * nb if half_rows else 0) + i, 0)

    def body(p_ref, s_ref, *rest):
        rest[-1][...] = s_ref[...]

    in_specs = [pl.BlockSpec((None, None, tr, c) if src_slot else (None, tr, c), src_map)]
    operands, aliases = [src], {}
    if dest is not None:
        in_specs.append(_ANY)
        operands.append(dest)
        aliases = {2: 0}
    return pl.pallas_call(
        body, name=name,
        grid_spec=pltpu.PrefetchScalarGridSpec(
            num_scalar_prefetch=1, grid=(nl, nb), in_specs=in_specs,
            out_specs=pl.BlockSpec((None, None, tr, c), dst_map)),
        out_shape=jax.ShapeDtypeStruct(dest_shape, src.dtype), input_output_aliases=aliases,
        compiler_params=_params("parallel", "parallel"),
    )(place, *operands)


def _gather_finish(name, lands):
    n = len(lands)

    def body(*refs):
        outs = refs[n:2 * n]
        fsend, frecv = refs[2 * n:]
        x, y, c, chips = _place()
        sib = (x, y, 1 - c)

        def relay(a, qi, cc):
            qx, qy = chips[qi]
            blk = _gather_block(outs[a], outs[a].shape[2], True, 2 * qx + qy, cc)
            return _remote(blk, blk, fsend.at[a, qi], frecv.at[a, qi], sib)

        relays = [relay(a, qi, c) for a in range(n) for qi in range(3)]
        for cp in relays:
            cp.start()
        for a in range(n):
            for qi in range(3):
                relay(a, qi, 1 - c).wait_recv()
        for cp in relays:
            cp.wait_send()

    outs = pl.pallas_call(
        body, name=name, in_specs=[_ANY] * n, out_specs=[_ANY] * n,
        out_shape=[jax.ShapeDtypeStruct(ld.shape, ld.dtype) for ld in lands],
        input_output_aliases={i: i for i in range(n)},
        scratch_shapes=[pltpu.SemaphoreType.DMA((n, 3))] * 2,
    )(*lands)
    return list(outs)


def _split_start(name, srcs, lands, plan):
    n = len(srcs)

    def body(*refs):
        s, ld, send, recv, token = refs[:n], refs[n:2 * n], refs[2 * n:3 * n], refs[3 * n:4 * n], refs[-1]
        for a, copies in enumerate(plan(s, ld)):
            for src, dst, peer in copies:
                _remote(src, dst, send[a], recv[a], peer).start()
        token[...] = jnp.zeros_like(token)

    both = list(srcs) + list(lands)
    outs = pl.pallas_call(
        body, name=name, in_specs=[_HBM] * (2 * n),
        out_specs=[_SEM] * (2 * n) + [_HBM] * (2 * n) + [pl.BlockSpec(memory_space=pltpu.VMEM)],
        out_shape=[pltpu.SemaphoreType.DMA(())] * (2 * n) + [pltpu.HBM(b.shape, b.dtype) for b in both]
        + [jax.ShapeDtypeStruct((8, LANES), F32)],
        input_output_aliases={i: 2 * n + i for i in range(2 * n)},
        compiler_params=pltpu.CompilerParams(has_side_effects=_EFFECT),
    )(*[_in_hbm(b) for b in both])
    return list(outs[:n]), list(outs[n:2 * n]), list(outs[2 * n:3 * n]), list(outs[3 * n:4 * n]), outs[-1]


def _split_wait(name, srcs, lands, send, recv, after, whole):
    n = len(srcs)

    def body(*refs):
        ld, snd, rcv = refs[n:2 * n], refs[2 * n:3 * n], refs[3 * n:4 * n]
        x, y, c, _ = _place()
        for a, blk in enumerate(whole(ld)):
            every = _remote(blk, blk, snd[a], rcv[a], (x, y, c))
            every.wait_send()
            every.wait_recv()

    both = list(srcs) + list(lands)
    outs = pl.pallas_call(
        body, name=name, in_specs=[_HBM] * (2 * n) + [_SEM] * (2 * n) + [_ANY], out_specs=[_HBM] * (2 * n),
        out_shape=[pltpu.HBM(b.shape, b.dtype) for b in both],
        input_output_aliases={i: i for i in range(2 * n)},
        compiler_params=pltpu.CompilerParams(has_side_effects=_EFFECT),
    )(*both, *send, *recv, after)
    return list(outs[:n]), list(outs[n:])


def _pair_plan(srcs, lands):
    x, y, c, _ = _place()
    return [[(s.at[:, :, _half(s.shape[2], 1 - c), :], ld, (x, y, 1 - c))] for s, ld in zip(srcs, lands)]


def _pair_whole(lands):
    return list(lands)


def _scatter_plan(srcs, lands):
    x, y, c, chips = _place()
    return [[(s.at[2 * qx + qy], ld.at[2 * x + y, :, _half(ld.shape[2], c), :], (qx, qy, c)) for qx, qy in chips]
            for s, ld in zip(srcs, lands)]


def _scatter_whole(lands):
    _, _, c, _ = _place()
    return [ld.at[pl.ds(0, 3), :, _half(ld.shape[2], c), :] for ld in lands]


def _pair_sum(name, grad, other, place):
    nj, nl, r, c = grad.shape
    h = r // 2
    tr = _tile(h, 256)
    nb = h // tr

    def body(p_ref, g_ref, o_ref, q_ref, d_ref):
        q = (g_ref[...] + o_ref[...]).astype(BF16)
        q_ref[...] = q

        @pl.when(pl.program_id(2) == p_ref[0])
        def _():
            d_ref[...] = q

    blk = (None, None, tr, c)
    return pl.pallas_call(
        body, name=name,
        grid_spec=pltpu.PrefetchScalarGridSpec(
            num_scalar_prefetch=1, grid=(nl, nb, nj),
            in_specs=[pl.BlockSpec(blk, lambda l, i, j, p: (j, l, p[1] * nb + i, 0)),
                      pl.BlockSpec(blk, lambda l, i, j, p: (j, l, i, 0))],
            out_specs=[pl.BlockSpec(blk, lambda l, i, j, p: (j, l, i, 0)),
                       pl.BlockSpec(blk, lambda l, i, j, p: (p[0], l, p[1] * nb + i, 0))]),
        out_shape=[jax.ShapeDtypeStruct((nj, nl, h, c), BF16), jax.ShapeDtypeStruct((nj, nl, r, c), BF16)],
        compiler_params=_params("parallel", "parallel", "arbitrary"),
    )(place, grad, other)


def _scatter_finish(name, dests):
    n = len(dests)

    def body(*refs):
        outs = refs[n:2 * n]
        fsend, frecv = refs[2 * n:]
        x, y, c, _ = _place()
        sib = (x, y, 1 - c)

        def relay(a, j, cc):
            blk = outs[a].at[j, :, _half(outs[a].shape[2], cc), :]
            return _remote(blk, blk, fsend.at[a, j], frecv.at[a, j], sib)

        relays = [relay(a, j, c) for a in range(n) for j in range(N_CHIPS)]
        for cp in relays:
            cp.start()
        for a in range(n):
            for j in range(N_CHIPS):
                relay(a, j, 1 - c).wait_recv()
        for cp in relays:
            cp.wait_send()

    outs = pl.pallas_call(
        body, name=name, in_specs=[_ANY] * n, out_specs=[_ANY] * n,
        out_shape=[jax.ShapeDtypeStruct(d.shape, d.dtype) for d in dests],
        input_output_aliases={i: i for i in range(n)},
        scratch_shapes=[pltpu.SemaphoreType.DMA((n, N_CHIPS))] * 2,
    )(*dests)
    return list(outs)


def _allreduce_small(name, packed):
    rows, lanes = packed.shape

    def body(x_ref, o_ref, all_ref, send, recv, lsem):
        x, y, c, chips = _place()
        me, sib = (x, y, c), (x, y, 1 - c)

        def slot(px, py, pc):
            return all_ref.at[pl.ds((4 * px + 2 * py + pc) * rows, rows), :]

        def copy(k, blk, to, src=None):
            return _remote(slot(*blk) if src is None else src, slot(*blk), send.at[k], recv.at[k], to)

        mine = pltpu.make_async_copy(x_ref, slot(*me), lsem)
        mine.start()
        first = [copy(0, me, sib, src=x_ref)] + [copy(1 + j, me, (*chip, c), src=x_ref) for j, chip in enumerate(chips)]
        for cp in first:
            cp.start()
        passed = [copy(4 + j, (*chip, c), sib) for j, chip in enumerate(chips)]
        for j, chip in enumerate(chips):
            copy(1 + j, (*chip, c), me).wait_recv()
            passed[j].start()
        copy(0, sib, me).wait_recv()
        for j, chip in enumerate(chips):
            copy(4 + j, (*chip, 1 - c), me).wait_recv()
        for cp in first + passed:
            cp.wait_send()
        mine.wait()
        total = all_ref[pl.ds(0, rows), :]
        for d in range(1, N_DEV):
            total = total + all_ref[pl.ds(d * rows, rows), :]
        o_ref[...] = total

    vmem = pl.BlockSpec(memory_space=pltpu.VMEM)
    return pl.pallas_call(
        body, name=name, in_specs=[vmem], out_specs=vmem, out_shape=jax.ShapeDtypeStruct(packed.shape, F32),
        scratch_shapes=[pltpu.VMEM((N_DEV * rows, lanes), F32), pltpu.SemaphoreType.DMA((7,)),
                        pltpu.SemaphoreType.DMA((7,)), pltpu.SemaphoreType.DMA],
        compiler_params=pltpu.CompilerParams(vmem_limit_bytes=VMEM_LIMIT),
    )(packed)


def _pack(parts):
    rows = []
    for p in parts:
        flat = p.reshape(-1)
        pad = (-flat.shape[0]) % PACK_ELEMS
        rows.append(jnp.pad(flat, (0, pad)).reshape(-1, LANES))
    return jnp.concatenate(rows, axis=0)


def _unpack(packed, shapes):
    out, row = [], 0
    for sh in shapes:
        size = math.prod(sh)
        nrows = -(-size // PACK_ELEMS) * (PACK_ELEMS // LANES)
        out.append(packed[row:row + nrows].reshape(-1)[:size].reshape(sh))
        row += nrows
    return out


def kernel(x, a_w_in, a_ln_g, a_ln_b, a_w_s, a_b_s, a_w_out, b_w_in, b_w_grp, b_scale, b_w_out, norm_mix, norm_mlp, mlp_w1, mlp_w2, final_norm, loss_target, m_a_w_in, m_a_ln_g, m_a_ln_b, m_a_w_s, m_a_b_s, m_a_w_out, m_b_w_in, m_b_w_grp, m_b_scale, m_b_w_out, m_norm_mix, m_norm_mlp, m_mlp_w1, m_mlp_w2, m_final_norm, v_a_w_in, v_a_ln_g, v_a_ln_b, v_a_w_s, v_a_b_s, v_a_w_out, v_b_w_in, v_b_w_grp, v_b_scale, v_b_w_out, v_norm_mix, v_norm_mlp, v_mlp_w1, v_mlp_w2, v_final_norm):
    xi, yi, ci = lax.axis_index("x"), lax.axis_index("y"), lax.axis_index("c")
    chip = 2 * xi + yi
    place = jnp.stack([chip, ci]).astype(jnp.int32)
    x2, tgt = x[0], loss_target[0]
    bw = b_scale.shape[1] * N_CHIPS

    units = dict(a_w_in=a_w_in, a_w_out=a_w_out, w1_0=mlp_w1[0:1], w2_0=mlp_w2[0:1], b_scale=b_scale.reshape(1, 1, -1),
                 b_w_in=b_w_in, b_w_grp=b_w_grp[0], b_w_out=b_w_out, w1_1=mlp_w1[1:2], w2_1=mlp_w2[1:2])
    unit_names = list(units)
    split = [k != "b_scale" for k in unit_names]
    send, recv, shards_t, lands_t, token = _gather_start(
        "gather_start", [units[k].astype(BF16) if s else units[k] for k, s in zip(unit_names, split)], split)
    col_sharded = dict(a_w_in=True, a_w_out=False, b_w_in=False, b_w_grp=False, b_w_out=False,
                       w1_0=True, w2_0=False, w1_1=True, w2_1=False)
    W = {}

    def arrive(keys, after):
        ids = [unit_names.index(k) for k in keys]
        sp = [split[i] for i in ids]
        sh, ld = _gather_wait(f"gather_wait_{keys[0]}", sp, [send[i] for i in ids], [recv[i] for i in ids],
                              [shards_t[i] for i in ids], [lands_t[i] for i in ids], after)
        ld = [_fill_own(f"gather_own_{k}", s, l, l.shape, place) for k, s, l in zip(keys, sh, ld)]
        relayed = iter(_gather_finish(f"gather_finish_{keys[0]}", [l for l, s in zip(ld, sp) if s]))
        for k, l, s in zip(keys, ld, sp):
            full = next(relayed) if s else l
            W[k] = _W4(full, col_sharded[k]) if k in col_sharded else full

    b_col = a_b_s[0].T

    def residual(acc, res):
        return (res + acc,)

    def sq_relu(acc):
        act = jnp.maximum(acc, 0.0)
        return act, act * act

    def mlp_fwd(tag, h, layer):
        hn = _rms_fwd(f"mlp{tag}_norm", h, norm_mlp[layer:layer + 1])
        arrive([f"w1_{layer}"], hn)
        act, act_sq = _mm_aw(f"mlp{tag}_up", hn, W[f"w1_{layer}"], out_dtypes=(BF16, BF16), epilogue=sq_relu)
        arrive([f"w2_{layer}"], act_sq)
        out = _mm_aw(f"mlp{tag}_down", act_sq, W[f"w2_{layer}"], extras=(h,), epilogue=residual)
        return out, (h, hn, act, act_sq)

    hn0 = _rms_fwd("mix_a_norm", x2, norm_mix[0:1])
    arrive(["a_w_in"], token)
    zpre = _mm_aw("mix_a_in", hn0, W["a_w_in"])
    gated = _gate_fwd("mix_a_gate", zpre, a_ln_g, a_ln_b, a_w_s[0], b_col)
    arrive(["a_w_out"], gated)
    h1 = _mm_aw("mix_a_out", gated, W["a_w_out"], extras=(x2,), epilogue=residual)
    h2, mlp0 = mlp_fwd("0", h1, 0)
    hn2 = _rms_fwd("mix_b_norm", h2, norm_mix[1:2])
    arrive(["b_scale", "b_w_in"], hn2)
    scale_full = W["b_scale"].reshape(1, bw)
    vb = _mm_aw("mix_b_in", hn2, W["b_w_in"])
    pooled = _pool_fwd("mix_b_pool", vb)
    arrive(["b_w_grp", "b_w_out"], pooled)
    mixed, ms = _mm_aw("mix_b_grp", pooled, W["b_w_grp"], groups=len(B_WINDOWS), extras=(scale_full,),
                       out_dtypes=(F32, BF16), epilogue=lambda acc, sc: (acc, acc * sc))
    h3 = _mm_aw("mix_b_out", ms, W["b_w_out"], extras=(h2,), epilogue=residual)
    h4, mlp1 = mlp_fwd("1", h3, 1)
    dh4, dh4_b, d_final, loss_part = _final("loss_head", h4, final_norm.reshape(1, -1), tgt)
    g1_like = _W4(None, True, shape=(N_CHIPS, 1, *W["w1_0"].arr.shape[2:]))
    g2_like = _W4(None, False, shape=(N_CHIPS, 1, *W["w2_0"].arr.shape[2:]))

    def exchange(tag, gs):
        zones = [lax.empty((g.shape[0], g.shape[1], g.shape[2] // 2, g.shape[3]), g.dtype) for g in gs]
        send, recv, srcs, zones, tok = _split_start(f"pair_start_{tag}", gs, zones, _pair_plan)
        return (tag, send, recv, srcs, zones), tok

    def reduce(state, after):
        tag, send, recv, srcs, zones = state
        srcs, zones = _split_wait(f"pair_wait_{tag}", srcs, zones, send, recv, after, _pair_whole)
        both = [_pair_sum(f"pair_sum_{tag}_{i}", g, o, place) for i, (g, o) in enumerate(zip(srcs, zones))]
        send, recv, sums, dests, tok = _split_start(f"scatter_start_{tag}", [b[0] for b in both],
                                                    [b[1] for b in both], _scatter_plan)
        return (tag, send, recv, sums, dests), tok

    def land(state, after):
        tag, send, recv, sums, dests = state
        _, dests = _split_wait(f"scatter_wait_{tag}", sums, dests, send, recv, after, _scatter_whole)
        return _scatter_finish(f"scatter_finish_{tag}", dests)

    def mlp_bwd(tag, dh, dh_b, saved, layer, deps, pending=None):
        h, hn, act, act_sq = saved
        dpre = _mm_aw(f"mlp{tag}_down_dx", dh_b, W[f"w2_{layer}"], transpose_w=True, extras=(act,),
                      out_dtypes=(BF16,), epilogue=lambda acc, a: (acc * (2.0 * a),), deps=deps)
        scattering, dw_deps = None, ()
        if pending is not None:
            scattering, tok = reduce(pending, dpre)
            dw_deps = (tok,)
        g_w2 = _mm_dw(f"mlp{tag}_down_dw", act_sq, dh_b, g2_like, deps=dw_deps)
        pair_w2, tok = exchange(f"w2_{layer}", [g_w2])
        dhn = _mm_aw(f"mlp{tag}_up_dx", dpre, W[f"w1_{layer}"], transpose_w=True, deps=(tok,))
        g_w1 = _mm_dw(f"mlp{tag}_up_dw", hn, dpre, g1_like)
        pair_w1, tok1 = exchange(f"w1_{layer}", [g_w1])
        scat_w2, tok2 = reduce(pair_w2, dhn)
        dh_in, dh_in_b, d_norm = _rms_bwd(f"mlp{tag}_norm_bwd", h, norm_mlp[layer:layer + 1], dhn, dh)
        return dh_in, dh_in_b, d_norm, pair_w1, scat_w2, (tok1, tok2), scattering

    dh3, dh3_b, d_norm_mlp1, pair_w1_1, scat_w2_1, toks, _ = mlp_bwd("1", dh4, dh4_b, mlp1, 1, ())
    dms = _mm_aw("mix_b_out_dx", dh3_b, W["b_w_out"], transpose_w=True, deps=toks)
    g_b_out = _mm_dw("mix_b_out_dw", ms, dh3_b, W["b_w_out"])
    scat_w1_1, tok = reduce(pair_w1_1, dms)
    dmixed, d_scale = _scale_bwd("mix_b_scale_bwd", dms, mixed, scale_full)
    dpooled = _mm_aw("mix_b_grp_dx", dmixed, W["b_w_grp"], groups=len(B_WINDOWS), transpose_w=True, deps=(tok,))
    g_b_grp = _mm_dw("mix_b_grp_dw", pooled, dmixed, W["b_w_grp"], groups=len(B_WINDOWS))
    dvb = _pool_bwd("mix_b_pool_bwd", dpooled)
    dhn2 = _mm_aw("mix_b_in_dx", dvb, W["b_w_in"], transpose_w=True)
    g_b_in = _mm_dw("mix_b_in_dw", hn2, dvb, W["b_w_in"])
    pair_b, tok = exchange("b", [g_b_out, g_b_grp, g_b_in])
    dh2, dh2_b, d_norm_mix1 = _rms_bwd("mix_b_norm_bwd", h2, norm_mix[1:2], dhn2, dh3)
    dh1, dh1_b, d_norm_mlp0, pair_w1_0, scat_w2_0, toks, scat_b = mlp_bwd("0", dh2, dh2_b, mlp0, 0, (tok,),
                                                                          pending=pair_b)
    dgated = _mm_aw("mix_a_out_dx", dh1_b, W["a_w_out"], transpose_w=True, deps=toks)
    g_a_out = _mm_dw("mix_a_out_dw", gated, dh1_b, W["a_w_out"])
    scat_w1_0, tok = reduce(pair_w1_0, dgated)
    dzpre, d_w_s, d_b_col, d_ln_g, d_ln_b = _gate_bwd("mix_a_gate_bwd", zpre, dgated, a_ln_g, a_ln_b, a_w_s[0], b_col)
    dhn0 = _mm_aw("mix_a_in_dx", dzpre, W["a_w_in"], transpose_w=True, deps=(tok,))
    g_a_in = _mm_dw("mix_a_in_dw", hn0, dzpre, W["a_w_in"])
    pair_a, tok = exchange("a", [g_a_out, g_a_in])
    dx, _, d_norm_mix0 = _rms_bwd("mix_a_norm_bwd", x2, norm_mix[0:1], dhn0, dh1)
    scat_a, _ = reduce(pair_a, dx)

    moments = dict(a_w_in=(m_a_w_in, v_a_w_in), a_w_out=(m_a_w_out, v_a_w_out), b_w_in=(m_b_w_in, v_b_w_in),
                   b_w_grp=(m_b_w_grp, v_b_w_grp), b_w_out=(m_b_w_out, v_b_w_out),
                   mlp_w1=(m_mlp_w1, v_mlp_w1), mlp_w2=(m_mlp_w2, v_mlp_w2))
    weights = dict(a_w_in=a_w_in, a_w_out=a_w_out, b_w_in=b_w_in, b_w_grp=b_w_grp, b_w_out=b_w_out,
                   mlp_w1=mlp_w1, mlp_w2=mlp_w2)
    landing = [(scat_w2_1, [("mlp_w2", 1)]), (scat_w1_1, [("mlp_w1", 1)]),
               (scat_b, [("b_w_out", 0), ("b_w_grp", 0), ("b_w_in", 0)]),
               (scat_w2_0, [("mlp_w2", 0)]), (scat_w1_0, [("mlp_w1", 0)]), (scat_a, [("a_w_out", 0), ("a_w_in", 0)])]
    results, after = {}, dx
    for state, members in landing:
        for (k, layer), parts in zip(members, land(state, after)):
            shard_shape = (-1, *parts.shape[2:])
            results[k] = _adam_shard(f"adam_{k}_{layer}", weights[k].reshape(shard_shape),
                                     moments[k][0].reshape(shard_shape), moments[k][1].reshape(shard_shape),
                                     parts, layer=layer, prev=results.get(k))
            after = results[k][1]
    grad_out, delta_out, m_out, v_out = {}, {}, {}, {}
    for k, res in results.items():
        grad_out[k], delta_out[k], m_out[k], v_out[k] = [r.reshape(weights[k].shape) for r in res]

    small = dict(a_ln_g=(a_ln_g, m_a_ln_g, v_a_ln_g), a_ln_b=(a_ln_b, m_a_ln_b, v_a_ln_b),
                 a_w_s=(a_w_s, m_a_w_s, v_a_w_s), a_b_s=(a_b_s, m_a_b_s, v_a_b_s),
                 b_scale=(b_scale, m_b_scale, v_b_scale), norm_mix=(norm_mix, m_norm_mix, v_norm_mix),
                 norm_mlp=(norm_mlp, m_norm_mlp, v_norm_mlp), final_norm=(final_norm, m_final_norm, v_final_norm))
    small_names = list(small)
    local = dict(a_ln_g=d_ln_g, a_ln_b=d_ln_b, a_w_s=d_w_s[None], a_b_s=d_b_col.T[None], b_scale=d_scale,
                 norm_mix=jnp.concatenate([d_norm_mix0, d_norm_mix1], axis=0),
                 norm_mlp=jnp.concatenate([d_norm_mlp0, d_norm_mlp1], axis=0), final_norm=d_final.reshape(-1))
    reduced = _allreduce_small("small_grad_allreduce", _pack([local[k] for k in small_names]))
    small_grads = dict(zip(small_names, _unpack(reduced, [local[k].shape for k in small_names])))
    shard_w = b_scale.shape[1]
    small_grads["b_scale"] = lax.dynamic_slice_in_dim(small_grads["b_scale"], chip * shard_w, shard_w, axis=1)
    small_grads = {k: small_grads[k].reshape(small[k][0].shape) for k in small_names}
    packed = [_pack([small[k][i] for k in small_names]) for i in range(3)]
    res = _adam_packed("adam_small", packed[0], _pack([small_grads[k] for k in small_names]), packed[1], packed[2])
    shapes = [small[k][0].shape for k in small_names]
    for k, d, nm, nv in zip(small_names, *[_unpack(r, shapes) for r in res]):
        grad_out[k], delta_out[k], m_out[k], v_out[k] = small_grads[k], d, nm, nv

    loss = lax.psum(loss_part[0, 0], ("x", "y", "c"))
    order = ["a_w_in", "a_ln_g", "a_ln_b", "a_w_s", "a_b_s", "a_w_out", "b_w_in", "b_w_grp", "b_scale", "b_w_out",
             "norm_mix", "norm_mlp", "mlp_w1", "mlp_w2", "final_norm"]
    return (loss, dx[None], *[grad_out[k] for k in order], *[delta_out[k] for k in order],
            *[m_out[k] for k in order], *[v_out[k] for k in order])
```

```python
import math

import jax
import jax.numpy as jnp
from jax import lax
from jax.experimental import pallas as pl
from jax.experimental.pallas import tpu as pltpu

F32 = jnp.float32
BF16 = jnp.bfloat16
MESH = pl.DeviceIdType.MESH

EPS = 1e-6
B_WINDOWS = (2, 4, 8, 16)
ADAM_LR = 0.001
ADAM_B1 = 0.9
ADAM_B2 = 0.999
ADAM_EPS = 1e-08
ADAM_WD = 0.01
ADAM_STEP = 10

N_CHIPS = 4
N_DEV = 8
LANES = 128
PACK_ELEMS = 8 * LANES
VMEM_LIMIT = 56 * 1024 * 1024
ROW_TILE = 256
MM_TM, MM_TN, MM_TK = 1024, 1024, 2048


_ANY = pl.BlockSpec(memory_space=pl.ANY)


def _tile(dim, pref):
    t = min(dim, pref)
    while dim % t:
        t //= 2
    return t


def _params(*sem):
    return pltpu.CompilerParams(dimension_semantics=sem, vmem_limit_bytes=VMEM_LIMIT)


class _W4:
    def __init__(self, arr, col_sharded, shape=None):
        self.arr = arr
        self.nj, self.nl, self.r, self.c = arr.shape if shape is None else shape
        self.col = col_sharded
        self.rows = self.r if col_sharded else self.nj * self.r
        self.cols = self.nj * self.c if col_sharded else self.c

    def tile_rows(self, pref):
        return _tile(self.r, pref)

    def tile_cols(self, pref):
        return _tile(self.c, pref)

    def index(self, layer, rb, cb, tr, tc):
        if self.col:
            n = self.c // tc
            return (cb // n, layer, rb, cb % n)
        n = self.r // tr
        return (rb // n, layer, rb % n, cb)


def _mm_aw(name, a, w, *, layer=0, groups=1, transpose_w=False, extras=(), out_dtypes=(F32,), epilogue=None,
           deps=()):
    s, ka_total = a.shape
    kdim, ndim = (w.cols, w.rows) if transpose_w else (w.rows, w.cols)
    assert ka_total == groups * kdim, (name, a.shape, kdim, groups)
    span = groups == 1 and not transpose_w and not w.col and kdim <= MM_TK
    if groups > 1:
        tm = _tile(s, MM_TM)
        tn, tk = (w.tile_rows(MM_TN), w.tile_cols(512)) if transpose_w else (w.tile_cols(MM_TN), w.tile_rows(512))
    else:
        tk = kdim if span else (w.tile_cols(MM_TK) if transpose_w else w.tile_rows(MM_TK))
        tm, tn_pref = (_tile(s, 2048), 512) if tk == kdim else (_tile(s, MM_TM), MM_TN)
        tn = w.tile_rows(tn_pref) if transpose_w else w.tile_cols(tn_pref)
    nk, nn = kdim // tk, ndim // tn

    def lay(g):
        return g if groups > 1 else layer

    a_spec = pl.BlockSpec((tm, tk), lambda g, i, n, k: (i, g * nk + k))
    if span:
        w_spec = pl.BlockSpec((w.nj, None, w.r, tn), lambda g, i, n, k: (0, layer, 0, n))
    elif transpose_w:
        w_spec = pl.BlockSpec((None, None, tn, tk), lambda g, i, n, k: w.index(lay(g), n, k, tn, tk))
    else:
        w_spec = pl.BlockSpec((None, None, tk, tn), lambda g, i, n, k: w.index(lay(g), k, n, tk, tn))
    ex_specs = []
    for e in extras:
        assert e.shape[1] == groups * ndim and e.shape[0] in (1, s), (name, e.shape)
        if e.shape[0] == 1:
            ex_specs.append(pl.BlockSpec((1, tn), lambda g, i, n, k: (0, g * nn + n)))
        else:
            ex_specs.append(pl.BlockSpec((tm, tn), lambda g, i, n, k: (i, g * nn + n)))
    out_spec = pl.BlockSpec((tm, tn), lambda g, i, n, k: (i, g * nn + n))
    n_ex, n_out, n_dep = len(extras), len(out_dtypes), len(deps)

    def body(a_ref, w_ref, *rest):
        ex, outs = rest[:n_ex], rest[n_ex + n_dep:n_ex + n_dep + n_out]
        av = a_ref[...]
        if av.dtype != BF16:
            av = av.astype(BF16)
        wv = w_ref[...].reshape(tk, tn) if span else w_ref[...]
        if transpose_w:
            prod = lax.dot_general(av, wv, (((1,), (1,)), ((), ())), preferred_element_type=F32)
        else:
            prod = jnp.dot(av, wv, preferred_element_type=F32)

        def finish(total):
            vals = (total,) if epilogue is None else epilogue(total, *[e[...] for e in ex])
            for o, v in zip(outs, vals):
                o[...] = v.astype(o.dtype)

        if nk == 1:
            finish(prod)
            return
        acc, k = rest[-1], pl.program_id(3)

        @pl.when(k == 0)
        def _():
            acc[...] = prod

        @pl.when(k > 0)
        def _():
            acc[...] += prod

        @pl.when(k == nk - 1)
        def _():
            finish(acc[...])

    outs = pl.pallas_call(
        body, name=name, grid=(groups, s // tm, nn, nk),
        in_specs=[a_spec, w_spec, *ex_specs] + [_ANY] * n_dep, out_specs=[out_spec] * n_out,
        out_shape=[jax.ShapeDtypeStruct((s, groups * ndim), dt) for dt in out_dtypes],
        scratch_shapes=[pltpu.VMEM((tm, tn), F32)] if nk > 1 else [],
        compiler_params=_params("parallel", "parallel", "parallel", "arbitrary"),
    )(a, w.arr, *extras, *deps)
    return outs[0] if n_out == 1 else outs


def _mm_dw(name, a, b, like, *, layer=0, groups=1, deps=()):
    s, ka_total = a.shape
    rows, cols = ka_total // groups, b.shape[1] // groups
    assert (rows, cols) == (like.rows, like.cols) and b.shape[0] == s, (name, a.shape, b.shape)
    tm, tn, tk = like.tile_rows(MM_TM), like.tile_cols(MM_TN), _tile(s, MM_TK)
    nr, nc, nk = rows // tm, cols // tn, s // tk

    def lay(g):
        return g if groups > 1 else layer

    in_specs = [pl.BlockSpec((tk, tm), lambda g, n, i, k: (k, g * nr + i)),
                pl.BlockSpec((tk, tn), lambda g, n, i, k: (k, g * nc + n))]
    in_specs += [_ANY] * len(deps)

    def body(a_ref, b_ref, *rest):
        av, bv = a_ref[...], b_ref[...]
        if av.dtype != BF16:
            av = av.astype(BF16)
        if bv.dtype != BF16:
            bv = bv.astype(BF16)
        prod = lax.dot_general(av, bv, (((0,), (0,)), ((), ())), preferred_element_type=F32)
        if nk == 1:
            rest[-1][...] = prod
            return
        o_ref, acc, k = rest[-2], rest[-1], pl.program_id(3)

        @pl.when(k == 0)
        def _():
            acc[...] = prod

        @pl.when(k > 0)
        def _():
            acc[...] += prod

        @pl.when(k == nk - 1)
        def _():
            o_ref[...] = acc[...]

    return pl.pallas_call(
        body, name=name, grid=(groups, nc, nr, nk), in_specs=in_specs,
        out_specs=pl.BlockSpec((None, None, tm, tn), lambda g, n, i, k: like.index(lay(g), i, n, tm, tn)),
        out_shape=jax.ShapeDtypeStruct((like.nj, like.nl, like.r, like.c), F32),
        scratch_shapes=[pltpu.VMEM((tm, tn), F32)] if nk > 1 else [],
        compiler_params=_params("parallel", "parallel", "parallel", "arbitrary"),
    )(a, b, *deps)


def _row_spec(tr, d):
    return pl.BlockSpec((tr, d), lambda i: (i, 0))


def _vec_spec(d):
    return pl.BlockSpec((1, d), lambda i: (0, 0))


def _rms_fwd(name, x, g):
    s, d = x.shape
    tr = _tile(s, ROW_TILE)

    def body(x_ref, g_ref, o_ref):
        xv = x_ref[...]
        r = lax.rsqrt(jnp.mean(xv * xv, axis=-1, keepdims=True) + EPS)
        o_ref[...] = (xv * r * g_ref[...]).astype(BF16)

    return pl.pallas_call(
        body, name=name, grid=(s // tr,), in_specs=[_row_spec(tr, d), _vec_spec(d)],
        out_specs=_row_spec(tr, d), out_shape=jax.ShapeDtypeStruct((s, d), BF16),
        compiler_params=_params("parallel"),
    )(x, g)


def _rms_bwd(name, x, g, dhn, dres):
    s, d = x.shape
    tr = _tile(s, ROW_TILE)

    def body(x_ref, g_ref, dhn_ref, dres_ref, dx_ref, dxb_ref, dg_ref):
        @pl.when(pl.program_id(0) == 0)
        def _():
            dg_ref[...] = jnp.zeros_like(dg_ref)

        xv = x_ref[...]
        r = lax.rsqrt(jnp.mean(xv * xv, axis=-1, keepdims=True) + EPS)
        xh = xv * r
        dy = dhn_ref[...]
        dg_ref[...] += jnp.sum(dy * xh, axis=0, keepdims=True)
        dxh = dy * g_ref[...]
        dx = dres_ref[...] + r * (dxh - xh * jnp.mean(dxh * xh, axis=-1, keepdims=True))
        dx_ref[...] = dx
        dxb_ref[...] = dx.astype(BF16)

    return pl.pallas_call(
        body, name=name, grid=(s // tr,),
        in_specs=[_row_spec(tr, d), _vec_spec(d), _row_spec(tr, d), _row_spec(tr, d)],
        out_specs=[_row_spec(tr, d), _row_spec(tr, d), _vec_spec(d)],
        out_shape=[jax.ShapeDtypeStruct((s, d), F32), jax.ShapeDtypeStruct((s, d), BF16),
                   jax.ShapeDtypeStruct((1, d), F32)],
        compiler_params=_params("arbitrary"),
    )(x, g, dhn, dres)


def _final(name, h, g, tgt):
    s, d = h.shape
    tr = _tile(s, ROW_TILE)

    def body(h_ref, g_ref, t_ref, dh_ref, dhb_ref, dg_ref, loss_ref):
        @pl.when(pl.program_id(0) == 0)
        def _():
            dg_ref[...] = jnp.zeros_like(dg_ref)
            loss_ref[...] = jnp.zeros_like(loss_ref)

        hv = h_ref[...]
        r = lax.rsqrt(jnp.mean(hv * hv, axis=-1, keepdims=True) + EPS)
        xh = hv * r
        gv = g_ref[...]
        err = xh * gv - t_ref[...]
        part = 0.5 * jnp.sum(jnp.mean(err * err, axis=-1, keepdims=True), axis=0, keepdims=True)
        loss_ref[...] += jnp.broadcast_to(part, loss_ref.shape)
        dy = err * (1.0 / d)
        dg_ref[...] += jnp.sum(dy * xh, axis=0, keepdims=True)
        dxh = dy * gv
        dh = r * (dxh - xh * jnp.mean(dxh * xh, axis=-1, keepdims=True))
        dh_ref[...] = dh
        dhb_ref[...] = dh.astype(BF16)

    return pl.pallas_call(
        body, name=name, grid=(s // tr,),
        in_specs=[_row_spec(tr, d), _vec_spec(d), _row_spec(tr, d)],
        out_specs=[_row_spec(tr, d), _row_spec(tr, d), _vec_spec(d), _vec_spec(LANES)],
        out_shape=[jax.ShapeDtypeStruct((s, d), F32), jax.ShapeDtypeStruct((s, d), BF16),
                   jax.ShapeDtypeStruct((1, d), F32), jax.ShapeDtypeStruct((1, LANES), F32)],
        compiler_params=_params("arbitrary"),
    )(h, g, tgt)


_SQRT_HALF = 1.0 / math.sqrt(2.0)
_INV_SQRT_2PI = 1.0 / math.sqrt(2.0 * math.pi)


def _gelu(x):
    return x * (lax.erf(x * _SQRT_HALF) + 1.0) * 0.5


def _gelu_grad(x):
    return 0.5 * (lax.erf(x * _SQRT_HALF) + 1.0) + x * jnp.exp(-0.5 * x * x) * _INV_SQRT_2PI


def _causal(chunk):
    t = lax.broadcasted_iota(jnp.int32, (chunk, chunk), 0)
    sidx = lax.broadcasted_iota(jnp.int32, (chunk, chunk), 1)
    return sidx <= t


def _layernorm_parts(v, g, b):
    mu = jnp.mean(v, axis=-1, keepdims=True)
    vc = v - mu
    rs = lax.rsqrt(jnp.mean(vc * vc, axis=-1, keepdims=True) + EPS)
    vhat = vc * rs
    return vhat, rs, vhat * g + b


def _gate_fwd(name, zpre, ln_g, ln_b, w_s, b_col):
    s, aw2 = zpre.shape
    aw = aw2 // 2
    ng, chunk, _ = w_s.shape
    dh = aw // ng

    def body(z_ref, g_ref, b_ref, ws_ref, bc_ref, o_ref):
        u = _gelu(z_ref[:, :aw])
        v = _gelu(z_ref[:, aw:])
        _, _, vln = _layernorm_parts(v, g_ref[...], b_ref[...])
        mask = _causal(chunk)
        for gi in range(ng):
            sl = slice(gi * dh, (gi + 1) * dh)
            wm = jnp.where(mask, ws_ref[gi], 0.0).astype(BF16)
            sg = jnp.dot(wm, vln[:, sl].astype(BF16), preferred_element_type=F32) + bc_ref[:, gi:gi + 1]
            o_ref[:, sl] = (u[:, sl] * sg).astype(BF16)

    return pl.pallas_call(
        body, name=name, grid=(s // chunk,),
        in_specs=[_row_spec(chunk, aw2), _vec_spec(aw), _vec_spec(aw),
                  pl.BlockSpec((ng, chunk, chunk), lambda i: (0, 0, 0)),
                  pl.BlockSpec((chunk, ng), lambda i: (0, 0))],
        out_specs=_row_spec(chunk, aw), out_shape=jax.ShapeDtypeStruct((s, aw), BF16),
        compiler_params=_params("parallel"),
    )(zpre, ln_g, ln_b, w_s, b_col)


def _gate_bwd(name, zpre, dgated, ln_g, ln_b, w_s, b_col):
    s, aw2 = zpre.shape
    aw = aw2 // 2
    ng, chunk, _ = w_s.shape
    dh = aw // ng

    def body(z_ref, dgt_ref, g_ref, b_ref, ws_ref, bc_ref, dz_ref, dws_ref, dbc_ref, dlg_ref, dlb_ref,
             du_scr, dvln_scr):
        @pl.when(pl.program_id(0) == 0)
        def _():
            dws_ref[...] = jnp.zeros_like(dws_ref)
            dbc_ref[...] = jnp.zeros_like(dbc_ref)
            dlg_ref[...] = jnp.zeros_like(dlg_ref)
            dlb_ref[...] = jnp.zeros_like(dlb_ref)

        zu = z_ref[:, :aw]
        zv = z_ref[:, aw:]
        u = _gelu(zu)
        lg = g_ref[...]
        vhat, rs, vln = _layernorm_parts(_gelu(zv), lg, b_ref[...])
        mask = _causal(chunk)
        for gi in range(ng):
            sl = slice(gi * dh, (gi + 1) * dh)
            wm = jnp.where(mask, ws_ref[gi], 0.0).astype(BF16)
            vg = vln[:, sl].astype(BF16)
            sg = jnp.dot(wm, vg, preferred_element_type=F32) + bc_ref[:, gi:gi + 1]
            dgt = dgt_ref[:, sl]
            du_scr[:, sl] = dgt * sg
            ds = dgt * u[:, sl]
            dbc_ref[:, gi:gi + 1] += jnp.sum(ds, axis=-1, keepdims=True)
            dsb = ds.astype(BF16)
            dwm = lax.dot_general(dsb, vg, (((1,), (1,)), ((), ())), preferred_element_type=F32)
            dws_ref[gi] += jnp.where(mask, dwm, 0.0)
            dvln_scr[:, sl] = lax.dot_general(wm, dsb, (((0,), (0,)), ((), ())), preferred_element_type=F32)
        dvln = dvln_scr[...]
        dlb_ref[...] += jnp.sum(dvln, axis=0, keepdims=True)
        dlg_ref[...] += jnp.sum(dvln * vhat, axis=0, keepdims=True)
        dvh = dvln * lg
        dv = rs * (dvh - jnp.mean(dvh, axis=-1, keepdims=True)
                   - vhat * jnp.mean(dvh * vhat, axis=-1, keepdims=True))
        dz_ref[:, :aw] = (du_scr[...] * _gelu_grad(zu)).astype(BF16)
        dz_ref[:, aw:] = (dv * _gelu_grad(zv)).astype(BF16)

    return pl.pallas_call(
        body, name=name, grid=(s // chunk,),
        in_specs=[_row_spec(chunk, aw2), _row_spec(chunk, aw), _vec_spec(aw), _vec_spec(aw),
                  pl.BlockSpec((ng, chunk, chunk), lambda i: (0, 0, 0)),
                  pl.BlockSpec((chunk, ng), lambda i: (0, 0))],
        out_specs=[_row_spec(chunk, aw2), pl.BlockSpec((ng, chunk, chunk), lambda i: (0, 0, 0)),
                   pl.BlockSpec((chunk, ng), lambda i: (0, 0)), _vec_spec(aw), _vec_spec(aw)],
        out_shape=[jax.ShapeDtypeStruct((s, aw2), BF16), jax.ShapeDtypeStruct((ng, chunk, chunk), F32),
                   jax.ShapeDtypeStruct((chunk, ng), F32), jax.ShapeDtypeStruct((1, aw), F32),
                   jax.ShapeDtypeStruct((1, aw), F32)],
        scratch_shapes=[pltpu.VMEM((chunk, aw), F32), pltpu.VMEM((chunk, aw), F32)],
        compiler_params=_params("arbitrary"),
    )(zpre, dgated, ln_g, ln_b, w_s, b_col)


def _pool_select(g, parts):
    out = parts[-1]
    for gi in range(len(parts) - 2, -1, -1):
        out = jnp.where(g == gi, parts[gi], out)
    return out


def _pool_specs(s, bw):
    head = bw // len(B_WINDOWS)
    tc = _tile(head, 256)
    nb = head // tc
    return tc, (len(B_WINDOWS), nb), pl.BlockSpec((s, tc), lambda g, j: (0, g * nb + j))


def _pool_window(g, t):
    w = _pool_select(g, [jnp.full(t.shape, wi, jnp.int32) for wi in B_WINDOWS])
    return jnp.minimum(t + 1, w).astype(F32)


def _pool_fwd(name, vb):
    assert B_WINDOWS == (2, 4, 8, 16)
    s, bw = vb.shape
    tc, grid, spec = _pool_specs(s, bw)

    def body(v_ref, o_ref):
        g = pl.program_id(0)
        v = v_ref[...]
        t = lax.broadcasted_iota(jnp.int32, (s, tc), 0)

        def down(x, k):
            return jnp.where(t >= k, pltpu.roll(x, k, 0), 0.0)

        sums, cur, k = [], v, 1
        for _ in B_WINDOWS:
            cur = cur + down(cur, k)
            sums.append(cur)
            k *= 2
        o_ref[...] = (_pool_select(g, sums) / _pool_window(g, t) - v).astype(BF16)

    return pl.pallas_call(
        body, name=name, grid=grid, in_specs=[spec], out_specs=spec,
        out_shape=jax.ShapeDtypeStruct((s, bw), BF16), compiler_params=_params("parallel", "parallel"),
    )(vb)


def _pool_bwd(name, dpooled):
    s, bw = dpooled.shape
    tc, grid, spec = _pool_specs(s, bw)

    def body(d_ref, o_ref):
        g = pl.program_id(0)
        dp = d_ref[...]
        t = lax.broadcasted_iota(jnp.int32, (s, tc), 0)

        def up(x, k):
            return jnp.where(t < s - k, pltpu.roll(x, s - k, 0), 0.0)

        sums, cur, k = [], dp / _pool_window(g, t), 1
        for _ in B_WINDOWS:
            cur = cur + up(cur, k)
            sums.append(cur)
            k *= 2
        o_ref[...] = (_pool_select(g, sums) - dp).astype(BF16)

    return pl.pallas_call(
        body, name=name, grid=grid, in_specs=[spec], out_specs=spec,
        out_shape=jax.ShapeDtypeStruct((s, bw), BF16), compiler_params=_params("parallel", "parallel"),
    )(dpooled)


def _scale_bwd(name, dms, mixed, scale):
    s, bw = dms.shape
    tr = _tile(s, ROW_TILE)

    def body(d_ref, m_ref, sc_ref, o_ref, ds_ref):
        @pl.when(pl.program_id(0) == 0)
        def _():
            ds_ref[...] = jnp.zeros_like(ds_ref)

        dv = d_ref[...]
        ds_ref[...] += jnp.sum(dv * m_ref[...], axis=0, keepdims=True)
        o_ref[...] = (dv * sc_ref[...]).astype(BF16)

    return pl.pallas_call(
        body, name=name, grid=(s // tr,), in_specs=[_row_spec(tr, bw), _row_spec(tr, bw), _vec_spec(bw)],
        out_specs=[_row_spec(tr, bw), _vec_spec(bw)],
        out_shape=[jax.ShapeDtypeStruct((s, bw), BF16), jax.ShapeDtypeStruct((1, bw), F32)],
        compiler_params=_params("arbitrary"),
    )(dms, mixed, scale)


def _adam_update(w, g, m, v):
    m = ADAM_B1 * m + (1.0 - ADAM_B1) * g
    v = ADAM_B2 * v + (1.0 - ADAM_B2) * (g * g)
    m_hat = m / (1.0 - ADAM_B1 ** ADAM_STEP)
    v_hat = v / (1.0 - ADAM_B2 ** ADAM_STEP)
    delta = -ADAM_LR * (m_hat / (jnp.sqrt(v_hat) + ADAM_EPS) + ADAM_WD * w)
    return delta, m, v


def _adam_shard(name, w, m, v, parts, layer=0, prev=None, deps=()):
    nl, r, c = w.shape
    nj, nlp = parts.shape[:2]
    tr = _tile(r, 128)
    spec = pl.BlockSpec((None, tr, c), lambda l, i: (layer + l, i, 0))

    def body(w_ref, m_ref, v_ref, p_ref, *rest):
        g_ref, d_ref, nm_ref, nv_ref = rest[-4:]
        g = p_ref[0].astype(F32)
        for j in range(1, nj):
            g = g + p_ref[j].astype(F32)
        delta, nm, nv = _adam_update(w_ref[...], g, m_ref[...], v_ref[...])
        g_ref[...] = g
        d_ref[...] = delta
        nm_ref[...] = nm
        nv_ref[...] = nv

    prev = () if prev is None else tuple(prev)
    return pl.pallas_call(
        body, name=name, grid=(nlp, r // tr),
        in_specs=[spec, spec, spec, pl.BlockSpec((nj, None, tr, c), lambda l, i: (0, l, i, 0))]
        + [_ANY] * (len(prev) + len(deps)),
        out_specs=[spec] * 4, out_shape=[jax.ShapeDtypeStruct(w.shape, F32)] * 4,
        input_output_aliases={4 + i: i for i in range(len(prev))},
        compiler_params=_params("parallel", "parallel"),
    )(w, m, v, parts, *prev, *deps)


def _adam_packed(name, w, g, m, v):
    rows, lanes = w.shape
    tr = _tile(rows, 512)
    spec = pl.BlockSpec((tr, lanes), lambda i: (i, 0))

    def body(w_ref, g_ref, m_ref, v_ref, d_ref, nm_ref, nv_ref):
        delta, nm, nv = _adam_update(w_ref[...], g_ref[...], m_ref[...], v_ref[...])
        d_ref[...] = delta
        nm_ref[...] = nm
        nv_ref[...] = nv

    return pl.pallas_call(
        body, name=name, grid=(rows // tr,), in_specs=[spec] * 4, out_specs=[spec] * 3,
        out_shape=[jax.ShapeDtypeStruct(w.shape, F32)] * 3, compiler_params=_params("parallel"),
    )(w, g, m, v)


def _place():
    x, y, c = lax.axis_index("x"), lax.axis_index("y"), lax.axis_index("c")
    chips = [(1 - x, y), (x, 1 - y), (1 - x, 1 - y)]
    return x, y, c, chips


def _remote(src, dst, send_sem, recv_sem, device):
    return pltpu.make_async_remote_copy(src_ref=src, dst_ref=dst, send_sem=send_sem, recv_sem=recv_sem,
                                        device_id=device, device_id_type=MESH)


def _half(ref_rows, cc):
    h = ref_rows // 2
    return pl.ds(cc * h, h)


_HBM = pl.BlockSpec(memory_space=pltpu.HBM)
_SEM = pl.BlockSpec(memory_space=pltpu.SEMAPHORE)
_EFFECT = pltpu.SideEffectType.DATAFLOW_SIDE_EFFECTING


def _in_hbm(arr):
    return pltpu.with_memory_space_constraint(arr, pltpu.HBM)


def _gather_block(land, shard_rows, split, j, cc):
    if split:
        return land.at[j, :, _half(shard_rows, cc), :]
    return land.at[j]


def _split_start(name, srcs, lands, plan, deps=()):
    ns, nl = len(srcs), len(lands)
    both = list(srcs) + list(lands)
    nb = ns + nl

    def body(*refs):
        s, ld = refs[:ns], refs[ns:nb]
        outs = refs[nb + len(deps):]
        send, recv, token = outs[:nl], outs[nl:2 * nl], outs[-1]
        for a, copies in enumerate(plan(s, ld)):
            for src, dst, peer in copies:
                _remote(src, dst, send[a], recv[a], peer).start()
        token[...] = jnp.zeros_like(token)

    outs = pl.pallas_call(
        body, name=name, in_specs=[_HBM] * nb + [_ANY] * len(deps),
        out_specs=[_SEM] * (2 * nl) + [_HBM] * nb + [pl.BlockSpec(memory_space=pltpu.VMEM)],
        out_shape=[pltpu.SemaphoreType.DMA(())] * (2 * nl) + [pltpu.HBM(b.shape, b.dtype) for b in both]
        + [jax.ShapeDtypeStruct((8, LANES), F32)],
        input_output_aliases={i: 2 * nl + i for i in range(nb)},
        compiler_params=pltpu.CompilerParams(has_side_effects=_EFFECT),
    )(*[_in_hbm(b) for b in both], *deps)
    thru = outs[2 * nl:2 * nl + nb]
    return list(outs[:nl]), list(outs[nl:2 * nl]), list(thru[:ns]), list(thru[ns:]), outs[-1]


def _split_wait(name, srcs, lands, send, recv, after, whole):
    ns, nl = len(srcs), len(lands)
    both = list(srcs) + list(lands)
    nb = ns + nl

    def body(*refs):
        ld, snd, rcv = refs[ns:nb], refs[nb:nb + nl], refs[nb + nl:nb + 2 * nl]
        x, y, c, _ = _place()
        for a, blk in enumerate(whole(ld)):
            every = _remote(blk, blk, snd[a], rcv[a], (x, y, c))
            every.wait_send()
            every.wait_recv()

    outs = pl.pallas_call(
        body, name=name, in_specs=[_HBM] * nb + [_SEM] * (2 * nl) + [_ANY], out_specs=[_HBM] * nb,
        out_shape=[pltpu.HBM(b.shape, b.dtype) for b in both],
        input_output_aliases={i: i for i in range(nb)},
        compiler_params=pltpu.CompilerParams(has_side_effects=_EFFECT),
    )(*both, *send, *recv, after)
    return list(outs[:ns]), list(outs[ns:])


def _gather_plan(split):
    def plan(srcs, lands):
        x, y, c, chips = _place()
        out = []
        for s, ld, sp in zip(srcs, lands, split):
            part = s.at[:, _half(s.shape[1], c), :] if sp else s
            out.append([(part, _gather_block(ld, s.shape[1], sp, 2 * x + y, c), (qx, qy, c)) for qx, qy in chips])
        return out
    return plan


def _gather_whole(split):
    def whole(lands):
        _, _, c, _ = _place()
        return [ld.at[pl.ds(0, 3), :, _half(ld.shape[2], c), :] if sp else ld.at[pl.ds(0, 3)]
                for ld, sp in zip(lands, split)]
    return whole


def _relay_plan(srcs, lands):
    x, y, c, _ = _place()
    out = []
    for ld in lands:
        blocks = [ld.at[j, :, _half(ld.shape[2], c), :] for j in range(N_CHIPS)]
        out.append([(blk, blk, (x, y, 1 - c)) for blk in blocks])
    return out


def _relay_whole(lands):
    _, _, c, _ = _place()
    return [ld.at[:, :, _half(ld.shape[2], c), :] for ld in lands]


def _fill_own(name, src, dest, dest_shape, place, src_slot=False, half_rows=False):
    nl, r, c = src.shape[-3:]
    tr = _tile(r, 512)
    nb = r // tr

    def src_map(l, i, p):
        return (p[0], l, i, 0) if src_slot else (l, i, 0)

    def dst_map(l, i, p):
        return (p[0], l, (p[1] * nb if half_rows else 0) + i, 0)

    def body(p_ref, s_ref, *rest):
        rest[-1][...] = s_ref[...]

    in_specs = [pl.BlockSpec((None, None, tr, c) if src_slot else (None, tr, c), src_map)]
    operands, aliases = [src], {}
    if dest is not None:
        in_specs.append(_ANY)
        operands.append(dest)
        aliases = {2: 0}
    return pl.pallas_call(
        body, name=name,
        grid_spec=pltpu.PrefetchScalarGridSpec(
            num_scalar_prefetch=1, grid=(nl, nb), in_specs=in_specs,
            out_specs=pl.BlockSpec((None, None, tr, c), dst_map)),
        out_shape=jax.ShapeDtypeStruct(dest_shape, src.dtype), input_output_aliases=aliases,
        compiler_params=_params("parallel", "parallel"),
    )(place, *operands)


def _gather_finish(name, lands):
    n = len(lands)

    def body(*refs):
        outs = refs[n:2 * n]
        fsend, frecv = refs[2 * n:]
        x, y, c, chips = _place()
        sib = (x, y, 1 - c)

        def relay(a, qi, cc):
            qx, qy = chips[qi]
            blk = _gather_block(outs[a], outs[a].shape[2], True, 2 * qx + qy, cc)
            return _remote(blk, blk, fsend.at[a, qi], frecv.at[a, qi], sib)

        relays = [relay(a, qi, c) for a in range(n) for qi in range(3)]
        for cp in relays:
            cp.start()
        for a in range(n):
            for qi in range(3):
                relay(a, qi, 1 - c).wait_recv()
        for cp in relays:
            cp.wait_send()

    outs = pl.pallas_call(
        body, name=name, in_specs=[_ANY] * n, out_specs=[_ANY] * n,
        out_shape=[jax.ShapeDtypeStruct(ld.shape, ld.dtype) for ld in lands],
        input_output_aliases={i: i for i in range(n)},
        scratch_shapes=[pltpu.SemaphoreType.DMA((n, 3))] * 2,
    )(*lands)
    return list(outs)


def _pair_plan(srcs, lands):
    x, y, c, _ = _place()
    return [[(s.at[:, :, _half(s.shape[2], 1 - c), :], ld, (x, y, 1 - c))] for s, ld in zip(srcs, lands)]


def _pair_whole(lands):
    return list(lands)


def _scatter_plan(srcs, lands):
    x, y, c, chips = _place()
    return [[(s.at[2 * qx + qy], ld.at[2 * x + y, :, _half(ld.shape[2], c), :], (qx, qy, c)) for qx, qy in chips]
            for s, ld in zip(srcs, lands)]


def _scatter_whole(lands):
    _, _, c, _ = _place()
    return [ld.at[pl.ds(0, 3), :, _half(ld.shape[2], c), :] for ld in lands]


def _pair_sum(name, grad, other, place):
    nj, nl, r, c = grad.shape
    h = r // 2
    tr = _tile(h, 256)
    nb = h // tr

    def body(p_ref, g_ref, o_ref, q_ref, d_ref):
        q = (g_ref[...] + o_ref[...]).astype(BF16)
        q_ref[...] = q

        @pl.when(pl.program_id(2) == p_ref[0])
        def _():
            d_ref[...] = q

    blk = (None, None, tr, c)
    return pl.pallas_call(
        body, name=name,
        grid_spec=pltpu.PrefetchScalarGridSpec(
            num_scalar_prefetch=1, grid=(nl, nb, nj),
            in_specs=[pl.BlockSpec(blk, lambda l, i, j, p: (j, l, p[1] * nb + i, 0)),
                      pl.BlockSpec(blk, lambda l, i, j, p: (j, l, i, 0))],
            out_specs=[pl.BlockSpec(blk, lambda l, i, j, p: (j, l, i, 0)),
                       pl.BlockSpec(blk, lambda l, i, j, p: (p[0], l, p[1] * nb + i, 0))]),
        out_shape=[jax.ShapeDtypeStruct((nj, nl, h, c), BF16), jax.ShapeDtypeStruct((nj, nl, r, c), BF16)],
        compiler_params=_params("parallel", "parallel", "arbitrary"),
    )(place, grad, other)


def _allreduce_small(name, packed):
    rows, lanes = packed.shape

    def body(x_ref, o_ref, all_ref, send, recv, lsem):
        x, y, c, chips = _place()
        me, sib = (x, y, c), (x, y, 1 - c)

        def slot(px, py, pc):
            return all_ref.at[pl.ds((4 * px + 2 * py + pc) * rows, rows), :]

        def copy(k, blk, to, src=None):
            return _remote(slot(*blk) if src is None else src, slot(*blk), send.at[k], recv.at[k], to)

        mine = pltpu.make_async_copy(x_ref, slot(*me), lsem)
        mine.start()
        first = [copy(0, me, sib, src=x_ref)] + [copy(1 + j, me, (*chip, c), src=x_ref) for j, chip in enumerate(chips)]
        for cp in first:
            cp.start()
        passed = [copy(4 + j, (*chip, c), sib) for j, chip in enumerate(chips)]
        for j, chip in enumerate(chips):
            copy(1 + j, (*chip, c), me).wait_recv()
            passed[j].start()
        copy(0, sib, me).wait_recv()
        for j, chip in enumerate(chips):
            copy(4 + j, (*chip, 1 - c), me).wait_recv()
        for cp in first + passed:
            cp.wait_send()
        mine.wait()
        total = all_ref[pl.ds(0, rows), :]
        for d in range(1, N_DEV):
            total = total + all_ref[pl.ds(d * rows, rows), :]
        o_ref[...] = total

    vmem = pl.BlockSpec(memory_space=pltpu.VMEM)
    return pl.pallas_call(
        body, name=name, in_specs=[vmem], out_specs=vmem, out_shape=jax.ShapeDtypeStruct(packed.shape, F32),
        scratch_shapes=[pltpu.VMEM((N_DEV * rows, lanes), F32), pltpu.SemaphoreType.DMA((7,)),
                        pltpu.SemaphoreType.DMA((7,)), pltpu.SemaphoreType.DMA],
        compiler_params=pltpu.CompilerParams(vmem_limit_bytes=VMEM_LIMIT),
    )(packed)


def _pack(parts):
    rows = []
    for p in parts:
        flat = p.reshape(-1)
        pad = (-flat.shape[0]) % PACK_ELEMS
        rows.append(jnp.pad(flat, (0, pad)).reshape(-1, LANES))
    return jnp.concatenate(rows, axis=0)


def _unpack(packed, shapes):
    out, row = [], 0
    for sh in shapes:
        size = math.prod(sh)
        nrows = -(-size // PACK_ELEMS) * (PACK_ELEMS // LANES)
        out.append(packed[row:row + nrows].reshape(-1)[:size].reshape(sh))
        row += nrows
    return out


def kernel(x, a_w_in, a_ln_g, a_ln_b, a_w_s, a_b_s, a_w_out, b_w_in, b_w_grp, b_scale, b_w_out, norm_mix, norm_mlp, mlp_w1, mlp_w2, final_norm, loss_target, m_a_w_in, m_a_ln_g, m_a_ln_b, m_a_w_s, m_a_b_s, m_a_w_out, m_b_w_in, m_b_w_grp, m_b_scale, m_b_w_out, m_norm_mix, m_norm_mlp, m_mlp_w1, m_mlp_w2, m_final_norm, v_a_w_in, v_a_ln_g, v_a_ln_b, v_a_w_s, v_a_b_s, v_a_w_out, v_b_w_in, v_b_w_grp, v_b_scale, v_b_w_out, v_norm_mix, v_norm_mlp, v_mlp_w1, v_mlp_w2, v_final_norm):
    xi, yi, ci = lax.axis_index("x"), lax.axis_index("y"), lax.axis_index("c")
    chip = 2 * xi + yi
    place = jnp.stack([chip, ci]).astype(jnp.int32)
    x2, tgt = x[0], loss_target[0]
    bw = b_scale.shape[1] * N_CHIPS

    units = dict(a_w_in=a_w_in, a_w_out=a_w_out, w1_0=mlp_w1[0:1], w2_0=mlp_w2[0:1], b_scale=b_scale.reshape(1, 1, -1),
                 b_w_in=b_w_in, b_w_grp=b_w_grp[0], b_w_out=b_w_out, w1_1=mlp_w1[1:2], w2_1=mlp_w2[1:2])
    col_sharded = dict(a_w_in=True, a_w_out=False, b_w_in=False, b_w_grp=False, b_w_out=False,
                       w1_0=True, w2_0=False, w1_1=True, w2_1=False)
    in_flight, W = {}, {}

    def launch(tag, keys, deps):
        sp = [k != "b_scale" for k in keys]
        shards = [units[k].astype(BF16) if s else units[k] for k, s in zip(keys, sp)]
        zones = [_fill_own(f"gather_own_{k}", s, None, (N_CHIPS, *s.shape), place) for k, s in zip(keys, shards)]
        send, recv, shards, zones, tok = _split_start(f"gather_start_{tag}", shards, zones, _gather_plan(sp), deps)
        in_flight.update({k: state for k, *state in zip(keys, send, recv, shards, zones, sp)})
        return tok

    def arrive(keys, after):
        send, recv, shards, zones, sp = zip(*[in_flight[k] for k in keys])
        _, zones = _split_wait(f"gather_wait_{keys[0]}", shards, zones, send, recv, after, _gather_whole(sp))
        relayed = iter(_gather_finish(f"gather_finish_{keys[0]}", [z for z, s in zip(zones, sp) if s]))
        for k, z, s in zip(keys, zones, sp):
            full = next(relayed) if s else z
            W[k] = _W4(full, col_sharded[k]) if k in col_sharded else full

    token = launch("first", ["a_w_in", "a_w_out"], ())
    token = launch("rest", ["w1_0", "w2_0", "b_scale", "b_w_in", "b_w_grp", "b_w_out", "w1_1", "w2_1"], (token,))

    b_col = a_b_s[0].T

    def residual(acc, res):
        return (res + acc,)

    def sq_relu(acc):
        act = jnp.maximum(acc, 0.0)
        return act, act * act

    def mlp_fwd(tag, h, layer):
        hn = _rms_fwd(f"mlp{tag}_norm", h, norm_mlp[layer:layer + 1])
        arrive([f"w1_{layer}"], hn)
        act, act_sq = _mm_aw(f"mlp{tag}_up", hn, W[f"w1_{layer}"], out_dtypes=(BF16, BF16), epilogue=sq_relu)
        arrive([f"w2_{layer}"], act_sq)
        out = _mm_aw(f"mlp{tag}_down", act_sq, W[f"w2_{layer}"], extras=(h,), epilogue=residual)
        return out, (h, hn, act, act_sq)

    hn0 = _rms_fwd("mix_a_norm", x2, norm_mix[0:1])
    arrive(["a_w_in"], token)
    zpre = _mm_aw("mix_a_in", hn0, W["a_w_in"])
    gated = _gate_fwd("mix_a_gate", zpre, a_ln_g, a_ln_b, a_w_s[0], b_col)
    arrive(["a_w_out"], gated)
    h1 = _mm_aw("mix_a_out", gated, W["a_w_out"], extras=(x2,), epilogue=residual)
    h2, mlp0 = mlp_fwd("0", h1, 0)
    hn2 = _rms_fwd("mix_b_norm", h2, norm_mix[1:2])
    arrive(["b_scale", "b_w_in"], hn2)
    scale_full = W["b_scale"].reshape(1, bw)
    vb = _mm_aw("mix_b_in", hn2, W["b_w_in"])
    pooled = _pool_fwd("mix_b_pool", vb)
    arrive(["b_w_grp", "b_w_out"], pooled)
    mixed, ms = _mm_aw("mix_b_grp", pooled, W["b_w_grp"], groups=len(B_WINDOWS), extras=(scale_full,),
                       out_dtypes=(F32, BF16), epilogue=lambda acc, sc: (acc, acc * sc))
    h3 = _mm_aw("mix_b_out", ms, W["b_w_out"], extras=(h2,), epilogue=residual)
    h4, mlp1 = mlp_fwd("1", h3, 1)
    dh4, dh4_b, d_final, loss_part = _final("loss_head", h4, final_norm.reshape(1, -1), tgt)
    g1_like = _W4(None, True, shape=(N_CHIPS, 1, *W["w1_0"].arr.shape[2:]))
    g2_like = _W4(None, False, shape=(N_CHIPS, 1, *W["w2_0"].arr.shape[2:]))

    def exchange(tag, gs):
        zones = [lax.empty((g.shape[0], g.shape[1], g.shape[2] // 2, g.shape[3]), g.dtype) for g in gs]
        send, recv, srcs, zones, tok = _split_start(f"pair_start_{tag}", gs, zones, _pair_plan)
        return (tag, send, recv, srcs, zones), tok

    def reduce(state, after):
        tag, send, recv, srcs, zones = state
        srcs, zones = _split_wait(f"pair_wait_{tag}", srcs, zones, send, recv, after, _pair_whole)
        both = [_pair_sum(f"pair_sum_{tag}_{i}", g, o, place) for i, (g, o) in enumerate(zip(srcs, zones))]
        send, recv, sums, dests, tok = _split_start(f"scatter_start_{tag}", [b[0] for b in both],
                                                    [b[1] for b in both], _scatter_plan)
        return (tag, send, recv, sums, dests), tok

    def relay(state, after):
        tag, send, recv, sums, dests = state
        _, dests = _split_wait(f"scatter_wait_{tag}", sums, dests, send, recv, after, _scatter_whole)
        send, recv, _, dests, tok = _split_start(f"relay_start_{tag}", [], dests, _relay_plan)
        return (tag, send, recv, dests), tok

    def land(state, after):
        tag, send, recv, dests = state
        return _split_wait(f"relay_wait_{tag}", [], dests, send, recv, after, _relay_whole)[1]

    def mlp_bwd(tag, dh, dh_b, saved, layer, deps, pending=None):
        h, hn, act, act_sq = saved
        dpre = _mm_aw(f"mlp{tag}_down_dx", dh_b, W[f"w2_{layer}"], transpose_w=True, extras=(act,),
                      out_dtypes=(BF16,), epilogue=lambda acc, a: (acc * (2.0 * a),), deps=deps)
        scattering, dw_deps = None, ()
        if pending is not None:
            scattering, tok = reduce(pending, dpre)
            dw_deps = (tok,)
        g_w2 = _mm_dw(f"mlp{tag}_down_dw", act_sq, dh_b, g2_like, deps=dw_deps)
        pair_w2, tok = exchange(f"w2_{layer}", [g_w2])
        dhn = _mm_aw(f"mlp{tag}_up_dx", dpre, W[f"w1_{layer}"], transpose_w=True, deps=(tok,))
        g_w1 = _mm_dw(f"mlp{tag}_up_dw", hn, dpre, g1_like)
        pair_w1, tok1 = exchange(f"w1_{layer}", [g_w1])
        scat_w2, tok2 = reduce(pair_w2, dhn)
        dh_in, dh_in_b, d_norm = _rms_bwd(f"mlp{tag}_norm_bwd", h, norm_mlp[layer:layer + 1], dhn, dh)
        return dh_in, dh_in_b, d_norm, pair_w1, scat_w2, (tok1, tok2), scattering

    dh3, dh3_b, d_norm_mlp1, pair_w1_1, scat_w2_1, toks, _ = mlp_bwd("1", dh4, dh4_b, mlp1, 1, ())
    dms = _mm_aw("mix_b_out_dx", dh3_b, W["b_w_out"], transpose_w=True, deps=toks)
    g_b_out = _mm_dw("mix_b_out_dw", ms, dh3_b, W["b_w_out"])
    scat_w1_1, tok = reduce(pair_w1_1, dms)
    dmixed, d_scale = _scale_bwd("mix_b_scale_bwd", dms, mixed, scale_full)
    dpooled = _mm_aw("mix_b_grp_dx", dmixed, W["b_w_grp"], groups=len(B_WINDOWS), transpose_w=True, deps=(tok,))
    g_b_grp = _mm_dw("mix_b_grp_dw", pooled, dmixed, W["b_w_grp"], groups=len(B_WINDOWS))
    dvb = _pool_bwd("mix_b_pool_bwd", dpooled)
    dhn2 = _mm_aw("mix_b_in_dx", dvb, W["b_w_in"], transpose_w=True)
    g_b_in = _mm_dw("mix_b_in_dw", hn2, dvb, W["b_w_in"])
    pair_b, tok = exchange("b", [g_b_out, g_b_grp, g_b_in])
    dh2, dh2_b, d_norm_mix1 = _rms_bwd("mix_b_norm_bwd", h2, norm_mix[1:2], dhn2, dh3)
    dh1, dh1_b, d_norm_mlp0, pair_w1_0, scat_w2_0, toks, scat_b = mlp_bwd("0", dh2, dh2_b, mlp0, 0, (tok,),
                                                                          pending=pair_b)
    dgated = _mm_aw("mix_a_out_dx", dh1_b, W["a_w_out"], transpose_w=True, deps=toks)
    g_a_out = _mm_dw("mix_a_out_dw", gated, dh1_b, W["a_w_out"])
    pair_a_out, tok_a = exchange("a_out", [g_a_out])
    scat_w1_0, tok = reduce(pair_w1_0, dgated)
    dzpre, d_w_s, d_b_col, d_ln_g, d_ln_b = _gate_bwd("mix_a_gate_bwd", zpre, dgated, a_ln_g, a_ln_b, a_w_s[0], b_col)
    dhn0 = _mm_aw("mix_a_in_dx", dzpre, W["a_w_in"], transpose_w=True, deps=(tok, tok_a))
    scat_a_out, tok = reduce(pair_a_out, dhn0)
    g_a_in = _mm_dw("mix_a_in_dw", hn0, dzpre, W["a_w_in"], deps=(tok,))
    pair_a_in, tok = exchange("a_in", [g_a_in])
    dx, _, d_norm_mix0 = _rms_bwd("mix_a_norm_bwd", x2, norm_mix[0:1], dhn0, dh1)
    scat_a_in, _ = reduce(pair_a_in, dx)

    moments = dict(a_w_in=(m_a_w_in, v_a_w_in), a_w_out=(m_a_w_out, v_a_w_out), b_w_in=(m_b_w_in, v_b_w_in),
                   b_w_grp=(m_b_w_grp, v_b_w_grp), b_w_out=(m_b_w_out, v_b_w_out),
                   mlp_w1=(m_mlp_w1, v_mlp_w1), mlp_w2=(m_mlp_w2, v_mlp_w2))
    weights = dict(a_w_in=a_w_in, a_w_out=a_w_out, b_w_in=b_w_in, b_w_grp=b_w_grp, b_w_out=b_w_out,
                   mlp_w1=mlp_w1, mlp_w2=mlp_w2)
    landing = [(scat_w2_1, [("mlp_w2", 1)]), (scat_w1_1, [("mlp_w1", 1)]),
               (scat_b, [("b_w_out", 0), ("b_w_grp", 0), ("b_w_in", 0)]),
               (scat_w2_0, [("mlp_w2", 0)]), (scat_w1_0, [("mlp_w1", 0)]),
               (scat_a_out, [("a_w_out", 0)]), (scat_a_in, [("a_w_in", 0)])]
    results, after = {}, dx
    relaying, tok = relay(landing[0][0], after)
    for i, (_, members) in enumerate(landing):
        arrived, deps = relaying, ()
        if i + 1 < len(landing):
            relaying, tok = relay(landing[i + 1][0], after)
            deps = (tok,)
        for (k, layer), parts in zip(members, land(arrived, tok)):
            shard_shape = (-1, *parts.shape[2:])
            results[k] = _adam_shard(f"adam_{k}_{layer}", weights[k].reshape(shard_shape),
                                     moments[k][0].reshape(shard_shape), moments[k][1].reshape(shard_shape),
                                     parts, layer=layer, prev=results.get(k), deps=deps)
            after = results[k][1]
    grad_out, delta_out, m_out, v_out = {}, {}, {}, {}
    for k, res in results.items():
        grad_out[k], delta_out[k], m_out[k], v_out[k] = [r.reshape(weights[k].shape) for r in res]

    small = dict(a_ln_g=(a_ln_g, m_a_ln_g, v_a_ln_g), a_ln_b=(a_ln_b, m_a_ln_b, v_a_ln_b),
                 a_w_s=(a_w_s, m_a_w_s, v_a_w_s), a_b_s=(a_b_s, m_a_b_s, v_a_b_s),
                 b_scale=(b_scale, m_b_scale, v_b_scale), norm_mix=(norm_mix, m_norm_mix, v_norm_mix),
                 norm_mlp=(norm_mlp, m_norm_mlp, v_norm_mlp), final_norm=(final_norm, m_final_norm, v_final_norm))
    small_names = list(small)
    local = dict(a_ln_g=d_ln_g, a_ln_b=d_ln_b, a_w_s=d_w_s[None], a_b_s=d_b_col.T[None], b_scale=d_scale,
                 norm_mix=jnp.concatenate([d_norm_mix0, d_norm_mix1], axis=0),
                 norm_mlp=jnp.concatenate([d_norm_mlp0, d_norm_mlp1], axis=0), final_norm=d_final.reshape(-1))
    reduced = _allreduce_small("small_grad_allreduce", _pack([local[k] for k in small_names]))
    small_grads = dict(zip(small_names, _unpack(reduced, [local[k].shape for k in small_names])))
    shard_w = b_scale.shape[1]
    small_grads["b_scale"] = lax.dynamic_slice_in_dim(small_grads["b_scale"], chip * shard_w, shard_w, axis=1)
    small_grads = {k: small_grads[k].reshape(small[k][0].shape) for k in small_names}
    packed = [_pack([small[k][i] for k in small_names]) for i in range(3)]
    res = _adam_packed("adam_small", packed[0], _pack([small_grads[k] for k in small_names]), packed[1], packed[2])
    shapes = [small[k][0].shape for k in small_names]
    for k, d, nm, nv in zip(small_names, *[_unpack(r, shapes) for r in res]):
        grad_out[k], delta_out[k], m_out[k], v_out[k] = small_grads[k], d, nm, nv

    loss = lax.psum(loss_part[0, 0], ("x", "y", "c"))
    order = ["a_w_in", "a_ln_g", "a_ln_b", "a_w_s", "a_b_s", "a_w_out", "b_w_in", "b_w_grp", "b_scale", "b_w_out",
             "norm_mix", "norm_mlp", "mlp_w1", "mlp_w2", "final_norm"]
    return (loss, dx[None], *[grad_out[k] for k in order], *[delta_out[k] for k in order],
            *[m_out[k] for k in order], *[v_out[k] for k in order])
```

```python
import math

import jax
import jax.numpy as jnp
from jax import lax
from jax.experimental import pallas as pl
from jax.experimental.pallas import tpu as pltpu

F32 = jnp.float32
BF16 = jnp.bfloat16
MESH = pl.DeviceIdType.MESH

EPS = 1e-6
B_WINDOWS = (2, 4, 8, 16)
ADAM_LR = 0.001
ADAM_B1 = 0.9
ADAM_B2 = 0.999
ADAM_EPS = 1e-08
ADAM_WD = 0.01
ADAM_STEP = 10

N_CHIPS = 4
N_DEV = 8
LANES = 128
PACK_ELEMS = 8 * LANES
VMEM_LIMIT = 56 * 1024 * 1024
ROW_TILE = 256
MM_TM, MM_TN, MM_TK = 1024, 1024, 2048


_ANY = pl.BlockSpec(memory_space=pl.ANY)


def _tile(dim, pref):
    t = min(dim, pref)
    while dim % t:
        t //= 2
    return t


def _params(*sem):
    return pltpu.CompilerParams(dimension_semantics=sem, vmem_limit_bytes=VMEM_LIMIT)


class _W4:
    def __init__(self, arr, col_sharded, shape=None):
        self.arr = arr
        self.nj, self.nl, self.r, self.c = arr.shape if shape is None else shape
        self.col = col_sharded
        self.rows = self.r if col_sharded else self.nj * self.r
        self.cols = self.nj * self.c if col_sharded else self.c

    def tile_rows(self, pref):
        return _tile(self.r, pref)

    def tile_cols(self, pref):
        return _tile(self.c, pref)

    def index(self, layer, rb, cb, tr, tc):
        if self.col:
            n = self.c // tc
            return (cb // n, layer, rb, cb % n)
        n = self.r // tr
        return (rb // n, layer, rb % n, cb)


def _mm_aw(name, a, w, *, layer=0, groups=1, transpose_w=False, extras=(), out_dtypes=(F32,), epilogue=None,
           deps=()):
    s, ka_total = a.shape
    kdim, ndim = (w.cols, w.rows) if transpose_w else (w.rows, w.cols)
    assert ka_total == groups * kdim, (name, a.shape, kdim, groups)
    span = groups == 1 and not transpose_w and not w.col and kdim <= MM_TK
    if groups > 1:
        tm = _tile(s, MM_TM)
        tn, tk = (w.tile_rows(MM_TN), w.tile_cols(512)) if transpose_w else (w.tile_cols(MM_TN), w.tile_rows(512))
    else:
        tk = kdim if span else (w.tile_cols(MM_TK) if transpose_w else w.tile_rows(MM_TK))
        tm, tn_pref = (_tile(s, 2048), 512) if tk == kdim else (_tile(s, MM_TM), MM_TN)
        tn = w.tile_rows(tn_pref) if transpose_w else w.tile_cols(tn_pref)
    nk, nn = kdim // tk, ndim // tn

    def lay(g):
        return g if groups > 1 else layer

    a_spec = pl.BlockSpec((tm, tk), lambda g, i, n, k: (i, g * nk + k))
    if span:
        w_spec = pl.BlockSpec((w.nj, None, w.r, tn), lambda g, i, n, k: (0, layer, 0, n))
    elif transpose_w:
        w_spec = pl.BlockSpec((None, None, tn, tk), lambda g, i, n, k: w.index(lay(g), n, k, tn, tk))
    else:
        w_spec = pl.BlockSpec((None, None, tk, tn), lambda g, i, n, k: w.index(lay(g), k, n, tk, tn))
    ex_specs = []
    for e in extras:
        assert e.shape[1] == groups * ndim and e.shape[0] in (1, s), (name, e.shape)
        if e.shape[0] == 1:
            ex_specs.append(pl.BlockSpec((1, tn), lambda g, i, n, k: (0, g * nn + n)))
        else:
            ex_specs.append(pl.BlockSpec((tm, tn), lambda g, i, n, k: (i, g * nn + n)))
    out_spec = pl.BlockSpec((tm, tn), lambda g, i, n, k: (i, g * nn + n))
    n_ex, n_out, n_dep = len(extras), len(out_dtypes), len(deps)

    def body(a_ref, w_ref, *rest):
        ex, outs = rest[:n_ex], rest[n_ex + n_dep:n_ex + n_dep + n_out]
        av = a_ref[...]
        if av.dtype != BF16:
            av = av.astype(BF16)
        wv = w_ref[...].reshape(tk, tn) if span else w_ref[...]
        if transpose_w:
            prod = lax.dot_general(av, wv, (((1,), (1,)), ((), ())), preferred_element_type=F32)
        else:
            prod = jnp.dot(av, wv, preferred_element_type=F32)

        def finish(total):
            vals = (total,) if epilogue is None else epilogue(total, *[e[...] for e in ex])
            for o, v in zip(outs, vals):
                o[...] = v.astype(o.dtype)

        if nk == 1:
            finish(prod)
            return
        acc, k = rest[-1], pl.program_id(3)

        @pl.when(k == 0)
        def _():
            acc[...] = prod

        @pl.when(k > 0)
        def _():
            acc[...] += prod

        @pl.when(k == nk - 1)
        def _():
            finish(acc[...])

    outs = pl.pallas_call(
        body, name=name, grid=(groups, s // tm, nn, nk),
        in_specs=[a_spec, w_spec, *ex_specs] + [_ANY] * n_dep, out_specs=[out_spec] * n_out,
        out_shape=[jax.ShapeDtypeStruct((s, groups * ndim), dt) for dt in out_dtypes],
        scratch_shapes=[pltpu.VMEM((tm, tn), F32)] if nk > 1 else [],
        compiler_params=_params("parallel", "parallel", "parallel", "arbitrary"),
    )(a, w.arr, *extras, *deps)
    return outs[0] if n_out == 1 else outs


def _mm_dw(name, a, b, like, *, layer=0, groups=1, deps=()):
    s, ka_total = a.shape
    rows, cols = ka_total // groups, b.shape[1] // groups
    assert (rows, cols) == (like.rows, like.cols) and b.shape[0] == s, (name, a.shape, b.shape)
    tm, tn, tk = like.tile_rows(MM_TM), like.tile_cols(MM_TN), _tile(s, MM_TK)
    nr, nc, nk = rows // tm, cols // tn, s // tk

    def lay(g):
        return g if groups > 1 else layer

    in_specs = [pl.BlockSpec((tk, tm), lambda g, n, i, k: (k, g * nr + i)),
                pl.BlockSpec((tk, tn), lambda g, n, i, k: (k, g * nc + n))]
    in_specs += [_ANY] * len(deps)

    def body(a_ref, b_ref, *rest):
        av, bv = a_ref[...], b_ref[...]
        if av.dtype != BF16:
            av = av.astype(BF16)
        if bv.dtype != BF16:
            bv = bv.astype(BF16)
        prod = lax.dot_general(av, bv, (((0,), (0,)), ((), ())), preferred_element_type=F32)
        if nk == 1:
            rest[-1][...] = prod.astype(BF16)
            return
        o_ref, acc, k = rest[-2], rest[-1], pl.program_id(3)

        @pl.when(k == 0)
        def _():
            acc[...] = prod

        @pl.when(k > 0)
        def _():
            acc[...] += prod

        @pl.when(k == nk - 1)
        def _():
            o_ref[...] = acc[...].astype(BF16)

    return pl.pallas_call(
        body, name=name, grid=(groups, nc, nr, nk), in_specs=in_specs,
        out_specs=pl.BlockSpec((None, None, tm, tn), lambda g, n, i, k: like.index(lay(g), i, n, tm, tn)),
        out_shape=jax.ShapeDtypeStruct((like.nj, like.nl, like.r, like.c), BF16),
        scratch_shapes=[pltpu.VMEM((tm, tn), F32)] if nk > 1 else [],
        compiler_params=_params("parallel", "parallel", "parallel", "arbitrary"),
    )(a, b, *deps)


def _row_spec(tr, d):
    return pl.BlockSpec((tr, d), lambda i: (i, 0))


def _vec_spec(d):
    return pl.BlockSpec((1, d), lambda i: (0, 0))


def _rms_fwd(name, x, g):
    s, d = x.shape
    tr = _tile(s, ROW_TILE)

    def body(x_ref, g_ref, o_ref):
        xv = x_ref[...]
        r = lax.rsqrt(jnp.mean(xv * xv, axis=-1, keepdims=True) + EPS)
        o_ref[...] = (xv * r * g_ref[...]).astype(BF16)

    return pl.pallas_call(
        body, name=name, grid=(s // tr,), in_specs=[_row_spec(tr, d), _vec_spec(d)],
        out_specs=_row_spec(tr, d), out_shape=jax.ShapeDtypeStruct((s, d), BF16),
        compiler_params=_params("parallel"),
    )(x, g)


def _rms_bwd(name, x, g, dhn, dres):
    s, d = x.shape
    tr = _tile(s, ROW_TILE)

    def body(x_ref, g_ref, dhn_ref, dres_ref, dx_ref, dxb_ref, dg_ref):
        @pl.when(pl.program_id(0) == 0)
        def _():
            dg_ref[...] = jnp.zeros_like(dg_ref)

        xv = x_ref[...]
        r = lax.rsqrt(jnp.mean(xv * xv, axis=-1, keepdims=True) + EPS)
        xh = xv * r
        dy = dhn_ref[...]
        dg_ref[...] += jnp.sum(dy * xh, axis=0, keepdims=True)
        dxh = dy * g_ref[...]
        dx = dres_ref[...] + r * (dxh - xh * jnp.mean(dxh * xh, axis=-1, keepdims=True))
        dx_ref[...] = dx
        dxb_ref[...] = dx.astype(BF16)

    return pl.pallas_call(
        body, name=name, grid=(s // tr,),
        in_specs=[_row_spec(tr, d), _vec_spec(d), _row_spec(tr, d), _row_spec(tr, d)],
        out_specs=[_row_spec(tr, d), _row_spec(tr, d), _vec_spec(d)],
        out_shape=[jax.ShapeDtypeStruct((s, d), F32), jax.ShapeDtypeStruct((s, d), BF16),
                   jax.ShapeDtypeStruct((1, d), F32)],
        compiler_params=_params("arbitrary"),
    )(x, g, dhn, dres)


def _final(name, h, g, tgt):
    s, d = h.shape
    tr = _tile(s, ROW_TILE)

    def body(h_ref, g_ref, t_ref, dh_ref, dhb_ref, dg_ref, loss_ref):
        @pl.when(pl.program_id(0) == 0)
        def _():
            dg_ref[...] = jnp.zeros_like(dg_ref)
            loss_ref[...] = jnp.zeros_like(loss_ref)

        hv = h_ref[...]
        r = lax.rsqrt(jnp.mean(hv * hv, axis=-1, keepdims=True) + EPS)
        xh = hv * r
        gv = g_ref[...]
        err = xh * gv - t_ref[...]
        part = 0.5 * jnp.sum(jnp.mean(err * err, axis=-1, keepdims=True), axis=0, keepdims=True)
        loss_ref[...] += jnp.broadcast_to(part, loss_ref.shape)
        dy = err * (1.0 / d)
        dg_ref[...] += jnp.sum(dy * xh, axis=0, keepdims=True)
        dxh = dy * gv
        dh = r * (dxh - xh * jnp.mean(dxh * xh, axis=-1, keepdims=True))
        dh_ref[...] = dh
        dhb_ref[...] = dh.astype(BF16)

    return pl.pallas_call(
        body, name=name, grid=(s // tr,),
        in_specs=[_row_spec(tr, d), _vec_spec(d), _row_spec(tr, d)],
        out_specs=[_row_spec(tr, d), _row_spec(tr, d), _vec_spec(d), _vec_spec(LANES)],
        out_shape=[jax.ShapeDtypeStruct((s, d), F32), jax.ShapeDtypeStruct((s, d), BF16),
                   jax.ShapeDtypeStruct((1, d), F32), jax.ShapeDtypeStruct((1, LANES), F32)],
        compiler_params=_params("arbitrary"),
    )(h, g, tgt)


_SQRT_HALF = 1.0 / math.sqrt(2.0)
_INV_SQRT_2PI = 1.0 / math.sqrt(2.0 * math.pi)


def _gelu(x):
    return x * (lax.erf(x * _SQRT_HALF) + 1.0) * 0.5


def _gelu_grad(x):
    return 0.5 * (lax.erf(x * _SQRT_HALF) + 1.0) + x * jnp.exp(-0.5 * x * x) * _INV_SQRT_2PI


def _causal(chunk):
    t = lax.broadcasted_iota(jnp.int32, (chunk, chunk), 0)
    sidx = lax.broadcasted_iota(jnp.int32, (chunk, chunk), 1)
    return sidx <= t


def _layernorm_parts(v, g, b):
    mu = jnp.mean(v, axis=-1, keepdims=True)
    vc = v - mu
    rs = lax.rsqrt(jnp.mean(vc * vc, axis=-1, keepdims=True) + EPS)
    vhat = vc * rs
    return vhat, rs, vhat * g + b


def _gate_fwd(name, zpre, ln_g, ln_b, w_s, b_col):
    s, aw2 = zpre.shape
    aw = aw2 // 2
    ng, chunk, _ = w_s.shape
    dh = aw // ng

    def body(z_ref, g_ref, b_ref, ws_ref, bc_ref, o_ref):
        u = _gelu(z_ref[:, :aw])
        v = _gelu(z_ref[:, aw:])
        _, _, vln = _layernorm_parts(v, g_ref[...], b_ref[...])
        mask = _causal(chunk)
        for gi in range(ng):
            sl = slice(gi * dh, (gi + 1) * dh)
            wm = jnp.where(mask, ws_ref[gi], 0.0).astype(BF16)
            sg = jnp.dot(wm, vln[:, sl].astype(BF16), preferred_element_type=F32) + bc_ref[:, gi:gi + 1]
            o_ref[:, sl] = (u[:, sl] * sg).astype(BF16)

    return pl.pallas_call(
        body, name=name, grid=(s // chunk,),
        in_specs=[_row_spec(chunk, aw2), _vec_spec(aw), _vec_spec(aw),
                  pl.BlockSpec((ng, chunk, chunk), lambda i: (0, 0, 0)),
                  pl.BlockSpec((chunk, ng), lambda i: (0, 0))],
        out_specs=_row_spec(chunk, aw), out_shape=jax.ShapeDtypeStruct((s, aw), BF16),
        compiler_params=_params("parallel"),
    )(zpre, ln_g, ln_b, w_s, b_col)


def _gate_bwd(name, zpre, dgated, ln_g, ln_b, w_s, b_col):
    s, aw2 = zpre.shape
    aw = aw2 // 2
    ng, chunk, _ = w_s.shape
    dh = aw // ng

    def body(z_ref, dgt_ref, g_ref, b_ref, ws_ref, bc_ref, dz_ref, dws_ref, dbc_ref, dlg_ref, dlb_ref,
             du_scr, dvln_scr):
        @pl.when(pl.program_id(0) == 0)
        def _():
            dws_ref[...] = jnp.zeros_like(dws_ref)
            dbc_ref[...] = jnp.zeros_like(dbc_ref)
            dlg_ref[...] = jnp.zeros_like(dlg_ref)
            dlb_ref[...] = jnp.zeros_like(dlb_ref)

        zu = z_ref[:, :aw]
        zv = z_ref[:, aw:]
        u = _gelu(zu)
        lg = g_ref[...]
        vhat, rs, vln = _layernorm_parts(_gelu(zv), lg, b_ref[...])
        mask = _causal(chunk)
        for gi in range(ng):
            sl = slice(gi * dh, (gi + 1) * dh)
            wm = jnp.where(mask, ws_ref[gi], 0.0).astype(BF16)
            vg = vln[:, sl].astype(BF16)
            sg = jnp.dot(wm, vg, preferred_element_type=F32) + bc_ref[:, gi:gi + 1]
            dgt = dgt_ref[:, sl]
            du_scr[:, sl] = dgt * sg
            ds = dgt * u[:, sl]
            dbc_ref[:, gi:gi + 1] += jnp.sum(ds, axis=-1, keepdims=True)
            dsb = ds.astype(BF16)
            dwm = lax.dot_general(dsb, vg, (((1,), (1,)), ((), ())), preferred_element_type=F32)
            dws_ref[gi] += jnp.where(mask, dwm, 0.0)
            dvln_scr[:, sl] = lax.dot_general(wm, dsb, (((0,), (0,)), ((), ())), preferred_element_type=F32)
        dvln = dvln_scr[...]
        dlb_ref[...] += jnp.sum(dvln, axis=0, keepdims=True)
        dlg_ref[...] += jnp.sum(dvln * vhat, axis=0, keepdims=True)
        dvh = dvln * lg
        dv = rs * (dvh - jnp.mean(dvh, axis=-1, keepdims=True)
                   - vhat * jnp.mean(dvh * vhat, axis=-1, keepdims=True))
        dz_ref[:, :aw] = (du_scr[...] * _gelu_grad(zu)).astype(BF16)
        dz_ref[:, aw:] = (dv * _gelu_grad(zv)).astype(BF16)

    return pl.pallas_call(
        body, name=name, grid=(s // chunk,),
        in_specs=[_row_spec(chunk, aw2), _row_spec(chunk, aw), _vec_spec(aw), _vec_spec(aw),
                  pl.BlockSpec((ng, chunk, chunk), lambda i: (0, 0, 0)),
                  pl.BlockSpec((chunk, ng), lambda i: (0, 0))],
        out_specs=[_row_spec(chunk, aw2), pl.BlockSpec((ng, chunk, chunk), lambda i: (0, 0, 0)),
                   pl.BlockSpec((chunk, ng), lambda i: (0, 0)), _vec_spec(aw), _vec_spec(aw)],
        out_shape=[jax.ShapeDtypeStruct((s, aw2), BF16), jax.ShapeDtypeStruct((ng, chunk, chunk), F32),
                   jax.ShapeDtypeStruct((chunk, ng), F32), jax.ShapeDtypeStruct((1, aw), F32),
                   jax.ShapeDtypeStruct((1, aw), F32)],
        scratch_shapes=[pltpu.VMEM((chunk, aw), F32), pltpu.VMEM((chunk, aw), F32)],
        compiler_params=_params("arbitrary"),
    )(zpre, dgated, ln_g, ln_b, w_s, b_col)


def _pool_select(g, parts):
    out = parts[-1]
    for gi in range(len(parts) - 2, -1, -1):
        out = jnp.where(g == gi, parts[gi], out)
    return out


def _pool_specs(s, bw):
    head = bw // len(B_WINDOWS)
    tc = _tile(head, 256)
    nb = head // tc
    return tc, (len(B_WINDOWS), nb), pl.BlockSpec((s, tc), lambda g, j: (0, g * nb + j))


def _pool_window(g, t):
    w = _pool_select(g, [jnp.full(t.shape, wi, jnp.int32) for wi in B_WINDOWS])
    return jnp.minimum(t + 1, w).astype(F32)


def _pool_fwd(name, vb):
    assert B_WINDOWS == (2, 4, 8, 16)
    s, bw = vb.shape
    tc, grid, spec = _pool_specs(s, bw)

    def body(v_ref, o_ref):
        g = pl.program_id(0)
        v = v_ref[...]
        t = lax.broadcasted_iota(jnp.int32, (s, tc), 0)

        def down(x, k):
            return jnp.where(t >= k, pltpu.roll(x, k, 0), 0.0)

        sums, cur, k = [], v, 1
        for _ in B_WINDOWS:
            cur = cur + down(cur, k)
            sums.append(cur)
            k *= 2
        o_ref[...] = (_pool_select(g, sums) / _pool_window(g, t) - v).astype(BF16)

    return pl.pallas_call(
        body, name=name, grid=grid, in_specs=[spec], out_specs=spec,
        out_shape=jax.ShapeDtypeStruct((s, bw), BF16), compiler_params=_params("parallel", "parallel"),
    )(vb)


def _pool_bwd(name, dpooled):
    s, bw = dpooled.shape
    tc, grid, spec = _pool_specs(s, bw)

    def body(d_ref, o_ref):
        g = pl.program_id(0)
        dp = d_ref[...]
        t = lax.broadcasted_iota(jnp.int32, (s, tc), 0)

        def up(x, k):
            return jnp.where(t < s - k, pltpu.roll(x, s - k, 0), 0.0)

        sums, cur, k = [], dp / _pool_window(g, t), 1
        for _ in B_WINDOWS:
            cur = cur + up(cur, k)
            sums.append(cur)
            k *= 2
        o_ref[...] = (_pool_select(g, sums) - dp).astype(BF16)

    return pl.pallas_call(
        body, name=name, grid=grid, in_specs=[spec], out_specs=spec,
        out_shape=jax.ShapeDtypeStruct((s, bw), BF16), compiler_params=_params("parallel", "parallel"),
    )(dpooled)


def _scale_bwd(name, dms, mixed, scale):
    s, bw = dms.shape
    tr = _tile(s, ROW_TILE)

    def body(d_ref, m_ref, sc_ref, o_ref, ds_ref):
        @pl.when(pl.program_id(0) == 0)
        def _():
            ds_ref[...] = jnp.zeros_like(ds_ref)

        dv = d_ref[...]
        ds_ref[...] += jnp.sum(dv * m_ref[...], axis=0, keepdims=True)
        o_ref[...] = (dv * sc_ref[...]).astype(BF16)

    return pl.pallas_call(
        body, name=name, grid=(s // tr,), in_specs=[_row_spec(tr, bw), _row_spec(tr, bw), _vec_spec(bw)],
        out_specs=[_row_spec(tr, bw), _vec_spec(bw)],
        out_shape=[jax.ShapeDtypeStruct((s, bw), BF16), jax.ShapeDtypeStruct((1, bw), F32)],
        compiler_params=_params("arbitrary"),
    )(dms, mixed, scale)


def _adam_update(w, g, m, v):
    m = ADAM_B1 * m + (1.0 - ADAM_B1) * g
    v = ADAM_B2 * v + (1.0 - ADAM_B2) * (g * g)
    m_hat = m / (1.0 - ADAM_B1 ** ADAM_STEP)
    v_hat = v / (1.0 - ADAM_B2 ** ADAM_STEP)
    delta = -ADAM_LR * (m_hat / (jnp.sqrt(v_hat) + ADAM_EPS) + ADAM_WD * w)
    return delta, m, v


def _adam_shard(name, w, m, v, parts, layer=0, prev=None, deps=()):
    nl, r, c = w.shape
    nj, nlp = parts.shape[:2]
    tr = _tile(r, 128)
    spec = pl.BlockSpec((None, tr, c), lambda l, i: (layer + l, i, 0))

    def body(w_ref, m_ref, v_ref, p_ref, *rest):
        g_ref, d_ref, nm_ref, nv_ref = rest[-4:]
        g = p_ref[0].astype(F32)
        for j in range(1, nj):
            g = g + p_ref[j].astype(F32)
        delta, nm, nv = _adam_update(w_ref[...], g, m_ref[...], v_ref[...])
        g_ref[...] = g
        d_ref[...] = delta
        nm_ref[...] = nm
        nv_ref[...] = nv

    prev = () if prev is None else tuple(prev)
    return pl.pallas_call(
        body, name=name, grid=(nlp, r // tr),
        in_specs=[spec, spec, spec, pl.BlockSpec((nj, None, tr, c), lambda l, i: (0, l, i, 0))]
        + [_ANY] * (len(prev) + len(deps)),
        out_specs=[spec] * 4, out_shape=[jax.ShapeDtypeStruct(w.shape, F32)] * 4,
        input_output_aliases={4 + i: i for i in range(len(prev))},
        compiler_params=_params("parallel", "parallel"),
    )(w, m, v, parts, *prev, *deps)


def _adam_packed(name, w, g, m, v):
    rows, lanes = w.shape
    tr = _tile(rows, 512)
    spec = pl.BlockSpec((tr, lanes), lambda i: (i, 0))

    def body(w_ref, g_ref, m_ref, v_ref, d_ref, nm_ref, nv_ref):
        delta, nm, nv = _adam_update(w_ref[...], g_ref[...], m_ref[...], v_ref[...])
        d_ref[...] = delta
        nm_ref[...] = nm
        nv_ref[...] = nv

    return pl.pallas_call(
        body, name=name, grid=(rows // tr,), in_specs=[spec] * 4, out_specs=[spec] * 3,
        out_shape=[jax.ShapeDtypeStruct(w.shape, F32)] * 3, compiler_params=_params("parallel"),
    )(w, g, m, v)


def _place():
    x, y, c = lax.axis_index("x"), lax.axis_index("y"), lax.axis_index("c")
    chips = [(1 - x, y), (x, 1 - y), (1 - x, 1 - y)]
    return x, y, c, chips


def _remote(src, dst, send_sem, recv_sem, device):
    return pltpu.make_async_remote_copy(src_ref=src, dst_ref=dst, send_sem=send_sem, recv_sem=recv_sem,
                                        device_id=device, device_id_type=MESH)


def _half(ref_rows, cc):
    h = ref_rows // 2
    return pl.ds(cc * h, h)


_HBM = pl.BlockSpec(memory_space=pltpu.HBM)
_SEM = pl.BlockSpec(memory_space=pltpu.SEMAPHORE)
_EFFECT = pltpu.SideEffectType.DATAFLOW_SIDE_EFFECTING


def _in_hbm(arr):
    return pltpu.with_memory_space_constraint(arr, pltpu.HBM)


def _gather_block(land, shard_rows, split, j, cc):
    if split:
        return land.at[j, :, _half(shard_rows, cc), :]
    return land.at[j]


def _split_start(name, srcs, lands, plan, deps=()):
    ns, nl = len(srcs), len(lands)
    both = list(srcs) + list(lands)
    nb = ns + nl

    def body(*refs):
        s, ld = refs[:ns], refs[ns:nb]
        outs = refs[nb + len(deps):]
        send, recv, token = outs[:nl], outs[nl:2 * nl], outs[-1]
        for a, copies in enumerate(plan(s, ld)):
            for src, dst, peer in copies:
                _remote(src, dst, send[a], recv[a], peer).start()
        token[...] = jnp.zeros_like(token)

    outs = pl.pallas_call(
        body, name=name, in_specs=[_HBM] * nb + [_ANY] * len(deps),
        out_specs=[_SEM] * (2 * nl) + [_HBM] * nb + [pl.BlockSpec(memory_space=pltpu.VMEM)],
        out_shape=[pltpu.SemaphoreType.DMA(())] * (2 * nl) + [pltpu.HBM(b.shape, b.dtype) for b in both]
        + [jax.ShapeDtypeStruct((8, LANES), F32)],
        input_output_aliases={i: 2 * nl + i for i in range(nb)},
        compiler_params=pltpu.CompilerParams(has_side_effects=_EFFECT),
    )(*[_in_hbm(b) for b in both], *deps)
    thru = outs[2 * nl:2 * nl + nb]
    return list(outs[:nl]), list(outs[nl:2 * nl]), list(thru[:ns]), list(thru[ns:]), outs[-1]


def _split_wait(name, srcs, lands, send, recv, after, whole):
    ns, nl = len(srcs), len(lands)
    both = list(srcs) + list(lands)
    nb = ns + nl

    def body(*refs):
        ld, snd, rcv = refs[ns:nb], refs[nb:nb + nl], refs[nb + nl:nb + 2 * nl]
        x, y, c, _ = _place()
        for a, blk in enumerate(whole(ld)):
            every = _remote(blk, blk, snd[a], rcv[a], (x, y, c))
            every.wait_send()
            every.wait_recv()

    outs = pl.pallas_call(
        body, name=name, in_specs=[_HBM] * nb + [_SEM] * (2 * nl) + [_ANY], out_specs=[_HBM] * nb,
        out_shape=[pltpu.HBM(b.shape, b.dtype) for b in both],
        input_output_aliases={i: i for i in range(nb)},
        compiler_params=pltpu.CompilerParams(has_side_effects=_EFFECT),
    )(*both, *send, *recv, after)
    return list(outs[:ns]), list(outs[ns:])


def _gather_plan(split):
    def plan(srcs, lands):
        x, y, c, chips = _place()
        out = []
        for ld, sp in zip(lands, split):
            blk = _gather_block(ld, ld.shape[2], sp, 2 * x + y, c)
            out.append([(blk, blk, (qx, qy, c)) for qx, qy in chips])
        return out
    return plan


def _gather_whole(split):
    def whole(lands):
        _, _, c, _ = _place()
        return [ld.at[pl.ds(0, 3), :, _half(ld.shape[2], c), :] if sp else ld.at[pl.ds(0, 3)]
                for ld, sp in zip(lands, split)]
    return whole


def _relay_plan(srcs, lands):
    x, y, c, _ = _place()
    out = []
    for ld in lands:
        blocks = [ld.at[j, :, _half(ld.shape[2], c), :] for j in range(N_CHIPS)]
        out.append([(blk, blk, (x, y, 1 - c)) for blk in blocks])
    return out


def _relay_whole(lands):
    _, _, c, _ = _place()
    return [ld.at[:, :, _half(ld.shape[2], c), :] for ld in lands]


def _fill_own(name, shard, dtype, place):
    nl, r, c = shard.shape
    tr = _tile(r, 512)

    def body(p_ref, s_ref, o_ref):
        o_ref[...] = s_ref[...].astype(o_ref.dtype)

    return pl.pallas_call(
        body, name=name,
        grid_spec=pltpu.PrefetchScalarGridSpec(
            num_scalar_prefetch=1, grid=(nl, r // tr),
            in_specs=[pl.BlockSpec((None, tr, c), lambda l, i, p: (l, i, 0))],
            out_specs=pl.BlockSpec((None, None, tr, c), lambda l, i, p: (p[0], l, i, 0))),
        out_shape=jax.ShapeDtypeStruct((N_CHIPS, nl, r, c), dtype),
        compiler_params=_params("parallel", "parallel"),
    )(place, shard)


def _gather_finish(name, lands):
    n = len(lands)

    def body(*refs):
        outs = refs[n:2 * n]
        fsend, frecv = refs[2 * n:]
        x, y, c, chips = _place()
        sib = (x, y, 1 - c)

        def relay(a, qi, cc):
            qx, qy = chips[qi]
            blk = _gather_block(outs[a], outs[a].shape[2], True, 2 * qx + qy, cc)
            return _remote(blk, blk, fsend.at[a, qi], frecv.at[a, qi], sib)

        relays = [relay(a, qi, c) for a in range(n) for qi in range(3)]
        for cp in relays:
            cp.start()
        for a in range(n):
            for qi in range(3):
                relay(a, qi, 1 - c).wait_recv()
        for cp in relays:
            cp.wait_send()

    outs = pl.pallas_call(
        body, name=name, in_specs=[_ANY] * n, out_specs=[_ANY] * n,
        out_shape=[jax.ShapeDtypeStruct(ld.shape, ld.dtype) for ld in lands],
        input_output_aliases={i: i for i in range(n)},
        scratch_shapes=[pltpu.SemaphoreType.DMA((n, 3))] * 2,
    )(*lands)
    return list(outs)


def _pair_plan(srcs, lands):
    x, y, c, _ = _place()
    return [[(s.at[:, :, _half(s.shape[2], 1 - c), :], ld, (x, y, 1 - c))] for s, ld in zip(srcs, lands)]


def _pair_whole(lands):
    return list(lands)


def _scatter_plan(srcs, lands):
    x, y, c, chips = _place()
    return [[(s.at[2 * qx + qy], ld.at[2 * x + y, :, _half(ld.shape[2], c), :], (qx, qy, c)) for qx, qy in chips]
            for s, ld in zip(srcs, lands)]


def _scatter_whole(lands):
    _, _, c, _ = _place()
    return [ld.at[pl.ds(0, 3), :, _half(ld.shape[2], c), :] for ld in lands]


def _pair_sum(name, grad, other, place):
    nj, nl, r, c = grad.shape
    h = r // 2
    tr = _tile(h, 256)
    nb = h // tr

    def body(p_ref, g_ref, o_ref, q_ref, d_ref):
        q = (g_ref[...].astype(F32) + o_ref[...].astype(F32)).astype(BF16)
        q_ref[...] = q

        @pl.when(pl.program_id(2) == p_ref[0])
        def _():
            d_ref[...] = q

    blk = (None, None, tr, c)
    return pl.pallas_call(
        body, name=name,
        grid_spec=pltpu.PrefetchScalarGridSpec(
            num_scalar_prefetch=1, grid=(nl, nb, nj),
            in_specs=[pl.BlockSpec(blk, lambda l, i, j, p: (j, l, p[1] * nb + i, 0)),
                      pl.BlockSpec(blk, lambda l, i, j, p: (j, l, i, 0))],
            out_specs=[pl.BlockSpec(blk, lambda l, i, j, p: (j, l, i, 0)),
                       pl.BlockSpec(blk, lambda l, i, j, p: (p[0], l, p[1] * nb + i, 0))]),
        out_shape=[jax.ShapeDtypeStruct((nj, nl, h, c), BF16), jax.ShapeDtypeStruct((nj, nl, r, c), BF16)],
        compiler_params=_params("parallel", "parallel", "arbitrary"),
    )(place, grad, other)


def _allreduce_small(name, packed):
    rows, lanes = packed.shape

    def body(x_ref, o_ref, all_ref, send, recv, lsem):
        x, y, c, chips = _place()
        me, sib = (x, y, c), (x, y, 1 - c)

        def slot(px, py, pc):
            return all_ref.at[pl.ds((4 * px + 2 * py + pc) * rows, rows), :]

        def copy(k, blk, to, src=None):
            return _remote(slot(*blk) if src is None else src, slot(*blk), send.at[k], recv.at[k], to)

        mine = pltpu.make_async_copy(x_ref, slot(*me), lsem)
        mine.start()
        first = [copy(0, me, sib, src=x_ref)] + [copy(1 + j, me, (*chip, c), src=x_ref) for j, chip in enumerate(chips)]
        for cp in first:
            cp.start()
        passed = [copy(4 + j, (*chip, c), sib) for j, chip in enumerate(chips)]
        for j, chip in enumerate(chips):
            copy(1 + j, (*chip, c), me).wait_recv()
            passed[j].start()
        copy(0, sib, me).wait_recv()
        for j, chip in enumerate(chips):
            copy(4 + j, (*chip, 1 - c), me).wait_recv()
        for cp in first + passed:
            cp.wait_send()
        mine.wait()
        total = all_ref[pl.ds(0, rows), :]
        for d in range(1, N_DEV):
            total = total + all_ref[pl.ds(d * rows, rows), :]
        o_ref[...] = total

    vmem = pl.BlockSpec(memory_space=pltpu.VMEM)
    return pl.pallas_call(
        body, name=name, in_specs=[vmem], out_specs=vmem, out_shape=jax.ShapeDtypeStruct(packed.shape, F32),
        scratch_shapes=[pltpu.VMEM((N_DEV * rows, lanes), F32), pltpu.SemaphoreType.DMA((7,)),
                        pltpu.SemaphoreType.DMA((7,)), pltpu.SemaphoreType.DMA],
        compiler_params=pltpu.CompilerParams(vmem_limit_bytes=VMEM_LIMIT),
    )(packed)


def _pack(parts):
    rows = []
    for p in parts:
        flat = p.reshape(-1)
        pad = (-flat.shape[0]) % PACK_ELEMS
        rows.append(jnp.pad(flat, (0, pad)).reshape(-1, LANES))
    return jnp.concatenate(rows, axis=0)


def _unpack(packed, shapes):
    out, row = [], 0
    for sh in shapes:
        size = math.prod(sh)
        nrows = -(-size // PACK_ELEMS) * (PACK_ELEMS // LANES)
        out.append(packed[row:row + nrows].reshape(-1)[:size].reshape(sh))
        row += nrows
    return out


def kernel(x, a_w_in, a_ln_g, a_ln_b, a_w_s, a_b_s, a_w_out, b_w_in, b_w_grp, b_scale, b_w_out, norm_mix, norm_mlp, mlp_w1, mlp_w2, final_norm, loss_target, m_a_w_in, m_a_ln_g, m_a_ln_b, m_a_w_s, m_a_b_s, m_a_w_out, m_b_w_in, m_b_w_grp, m_b_scale, m_b_w_out, m_norm_mix, m_norm_mlp, m_mlp_w1, m_mlp_w2, m_final_norm, v_a_w_in, v_a_ln_g, v_a_ln_b, v_a_w_s, v_a_b_s, v_a_w_out, v_b_w_in, v_b_w_grp, v_b_scale, v_b_w_out, v_norm_mix, v_norm_mlp, v_mlp_w1, v_mlp_w2, v_final_norm):
    xi, yi, ci = lax.axis_index("x"), lax.axis_index("y"), lax.axis_index("c")
    chip = 2 * xi + yi
    place = jnp.stack([chip, ci]).astype(jnp.int32)
    x2, tgt = x[0], loss_target[0]
    bw = b_scale.shape[1] * N_CHIPS

    units = dict(a_w_in=a_w_in, a_w_out=a_w_out, w1_0=mlp_w1[0:1], w2_0=mlp_w2[0:1], b_scale=b_scale.reshape(1, 1, -1),
                 b_w_in=b_w_in, b_w_grp=b_w_grp[0], b_w_out=b_w_out, w1_1=mlp_w1[1:2], w2_1=mlp_w2[1:2])
    col_sharded = dict(a_w_in=True, a_w_out=False, b_w_in=False, b_w_grp=False, b_w_out=False,
                       w1_0=True, w2_0=False, w1_1=True, w2_1=False)
    in_flight, W = {}, {}

    def launch(tag, keys, deps):
        sp = [k != "b_scale" for k in keys]
        zones = [_fill_own(f"gather_own_{k}", units[k], BF16 if s else F32, place) for k, s in zip(keys, sp)]
        send, recv, _, zones, tok = _split_start(f"gather_start_{tag}", [], zones, _gather_plan(sp), deps)
        in_flight.update({k: state for k, *state in zip(keys, send, recv, zones, sp)})
        return tok

    def arrive(keys, after):
        send, recv, zones, sp = zip(*[in_flight[k] for k in keys])
        _, zones = _split_wait(f"gather_wait_{keys[0]}", [], zones, send, recv, after, _gather_whole(sp))
        relayed = iter(_gather_finish(f"gather_finish_{keys[0]}", [z for z, s in zip(zones, sp) if s]))
        for k, z, s in zip(keys, zones, sp):
            full = next(relayed) if s else z
            W[k] = _W4(full, col_sharded[k]) if k in col_sharded else full

    token = launch("first", ["a_w_in", "a_w_out"], ())
    token = launch("rest", ["w1_0", "w2_0", "b_scale", "b_w_in", "b_w_grp", "b_w_out", "w1_1", "w2_1"], (token,))

    b_col = a_b_s[0].T

    def residual(acc, res):
        return (res + acc,)

    def sq_relu(acc):
        act = jnp.maximum(acc, 0.0)
        return act, act * act

    def mlp_fwd(tag, h, layer):
        hn = _rms_fwd(f"mlp{tag}_norm", h, norm_mlp[layer:layer + 1])
        arrive([f"w1_{layer}"], hn)
        act, act_sq = _mm_aw(f"mlp{tag}_up", hn, W[f"w1_{layer}"], out_dtypes=(BF16, BF16), epilogue=sq_relu)
        arrive([f"w2_{layer}"], act_sq)
        out = _mm_aw(f"mlp{tag}_down", act_sq, W[f"w2_{layer}"], extras=(h,), epilogue=residual)
        return out, (h, hn, act, act_sq)

    hn0 = _rms_fwd("mix_a_norm", x2, norm_mix[0:1])
    arrive(["a_w_in"], token)
    zpre = _mm_aw("mix_a_in", hn0, W["a_w_in"])
    gated = _gate_fwd("mix_a_gate", zpre, a_ln_g, a_ln_b, a_w_s[0], b_col)
    arrive(["a_w_out"], gated)
    h1 = _mm_aw("mix_a_out", gated, W["a_w_out"], extras=(x2,), epilogue=residual)
    h2, mlp0 = mlp_fwd("0", h1, 0)
    hn2 = _rms_fwd("mix_b_norm", h2, norm_mix[1:2])
    arrive(["b_scale", "b_w_in"], hn2)
    scale_full = W["b_scale"].reshape(1, bw)
    vb = _mm_aw("mix_b_in", hn2, W["b_w_in"])
    pooled = _pool_fwd("mix_b_pool", vb)
    arrive(["b_w_grp", "b_w_out"], pooled)
    mixed, ms = _mm_aw("mix_b_grp", pooled, W["b_w_grp"], groups=len(B_WINDOWS), extras=(scale_full,),
                       out_dtypes=(F32, BF16), epilogue=lambda acc, sc: (acc, acc * sc))
    h3 = _mm_aw("mix_b_out", ms, W["b_w_out"], extras=(h2,), epilogue=residual)
    h4, mlp1 = mlp_fwd("1", h3, 1)
    dh4, dh4_b, d_final, loss_part = _final("loss_head", h4, final_norm.reshape(1, -1), tgt)
    g1_like = _W4(None, True, shape=(N_CHIPS, 1, *W["w1_0"].arr.shape[2:]))
    g2_like = _W4(None, False, shape=(N_CHIPS, 1, *W["w2_0"].arr.shape[2:]))

    def exchange(tag, gs):
        zones = [lax.empty((g.shape[0], g.shape[1], g.shape[2] // 2, g.shape[3]), g.dtype) for g in gs]
        send, recv, srcs, zones, tok = _split_start(f"pair_start_{tag}", gs, zones, _pair_plan)
        return (tag, send, recv, srcs, zones), tok

    def reduce(state, after):
        tag, send, recv, srcs, zones = state
        srcs, zones = _split_wait(f"pair_wait_{tag}", srcs, zones, send, recv, after, _pair_whole)
        both = [_pair_sum(f"pair_sum_{tag}_{i}", g, o, place) for i, (g, o) in enumerate(zip(srcs, zones))]
        send, recv, sums, dests, tok = _split_start(f"scatter_start_{tag}", [b[0] for b in both],
                                                    [b[1] for b in both], _scatter_plan)
        return (tag, send, recv, sums, dests), tok

    def relay(state, after):
        tag, send, recv, sums, dests = state
        _, dests = _split_wait(f"scatter_wait_{tag}", sums, dests, send, recv, after, _scatter_whole)
        send, recv, _, dests, tok = _split_start(f"relay_start_{tag}", [], dests, _relay_plan)
        return (tag, send, recv, dests), tok

    def land(state, after):
        tag, send, recv, dests = state
        return _split_wait(f"relay_wait_{tag}", [], dests, send, recv, after, _relay_whole)[1]

    def mlp_bwd(tag, dh, dh_b, saved, layer, deps, pending=None):
        h, hn, act, act_sq = saved
        dpre = _mm_aw(f"mlp{tag}_down_dx", dh_b, W[f"w2_{layer}"], transpose_w=True, extras=(act,),
                      out_dtypes=(BF16,), epilogue=lambda acc, a: (acc * (2.0 * a),), deps=deps)
        scattering, dw_deps = None, ()
        if pending is not None:
            scattering, tok = reduce(pending, dpre)
            dw_deps = (tok,)
        g_w2 = _mm_dw(f"mlp{tag}_down_dw", act_sq, dh_b, g2_like, deps=dw_deps)
        pair_w2, tok = exchange(f"w2_{layer}", [g_w2])
        dhn = _mm_aw(f"mlp{tag}_up_dx", dpre, W[f"w1_{layer}"], transpose_w=True, deps=(tok,))
        g_w1 = _mm_dw(f"mlp{tag}_up_dw", hn, dpre, g1_like)
        pair_w1, tok1 = exchange(f"w1_{layer}", [g_w1])
        scat_w2, tok2 = reduce(pair_w2, dhn)
        dh_in, dh_in_b, d_norm = _rms_bwd(f"mlp{tag}_norm_bwd", h, norm_mlp[layer:layer + 1], dhn, dh)
        return dh_in, dh_in_b, d_norm, pair_w1, scat_w2, (tok1, tok2), scattering

    dh3, dh3_b, d_norm_mlp1, pair_w1_1, scat_w2_1, toks, _ = mlp_bwd("1", dh4, dh4_b, mlp1, 1, ())
    dms = _mm_aw("mix_b_out_dx", dh3_b, W["b_w_out"], transpose_w=True, deps=toks)
    g_b_out = _mm_dw("mix_b_out_dw", ms, dh3_b, W["b_w_out"])
    scat_w1_1, tok = reduce(pair_w1_1, dms)
    dmixed, d_scale = _scale_bwd("mix_b_scale_bwd", dms, mixed, scale_full)
    dpooled = _mm_aw("mix_b_grp_dx", dmixed, W["b_w_grp"], groups=len(B_WINDOWS), transpose_w=True, deps=(tok,))
    g_b_grp = _mm_dw("mix_b_grp_dw", pooled, dmixed, W["b_w_grp"], groups=len(B_WINDOWS))
    dvb = _pool_bwd("mix_b_pool_bwd", dpooled)
    dhn2 = _mm_aw("mix_b_in_dx", dvb, W["b_w_in"], transpose_w=True)
    g_b_in = _mm_dw("mix_b_in_dw", hn2, dvb, W["b_w_in"])
    pair_b, tok = exchange("b", [g_b_out, g_b_grp, g_b_in])
    dh2, dh2_b, d_norm_mix1 = _rms_bwd("mix_b_norm_bwd", h2, norm_mix[1:2], dhn2, dh3)
    dh1, dh1_b, d_norm_mlp0, pair_w1_0, scat_w2_0, toks, scat_b = mlp_bwd("0", dh2, dh2_b, mlp0, 0, (tok,),
                                                                          pending=pair_b)
    dgated = _mm_aw("mix_a_out_dx", dh1_b, W["a_w_out"], transpose_w=True, deps=toks)
    g_a_out = _mm_dw("mix_a_out_dw", gated, dh1_b, W["a_w_out"])
    pair_a_out, tok_a = exchange("a_out", [g_a_out])
    scat_w1_0, tok = reduce(pair_w1_0, dgated)
    dzpre, d_w_s, d_b_col, d_ln_g, d_ln_b = _gate_bwd("mix_a_gate_bwd", zpre, dgated, a_ln_g, a_ln_b, a_w_s[0], b_col)
    dhn0 = _mm_aw("mix_a_in_dx", dzpre, W["a_w_in"], transpose_w=True, deps=(tok, tok_a))
    scat_a_out, tok = reduce(pair_a_out, dhn0)
    g_a_in = _mm_dw("mix_a_in_dw", hn0, dzpre, W["a_w_in"], deps=(tok,))
    pair_a_in, tok = exchange("a_in", [g_a_in])
    dx, _, d_norm_mix0 = _rms_bwd("mix_a_norm_bwd", x2, norm_mix[0:1], dhn0, dh1)
    scat_a_in, _ = reduce(pair_a_in, dx)

    moments = dict(a_w_in=(m_a_w_in, v_a_w_in), a_w_out=(m_a_w_out, v_a_w_out), b_w_in=(m_b_w_in, v_b_w_in),
                   b_w_grp=(m_b_w_grp, v_b_w_grp), b_w_out=(m_b_w_out, v_b_w_out),
                   mlp_w1=(m_mlp_w1, v_mlp_w1), mlp_w2=(m_mlp_w2, v_mlp_w2))
    weights = dict(a_w_in=a_w_in, a_w_out=a_w_out, b_w_in=b_w_in, b_w_grp=b_w_grp, b_w_out=b_w_out,
                   mlp_w1=mlp_w1, mlp_w2=mlp_w2)
    landing = [(scat_w2_1, [("mlp_w2", 1)]), (scat_w1_1, [("mlp_w1", 1)]),
               (scat_b, [("b_w_out", 0), ("b_w_grp", 0), ("b_w_in", 0)]),
               (scat_w2_0, [("mlp_w2", 0)]), (scat_w1_0, [("mlp_w1", 0)]),
               (scat_a_out, [("a_w_out", 0)]), (scat_a_in, [("a_w_in", 0)])]
    results, after = {}, dx
    relaying, tok = relay(landing[0][0], after)
    for i, (_, members) in enumerate(landing):
        arrived, deps = relaying, ()
        if i + 1 < len(landing):
            relaying, tok = relay(landing[i + 1][0], after)
            deps = (tok,)
        for (k, layer), parts in zip(members, land(arrived, tok)):
            shard_shape = (-1, *parts.shape[2:])
            results[k] = _adam_shard(f"adam_{k}_{layer}", weights[k].reshape(shard_shape),
                                     moments[k][0].reshape(shard_shape), moments[k][1].reshape(shard_shape),
                                     parts, layer=layer, prev=results.get(k), deps=deps)
            after = results[k][1]
    grad_out, delta_out, m_out, v_out = {}, {}, {}, {}
    for k, res in results.items():
        grad_out[k], delta_out[k], m_out[k], v_out[k] = [r.reshape(weights[k].shape) for r in res]

    small = dict(a_ln_g=(a_ln_g, m_a_ln_g, v_a_ln_g), a_ln_b=(a_ln_b, m_a_ln_b, v_a_ln_b),
                 a_w_s=(a_w_s, m_a_w_s, v_a_w_s), a_b_s=(a_b_s, m_a_b_s, v_a_b_s),
                 b_scale=(b_scale, m_b_scale, v_b_scale), norm_mix=(norm_mix, m_norm_mix, v_norm_mix),
                 norm_mlp=(norm_mlp, m_norm_mlp, v_norm_mlp), final_norm=(final_norm, m_final_norm, v_final_norm))
    small_names = list(small)
    local = dict(a_ln_g=d_ln_g, a_ln_b=d_ln_b, a_w_s=d_w_s[None], a_b_s=d_b_col.T[None], b_scale=d_scale,
                 norm_mix=jnp.concatenate([d_norm_mix0, d_norm_mix1], axis=0),
                 norm_mlp=jnp.concatenate([d_norm_mlp0, d_norm_mlp1], axis=0), final_norm=d_final.reshape(-1))
    reduced = _allreduce_small("small_grad_allreduce", _pack([local[k] for k in small_names]))
    small_grads = dict(zip(small_names, _unpack(reduced, [local[k].shape for k in small_names])))
    shard_w = b_scale.shape[1]
    small_grads["b_scale"] = lax.dynamic_slice_in_dim(small_grads["b_scale"], chip * shard_w, shard_w, axis=1)
    small_grads = {k: small_grads[k].reshape(small[k][0].shape) for k in small_names}
    packed = [_pack([small[k][i] for k in small_names]) for i in range(3)]
    res = _adam_packed("adam_small", packed[0], _pack([small_grads[k] for k in small_names]), packed[1], packed[2])
    shapes = [small[k][0].shape for k in small_names]
    for k, d, nm, nv in zip(small_names, *[_unpack(r, shapes) for r in res]):
        grad_out[k], delta_out[k], m_out[k], v_out[k] = small_grads[k], d, nm, nv

    loss = lax.psum(loss_part[0, 0], ("x", "y", "c"))
    order = ["a_w_in", "a_ln_g", "a_ln_b", "a_w_s", "a_b_s", "a_w_out", "b_w_in", "b_w_grp", "b_scale", "b_w_out",
             "norm_mix", "norm_mlp", "mlp_w1", "mlp_w2", "final_norm"]
    return (loss, dx[None], *[grad_out[k] for k in order], *[delta_out[k] for k in order],
            *[m_out[k] for k in order], *[v_out[k] for k in order])
```

```python
import math

import jax
import jax.numpy as jnp
from jax import lax
from jax.experimental import pallas as pl
from jax.experimental.pallas import tpu as pltpu

F32 = jnp.float32
BF16 = jnp.bfloat16
MESH = pl.DeviceIdType.MESH

EPS = 1e-6
B_WINDOWS = (2, 4, 8, 16)
ADAM_LR = 0.001
ADAM_B1 = 0.9
ADAM_B2 = 0.999
ADAM_EPS = 1e-08
ADAM_WD = 0.01
ADAM_STEP = 10

N_CHIPS = 4
N_DEV = 8
LANES = 128
PACK_ELEMS = 8 * LANES
VMEM_LIMIT = 56 * 1024 * 1024
ROW_TILE = 256
MM_TM, MM_TN, MM_TK = 1024, 1024, 2048


_ANY = pl.BlockSpec(memory_space=pl.ANY)


def _tile(dim, pref):
    t = min(dim, pref)
    while dim % t:
        t //= 2
    return t


def _params(*sem):
    return pltpu.CompilerParams(dimension_semantics=sem, vmem_limit_bytes=VMEM_LIMIT)


class _W4:
    def __init__(self, arr, col_sharded, shape=None):
        self.arr = arr
        self.nj, self.nl, self.r, self.c = arr.shape if shape is None else shape
        self.col = col_sharded
        self.rows = self.r if col_sharded else self.nj * self.r
        self.cols = self.nj * self.c if col_sharded else self.c

    def tile_rows(self, pref):
        return _tile(self.r, pref)

    def tile_cols(self, pref):
        return _tile(self.c, pref)

    def index(self, layer, rb, cb, tr, tc):
        if self.col:
            n = self.c // tc
            return (cb // n, layer, rb, cb % n)
        n = self.r // tr
        return (rb // n, layer, rb % n, cb)


def _mm_aw(name, a, w, *, layer=0, groups=1, transpose_w=False, extras=(), out_dtypes=(F32,), epilogue=None,
           deps=()):
    s, ka_total = a.shape
    kdim, ndim = (w.cols, w.rows) if transpose_w else (w.rows, w.cols)
    assert ka_total == groups * kdim, (name, a.shape, kdim, groups)
    span = groups == 1 and not transpose_w and not w.col and kdim <= MM_TK
    if groups > 1:
        tm = _tile(s, MM_TM)
        tn, tk = (w.tile_rows(MM_TN), w.tile_cols(512)) if transpose_w else (w.tile_cols(MM_TN), w.tile_rows(512))
    else:
        tk = kdim if span else (w.tile_cols(MM_TK) if transpose_w else w.tile_rows(MM_TK))
        tm, tn_pref = (_tile(s, 2048), 512) if tk == kdim else (_tile(s, MM_TM), MM_TN)
        tn = w.tile_rows(tn_pref) if transpose_w else w.tile_cols(tn_pref)
    nk, nn = kdim // tk, ndim // tn

    def lay(g):
        return g if groups > 1 else layer

    a_spec = pl.BlockSpec((tm, tk), lambda g, i, n, k: (i, g * nk + k))
    if span:
        w_spec = pl.BlockSpec((w.nj, None, w.r, tn), lambda g, i, n, k: (0, layer, 0, n))
    elif transpose_w:
        w_spec = pl.BlockSpec((None, None, tn, tk), lambda g, i, n, k: w.index(lay(g), n, k, tn, tk))
    else:
        w_spec = pl.BlockSpec((None, None, tk, tn), lambda g, i, n, k: w.index(lay(g), k, n, tk, tn))
    ex_specs = []
    for e in extras:
        assert e.shape[1] == groups * ndim and e.shape[0] in (1, s), (name, e.shape)
        if e.shape[0] == 1:
            ex_specs.append(pl.BlockSpec((1, tn), lambda g, i, n, k: (0, g * nn + n)))
        else:
            ex_specs.append(pl.BlockSpec((tm, tn), lambda g, i, n, k: (i, g * nn + n)))
    out_spec = pl.BlockSpec((tm, tn), lambda g, i, n, k: (i, g * nn + n))
    n_ex, n_out, n_dep = len(extras), len(out_dtypes), len(deps)

    def body(a_ref, w_ref, *rest):
        ex, outs = rest[:n_ex], rest[n_ex + n_dep:n_ex + n_dep + n_out]
        av = a_ref[...]
        if av.dtype != BF16:
            av = av.astype(BF16)
        wv = w_ref[...].reshape(tk, tn) if span else w_ref[...]
        if transpose_w:
            prod = lax.dot_general(av, wv, (((1,), (1,)), ((), ())), preferred_element_type=F32)
        else:
            prod = jnp.dot(av, wv, preferred_element_type=F32)

        def finish(total):
            vals = (total,) if epilogue is None else epilogue(total, *[e[...] for e in ex])
            for o, v in zip(outs, vals):
                o[...] = v.astype(o.dtype)

        if nk == 1:
            finish(prod)
            return
        acc, k = rest[-1], pl.program_id(3)

        @pl.when(k == 0)
        def _():
            acc[...] = prod

        @pl.when(k > 0)
        def _():
            acc[...] += prod

        @pl.when(k == nk - 1)
        def _():
            finish(acc[...])

    outs = pl.pallas_call(
        body, name=name, grid=(groups, s // tm, nn, nk),
        in_specs=[a_spec, w_spec, *ex_specs] + [_ANY] * n_dep, out_specs=[out_spec] * n_out,
        out_shape=[jax.ShapeDtypeStruct((s, groups * ndim), dt) for dt in out_dtypes],
        scratch_shapes=[pltpu.VMEM((tm, tn), F32)] if nk > 1 else [],
        compiler_params=_params("parallel", "parallel", "parallel", "arbitrary"),
    )(a, w.arr, *extras, *deps)
    return outs[0] if n_out == 1 else outs


def _mm_dw(name, a, b, like, *, layer=0, groups=1, deps=()):
    s, ka_total = a.shape
    rows, cols = ka_total // groups, b.shape[1] // groups
    assert (rows, cols) == (like.rows, like.cols) and b.shape[0] == s, (name, a.shape, b.shape)
    tm, tn, tk = like.tile_rows(MM_TM), like.tile_cols(MM_TN), _tile(s, MM_TK)
    nr, nc, nk = rows // tm, cols // tn, s // tk

    def lay(g):
        return g if groups > 1 else layer

    in_specs = [pl.BlockSpec((tk, tm), lambda g, n, i, k: (k, g * nr + i)),
                pl.BlockSpec((tk, tn), lambda g, n, i, k: (k, g * nc + n))]
    in_specs += [_ANY] * len(deps)

    def body(a_ref, b_ref, *rest):
        av, bv = a_ref[...], b_ref[...]
        if av.dtype != BF16:
            av = av.astype(BF16)
        if bv.dtype != BF16:
            bv = bv.astype(BF16)
        prod = lax.dot_general(av, bv, (((0,), (0,)), ((), ())), preferred_element_type=F32)
        if nk == 1:
            rest[-1][...] = prod.astype(BF16)
            return
        o_ref, acc, k = rest[-2], rest[-1], pl.program_id(3)

        @pl.when(k == 0)
        def _():
            acc[...] = prod

        @pl.when(k > 0)
        def _():
            acc[...] += prod

        @pl.when(k == nk - 1)
        def _():
            o_ref[...] = acc[...].astype(BF16)

    return pl.pallas_call(
        body, name=name, grid=(groups, nc, nr, nk), in_specs=in_specs,
        out_specs=pl.BlockSpec((None, None, tm, tn), lambda g, n, i, k: like.index(lay(g), i, n, tm, tn)),
        out_shape=jax.ShapeDtypeStruct((like.nj, like.nl, like.r, like.c), BF16),
        scratch_shapes=[pltpu.VMEM((tm, tn), F32)] if nk > 1 else [],
        compiler_params=_params("parallel", "parallel", "parallel", "arbitrary"),
    )(a, b, *deps)


def _row_spec(tr, d):
    return pl.BlockSpec((tr, d), lambda i: (i, 0))


def _vec_spec(d):
    return pl.BlockSpec((1, d), lambda i: (0, 0))


def _rms_fwd(name, x, g):
    s, d = x.shape
    tr = _tile(s, ROW_TILE)

    def body(x_ref, g_ref, o_ref):
        xv = x_ref[...]
        r = lax.rsqrt(jnp.mean(xv * xv, axis=-1, keepdims=True) + EPS)
        o_ref[...] = (xv * r * g_ref[...]).astype(BF16)

    return pl.pallas_call(
        body, name=name, grid=(s // tr,), in_specs=[_row_spec(tr, d), _vec_spec(d)],
        out_specs=_row_spec(tr, d), out_shape=jax.ShapeDtypeStruct((s, d), BF16),
        compiler_params=_params("parallel"),
    )(x, g)


def _rms_bwd(name, x, g, dhn, dres):
    s, d = x.shape
    tr = _tile(s, ROW_TILE)

    def body(x_ref, g_ref, dhn_ref, dres_ref, dx_ref, dxb_ref, dg_ref):
        @pl.when(pl.program_id(0) == 0)
        def _():
            dg_ref[...] = jnp.zeros_like(dg_ref)

        xv = x_ref[...]
        r = lax.rsqrt(jnp.mean(xv * xv, axis=-1, keepdims=True) + EPS)
        xh = xv * r
        dy = dhn_ref[...]
        dg_ref[...] += jnp.sum(dy * xh, axis=0, keepdims=True)
        dxh = dy * g_ref[...]
        dx = dres_ref[...] + r * (dxh - xh * jnp.mean(dxh * xh, axis=-1, keepdims=True))
        dx_ref[...] = dx
        dxb_ref[...] = dx.astype(BF16)

    return pl.pallas_call(
        body, name=name, grid=(s // tr,),
        in_specs=[_row_spec(tr, d), _vec_spec(d), _row_spec(tr, d), _row_spec(tr, d)],
        out_specs=[_row_spec(tr, d), _row_spec(tr, d), _vec_spec(d)],
        out_shape=[jax.ShapeDtypeStruct((s, d), F32), jax.ShapeDtypeStruct((s, d), BF16),
                   jax.ShapeDtypeStruct((1, d), F32)],
        compiler_params=_params("arbitrary"),
    )(x, g, dhn, dres)


def _final(name, h, g, tgt):
    s, d = h.shape
    tr = _tile(s, ROW_TILE)

    def body(h_ref, g_ref, t_ref, dh_ref, dhb_ref, dg_ref, loss_ref):
        @pl.when(pl.program_id(0) == 0)
        def _():
            dg_ref[...] = jnp.zeros_like(dg_ref)
            loss_ref[...] = jnp.zeros_like(loss_ref)

        hv = h_ref[...]
        r = lax.rsqrt(jnp.mean(hv * hv, axis=-1, keepdims=True) + EPS)
        xh = hv * r
        gv = g_ref[...]
        err = xh * gv - t_ref[...]
        part = 0.5 * jnp.sum(jnp.mean(err * err, axis=-1, keepdims=True), axis=0, keepdims=True)
        loss_ref[...] += jnp.broadcast_to(part, loss_ref.shape)
        dy = err * (1.0 / d)
        dg_ref[...] += jnp.sum(dy * xh, axis=0, keepdims=True)
        dxh = dy * gv
        dh = r * (dxh - xh * jnp.mean(dxh * xh, axis=-1, keepdims=True))
        dh_ref[...] = dh
        dhb_ref[...] = dh.astype(BF16)

    return pl.pallas_call(
        body, name=name, grid=(s // tr,),
        in_specs=[_row_spec(tr, d), _vec_spec(d), _row_spec(tr, d)],
        out_specs=[_row_spec(tr, d), _row_spec(tr, d), _vec_spec(d), _vec_spec(LANES)],
        out_shape=[jax.ShapeDtypeStruct((s, d), F32), jax.ShapeDtypeStruct((s, d), BF16),
                   jax.ShapeDtypeStruct((1, d), F32), jax.ShapeDtypeStruct((1, LANES), F32)],
        compiler_params=_params("arbitrary"),
    )(h, g, tgt)


_SQRT_HALF = 1.0 / math.sqrt(2.0)
_INV_SQRT_2PI = 1.0 / math.sqrt(2.0 * math.pi)


def _gelu(x):
    return x * (lax.erf(x * _SQRT_HALF) + 1.0) * 0.5


def _gelu_grad(x):
    return 0.5 * (lax.erf(x * _SQRT_HALF) + 1.0) + x * jnp.exp(-0.5 * x * x) * _INV_SQRT_2PI


def _causal(chunk):
    t = lax.broadcasted_iota(jnp.int32, (chunk, chunk), 0)
    sidx = lax.broadcasted_iota(jnp.int32, (chunk, chunk), 1)
    return sidx <= t


def _layernorm_parts(v, g, b):
    mu = jnp.mean(v, axis=-1, keepdims=True)
    vc = v - mu
    rs = lax.rsqrt(jnp.mean(vc * vc, axis=-1, keepdims=True) + EPS)
    vhat = vc * rs
    return vhat, rs, vhat * g + b


def _gate_fwd(name, zpre, ln_g, ln_b, w_s, b_col):
    s, aw2 = zpre.shape
    aw = aw2 // 2
    ng, chunk, _ = w_s.shape
    dh = aw // ng

    def body(z_ref, g_ref, b_ref, ws_ref, bc_ref, o_ref):
        u = _gelu(z_ref[:, :aw])
        v = _gelu(z_ref[:, aw:])
        _, _, vln = _layernorm_parts(v, g_ref[...], b_ref[...])
        mask = _causal(chunk)
        for gi in range(ng):
            sl = slice(gi * dh, (gi + 1) * dh)
            wm = jnp.where(mask, ws_ref[gi], 0.0).astype(BF16)
            sg = jnp.dot(wm, vln[:, sl].astype(BF16), preferred_element_type=F32) + bc_ref[:, gi:gi + 1]
            o_ref[:, sl] = (u[:, sl] * sg).astype(BF16)

    return pl.pallas_call(
        body, name=name, grid=(s // chunk,),
        in_specs=[_row_spec(chunk, aw2), _vec_spec(aw), _vec_spec(aw),
                  pl.BlockSpec((ng, chunk, chunk), lambda i: (0, 0, 0)),
                  pl.BlockSpec((chunk, ng), lambda i: (0, 0))],
        out_specs=_row_spec(chunk, aw), out_shape=jax.ShapeDtypeStruct((s, aw), BF16),
        compiler_params=_params("parallel"),
    )(zpre, ln_g, ln_b, w_s, b_col)


def _gate_bwd(name, zpre, dgated, ln_g, ln_b, w_s, b_col):
    s, aw2 = zpre.shape
    aw = aw2 // 2
    ng, chunk, _ = w_s.shape
    dh = aw // ng

    def body(z_ref, dgt_ref, g_ref, b_ref, ws_ref, bc_ref, dz_ref, dws_ref, dbc_ref, dlg_ref, dlb_ref,
             du_scr, dvln_scr):
        @pl.when(pl.program_id(0) == 0)
        def _():
            dws_ref[...] = jnp.zeros_like(dws_ref)
            dbc_ref[...] = jnp.zeros_like(dbc_ref)
            dlg_ref[...] = jnp.zeros_like(dlg_ref)
            dlb_ref[...] = jnp.zeros_like(dlb_ref)

        zu = z_ref[:, :aw]
        zv = z_ref[:, aw:]
        u = _gelu(zu)
        lg = g_ref[...]
        vhat, rs, vln = _layernorm_parts(_gelu(zv), lg, b_ref[...])
        mask = _causal(chunk)
        for gi in range(ng):
            sl = slice(gi * dh, (gi + 1) * dh)
            wm = jnp.where(mask, ws_ref[gi], 0.0).astype(BF16)
            vg = vln[:, sl].astype(BF16)
            sg = jnp.dot(wm, vg, preferred_element_type=F32) + bc_ref[:, gi:gi + 1]
            dgt = dgt_ref[:, sl]
            du_scr[:, sl] = dgt * sg
            ds = dgt * u[:, sl]
            dbc_ref[:, gi:gi + 1] += jnp.sum(ds, axis=-1, keepdims=True)
            dsb = ds.astype(BF16)
            dwm = lax.dot_general(dsb, vg, (((1,), (1,)), ((), ())), preferred_element_type=F32)
            dws_ref[gi] += jnp.where(mask, dwm, 0.0)
            dvln_scr[:, sl] = lax.dot_general(wm, dsb, (((0,), (0,)), ((), ())), preferred_element_type=F32)
        dvln = dvln_scr[...]
        dlb_ref[...] += jnp.sum(dvln, axis=0, keepdims=True)
        dlg_ref[...] += jnp.sum(dvln * vhat, axis=0, keepdims=True)
        dvh = dvln * lg
        dv = rs * (dvh - jnp.mean(dvh, axis=-1, keepdims=True)
                   - vhat * jnp.mean(dvh * vhat, axis=-1, keepdims=True))
        dz_ref[:, :aw] = (du_scr[...] * _gelu_grad(zu)).astype(BF16)
        dz_ref[:, aw:] = (dv * _gelu_grad(zv)).astype(BF16)

    return pl.pallas_call(
        body, name=name, grid=(s // chunk,),
        in_specs=[_row_spec(chunk, aw2), _row_spec(chunk, aw), _vec_spec(aw), _vec_spec(aw),
                  pl.BlockSpec((ng, chunk, chunk), lambda i: (0, 0, 0)),
                  pl.BlockSpec((chunk, ng), lambda i: (0, 0))],
        out_specs=[_row_spec(chunk, aw2), pl.BlockSpec((ng, chunk, chunk), lambda i: (0, 0, 0)),
                   pl.BlockSpec((chunk, ng), lambda i: (0, 0)), _vec_spec(aw), _vec_spec(aw)],
        out_shape=[jax.ShapeDtypeStruct((s, aw2), BF16), jax.ShapeDtypeStruct((ng, chunk, chunk), F32),
                   jax.ShapeDtypeStruct((chunk, ng), F32), jax.ShapeDtypeStruct((1, aw), F32),
                   jax.ShapeDtypeStruct((1, aw), F32)],
        scratch_shapes=[pltpu.VMEM((chunk, aw), F32), pltpu.VMEM((chunk, aw), F32)],
        compiler_params=_params("arbitrary"),
    )(zpre, dgated, ln_g, ln_b, w_s, b_col)


def _pool_select(g, parts):
    out = parts[-1]
    for gi in range(len(parts) - 2, -1, -1):
        out = jnp.where(g == gi, parts[gi], out)
    return out


def _pool_specs(s, bw):
    head = bw // len(B_WINDOWS)
    tc = _tile(head, 256)
    nb = head // tc
    return tc, (len(B_WINDOWS), nb), pl.BlockSpec((s, tc), lambda g, j: (0, g * nb + j))


def _pool_window(g, t):
    w = _pool_select(g, [jnp.full(t.shape, wi, jnp.int32) for wi in B_WINDOWS])
    return jnp.minimum(t + 1, w).astype(F32)


def _pool_fwd(name, vb):
    assert B_WINDOWS == (2, 4, 8, 16)
    s, bw = vb.shape
    tc, grid, spec = _pool_specs(s, bw)

    def body(v_ref, o_ref):
        g = pl.program_id(0)
        v = v_ref[...]
        t = lax.broadcasted_iota(jnp.int32, (s, tc), 0)

        def down(x, k):
            return jnp.where(t >= k, pltpu.roll(x, k, 0), 0.0)

        sums, cur, k = [], v, 1
        for _ in B_WINDOWS:
            cur = cur + down(cur, k)
            sums.append(cur)
            k *= 2
        o_ref[...] = (_pool_select(g, sums) / _pool_window(g, t) - v).astype(BF16)

    return pl.pallas_call(
        body, name=name, grid=grid, in_specs=[spec], out_specs=spec,
        out_shape=jax.ShapeDtypeStruct((s, bw), BF16), compiler_params=_params("parallel", "parallel"),
    )(vb)


def _pool_bwd(name, dpooled):
    s, bw = dpooled.shape
    tc, grid, spec = _pool_specs(s, bw)

    def body(d_ref, o_ref):
        g = pl.program_id(0)
        dp = d_ref[...]
        t = lax.broadcasted_iota(jnp.int32, (s, tc), 0)

        def up(x, k):
            return jnp.where(t < s - k, pltpu.roll(x, s - k, 0), 0.0)

        sums, cur, k = [], dp / _pool_window(g, t), 1
        for _ in B_WINDOWS:
            cur = cur + up(cur, k)
            sums.append(cur)
            k *= 2
        o_ref[...] = (_pool_select(g, sums) - dp).astype(BF16)

    return pl.pallas_call(
        body, name=name, grid=grid, in_specs=[spec], out_specs=spec,
        out_shape=jax.ShapeDtypeStruct((s, bw), BF16), compiler_params=_params("parallel", "parallel"),
    )(dpooled)


def _scale_bwd(name, dms, mixed, scale):
    s, bw = dms.shape
    tr = _tile(s, ROW_TILE)

    def body(d_ref, m_ref, sc_ref, o_ref, ds_ref):
        @pl.when(pl.program_id(0) == 0)
        def _():
            ds_ref[...] = jnp.zeros_like(ds_ref)

        dv = d_ref[...]
        ds_ref[...] += jnp.sum(dv * m_ref[...], axis=0, keepdims=True)
        o_ref[...] = (dv * sc_ref[...]).astype(BF16)

    return pl.pallas_call(
        body, name=name, grid=(s // tr,), in_specs=[_row_spec(tr, bw), _row_spec(tr, bw), _vec_spec(bw)],
        out_specs=[_row_spec(tr, bw), _vec_spec(bw)],
        out_shape=[jax.ShapeDtypeStruct((s, bw), BF16), jax.ShapeDtypeStruct((1, bw), F32)],
        compiler_params=_params("arbitrary"),
    )(dms, mixed, scale)


def _adam_update(w, g, m, v):
    m = ADAM_B1 * m + (1.0 - ADAM_B1) * g
    v = ADAM_B2 * v + (1.0 - ADAM_B2) * (g * g)
    m_hat = m / (1.0 - ADAM_B1 ** ADAM_STEP)
    v_hat = v / (1.0 - ADAM_B2 ** ADAM_STEP)
    delta = -ADAM_LR * (m_hat / (jnp.sqrt(v_hat) + ADAM_EPS) + ADAM_WD * w)
    return delta, m, v


def _adam_shard(name, w, m, v, parts, layer=0, prev=None, deps=()):
    nl, r, c = w.shape
    nj, nlp = parts.shape[:2]
    tr = _tile(r, 128)
    spec = pl.BlockSpec((None, tr, c), lambda l, i: (layer + l, i, 0))

    def body(w_ref, m_ref, v_ref, p_ref, *rest):
        g_ref, d_ref, nm_ref, nv_ref = rest[-4:]
        g = p_ref[0].astype(F32)
        for j in range(1, nj):
            g = g + p_ref[j].astype(F32)
        delta, nm, nv = _adam_update(w_ref[...], g, m_ref[...], v_ref[...])
        g_ref[...] = g
        d_ref[...] = delta
        nm_ref[...] = nm
        nv_ref[...] = nv

    prev = () if prev is None else tuple(prev)
    return pl.pallas_call(
        body, name=name, grid=(nlp, r // tr),
        in_specs=[spec, spec, spec, pl.BlockSpec((nj, None, tr, c), lambda l, i: (0, l, i, 0))]
        + [_ANY] * (len(prev) + len(deps)),
        out_specs=[spec] * 4, out_shape=[jax.ShapeDtypeStruct(w.shape, F32)] * 4,
        input_output_aliases={4 + i: i for i in range(len(prev))},
        compiler_params=_params("parallel", "parallel"),
    )(w, m, v, parts, *prev, *deps)


def _adam_packed(name, w, g, m, v):
    rows, lanes = w.shape
    tr = _tile(rows, 512)
    spec = pl.BlockSpec((tr, lanes), lambda i: (i, 0))

    def body(w_ref, g_ref, m_ref, v_ref, d_ref, nm_ref, nv_ref):
        delta, nm, nv = _adam_update(w_ref[...], g_ref[...], m_ref[...], v_ref[...])
        d_ref[...] = delta
        nm_ref[...] = nm
        nv_ref[...] = nv

    return pl.pallas_call(
        body, name=name, grid=(rows // tr,), in_specs=[spec] * 4, out_specs=[spec] * 3,
        out_shape=[jax.ShapeDtypeStruct(w.shape, F32)] * 3, compiler_params=_params("parallel"),
    )(w, g, m, v)


def _place():
    x, y, c = lax.axis_index("x"), lax.axis_index("y"), lax.axis_index("c")
    chips = [(1 - x, y), (x, 1 - y), (1 - x, 1 - y)]
    return x, y, c, chips


def _remote(src, dst, send_sem, recv_sem, device):
    return pltpu.make_async_remote_copy(src_ref=src, dst_ref=dst, send_sem=send_sem, recv_sem=recv_sem,
                                        device_id=device, device_id_type=MESH)


def _half(ref_rows, cc):
    h = ref_rows // 2
    return pl.ds(cc * h, h)


_HBM = pl.BlockSpec(memory_space=pltpu.HBM)
_SEM = pl.BlockSpec(memory_space=pltpu.SEMAPHORE)
_EFFECT = pltpu.SideEffectType.DATAFLOW_SIDE_EFFECTING


def _in_hbm(arr):
    return pltpu.with_memory_space_constraint(arr, pltpu.HBM)


def _gather_block(land, shard_rows, split, j, cc):
    if split:
        return land.at[j, :, _half(shard_rows, cc), :]
    return land.at[j]


def _split_start(name, srcs, lands, plan, deps=()):
    ns, nl = len(srcs), len(lands)
    both = list(srcs) + list(lands)
    nb = ns + nl

    def body(*refs):
        s, ld = refs[:ns], refs[ns:nb]
        outs = refs[nb + len(deps):]
        send, recv, token = outs[:nl], outs[nl:2 * nl], outs[-1]
        for a, copies in enumerate(plan(s, ld)):
            for src, dst, peer in copies:
                _remote(src, dst, send[a], recv[a], peer).start()
        token[...] = jnp.zeros_like(token)

    outs = pl.pallas_call(
        body, name=name, in_specs=[_HBM] * nb + [_ANY] * len(deps),
        out_specs=[_SEM] * (2 * nl) + [_HBM] * nb + [pl.BlockSpec(memory_space=pltpu.VMEM)],
        out_shape=[pltpu.SemaphoreType.DMA(())] * (2 * nl) + [pltpu.HBM(b.shape, b.dtype) for b in both]
        + [jax.ShapeDtypeStruct((8, LANES), F32)],
        input_output_aliases={i: 2 * nl + i for i in range(nb)},
        compiler_params=pltpu.CompilerParams(has_side_effects=_EFFECT),
    )(*[_in_hbm(b) for b in both], *deps)
    thru = outs[2 * nl:2 * nl + nb]
    return list(outs[:nl]), list(outs[nl:2 * nl]), list(thru[:ns]), list(thru[ns:]), outs[-1]


def _split_wait(name, srcs, lands, send, recv, after, whole):
    ns, nl = len(srcs), len(lands)
    both = list(srcs) + list(lands)
    nb = ns + nl

    def body(*refs):
        ld, snd, rcv = refs[ns:nb], refs[nb:nb + nl], refs[nb + nl:nb + 2 * nl]
        x, y, c, _ = _place()
        for a, blk in enumerate(whole(ld)):
            every = _remote(blk, blk, snd[a], rcv[a], (x, y, c))
            every.wait_send()
            every.wait_recv()

    outs = pl.pallas_call(
        body, name=name, in_specs=[_HBM] * nb + [_SEM] * (2 * nl) + [_ANY], out_specs=[_HBM] * nb,
        out_shape=[pltpu.HBM(b.shape, b.dtype) for b in both],
        input_output_aliases={i: i for i in range(nb)},
        compiler_params=pltpu.CompilerParams(has_side_effects=_EFFECT),
    )(*both, *send, *recv, after)
    return list(outs[:ns]), list(outs[ns:])


def _gather_plan(split):
    def plan(srcs, lands):
        x, y, c, chips = _place()
        out = []
        for ld, sp in zip(lands, split):
            blk = _gather_block(ld, ld.shape[2], sp, 2 * x + y, c)
            out.append([(blk, blk, (qx, qy, c)) for qx, qy in chips])
        return out
    return plan


def _gather_whole(split):
    def whole(lands):
        _, _, c, _ = _place()
        return [ld.at[pl.ds(0, 3), :, _half(ld.shape[2], c), :] if sp else ld.at[pl.ds(0, 3)]
                for ld, sp in zip(lands, split)]
    return whole


def _relay_plan(srcs, lands):
    x, y, c, _ = _place()
    out = []
    for ld in lands:
        blocks = [ld.at[j, :, _half(ld.shape[2], c), :] for j in range(N_CHIPS)]
        out.append([(blk, blk, (x, y, 1 - c)) for blk in blocks])
    return out


def _relay_whole(lands):
    _, _, c, _ = _place()
    return [ld.at[:, :, _half(ld.shape[2], c), :] for ld in lands]


def _fill_own(name, shard, dtype, place, layer=None):
    nl, r, c = shard.shape
    first = 0
    if layer is not None:
        nl, first = 1, layer
    tr = _tile(r, 512)

    def body(p_ref, s_ref, o_ref):
        o_ref[...] = s_ref[...].astype(o_ref.dtype)

    return pl.pallas_call(
        body, name=name,
        grid_spec=pltpu.PrefetchScalarGridSpec(
            num_scalar_prefetch=1, grid=(nl, r // tr),
            in_specs=[pl.BlockSpec((None, tr, c), lambda l, i, p: (first + l, i, 0))],
            out_specs=pl.BlockSpec((None, None, tr, c), lambda l, i, p: (p[0], l, i, 0))),
        out_shape=jax.ShapeDtypeStruct((N_CHIPS, nl, r, c), dtype),
        compiler_params=_params("parallel", "parallel"),
    )(place, shard)


def _gather_finish(name, lands):
    n = len(lands)

    def body(*refs):
        outs = refs[n:2 * n]
        fsend, frecv = refs[2 * n:]
        x, y, c, chips = _place()
        sib = (x, y, 1 - c)

        def relay(a, qi, cc):
            qx, qy = chips[qi]
            blk = _gather_block(outs[a], outs[a].shape[2], True, 2 * qx + qy, cc)
            return _remote(blk, blk, fsend.at[a, qi], frecv.at[a, qi], sib)

        relays = [relay(a, qi, c) for a in range(n) for qi in range(3)]
        for cp in relays:
            cp.start()
        for a in range(n):
            for qi in range(3):
                relay(a, qi, 1 - c).wait_recv()
        for cp in relays:
            cp.wait_send()

    outs = pl.pallas_call(
        body, name=name, in_specs=[_ANY] * n, out_specs=[_ANY] * n,
        out_shape=[jax.ShapeDtypeStruct(ld.shape, ld.dtype) for ld in lands],
        input_output_aliases={i: i for i in range(n)},
        scratch_shapes=[pltpu.SemaphoreType.DMA((n, 3))] * 2,
    )(*lands)
    return list(outs)


def _pair_plan(srcs, lands):
    x, y, c, _ = _place()
    return [[(s.at[:, :, _half(s.shape[2], 1 - c), :], ld, (x, y, 1 - c))] for s, ld in zip(srcs, lands)]


def _pair_whole(lands):
    return list(lands)


def _scatter_plan(srcs, lands):
    x, y, c, chips = _place()
    return [[(s.at[2 * qx + qy], ld.at[2 * x + y, :, _half(ld.shape[2], c), :], (qx, qy, c)) for qx, qy in chips]
            for s, ld in zip(srcs, lands)]


def _scatter_whole(lands):
    _, _, c, _ = _place()
    return [ld.at[pl.ds(0, 3), :, _half(ld.shape[2], c), :] for ld in lands]


def _pair_sum(name, grad, other, place):
    nj, nl, r, c = grad.shape
    h = r // 2
    tr = _tile(h, 256)
    nb = h // tr

    def body(p_ref, g_ref, o_ref, q_ref, d_ref):
        q = (g_ref[...].astype(F32) + o_ref[...].astype(F32)).astype(BF16)
        q_ref[...] = q

        @pl.when(pl.program_id(2) == p_ref[0])
        def _():
            d_ref[...] = q

    blk = (None, None, tr, c)
    return pl.pallas_call(
        body, name=name,
        grid_spec=pltpu.PrefetchScalarGridSpec(
            num_scalar_prefetch=1, grid=(nl, nb, nj),
            in_specs=[pl.BlockSpec(blk, lambda l, i, j, p: (j, l, p[1] * nb + i, 0)),
                      pl.BlockSpec(blk, lambda l, i, j, p: (j, l, i, 0))],
            out_specs=[pl.BlockSpec(blk, lambda l, i, j, p: (j, l, i, 0)),
                       pl.BlockSpec(blk, lambda l, i, j, p: (p[0], l, p[1] * nb + i, 0))]),
        out_shape=[jax.ShapeDtypeStruct((nj, nl, h, c), BF16), jax.ShapeDtypeStruct((nj, nl, r, c), BF16)],
        compiler_params=_params("parallel", "parallel", "arbitrary"),
    )(place, grad, other)


def _allreduce_small(name, packed):
    rows, lanes = packed.shape

    def body(x_ref, o_ref, all_ref, send, recv, lsem):
        x, y, c, chips = _place()
        me, sib = (x, y, c), (x, y, 1 - c)

        def slot(px, py, pc):
            return all_ref.at[pl.ds((4 * px + 2 * py + pc) * rows, rows), :]

        def copy(k, blk, to, src=None):
            return _remote(slot(*blk) if src is None else src, slot(*blk), send.at[k], recv.at[k], to)

        mine = pltpu.make_async_copy(x_ref, slot(*me), lsem)
        mine.start()
        first = [copy(0, me, sib, src=x_ref)] + [copy(1 + j, me, (*chip, c), src=x_ref) for j, chip in enumerate(chips)]
        for cp in first:
            cp.start()
        passed = [copy(4 + j, (*chip, c), sib) for j, chip in enumerate(chips)]
        for j, chip in enumerate(chips):
            copy(1 + j, (*chip, c), me).wait_recv()
            passed[j].start()
        copy(0, sib, me).wait_recv()
        for j, chip in enumerate(chips):
            copy(4 + j, (*chip, 1 - c), me).wait_recv()
        for cp in first + passed:
            cp.wait_send()
        mine.wait()
        total = all_ref[pl.ds(0, rows), :]
        for d in range(1, N_DEV):
            total = total + all_ref[pl.ds(d * rows, rows), :]
        o_ref[...] = total

    vmem = pl.BlockSpec(memory_space=pltpu.VMEM)
    return pl.pallas_call(
        body, name=name, in_specs=[vmem], out_specs=vmem, out_shape=jax.ShapeDtypeStruct(packed.shape, F32),
        scratch_shapes=[pltpu.VMEM((N_DEV * rows, lanes), F32), pltpu.SemaphoreType.DMA((7,)),
                        pltpu.SemaphoreType.DMA((7,)), pltpu.SemaphoreType.DMA],
        compiler_params=pltpu.CompilerParams(vmem_limit_bytes=VMEM_LIMIT),
    )(packed)


def _pack(parts):
    rows = []
    for p in parts:
        flat = p.reshape(-1)
        pad = (-flat.shape[0]) % PACK_ELEMS
        rows.append(jnp.pad(flat, (0, pad)).reshape(-1, LANES))
    return jnp.concatenate(rows, axis=0)


def _unpack(packed, shapes):
    out, row = [], 0
    for sh in shapes:
        size = math.prod(sh)
        nrows = -(-size // PACK_ELEMS) * (PACK_ELEMS // LANES)
        out.append(packed[row:row + nrows].reshape(-1)[:size].reshape(sh))
        row += nrows
    return out


def kernel(x, a_w_in, a_ln_g, a_ln_b, a_w_s, a_b_s, a_w_out, b_w_in, b_w_grp, b_scale, b_w_out, norm_mix, norm_mlp, mlp_w1, mlp_w2, final_norm, loss_target, m_a_w_in, m_a_ln_g, m_a_ln_b, m_a_w_s, m_a_b_s, m_a_w_out, m_b_w_in, m_b_w_grp, m_b_scale, m_b_w_out, m_norm_mix, m_norm_mlp, m_mlp_w1, m_mlp_w2, m_final_norm, v_a_w_in, v_a_ln_g, v_a_ln_b, v_a_w_s, v_a_b_s, v_a_w_out, v_b_w_in, v_b_w_grp, v_b_scale, v_b_w_out, v_norm_mix, v_norm_mlp, v_mlp_w1, v_mlp_w2, v_final_norm):
    xi, yi, ci = lax.axis_index("x"), lax.axis_index("y"), lax.axis_index("c")
    chip = 2 * xi + yi
    place = jnp.stack([chip, ci]).astype(jnp.int32)
    x2, tgt = x[0], loss_target[0]
    bw = b_scale.shape[1] * N_CHIPS

    units = dict(a_w_in=(a_w_in, None), a_w_out=(a_w_out, None), w1_0=(mlp_w1, 0), w2_0=(mlp_w2, 0),
                 b_scale=(b_scale.reshape(1, 1, -1), None), b_w_in=(b_w_in, None), b_w_grp=(b_w_grp[0], None),
                 b_w_out=(b_w_out, None), w1_1=(mlp_w1, 1), w2_1=(mlp_w2, 1))
    col_sharded = dict(a_w_in=True, a_w_out=False, b_w_in=False, b_w_grp=False, b_w_out=False,
                       w1_0=True, w2_0=False, w1_1=True, w2_1=False)
    in_flight, W = {}, {}

    def launch(tag, keys, deps):
        sp = [k != "b_scale" for k in keys]
        zones = [_fill_own(f"gather_own_{k}", units[k][0], BF16 if s else F32, place, layer=units[k][1])
                 for k, s in zip(keys, sp)]
        send, recv, _, zones, tok = _split_start(f"gather_start_{tag}", [], zones, _gather_plan(sp), deps)
        in_flight.update({k: state for k, *state in zip(keys, send, recv, zones, sp)})
        return tok

    def arrive(keys, after):
        send, recv, zones, sp = zip(*[in_flight[k] for k in keys])
        _, zones = _split_wait(f"gather_wait_{keys[0]}", [], zones, send, recv, after, _gather_whole(sp))
        relayed = iter(_gather_finish(f"gather_finish_{keys[0]}", [z for z, s in zip(zones, sp) if s]))
        for k, z, s in zip(keys, zones, sp):
            full = next(relayed) if s else z
            W[k] = _W4(full, col_sharded[k]) if k in col_sharded else full

    token = launch("first", ["a_w_in", "a_w_out"], ())
    token = launch("rest", ["w1_0", "w2_0", "b_scale", "b_w_in", "b_w_grp", "b_w_out", "w1_1", "w2_1"], (token,))

    b_col = a_b_s[0].T

    def residual(acc, res):
        return (res + acc,)

    def sq_relu(acc):
        act = jnp.maximum(acc, 0.0)
        return act, act * act

    def mlp_fwd(tag, h, layer):
        hn = _rms_fwd(f"mlp{tag}_norm", h, norm_mlp[layer:layer + 1])
        arrive([f"w1_{layer}"], hn)
        act, act_sq = _mm_aw(f"mlp{tag}_up", hn, W[f"w1_{layer}"], out_dtypes=(BF16, BF16), epilogue=sq_relu)
        arrive([f"w2_{layer}"], act_sq)
        out = _mm_aw(f"mlp{tag}_down", act_sq, W[f"w2_{layer}"], extras=(h,), epilogue=residual)
        return out, (h, hn, act, act_sq)

    hn0 = _rms_fwd("mix_a_norm", x2, norm_mix[0:1])
    arrive(["a_w_in"], token)
    zpre = _mm_aw("mix_a_in", hn0, W["a_w_in"])
    gated = _gate_fwd("mix_a_gate", zpre, a_ln_g, a_ln_b, a_w_s[0], b_col)
    arrive(["a_w_out"], gated)
    h1 = _mm_aw("mix_a_out", gated, W["a_w_out"], extras=(x2,), epilogue=residual)
    h2, mlp0 = mlp_fwd("0", h1, 0)
    hn2 = _rms_fwd("mix_b_norm", h2, norm_mix[1:2])
    arrive(["b_scale", "b_w_in"], hn2)
    scale_full = W["b_scale"].reshape(1, bw)
    vb = _mm_aw("mix_b_in", hn2, W["b_w_in"])
    pooled = _pool_fwd("mix_b_pool", vb)
    arrive(["b_w_grp", "b_w_out"], pooled)
    mixed, ms = _mm_aw("mix_b_grp", pooled, W["b_w_grp"], groups=len(B_WINDOWS), extras=(scale_full,),
                       out_dtypes=(F32, BF16), epilogue=lambda acc, sc: (acc, acc * sc))
    h3 = _mm_aw("mix_b_out", ms, W["b_w_out"], extras=(h2,), epilogue=residual)
    h4, mlp1 = mlp_fwd("1", h3, 1)
    dh4, dh4_b, d_final, loss_part = _final("loss_head", h4, final_norm.reshape(1, -1), tgt)
    g1_like = _W4(None, True, shape=(N_CHIPS, 1, *W["w1_0"].arr.shape[2:]))
    g2_like = _W4(None, False, shape=(N_CHIPS, 1, *W["w2_0"].arr.shape[2:]))

    def exchange(tag, gs):
        zones = [lax.empty((g.shape[0], g.shape[1], g.shape[2] // 2, g.shape[3]), g.dtype) for g in gs]
        send, recv, srcs, zones, tok = _split_start(f"pair_start_{tag}", gs, zones, _pair_plan)
        return (tag, send, recv, srcs, zones), tok

    def reduce(state, after):
        tag, send, recv, srcs, zones = state
        srcs, zones = _split_wait(f"pair_wait_{tag}", srcs, zones, send, recv, after, _pair_whole)
        both = [_pair_sum(f"pair_sum_{tag}_{i}", g, o, place) for i, (g, o) in enumerate(zip(srcs, zones))]
        send, recv, sums, dests, tok = _split_start(f"scatter_start_{tag}", [b[0] for b in both],
                                                    [b[1] for b in both], _scatter_plan)
        return (tag, send, recv, sums, dests), tok

    def relay(state, after):
        tag, send, recv, sums, dests = state
        _, dests = _split_wait(f"scatter_wait_{tag}", sums, dests, send, recv, after, _scatter_whole)
        send, recv, _, dests, tok = _split_start(f"relay_start_{tag}", [], dests, _relay_plan)
        return (tag, send, recv, dests), tok

    def land(state, after):
        tag, send, recv, dests = state
        return _split_wait(f"relay_wait_{tag}", [], dests, send, recv, after, _relay_whole)[1]

    def mlp_bwd(tag, dh, dh_b, saved, layer, deps, pending=None):
        h, hn, act, act_sq = saved
        dpre = _mm_aw(f"mlp{tag}_down_dx", dh_b, W[f"w2_{layer}"], transpose_w=True, extras=(act,),
                      out_dtypes=(BF16,), epilogue=lambda acc, a: (acc * (2.0 * a),), deps=deps)
        scattering, dw_deps = None, ()
        if pending is not None:
            scattering, tok = reduce(pending, dpre)
            dw_deps = (tok,)
        g_w2 = _mm_dw(f"mlp{tag}_down_dw", act_sq, dh_b, g2_like, deps=dw_deps)
        pair_w2, tok = exchange(f"w2_{layer}", [g_w2])
        dhn = _mm_aw(f"mlp{tag}_up_dx", dpre, W[f"w1_{layer}"], transpose_w=True, deps=(tok,))
        g_w1 = _mm_dw(f"mlp{tag}_up_dw", hn, dpre, g1_like)
        pair_w1, tok1 = exchange(f"w1_{layer}", [g_w1])
        scat_w2, tok2 = reduce(pair_w2, dhn)
        dh_in, dh_in_b, d_norm = _rms_bwd(f"mlp{tag}_norm_bwd", h, norm_mlp[layer:layer + 1], dhn, dh)
        return dh_in, dh_in_b, d_norm, pair_w1, scat_w2, (tok1, tok2), scattering

    dh3, dh3_b, d_norm_mlp1, pair_w1_1, scat_w2_1, toks, _ = mlp_bwd("1", dh4, dh4_b, mlp1, 1, ())
    dms = _mm_aw("mix_b_out_dx", dh3_b, W["b_w_out"], transpose_w=True, deps=toks)
    g_b_out = _mm_dw("mix_b_out_dw", ms, dh3_b, W["b_w_out"])
    scat_w1_1, tok = reduce(pair_w1_1, dms)
    dmixed, d_scale = _scale_bwd("mix_b_scale_bwd", dms, mixed, scale_full)
    dpooled = _mm_aw("mix_b_grp_dx", dmixed, W["b_w_grp"], groups=len(B_WINDOWS), transpose_w=True, deps=(tok,))
    g_b_grp = _mm_dw("mix_b_grp_dw", pooled, dmixed, W["b_w_grp"], groups=len(B_WINDOWS))
    dvb = _pool_bwd("mix_b_pool_bwd", dpooled)
    dhn2 = _mm_aw("mix_b_in_dx", dvb, W["b_w_in"], transpose_w=True)
    g_b_in = _mm_dw("mix_b_in_dw", hn2, dvb, W["b_w_in"])
    pair_b, tok = exchange("b", [g_b_out, g_b_grp, g_b_in])
    dh2, dh2_b, d_norm_mix1 = _rms_bwd("mix_b_norm_bwd", h2, norm_mix[1:2], dhn2, dh3)
    dh1, dh1_b, d_norm_mlp0, pair_w1_0, scat_w2_0, toks, scat_b = mlp_bwd("0", dh2, dh2_b, mlp0, 0, (tok,),
                                                                          pending=pair_b)
    dgated = _mm_aw("mix_a_out_dx", dh1_b, W["a_w_out"], transpose_w=True, deps=toks)
    g_a_out = _mm_dw("mix_a_out_dw", gated, dh1_b, W["a_w_out"])
    pair_a_out, tok_a = exchange("a_out", [g_a_out])
    scat_w1_0, tok = reduce(pair_w1_0, dgated)
    dzpre, d_w_s, d_b_col, d_ln_g, d_ln_b = _gate_bwd("mix_a_gate_bwd", zpre, dgated, a_ln_g, a_ln_b, a_w_s[0], b_col)
    dhn0 = _mm_aw("mix_a_in_dx", dzpre, W["a_w_in"], transpose_w=True, deps=(tok, tok_a))
    scat_a_out, tok = reduce(pair_a_out, dhn0)
    early = [relay(state, dhn0) for state in (scat_w2_1, scat_w1_1, scat_b)]
    g_a_in = _mm_dw("mix_a_in_dw", hn0, dzpre, W["a_w_in"], deps=(tok, *[t for _, t in early]))
    pair_a_in, tok = exchange("a_in", [g_a_in])
    dx, _, d_norm_mix0 = _rms_bwd("mix_a_norm_bwd", x2, norm_mix[0:1], dhn0, dh1)
    scat_a_in, _ = reduce(pair_a_in, dx)

    moments = dict(a_w_in=(m_a_w_in, v_a_w_in), a_w_out=(m_a_w_out, v_a_w_out), b_w_in=(m_b_w_in, v_b_w_in),
                   b_w_grp=(m_b_w_grp, v_b_w_grp), b_w_out=(m_b_w_out, v_b_w_out),
                   mlp_w1=(m_mlp_w1, v_mlp_w1), mlp_w2=(m_mlp_w2, v_mlp_w2))
    weights = dict(a_w_in=a_w_in, a_w_out=a_w_out, b_w_in=b_w_in, b_w_grp=b_w_grp, b_w_out=b_w_out,
                   mlp_w1=mlp_w1, mlp_w2=mlp_w2)
    landing = [(scat_w2_1, [("mlp_w2", 1)]), (scat_w1_1, [("mlp_w1", 1)]),
               (scat_b, [("b_w_out", 0), ("b_w_grp", 0), ("b_w_in", 0)]),
               (scat_w2_0, [("mlp_w2", 0)]), (scat_w1_0, [("mlp_w1", 0)]),
               (scat_a_out, [("a_w_out", 0)]), (scat_a_in, [("a_w_in", 0)])]
    results, after = {}, dx
    relays = list(early)
    for i, (_, members) in enumerate(landing):
        deps = ()
        if len(relays) == i + 1 < len(landing):
            relays.append(relay(landing[i + 1][0], after))
            deps = (relays[-1][1],)
        for (k, layer), parts in zip(members, land(relays[i][0], relays[-1][1] if deps else after)):
            shard_shape = (-1, *parts.shape[2:])
            results[k] = _adam_shard(f"adam_{k}_{layer}", weights[k].reshape(shard_shape),
                                     moments[k][0].reshape(shard_shape), moments[k][1].reshape(shard_shape),
                                     parts, layer=layer, prev=results.get(k), deps=deps)
            after = results[k][1]
    grad_out, delta_out, m_out, v_out = {}, {}, {}, {}
    for k, res in results.items():
        grad_out[k], delta_out[k], m_out[k], v_out[k] = [r.reshape(weights[k].shape) for r in res]

    small = dict(a_ln_g=(a_ln_g, m_a_ln_g, v_a_ln_g), a_ln_b=(a_ln_b, m_a_ln_b, v_a_ln_b),
                 a_w_s=(a_w_s, m_a_w_s, v_a_w_s), a_b_s=(a_b_s, m_a_b_s, v_a_b_s),
                 b_scale=(b_scale, m_b_scale, v_b_scale), norm_mix=(norm_mix, m_norm_mix, v_norm_mix),
                 norm_mlp=(norm_mlp, m_norm_mlp, v_norm_mlp), final_norm=(final_norm, m_final_norm, v_final_norm))
    small_names = list(small)
    local = dict(a_ln_g=d_ln_g, a_ln_b=d_ln_b, a_w_s=d_w_s[None], a_b_s=d_b_col.T[None], b_scale=d_scale,
                 norm_mix=jnp.concatenate([d_norm_mix0, d_norm_mix1], axis=0),
                 norm_mlp=jnp.concatenate([d_norm_mlp0, d_norm_mlp1], axis=0), final_norm=d_final.reshape(-1))
    reduced = _allreduce_small("small_grad_allreduce", _pack([local[k] for k in small_names]))
    small_grads = dict(zip(small_names, _unpack(reduced, [local[k].shape for k in small_names])))
    shard_w = b_scale.shape[1]
    small_grads["b_scale"] = lax.dynamic_slice_in_dim(small_grads["b_scale"], chip * shard_w, shard_w, axis=1)
    small_grads = {k: small_grads[k].reshape(small[k][0].shape) for k in small_names}
    packed = [_pack([small[k][i] for k in small_names]) for i in range(3)]
    res = _adam_packed("adam_small", packed[0], _pack([small_grads[k] for k in small_names]), packed[1], packed[2])
    shapes = [small[k][0].shape for k in small_names]
    for k, d, nm, nv in zip(small_names, *[_unpack(r, shapes) for r in res]):
        grad_out[k], delta_out[k], m_out[k], v_out[k] = small_grads[k], d, nm, nv

    loss = lax.psum(loss_part[0, 0], ("x", "y", "c"))
    order = ["a_w_in", "a_ln_g", "a_ln_b", "a_w_s", "a_b_s", "a_w_out", "b_w_in", "b_w_grp", "b_scale", "b_w_out",
             "norm_mix", "norm_mlp", "mlp_w1", "mlp_w2", "final_norm"]
    return (loss, dx[None], *[grad_out[k] for k in order], *[delta_out[k] for k in order],
            *[m_out[k] for k in order], *[v_out[k] for k in order])
```

```python
import math

import jax
import jax.numpy as jnp
from jax import lax
from jax.experimental import pallas as pl
from jax.experimental.pallas import tpu as pltpu

F32 = jnp.float32
BF16 = jnp.bfloat16
MESH = pl.DeviceIdType.MESH

EPS = 1e-6
B_WINDOWS = (2, 4, 8, 16)
ADAM_LR = 0.001
ADAM_B1 = 0.9
ADAM_B2 = 0.999
ADAM_EPS = 1e-08
ADAM_WD = 0.01
ADAM_STEP = 10

N_CHIPS = 4
N_DEV = 8
LANES = 128
PACK_ELEMS = 8 * LANES
VMEM_LIMIT = 56 * 1024 * 1024
ROW_TILE = 512
MM_TM, MM_TN, MM_TK = 1024, 1024, 2048


_ANY = pl.BlockSpec(memory_space=pl.ANY)


def _tile(dim, pref):
    t = min(dim, pref)
    while dim % t:
        t //= 2
    return t


def _params(*sem):
    return pltpu.CompilerParams(dimension_semantics=sem, vmem_limit_bytes=VMEM_LIMIT)


class _W4:
    def __init__(self, arr, col_sharded, shape=None):
        self.arr = arr
        self.nj, self.nl, self.r, self.c = arr.shape if shape is None else shape
        self.col = col_sharded
        self.rows = self.r if col_sharded else self.nj * self.r
        self.cols = self.nj * self.c if col_sharded else self.c

    def tile_rows(self, pref):
        return _tile(self.r, pref)

    def tile_cols(self, pref):
        return _tile(self.c, pref)

    def index(self, layer, rb, cb, tr, tc):
        if self.col:
            n = self.c // tc
            return (cb // n, layer, rb, cb % n)
        n = self.r // tr
        return (rb // n, layer, rb % n, cb)


def _mm_aw(name, a, w, *, layer=0, groups=1, transpose_w=False, extras=(), out_dtypes=(F32,), epilogue=None,
           deps=()):
    s, ka_total = a.shape
    kdim, ndim = (w.cols, w.rows) if transpose_w else (w.rows, w.cols)
    assert ka_total == groups * kdim, (name, a.shape, kdim, groups)
    span = groups == 1 and not transpose_w and not w.col and kdim <= MM_TK
    if groups > 1:
        tm = _tile(s, MM_TM)
        tn, tk = (w.tile_rows(MM_TN), w.tile_cols(512)) if transpose_w else (w.tile_cols(MM_TN), w.tile_rows(512))
    else:
        tk = kdim if span else (w.tile_cols(MM_TK) if transpose_w else w.tile_rows(MM_TK))
        tm, tn_pref = (_tile(s, 2048), 512) if tk == kdim else (_tile(s, MM_TM), MM_TN)
        tn = w.tile_rows(tn_pref) if transpose_w else w.tile_cols(tn_pref)
    nk, nn = kdim // tk, ndim // tn

    def lay(g):
        return g if groups > 1 else layer

    a_spec = pl.BlockSpec((tm, tk), lambda g, i, n, k: (i, g * nk + k))
    if span:
        w_spec = pl.BlockSpec((w.nj, None, w.r, tn), lambda g, i, n, k: (0, layer, 0, n))
    elif transpose_w:
        w_spec = pl.BlockSpec((None, None, tn, tk), lambda g, i, n, k: w.index(lay(g), n, k, tn, tk))
    else:
        w_spec = pl.BlockSpec((None, None, tk, tn), lambda g, i, n, k: w.index(lay(g), k, n, tk, tn))
    ex_specs = []
    for e in extras:
        assert e.shape[1] == groups * ndim and e.shape[0] in (1, s), (name, e.shape)
        if e.shape[0] == 1:
            ex_specs.append(pl.BlockSpec((1, tn), lambda g, i, n, k: (0, g * nn + n)))
        else:
            ex_specs.append(pl.BlockSpec((tm, tn), lambda g, i, n, k: (i, g * nn + n)))
    out_spec = pl.BlockSpec((tm, tn), lambda g, i, n, k: (i, g * nn + n))
    n_ex, n_out, n_dep = len(extras), len(out_dtypes), len(deps)

    def body(a_ref, w_ref, *rest):
        ex, outs = rest[:n_ex], rest[n_ex + n_dep:n_ex + n_dep + n_out]
        av = a_ref[...]
        if av.dtype != BF16:
            av = av.astype(BF16)
        wv = w_ref[...].reshape(tk, tn) if span else w_ref[...]
        if transpose_w:
            prod = lax.dot_general(av, wv, (((1,), (1,)), ((), ())), preferred_element_type=F32)
        else:
            prod = jnp.dot(av, wv, preferred_element_type=F32)

        def finish(total):
            vals = (total,) if epilogue is None else epilogue(total, *[e[...] for e in ex])
            for o, v in zip(outs, vals):
                o[...] = v.astype(o.dtype)

        if nk == 1:
            finish(prod)
            return
        acc, k = rest[-1], pl.program_id(3)

        @pl.when(k == 0)
        def _():
            acc[...] = prod

        @pl.when(k > 0)
        def _():
            acc[...] += prod

        @pl.when(k == nk - 1)
        def _():
            finish(acc[...])

    outs = pl.pallas_call(
        body, name=name, grid=(groups, s // tm, nn, nk),
        in_specs=[a_spec, w_spec, *ex_specs] + [_ANY] * n_dep, out_specs=[out_spec] * n_out,
        out_shape=[jax.ShapeDtypeStruct((s, groups * ndim), dt) for dt in out_dtypes],
        scratch_shapes=[pltpu.VMEM((tm, tn), F32)] if nk > 1 else [],
        compiler_params=_params("parallel", "parallel", "parallel", "arbitrary"),
    )(a, w.arr, *extras, *deps)
    return outs[0] if n_out == 1 else outs


def _mm_dw(name, a, b, like, *, layer=0, groups=1, deps=()):
    s, ka_total = a.shape
    rows, cols = ka_total // groups, b.shape[1] // groups
    assert (rows, cols) == (like.rows, like.cols) and b.shape[0] == s, (name, a.shape, b.shape)
    tm, tn, tk = like.tile_rows(MM_TM), like.tile_cols(MM_TN), _tile(s, MM_TK)
    nr, nc, nk = rows // tm, cols // tn, s // tk

    def lay(g):
        return g if groups > 1 else layer

    in_specs = [pl.BlockSpec((tk, tm), lambda g, n, i, k: (k, g * nr + i)),
                pl.BlockSpec((tk, tn), lambda g, n, i, k: (k, g * nc + n))]
    in_specs += [_ANY] * len(deps)

    def body(a_ref, b_ref, *rest):
        av, bv = a_ref[...], b_ref[...]
        if av.dtype != BF16:
            av = av.astype(BF16)
        if bv.dtype != BF16:
            bv = bv.astype(BF16)
        prod = lax.dot_general(av, bv, (((0,), (0,)), ((), ())), preferred_element_type=F32)
        if nk == 1:
            rest[-1][...] = prod.astype(BF16)
            return
        o_ref, acc, k = rest[-2], rest[-1], pl.program_id(3)

        @pl.when(k == 0)
        def _():
            acc[...] = prod

        @pl.when(k > 0)
        def _():
            acc[...] += prod

        @pl.when(k == nk - 1)
        def _():
            o_ref[...] = acc[...].astype(BF16)

    return pl.pallas_call(
        body, name=name, grid=(groups, nc, nr, nk), in_specs=in_specs,
        out_specs=pl.BlockSpec((None, None, tm, tn), lambda g, n, i, k: like.index(lay(g), i, n, tm, tn)),
        out_shape=jax.ShapeDtypeStruct((like.nj, like.nl, like.r, like.c), BF16),
        scratch_shapes=[pltpu.VMEM((tm, tn), F32)] if nk > 1 else [],
        compiler_params=_params("parallel", "parallel", "parallel", "arbitrary"),
    )(a, b, *deps)


def _row_spec(tr, d):
    return pl.BlockSpec((tr, d), lambda i: (i, 0))


def _vec_spec(d):
    return pl.BlockSpec((1, d), lambda i: (0, 0))


def _rms_fwd(name, x, g):
    s, d = x.shape
    tr = _tile(s, ROW_TILE)

    def body(x_ref, g_ref, o_ref):
        xv = x_ref[...]
        r = lax.rsqrt(jnp.mean(xv * xv, axis=-1, keepdims=True) + EPS)
        o_ref[...] = (xv * r * g_ref[...]).astype(BF16)

    return pl.pallas_call(
        body, name=name, grid=(s // tr,), in_specs=[_row_spec(tr, d), _vec_spec(d)],
        out_specs=_row_spec(tr, d), out_shape=jax.ShapeDtypeStruct((s, d), BF16),
        compiler_params=_params("parallel"),
    )(x, g)


def _rms_bwd(name, x, g, dhn, dres):
    s, d = x.shape
    tr = _tile(s, ROW_TILE)

    def body(x_ref, g_ref, dhn_ref, dres_ref, dx_ref, dxb_ref, dg_ref):
        @pl.when(pl.program_id(0) == 0)
        def _():
            dg_ref[...] = jnp.zeros_like(dg_ref)

        xv = x_ref[...]
        r = lax.rsqrt(jnp.mean(xv * xv, axis=-1, keepdims=True) + EPS)
        xh = xv * r
        dy = dhn_ref[...]
        dg_ref[...] += jnp.sum(dy * xh, axis=0, keepdims=True)
        dxh = dy * g_ref[...]
        dx = dres_ref[...] + r * (dxh - xh * jnp.mean(dxh * xh, axis=-1, keepdims=True))
        dx_ref[...] = dx
        dxb_ref[...] = dx.astype(BF16)

    return pl.pallas_call(
        body, name=name, grid=(s // tr,),
        in_specs=[_row_spec(tr, d), _vec_spec(d), _row_spec(tr, d), _row_spec(tr, d)],
        out_specs=[_row_spec(tr, d), _row_spec(tr, d), _vec_spec(d)],
        out_shape=[jax.ShapeDtypeStruct((s, d), F32), jax.ShapeDtypeStruct((s, d), BF16),
                   jax.ShapeDtypeStruct((1, d), F32)],
        compiler_params=_params("arbitrary"),
    )(x, g, dhn, dres)


def _final(name, h, g, tgt):
    s, d = h.shape
    tr = _tile(s, ROW_TILE)

    def body(h_ref, g_ref, t_ref, dh_ref, dhb_ref, dg_ref, loss_ref):
        @pl.when(pl.program_id(0) == 0)
        def _():
            dg_ref[...] = jnp.zeros_like(dg_ref)
            loss_ref[...] = jnp.zeros_like(loss_ref)

        hv = h_ref[...]
        r = lax.rsqrt(jnp.mean(hv * hv, axis=-1, keepdims=True) + EPS)
        xh = hv * r
        gv = g_ref[...]
        err = xh * gv - t_ref[...]
        part = 0.5 * jnp.sum(jnp.mean(err * err, axis=-1, keepdims=True), axis=0, keepdims=True)
        loss_ref[...] += jnp.broadcast_to(part, loss_ref.shape)
        dy = err * (1.0 / d)
        dg_ref[...] += jnp.sum(dy * xh, axis=0, keepdims=True)
        dxh = dy * gv
        dh = r * (dxh - xh * jnp.mean(dxh * xh, axis=-1, keepdims=True))
        dh_ref[...] = dh
        dhb_ref[...] = dh.astype(BF16)

    return pl.pallas_call(
        body, name=name, grid=(s // tr,),
        in_specs=[_row_spec(tr, d), _vec_spec(d), _row_spec(tr, d)],
        out_specs=[_row_spec(tr, d), _row_spec(tr, d), _vec_spec(d), _vec_spec(LANES)],
        out_shape=[jax.ShapeDtypeStruct((s, d), F32), jax.ShapeDtypeStruct((s, d), BF16),
                   jax.ShapeDtypeStruct((1, d), F32), jax.ShapeDtypeStruct((1, LANES), F32)],
        compiler_params=_params("arbitrary"),
    )(h, g, tgt)


_SQRT_HALF = 1.0 / math.sqrt(2.0)
_INV_SQRT_2PI = 1.0 / math.sqrt(2.0 * math.pi)


def _gelu(x):
    return x * (lax.erf(x * _SQRT_HALF) + 1.0) * 0.5


def _gelu_grad(x):
    return 0.5 * (lax.erf(x * _SQRT_HALF) + 1.0) + x * jnp.exp(-0.5 * x * x) * _INV_SQRT_2PI


def _causal(chunk):
    t = lax.broadcasted_iota(jnp.int32, (chunk, chunk), 0)
    sidx = lax.broadcasted_iota(jnp.int32, (chunk, chunk), 1)
    return sidx <= t


def _layernorm_parts(v, g, b):
    mu = jnp.mean(v, axis=-1, keepdims=True)
    vc = v - mu
    rs = lax.rsqrt(jnp.mean(vc * vc, axis=-1, keepdims=True) + EPS)
    vhat = vc * rs
    return vhat, rs, vhat * g + b


def _gate_fwd(name, zpre, ln_g, ln_b, w_s, b_col):
    s, aw2 = zpre.shape
    aw = aw2 // 2
    ng, chunk, _ = w_s.shape
    dh = aw // ng

    def body(z_ref, g_ref, b_ref, ws_ref, bc_ref, o_ref):
        u = _gelu(z_ref[:, :aw])
        v = _gelu(z_ref[:, aw:])
        _, _, vln = _layernorm_parts(v, g_ref[...], b_ref[...])
        mask = _causal(chunk)
        for gi in range(ng):
            sl = slice(gi * dh, (gi + 1) * dh)
            wm = jnp.where(mask, ws_ref[gi], 0.0).astype(BF16)
            sg = jnp.dot(wm, vln[:, sl].astype(BF16), preferred_element_type=F32) + bc_ref[:, gi:gi + 1]
            o_ref[:, sl] = (u[:, sl] * sg).astype(BF16)

    return pl.pallas_call(
        body, name=name, grid=(s // chunk,),
        in_specs=[_row_spec(chunk, aw2), _vec_spec(aw), _vec_spec(aw),
                  pl.BlockSpec((ng, chunk, chunk), lambda i: (0, 0, 0)),
                  pl.BlockSpec((chunk, ng), lambda i: (0, 0))],
        out_specs=_row_spec(chunk, aw), out_shape=jax.ShapeDtypeStruct((s, aw), BF16),
        compiler_params=_params("parallel"),
    )(zpre, ln_g, ln_b, w_s, b_col)


def _gate_bwd(name, zpre, dgated, ln_g, ln_b, w_s, b_col):
    s, aw2 = zpre.shape
    aw = aw2 // 2
    ng, chunk, _ = w_s.shape
    dh = aw // ng

    def body(z_ref, dgt_ref, g_ref, b_ref, ws_ref, bc_ref, dz_ref, dws_ref, dbc_ref, dlg_ref, dlb_ref,
             du_scr, dvln_scr):
        @pl.when(pl.program_id(0) == 0)
        def _():
            dws_ref[...] = jnp.zeros_like(dws_ref)
            dbc_ref[...] = jnp.zeros_like(dbc_ref)
            dlg_ref[...] = jnp.zeros_like(dlg_ref)
            dlb_ref[...] = jnp.zeros_like(dlb_ref)

        zu = z_ref[:, :aw]
        zv = z_ref[:, aw:]
        u = _gelu(zu)
        lg = g_ref[...]
        vhat, rs, vln = _layernorm_parts(_gelu(zv), lg, b_ref[...])
        mask = _causal(chunk)
        for gi in range(ng):
            sl = slice(gi * dh, (gi + 1) * dh)
            wm = jnp.where(mask, ws_ref[gi], 0.0).astype(BF16)
            vg = vln[:, sl].astype(BF16)
            sg = jnp.dot(wm, vg, preferred_element_type=F32) + bc_ref[:, gi:gi + 1]
            dgt = dgt_ref[:, sl]
            du_scr[:, sl] = dgt * sg
            ds = dgt * u[:, sl]
            dbc_ref[:, gi:gi + 1] += jnp.sum(ds, axis=-1, keepdims=True)
            dsb = ds.astype(BF16)
            dwm = lax.dot_general(dsb, vg, (((1,), (1,)), ((), ())), preferred_element_type=F32)
            dws_ref[gi] += jnp.where(mask, dwm, 0.0)
            dvln_scr[:, sl] = lax.dot_general(wm, dsb, (((0,), (0,)), ((), ())), preferred_element_type=F32)
        dvln = dvln_scr[...]
        dlb_ref[...] += jnp.sum(dvln, axis=0, keepdims=True)
        dlg_ref[...] += jnp.sum(dvln * vhat, axis=0, keepdims=True)
        dvh = dvln * lg
        dv = rs * (dvh - jnp.mean(dvh, axis=-1, keepdims=True)
                   - vhat * jnp.mean(dvh * vhat, axis=-1, keepdims=True))
        dz_ref[:, :aw] = (du_scr[...] * _gelu_grad(zu)).astype(BF16)
        dz_ref[:, aw:] = (dv * _gelu_grad(zv)).astype(BF16)

    return pl.pallas_call(
        body, name=name, grid=(s // chunk,),
        in_specs=[_row_spec(chunk, aw2), _row_spec(chunk, aw), _vec_spec(aw), _vec_spec(aw),
                  pl.BlockSpec((ng, chunk, chunk), lambda i: (0, 0, 0)),
                  pl.BlockSpec((chunk, ng), lambda i: (0, 0))],
        out_specs=[_row_spec(chunk, aw2), pl.BlockSpec((ng, chunk, chunk), lambda i: (0, 0, 0)),
                   pl.BlockSpec((chunk, ng), lambda i: (0, 0)), _vec_spec(aw), _vec_spec(aw)],
        out_shape=[jax.ShapeDtypeStruct((s, aw2), BF16), jax.ShapeDtypeStruct((ng, chunk, chunk), F32),
                   jax.ShapeDtypeStruct((chunk, ng), F32), jax.ShapeDtypeStruct((1, aw), F32),
                   jax.ShapeDtypeStruct((1, aw), F32)],
        scratch_shapes=[pltpu.VMEM((chunk, aw), F32), pltpu.VMEM((chunk, aw), F32)],
        compiler_params=_params("arbitrary"),
    )(zpre, dgated, ln_g, ln_b, w_s, b_col)


def _pool_select(g, parts):
    out = parts[-1]
    for gi in range(len(parts) - 2, -1, -1):
        out = jnp.where(g == gi, parts[gi], out)
    return out


def _pool_specs(s, bw):
    head = bw // len(B_WINDOWS)
    tc = _tile(head, 256)
    nb = head // tc
    return tc, (len(B_WINDOWS), nb), pl.BlockSpec((s, tc), lambda g, j: (0, g * nb + j))


def _pool_window(g, t):
    w = _pool_select(g, [jnp.full(t.shape, wi, jnp.int32) for wi in B_WINDOWS])
    return jnp.minimum(t + 1, w).astype(F32)


def _pool_fwd(name, vb):
    assert B_WINDOWS == (2, 4, 8, 16)
    s, bw = vb.shape
    tc, grid, spec = _pool_specs(s, bw)

    def body(v_ref, o_ref):
        g = pl.program_id(0)
        v = v_ref[...]
        t = lax.broadcasted_iota(jnp.int32, (s, tc), 0)

        def down(x, k):
            return jnp.where(t >= k, pltpu.roll(x, k, 0), 0.0)

        sums, cur, k = [], v, 1
        for _ in B_WINDOWS:
            cur = cur + down(cur, k)
            sums.append(cur)
            k *= 2
        o_ref[...] = (_pool_select(g, sums) / _pool_window(g, t) - v).astype(BF16)

    return pl.pallas_call(
        body, name=name, grid=grid, in_specs=[spec], out_specs=spec,
        out_shape=jax.ShapeDtypeStruct((s, bw), BF16), compiler_params=_params("parallel", "parallel"),
    )(vb)


def _pool_bwd(name, dpooled):
    s, bw = dpooled.shape
    tc, grid, spec = _pool_specs(s, bw)

    def body(d_ref, o_ref):
        g = pl.program_id(0)
        dp = d_ref[...]
        t = lax.broadcasted_iota(jnp.int32, (s, tc), 0)

        def up(x, k):
            return jnp.where(t < s - k, pltpu.roll(x, s - k, 0), 0.0)

        sums, cur, k = [], dp / _pool_window(g, t), 1
        for _ in B_WINDOWS:
            cur = cur + up(cur, k)
            sums.append(cur)
            k *= 2
        o_ref[...] = (_pool_select(g, sums) - dp).astype(BF16)

    return pl.pallas_call(
        body, name=name, grid=grid, in_specs=[spec], out_specs=spec,
        out_shape=jax.ShapeDtypeStruct((s, bw), BF16), compiler_params=_params("parallel", "parallel"),
    )(dpooled)


def _scale_bwd(name, dms, mixed, scale):
    s, bw = dms.shape
    tr = _tile(s, ROW_TILE)

    def body(d_ref, m_ref, sc_ref, o_ref, ds_ref):
        @pl.when(pl.program_id(0) == 0)
        def _():
            ds_ref[...] = jnp.zeros_like(ds_ref)

        dv = d_ref[...]
        ds_ref[...] += jnp.sum(dv * m_ref[...], axis=0, keepdims=True)
        o_ref[...] = (dv * sc_ref[...]).astype(BF16)

    return pl.pallas_call(
        body, name=name, grid=(s // tr,), in_specs=[_row_spec(tr, bw), _row_spec(tr, bw), _vec_spec(bw)],
        out_specs=[_row_spec(tr, bw), _vec_spec(bw)],
        out_shape=[jax.ShapeDtypeStruct((s, bw), BF16), jax.ShapeDtypeStruct((1, bw), F32)],
        compiler_params=_params("arbitrary"),
    )(dms, mixed, scale)


def _adam_update(w, g, m, v):
    m = ADAM_B1 * m + (1.0 - ADAM_B1) * g
    v = ADAM_B2 * v + (1.0 - ADAM_B2) * (g * g)
    m_hat = m / (1.0 - ADAM_B1 ** ADAM_STEP)
    v_hat = v / (1.0 - ADAM_B2 ** ADAM_STEP)
    delta = -ADAM_LR * (m_hat / (jnp.sqrt(v_hat) + ADAM_EPS) + ADAM_WD * w)
    return delta, m, v


def _adam_shard(name, w, m, v, parts, layer=0, prev=None, deps=()):
    nl, r, c = w.shape
    nj, nlp = parts.shape[:2]
    tr = _tile(r, ROW_TILE // 2)
    spec = pl.BlockSpec((None, tr, c), lambda l, i: (layer + l, i, 0))

    def body(w_ref, m_ref, v_ref, p_ref, *rest):
        g_ref, d_ref, nm_ref, nv_ref = rest[-4:]
        g = p_ref[0].astype(F32)
        for j in range(1, nj):
            g = g + p_ref[j].astype(F32)
        delta, nm, nv = _adam_update(w_ref[...], g, m_ref[...], v_ref[...])
        g_ref[...] = g
        d_ref[...] = delta
        nm_ref[...] = nm
        nv_ref[...] = nv

    prev = () if prev is None else tuple(prev)
    return pl.pallas_call(
        body, name=name, grid=(nlp, r // tr),
        in_specs=[spec, spec, spec, pl.BlockSpec((nj, None, tr, c), lambda l, i: (0, l, i, 0))]
        + [_ANY] * (len(prev) + len(deps)),
        out_specs=[spec] * 4, out_shape=[jax.ShapeDtypeStruct(w.shape, F32)] * 4,
        input_output_aliases={4 + i: i for i in range(len(prev))},
        compiler_params=_params("parallel", "parallel"),
    )(w, m, v, parts, *prev, *deps)


def _adam_packed(name, w, g, m, v):
    rows, lanes = w.shape
    tr = _tile(rows, 512)
    spec = pl.BlockSpec((tr, lanes), lambda i: (i, 0))

    def body(w_ref, g_ref, m_ref, v_ref, d_ref, nm_ref, nv_ref):
        delta, nm, nv = _adam_update(w_ref[...], g_ref[...], m_ref[...], v_ref[...])
        d_ref[...] = delta
        nm_ref[...] = nm
        nv_ref[...] = nv

    return pl.pallas_call(
        body, name=name, grid=(rows // tr,), in_specs=[spec] * 4, out_specs=[spec] * 3,
        out_shape=[jax.ShapeDtypeStruct(w.shape, F32)] * 3, compiler_params=_params("parallel"),
    )(w, g, m, v)


def _place():
    x, y, c = lax.axis_index("x"), lax.axis_index("y"), lax.axis_index("c")
    chips = [(1 - x, y), (x, 1 - y), (1 - x, 1 - y)]
    return x, y, c, chips


def _remote(src, dst, send_sem, recv_sem, device):
    return pltpu.make_async_remote_copy(src_ref=src, dst_ref=dst, send_sem=send_sem, recv_sem=recv_sem,
                                        device_id=device, device_id_type=MESH)


def _half(ref_rows, cc):
    h = ref_rows // 2
    return pl.ds(cc * h, h)


_HBM = pl.BlockSpec(memory_space=pltpu.HBM)
_SEM = pl.BlockSpec(memory_space=pltpu.SEMAPHORE)
_EFFECT = pltpu.SideEffectType.DATAFLOW_SIDE_EFFECTING


def _in_hbm(arr):
    return pltpu.with_memory_space_constraint(arr, pltpu.HBM)


def _gather_block(land, shard_rows, split, j, cc):
    if split:
        return land.at[j, :, _half(shard_rows, cc), :]
    return land.at[j]


def _split_start(name, srcs, lands, plan, deps=()):
    ns, nl = len(srcs), len(lands)
    both = list(srcs) + list(lands)
    nb = ns + nl

    def body(*refs):
        s, ld = refs[:ns], refs[ns:nb]
        outs = refs[nb + len(deps):]
        send, recv, token = outs[:nl], outs[nl:2 * nl], outs[-1]
        for a, copies in enumerate(plan(s, ld)):
            for src, dst, peer in copies:
                _remote(src, dst, send[a], recv[a], peer).start()
        token[...] = jnp.zeros_like(token)

    outs = pl.pallas_call(
        body, name=name, in_specs=[_HBM] * nb + [_ANY] * len(deps),
        out_specs=[_SEM] * (2 * nl) + [_HBM] * nb + [pl.BlockSpec(memory_space=pltpu.VMEM)],
        out_shape=[pltpu.SemaphoreType.DMA(())] * (2 * nl) + [pltpu.HBM(b.shape, b.dtype) for b in both]
        + [jax.ShapeDtypeStruct((8, LANES), F32)],
        input_output_aliases={i: 2 * nl + i for i in range(nb)},
        compiler_params=pltpu.CompilerParams(has_side_effects=_EFFECT),
    )(*[_in_hbm(b) for b in both], *deps)
    thru = outs[2 * nl:2 * nl + nb]
    return list(outs[:nl]), list(outs[nl:2 * nl]), list(thru[:ns]), list(thru[ns:]), outs[-1]


def _split_wait(name, srcs, lands, send, recv, after, whole):
    ns, nl = len(srcs), len(lands)
    both = list(srcs) + list(lands)
    nb = ns + nl

    def body(*refs):
        ld, snd, rcv = refs[ns:nb], refs[nb:nb + nl], refs[nb + nl:nb + 2 * nl]
        x, y, c, _ = _place()
        for a, blk in enumerate(whole(ld)):
            every = _remote(blk, blk, snd[a], rcv[a], (x, y, c))
            every.wait_send()
            every.wait_recv()

    outs = pl.pallas_call(
        body, name=name, in_specs=[_HBM] * nb + [_SEM] * (2 * nl) + [_ANY], out_specs=[_HBM] * nb,
        out_shape=[pltpu.HBM(b.shape, b.dtype) for b in both],
        input_output_aliases={i: i for i in range(nb)},
        compiler_params=pltpu.CompilerParams(has_side_effects=_EFFECT),
    )(*both, *send, *recv, after)
    return list(outs[:ns]), list(outs[ns:])


def _gather_plan(split):
    def plan(srcs, lands):
        x, y, c, chips = _place()
        out = []
        for ld, sp in zip(lands, split):
            blk = _gather_block(ld, ld.shape[2], sp, 2 * x + y, c)
            out.append([(blk, blk, (qx, qy, c)) for qx, qy in chips])
        return out
    return plan


def _gather_whole(split):
    def whole(lands):
        _, _, c, _ = _place()
        return [ld.at[pl.ds(0, 3), :, _half(ld.shape[2], c), :] if sp else ld.at[pl.ds(0, 3)]
                for ld, sp in zip(lands, split)]
    return whole


def _relay_plan(srcs, lands):
    x, y, c, _ = _place()
    out = []
    for ld in lands:
        blocks = [ld.at[j, :, _half(ld.shape[2], c), :] for j in range(N_CHIPS)]
        out.append([(blk, blk, (x, y, 1 - c)) for blk in blocks])
    return out


def _relay_whole(lands):
    _, _, c, _ = _place()
    return [ld.at[:, :, _half(ld.shape[2], c), :] for ld in lands]


def _fill_own(name, shard, dtype, place, layer=None):
    nl, r, c = shard.shape
    first = 0
    if layer is not None:
        nl, first = 1, layer
    tr = _tile(r, 512)

    def body(p_ref, s_ref, o_ref):
        o_ref[...] = s_ref[...].astype(o_ref.dtype)

    return pl.pallas_call(
        body, name=name,
        grid_spec=pltpu.PrefetchScalarGridSpec(
            num_scalar_prefetch=1, grid=(nl, r // tr),
            in_specs=[pl.BlockSpec((None, tr, c), lambda l, i, p: (first + l, i, 0))],
            out_specs=pl.BlockSpec((None, None, tr, c), lambda l, i, p: (p[0], l, i, 0))),
        out_shape=jax.ShapeDtypeStruct((N_CHIPS, nl, r, c), dtype),
        compiler_params=_params("parallel", "parallel"),
    )(place, shard)


def _gather_finish(name, lands):
    n = len(lands)

    def body(*refs):
        outs = refs[n:2 * n]
        fsend, frecv = refs[2 * n:]
        x, y, c, chips = _place()
        sib = (x, y, 1 - c)

        def relay(a, qi, cc):
            qx, qy = chips[qi]
            blk = _gather_block(outs[a], outs[a].shape[2], True, 2 * qx + qy, cc)
            return _remote(blk, blk, fsend.at[a, qi], frecv.at[a, qi], sib)

        relays = [relay(a, qi, c) for a in range(n) for qi in range(3)]
        for cp in relays:
            cp.start()
        for a in range(n):
            for qi in range(3):
                relay(a, qi, 1 - c).wait_recv()
        for cp in relays:
            cp.wait_send()

    outs = pl.pallas_call(
        body, name=name, in_specs=[_ANY] * n, out_specs=[_ANY] * n,
        out_shape=[jax.ShapeDtypeStruct(ld.shape, ld.dtype) for ld in lands],
        input_output_aliases={i: i for i in range(n)},
        scratch_shapes=[pltpu.SemaphoreType.DMA((n, 3))] * 2,
    )(*lands)
    return list(outs)


def _pair_plan(srcs, lands):
    x, y, c, _ = _place()
    return [[(s.at[:, :, _half(s.shape[2], 1 - c), :], ld, (x, y, 1 - c))] for s, ld in zip(srcs, lands)]


def _pair_whole(lands):
    return list(lands)


def _scatter_plan(srcs, lands):
    x, y, c, chips = _place()
    return [[(s.at[2 * qx + qy], ld.at[2 * x + y, :, _half(ld.shape[2], c), :], (qx, qy, c)) for qx, qy in chips]
            for s, ld in zip(srcs, lands)]


def _scatter_whole(lands):
    _, _, c, _ = _place()
    return [ld.at[pl.ds(0, 3), :, _half(ld.shape[2], c), :] for ld in lands]


def _pair_sum(name, grad, other, place):
    nj, nl, r, c = grad.shape
    h = r // 2
    tr = _tile(h, 2 * ROW_TILE)
    nb = h // tr

    def body(p_ref, g_ref, o_ref, q_ref, d_ref):
        q = (g_ref[...].astype(F32) + o_ref[...].astype(F32)).astype(BF16)
        q_ref[...] = q

        @pl.when(pl.program_id(2) == p_ref[0])
        def _():
            d_ref[...] = q

    blk = (None, None, tr, c)
    return pl.pallas_call(
        body, name=name,
        grid_spec=pltpu.PrefetchScalarGridSpec(
            num_scalar_prefetch=1, grid=(nl, nb, nj),
            in_specs=[pl.BlockSpec(blk, lambda l, i, j, p: (j, l, p[1] * nb + i, 0)),
                      pl.BlockSpec(blk, lambda l, i, j, p: (j, l, i, 0))],
            out_specs=[pl.BlockSpec(blk, lambda l, i, j, p: (j, l, i, 0)),
                       pl.BlockSpec(blk, lambda l, i, j, p: (p[0], l, p[1] * nb + i, 0))]),
        out_shape=[jax.ShapeDtypeStruct((nj, nl, h, c), BF16), jax.ShapeDtypeStruct((nj, nl, r, c), BF16)],
        compiler_params=_params("parallel", "parallel", "arbitrary"),
    )(place, grad, other)


def _allreduce_small(name, packed):
    rows, lanes = packed.shape

    def body(x_ref, o_ref, all_ref, send, recv, lsem):
        x, y, c, chips = _place()
        me, sib = (x, y, c), (x, y, 1 - c)

        def slot(px, py, pc):
            return all_ref.at[pl.ds((4 * px + 2 * py + pc) * rows, rows), :]

        def copy(k, blk, to, src=None):
            return _remote(slot(*blk) if src is None else src, slot(*blk), send.at[k], recv.at[k], to)

        mine = pltpu.make_async_copy(x_ref, slot(*me), lsem)
        mine.start()
        first = [copy(0, me, sib, src=x_ref)] + [copy(1 + j, me, (*chip, c), src=x_ref) for j, chip in enumerate(chips)]
        for cp in first:
            cp.start()
        passed = [copy(4 + j, (*chip, c), sib) for j, chip in enumerate(chips)]
        for j, chip in enumerate(chips):
            copy(1 + j, (*chip, c), me).wait_recv()
            passed[j].start()
        copy(0, sib, me).wait_recv()
        for j, chip in enumerate(chips):
            copy(4 + j, (*chip, 1 - c), me).wait_recv()
        for cp in first + passed:
            cp.wait_send()
        mine.wait()
        total = all_ref[pl.ds(0, rows), :]
        for d in range(1, N_DEV):
            total = total + all_ref[pl.ds(d * rows, rows), :]
        o_ref[...] = total

    vmem = pl.BlockSpec(memory_space=pltpu.VMEM)
    return pl.pallas_call(
        body, name=name, in_specs=[vmem], out_specs=vmem, out_shape=jax.ShapeDtypeStruct(packed.shape, F32),
        scratch_shapes=[pltpu.VMEM((N_DEV * rows, lanes), F32), pltpu.SemaphoreType.DMA((7,)),
                        pltpu.SemaphoreType.DMA((7,)), pltpu.SemaphoreType.DMA],
        compiler_params=pltpu.CompilerParams(vmem_limit_bytes=VMEM_LIMIT),
    )(packed)


def _pack(parts):
    rows = []
    for p in parts:
        flat = p.reshape(-1)
        pad = (-flat.shape[0]) % PACK_ELEMS
        rows.append(jnp.pad(flat, (0, pad)).reshape(-1, LANES))
    return jnp.concatenate(rows, axis=0)


def _unpack(packed, shapes):
    out, row = [], 0
    for sh in shapes:
        size = math.prod(sh)
        nrows = -(-size // PACK_ELEMS) * (PACK_ELEMS // LANES)
        out.append(packed[row:row + nrows].reshape(-1)[:size].reshape(sh))
        row += nrows
    return out


def kernel(x, a_w_in, a_ln_g, a_ln_b, a_w_s, a_b_s, a_w_out, b_w_in, b_w_grp, b_scale, b_w_out, norm_mix, norm_mlp, mlp_w1, mlp_w2, final_norm, loss_target, m_a_w_in, m_a_ln_g, m_a_ln_b, m_a_w_s, m_a_b_s, m_a_w_out, m_b_w_in, m_b_w_grp, m_b_scale, m_b_w_out, m_norm_mix, m_norm_mlp, m_mlp_w1, m_mlp_w2, m_final_norm, v_a_w_in, v_a_ln_g, v_a_ln_b, v_a_w_s, v_a_b_s, v_a_w_out, v_b_w_in, v_b_w_grp, v_b_scale, v_b_w_out, v_norm_mix, v_norm_mlp, v_mlp_w1, v_mlp_w2, v_final_norm):
    xi, yi, ci = lax.axis_index("x"), lax.axis_index("y"), lax.axis_index("c")
    chip = 2 * xi + yi
    place = jnp.stack([chip, ci]).astype(jnp.int32)
    x2, tgt = x[0], loss_target[0]
    bw = b_scale.shape[1] * N_CHIPS

    units = dict(a_w_in=(a_w_in, None), a_w_out=(a_w_out, None), w1_0=(mlp_w1, 0), w2_0=(mlp_w2, 0),
                 b_scale=(b_scale.reshape(1, 1, -1), None), b_w_in=(b_w_in, None), b_w_grp=(b_w_grp[0], None),
                 b_w_out=(b_w_out, None), w1_1=(mlp_w1, 1), w2_1=(mlp_w2, 1))
    col_sharded = dict(a_w_in=True, a_w_out=False, b_w_in=False, b_w_grp=False, b_w_out=False,
                       w1_0=True, w2_0=False, w1_1=True, w2_1=False)
    in_flight, W = {}, {}

    def launch(tag, keys, deps):
        sp = [k != "b_scale" for k in keys]
        zones = [_fill_own(f"gather_own_{k}", units[k][0], BF16 if s else F32, place, layer=units[k][1])
                 for k, s in zip(keys, sp)]
        send, recv, _, zones, tok = _split_start(f"gather_start_{tag}", [], zones, _gather_plan(sp), deps)
        in_flight.update({k: state for k, *state in zip(keys, send, recv, zones, sp)})
        return tok

    def arrive(keys, after):
        send, recv, zones, sp = zip(*[in_flight[k] for k in keys])
        _, zones = _split_wait(f"gather_wait_{keys[0]}", [], zones, send, recv, after, _gather_whole(sp))
        relayed = iter(_gather_finish(f"gather_finish_{keys[0]}", [z for z, s in zip(zones, sp) if s]))
        for k, z, s in zip(keys, zones, sp):
            full = next(relayed) if s else z
            W[k] = _W4(full, col_sharded[k]) if k in col_sharded else full

    token = launch("first", ["a_w_in", "a_w_out"], ())
    token = launch("rest", ["w1_0", "w2_0", "b_scale", "b_w_in", "b_w_grp", "b_w_out", "w1_1", "w2_1"], (token,))

    b_col = a_b_s[0].T

    def residual(acc, res):
        return (res + acc,)

    def sq_relu(acc):
        act = jnp.maximum(acc, 0.0)
        return act, act * act

    def mlp_fwd(tag, h, layer):
        hn = _rms_fwd(f"mlp{tag}_norm", h, norm_mlp[layer:layer + 1])
        arrive([f"w1_{layer}"], hn)
        act, act_sq = _mm_aw(f"mlp{tag}_up", hn, W[f"w1_{layer}"], out_dtypes=(BF16, BF16), epilogue=sq_relu)
        arrive([f"w2_{layer}"], act_sq)
        out = _mm_aw(f"mlp{tag}_down", act_sq, W[f"w2_{layer}"], extras=(h,), epilogue=residual)
        return out, (h, hn, act, act_sq)

    hn0 = _rms_fwd("mix_a_norm", x2, norm_mix[0:1])
    arrive(["a_w_in"], token)
    zpre = _mm_aw("mix_a_in", hn0, W["a_w_in"])
    gated = _gate_fwd("mix_a_gate", zpre, a_ln_g, a_ln_b, a_w_s[0], b_col)
    arrive(["a_w_out"], gated)
    h1 = _mm_aw("mix_a_out", gated, W["a_w_out"], extras=(x2,), epilogue=residual)
    h2, mlp0 = mlp_fwd("0", h1, 0)
    hn2 = _rms_fwd("mix_b_norm", h2, norm_mix[1:2])
    arrive(["b_scale", "b_w_in"], hn2)
    scale_full = W["b_scale"].reshape(1, bw)
    vb = _mm_aw("mix_b_in", hn2, W["b_w_in"])
    pooled = _pool_fwd("mix_b_pool", vb)
    arrive(["b_w_grp", "b_w_out"], pooled)
    mixed, ms = _mm_aw("mix_b_grp", pooled, W["b_w_grp"], groups=len(B_WINDOWS), extras=(scale_full,),
                       out_dtypes=(F32, BF16), epilogue=lambda acc, sc: (acc, acc * sc))
    h3 = _mm_aw("mix_b_out", ms, W["b_w_out"], extras=(h2,), epilogue=residual)
    h4, mlp1 = mlp_fwd("1", h3, 1)
    dh4, dh4_b, d_final, loss_part = _final("loss_head", h4, final_norm.reshape(1, -1), tgt)
    g1_like = _W4(None, True, shape=(N_CHIPS, 1, *W["w1_0"].arr.shape[2:]))
    g2_like = _W4(None, False, shape=(N_CHIPS, 1, *W["w2_0"].arr.shape[2:]))

    def exchange(tag, gs):
        zones = [lax.empty((g.shape[0], g.shape[1], g.shape[2] // 2, g.shape[3]), g.dtype) for g in gs]
        send, recv, srcs, zones, tok = _split_start(f"pair_start_{tag}", gs, zones, _pair_plan)
        return (tag, send, recv, srcs, zones), tok

    def reduce(state, after):
        tag, send, recv, srcs, zones = state
        srcs, zones = _split_wait(f"pair_wait_{tag}", srcs, zones, send, recv, after, _pair_whole)
        both = [_pair_sum(f"pair_sum_{tag}_{i}", g, o, place) for i, (g, o) in enumerate(zip(srcs, zones))]
        send, recv, sums, dests, tok = _split_start(f"scatter_start_{tag}", [b[0] for b in both],
                                                    [b[1] for b in both], _scatter_plan)
        return (tag, send, recv, sums, dests), tok

    def relay(state, after):
        tag, send, recv, sums, dests = state
        _, dests = _split_wait(f"scatter_wait_{tag}", sums, dests, send, recv, after, _scatter_whole)
        send, recv, _, dests, tok = _split_start(f"relay_start_{tag}", [], dests, _relay_plan)
        return (tag, send, recv, dests), tok

    def land(state, after):
        tag, send, recv, dests = state
        return _split_wait(f"relay_wait_{tag}", [], dests, send, recv, after, _relay_whole)[1]

    def mlp_bwd(tag, dh, dh_b, saved, layer, deps, pending=None):
        h, hn, act, act_sq = saved
        dpre = _mm_aw(f"mlp{tag}_down_dx", dh_b, W[f"w2_{layer}"], transpose_w=True, extras=(act,),
                      out_dtypes=(BF16,), epilogue=lambda acc, a: (acc * (2.0 * a),), deps=deps)
        scattering, dw_deps = None, ()
        if pending is not None:
            scattering, tok = reduce(pending, dpre)
            dw_deps = (tok,)
        g_w2 = _mm_dw(f"mlp{tag}_down_dw", act_sq, dh_b, g2_like, deps=dw_deps)
        pair_w2, tok = exchange(f"w2_{layer}", [g_w2])
        dhn = _mm_aw(f"mlp{tag}_up_dx", dpre, W[f"w1_{layer}"], transpose_w=True, deps=(tok,))
        g_w1 = _mm_dw(f"mlp{tag}_up_dw", hn, dpre, g1_like)
        pair_w1, tok1 = exchange(f"w1_{layer}", [g_w1])
        scat_w2, tok2 = reduce(pair_w2, dhn)
        dh_in, dh_in_b, d_norm = _rms_bwd(f"mlp{tag}_norm_bwd", h, norm_mlp[layer:layer + 1], dhn, dh)
        return dh_in, dh_in_b, d_norm, pair_w1, scat_w2, (tok1, tok2), scattering

    dh3, dh3_b, d_norm_mlp1, pair_w1_1, scat_w2_1, toks, _ = mlp_bwd("1", dh4, dh4_b, mlp1, 1, ())
    dms = _mm_aw("mix_b_out_dx", dh3_b, W["b_w_out"], transpose_w=True, deps=toks)
    g_b_out = _mm_dw("mix_b_out_dw", ms, dh3_b, W["b_w_out"])
    scat_w1_1, tok = reduce(pair_w1_1, dms)
    dmixed, d_scale = _scale_bwd("mix_b_scale_bwd", dms, mixed, scale_full)
    dpooled = _mm_aw("mix_b_grp_dx", dmixed, W["b_w_grp"], groups=len(B_WINDOWS), transpose_w=True, deps=(tok,))
    g_b_grp = _mm_dw("mix_b_grp_dw", pooled, dmixed, W["b_w_grp"], groups=len(B_WINDOWS))
    dvb = _pool_bwd("mix_b_pool_bwd", dpooled)
    dhn2 = _mm_aw("mix_b_in_dx", dvb, W["b_w_in"], transpose_w=True)
    g_b_in = _mm_dw("mix_b_in_dw", hn2, dvb, W["b_w_in"])
    pair_b, tok = exchange("b", [g_b_out, g_b_grp, g_b_in])
    dh2, dh2_b, d_norm_mix1 = _rms_bwd("mix_b_norm_bwd", h2, norm_mix[1:2], dhn2, dh3)
    dh1, dh1_b, d_norm_mlp0, pair_w1_0, scat_w2_0, toks, scat_b = mlp_bwd("0", dh2, dh2_b, mlp0, 0, (tok,),
                                                                          pending=pair_b)
    dgated = _mm_aw("mix_a_out_dx", dh1_b, W["a_w_out"], transpose_w=True, deps=toks)
    g_a_out = _mm_dw("mix_a_out_dw", gated, dh1_b, W["a_w_out"])
    pair_a_out, tok_a = exchange("a_out", [g_a_out])
    scat_w1_0, tok = reduce(pair_w1_0, dgated)
    early = [relay(state, dgated) for state in (scat_w2_1, scat_w1_1, scat_b)]
    dzpre, d_w_s, d_b_col, d_ln_g, d_ln_b = _gate_bwd("mix_a_gate_bwd", zpre, dgated, a_ln_g, a_ln_b, a_w_s[0], b_col)
    dhn0 = _mm_aw("mix_a_in_dx", dzpre, W["a_w_in"], transpose_w=True, deps=(tok, tok_a, *[t for _, t in early]))
    scat_a_out, tok = reduce(pair_a_out, dhn0)
    g_a_in = _mm_dw("mix_a_in_dw", hn0, dzpre, W["a_w_in"], deps=(tok,))
    pair_a_in, tok = exchange("a_in", [g_a_in])
    dx, _, d_norm_mix0 = _rms_bwd("mix_a_norm_bwd", x2, norm_mix[0:1], dhn0, dh1)
    scat_a_in, _ = reduce(pair_a_in, dx)

    moments = dict(a_w_in=(m_a_w_in, v_a_w_in), a_w_out=(m_a_w_out, v_a_w_out), b_w_in=(m_b_w_in, v_b_w_in),
                   b_w_grp=(m_b_w_grp, v_b_w_grp), b_w_out=(m_b_w_out, v_b_w_out),
                   mlp_w1=(m_mlp_w1, v_mlp_w1), mlp_w2=(m_mlp_w2, v_mlp_w2))
    weights = dict(a_w_in=a_w_in, a_w_out=a_w_out, b_w_in=b_w_in, b_w_grp=b_w_grp, b_w_out=b_w_out,
                   mlp_w1=mlp_w1, mlp_w2=mlp_w2)
    landing = [(scat_w2_1, [("mlp_w2", 1)]), (scat_w1_1, [("mlp_w1", 1)]),
               (scat_b, [("b_w_out", 0), ("b_w_grp", 0), ("b_w_in", 0)]),
               (scat_w2_0, [("mlp_w2", 0)]), (scat_w1_0, [("mlp_w1", 0)]),
               (scat_a_out, [("a_w_out", 0)]), (scat_a_in, [("a_w_in", 0)])]
    results, after = {}, dx
    relays = list(early)
    for i, (_, members) in enumerate(landing):
        deps = ()
        if len(relays) == i + 1 < len(landing):
            relays.append(relay(landing[i + 1][0], after))
            deps = (relays[-1][1],)
        for (k, layer), parts in zip(members, land(relays[i][0], relays[-1][1] if deps else after)):
            shard_shape = (-1, *parts.shape[2:])
            results[k] = _adam_shard(f"adam_{k}_{layer}", weights[k].reshape(shard_shape),
                                     moments[k][0].reshape(shard_shape), moments[k][1].reshape(shard_shape),
                                     parts, layer=layer, prev=results.get(k), deps=deps)
            after = results[k][1]
    grad_out, delta_out, m_out, v_out = {}, {}, {}, {}
    for k, res in results.items():
        grad_out[k], delta_out[k], m_out[k], v_out[k] = [r.reshape(weights[k].shape) for r in res]

    small = dict(a_ln_g=(a_ln_g, m_a_ln_g, v_a_ln_g), a_ln_b=(a_ln_b, m_a_ln_b, v_a_ln_b),
                 a_w_s=(a_w_s, m_a_w_s, v_a_w_s), a_b_s=(a_b_s, m_a_b_s, v_a_b_s),
                 b_scale=(b_scale, m_b_scale, v_b_scale), norm_mix=(norm_mix, m_norm_mix, v_norm_mix),
                 norm_mlp=(norm_mlp, m_norm_mlp, v_norm_mlp), final_norm=(final_norm, m_final_norm, v_final_norm))
    small_names = list(small)
    local = dict(a_ln_g=d_ln_g, a_ln_b=d_ln_b, a_w_s=d_w_s[None], a_b_s=d_b_col.T[None], b_scale=d_scale,
                 norm_mix=jnp.concatenate([d_norm_mix0, d_norm_mix1], axis=0),
                 norm_mlp=jnp.concatenate([d_norm_mlp0, d_norm_mlp1], axis=0), final_norm=d_final.reshape(-1))
    reduced = _allreduce_small("small_grad_allreduce", _pack([local[k] for k in small_names]))
    small_grads = dict(zip(small_names, _unpack(reduced, [local[k].shape for k in small_names])))
    shard_w = b_scale.shape[1]
    small_grads["b_scale"] = lax.dynamic_slice_in_dim(small_grads["b_scale"], chip * shard_w, shard_w, axis=1)
    small_grads = {k: small_grads[k].reshape(small[k][0].shape) for k in small_names}
    packed = [_pack([small[k][i] for k in small_names]) for i in range(3)]
    res = _adam_packed("adam_small", packed[0], _pack([small_grads[k] for k in small_names]), packed[1], packed[2])
    shapes = [small[k][0].shape for k in small_names]
    for k, d, nm, nv in zip(small_names, *[_unpack(r, shapes) for r in res]):
        grad_out[k], delta_out[k], m_out[k], v_out[k] = small_grads[k], d, nm, nv

    loss = lax.psum(loss_part[0, 0], ("x", "y", "c"))
    order = ["a_w_in", "a_ln_g", "a_ln_b", "a_w_s", "a_b_s", "a_w_out", "b_w_in", "b_w_grp", "b_scale", "b_w_out",
             "norm_mix", "norm_mlp", "mlp_w1", "mlp_w2", "final_norm"]
    return (loss, dx[None], *[grad_out[k] for k in order], *[delta_out[k] for k in order],
            *[m_out[k] for k in order], *[v_out[k] for k in order])
```

```python
import math

import jax
import jax.numpy as jnp
from jax import lax
from jax.experimental import pallas as pl
from jax.experimental.pallas import tpu as pltpu

F32 = jnp.float32
BF16 = jnp.bfloat16
MESH = pl.DeviceIdType.MESH

EPS = 1e-6
B_WINDOWS = (2, 4, 8, 16)
ADAM_LR = 0.001
ADAM_B1 = 0.9
ADAM_B2 = 0.999
ADAM_EPS = 1e-08
ADAM_WD = 0.01
ADAM_STEP = 10

N_CHIPS = 4
N_DEV = 8
LANES = 128
PACK_ELEMS = 8 * LANES
VMEM_LIMIT = 56 * 1024 * 1024
ROW_TILE = 512
MM_TM, MM_TN, MM_TK = 1024, 1024, 2048


_ANY = pl.BlockSpec(memory_space=pl.ANY)


def _tile(dim, pref):
    t = min(dim, pref)
    while dim % t:
        t //= 2
    return t


def _params(*sem):
    return pltpu.CompilerParams(dimension_semantics=sem, vmem_limit_bytes=VMEM_LIMIT)


class _W4:
    def __init__(self, arr, col_sharded, shape=None):
        self.arr = arr
        self.nj, self.nl, self.r, self.c = arr.shape if shape is None else shape
        self.col = col_sharded
        self.rows = self.r if col_sharded else self.nj * self.r
        self.cols = self.nj * self.c if col_sharded else self.c

    def tile_rows(self, pref):
        return _tile(self.r, pref)

    def tile_cols(self, pref):
        return _tile(self.c, pref)

    def index(self, layer, rb, cb, tr, tc):
        if self.col:
            n = self.c // tc
            return (cb // n, layer, rb, cb % n)
        n = self.r // tr
        return (rb // n, layer, rb % n, cb)


def _mm_aw(name, a, w, *, layer=0, groups=1, transpose_w=False, extras=(), out_dtypes=(F32,), epilogue=None,
           deps=()):
    s, ka_total = a.shape
    kdim, ndim = (w.cols, w.rows) if transpose_w else (w.rows, w.cols)
    assert ka_total == groups * kdim, (name, a.shape, kdim, groups)
    span = groups == 1 and not transpose_w and not w.col and kdim <= MM_TK
    if groups > 1:
        tm = _tile(s, MM_TM)
        tn, tk = (w.tile_rows(MM_TN), w.tile_cols(512)) if transpose_w else (w.tile_cols(MM_TN), w.tile_rows(512))
    else:
        tk = kdim if span else (w.tile_cols(MM_TK) if transpose_w else w.tile_rows(MM_TK))
        tm, tn_pref = (_tile(s, 2048), 512) if tk == kdim else (_tile(s, MM_TM), MM_TN)
        tn = w.tile_rows(tn_pref) if transpose_w else w.tile_cols(tn_pref)
    nk, nn = kdim // tk, ndim // tn

    def lay(g):
        return g if groups > 1 else layer

    a_spec = pl.BlockSpec((tm, tk), lambda g, i, n, k: (i, g * nk + k))
    if span:
        w_spec = pl.BlockSpec((w.nj, None, w.r, tn), lambda g, i, n, k: (0, layer, 0, n))
    elif transpose_w:
        w_spec = pl.BlockSpec((None, None, tn, tk), lambda g, i, n, k: w.index(lay(g), n, k, tn, tk))
    else:
        w_spec = pl.BlockSpec((None, None, tk, tn), lambda g, i, n, k: w.index(lay(g), k, n, tk, tn))
    ex_specs = []
    for e in extras:
        assert e.shape[1] == groups * ndim and e.shape[0] in (1, s), (name, e.shape)
        if e.shape[0] == 1:
            ex_specs.append(pl.BlockSpec((1, tn), lambda g, i, n, k: (0, g * nn + n)))
        else:
            ex_specs.append(pl.BlockSpec((tm, tn), lambda g, i, n, k: (i, g * nn + n)))
    out_spec = pl.BlockSpec((tm, tn), lambda g, i, n, k: (i, g * nn + n))
    n_ex, n_out, n_dep = len(extras), len(out_dtypes), len(deps)

    def body(a_ref, w_ref, *rest):
        ex, outs = rest[:n_ex], rest[n_ex + n_dep:n_ex + n_dep + n_out]
        av = a_ref[...]
        if av.dtype != BF16:
            av = av.astype(BF16)
        wv = w_ref[...].reshape(tk, tn) if span else w_ref[...]
        if transpose_w:
            prod = lax.dot_general(av, wv, (((1,), (1,)), ((), ())), preferred_element_type=F32)
        else:
            prod = jnp.dot(av, wv, preferred_element_type=F32)

        def finish(total):
            vals = (total,) if epilogue is None else epilogue(total, *[e[...] for e in ex])
            for o, v in zip(outs, vals):
                o[...] = v.astype(o.dtype)

        if nk == 1:
            finish(prod)
            return
        acc, k = rest[-1], pl.program_id(3)

        @pl.when(k == 0)
        def _():
            acc[...] = prod

        @pl.when(k > 0)
        def _():
            acc[...] += prod

        @pl.when(k == nk - 1)
        def _():
            finish(acc[...])

    outs = pl.pallas_call(
        body, name=name, grid=(groups, s // tm, nn, nk),
        in_specs=[a_spec, w_spec, *ex_specs] + [_ANY] * n_dep, out_specs=[out_spec] * n_out,
        out_shape=[jax.ShapeDtypeStruct((s, groups * ndim), dt) for dt in out_dtypes],
        scratch_shapes=[pltpu.VMEM((tm, tn), F32)] if nk > 1 else [],
        compiler_params=_params("parallel", "parallel", "parallel", "arbitrary"),
    )(a, w.arr, *extras, *deps)
    return outs[0] if n_out == 1 else outs


def _mm_dw(name, a, b, like, *, layer=0, groups=1, deps=()):
    s, ka_total = a.shape
    rows, cols = ka_total // groups, b.shape[1] // groups
    assert (rows, cols) == (like.rows, like.cols) and b.shape[0] == s, (name, a.shape, b.shape)
    tm, tn, tk = like.tile_rows(MM_TM), like.tile_cols(MM_TN), _tile(s, MM_TK)
    nr, nc, nk = rows // tm, cols // tn, s // tk

    def lay(g):
        return g if groups > 1 else layer

    in_specs = [pl.BlockSpec((tk, tm), lambda g, n, i, k: (k, g * nr + i)),
                pl.BlockSpec((tk, tn), lambda g, n, i, k: (k, g * nc + n))]
    in_specs += [_ANY] * len(deps)

    def body(a_ref, b_ref, *rest):
        av, bv = a_ref[...], b_ref[...]
        if av.dtype != BF16:
            av = av.astype(BF16)
        if bv.dtype != BF16:
            bv = bv.astype(BF16)
        prod = lax.dot_general(av, bv, (((0,), (0,)), ((), ())), preferred_element_type=F32)
        if nk == 1:
            rest[-1][...] = prod.astype(BF16)
            return
        o_ref, acc, k = rest[-2], rest[-1], pl.program_id(3)

        @pl.when(k == 0)
        def _():
            acc[...] = prod

        @pl.when(k > 0)
        def _():
            acc[...] += prod

        @pl.when(k == nk - 1)
        def _():
            o_ref[...] = acc[...].astype(BF16)

    return pl.pallas_call(
        body, name=name, grid=(groups, nc, nr, nk), in_specs=in_specs,
        out_specs=pl.BlockSpec((None, None, tm, tn), lambda g, n, i, k: like.index(lay(g), i, n, tm, tn)),
        out_shape=jax.ShapeDtypeStruct((like.nj, like.nl, like.r, like.c), BF16),
        scratch_shapes=[pltpu.VMEM((tm, tn), F32)] if nk > 1 else [],
        compiler_params=_params("parallel", "parallel", "parallel", "arbitrary"),
    )(a, b, *deps)


def _row_spec(tr, d):
    return pl.BlockSpec((tr, d), lambda i: (i, 0))


def _vec_spec(d):
    return pl.BlockSpec((1, d), lambda i: (0, 0))


def _rms_fwd(name, x, g):
    s, d = x.shape
    tr = _tile(s, ROW_TILE)

    def body(x_ref, g_ref, o_ref):
        xv = x_ref[...]
        r = lax.rsqrt(jnp.mean(xv * xv, axis=-1, keepdims=True) + EPS)
        o_ref[...] = (xv * r * g_ref[...]).astype(BF16)

    return pl.pallas_call(
        body, name=name, grid=(s // tr,), in_specs=[_row_spec(tr, d), _vec_spec(d)],
        out_specs=_row_spec(tr, d), out_shape=jax.ShapeDtypeStruct((s, d), BF16),
        compiler_params=_params("parallel"),
    )(x, g)


def _rms_bwd(name, x, g, dhn, dres):
    s, d = x.shape
    tr = _tile(s, ROW_TILE)

    def body(x_ref, g_ref, dhn_ref, dres_ref, dx_ref, dxb_ref, dg_ref):
        @pl.when(pl.program_id(0) == 0)
        def _():
            dg_ref[...] = jnp.zeros_like(dg_ref)

        xv = x_ref[...]
        r = lax.rsqrt(jnp.mean(xv * xv, axis=-1, keepdims=True) + EPS)
        xh = xv * r
        dy = dhn_ref[...]
        dg_ref[...] += jnp.sum(dy * xh, axis=0, keepdims=True)
        dxh = dy * g_ref[...]
        dx = dres_ref[...] + r * (dxh - xh * jnp.mean(dxh * xh, axis=-1, keepdims=True))
        dx_ref[...] = dx
        dxb_ref[...] = dx.astype(BF16)

    return pl.pallas_call(
        body, name=name, grid=(s // tr,),
        in_specs=[_row_spec(tr, d), _vec_spec(d), _row_spec(tr, d), _row_spec(tr, d)],
        out_specs=[_row_spec(tr, d), _row_spec(tr, d), _vec_spec(d)],
        out_shape=[jax.ShapeDtypeStruct((s, d), F32), jax.ShapeDtypeStruct((s, d), BF16),
                   jax.ShapeDtypeStruct((1, d), F32)],
        compiler_params=_params("arbitrary"),
    )(x, g, dhn, dres)


def _final(name, h, g, tgt):
    s, d = h.shape
    tr = _tile(s, ROW_TILE)

    def body(h_ref, g_ref, t_ref, dh_ref, dhb_ref, dg_ref, loss_ref):
        @pl.when(pl.program_id(0) == 0)
        def _():
            dg_ref[...] = jnp.zeros_like(dg_ref)
            loss_ref[...] = jnp.zeros_like(loss_ref)

        hv = h_ref[...]
        r = lax.rsqrt(jnp.mean(hv * hv, axis=-1, keepdims=True) + EPS)
        xh = hv * r
        gv = g_ref[...]
        err = xh * gv - t_ref[...]
        part = 0.5 * jnp.sum(jnp.mean(err * err, axis=-1, keepdims=True), axis=0, keepdims=True)
        loss_ref[...] += jnp.broadcast_to(part, loss_ref.shape)
        dy = err * (1.0 / d)
        dg_ref[...] += jnp.sum(dy * xh, axis=0, keepdims=True)
        dxh = dy * gv
        dh = r * (dxh - xh * jnp.mean(dxh * xh, axis=-1, keepdims=True))
        dh_ref[...] = dh
        dhb_ref[...] = dh.astype(BF16)

    return pl.pallas_call(
        body, name=name, grid=(s // tr,),
        in_specs=[_row_spec(tr, d), _vec_spec(d), _row_spec(tr, d)],
        out_specs=[_row_spec(tr, d), _row_spec(tr, d), _vec_spec(d), _vec_spec(LANES)],
        out_shape=[jax.ShapeDtypeStruct((s, d), F32), jax.ShapeDtypeStruct((s, d), BF16),
                   jax.ShapeDtypeStruct((1, d), F32), jax.ShapeDtypeStruct((1, LANES), F32)],
        compiler_params=_params("arbitrary"),
    )(h, g, tgt)


_SQRT_HALF = 1.0 / math.sqrt(2.0)
_INV_SQRT_2PI = 1.0 / math.sqrt(2.0 * math.pi)


def _gelu(x):
    return x * (lax.erf(x * _SQRT_HALF) + 1.0) * 0.5


def _gelu_grad(x):
    return 0.5 * (lax.erf(x * _SQRT_HALF) + 1.0) + x * jnp.exp(-0.5 * x * x) * _INV_SQRT_2PI


def _causal(chunk):
    t = lax.broadcasted_iota(jnp.int32, (chunk, chunk), 0)
    sidx = lax.broadcasted_iota(jnp.int32, (chunk, chunk), 1)
    return sidx <= t


def _layernorm_parts(v, g, b):
    mu = jnp.mean(v, axis=-1, keepdims=True)
    vc = v - mu
    rs = lax.rsqrt(jnp.mean(vc * vc, axis=-1, keepdims=True) + EPS)
    vhat = vc * rs
    return vhat, rs, vhat * g + b


def _gate_fwd(name, zpre, ln_g, ln_b, w_s, b_col):
    s, aw2 = zpre.shape
    aw = aw2 // 2
    ng, chunk, _ = w_s.shape
    dh = aw // ng

    def body(z_ref, g_ref, b_ref, ws_ref, bc_ref, o_ref):
        u = _gelu(z_ref[:, :aw])
        v = _gelu(z_ref[:, aw:])
        _, _, vln = _layernorm_parts(v, g_ref[...], b_ref[...])
        mask = _causal(chunk)
        for gi in range(ng):
            sl = slice(gi * dh, (gi + 1) * dh)
            wm = jnp.where(mask, ws_ref[gi], 0.0).astype(BF16)
            sg = jnp.dot(wm, vln[:, sl].astype(BF16), preferred_element_type=F32) + bc_ref[:, gi:gi + 1]
            o_ref[:, sl] = (u[:, sl] * sg).astype(BF16)

    return pl.pallas_call(
        body, name=name, grid=(s // chunk,),
        in_specs=[_row_spec(chunk, aw2), _vec_spec(aw), _vec_spec(aw),
                  pl.BlockSpec((ng, chunk, chunk), lambda i: (0, 0, 0)),
                  pl.BlockSpec((chunk, ng), lambda i: (0, 0))],
        out_specs=_row_spec(chunk, aw), out_shape=jax.ShapeDtypeStruct((s, aw), BF16),
        compiler_params=_params("parallel"),
    )(zpre, ln_g, ln_b, w_s, b_col)


def _gate_bwd(name, zpre, dgated, ln_g, ln_b, w_s, b_col):
    s, aw2 = zpre.shape
    aw = aw2 // 2
    ng, chunk, _ = w_s.shape
    dh = aw // ng

    def body(z_ref, dgt_ref, g_ref, b_ref, ws_ref, bc_ref, dz_ref, dws_ref, dbc_ref, dlg_ref, dlb_ref,
             du_scr, dvln_scr):
        @pl.when(pl.program_id(0) == 0)
        def _():
            dws_ref[...] = jnp.zeros_like(dws_ref)
            dbc_ref[...] = jnp.zeros_like(dbc_ref)
            dlg_ref[...] = jnp.zeros_like(dlg_ref)
            dlb_ref[...] = jnp.zeros_like(dlb_ref)

        zu = z_ref[:, :aw]
        zv = z_ref[:, aw:]
        u = _gelu(zu)
        lg = g_ref[...]
        vhat, rs, vln = _layernorm_parts(_gelu(zv), lg, b_ref[...])
        mask = _causal(chunk)
        for gi in range(ng):
            sl = slice(gi * dh, (gi + 1) * dh)
            wm = jnp.where(mask, ws_ref[gi], 0.0).astype(BF16)
            vg = vln[:, sl].astype(BF16)
            sg = jnp.dot(wm, vg, preferred_element_type=F32) + bc_ref[:, gi:gi + 1]
            dgt = dgt_ref[:, sl]
            du_scr[:, sl] = dgt * sg
            ds = dgt * u[:, sl]
            dbc_ref[:, gi:gi + 1] += jnp.sum(ds, axis=-1, keepdims=True)
            dsb = ds.astype(BF16)
            dwm = lax.dot_general(dsb, vg, (((1,), (1,)), ((), ())), preferred_element_type=F32)
            dws_ref[gi] += jnp.where(mask, dwm, 0.0)
            dvln_scr[:, sl] = lax.dot_general(wm, dsb, (((0,), (0,)), ((), ())), preferred_element_type=F32)
        dvln = dvln_scr[...]
        dlb_ref[...] += jnp.sum(dvln, axis=0, keepdims=True)
        dlg_ref[...] += jnp.sum(dvln * vhat, axis=0, keepdims=True)
        dvh = dvln * lg
        dv = rs * (dvh - jnp.mean(dvh, axis=-1, keepdims=True)
                   - vhat * jnp.mean(dvh * vhat, axis=-1, keepdims=True))
        dz_ref[:, :aw] = (du_scr[...] * _gelu_grad(zu)).astype(BF16)
        dz_ref[:, aw:] = (dv * _gelu_grad(zv)).astype(BF16)

    return pl.pallas_call(
        body, name=name, grid=(s // chunk,),
        in_specs=[_row_spec(chunk, aw2), _row_spec(chunk, aw), _vec_spec(aw), _vec_spec(aw),
                  pl.BlockSpec((ng, chunk, chunk), lambda i: (0, 0, 0)),
                  pl.BlockSpec((chunk, ng), lambda i: (0, 0))],
        out_specs=[_row_spec(chunk, aw2), pl.BlockSpec((ng, chunk, chunk), lambda i: (0, 0, 0)),
                   pl.BlockSpec((chunk, ng), lambda i: (0, 0)), _vec_spec(aw), _vec_spec(aw)],
        out_shape=[jax.ShapeDtypeStruct((s, aw2), BF16), jax.ShapeDtypeStruct((ng, chunk, chunk), F32),
                   jax.ShapeDtypeStruct((chunk, ng), F32), jax.ShapeDtypeStruct((1, aw), F32),
                   jax.ShapeDtypeStruct((1, aw), F32)],
        scratch_shapes=[pltpu.VMEM((chunk, aw), F32), pltpu.VMEM((chunk, aw), F32)],
        compiler_params=_params("arbitrary"),
    )(zpre, dgated, ln_g, ln_b, w_s, b_col)


def _pool_select(g, parts):
    out = parts[-1]
    for gi in range(len(parts) - 2, -1, -1):
        out = jnp.where(g == gi, parts[gi], out)
    return out


def _pool_specs(s, bw):
    head = bw // len(B_WINDOWS)
    tc = _tile(head, 256)
    nb = head // tc
    return tc, (len(B_WINDOWS), nb), pl.BlockSpec((s, tc), lambda g, j: (0, g * nb + j))


def _pool_window(g, t):
    w = _pool_select(g, [jnp.full(t.shape, wi, jnp.int32) for wi in B_WINDOWS])
    return jnp.minimum(t + 1, w).astype(F32)


def _pool_fwd(name, vb):
    assert B_WINDOWS == (2, 4, 8, 16)
    s, bw = vb.shape
    tc, grid, spec = _pool_specs(s, bw)

    def body(v_ref, o_ref):
        g = pl.program_id(0)
        v = v_ref[...]
        t = lax.broadcasted_iota(jnp.int32, (s, tc), 0)

        def down(x, k):
            return jnp.where(t >= k, pltpu.roll(x, k, 0), 0.0)

        sums, cur, k = [], v, 1
        for _ in B_WINDOWS:
            cur = cur + down(cur, k)
            sums.append(cur)
            k *= 2
        o_ref[...] = (_pool_select(g, sums) / _pool_window(g, t) - v).astype(BF16)

    return pl.pallas_call(
        body, name=name, grid=grid, in_specs=[spec], out_specs=spec,
        out_shape=jax.ShapeDtypeStruct((s, bw), BF16), compiler_params=_params("parallel", "parallel"),
    )(vb)


def _pool_bwd(name, dpooled):
    s, bw = dpooled.shape
    tc, grid, spec = _pool_specs(s, bw)

    def body(d_ref, o_ref):
        g = pl.program_id(0)
        dp = d_ref[...]
        t = lax.broadcasted_iota(jnp.int32, (s, tc), 0)

        def up(x, k):
            return jnp.where(t < s - k, pltpu.roll(x, s - k, 0), 0.0)

        sums, cur, k = [], dp / _pool_window(g, t), 1
        for _ in B_WINDOWS:
            cur = cur + up(cur, k)
            sums.append(cur)
            k *= 2
        o_ref[...] = (_pool_select(g, sums) - dp).astype(BF16)

    return pl.pallas_call(
        body, name=name, grid=grid, in_specs=[spec], out_specs=spec,
        out_shape=jax.ShapeDtypeStruct((s, bw), BF16), compiler_params=_params("parallel", "parallel"),
    )(dpooled)


def _scale_bwd(name, dms, mixed, scale):
    s, bw = dms.shape
    tr = _tile(s, ROW_TILE)

    def body(d_ref, m_ref, sc_ref, o_ref, ds_ref):
        @pl.when(pl.program_id(0) == 0)
        def _():
            ds_ref[...] = jnp.zeros_like(ds_ref)

        dv = d_ref[...]
        ds_ref[...] += jnp.sum(dv * m_ref[...], axis=0, keepdims=True)
        o_ref[...] = (dv * sc_ref[...]).astype(BF16)

    return pl.pallas_call(
        body, name=name, grid=(s // tr,), in_specs=[_row_spec(tr, bw), _row_spec(tr, bw), _vec_spec(bw)],
        out_specs=[_row_spec(tr, bw), _vec_spec(bw)],
        out_shape=[jax.ShapeDtypeStruct((s, bw), BF16), jax.ShapeDtypeStruct((1, bw), F32)],
        compiler_params=_params("arbitrary"),
    )(dms, mixed, scale)


def _adam_update(w, g, m, v):
    m = ADAM_B1 * m + (1.0 - ADAM_B1) * g
    v = ADAM_B2 * v + (1.0 - ADAM_B2) * (g * g)
    m_hat = m / (1.0 - ADAM_B1 ** ADAM_STEP)
    v_hat = v / (1.0 - ADAM_B2 ** ADAM_STEP)
    delta = -ADAM_LR * (m_hat / (jnp.sqrt(v_hat) + ADAM_EPS) + ADAM_WD * w)
    return delta, m, v


def _adam_shard(name, w, m, v, parts, layer=0, prev=None, deps=()):
    nl, r, c = w.shape
    nj, nlp = parts.shape[:2]
    tr = _tile(r, ROW_TILE // 2)
    spec = pl.BlockSpec((None, tr, c), lambda l, i: (layer + l, i, 0))

    def body(w_ref, m_ref, v_ref, p_ref, *rest):
        g_ref, d_ref, nm_ref, nv_ref = rest[-4:]
        g = p_ref[0].astype(F32)
        for j in range(1, nj):
            g = g + p_ref[j].astype(F32)
        delta, nm, nv = _adam_update(w_ref[...], g, m_ref[...], v_ref[...])
        g_ref[...] = g
        d_ref[...] = delta
        nm_ref[...] = nm
        nv_ref[...] = nv

    prev = () if prev is None else tuple(prev)
    return pl.pallas_call(
        body, name=name, grid=(nlp, r // tr),
        in_specs=[spec, spec, spec, pl.BlockSpec((nj, None, tr, c), lambda l, i: (0, l, i, 0))]
        + [_ANY] * (len(prev) + len(deps)),
        out_specs=[spec] * 4, out_shape=[jax.ShapeDtypeStruct(w.shape, F32)] * 4,
        input_output_aliases={4 + i: i for i in range(len(prev))},
        compiler_params=_params("parallel", "parallel"),
    )(w, m, v, parts, *prev, *deps)


def _adam_packed(name, w, g, m, v):
    rows, lanes = w.shape
    tr = rows
    spec = pl.BlockSpec((tr, lanes), lambda i: (i, 0))

    def body(w_ref, g_ref, m_ref, v_ref, d_ref, nm_ref, nv_ref):
        delta, nm, nv = _adam_update(w_ref[...], g_ref[...], m_ref[...], v_ref[...])
        d_ref[...] = delta
        nm_ref[...] = nm
        nv_ref[...] = nv

    return pl.pallas_call(
        body, name=name, grid=(rows // tr,), in_specs=[spec] * 4, out_specs=[spec] * 3,
        out_shape=[jax.ShapeDtypeStruct(w.shape, F32)] * 3, compiler_params=_params("parallel"),
    )(w, g, m, v)


def _place():
    x, y, c = lax.axis_index("x"), lax.axis_index("y"), lax.axis_index("c")
    chips = [(1 - x, y), (x, 1 - y), (1 - x, 1 - y)]
    return x, y, c, chips


def _remote(src, dst, send_sem, recv_sem, device):
    return pltpu.make_async_remote_copy(src_ref=src, dst_ref=dst, send_sem=send_sem, recv_sem=recv_sem,
                                        device_id=device, device_id_type=MESH)


def _half(ref_rows, cc):
    h = ref_rows // 2
    return pl.ds(cc * h, h)


_HBM = pl.BlockSpec(memory_space=pltpu.HBM)
_SEM = pl.BlockSpec(memory_space=pltpu.SEMAPHORE)
_EFFECT = pltpu.SideEffectType.DATAFLOW_SIDE_EFFECTING


def _in_hbm(arr):
    return pltpu.with_memory_space_constraint(arr, pltpu.HBM)


def _gather_rows(land, cc, part=(0, 1)):
    k, n = part
    h = land.shape[2] // 2
    return pl.ds(cc * h + k * (h // n), h // n)


def _gather_block(land, split, j, cc, part=(0, 1)):
    if split:
        return land.at[j, :, _gather_rows(land, cc, part), :]
    return land.at[j]


def _split_start(name, srcs, lands, plan, deps=(), groups=None):
    ns = len(srcs)
    nl = len(lands) if groups is None else groups
    both = list(srcs) + list(lands)
    nb = len(both)

    def body(*refs):
        s, ld = refs[:ns], refs[ns:nb]
        outs = refs[nb + len(deps):]
        send, recv, token = outs[:nl], outs[nl:2 * nl], outs[-1]
        for a, copies in enumerate(plan(s, ld)):
            for src, dst, peer in copies:
                _remote(src, dst, send[a], recv[a], peer).start()
        token[...] = jnp.zeros_like(token)

    outs = pl.pallas_call(
        body, name=name, in_specs=[_HBM] * nb + [_ANY] * len(deps),
        out_specs=[_SEM] * (2 * nl) + [_HBM] * nb + [pl.BlockSpec(memory_space=pltpu.VMEM)],
        out_shape=[pltpu.SemaphoreType.DMA(())] * (2 * nl) + [pltpu.HBM(b.shape, b.dtype) for b in both]
        + [jax.ShapeDtypeStruct((8, LANES), F32)],
        input_output_aliases={i: 2 * nl + i for i in range(nb)},
        compiler_params=pltpu.CompilerParams(has_side_effects=_EFFECT),
    )(*[_in_hbm(b) for b in both], *deps)
    thru = outs[2 * nl:2 * nl + nb]
    return list(outs[:nl]), list(outs[nl:2 * nl]), list(thru[:ns]), list(thru[ns:]), outs[-1]


def _split_wait(name, srcs, lands, send, recv, after, whole):
    ns, nl = len(srcs), len(send)
    both = list(srcs) + list(lands)
    nb = len(both)

    def body(*refs):
        ld, snd, rcv = refs[ns:nb], refs[nb:nb + nl], refs[nb + nl:nb + 2 * nl]
        x, y, c, _ = _place()
        for a, blk in enumerate(whole(ld)):
            every = _remote(blk, blk, snd[a], rcv[a], (x, y, c))
            every.wait_send()
            every.wait_recv()

    outs = pl.pallas_call(
        body, name=name, in_specs=[_HBM] * nb + [_SEM] * (2 * nl) + [_ANY], out_specs=[_HBM] * nb,
        out_shape=[pltpu.HBM(b.shape, b.dtype) for b in both],
        input_output_aliases={i: i for i in range(nb)},
        compiler_params=pltpu.CompilerParams(has_side_effects=_EFFECT),
    )(*both, *send, *recv, after)
    return list(outs[:ns]), list(outs[ns:])


def _gather_plan(split, parts):
    def plan(srcs, lands):
        x, y, c, chips = _place()
        out = []
        for ld, sp, n in zip(lands, split, parts):
            for k in range(n):
                blk = _gather_block(ld, sp, 2 * x + y, c, (k, n))
                out.append([(blk, blk, (qx, qy, c)) for qx, qy in chips])
        return out
    return plan


def _gather_whole(split, part=(0, 1)):
    def whole(lands):
        _, _, c, _ = _place()
        return [ld.at[pl.ds(0, 3), :, _gather_rows(ld, c, part), :] if sp else ld.at[pl.ds(0, 3)]
                for ld, sp in zip(lands, split)]
    return whole


def _relay_plan(srcs, lands):
    x, y, c, _ = _place()
    out = []
    for ld in lands:
        blocks = [ld.at[j, :, _half(ld.shape[2], c), :] for j in range(N_CHIPS)]
        out.append([(blk, blk, (x, y, 1 - c)) for blk in blocks])
    return out


def _relay_whole(lands):
    _, _, c, _ = _place()
    return [ld.at[:, :, _half(ld.shape[2], c), :] for ld in lands]


def _fill_own(name, shard, dtype, place, layer=None):
    nl, r, c = shard.shape
    first = 0
    if layer is not None:
        nl, first = 1, layer
    tr = _tile(r, 512)

    def body(p_ref, s_ref, o_ref):
        o_ref[...] = s_ref[...].astype(o_ref.dtype)

    return pl.pallas_call(
        body, name=name,
        grid_spec=pltpu.PrefetchScalarGridSpec(
            num_scalar_prefetch=1, grid=(nl, r // tr),
            in_specs=[pl.BlockSpec((None, tr, c), lambda l, i, p: (first + l, i, 0))],
            out_specs=pl.BlockSpec((None, None, tr, c), lambda l, i, p: (p[0], l, i, 0))),
        out_shape=jax.ShapeDtypeStruct((N_CHIPS, nl, r, c), dtype),
        compiler_params=_params("parallel", "parallel"),
    )(place, shard)


def _gather_finish(name, lands, part=(0, 1)):
    n = len(lands)

    def body(*refs):
        outs = refs[n:2 * n]
        fsend, frecv = refs[2 * n:]
        x, y, c, chips = _place()
        sib = (x, y, 1 - c)

        def relay(a, qi, cc):
            qx, qy = chips[qi]
            blk = _gather_block(outs[a], True, 2 * qx + qy, cc, part)
            return _remote(blk, blk, fsend.at[a, qi], frecv.at[a, qi], sib)

        relays = [relay(a, qi, c) for a in range(n) for qi in range(3)]
        for cp in relays:
            cp.start()
        for a in range(n):
            for qi in range(3):
                relay(a, qi, 1 - c).wait_recv()
        for cp in relays:
            cp.wait_send()

    outs = pl.pallas_call(
        body, name=name, in_specs=[_ANY] * n, out_specs=[_ANY] * n,
        out_shape=[jax.ShapeDtypeStruct(ld.shape, ld.dtype) for ld in lands],
        input_output_aliases={i: i for i in range(n)},
        scratch_shapes=[pltpu.SemaphoreType.DMA((n, 3))] * 2,
    )(*lands)
    return list(outs)


def _pair_plan(srcs, lands):
    x, y, c, _ = _place()
    return [[(s.at[:, :, _half(s.shape[2], 1 - c), :], ld, (x, y, 1 - c))] for s, ld in zip(srcs, lands)]


def _pair_whole(lands):
    return list(lands)


def _scatter_plan(srcs, lands):
    x, y, c, chips = _place()
    return [[(s.at[2 * qx + qy], ld.at[2 * x + y, :, _half(ld.shape[2], c), :], (qx, qy, c)) for qx, qy in chips]
            for s, ld in zip(srcs, lands)]


def _scatter_whole(lands):
    _, _, c, _ = _place()
    return [ld.at[pl.ds(0, 3), :, _half(ld.shape[2], c), :] for ld in lands]


def _pair_sum(name, grad, other, place):
    nj, nl, r, c = grad.shape
    h = r // 2
    tr = _tile(h, 2 * ROW_TILE)
    nb = h // tr

    def body(p_ref, g_ref, o_ref, q_ref, d_ref):
        q = (g_ref[...].astype(F32) + o_ref[...].astype(F32)).astype(BF16)
        q_ref[...] = q

        @pl.when(pl.program_id(2) == p_ref[0])
        def _():
            d_ref[...] = q

    blk = (None, None, tr, c)
    return pl.pallas_call(
        body, name=name,
        grid_spec=pltpu.PrefetchScalarGridSpec(
            num_scalar_prefetch=1, grid=(nl, nb, nj),
            in_specs=[pl.BlockSpec(blk, lambda l, i, j, p: (j, l, p[1] * nb + i, 0)),
                      pl.BlockSpec(blk, lambda l, i, j, p: (j, l, i, 0))],
            out_specs=[pl.BlockSpec(blk, lambda l, i, j, p: (j, l, i, 0)),
                       pl.BlockSpec(blk, lambda l, i, j, p: (p[0], l, p[1] * nb + i, 0))]),
        out_shape=[jax.ShapeDtypeStruct((nj, nl, h, c), BF16), jax.ShapeDtypeStruct((nj, nl, r, c), BF16)],
        compiler_params=_params("parallel", "parallel", "arbitrary"),
    )(place, grad, other)


def _own_of_eight(name, packed, device):
    rows, lanes = packed.shape
    tr = rows

    def body(d_ref, s_ref, o_ref):
        o_ref[...] = s_ref[...]

    return pl.pallas_call(
        body, name=name,
        grid_spec=pltpu.PrefetchScalarGridSpec(
            num_scalar_prefetch=1, grid=(rows // tr,),
            in_specs=[pl.BlockSpec((tr, lanes), lambda i, d: (i, 0))],
            out_specs=pl.BlockSpec((None, tr, lanes), lambda i, d: (d[0], i, 0))),
        out_shape=jax.ShapeDtypeStruct((N_DEV, rows, lanes), packed.dtype),
        compiler_params=_params("parallel"),
    )(device, packed)


def _all_plan(srcs, lands):
    x, y, c, _ = _place()
    (ld,) = lands
    blk = ld.at[4 * x + 2 * y + c]
    flips = [(a, b, d) for a in (0, 1) for b in (0, 1) for d in (0, 1) if a + b + d]
    return [[(blk, blk, (x + a - 2 * a * x, y + b - 2 * b * y, c + d - 2 * d * c)) for a, b, d in flips]]


def _all_whole(lands):
    return [lands[0].at[pl.ds(0, N_DEV - 1)]]


def _sum_of_eight(name, slots):
    n, rows, lanes = slots.shape
    tr = rows

    def body(s_ref, o_ref):
        total = s_ref[0]
        for d in range(1, n):
            total = total + s_ref[d]
        o_ref[...] = total

    return pl.pallas_call(
        body, name=name, grid=(rows // tr,),
        in_specs=[pl.BlockSpec((n, tr, lanes), lambda i: (0, i, 0))],
        out_specs=pl.BlockSpec((tr, lanes), lambda i: (i, 0)),
        out_shape=jax.ShapeDtypeStruct((rows, lanes), F32), compiler_params=_params("parallel"),
    )(slots)


def _pack(parts):
    rows = []
    for p in parts:
        flat = p.reshape(-1)
        pad = (-flat.shape[0]) % PACK_ELEMS
        rows.append(jnp.pad(flat, (0, pad)).reshape(-1, LANES))
    return jnp.concatenate(rows, axis=0)


def _unpack(packed, shapes):
    out, row = [], 0
    for sh in shapes:
        size = math.prod(sh)
        nrows = -(-size // PACK_ELEMS) * (PACK_ELEMS // LANES)
        out.append(packed[row:row + nrows].reshape(-1)[:size].reshape(sh))
        row += nrows
    return out


def kernel(x, a_w_in, a_ln_g, a_ln_b, a_w_s, a_b_s, a_w_out, b_w_in, b_w_grp, b_scale, b_w_out, norm_mix, norm_mlp, mlp_w1, mlp_w2, final_norm, loss_target, m_a_w_in, m_a_ln_g, m_a_ln_b, m_a_w_s, m_a_b_s, m_a_w_out, m_b_w_in, m_b_w_grp, m_b_scale, m_b_w_out, m_norm_mix, m_norm_mlp, m_mlp_w1, m_mlp_w2, m_final_norm, v_a_w_in, v_a_ln_g, v_a_ln_b, v_a_w_s, v_a_b_s, v_a_w_out, v_b_w_in, v_b_w_grp, v_b_scale, v_b_w_out, v_norm_mix, v_norm_mlp, v_mlp_w1, v_mlp_w2, v_final_norm):
    xi, yi, ci = lax.axis_index("x"), lax.axis_index("y"), lax.axis_index("c")
    chip = 2 * xi + yi
    place = jnp.stack([chip, ci]).astype(jnp.int32)
    x2, tgt = x[0], loss_target[0]
    bw = b_scale.shape[1] * N_CHIPS

    units = dict(a_w_in=(a_w_in, None), a_w_out=(a_w_out, None), w1_0=(mlp_w1, 0), w2_0=(mlp_w2, 0),
                 b_scale=(b_scale.reshape(1, 1, -1), None), b_w_in=(b_w_in, None), b_w_grp=(b_w_grp[0], None),
                 b_w_out=(b_w_out, None), w1_1=(mlp_w1, 1), w2_1=(mlp_w2, 1))
    col_sharded = dict(a_w_in=True, a_w_out=False, b_w_in=False, b_w_grp=False, b_w_out=False,
                       w1_0=True, w2_0=False, w1_1=True, w2_1=False)
    in_flight, W = {}, {}

    def launch(tag, keys, deps):
        sp = [k != "b_scale" for k in keys]
        parts = [pieces.get(k, 1) for k in keys]
        zones = [_fill_own(f"gather_own_{k}", units[k][0], BF16 if s else F32, place, layer=units[k][1])
                 for k, s in zip(keys, sp)]
        send, recv, _, zones, tok = _split_start(f"gather_start_{tag}", [], zones, _gather_plan(sp, parts), deps,
                                                 groups=sum(parts))
        first = 0
        for k, z, s, n in zip(keys, zones, sp, parts):
            in_flight[k] = (send[first:first + n], recv[first:first + n], z, s)
            first += n
        return tok

    def arrive(keys, after):
        send, recv, zones, sp = zip(*[in_flight[k] for k in keys])
        n = len(send[0])
        if n > 1:
            (key,), zones = keys, list(zones)
            for k in range(n):
                _, zones = _split_wait(f"gather_wait_{key}_{k}", [], zones, [send[0][k]], [recv[0][k]], after,
                                       _gather_whole(sp, (k, n)))
                zones = _gather_finish(f"gather_finish_{key}_{k}", zones, (k, n))
            W[key] = _W4(zones[0], col_sharded[key])
            return
        _, zones = _split_wait(f"gather_wait_{keys[0]}", [], zones, [s[0] for s in send], [r[0] for r in recv],
                               after, _gather_whole(sp))
        relayed = iter(_gather_finish(f"gather_finish_{keys[0]}", [z for z, s in zip(zones, sp) if s]))
        for k, z, s in zip(keys, zones, sp):
            full = next(relayed) if s else z
            W[k] = _W4(full, col_sharded[k]) if k in col_sharded else full

    pieces = dict(w1_0=2, w2_0=2, w1_1=2, w2_1=2)

    token = launch("first", ["a_w_in", "a_w_out"], ())
    token = launch("rest", ["w1_0", "w2_0", "b_scale", "b_w_in", "b_w_grp", "b_w_out", "w1_1", "w2_1"], (token,))

    b_col = a_b_s[0].T

    def residual(acc, res):
        return (res + acc,)

    def sq_relu(acc):
        act = jnp.maximum(acc, 0.0)
        return act, act * act

    def mlp_fwd(tag, h, layer):
        hn = _rms_fwd(f"mlp{tag}_norm", h, norm_mlp[layer:layer + 1])
        arrive([f"w1_{layer}"], hn)
        act, act_sq = _mm_aw(f"mlp{tag}_up", hn, W[f"w1_{layer}"], out_dtypes=(BF16, BF16), epilogue=sq_relu)
        arrive([f"w2_{layer}"], act_sq)
        out = _mm_aw(f"mlp{tag}_down", act_sq, W[f"w2_{layer}"], extras=(h,), epilogue=residual)
        return out, (h, hn, act, act_sq)

    hn0 = _rms_fwd("mix_a_norm", x2, norm_mix[0:1])
    arrive(["a_w_in"], token)
    zpre = _mm_aw("mix_a_in", hn0, W["a_w_in"])
    gated = _gate_fwd("mix_a_gate", zpre, a_ln_g, a_ln_b, a_w_s[0], b_col)
    arrive(["a_w_out"], gated)
    h1 = _mm_aw("mix_a_out", gated, W["a_w_out"], extras=(x2,), epilogue=residual)
    h2, mlp0 = mlp_fwd("0", h1, 0)
    hn2 = _rms_fwd("mix_b_norm", h2, norm_mix[1:2])
    arrive(["b_scale", "b_w_in"], hn2)
    scale_full = W["b_scale"].reshape(1, bw)
    vb = _mm_aw("mix_b_in", hn2, W["b_w_in"])
    pooled = _pool_fwd("mix_b_pool", vb)
    arrive(["b_w_grp", "b_w_out"], pooled)
    mixed, ms = _mm_aw("mix_b_grp", pooled, W["b_w_grp"], groups=len(B_WINDOWS), extras=(scale_full,),
                       out_dtypes=(F32, BF16), epilogue=lambda acc, sc: (acc, acc * sc))
    h3 = _mm_aw("mix_b_out", ms, W["b_w_out"], extras=(h2,), epilogue=residual)
    h4, mlp1 = mlp_fwd("1", h3, 1)
    dh4, dh4_b, d_final, loss_part = _final("loss_head", h4, final_norm.reshape(1, -1), tgt)
    g1_like = _W4(None, True, shape=(N_CHIPS, 1, *W["w1_0"].arr.shape[2:]))
    g2_like = _W4(None, False, shape=(N_CHIPS, 1, *W["w2_0"].arr.shape[2:]))

    def exchange(tag, gs):
        zones = [lax.empty((g.shape[0], g.shape[1], g.shape[2] // 2, g.shape[3]), g.dtype) for g in gs]
        send, recv, srcs, zones, tok = _split_start(f"pair_start_{tag}", gs, zones, _pair_plan)
        return (tag, send, recv, srcs, zones), tok

    def reduce(state, after):
        tag, send, recv, srcs, zones = state
        srcs, zones = _split_wait(f"pair_wait_{tag}", srcs, zones, send, recv, after, _pair_whole)
        both = [_pair_sum(f"pair_sum_{tag}_{i}", g, o, place) for i, (g, o) in enumerate(zip(srcs, zones))]
        send, recv, sums, dests, tok = _split_start(f"scatter_start_{tag}", [b[0] for b in both],
                                                    [b[1] for b in both], _scatter_plan)
        return (tag, send, recv, sums, dests), tok

    def relay(state, after):
        tag, send, recv, sums, dests = state
        _, dests = _split_wait(f"scatter_wait_{tag}", sums, dests, send, recv, after, _scatter_whole)
        send, recv, _, dests, tok = _split_start(f"relay_start_{tag}", [], dests, _relay_plan)
        return (tag, send, recv, dests), tok

    def land(state, after):
        tag, send, recv, dests = state
        return _split_wait(f"relay_wait_{tag}", [], dests, send, recv, after, _relay_whole)[1]

    def mlp_bwd(tag, dh, dh_b, saved, layer, deps, pending=None):
        h, hn, act, act_sq = saved
        dpre = _mm_aw(f"mlp{tag}_down_dx", dh_b, W[f"w2_{layer}"], transpose_w=True, extras=(act,),
                      out_dtypes=(BF16,), epilogue=lambda acc, a: (acc * (2.0 * a),), deps=deps)
        scattering, dw_deps = None, ()
        if pending is not None:
            scattering, tok = reduce(pending, dpre)
            dw_deps = (tok,)
        g_w2 = _mm_dw(f"mlp{tag}_down_dw", act_sq, dh_b, g2_like, deps=dw_deps)
        pair_w2, tok = exchange(f"w2_{layer}", [g_w2])
        dhn = _mm_aw(f"mlp{tag}_up_dx", dpre, W[f"w1_{layer}"], transpose_w=True, deps=(tok,))
        g_w1 = _mm_dw(f"mlp{tag}_up_dw", hn, dpre, g1_like)
        pair_w1, tok1 = exchange(f"w1_{layer}", [g_w1])
        scat_w2, tok2 = reduce(pair_w2, dhn)
        dh_in, dh_in_b, d_norm = _rms_bwd(f"mlp{tag}_norm_bwd", h, norm_mlp[layer:layer + 1], dhn, dh)
        return dh_in, dh_in_b, d_norm, pair_w1, scat_w2, (tok1, tok2), scattering

    dh3, dh3_b, d_norm_mlp1, pair_w1_1, scat_w2_1, toks, _ = mlp_bwd("1", dh4, dh4_b, mlp1, 1, ())
    dms = _mm_aw("mix_b_out_dx", dh3_b, W["b_w_out"], transpose_w=True, deps=toks)
    g_b_out = _mm_dw("mix_b_out_dw", ms, dh3_b, W["b_w_out"])
    scat_w1_1, tok = reduce(pair_w1_1, dms)
    dmixed, d_scale = _scale_bwd("mix_b_scale_bwd", dms, mixed, scale_full)
    dpooled = _mm_aw("mix_b_grp_dx", dmixed, W["b_w_grp"], groups=len(B_WINDOWS), transpose_w=True, deps=(tok,))
    g_b_grp = _mm_dw("mix_b_grp_dw", pooled, dmixed, W["b_w_grp"], groups=len(B_WINDOWS))
    dvb = _pool_bwd("mix_b_pool_bwd", dpooled)
    dhn2 = _mm_aw("mix_b_in_dx", dvb, W["b_w_in"], transpose_w=True)
    g_b_in = _mm_dw("mix_b_in_dw", hn2, dvb, W["b_w_in"])
    pair_b, tok = exchange("b", [g_b_out, g_b_grp, g_b_in])
    dh2, dh2_b, d_norm_mix1 = _rms_bwd("mix_b_norm_bwd", h2, norm_mix[1:2], dhn2, dh3)
    dh1, dh1_b, d_norm_mlp0, pair_w1_0, scat_w2_0, toks, scat_b = mlp_bwd("0", dh2, dh2_b, mlp0, 0, (tok,),
                                                                          pending=pair_b)
    dgated = _mm_aw("mix_a_out_dx", dh1_b, W["a_w_out"], transpose_w=True, deps=toks)
    g_a_out = _mm_dw("mix_a_out_dw", gated, dh1_b, W["a_w_out"])
    pair_a_out, tok_a = exchange("a_out", [g_a_out])
    scat_w1_0, tok = reduce(pair_w1_0, dgated)
    early = [relay(state, dgated) for state in (scat_w2_1, scat_w1_1, scat_b)]
    dzpre, d_w_s, d_b_col, d_ln_g, d_ln_b = _gate_bwd("mix_a_gate_bwd", zpre, dgated, a_ln_g, a_ln_b, a_w_s[0], b_col)
    dhn0 = _mm_aw("mix_a_in_dx", dzpre, W["a_w_in"], transpose_w=True, deps=(tok, tok_a, *[t for _, t in early]))
    scat_a_out, tok = reduce(pair_a_out, dhn0)
    g_a_in = _mm_dw("mix_a_in_dw", hn0, dzpre, W["a_w_in"], deps=(tok,))
    pair_a_in, tok = exchange("a_in", [g_a_in])
    dx, _, d_norm_mix0 = _rms_bwd("mix_a_norm_bwd", x2, norm_mix[0:1], dhn0, dh1)
    scat_a_in, _ = reduce(pair_a_in, dx)

    small = dict(a_ln_g=(a_ln_g, m_a_ln_g, v_a_ln_g), a_ln_b=(a_ln_b, m_a_ln_b, v_a_ln_b),
                 a_w_s=(a_w_s, m_a_w_s, v_a_w_s), a_b_s=(a_b_s, m_a_b_s, v_a_b_s),
                 b_scale=(b_scale, m_b_scale, v_b_scale), norm_mix=(norm_mix, m_norm_mix, v_norm_mix),
                 norm_mlp=(norm_mlp, m_norm_mlp, v_norm_mlp), final_norm=(final_norm, m_final_norm, v_final_norm))
    small_names = list(small)
    local = dict(a_ln_g=d_ln_g, a_ln_b=d_ln_b, a_w_s=d_w_s[None], a_b_s=d_b_col.T[None], b_scale=d_scale,
                 norm_mix=jnp.concatenate([d_norm_mix0, d_norm_mix1], axis=0),
                 norm_mlp=jnp.concatenate([d_norm_mlp0, d_norm_mlp1], axis=0), final_norm=d_final.reshape(-1))
    device = (4 * xi + 2 * yi + ci).astype(jnp.int32).reshape(1)
    slots = _own_of_eight("small_own", _pack([local[k] for k in small_names]), device)
    small_send, small_recv, _, (slots,), small_tok = _split_start("small_start", [], [slots], _all_plan)

    moments = dict(a_w_in=(m_a_w_in, v_a_w_in), a_w_out=(m_a_w_out, v_a_w_out), b_w_in=(m_b_w_in, v_b_w_in),
                   b_w_grp=(m_b_w_grp, v_b_w_grp), b_w_out=(m_b_w_out, v_b_w_out),
                   mlp_w1=(m_mlp_w1, v_mlp_w1), mlp_w2=(m_mlp_w2, v_mlp_w2))
    weights = dict(a_w_in=a_w_in, a_w_out=a_w_out, b_w_in=b_w_in, b_w_grp=b_w_grp, b_w_out=b_w_out,
                   mlp_w1=mlp_w1, mlp_w2=mlp_w2)
    landing = [(scat_w2_1, [("mlp_w2", 1)]), (scat_w1_1, [("mlp_w1", 1)]),
               (scat_b, [("b_w_out", 0), ("b_w_grp", 0), ("b_w_in", 0)]),
               (scat_w2_0, [("mlp_w2", 0)]), (scat_w1_0, [("mlp_w1", 0)]),
               (scat_a_out, [("a_w_out", 0)]), (scat_a_in, [("a_w_in", 0)])]
    results, after = {}, dx
    relays = list(early)
    for i, (_, members) in enumerate(landing):
        deps = (small_tok,) if i == 0 else ()
        if len(relays) == i + 1 < len(landing):
            relays.append(relay(landing[i + 1][0], after))
            deps = (relays[-1][1],)
        for (k, layer), parts in zip(members, land(relays[i][0], relays[-1][1] if deps else after)):
            shard_shape = (-1, *parts.shape[2:])
            results[k] = _adam_shard(f"adam_{k}_{layer}", weights[k].reshape(shard_shape),
                                     moments[k][0].reshape(shard_shape), moments[k][1].reshape(shard_shape),
                                     parts, layer=layer, prev=results.get(k), deps=deps)
            after = results[k][1]
    grad_out, delta_out, m_out, v_out = {}, {}, {}, {}
    for k, res in results.items():
        grad_out[k], delta_out[k], m_out[k], v_out[k] = [r.reshape(weights[k].shape) for r in res]

    _, (slots,) = _split_wait("small_wait", [], [slots], small_send, small_recv, after, _all_whole)
    reduced = _sum_of_eight("small_sum", slots)
    small_grads = dict(zip(small_names, _unpack(reduced, [local[k].shape for k in small_names])))
    shard_w = b_scale.shape[1]
    small_grads["b_scale"] = lax.dynamic_slice_in_dim(small_grads["b_scale"], chip * shard_w, shard_w, axis=1)
    small_grads = {k: small_grads[k].reshape(small[k][0].shape) for k in small_names}
    packed = [_pack([small[k][i] for k in small_names]) for i in range(3)]
    res = _adam_packed("adam_small", packed[0], _pack([small_grads[k] for k in small_names]), packed[1], packed[2])
    shapes = [small[k][0].shape for k in small_names]
    for k, d, nm, nv in zip(small_names, *[_unpack(r, shapes) for r in res]):
        grad_out[k], delta_out[k], m_out[k], v_out[k] = small_grads[k], d, nm, nv

    loss = lax.psum(loss_part[0, 0], ("x", "y", "c"))
    order = ["a_w_in", "a_ln_g", "a_ln_b", "a_w_s", "a_b_s", "a_w_out", "b_w_in", "b_w_grp", "b_scale", "b_w_out",
             "norm_mix", "norm_mlp", "mlp_w1", "mlp_w2", "final_norm"]
    return (loss, dx[None], *[grad_out[k] for k in order], *[delta_out[k] for k in order],
            *[m_out[k] for k in order], *[v_out[k] for k in order])
```

```python
import math

import jax
import jax.numpy as jnp
from jax import lax
from jax.experimental import pallas as pl
from jax.experimental.pallas import tpu as pltpu

F32 = jnp.float32
BF16 = jnp.bfloat16
MESH = pl.DeviceIdType.MESH

EPS = 1e-6
B_WINDOWS = (2, 4, 8, 16)
ADAM_LR = 0.001
ADAM_B1 = 0.9
ADAM_B2 = 0.999
ADAM_EPS = 1e-08
ADAM_WD = 0.01
ADAM_STEP = 10

N_CHIPS = 4
N_DEV = 8
LANES = 128
PACK_ELEMS = 8 * LANES
VMEM_LIMIT = 56 * 1024 * 1024
ROW_TILE = 512
MM_TM, MM_TN, MM_TK = 1024, 1024, 2048


_ANY = pl.BlockSpec(memory_space=pl.ANY)


def _tile(dim, pref):
    t = min(dim, pref)
    while dim % t:
        t //= 2
    return t


def _params(*sem):
    return pltpu.CompilerParams(dimension_semantics=sem, vmem_limit_bytes=VMEM_LIMIT)


class _W4:
    def __init__(self, arr, col_sharded, shape=None):
        self.arr = arr
        self.nj, self.nl, self.r, self.c = arr.shape if shape is None else shape
        self.col = col_sharded
        self.rows = self.r if col_sharded else self.nj * self.r
        self.cols = self.nj * self.c if col_sharded else self.c

    def tile_rows(self, pref):
        return _tile(self.r, pref)

    def tile_cols(self, pref):
        return _tile(self.c, pref)

    def index(self, layer, rb, cb, tr, tc):
        if self.col:
            n = self.c // tc
            return (cb // n, layer, rb, cb % n)
        n = self.r // tr
        return (rb // n, layer, rb % n, cb)


def _mm_aw(name, a, w, *, layer=0, groups=1, transpose_w=False, extras=(), out_dtypes=(F32,), epilogue=None,
           deps=()):
    s, ka_total = a.shape
    kdim, ndim = (w.cols, w.rows) if transpose_w else (w.rows, w.cols)
    assert ka_total == groups * kdim, (name, a.shape, kdim, groups)
    span = groups == 1 and not transpose_w and not w.col and kdim <= MM_TK
    if groups > 1:
        tm = _tile(s, MM_TM)
        tn, tk = (w.tile_rows(MM_TN), w.tile_cols(512)) if transpose_w else (w.tile_cols(MM_TN), w.tile_rows(512))
    else:
        tk = kdim if span else (w.tile_cols(MM_TK) if transpose_w else w.tile_rows(MM_TK))
        tm, tn_pref = (_tile(s, 2048), 512) if tk == kdim else (_tile(s, MM_TM), MM_TN)
        tn = w.tile_rows(tn_pref) if transpose_w else w.tile_cols(tn_pref)
    nk, nn = kdim // tk, ndim // tn

    def lay(g):
        return g if groups > 1 else layer

    a_spec = pl.BlockSpec((tm, tk), lambda g, i, n, k: (i, g * nk + k))
    if span:
        w_spec = pl.BlockSpec((w.nj, None, w.r, tn), lambda g, i, n, k: (0, layer, 0, n))
    elif transpose_w:
        w_spec = pl.BlockSpec((None, None, tn, tk), lambda g, i, n, k: w.index(lay(g), n, k, tn, tk))
    else:
        w_spec = pl.BlockSpec((None, None, tk, tn), lambda g, i, n, k: w.index(lay(g), k, n, tk, tn))
    ex_specs = []
    for e in extras:
        assert e.shape[1] == groups * ndim and e.shape[0] in (1, s), (name, e.shape)
        if e.shape[0] == 1:
            ex_specs.append(pl.BlockSpec((1, tn), lambda g, i, n, k: (0, g * nn + n)))
        else:
            ex_specs.append(pl.BlockSpec((tm, tn), lambda g, i, n, k: (i, g * nn + n)))
    out_spec = pl.BlockSpec((tm, tn), lambda g, i, n, k: (i, g * nn + n))
    n_ex, n_out, n_dep = len(extras), len(out_dtypes), len(deps)

    def body(a_ref, w_ref, *rest):
        ex, outs = rest[:n_ex], rest[n_ex + n_dep:n_ex + n_dep + n_out]
        av = a_ref[...]
        if av.dtype != BF16:
            av = av.astype(BF16)
        wv = w_ref[...].reshape(tk, tn) if span else w_ref[...]
        if transpose_w:
            prod = lax.dot_general(av, wv, (((1,), (1,)), ((), ())), preferred_element_type=F32)
        else:
            prod = jnp.dot(av, wv, preferred_element_type=F32)

        def finish(total):
            vals = (total,) if epilogue is None else epilogue(total, *[e[...] for e in ex])
            for o, v in zip(outs, vals):
                o[...] = v.astype(o.dtype)

        if nk == 1:
            finish(prod)
            return
        acc, k = rest[-1], pl.program_id(3)

        @pl.when(k == 0)
        def _():
            acc[...] = prod

        @pl.when(k > 0)
        def _():
            acc[...] += prod

        @pl.when(k == nk - 1)
        def _():
            finish(acc[...])

    outs = pl.pallas_call(
        body, name=name, grid=(groups, s // tm, nn, nk),
        in_specs=[a_spec, w_spec, *ex_specs] + [_ANY] * n_dep, out_specs=[out_spec] * n_out,
        out_shape=[jax.ShapeDtypeStruct((s, groups * ndim), dt) for dt in out_dtypes],
        scratch_shapes=[pltpu.VMEM((tm, tn), F32)] if nk > 1 else [],
        compiler_params=_params("parallel", "parallel", "parallel", "arbitrary"),
    )(a, w.arr, *extras, *deps)
    return outs[0] if n_out == 1 else outs


class _AdamRider:
    def __init__(self, w, m, v, parts, layer, prev=None):
        self.operands = [w, m, v, parts, *(prev or ())]
        self.layer, self.n_prev = layer, len(prev or ())
        self.out_shape = [jax.ShapeDtypeStruct(w.shape, F32)] * 4
        self.aliases = {4 + i: i for i in range(self.n_prev)}

    def specs(self, steps, step_of):
        w, parts = self.operands[0], self.operands[3]
        _, r, c = w.shape
        tr = r // steps
        assert tr * steps == r and tr % 8 == 0 and parts.shape[1] == 1, (w.shape, parts.shape, steps)
        spec = pl.BlockSpec((None, tr, c), lambda *ids: (self.layer, step_of(*ids), 0))
        pspec = pl.BlockSpec((parts.shape[0], None, tr, c), lambda *ids: (0, 0, step_of(*ids), 0))
        return [spec, spec, spec, pspec] + [_ANY] * self.n_prev, [spec] * 4

    def body(self, ins, outs):
        w_ref, m_ref, v_ref, p_ref = ins[:4]
        g = p_ref[0].astype(F32)
        for j in range(1, p_ref.shape[0]):
            g = g + p_ref[j].astype(F32)
        delta, nm, nv = _adam_update(w_ref[...], g, m_ref[...], v_ref[...])
        for o, val in zip(outs, (g, delta, nm, nv)):
            o[...] = val


def _mm_dw(name, a, b, like, *, layer=0, groups=1, deps=(), rider=None):
    s, ka_total = a.shape
    rows, cols = ka_total // groups, b.shape[1] // groups
    assert (rows, cols) == (like.rows, like.cols) and b.shape[0] == s, (name, a.shape, b.shape)
    tm, tn, tk = like.tile_rows(MM_TM), like.tile_cols(MM_TN), _tile(s, MM_TK)
    nr, nc, nk = rows // tm, cols // tn, s // tk

    def lay(g):
        return g if groups > 1 else layer

    def step_of(g, n, i, k):
        return ((g * nc + n) * nr + i) * nk + k

    in_specs = [pl.BlockSpec((tk, tm), lambda g, n, i, k: (k, g * nr + i)),
                pl.BlockSpec((tk, tn), lambda g, n, i, k: (k, g * nc + n))]
    in_specs += [_ANY] * len(deps)
    out_specs = [pl.BlockSpec((None, None, tm, tn), lambda g, n, i, k: like.index(lay(g), i, n, tm, tn))]
    out_shape = [jax.ShapeDtypeStruct((like.nj, like.nl, like.r, like.c), BF16)]
    operands, aliases, n_ri, n_ro = [a, b, *deps], {}, 0, 0
    if rider is not None:
        r_in, r_out = rider.specs(groups * nc * nr * nk, step_of)
        n_ri, n_ro = len(r_in), len(r_out)
        aliases = {len(operands) + i: 1 + o for i, o in rider.aliases.items()}
        in_specs, out_specs, out_shape = in_specs + r_in, out_specs + r_out, out_shape + rider.out_shape
        operands += rider.operands
    n_dep = len(deps)

    def body(a_ref, b_ref, *rest):
        r_ins, o_ref = rest[n_dep:n_dep + n_ri], rest[n_dep + n_ri]
        r_outs = rest[n_dep + n_ri + 1:n_dep + n_ri + 1 + n_ro]
        if rider is not None:
            rider.body(r_ins, r_outs)
        av, bv = a_ref[...], b_ref[...]
        if av.dtype != BF16:
            av = av.astype(BF16)
        if bv.dtype != BF16:
            bv = bv.astype(BF16)
        prod = lax.dot_general(av, bv, (((0,), (0,)), ((), ())), preferred_element_type=F32)
        if nk == 1:
            o_ref[...] = prod.astype(BF16)
            return
        acc, k = rest[-1], pl.program_id(3)

        @pl.when(k == 0)
        def _():
            acc[...] = prod

        @pl.when(k > 0)
        def _():
            acc[...] += prod

        @pl.when(k == nk - 1)
        def _():
            o_ref[...] = acc[...].astype(BF16)

    outs = pl.pallas_call(
        body, name=name, grid=(groups, nc, nr, nk), in_specs=in_specs, out_specs=out_specs, out_shape=out_shape,
        scratch_shapes=[pltpu.VMEM((tm, tn), F32)] if nk > 1 else [], input_output_aliases=aliases,
        compiler_params=_params("parallel", "parallel", "parallel", "arbitrary"),
    )(*operands)
    return outs[0] if rider is None else (outs[0], list(outs[1:]))


def _row_spec(tr, d):
    return pl.BlockSpec((tr, d), lambda i: (i, 0))


def _vec_spec(d):
    return pl.BlockSpec((1, d), lambda i: (0, 0))


def _rms_fwd(name, x, g):
    s, d = x.shape
    tr = _tile(s, ROW_TILE)

    def body(x_ref, g_ref, o_ref):
        xv = x_ref[...]
        r = lax.rsqrt(jnp.mean(xv * xv, axis=-1, keepdims=True) + EPS)
        o_ref[...] = (xv * r * g_ref[...]).astype(BF16)

    return pl.pallas_call(
        body, name=name, grid=(s // tr,), in_specs=[_row_spec(tr, d), _vec_spec(d)],
        out_specs=_row_spec(tr, d), out_shape=jax.ShapeDtypeStruct((s, d), BF16),
        compiler_params=_params("parallel"),
    )(x, g)


def _rms_bwd(name, x, g, dhn, dres):
    s, d = x.shape
    tr = _tile(s, ROW_TILE)

    def body(x_ref, g_ref, dhn_ref, dres_ref, dx_ref, dxb_ref, dg_ref):
        @pl.when(pl.program_id(0) == 0)
        def _():
            dg_ref[...] = jnp.zeros_like(dg_ref)

        xv = x_ref[...]
        r = lax.rsqrt(jnp.mean(xv * xv, axis=-1, keepdims=True) + EPS)
        xh = xv * r
        dy = dhn_ref[...]
        dg_ref[...] += jnp.sum(dy * xh, axis=0, keepdims=True)
        dxh = dy * g_ref[...]
        dx = dres_ref[...] + r * (dxh - xh * jnp.mean(dxh * xh, axis=-1, keepdims=True))
        dx_ref[...] = dx
        dxb_ref[...] = dx.astype(BF16)

    return pl.pallas_call(
        body, name=name, grid=(s // tr,),
        in_specs=[_row_spec(tr, d), _vec_spec(d), _row_spec(tr, d), _row_spec(tr, d)],
        out_specs=[_row_spec(tr, d), _row_spec(tr, d), _vec_spec(d)],
        out_shape=[jax.ShapeDtypeStruct((s, d), F32), jax.ShapeDtypeStruct((s, d), BF16),
                   jax.ShapeDtypeStruct((1, d), F32)],
        compiler_params=_params("arbitrary"),
    )(x, g, dhn, dres)


def _final(name, h, g, tgt):
    s, d = h.shape
    tr = _tile(s, ROW_TILE)

    def body(h_ref, g_ref, t_ref, dh_ref, dhb_ref, dg_ref, loss_ref):
        @pl.when(pl.program_id(0) == 0)
        def _():
            dg_ref[...] = jnp.zeros_like(dg_ref)
            loss_ref[...] = jnp.zeros_like(loss_ref)

        hv = h_ref[...]
        r = lax.rsqrt(jnp.mean(hv * hv, axis=-1, keepdims=True) + EPS)
        xh = hv * r
        gv = g_ref[...]
        err = xh * gv - t_ref[...]
        part = 0.5 * jnp.sum(jnp.mean(err * err, axis=-1, keepdims=True), axis=0, keepdims=True)
        loss_ref[...] += jnp.broadcast_to(part, loss_ref.shape)
        dy = err * (1.0 / d)
        dg_ref[...] += jnp.sum(dy * xh, axis=0, keepdims=True)
        dxh = dy * gv
        dh = r * (dxh - xh * jnp.mean(dxh * xh, axis=-1, keepdims=True))
        dh_ref[...] = dh
        dhb_ref[...] = dh.astype(BF16)

    return pl.pallas_call(
        body, name=name, grid=(s // tr,),
        in_specs=[_row_spec(tr, d), _vec_spec(d), _row_spec(tr, d)],
        out_specs=[_row_spec(tr, d), _row_spec(tr, d), _vec_spec(d), _vec_spec(LANES)],
        out_shape=[jax.ShapeDtypeStruct((s, d), F32), jax.ShapeDtypeStruct((s, d), BF16),
                   jax.ShapeDtypeStruct((1, d), F32), jax.ShapeDtypeStruct((1, LANES), F32)],
        compiler_params=_params("arbitrary"),
    )(h, g, tgt)


_SQRT_HALF = 1.0 / math.sqrt(2.0)
_INV_SQRT_2PI = 1.0 / math.sqrt(2.0 * math.pi)


def _gelu(x):
    return x * (lax.erf(x * _SQRT_HALF) + 1.0) * 0.5


def _gelu_grad(x):
    return 0.5 * (lax.erf(x * _SQRT_HALF) + 1.0) + x * jnp.exp(-0.5 * x * x) * _INV_SQRT_2PI


def _causal(chunk):
    t = lax.broadcasted_iota(jnp.int32, (chunk, chunk), 0)
    sidx = lax.broadcasted_iota(jnp.int32, (chunk, chunk), 1)
    return sidx <= t


def _layernorm_parts(v, g, b):
    mu = jnp.mean(v, axis=-1, keepdims=True)
    vc = v - mu
    rs = lax.rsqrt(jnp.mean(vc * vc, axis=-1, keepdims=True) + EPS)
    vhat = vc * rs
    return vhat, rs, vhat * g + b


def _gate_fwd(name, zpre, ln_g, ln_b, w_s, b_col):
    s, aw2 = zpre.shape
    aw = aw2 // 2
    ng, chunk, _ = w_s.shape
    dh = aw // ng

    def body(z_ref, g_ref, b_ref, ws_ref, bc_ref, o_ref):
        u = _gelu(z_ref[:, :aw])
        v = _gelu(z_ref[:, aw:])
        _, _, vln = _layernorm_parts(v, g_ref[...], b_ref[...])
        mask = _causal(chunk)
        for gi in range(ng):
            sl = slice(gi * dh, (gi + 1) * dh)
            wm = jnp.where(mask, ws_ref[gi], 0.0).astype(BF16)
            sg = jnp.dot(wm, vln[:, sl].astype(BF16), preferred_element_type=F32) + bc_ref[:, gi:gi + 1]
            o_ref[:, sl] = (u[:, sl] * sg).astype(BF16)

    return pl.pallas_call(
        body, name=name, grid=(s // chunk,),
        in_specs=[_row_spec(chunk, aw2), _vec_spec(aw), _vec_spec(aw),
                  pl.BlockSpec((ng, chunk, chunk), lambda i: (0, 0, 0)),
                  pl.BlockSpec((chunk, ng), lambda i: (0, 0))],
        out_specs=_row_spec(chunk, aw), out_shape=jax.ShapeDtypeStruct((s, aw), BF16),
        compiler_params=_params("parallel"),
    )(zpre, ln_g, ln_b, w_s, b_col)


def _gate_bwd(name, zpre, dgated, ln_g, ln_b, w_s, b_col):
    s, aw2 = zpre.shape
    aw = aw2 // 2
    ng, chunk, _ = w_s.shape
    dh = aw // ng

    def body(z_ref, dgt_ref, g_ref, b_ref, ws_ref, bc_ref, dz_ref, dws_ref, dbc_ref, dlg_ref, dlb_ref,
             du_scr, dvln_scr):
        @pl.when(pl.program_id(0) == 0)
        def _():
            dws_ref[...] = jnp.zeros_like(dws_ref)
            dbc_ref[...] = jnp.zeros_like(dbc_ref)
            dlg_ref[...] = jnp.zeros_like(dlg_ref)
            dlb_ref[...] = jnp.zeros_like(dlb_ref)

        zu = z_ref[:, :aw]
        zv = z_ref[:, aw:]
        u = _gelu(zu)
        lg = g_ref[...]
        vhat, rs, vln = _layernorm_parts(_gelu(zv), lg, b_ref[...])
        mask = _causal(chunk)
        for gi in range(ng):
            sl = slice(gi * dh, (gi + 1) * dh)
            wm = jnp.where(mask, ws_ref[gi], 0.0).astype(BF16)
            vg = vln[:, sl].astype(BF16)
            sg = jnp.dot(wm, vg, preferred_element_type=F32) + bc_ref[:, gi:gi + 1]
            dgt = dgt_ref[:, sl]
            du_scr[:, sl] = dgt * sg
            ds = dgt * u[:, sl]
            dbc_ref[:, gi:gi + 1] += jnp.sum(ds, axis=-1, keepdims=True)
            dsb = ds.astype(BF16)
            dwm = lax.dot_general(dsb, vg, (((1,), (1,)), ((), ())), preferred_element_type=F32)
            dws_ref[gi] += jnp.where(mask, dwm, 0.0)
            dvln_scr[:, sl] = lax.dot_general(wm, dsb, (((0,), (0,)), ((), ())), preferred_element_type=F32)
        dvln = dvln_scr[...]
        dlb_ref[...] += jnp.sum(dvln, axis=0, keepdims=True)
        dlg_ref[...] += jnp.sum(dvln * vhat, axis=0, keepdims=True)
        dvh = dvln * lg
        dv = rs * (dvh - jnp.mean(dvh, axis=-1, keepdims=True)
                   - vhat * jnp.mean(dvh * vhat, axis=-1, keepdims=True))
        dz_ref[:, :aw] = (du_scr[...] * _gelu_grad(zu)).astype(BF16)
        dz_ref[:, aw:] = (dv * _gelu_grad(zv)).astype(BF16)

    return pl.pallas_call(
        body, name=name, grid=(s // chunk,),
        in_specs=[_row_spec(chunk, aw2), _row_spec(chunk, aw), _vec_spec(aw), _vec_spec(aw),
                  pl.BlockSpec((ng, chunk, chunk), lambda i: (0, 0, 0)),
                  pl.BlockSpec((chunk, ng), lambda i: (0, 0))],
        out_specs=[_row_spec(chunk, aw2), pl.BlockSpec((ng, chunk, chunk), lambda i: (0, 0, 0)),
                   pl.BlockSpec((chunk, ng), lambda i: (0, 0)), _vec_spec(aw), _vec_spec(aw)],
        out_shape=[jax.ShapeDtypeStruct((s, aw2), BF16), jax.ShapeDtypeStruct((ng, chunk, chunk), F32),
                   jax.ShapeDtypeStruct((chunk, ng), F32), jax.ShapeDtypeStruct((1, aw), F32),
                   jax.ShapeDtypeStruct((1, aw), F32)],
        scratch_shapes=[pltpu.VMEM((chunk, aw), F32), pltpu.VMEM((chunk, aw), F32)],
        compiler_params=_params("arbitrary"),
    )(zpre, dgated, ln_g, ln_b, w_s, b_col)


def _pool_select(g, parts):
    out = parts[-1]
    for gi in range(len(parts) - 2, -1, -1):
        out = jnp.where(g == gi, parts[gi], out)
    return out


def _pool_specs(s, bw):
    head = bw // len(B_WINDOWS)
    tc = _tile(head, 256)
    nb = head // tc
    return tc, (len(B_WINDOWS), nb), pl.BlockSpec((s, tc), lambda g, j: (0, g * nb + j))


def _pool_window(g, t):
    w = _pool_select(g, [jnp.full(t.shape, wi, jnp.int32) for wi in B_WINDOWS])
    return jnp.minimum(t + 1, w).astype(F32)


def _pool_fwd(name, vb):
    assert B_WINDOWS == (2, 4, 8, 16)
    s, bw = vb.shape
    tc, grid, spec = _pool_specs(s, bw)

    def body(v_ref, o_ref):
        g = pl.program_id(0)
        v = v_ref[...]
        t = lax.broadcasted_iota(jnp.int32, (s, tc), 0)

        def down(x, k):
            return jnp.where(t >= k, pltpu.roll(x, k, 0), 0.0)

        sums, cur, k = [], v, 1
        for _ in B_WINDOWS:
            cur = cur + down(cur, k)
            sums.append(cur)
            k *= 2
        o_ref[...] = (_pool_select(g, sums) / _pool_window(g, t) - v).astype(BF16)

    return pl.pallas_call(
        body, name=name, grid=grid, in_specs=[spec], out_specs=spec,
        out_shape=jax.ShapeDtypeStruct((s, bw), BF16), compiler_params=_params("parallel", "parallel"),
    )(vb)


def _pool_bwd(name, dpooled):
    s, bw = dpooled.shape
    tc, grid, spec = _pool_specs(s, bw)

    def body(d_ref, o_ref):
        g = pl.program_id(0)
        dp = d_ref[...]
        t = lax.broadcasted_iota(jnp.int32, (s, tc), 0)

        def up(x, k):
            return jnp.where(t < s - k, pltpu.roll(x, s - k, 0), 0.0)

        sums, cur, k = [], dp / _pool_window(g, t), 1
        for _ in B_WINDOWS:
            cur = cur + up(cur, k)
            sums.append(cur)
            k *= 2
        o_ref[...] = (_pool_select(g, sums) - dp).astype(BF16)

    return pl.pallas_call(
        body, name=name, grid=grid, in_specs=[spec], out_specs=spec,
        out_shape=jax.ShapeDtypeStruct((s, bw), BF16), compiler_params=_params("parallel", "parallel"),
    )(dpooled)


def _scale_bwd(name, dms, mixed, scale):
    s, bw = dms.shape
    tr = _tile(s, ROW_TILE)

    def body(d_ref, m_ref, sc_ref, o_ref, ds_ref):
        @pl.when(pl.program_id(0) == 0)
        def _():
            ds_ref[...] = jnp.zeros_like(ds_ref)

        dv = d_ref[...]
        ds_ref[...] += jnp.sum(dv * m_ref[...], axis=0, keepdims=True)
        o_ref[...] = (dv * sc_ref[...]).astype(BF16)

    return pl.pallas_call(
        body, name=name, grid=(s // tr,), in_specs=[_row_spec(tr, bw), _row_spec(tr, bw), _vec_spec(bw)],
        out_specs=[_row_spec(tr, bw), _vec_spec(bw)],
        out_shape=[jax.ShapeDtypeStruct((s, bw), BF16), jax.ShapeDtypeStruct((1, bw), F32)],
        compiler_params=_params("arbitrary"),
    )(dms, mixed, scale)


def _adam_update(w, g, m, v):
    m = ADAM_B1 * m + (1.0 - ADAM_B1) * g
    v = ADAM_B2 * v + (1.0 - ADAM_B2) * (g * g)
    m_hat = m / (1.0 - ADAM_B1 ** ADAM_STEP)
    v_hat = v / (1.0 - ADAM_B2 ** ADAM_STEP)
    delta = -ADAM_LR * (m_hat / (jnp.sqrt(v_hat) + ADAM_EPS) + ADAM_WD * w)
    return delta, m, v


def _adam_shard(name, w, m, v, parts, layer=0, prev=None, deps=()):
    nl, r, c = w.shape
    nj, nlp = parts.shape[:2]
    tr = _tile(r, ROW_TILE // 2)
    spec = pl.BlockSpec((None, tr, c), lambda l, i: (layer + l, i, 0))

    def body(w_ref, m_ref, v_ref, p_ref, *rest):
        g_ref, d_ref, nm_ref, nv_ref = rest[-4:]
        g = p_ref[0].astype(F32)
        for j in range(1, nj):
            g = g + p_ref[j].astype(F32)
        delta, nm, nv = _adam_update(w_ref[...], g, m_ref[...], v_ref[...])
        g_ref[...] = g
        d_ref[...] = delta
        nm_ref[...] = nm
        nv_ref[...] = nv

    prev = () if prev is None else tuple(prev)
    return pl.pallas_call(
        body, name=name, grid=(nlp, r // tr),
        in_specs=[spec, spec, spec, pl.BlockSpec((nj, None, tr, c), lambda l, i: (0, l, i, 0))]
        + [_ANY] * (len(prev) + len(deps)),
        out_specs=[spec] * 4, out_shape=[jax.ShapeDtypeStruct(w.shape, F32)] * 4,
        input_output_aliases={4 + i: i for i in range(len(prev))},
        compiler_params=_params("parallel", "parallel"),
    )(w, m, v, parts, *prev, *deps)


def _adam_packed(name, w, g, m, v):
    rows, lanes = w.shape
    tr = rows
    spec = pl.BlockSpec((tr, lanes), lambda i: (i, 0))

    def body(w_ref, g_ref, m_ref, v_ref, d_ref, nm_ref, nv_ref):
        delta, nm, nv = _adam_update(w_ref[...], g_ref[...], m_ref[...], v_ref[...])
        d_ref[...] = delta
        nm_ref[...] = nm
        nv_ref[...] = nv

    return pl.pallas_call(
        body, name=name, grid=(rows // tr,), in_specs=[spec] * 4, out_specs=[spec] * 3,
        out_shape=[jax.ShapeDtypeStruct(w.shape, F32)] * 3, compiler_params=_params("parallel"),
    )(w, g, m, v)


def _place():
    x, y, c = lax.axis_index("x"), lax.axis_index("y"), lax.axis_index("c")
    chips = [(1 - x, y), (x, 1 - y), (1 - x, 1 - y)]
    return x, y, c, chips


def _remote(src, dst, send_sem, recv_sem, device):
    return pltpu.make_async_remote_copy(src_ref=src, dst_ref=dst, send_sem=send_sem, recv_sem=recv_sem,
                                        device_id=device, device_id_type=MESH)


def _half(ref_rows, cc):
    h = ref_rows // 2
    return pl.ds(cc * h, h)


_HBM = pl.BlockSpec(memory_space=pltpu.HBM)
_SEM = pl.BlockSpec(memory_space=pltpu.SEMAPHORE)
_EFFECT = pltpu.SideEffectType.DATAFLOW_SIDE_EFFECTING


def _in_hbm(arr):
    return pltpu.with_memory_space_constraint(arr, pltpu.HBM)


def _gather_rows(land, cc, part=(0, 1)):
    k, n = part
    h = land.shape[2] // 2
    return pl.ds(cc * h + k * (h // n), h // n)


def _gather_block(land, split, j, cc, part=(0, 1)):
    if split:
        return land.at[j, :, _gather_rows(land, cc, part), :]
    return land.at[j]


def _split_start(name, srcs, lands, plan, deps=(), groups=None):
    ns = len(srcs)
    nl = len(lands) if groups is None else groups
    both = list(srcs) + list(lands)
    nb = len(both)

    def body(*refs):
        s, ld = refs[:ns], refs[ns:nb]
        outs = refs[nb + len(deps):]
        send, recv, token = outs[:nl], outs[nl:2 * nl], outs[-1]
        for a, copies in enumerate(plan(s, ld)):
            for src, dst, peer in copies:
                _remote(src, dst, send[a], recv[a], peer).start()
        token[...] = jnp.zeros_like(token)

    outs = pl.pallas_call(
        body, name=name, in_specs=[_HBM] * nb + [_ANY] * len(deps),
        out_specs=[_SEM] * (2 * nl) + [_HBM] * nb + [pl.BlockSpec(memory_space=pltpu.VMEM)],
        out_shape=[pltpu.SemaphoreType.DMA(())] * (2 * nl) + [pltpu.HBM(b.shape, b.dtype) for b in both]
        + [jax.ShapeDtypeStruct((8, LANES), F32)],
        input_output_aliases={i: 2 * nl + i for i in range(nb)},
        compiler_params=pltpu.CompilerParams(has_side_effects=_EFFECT),
    )(*[_in_hbm(b) for b in both], *deps)
    thru = outs[2 * nl:2 * nl + nb]
    return list(outs[:nl]), list(outs[nl:2 * nl]), list(thru[:ns]), list(thru[ns:]), outs[-1]


def _split_wait(name, srcs, lands, send, recv, after, whole):
    ns, nl = len(srcs), len(send)
    both = list(srcs) + list(lands)
    nb = len(both)

    def body(*refs):
        ld, snd, rcv = refs[ns:nb], refs[nb:nb + nl], refs[nb + nl:nb + 2 * nl]
        x, y, c, _ = _place()
        for a, blk in enumerate(whole(ld)):
            every = _remote(blk, blk, snd[a], rcv[a], (x, y, c))
            every.wait_send()
            every.wait_recv()

    outs = pl.pallas_call(
        body, name=name, in_specs=[_HBM] * nb + [_SEM] * (2 * nl) + [_ANY], out_specs=[_HBM] * nb,
        out_shape=[pltpu.HBM(b.shape, b.dtype) for b in both],
        input_output_aliases={i: i for i in range(nb)},
        compiler_params=pltpu.CompilerParams(has_side_effects=_EFFECT),
    )(*both, *send, *recv, after)
    return list(outs[:ns]), list(outs[ns:])


def _gather_plan(split, parts):
    def plan(srcs, lands):
        x, y, c, chips = _place()
        out = []
        for ld, sp, n in zip(lands, split, parts):
            for k in range(n):
                blk = _gather_block(ld, sp, 2 * x + y, c, (k, n))
                out.append([(blk, blk, (qx, qy, c)) for qx, qy in chips])
        return out
    return plan


def _gather_whole(split, part=(0, 1)):
    def whole(lands):
        _, _, c, _ = _place()
        return [ld.at[pl.ds(0, 3), :, _gather_rows(ld, c, part), :] if sp else ld.at[pl.ds(0, 3)]
                for ld, sp in zip(lands, split)]
    return whole


def _relay_plan(srcs, lands):
    x, y, c, _ = _place()
    out = []
    for ld in lands:
        blocks = [ld.at[j, :, _half(ld.shape[2], c), :] for j in range(N_CHIPS)]
        out.append([(blk, blk, (x, y, 1 - c)) for blk in blocks])
    return out


def _relay_whole(lands):
    _, _, c, _ = _place()
    return [ld.at[:, :, _half(ld.shape[2], c), :] for ld in lands]


def _fill_own(name, shard, dtype, place, layer=None):
    nl, r, c = shard.shape
    first = 0
    if layer is not None:
        nl, first = 1, layer
    tr = _tile(r, 512)

    def body(p_ref, s_ref, o_ref):
        o_ref[...] = s_ref[...].astype(o_ref.dtype)

    return pl.pallas_call(
        body, name=name,
        grid_spec=pltpu.PrefetchScalarGridSpec(
            num_scalar_prefetch=1, grid=(nl, r // tr),
            in_specs=[pl.BlockSpec((None, tr, c), lambda l, i, p: (first + l, i, 0))],
            out_specs=pl.BlockSpec((None, None, tr, c), lambda l, i, p: (p[0], l, i, 0))),
        out_shape=jax.ShapeDtypeStruct((N_CHIPS, nl, r, c), dtype),
        compiler_params=_params("parallel", "parallel"),
    )(place, shard)


def _gather_finish(name, lands, part=(0, 1)):
    n = len(lands)

    def body(*refs):
        outs = refs[n:2 * n]
        fsend, frecv = refs[2 * n:]
        x, y, c, chips = _place()
        sib = (x, y, 1 - c)

        def relay(a, qi, cc):
            qx, qy = chips[qi]
            blk = _gather_block(outs[a], True, 2 * qx + qy, cc, part)
            return _remote(blk, blk, fsend.at[a, qi], frecv.at[a, qi], sib)

        relays = [relay(a, qi, c) for a in range(n) for qi in range(3)]
        for cp in relays:
            cp.start()
        for a in range(n):
            for qi in range(3):
                relay(a, qi, 1 - c).wait_recv()
        for cp in relays:
            cp.wait_send()

    outs = pl.pallas_call(
        body, name=name, in_specs=[_ANY] * n, out_specs=[_ANY] * n,
        out_shape=[jax.ShapeDtypeStruct(ld.shape, ld.dtype) for ld in lands],
        input_output_aliases={i: i for i in range(n)},
        scratch_shapes=[pltpu.SemaphoreType.DMA((n, 3))] * 2,
    )(*lands)
    return list(outs)


def _pair_plan(srcs, lands):
    x, y, c, _ = _place()
    return [[(s.at[:, :, _half(s.shape[2], 1 - c), :], ld, (x, y, 1 - c))] for s, ld in zip(srcs, lands)]


def _pair_whole(lands):
    return list(lands)


def _scatter_plan(srcs, lands):
    x, y, c, chips = _place()
    return [[(s.at[2 * qx + qy], ld.at[2 * x + y, :, _half(ld.shape[2], c), :], (qx, qy, c)) for qx, qy in chips]
            for s, ld in zip(srcs, lands)]


def _scatter_whole(lands):
    _, _, c, _ = _place()
    return [ld.at[pl.ds(0, 3), :, _half(ld.shape[2], c), :] for ld in lands]


def _pair_sum(name, grad, other, place):
    nj, nl, r, c = grad.shape
    h = r // 2
    tr = _tile(h, 2 * ROW_TILE)
    nb = h // tr

    def body(p_ref, g_ref, o_ref, q_ref, d_ref):
        q = (g_ref[...].astype(F32) + o_ref[...].astype(F32)).astype(BF16)
        q_ref[...] = q

        @pl.when(pl.program_id(2) == p_ref[0])
        def _():
            d_ref[...] = q

    blk = (None, None, tr, c)
    return pl.pallas_call(
        body, name=name,
        grid_spec=pltpu.PrefetchScalarGridSpec(
            num_scalar_prefetch=1, grid=(nl, nb, nj),
            in_specs=[pl.BlockSpec(blk, lambda l, i, j, p: (j, l, p[1] * nb + i, 0)),
                      pl.BlockSpec(blk, lambda l, i, j, p: (j, l, i, 0))],
            out_specs=[pl.BlockSpec(blk, lambda l, i, j, p: (j, l, i, 0)),
                       pl.BlockSpec(blk, lambda l, i, j, p: (p[0], l, p[1] * nb + i, 0))]),
        out_shape=[jax.ShapeDtypeStruct((nj, nl, h, c), BF16), jax.ShapeDtypeStruct((nj, nl, r, c), BF16)],
        compiler_params=_params("parallel", "parallel", "arbitrary"),
    )(place, grad, other)


def _own_of_eight(name, packed, device):
    rows, lanes = packed.shape
    tr = rows

    def body(d_ref, s_ref, o_ref):
        o_ref[...] = s_ref[...]

    return pl.pallas_call(
        body, name=name,
        grid_spec=pltpu.PrefetchScalarGridSpec(
            num_scalar_prefetch=1, grid=(rows // tr,),
            in_specs=[pl.BlockSpec((tr, lanes), lambda i, d: (i, 0))],
            out_specs=pl.BlockSpec((None, tr, lanes), lambda i, d: (d[0], i, 0))),
        out_shape=jax.ShapeDtypeStruct((N_DEV, rows, lanes), packed.dtype),
        compiler_params=_params("parallel"),
    )(device, packed)


def _all_plan(srcs, lands):
    x, y, c, _ = _place()
    (ld,) = lands
    blk = ld.at[4 * x + 2 * y + c]
    flips = [(a, b, d) for a in (0, 1) for b in (0, 1) for d in (0, 1) if a + b + d]
    return [[(blk, blk, (x + a - 2 * a * x, y + b - 2 * b * y, c + d - 2 * d * c)) for a, b, d in flips]]


def _all_whole(lands):
    return [lands[0].at[pl.ds(0, N_DEV - 1)]]


def _sum_of_eight(name, slots):
    n, rows, lanes = slots.shape
    tr = rows

    def body(s_ref, o_ref):
        total = s_ref[0]
        for d in range(1, n):
            total = total + s_ref[d]
        o_ref[...] = total

    return pl.pallas_call(
        body, name=name, grid=(rows // tr,),
        in_specs=[pl.BlockSpec((n, tr, lanes), lambda i: (0, i, 0))],
        out_specs=pl.BlockSpec((tr, lanes), lambda i: (i, 0)),
        out_shape=jax.ShapeDtypeStruct((rows, lanes), F32), compiler_params=_params("parallel"),
    )(slots)


def _pack(parts):
    rows = []
    for p in parts:
        flat = p.reshape(-1)
        pad = (-flat.shape[0]) % PACK_ELEMS
        rows.append(jnp.pad(flat, (0, pad)).reshape(-1, LANES))
    return jnp.concatenate(rows, axis=0)


def _unpack(packed, shapes):
    out, row = [], 0
    for sh in shapes:
        size = math.prod(sh)
        nrows = -(-size // PACK_ELEMS) * (PACK_ELEMS // LANES)
        out.append(packed[row:row + nrows].reshape(-1)[:size].reshape(sh))
        row += nrows
    return out


def kernel(x, a_w_in, a_ln_g, a_ln_b, a_w_s, a_b_s, a_w_out, b_w_in, b_w_grp, b_scale, b_w_out, norm_mix, norm_mlp, mlp_w1, mlp_w2, final_norm, loss_target, m_a_w_in, m_a_ln_g, m_a_ln_b, m_a_w_s, m_a_b_s, m_a_w_out, m_b_w_in, m_b_w_grp, m_b_scale, m_b_w_out, m_norm_mix, m_norm_mlp, m_mlp_w1, m_mlp_w2, m_final_norm, v_a_w_in, v_a_ln_g, v_a_ln_b, v_a_w_s, v_a_b_s, v_a_w_out, v_b_w_in, v_b_w_grp, v_b_scale, v_b_w_out, v_norm_mix, v_norm_mlp, v_mlp_w1, v_mlp_w2, v_final_norm):
    xi, yi, ci = lax.axis_index("x"), lax.axis_index("y"), lax.axis_index("c")
    chip = 2 * xi + yi
    place = jnp.stack([chip, ci]).astype(jnp.int32)
    x2, tgt = x[0], loss_target[0]
    bw = b_scale.shape[1] * N_CHIPS

    units = dict(a_w_in=(a_w_in, None), a_w_out=(a_w_out, None), w1_0=(mlp_w1, 0), w2_0=(mlp_w2, 0),
                 b_scale=(b_scale.reshape(1, 1, -1), None), b_w_in=(b_w_in, None), b_w_grp=(b_w_grp[0], None),
                 b_w_out=(b_w_out, None), w1_1=(mlp_w1, 1), w2_1=(mlp_w2, 1))
    col_sharded = dict(a_w_in=True, a_w_out=False, b_w_in=False, b_w_grp=False, b_w_out=False,
                       w1_0=True, w2_0=False, w1_1=True, w2_1=False)
    in_flight, W = {}, {}

    def launch(tag, keys, deps):
        sp = [k != "b_scale" for k in keys]
        parts = [pieces.get(k, 1) for k in keys]
        zones = [_fill_own(f"gather_own_{k}", units[k][0], BF16 if s else F32, place, layer=units[k][1])
                 for k, s in zip(keys, sp)]
        send, recv, _, zones, tok = _split_start(f"gather_start_{tag}", [], zones, _gather_plan(sp, parts), deps,
                                                 groups=sum(parts))
        first = 0
        for k, z, s, n in zip(keys, zones, sp, parts):
            in_flight[k] = (send[first:first + n], recv[first:first + n], z, s)
            first += n
        return tok

    def arrive(keys, after):
        send, recv, zones, sp = zip(*[in_flight[k] for k in keys])
        n = len(send[0])
        if n > 1:
            (key,), zones = keys, list(zones)
            for k in range(n):
                _, zones = _split_wait(f"gather_wait_{key}_{k}", [], zones, [send[0][k]], [recv[0][k]], after,
                                       _gather_whole(sp, (k, n)))
                zones = _gather_finish(f"gather_finish_{key}_{k}", zones, (k, n))
            W[key] = _W4(zones[0], col_sharded[key])
            return
        _, zones = _split_wait(f"gather_wait_{keys[0]}", [], zones, [s[0] for s in send], [r[0] for r in recv],
                               after, _gather_whole(sp))
        relayed = iter(_gather_finish(f"gather_finish_{keys[0]}", [z for z, s in zip(zones, sp) if s]))
        for k, z, s in zip(keys, zones, sp):
            full = next(relayed) if s else z
            W[k] = _W4(full, col_sharded[k]) if k in col_sharded else full

    pieces = dict(w1_0=2, w2_0=2, w1_1=2, w2_1=2)

    token = launch("first", ["a_w_in", "a_w_out"], ())
    token = launch("rest", ["w1_0", "w2_0", "b_scale", "b_w_in", "b_w_grp", "b_w_out", "w1_1", "w2_1"], (token,))

    b_col = a_b_s[0].T

    def residual(acc, res):
        return (res + acc,)

    def sq_relu(acc):
        act = jnp.maximum(acc, 0.0)
        return act, act * act

    def mlp_fwd(tag, h, layer):
        hn = _rms_fwd(f"mlp{tag}_norm", h, norm_mlp[layer:layer + 1])
        arrive([f"w1_{layer}"], hn)
        act, act_sq = _mm_aw(f"mlp{tag}_up", hn, W[f"w1_{layer}"], out_dtypes=(BF16, BF16), epilogue=sq_relu)
        arrive([f"w2_{layer}"], act_sq)
        out = _mm_aw(f"mlp{tag}_down", act_sq, W[f"w2_{layer}"], extras=(h,), epilogue=residual)
        return out, (h, hn, act, act_sq)

    hn0 = _rms_fwd("mix_a_norm", x2, norm_mix[0:1])
    arrive(["a_w_in"], token)
    zpre = _mm_aw("mix_a_in", hn0, W["a_w_in"])
    gated = _gate_fwd("mix_a_gate", zpre, a_ln_g, a_ln_b, a_w_s[0], b_col)
    arrive(["a_w_out"], gated)
    h1 = _mm_aw("mix_a_out", gated, W["a_w_out"], extras=(x2,), epilogue=residual)
    h2, mlp0 = mlp_fwd("0", h1, 0)
    hn2 = _rms_fwd("mix_b_norm", h2, norm_mix[1:2])
    arrive(["b_scale", "b_w_in"], hn2)
    scale_full = W["b_scale"].reshape(1, bw)
    vb = _mm_aw("mix_b_in", hn2, W["b_w_in"])
    pooled = _pool_fwd("mix_b_pool", vb)
    arrive(["b_w_grp", "b_w_out"], pooled)
    mixed, ms = _mm_aw("mix_b_grp", pooled, W["b_w_grp"], groups=len(B_WINDOWS), extras=(scale_full,),
                       out_dtypes=(F32, BF16), epilogue=lambda acc, sc: (acc, acc * sc))
    h3 = _mm_aw("mix_b_out", ms, W["b_w_out"], extras=(h2,), epilogue=residual)
    h4, mlp1 = mlp_fwd("1", h3, 1)
    dh4, dh4_b, d_final, loss_part = _final("loss_head", h4, final_norm.reshape(1, -1), tgt)
    g1_like = _W4(None, True, shape=(N_CHIPS, 1, *W["w1_0"].arr.shape[2:]))
    g2_like = _W4(None, False, shape=(N_CHIPS, 1, *W["w2_0"].arr.shape[2:]))

    def exchange(tag, gs):
        zones = [lax.empty((g.shape[0], g.shape[1], g.shape[2] // 2, g.shape[3]), g.dtype) for g in gs]
        send, recv, srcs, zones, tok = _split_start(f"pair_start_{tag}", gs, zones, _pair_plan)
        return (tag, send, recv, srcs, zones), tok

    def reduce(state, after):
        tag, send, recv, srcs, zones = state
        srcs, zones = _split_wait(f"pair_wait_{tag}", srcs, zones, send, recv, after, _pair_whole)
        both = [_pair_sum(f"pair_sum_{tag}_{i}", g, o, place) for i, (g, o) in enumerate(zip(srcs, zones))]
        send, recv, sums, dests, tok = _split_start(f"scatter_start_{tag}", [b[0] for b in both],
                                                    [b[1] for b in both], _scatter_plan)
        return (tag, send, recv, sums, dests), tok

    def relay(state, after):
        tag, send, recv, sums, dests = state
        _, dests = _split_wait(f"scatter_wait_{tag}", sums, dests, send, recv, after, _scatter_whole)
        send, recv, _, dests, tok = _split_start(f"relay_start_{tag}", [], dests, _relay_plan)
        return (tag, send, recv, dests), tok

    def land(state, after):
        tag, send, recv, dests = state
        return _split_wait(f"relay_wait_{tag}", [], dests, send, recv, after, _relay_whole)[1]

    moments = dict(a_w_in=(m_a_w_in, v_a_w_in), a_w_out=(m_a_w_out, v_a_w_out), b_w_in=(m_b_w_in, v_b_w_in),
                   b_w_grp=(m_b_w_grp, v_b_w_grp), b_w_out=(m_b_w_out, v_b_w_out),
                   mlp_w1=(m_mlp_w1, v_mlp_w1), mlp_w2=(m_mlp_w2, v_mlp_w2))
    weights = dict(a_w_in=a_w_in, a_w_out=a_w_out, b_w_in=b_w_in, b_w_grp=b_w_grp, b_w_out=b_w_out,
                   mlp_w1=mlp_w1, mlp_w2=mlp_w2)
    results = {}

    def carried_adam(key, layer, relaying, after):
        (parts,) = land(relaying, after)
        return _AdamRider(weights[key], *moments[key], parts, layer, results.get(key))

    def mlp_bwd(tag, dh, dh_b, saved, layer, deps, pending=None, landed=None):
        h, hn, act, act_sq = saved
        relaying = None
        if landed is not None:
            relaying, tok = relay(landed[0], dh_b)
            deps = (*deps, tok)
        dpre = _mm_aw(f"mlp{tag}_down_dx", dh_b, W[f"w2_{layer}"], transpose_w=True, extras=(act,),
                      out_dtypes=(BF16,), epilogue=lambda acc, a: (acc * (2.0 * a),), deps=deps)
        scattering, dw_deps = None, ()
        if pending is not None:
            scattering, tok = reduce(pending, dpre)
            dw_deps = (tok,)
        if landed is None:
            g_w2 = _mm_dw(f"mlp{tag}_down_dw", act_sq, dh_b, g2_like, deps=dw_deps)
        else:
            g_w2, results["mlp_w2"] = _mm_dw(f"mlp{tag}_down_dw", act_sq, dh_b, g2_like, deps=dw_deps,
                                             rider=carried_adam("mlp_w2", layer + 1, relaying, dpre))
            relaying, tok_r = relay(landed[1], g_w2)
        pair_w2, tok = exchange(f"w2_{layer}", [g_w2])
        dhn = _mm_aw(f"mlp{tag}_up_dx", dpre, W[f"w1_{layer}"], transpose_w=True,
                     deps=(tok,) if landed is None else (tok, tok_r))
        if landed is None:
            g_w1 = _mm_dw(f"mlp{tag}_up_dw", hn, dpre, g1_like)
        else:
            g_w1, results["mlp_w1"] = _mm_dw(f"mlp{tag}_up_dw", hn, dpre, g1_like,
                                             rider=carried_adam("mlp_w1", layer + 1, relaying, dhn))
        pair_w1, tok1 = exchange(f"w1_{layer}", [g_w1])
        scat_w2, tok2 = reduce(pair_w2, dhn)
        dh_in, dh_in_b, d_norm = _rms_bwd(f"mlp{tag}_norm_bwd", h, norm_mlp[layer:layer + 1], dhn, dh)
        return dh_in, dh_in_b, d_norm, pair_w1, scat_w2, (tok1, tok2), scattering

    dh3, dh3_b, d_norm_mlp1, pair_w1_1, scat_w2_1, toks, _ = mlp_bwd("1", dh4, dh4_b, mlp1, 1, ())
    dms = _mm_aw("mix_b_out_dx", dh3_b, W["b_w_out"], transpose_w=True, deps=toks)
    g_b_out = _mm_dw("mix_b_out_dw", ms, dh3_b, W["b_w_out"])
    scat_w1_1, tok = reduce(pair_w1_1, dms)
    dmixed, d_scale = _scale_bwd("mix_b_scale_bwd", dms, mixed, scale_full)
    dpooled = _mm_aw("mix_b_grp_dx", dmixed, W["b_w_grp"], groups=len(B_WINDOWS), transpose_w=True, deps=(tok,))
    g_b_grp = _mm_dw("mix_b_grp_dw", pooled, dmixed, W["b_w_grp"], groups=len(B_WINDOWS))
    dvb = _pool_bwd("mix_b_pool_bwd", dpooled)
    dhn2 = _mm_aw("mix_b_in_dx", dvb, W["b_w_in"], transpose_w=True)
    g_b_in = _mm_dw("mix_b_in_dw", hn2, dvb, W["b_w_in"])
    pair_b, tok = exchange("b", [g_b_out, g_b_grp, g_b_in])
    dh2, dh2_b, d_norm_mix1 = _rms_bwd("mix_b_norm_bwd", h2, norm_mix[1:2], dhn2, dh3)
    dh1, dh1_b, d_norm_mlp0, pair_w1_0, scat_w2_0, toks, scat_b = mlp_bwd(
        "0", dh2, dh2_b, mlp0, 0, (tok,), pending=pair_b, landed=(scat_w2_1, scat_w1_1))
    dgated = _mm_aw("mix_a_out_dx", dh1_b, W["a_w_out"], transpose_w=True, deps=toks)
    g_a_out = _mm_dw("mix_a_out_dw", gated, dh1_b, W["a_w_out"])
    pair_a_out, tok_a = exchange("a_out", [g_a_out])
    scat_w1_0, tok = reduce(pair_w1_0, dgated)
    early = [relay(scat_b, dgated)]
    dzpre, d_w_s, d_b_col, d_ln_g, d_ln_b = _gate_bwd("mix_a_gate_bwd", zpre, dgated, a_ln_g, a_ln_b, a_w_s[0], b_col)
    dhn0 = _mm_aw("mix_a_in_dx", dzpre, W["a_w_in"], transpose_w=True, deps=(tok, tok_a, *[t for _, t in early]))
    scat_a_out, tok = reduce(pair_a_out, dhn0)
    g_a_in = _mm_dw("mix_a_in_dw", hn0, dzpre, W["a_w_in"], deps=(tok,))
    pair_a_in, tok = exchange("a_in", [g_a_in])
    dx, _, d_norm_mix0 = _rms_bwd("mix_a_norm_bwd", x2, norm_mix[0:1], dhn0, dh1)
    scat_a_in, _ = reduce(pair_a_in, dx)

    small = dict(a_ln_g=(a_ln_g, m_a_ln_g, v_a_ln_g), a_ln_b=(a_ln_b, m_a_ln_b, v_a_ln_b),
                 a_w_s=(a_w_s, m_a_w_s, v_a_w_s), a_b_s=(a_b_s, m_a_b_s, v_a_b_s),
                 b_scale=(b_scale, m_b_scale, v_b_scale), norm_mix=(norm_mix, m_norm_mix, v_norm_mix),
                 norm_mlp=(norm_mlp, m_norm_mlp, v_norm_mlp), final_norm=(final_norm, m_final_norm, v_final_norm))
    small_names = list(small)
    local = dict(a_ln_g=d_ln_g, a_ln_b=d_ln_b, a_w_s=d_w_s[None], a_b_s=d_b_col.T[None], b_scale=d_scale,
                 norm_mix=jnp.concatenate([d_norm_mix0, d_norm_mix1], axis=0),
                 norm_mlp=jnp.concatenate([d_norm_mlp0, d_norm_mlp1], axis=0), final_norm=d_final.reshape(-1))
    device = (4 * xi + 2 * yi + ci).astype(jnp.int32).reshape(1)
    slots = _own_of_eight("small_own", _pack([local[k] for k in small_names]), device)
    small_send, small_recv, _, (slots,), small_tok = _split_start("small_start", [], [slots], _all_plan)

    landing = [(scat_b, [("b_w_out", 0), ("b_w_grp", 0), ("b_w_in", 0)]),
               (scat_w2_0, [("mlp_w2", 0)]), (scat_w1_0, [("mlp_w1", 0)]),
               (scat_a_out, [("a_w_out", 0)]), (scat_a_in, [("a_w_in", 0)])]
    after = dx
    relays = list(early)
    for i, (_, members) in enumerate(landing):
        deps = (small_tok,) if i == 0 else ()
        if len(relays) == i + 1 < len(landing):
            relays.append(relay(landing[i + 1][0], after))
            deps = (relays[-1][1],)
        for (k, layer), parts in zip(members, land(relays[i][0], relays[-1][1] if deps else after)):
            shard_shape = (-1, *parts.shape[2:])
            results[k] = _adam_shard(f"adam_{k}_{layer}", weights[k].reshape(shard_shape),
                                     moments[k][0].reshape(shard_shape), moments[k][1].reshape(shard_shape),
                                     parts, layer=layer, prev=results.get(k), deps=deps)
            after = results[k][1]
    grad_out, delta_out, m_out, v_out = {}, {}, {}, {}
    for k, res in results.items():
        grad_out[k], delta_out[k], m_out[k], v_out[k] = [r.reshape(weights[k].shape) for r in res]

    _, (slots,) = _split_wait("small_wait", [], [slots], small_send, small_recv, after, _all_whole)
    reduced = _sum_of_eight("small_sum", slots)
    small_grads = dict(zip(small_names, _unpack(reduced, [local[k].shape for k in small_names])))
    shard_w = b_scale.shape[1]
    small_grads["b_scale"] = lax.dynamic_slice_in_dim(small_grads["b_scale"], chip * shard_w, shard_w, axis=1)
    small_grads = {k: small_grads[k].reshape(small[k][0].shape) for k in small_names}
    packed = [_pack([small[k][i] for k in small_names]) for i in range(3)]
    res = _adam_packed("adam_small", packed[0], _pack([small_grads[k] for k in small_names]), packed[1], packed[2])
    shapes = [small[k][0].shape for k in small_names]
    for k, d, nm, nv in zip(small_names, *[_unpack(r, shapes) for r in res]):
        grad_out[k], delta_out[k], m_out[k], v_out[k] = small_grads[k], d, nm, nv

    loss = lax.psum(loss_part[0, 0], ("x", "y", "c"))
    order = ["a_w_in", "a_ln_g", "a_ln_b", "a_w_s", "a_b_s", "a_w_out", "b_w_in", "b_w_grp", "b_scale", "b_w_out",
             "norm_mix", "norm_mlp", "mlp_w1", "mlp_w2", "final_norm"]
    return (loss, dx[None], *[grad_out[k] for k in order], *[delta_out[k] for k in order],
            *[m_out[k] for k in order], *[v_out[k] for k in order])
```

```python
import math

import jax
import jax.numpy as jnp
from jax import lax
from jax.experimental import pallas as pl
from jax.experimental.pallas import tpu as pltpu

F32 = jnp.float32
BF16 = jnp.bfloat16
MESH = pl.DeviceIdType.MESH

EPS = 1e-6
B_WINDOWS = (2, 4, 8, 16)
ADAM_LR = 0.001
ADAM_B1 = 0.9
ADAM_B2 = 0.999
ADAM_EPS = 1e-08
ADAM_WD = 0.01
ADAM_STEP = 10

N_CHIPS = 4
N_DEV = 8
LANES = 128
PACK_ELEMS = 8 * LANES
VMEM_LIMIT = 56 * 1024 * 1024
ROW_TILE = 512
MM_TM, MM_TN, MM_TK = 1024, 1024, 2048


_ANY = pl.BlockSpec(memory_space=pl.ANY)


def _tile(dim, pref):
    t = min(dim, pref)
    while dim % t:
        t //= 2
    return t


def _params(*sem):
    return pltpu.CompilerParams(dimension_semantics=sem, vmem_limit_bytes=VMEM_LIMIT)


class _W4:
    def __init__(self, arr, col_sharded, shape=None):
        self.arr = arr
        self.nj, self.nl, self.r, self.c = arr.shape if shape is None else shape
        self.col = col_sharded
        self.rows = self.r if col_sharded else self.nj * self.r
        self.cols = self.nj * self.c if col_sharded else self.c

    def tile_rows(self, pref):
        return _tile(self.r, pref)

    def tile_cols(self, pref):
        return _tile(self.c, pref)

    def index(self, layer, rb, cb, tr, tc):
        if self.col:
            n = self.c // tc
            return (cb // n, layer, rb, cb % n)
        n = self.r // tr
        return (rb // n, layer, rb % n, cb)


def _mm_aw(name, a, w, *, layer=0, groups=1, transpose_w=False, extras=(), out_dtypes=(F32,), epilogue=None,
           deps=()):
    s, ka_total = a.shape
    kdim, ndim = (w.cols, w.rows) if transpose_w else (w.rows, w.cols)
    assert ka_total == groups * kdim, (name, a.shape, kdim, groups)
    span = not w.col and ((not transpose_w and kdim <= MM_TK) or (transpose_w and groups > 1))
    if span and transpose_w:
        tm, tn, tk = _tile(s, 2048), ndim, w.tile_cols(MM_TK)
    else:
        tk = kdim if span else (w.tile_cols(MM_TK) if transpose_w else w.tile_rows(MM_TK))
        tm, tn_pref = (_tile(s, 2048), 512) if tk == kdim else (_tile(s, MM_TM), MM_TN)
        tn = w.tile_rows(tn_pref) if transpose_w else w.tile_cols(tn_pref)
    nk, nn = kdim // tk, ndim // tn

    def lay(g):
        return g if groups > 1 else layer

    a_spec = pl.BlockSpec((tm, tk), lambda g, i, n, k: (i, g * nk + k))
    if span and transpose_w:
        w_spec = pl.BlockSpec((w.nj, None, w.r, tk), lambda g, i, n, k: (0, lay(g), 0, k))
    elif span:
        w_spec = pl.BlockSpec((w.nj, None, w.r, tn), lambda g, i, n, k: (0, lay(g), 0, n))
    elif transpose_w:
        w_spec = pl.BlockSpec((None, None, tn, tk), lambda g, i, n, k: w.index(lay(g), n, k, tn, tk))
    else:
        w_spec = pl.BlockSpec((None, None, tk, tn), lambda g, i, n, k: w.index(lay(g), k, n, tk, tn))
    ex_specs = []
    for e in extras:
        assert e.shape[1] == groups * ndim and e.shape[0] in (1, s), (name, e.shape)
        if e.shape[0] == 1:
            ex_specs.append(pl.BlockSpec((1, tn), lambda g, i, n, k: (0, g * nn + n)))
        else:
            ex_specs.append(pl.BlockSpec((tm, tn), lambda g, i, n, k: (i, g * nn + n)))
    out_spec = pl.BlockSpec((tm, tn), lambda g, i, n, k: (i, g * nn + n))
    n_ex, n_out, n_dep = len(extras), len(out_dtypes), len(deps)

    def body(a_ref, w_ref, *rest):
        ex, outs = rest[:n_ex], rest[n_ex + n_dep:n_ex + n_dep + n_out]
        av = a_ref[...]
        if av.dtype != BF16:
            av = av.astype(BF16)
        wv = w_ref[...]
        if span:
            wv = wv.reshape((tn, tk) if transpose_w else (tk, tn))
        if transpose_w:
            prod = lax.dot_general(av, wv, (((1,), (1,)), ((), ())), preferred_element_type=F32)
        else:
            prod = jnp.dot(av, wv, preferred_element_type=F32)

        def finish(total):
            vals = (total,) if epilogue is None else epilogue(total, *[e[...] for e in ex])
            for o, v in zip(outs, vals):
                o[...] = v.astype(o.dtype)

        if nk == 1:
            finish(prod)
            return
        acc, k = rest[-1], pl.program_id(3)

        @pl.when(k == 0)
        def _():
            acc[...] = prod

        @pl.when(k > 0)
        def _():
            acc[...] += prod

        @pl.when(k == nk - 1)
        def _():
            finish(acc[...])

    outs = pl.pallas_call(
        body, name=name, grid=(groups, s // tm, nn, nk),
        in_specs=[a_spec, w_spec, *ex_specs] + [_ANY] * n_dep, out_specs=[out_spec] * n_out,
        out_shape=[jax.ShapeDtypeStruct((s, groups * ndim), dt) for dt in out_dtypes],
        scratch_shapes=[pltpu.VMEM((tm, tn), F32)] if nk > 1 else [],
        compiler_params=_params("parallel", "parallel", "parallel", "arbitrary"),
    )(a, w.arr, *extras, *deps)
    return outs[0] if n_out == 1 else outs


def _mm_dw(name, a, b, like, *, layer=0, groups=1, deps=()):
    s, ka_total = a.shape
    rows, cols = ka_total // groups, b.shape[1] // groups
    assert (rows, cols) == (like.rows, like.cols) and b.shape[0] == s, (name, a.shape, b.shape)
    span = groups > 1 and not like.col
    tm, tn, tk = rows if span else like.tile_rows(MM_TM), like.tile_cols(MM_TN), _tile(s, MM_TK)
    nr, nc, nk = rows // tm, cols // tn, s // tk
    assert nk == 1 or not span

    def lay(g):
        return g if groups > 1 else layer

    in_specs = [pl.BlockSpec((tk, tm), lambda g, n, i, k: (k, g * nr + i)),
                pl.BlockSpec((tk, tn), lambda g, n, i, k: (k, g * nc + n))]
    in_specs += [_ANY] * len(deps)

    def body(a_ref, b_ref, *rest):
        av, bv = a_ref[...], b_ref[...]
        if av.dtype != BF16:
            av = av.astype(BF16)
        if bv.dtype != BF16:
            bv = bv.astype(BF16)
        prod = lax.dot_general(av, bv, (((0,), (0,)), ((), ())), preferred_element_type=F32)
        if nk == 1:
            rest[-1][...] = prod.astype(BF16).reshape(rest[-1].shape)
            return
        o_ref, acc, k = rest[-2], rest[-1], pl.program_id(3)

        @pl.when(k == 0)
        def _():
            acc[...] = prod

        @pl.when(k > 0)
        def _():
            acc[...] += prod

        @pl.when(k == nk - 1)
        def _():
            o_ref[...] = acc[...].astype(BF16)

    if span:
        out_spec = pl.BlockSpec((like.nj, None, like.r, tn), lambda g, n, i, k: (0, g, 0, n))
    else:
        out_spec = pl.BlockSpec((None, None, tm, tn), lambda g, n, i, k: like.index(lay(g), i, n, tm, tn))
    return pl.pallas_call(
        body, name=name, grid=(groups, nc, nr, nk), in_specs=in_specs, out_specs=out_spec,
        out_shape=jax.ShapeDtypeStruct((like.nj, like.nl, like.r, like.c), BF16),
        scratch_shapes=[pltpu.VMEM((tm, tn), F32)] if nk > 1 else [],
        compiler_params=_params("parallel", "parallel", "parallel", "arbitrary"),
    )(a, b, *deps)


def _row_spec(tr, d):
    return pl.BlockSpec((tr, d), lambda i: (i, 0))


def _vec_spec(d):
    return pl.BlockSpec((1, d), lambda i: (0, 0))


def _rms_fwd(name, x, g):
    s, d = x.shape
    tr = _tile(s, ROW_TILE)

    def body(x_ref, g_ref, o_ref):
        xv = x_ref[...]
        r = lax.rsqrt(jnp.mean(xv * xv, axis=-1, keepdims=True) + EPS)
        o_ref[...] = (xv * r * g_ref[...]).astype(BF16)

    return pl.pallas_call(
        body, name=name, grid=(s // tr,), in_specs=[_row_spec(tr, d), _vec_spec(d)],
        out_specs=_row_spec(tr, d), out_shape=jax.ShapeDtypeStruct((s, d), BF16),
        compiler_params=_params("parallel"),
    )(x, g)


def _rms_bwd(name, x, g, dhn, dres):
    s, d = x.shape
    tr = _tile(s, ROW_TILE)

    def body(x_ref, g_ref, dhn_ref, dres_ref, dx_ref, dxb_ref, dg_ref):
        @pl.when(pl.program_id(0) == 0)
        def _():
            dg_ref[...] = jnp.zeros_like(dg_ref)

        xv = x_ref[...]
        r = lax.rsqrt(jnp.mean(xv * xv, axis=-1, keepdims=True) + EPS)
        xh = xv * r
        dy = dhn_ref[...]
        dg_ref[...] += jnp.sum(dy * xh, axis=0, keepdims=True)
        dxh = dy * g_ref[...]
        dx = dres_ref[...] + r * (dxh - xh * jnp.mean(dxh * xh, axis=-1, keepdims=True))
        dx_ref[...] = dx
        dxb_ref[...] = dx.astype(BF16)

    return pl.pallas_call(
        body, name=name, grid=(s // tr,),
        in_specs=[_row_spec(tr, d), _vec_spec(d), _row_spec(tr, d), _row_spec(tr, d)],
        out_specs=[_row_spec(tr, d), _row_spec(tr, d), _vec_spec(d)],
        out_shape=[jax.ShapeDtypeStruct((s, d), F32), jax.ShapeDtypeStruct((s, d), BF16),
                   jax.ShapeDtypeStruct((1, d), F32)],
        compiler_params=_params("arbitrary"),
    )(x, g, dhn, dres)


def _final(name, h, g, tgt):
    s, d = h.shape
    tr = _tile(s, ROW_TILE)

    def body(h_ref, g_ref, t_ref, dh_ref, dhb_ref, dg_ref, loss_ref):
        @pl.when(pl.program_id(0) == 0)
        def _():
            dg_ref[...] = jnp.zeros_like(dg_ref)
            loss_ref[...] = jnp.zeros_like(loss_ref)

        hv = h_ref[...]
        r = lax.rsqrt(jnp.mean(hv * hv, axis=-1, keepdims=True) + EPS)
        xh = hv * r
        gv = g_ref[...]
        err = xh * gv - t_ref[...]
        part = 0.5 * jnp.sum(jnp.mean(err * err, axis=-1, keepdims=True), axis=0, keepdims=True)
        loss_ref[...] += jnp.broadcast_to(part, loss_ref.shape)
        dy = err * (1.0 / d)
        dg_ref[...] += jnp.sum(dy * xh, axis=0, keepdims=True)
        dxh = dy * gv
        dh = r * (dxh - xh * jnp.mean(dxh * xh, axis=-1, keepdims=True))
        dh_ref[...] = dh
        dhb_ref[...] = dh.astype(BF16)

    return pl.pallas_call(
        body, name=name, grid=(s // tr,),
        in_specs=[_row_spec(tr, d), _vec_spec(d), _row_spec(tr, d)],
        out_specs=[_row_spec(tr, d), _row_spec(tr, d), _vec_spec(d), _vec_spec(LANES)],
        out_shape=[jax.ShapeDtypeStruct((s, d), F32), jax.ShapeDtypeStruct((s, d), BF16),
                   jax.ShapeDtypeStruct((1, d), F32), jax.ShapeDtypeStruct((1, LANES), F32)],
        compiler_params=_params("arbitrary"),
    )(h, g, tgt)


_SQRT_HALF = 1.0 / math.sqrt(2.0)
_INV_SQRT_2PI = 1.0 / math.sqrt(2.0 * math.pi)


def _gelu(x):
    return x * (lax.erf(x * _SQRT_HALF) + 1.0) * 0.5


def _gelu_grad(x):
    return 0.5 * (lax.erf(x * _SQRT_HALF) + 1.0) + x * jnp.exp(-0.5 * x * x) * _INV_SQRT_2PI


def _causal(chunk):
    t = lax.broadcasted_iota(jnp.int32, (chunk, chunk), 0)
    sidx = lax.broadcasted_iota(jnp.int32, (chunk, chunk), 1)
    return sidx <= t


def _layernorm_parts(v, g, b):
    mu = jnp.mean(v, axis=-1, keepdims=True)
    vc = v - mu
    rs = lax.rsqrt(jnp.mean(vc * vc, axis=-1, keepdims=True) + EPS)
    vhat = vc * rs
    return vhat, rs, vhat * g + b


def _gate_fwd(name, zpre, ln_g, ln_b, w_s, b_col):
    s, aw2 = zpre.shape
    aw = aw2 // 2
    ng, chunk, _ = w_s.shape
    dh = aw // ng

    def body(z_ref, g_ref, b_ref, ws_ref, bc_ref, o_ref):
        u = _gelu(z_ref[:, :aw])
        v = _gelu(z_ref[:, aw:])
        _, _, vln = _layernorm_parts(v, g_ref[...], b_ref[...])
        mask = _causal(chunk)
        for gi in range(ng):
            sl = slice(gi * dh, (gi + 1) * dh)
            wm = jnp.where(mask, ws_ref[gi], 0.0).astype(BF16)
            sg = jnp.dot(wm, vln[:, sl].astype(BF16), preferred_element_type=F32) + bc_ref[:, gi:gi + 1]
            o_ref[:, sl] = (u[:, sl] * sg).astype(BF16)

    return pl.pallas_call(
        body, name=name, grid=(s // chunk,),
        in_specs=[_row_spec(chunk, aw2), _vec_spec(aw), _vec_spec(aw),
                  pl.BlockSpec((ng, chunk, chunk), lambda i: (0, 0, 0)),
                  pl.BlockSpec((chunk, ng), lambda i: (0, 0))],
        out_specs=_row_spec(chunk, aw), out_shape=jax.ShapeDtypeStruct((s, aw), BF16),
        compiler_params=_params("parallel"),
    )(zpre, ln_g, ln_b, w_s, b_col)


def _gate_bwd(name, zpre, dgated, ln_g, ln_b, w_s, b_col):
    s, aw2 = zpre.shape
    aw = aw2 // 2
    ng, chunk, _ = w_s.shape
    dh = aw // ng

    def body(z_ref, dgt_ref, g_ref, b_ref, ws_ref, bc_ref, dz_ref, dws_ref, dbc_ref, dlg_ref, dlb_ref,
             du_scr, dvln_scr):
        @pl.when(pl.program_id(0) == 0)
        def _():
            dws_ref[...] = jnp.zeros_like(dws_ref)
            dbc_ref[...] = jnp.zeros_like(dbc_ref)
            dlg_ref[...] = jnp.zeros_like(dlg_ref)
            dlb_ref[...] = jnp.zeros_like(dlb_ref)

        zu = z_ref[:, :aw]
        zv = z_ref[:, aw:]
        u = _gelu(zu)
        lg = g_ref[...]
        vhat, rs, vln = _layernorm_parts(_gelu(zv), lg, b_ref[...])
        mask = _causal(chunk)
        for gi in range(ng):
            sl = slice(gi * dh, (gi + 1) * dh)
            wm = jnp.where(mask, ws_ref[gi], 0.0).astype(BF16)
            vg = vln[:, sl].astype(BF16)
            sg = jnp.dot(wm, vg, preferred_element_type=F32) + bc_ref[:, gi:gi + 1]
            dgt = dgt_ref[:, sl]
            du_scr[:, sl] = dgt * sg
            ds = dgt * u[:, sl]
            dbc_ref[:, gi:gi + 1] += jnp.sum(ds, axis=-1, keepdims=True)
            dsb = ds.astype(BF16)
            dwm = lax.dot_general(dsb, vg, (((1,), (1,)), ((), ())), preferred_element_type=F32)
            dws_ref[gi] += jnp.where(mask, dwm, 0.0)
            dvln_scr[:, sl] = lax.dot_general(wm, dsb, (((0,), (0,)), ((), ())), preferred_element_type=F32)
        dvln = dvln_scr[...]
        dlb_ref[...] += jnp.sum(dvln, axis=0, keepdims=True)
        dlg_ref[...] += jnp.sum(dvln * vhat, axis=0, keepdims=True)
        dvh = dvln * lg
        dv = rs * (dvh - jnp.mean(dvh, axis=-1, keepdims=True)
                   - vhat * jnp.mean(dvh * vhat, axis=-1, keepdims=True))
        dz_ref[:, :aw] = (du_scr[...] * _gelu_grad(zu)).astype(BF16)
        dz_ref[:, aw:] = (dv * _gelu_grad(zv)).astype(BF16)

    return pl.pallas_call(
        body, name=name, grid=(s // chunk,),
        in_specs=[_row_spec(chunk, aw2), _row_spec(chunk, aw), _vec_spec(aw), _vec_spec(aw),
                  pl.BlockSpec((ng, chunk, chunk), lambda i: (0, 0, 0)),
                  pl.BlockSpec((chunk, ng), lambda i: (0, 0))],
        out_specs=[_row_spec(chunk, aw2), pl.BlockSpec((ng, chunk, chunk), lambda i: (0, 0, 0)),
                   pl.BlockSpec((chunk, ng), lambda i: (0, 0)), _vec_spec(aw), _vec_spec(aw)],
        out_shape=[jax.ShapeDtypeStruct((s, aw2), BF16), jax.ShapeDtypeStruct((ng, chunk, chunk), F32),
                   jax.ShapeDtypeStruct((chunk, ng), F32), jax.ShapeDtypeStruct((1, aw), F32),
                   jax.ShapeDtypeStruct((1, aw), F32)],
        scratch_shapes=[pltpu.VMEM((chunk, aw), F32), pltpu.VMEM((chunk, aw), F32)],
        compiler_params=_params("arbitrary"),
    )(zpre, dgated, ln_g, ln_b, w_s, b_col)


def _pool_select(g, parts):
    out = parts[-1]
    for gi in range(len(parts) - 2, -1, -1):
        out = jnp.where(g == gi, parts[gi], out)
    return out


def _pool_specs(s, bw):
    head = bw // len(B_WINDOWS)
    tc = _tile(head, 256)
    nb = head // tc
    return tc, (len(B_WINDOWS), nb), pl.BlockSpec((s, tc), lambda g, j: (0, g * nb + j))


def _pool_window(g, t):
    w = _pool_select(g, [jnp.full(t.shape, wi, jnp.int32) for wi in B_WINDOWS])
    return jnp.minimum(t + 1, w).astype(F32)


def _pool_fwd(name, vb):
    assert B_WINDOWS == (2, 4, 8, 16)
    s, bw = vb.shape
    tc, grid, spec = _pool_specs(s, bw)

    def body(v_ref, o_ref):
        g = pl.program_id(0)
        v = v_ref[...]
        t = lax.broadcasted_iota(jnp.int32, (s, tc), 0)

        def down(x, k):
            return jnp.where(t >= k, pltpu.roll(x, k, 0), 0.0)

        sums, cur, k = [], v, 1
        for _ in B_WINDOWS:
            cur = cur + down(cur, k)
            sums.append(cur)
            k *= 2
        o_ref[...] = (_pool_select(g, sums) / _pool_window(g, t) - v).astype(BF16)

    return pl.pallas_call(
        body, name=name, grid=grid, in_specs=[spec], out_specs=spec,
        out_shape=jax.ShapeDtypeStruct((s, bw), BF16), compiler_params=_params("parallel", "parallel"),
    )(vb)


def _pool_bwd(name, dpooled):
    s, bw = dpooled.shape
    tc, grid, spec = _pool_specs(s, bw)

    def body(d_ref, o_ref):
        g = pl.program_id(0)
        dp = d_ref[...]
        t = lax.broadcasted_iota(jnp.int32, (s, tc), 0)

        def up(x, k):
            return jnp.where(t < s - k, pltpu.roll(x, s - k, 0), 0.0)

        sums, cur, k = [], dp / _pool_window(g, t), 1
        for _ in B_WINDOWS:
            cur = cur + up(cur, k)
            sums.append(cur)
            k *= 2
        o_ref[...] = (_pool_select(g, sums) - dp).astype(BF16)

    return pl.pallas_call(
        body, name=name, grid=grid, in_specs=[spec], out_specs=spec,
        out_shape=jax.ShapeDtypeStruct((s, bw), BF16), compiler_params=_params("parallel", "parallel"),
    )(dpooled)


def _scale_bwd(name, dms, mixed, scale):
    s, bw = dms.shape
    tr = _tile(s, ROW_TILE)

    def body(d_ref, m_ref, sc_ref, o_ref, ds_ref):
        @pl.when(pl.program_id(0) == 0)
        def _():
            ds_ref[...] = jnp.zeros_like(ds_ref)

        dv = d_ref[...]
        ds_ref[...] += jnp.sum(dv * m_ref[...], axis=0, keepdims=True)
        o_ref[...] = (dv * sc_ref[...]).astype(BF16)

    return pl.pallas_call(
        body, name=name, grid=(s // tr,), in_specs=[_row_spec(tr, bw), _row_spec(tr, bw), _vec_spec(bw)],
        out_specs=[_row_spec(tr, bw), _vec_spec(bw)],
        out_shape=[jax.ShapeDtypeStruct((s, bw), BF16), jax.ShapeDtypeStruct((1, bw), F32)],
        compiler_params=_params("arbitrary"),
    )(dms, mixed, scale)


def _adam_update(w, g, m, v):
    m = ADAM_B1 * m + (1.0 - ADAM_B1) * g
    v = ADAM_B2 * v + (1.0 - ADAM_B2) * (g * g)
    m_hat = m / (1.0 - ADAM_B1 ** ADAM_STEP)
    v_hat = v / (1.0 - ADAM_B2 ** ADAM_STEP)
    delta = -ADAM_LR * (m_hat / (jnp.sqrt(v_hat) + ADAM_EPS) + ADAM_WD * w)
    return delta, m, v


def _adam_shard(name, w, m, v, parts, layer=0, prev=None, deps=()):
    nl, r, c = w.shape
    nj, nlp = parts.shape[:2]
    tr = _tile(r, ROW_TILE // 2)
    spec = pl.BlockSpec((None, tr, c), lambda l, i: (layer + l, i, 0))

    def body(w_ref, m_ref, v_ref, p_ref, *rest):
        g_ref, d_ref, nm_ref, nv_ref = rest[-4:]
        g = p_ref[0].astype(F32)
        for j in range(1, nj):
            g = g + p_ref[j].astype(F32)
        delta, nm, nv = _adam_update(w_ref[...], g, m_ref[...], v_ref[...])
        g_ref[...] = g
        d_ref[...] = delta
        nm_ref[...] = nm
        nv_ref[...] = nv

    prev = () if prev is None else tuple(prev)
    return pl.pallas_call(
        body, name=name, grid=(nlp, r // tr),
        in_specs=[spec, spec, spec, pl.BlockSpec((nj, None, tr, c), lambda l, i: (0, l, i, 0))]
        + [_ANY] * (len(prev) + len(deps)),
        out_specs=[spec] * 4, out_shape=[jax.ShapeDtypeStruct(w.shape, F32)] * 4,
        input_output_aliases={4 + i: i for i in range(len(prev))},
        compiler_params=_params("parallel", "parallel"),
    )(w, m, v, parts, *prev, *deps)


def _adam_packed(name, w, g, m, v):
    rows, lanes = w.shape
    tr = rows
    spec = pl.BlockSpec((tr, lanes), lambda i: (i, 0))

    def body(w_ref, g_ref, m_ref, v_ref, d_ref, nm_ref, nv_ref):
        delta, nm, nv = _adam_update(w_ref[...], g_ref[...], m_ref[...], v_ref[...])
        d_ref[...] = delta
        nm_ref[...] = nm
        nv_ref[...] = nv

    return pl.pallas_call(
        body, name=name, grid=(rows // tr,), in_specs=[spec] * 4, out_specs=[spec] * 3,
        out_shape=[jax.ShapeDtypeStruct(w.shape, F32)] * 3, compiler_params=_params("parallel"),
    )(w, g, m, v)


def _place():
    x, y, c = lax.axis_index("x"), lax.axis_index("y"), lax.axis_index("c")
    chips = [(1 - x, y), (x, 1 - y), (1 - x, 1 - y)]
    return x, y, c, chips


def _remote(src, dst, send_sem, recv_sem, device):
    return pltpu.make_async_remote_copy(src_ref=src, dst_ref=dst, send_sem=send_sem, recv_sem=recv_sem,
                                        device_id=device, device_id_type=MESH)


def _half(ref_rows, cc):
    h = ref_rows // 2
    return pl.ds(cc * h, h)


_HBM = pl.BlockSpec(memory_space=pltpu.HBM)
_SEM = pl.BlockSpec(memory_space=pltpu.SEMAPHORE)
_EFFECT = pltpu.SideEffectType.DATAFLOW_SIDE_EFFECTING


def _in_hbm(arr):
    return pltpu.with_memory_space_constraint(arr, pltpu.HBM)


def _gather_rows(land, cc, part=(0, 1)):
    k, n = part
    h = land.shape[2] // 2
    return pl.ds(cc * h + k * (h // n), h // n)


def _gather_block(land, split, j, cc, part=(0, 1)):
    if split:
        return land.at[j, :, _gather_rows(land, cc, part), :]
    return land.at[j]


def _split_start(name, srcs, lands, plan, deps=(), groups=None):
    ns = len(srcs)
    nl = len(lands) if groups is None else groups
    both = list(srcs) + list(lands)
    nb = len(both)

    def body(*refs):
        s, ld = refs[:ns], refs[ns:nb]
        outs = refs[nb + len(deps):]
        send, recv, token = outs[:nl], outs[nl:2 * nl], outs[-1]
        for a, copies in enumerate(plan(s, ld)):
            for src, dst, peer in copies:
                _remote(src, dst, send[a], recv[a], peer).start()
        token[...] = jnp.zeros_like(token)

    outs = pl.pallas_call(
        body, name=name, in_specs=[_HBM] * nb + [_ANY] * len(deps),
        out_specs=[_SEM] * (2 * nl) + [_HBM] * nb + [pl.BlockSpec(memory_space=pltpu.VMEM)],
        out_shape=[pltpu.SemaphoreType.DMA(())] * (2 * nl) + [pltpu.HBM(b.shape, b.dtype) for b in both]
        + [jax.ShapeDtypeStruct((8, LANES), F32)],
        input_output_aliases={i: 2 * nl + i for i in range(nb)},
        compiler_params=pltpu.CompilerParams(has_side_effects=_EFFECT),
    )(*[_in_hbm(b) for b in both], *deps)
    thru = outs[2 * nl:2 * nl + nb]
    return list(outs[:nl]), list(outs[nl:2 * nl]), list(thru[:ns]), list(thru[ns:]), outs[-1]


def _split_wait(name, srcs, lands, send, recv, after, whole):
    ns, nl = len(srcs), len(send)
    both = list(srcs) + list(lands)
    nb = len(both)

    def body(*refs):
        ld, snd, rcv = refs[ns:nb], refs[nb:nb + nl], refs[nb + nl:nb + 2 * nl]
        x, y, c, _ = _place()
        for a, blk in enumerate(whole(ld)):
            every = _remote(blk, blk, snd[a], rcv[a], (x, y, c))
            every.wait_send()
            every.wait_recv()

    outs = pl.pallas_call(
        body, name=name, in_specs=[_HBM] * nb + [_SEM] * (2 * nl) + [_ANY], out_specs=[_HBM] * nb,
        out_shape=[pltpu.HBM(b.shape, b.dtype) for b in both],
        input_output_aliases={i: i for i in range(nb)},
        compiler_params=pltpu.CompilerParams(has_side_effects=_EFFECT),
    )(*both, *send, *recv, after)
    return list(outs[:ns]), list(outs[ns:])


def _gather_plan(split, parts):
    def plan(srcs, lands):
        x, y, c, chips = _place()
        out = []
        for ld, sp, n in zip(lands, split, parts):
            for k in range(n):
                blk = _gather_block(ld, sp, 2 * x + y, c, (k, n))
                out.append([(blk, blk, (qx, qy, c)) for qx, qy in chips])
        return out
    return plan


def _gather_whole(split, part=(0, 1)):
    def whole(lands):
        _, _, c, _ = _place()
        return [ld.at[pl.ds(0, 3), :, _gather_rows(ld, c, part), :] if sp else ld.at[pl.ds(0, 3)]
                for ld, sp in zip(lands, split)]
    return whole


def _relay_plan(srcs, lands):
    x, y, c, _ = _place()
    out = []
    for ld in lands:
        blocks = [ld.at[j, :, _half(ld.shape[2], c), :] for j in range(N_CHIPS)]
        out.append([(blk, blk, (x, y, 1 - c)) for blk in blocks])
    return out


def _relay_whole(lands):
    _, _, c, _ = _place()
    return [ld.at[:, :, _half(ld.shape[2], c), :] for ld in lands]


def _fill_own(name, shard, dtype, place, layer=None):
    nl, r, c = shard.shape
    first = 0
    if layer is not None:
        nl, first = 1, layer
    tr = _tile(r, 512)

    def body(p_ref, s_ref, o_ref):
        o_ref[...] = s_ref[...].astype(o_ref.dtype)

    return pl.pallas_call(
        body, name=name,
        grid_spec=pltpu.PrefetchScalarGridSpec(
            num_scalar_prefetch=1, grid=(nl, r // tr),
            in_specs=[pl.BlockSpec((None, tr, c), lambda l, i, p: (first + l, i, 0))],
            out_specs=pl.BlockSpec((None, None, tr, c), lambda l, i, p: (p[0], l, i, 0))),
        out_shape=jax.ShapeDtypeStruct((N_CHIPS, nl, r, c), dtype),
        compiler_params=_params("parallel", "parallel"),
    )(place, shard)


def _gather_finish(name, lands, part=(0, 1)):
    n = len(lands)

    def body(*refs):
        outs = refs[n:2 * n]
        fsend, frecv = refs[2 * n:]
        x, y, c, chips = _place()
        sib = (x, y, 1 - c)

        def relay(a, qi, cc):
            qx, qy = chips[qi]
            blk = _gather_block(outs[a], True, 2 * qx + qy, cc, part)
            return _remote(blk, blk, fsend.at[a, qi], frecv.at[a, qi], sib)

        relays = [relay(a, qi, c) for a in range(n) for qi in range(3)]
        for cp in relays:
            cp.start()
        for a in range(n):
            for qi in range(3):
                relay(a, qi, 1 - c).wait_recv()
        for cp in relays:
            cp.wait_send()

    outs = pl.pallas_call(
        body, name=name, in_specs=[_ANY] * n, out_specs=[_ANY] * n,
        out_shape=[jax.ShapeDtypeStruct(ld.shape, ld.dtype) for ld in lands],
        input_output_aliases={i: i for i in range(n)},
        scratch_shapes=[pltpu.SemaphoreType.DMA((n, 3))] * 2,
    )(*lands)
    return list(outs)


def _pair_plan(srcs, lands):
    x, y, c, _ = _place()
    return [[(s.at[:, :, _half(s.shape[2], 1 - c), :], ld, (x, y, 1 - c))] for s, ld in zip(srcs, lands)]


def _pair_whole(lands):
    return list(lands)


def _scatter_plan(srcs, lands):
    x, y, c, chips = _place()
    return [[(s.at[2 * qx + qy], ld.at[2 * x + y, :, _half(ld.shape[2], c), :], (qx, qy, c)) for qx, qy in chips]
            for s, ld in zip(srcs, lands)]


def _scatter_whole(lands):
    _, _, c, _ = _place()
    return [ld.at[pl.ds(0, 3), :, _half(ld.shape[2], c), :] for ld in lands]


def _pair_sum(name, grad, other, place):
    nj, nl, r, c = grad.shape
    h = r // 2
    tr = _tile(h, 2 * ROW_TILE)
    nb = h // tr

    def body(p_ref, g_ref, o_ref, q_ref, d_ref):
        q = (g_ref[...].astype(F32) + o_ref[...].astype(F32)).astype(BF16)
        q_ref[...] = q

        @pl.when(pl.program_id(2) == p_ref[0])
        def _():
            d_ref[...] = q

    blk = (None, None, tr, c)
    return pl.pallas_call(
        body, name=name,
        grid_spec=pltpu.PrefetchScalarGridSpec(
            num_scalar_prefetch=1, grid=(nl, nb, nj),
            in_specs=[pl.BlockSpec(blk, lambda l, i, j, p: (j, l, p[1] * nb + i, 0)),
                      pl.BlockSpec(blk, lambda l, i, j, p: (j, l, i, 0))],
            out_specs=[pl.BlockSpec(blk, lambda l, i, j, p: (j, l, i, 0)),
                       pl.BlockSpec(blk, lambda l, i, j, p: (p[0], l, p[1] * nb + i, 0))]),
        out_shape=[jax.ShapeDtypeStruct((nj, nl, h, c), BF16), jax.ShapeDtypeStruct((nj, nl, r, c), BF16)],
        compiler_params=_params("parallel", "parallel", "arbitrary"),
    )(place, grad, other)


def _own_of_eight(name, packed, device):
    rows, lanes = packed.shape
    tr = rows

    def body(d_ref, s_ref, o_ref):
        o_ref[...] = s_ref[...]

    return pl.pallas_call(
        body, name=name,
        grid_spec=pltpu.PrefetchScalarGridSpec(
            num_scalar_prefetch=1, grid=(rows // tr,),
            in_specs=[pl.BlockSpec((tr, lanes), lambda i, d: (i, 0))],
            out_specs=pl.BlockSpec((None, tr, lanes), lambda i, d: (d[0], i, 0))),
        out_shape=jax.ShapeDtypeStruct((N_DEV, rows, lanes), packed.dtype),
        compiler_params=_params("parallel"),
    )(device, packed)


def _all_plan(srcs, lands):
    x, y, c, _ = _place()
    (ld,) = lands
    blk = ld.at[4 * x + 2 * y + c]
    flips = [(a, b, d) for a in (0, 1) for b in (0, 1) for d in (0, 1) if a + b + d]
    return [[(blk, blk, (x + a - 2 * a * x, y + b - 2 * b * y, c + d - 2 * d * c)) for a, b, d in flips]]


def _all_whole(lands):
    return [lands[0].at[pl.ds(0, N_DEV - 1)]]


def _sum_of_eight(name, slots):
    n, rows, lanes = slots.shape
    tr = rows

    def body(s_ref, o_ref):
        total = s_ref[0]
        for d in range(1, n):
            total = total + s_ref[d]
        o_ref[...] = total

    return pl.pallas_call(
        body, name=name, grid=(rows // tr,),
        in_specs=[pl.BlockSpec((n, tr, lanes), lambda i: (0, i, 0))],
        out_specs=pl.BlockSpec((tr, lanes), lambda i: (i, 0)),
        out_shape=jax.ShapeDtypeStruct((rows, lanes), F32), compiler_params=_params("parallel"),
    )(slots)


def _pack(parts):
    rows = []
    for p in parts:
        flat = p.reshape(-1)
        pad = (-flat.shape[0]) % PACK_ELEMS
        rows.append(jnp.pad(flat, (0, pad)).reshape(-1, LANES))
    return jnp.concatenate(rows, axis=0)


def _unpack(packed, shapes):
    out, row = [], 0
    for sh in shapes:
        size = math.prod(sh)
        nrows = -(-size // PACK_ELEMS) * (PACK_ELEMS // LANES)
        out.append(packed[row:row + nrows].reshape(-1)[:size].reshape(sh))
        row += nrows
    return out


def kernel(x, a_w_in, a_ln_g, a_ln_b, a_w_s, a_b_s, a_w_out, b_w_in, b_w_grp, b_scale, b_w_out, norm_mix, norm_mlp, mlp_w1, mlp_w2, final_norm, loss_target, m_a_w_in, m_a_ln_g, m_a_ln_b, m_a_w_s, m_a_b_s, m_a_w_out, m_b_w_in, m_b_w_grp, m_b_scale, m_b_w_out, m_norm_mix, m_norm_mlp, m_mlp_w1, m_mlp_w2, m_final_norm, v_a_w_in, v_a_ln_g, v_a_ln_b, v_a_w_s, v_a_b_s, v_a_w_out, v_b_w_in, v_b_w_grp, v_b_scale, v_b_w_out, v_norm_mix, v_norm_mlp, v_mlp_w1, v_mlp_w2, v_final_norm):
    xi, yi, ci = lax.axis_index("x"), lax.axis_index("y"), lax.axis_index("c")
    chip = 2 * xi + yi
    place = jnp.stack([chip, ci]).astype(jnp.int32)
    x2, tgt = x[0], loss_target[0]
    bw = b_scale.shape[1] * N_CHIPS

    units = dict(a_w_in=(a_w_in, None), a_w_out=(a_w_out, None), w1_0=(mlp_w1, 0), w2_0=(mlp_w2, 0),
                 b_scale=(b_scale.reshape(1, 1, -1), None), b_w_in=(b_w_in, None), b_w_grp=(b_w_grp[0], None),
                 b_w_out=(b_w_out, None), w1_1=(mlp_w1, 1), w2_1=(mlp_w2, 1))
    col_sharded = dict(a_w_in=True, a_w_out=False, b_w_in=False, b_w_grp=False, b_w_out=False,
                       w1_0=True, w2_0=False, w1_1=True, w2_1=False)
    in_flight, W = {}, {}

    def launch(tag, keys, deps):
        sp = [k != "b_scale" for k in keys]
        parts = [pieces.get(k, 1) for k in keys]
        zones = [_fill_own(f"gather_own_{k}", units[k][0], BF16 if s else F32, place, layer=units[k][1])
                 for k, s in zip(keys, sp)]
        send, recv, _, zones, tok = _split_start(f"gather_start_{tag}", [], zones, _gather_plan(sp, parts), deps,
                                                 groups=sum(parts))
        first = 0
        for k, z, s, n in zip(keys, zones, sp, parts):
            in_flight[k] = (send[first:first + n], recv[first:first + n], z, s)
            first += n
        return tok

    def arrive(keys, after):
        send, recv, zones, sp = zip(*[in_flight[k] for k in keys])
        n = len(send[0])
        if n > 1:
            (key,), zones = keys, list(zones)
            for k in range(n):
                _, zones = _split_wait(f"gather_wait_{key}_{k}", [], zones, [send[0][k]], [recv[0][k]], after,
                                       _gather_whole(sp, (k, n)))
                zones = _gather_finish(f"gather_finish_{key}_{k}", zones, (k, n))
            W[key] = _W4(zones[0], col_sharded[key])
            return
        _, zones = _split_wait(f"gather_wait_{keys[0]}", [], zones, [s[0] for s in send], [r[0] for r in recv],
                               after, _gather_whole(sp))
        relayed = iter(_gather_finish(f"gather_finish_{keys[0]}", [z for z, s in zip(zones, sp) if s]))
        for k, z, s in zip(keys, zones, sp):
            full = next(relayed) if s else z
            W[k] = _W4(full, col_sharded[k]) if k in col_sharded else full

    pieces = dict(w1_0=2, w2_0=2, w1_1=2, w2_1=2)

    token = launch("first", ["a_w_in", "a_w_out"], ())
    token = launch("rest", ["w1_0", "w2_0", "b_scale", "b_w_in", "b_w_grp", "b_w_out", "w1_1", "w2_1"], (token,))

    b_col = a_b_s[0].T

    def residual(acc, res):
        return (res + acc,)

    def sq_relu(acc):
        act = jnp.maximum(acc, 0.0)
        return act, act * act

    def mlp_fwd(tag, h, layer):
        hn = _rms_fwd(f"mlp{tag}_norm", h, norm_mlp[layer:layer + 1])
        arrive([f"w1_{layer}"], hn)
        act, act_sq = _mm_aw(f"mlp{tag}_up", hn, W[f"w1_{layer}"], out_dtypes=(BF16, BF16), epilogue=sq_relu)
        arrive([f"w2_{layer}"], act_sq)
        out = _mm_aw(f"mlp{tag}_down", act_sq, W[f"w2_{layer}"], extras=(h,), epilogue=residual)
        return out, (h, hn, act, act_sq)

    hn0 = _rms_fwd("mix_a_norm", x2, norm_mix[0:1])
    arrive(["a_w_in"], token)
    zpre = _mm_aw("mix_a_in", hn0, W["a_w_in"])
    gated = _gate_fwd("mix_a_gate", zpre, a_ln_g, a_ln_b, a_w_s[0], b_col)
    arrive(["a_w_out"], gated)
    h1 = _mm_aw("mix_a_out", gated, W["a_w_out"], extras=(x2,), epilogue=residual)
    h2, mlp0 = mlp_fwd("0", h1, 0)
    hn2 = _rms_fwd("mix_b_norm", h2, norm_mix[1:2])
    arrive(["b_scale", "b_w_in"], hn2)
    scale_full = W["b_scale"].reshape(1, bw)
    vb = _mm_aw("mix_b_in", hn2, W["b_w_in"])
    pooled = _pool_fwd("mix_b_pool", vb)
    arrive(["b_w_grp", "b_w_out"], pooled)
    mixed, ms = _mm_aw("mix_b_grp", pooled, W["b_w_grp"], groups=len(B_WINDOWS), extras=(scale_full,),
                       out_dtypes=(F32, BF16), epilogue=lambda acc, sc: (acc, acc * sc))
    h3 = _mm_aw("mix_b_out", ms, W["b_w_out"], extras=(h2,), epilogue=residual)
    h4, mlp1 = mlp_fwd("1", h3, 1)
    dh4, dh4_b, d_final, loss_part = _final("loss_head", h4, final_norm.reshape(1, -1), tgt)
    g1_like = _W4(None, True, shape=(N_CHIPS, 1, *W["w1_0"].arr.shape[2:]))
    g2_like = _W4(None, False, shape=(N_CHIPS, 1, *W["w2_0"].arr.shape[2:]))

    def exchange(tag, gs):
        zones = [lax.empty((g.shape[0], g.shape[1], g.shape[2] // 2, g.shape[3]), g.dtype) for g in gs]
        send, recv, srcs, zones, tok = _split_start(f"pair_start_{tag}", gs, zones, _pair_plan)
        return (tag, send, recv, srcs, zones), tok

    def reduce(state, after):
        tag, send, recv, srcs, zones = state
        srcs, zones = _split_wait(f"pair_wait_{tag}", srcs, zones, send, recv, after, _pair_whole)
        both = [_pair_sum(f"pair_sum_{tag}_{i}", g, o, place) for i, (g, o) in enumerate(zip(srcs, zones))]
        send, recv, sums, dests, tok = _split_start(f"scatter_start_{tag}", [b[0] for b in both],
                                                    [b[1] for b in both], _scatter_plan)
        return (tag, send, recv, sums, dests), tok

    def relay(state, after):
        tag, send, recv, sums, dests = state
        _, dests = _split_wait(f"scatter_wait_{tag}", sums, dests, send, recv, after, _scatter_whole)
        send, recv, _, dests, tok = _split_start(f"relay_start_{tag}", [], dests, _relay_plan)
        return (tag, send, recv, dests), tok

    def land(state, after):
        tag, send, recv, dests = state
        return _split_wait(f"relay_wait_{tag}", [], dests, send, recv, after, _relay_whole)[1]

    def mlp_bwd(tag, dh, dh_b, saved, layer, deps, pending=None):
        h, hn, act, act_sq = saved
        dpre = _mm_aw(f"mlp{tag}_down_dx", dh_b, W[f"w2_{layer}"], transpose_w=True, extras=(act,),
                      out_dtypes=(BF16,), epilogue=lambda acc, a: (acc * (2.0 * a),), deps=deps)
        scattering, dw_deps = None, ()
        if pending is not None:
            scattering, tok = reduce(pending, dpre)
            dw_deps = (tok,)
        g_w2 = _mm_dw(f"mlp{tag}_down_dw", act_sq, dh_b, g2_like, deps=dw_deps)
        pair_w2, tok = exchange(f"w2_{layer}", [g_w2])
        dhn = _mm_aw(f"mlp{tag}_up_dx", dpre, W[f"w1_{layer}"], transpose_w=True, deps=(tok,))
        g_w1 = _mm_dw(f"mlp{tag}_up_dw", hn, dpre, g1_like)
        pair_w1, tok1 = exchange(f"w1_{layer}", [g_w1])
        scat_w2, tok2 = reduce(pair_w2, dhn)
        dh_in, dh_in_b, d_norm = _rms_bwd(f"mlp{tag}_norm_bwd", h, norm_mlp[layer:layer + 1], dhn, dh)
        return dh_in, dh_in_b, d_norm, pair_w1, scat_w2, (tok1, tok2), scattering

    dh3, dh3_b, d_norm_mlp1, pair_w1_1, scat_w2_1, toks, _ = mlp_bwd("1", dh4, dh4_b, mlp1, 1, ())
    dms = _mm_aw("mix_b_out_dx", dh3_b, W["b_w_out"], transpose_w=True, deps=toks)
    g_b_out = _mm_dw("mix_b_out_dw", ms, dh3_b, W["b_w_out"])
    scat_w1_1, tok = reduce(pair_w1_1, dms)
    dmixed, d_scale = _scale_bwd("mix_b_scale_bwd", dms, mixed, scale_full)
    dpooled = _mm_aw("mix_b_grp_dx", dmixed, W["b_w_grp"], groups=len(B_WINDOWS), transpose_w=True, deps=(tok,))
    g_b_grp = _mm_dw("mix_b_grp_dw", pooled, dmixed, W["b_w_grp"], groups=len(B_WINDOWS))
    dvb = _pool_bwd("mix_b_pool_bwd", dpooled)
    dhn2 = _mm_aw("mix_b_in_dx", dvb, W["b_w_in"], transpose_w=True)
    g_b_in = _mm_dw("mix_b_in_dw", hn2, dvb, W["b_w_in"])
    pair_b, tok = exchange("b", [g_b_out, g_b_grp, g_b_in])
    dh2, dh2_b, d_norm_mix1 = _rms_bwd("mix_b_norm_bwd", h2, norm_mix[1:2], dhn2, dh3)
    dh1, dh1_b, d_norm_mlp0, pair_w1_0, scat_w2_0, toks, scat_b = mlp_bwd("0", dh2, dh2_b, mlp0, 0, (tok,),
                                                                          pending=pair_b)
    dgated = _mm_aw("mix_a_out_dx", dh1_b, W["a_w_out"], transpose_w=True, deps=toks)
    g_a_out = _mm_dw("mix_a_out_dw", gated, dh1_b, W["a_w_out"])
    pair_a_out, tok_a = exchange("a_out", [g_a_out])
    scat_w1_0, tok = reduce(pair_w1_0, dgated)
    early = [relay(state, dgated) for state in (scat_w2_1, scat_w1_1, scat_b)]
    dzpre, d_w_s, d_b_col, d_ln_g, d_ln_b = _gate_bwd("mix_a_gate_bwd", zpre, dgated, a_ln_g, a_ln_b, a_w_s[0], b_col)
    dhn0 = _mm_aw("mix_a_in_dx", dzpre, W["a_w_in"], transpose_w=True, deps=(tok, tok_a, *[t for _, t in early]))
    scat_a_out, tok = reduce(pair_a_out, dhn0)
    g_a_in = _mm_dw("mix_a_in_dw", hn0, dzpre, W["a_w_in"], deps=(tok,))
    pair_a_in, tok = exchange("a_in", [g_a_in])
    dx, _, d_norm_mix0 = _rms_bwd("mix_a_norm_bwd", x2, norm_mix[0:1], dhn0, dh1)
    scat_a_in, _ = reduce(pair_a_in, dx)

    small = dict(a_ln_g=(a_ln_g, m_a_ln_g, v_a_ln_g), a_ln_b=(a_ln_b, m_a_ln_b, v_a_ln_b),
                 a_w_s=(a_w_s, m_a_w_s, v_a_w_s), a_b_s=(a_b_s, m_a_b_s, v_a_b_s),
                 b_scale=(b_scale, m_b_scale, v_b_scale), norm_mix=(norm_mix, m_norm_mix, v_norm_mix),
                 norm_mlp=(norm_mlp, m_norm_mlp, v_norm_mlp), final_norm=(final_norm, m_final_norm, v_final_norm))
    small_names = list(small)
    local = dict(a_ln_g=d_ln_g, a_ln_b=d_ln_b, a_w_s=d_w_s[None], a_b_s=d_b_col.T[None], b_scale=d_scale,
                 norm_mix=jnp.concatenate([d_norm_mix0, d_norm_mix1], axis=0),
                 norm_mlp=jnp.concatenate([d_norm_mlp0, d_norm_mlp1], axis=0), final_norm=d_final.reshape(-1))
    device = (4 * xi + 2 * yi + ci).astype(jnp.int32).reshape(1)
    slots = _own_of_eight("small_own", _pack([local[k] for k in small_names]), device)
    small_send, small_recv, _, (slots,), small_tok = _split_start("small_start", [], [slots], _all_plan)

    moments = dict(a_w_in=(m_a_w_in, v_a_w_in), a_w_out=(m_a_w_out, v_a_w_out), b_w_in=(m_b_w_in, v_b_w_in),
                   b_w_grp=(m_b_w_grp, v_b_w_grp), b_w_out=(m_b_w_out, v_b_w_out),
                   mlp_w1=(m_mlp_w1, v_mlp_w1), mlp_w2=(m_mlp_w2, v_mlp_w2))
    weights = dict(a_w_in=a_w_in, a_w_out=a_w_out, b_w_in=b_w_in, b_w_grp=b_w_grp, b_w_out=b_w_out,
                   mlp_w1=mlp_w1, mlp_w2=mlp_w2)
    landing = [(scat_w2_1, [("mlp_w2", 1)]), (scat_w1_1, [("mlp_w1", 1)]),
               (scat_b, [("b_w_out", 0), ("b_w_grp", 0), ("b_w_in", 0)]),
               (scat_w2_0, [("mlp_w2", 0)]), (scat_w1_0, [("mlp_w1", 0)]),
               (scat_a_out, [("a_w_out", 0)]), (scat_a_in, [("a_w_in", 0)])]
    results, after = {}, dx
    relays = list(early)
    for i, (_, members) in enumerate(landing):
        deps = (small_tok,) if i == 0 else ()
        if len(relays) == i + 1 < len(landing):
            relays.append(relay(landing[i + 1][0], after))
            deps = (relays[-1][1],)
        for (k, layer), parts in zip(members, land(relays[i][0], relays[-1][1] if deps else after)):
            shard_shape = (-1, *parts.shape[2:])
            results[k] = _adam_shard(f"adam_{k}_{layer}", weights[k].reshape(shard_shape),
                                     moments[k][0].reshape(shard_shape), moments[k][1].reshape(shard_shape),
                                     parts, layer=layer, prev=results.get(k), deps=deps)
            after = results[k][1]
    grad_out, delta_out, m_out, v_out = {}, {}, {}, {}
    for k, res in results.items():
        grad_out[k], delta_out[k], m_out[k], v_out[k] = [r.reshape(weights[k].shape) for r in res]

    _, (slots,) = _split_wait("small_wait", [], [slots], small_send, small_recv, after, _all_whole)
    reduced = _sum_of_eight("small_sum", slots)
    small_grads = dict(zip(small_names, _unpack(reduced, [local[k].shape for k in small_names])))
    shard_w = b_scale.shape[1]
    small_grads["b_scale"] = lax.dynamic_slice_in_dim(small_grads["b_scale"], chip * shard_w, shard_w, axis=1)
    small_grads = {k: small_grads[k].reshape(small[k][0].shape) for k in small_names}
    packed = [_pack([small[k][i] for k in small_names]) for i in range(3)]
    res = _adam_packed("adam_small", packed[0], _pack([small_grads[k] for k in small_names]), packed[1], packed[2])
    shapes = [small[k][0].shape for k in small_names]
    for k, d, nm, nv in zip(small_names, *[_unpack(r, shapes) for r in res]):
        grad_out[k], delta_out[k], m_out[k], v_out[k] = small_grads[k], d, nm, nv

    loss = lax.psum(loss_part[0, 0], ("x", "y", "c"))
    order = ["a_w_in", "a_ln_g", "a_ln_b", "a_w_s", "a_b_s", "a_w_out", "b_w_in", "b_w_grp", "b_scale", "b_w_out",
             "norm_mix", "norm_mlp", "mlp_w1", "mlp_w2", "final_norm"]
    return (loss, dx[None], *[grad_out[k] for k in order], *[delta_out[k] for k in order],
            *[m_out[k] for k in order], *[v_out[k] for k in order])
```

```python
import math

import jax
import jax.numpy as jnp
from jax import lax
from jax.experimental import pallas as pl
from jax.experimental.pallas import tpu as pltpu

F32 = jnp.float32
BF16 = jnp.bfloat16
MESH = pl.DeviceIdType.MESH

EPS = 1e-6
B_WINDOWS = (2, 4, 8, 16)
ADAM_LR = 0.001
ADAM_B1 = 0.9
ADAM_B2 = 0.999
ADAM_EPS = 1e-08
ADAM_WD = 0.01
ADAM_STEP = 10

N_CHIPS = 4
N_DEV = 8
LANES = 128
PACK_ELEMS = 8 * LANES
VMEM_LIMIT = 56 * 1024 * 1024
ROW_TILE = 512
MM_TM, MM_TN, MM_TK = 1024, 1024, 2048


_ANY = pl.BlockSpec(memory_space=pl.ANY)


def _tile(dim, pref):
    t = min(dim, pref)
    while dim % t:
        t //= 2
    return t


def _params(*sem):
    return pltpu.CompilerParams(dimension_semantics=sem, vmem_limit_bytes=VMEM_LIMIT)


class _W4:
    def __init__(self, arr, col_sharded, shape=None):
        self.arr = arr
        self.nj, self.nl, self.r, self.c = arr.shape if shape is None else shape
        self.col = col_sharded
        self.rows = self.r if col_sharded else self.nj * self.r
        self.cols = self.nj * self.c if col_sharded else self.c

    def tile_rows(self, pref):
        return _tile(self.r, pref)

    def tile_cols(self, pref):
        return _tile(self.c, pref)

    def index(self, layer, rb, cb, tr, tc):
        if self.col:
            n = self.c // tc
            return (cb // n, layer, rb, cb % n)
        n = self.r // tr
        return (rb // n, layer, rb % n, cb)


def _mm_aw(name, a, w, *, layer=0, groups=1, transpose_w=False, extras=(), out_dtypes=(F32,), epilogue=None,
           deps=()):
    s, ka_total = a.shape
    kdim, ndim = (w.cols, w.rows) if transpose_w else (w.rows, w.cols)
    assert ka_total == groups * kdim, (name, a.shape, kdim, groups)
    span = not w.col and ((not transpose_w and kdim <= MM_TK) or (transpose_w and groups > 1))
    if span and transpose_w:
        tm, tn, tk = _tile(s, 2048), ndim, w.tile_cols(MM_TK)
    else:
        tk = kdim if span else (w.tile_cols(MM_TK) if transpose_w else w.tile_rows(MM_TK))
        tm, tn_pref = (_tile(s, 2048), 512) if tk == kdim else (_tile(s, MM_TM), MM_TN)
        tn = w.tile_rows(tn_pref) if transpose_w else w.tile_cols(tn_pref)
    nk, nn = kdim // tk, ndim // tn

    def lay(g):
        return g if groups > 1 else layer

    a_spec = pl.BlockSpec((tm, tk), lambda g, i, n, k: (i, g * nk + k))
    if span and transpose_w:
        w_spec = pl.BlockSpec((w.nj, None, w.r, tk), lambda g, i, n, k: (0, lay(g), 0, k))
    elif span:
        w_spec = pl.BlockSpec((w.nj, None, w.r, tn), lambda g, i, n, k: (0, lay(g), 0, n))
    elif transpose_w:
        w_spec = pl.BlockSpec((None, None, tn, tk), lambda g, i, n, k: w.index(lay(g), n, k, tn, tk))
    else:
        w_spec = pl.BlockSpec((None, None, tk, tn), lambda g, i, n, k: w.index(lay(g), k, n, tk, tn))
    ex_specs = []
    for e in extras:
        assert e.shape[1] == groups * ndim and e.shape[0] in (1, s), (name, e.shape)
        if e.shape[0] == 1:
            ex_specs.append(pl.BlockSpec((1, tn), lambda g, i, n, k: (0, g * nn + n)))
        else:
            ex_specs.append(pl.BlockSpec((tm, tn), lambda g, i, n, k: (i, g * nn + n)))
    out_spec = pl.BlockSpec((tm, tn), lambda g, i, n, k: (i, g * nn + n))
    n_ex, n_out, n_dep = len(extras), len(out_dtypes), len(deps)

    def body(a_ref, w_ref, *rest):
        ex, outs = rest[:n_ex], rest[n_ex + n_dep:n_ex + n_dep + n_out]
        av = a_ref[...]
        if av.dtype != BF16:
            av = av.astype(BF16)
        wv = w_ref[...]
        if span:
            wv = wv.reshape((tn, tk) if transpose_w else (tk, tn))
        if transpose_w:
            prod = lax.dot_general(av, wv, (((1,), (1,)), ((), ())), preferred_element_type=F32)
        else:
            prod = jnp.dot(av, wv, preferred_element_type=F32)

        def finish(total):
            vals = (total,) if epilogue is None else epilogue(total, *[e[...] for e in ex])
            for o, v in zip(outs, vals):
                o[...] = v.astype(o.dtype)

        if nk == 1:
            finish(prod)
            return
        acc, k = rest[-1], pl.program_id(3)

        @pl.when(k == 0)
        def _():
            acc[...] = prod

        @pl.when(k > 0)
        def _():
            acc[...] += prod

        @pl.when(k == nk - 1)
        def _():
            finish(acc[...])

    outs = pl.pallas_call(
        body, name=name, grid=(groups, s // tm, nn, nk),
        in_specs=[a_spec, w_spec, *ex_specs] + [_ANY] * n_dep, out_specs=[out_spec] * n_out,
        out_shape=[jax.ShapeDtypeStruct((s, groups * ndim), dt) for dt in out_dtypes],
        scratch_shapes=[pltpu.VMEM((tm, tn), F32)] if nk > 1 else [],
        compiler_params=_params("parallel", "parallel", "parallel", "arbitrary"),
    )(a, w.arr, *extras, *deps)
    return outs[0] if n_out == 1 else outs


def _mm_dw(name, a, b, like, *, layer=0, groups=1, deps=()):
    s, ka_total = a.shape
    rows, cols = ka_total // groups, b.shape[1] // groups
    assert (rows, cols) == (like.rows, like.cols) and b.shape[0] == s, (name, a.shape, b.shape)
    span = groups > 1 and not like.col
    tm, tn, tk = rows if span else like.tile_rows(MM_TM), like.tile_cols(MM_TN), _tile(s, MM_TK)
    nr, nc, nk = rows // tm, cols // tn, s // tk
    assert nk == 1 or not span

    def lay(g):
        return g if groups > 1 else layer

    in_specs = [pl.BlockSpec((tk, tm), lambda g, n, i, k: (k, g * nr + i)),
                pl.BlockSpec((tk, tn), lambda g, n, i, k: (k, g * nc + n))]
    in_specs += [_ANY] * len(deps)

    def body(a_ref, b_ref, *rest):
        av, bv = a_ref[...], b_ref[...]
        if av.dtype != BF16:
            av = av.astype(BF16)
        if bv.dtype != BF16:
            bv = bv.astype(BF16)
        prod = lax.dot_general(av, bv, (((0,), (0,)), ((), ())), preferred_element_type=F32)
        if nk == 1:
            rest[-1][...] = prod.astype(BF16).reshape(rest[-1].shape)
            return
        o_ref, acc, k = rest[-2], rest[-1], pl.program_id(3)

        @pl.when(k == 0)
        def _():
            acc[...] = prod

        @pl.when(k > 0)
        def _():
            acc[...] += prod

        @pl.when(k == nk - 1)
        def _():
            o_ref[...] = acc[...].astype(BF16)

    if span:
        out_spec = pl.BlockSpec((like.nj, None, like.r, tn), lambda g, n, i, k: (0, g, 0, n))
    else:
        out_spec = pl.BlockSpec((None, None, tm, tn), lambda g, n, i, k: like.index(lay(g), i, n, tm, tn))
    return pl.pallas_call(
        body, name=name, grid=(groups, nc, nr, nk), in_specs=in_specs, out_specs=out_spec,
        out_shape=jax.ShapeDtypeStruct((like.nj, like.nl, like.r, like.c), BF16),
        scratch_shapes=[pltpu.VMEM((tm, tn), F32)] if nk > 1 else [],
        compiler_params=_params("parallel", "parallel", "parallel", "arbitrary"),
    )(a, b, *deps)


def _row_spec(tr, d):
    return pl.BlockSpec((tr, d), lambda i: (i, 0))


def _vec_spec(d):
    return pl.BlockSpec((1, d), lambda i: (0, 0))


def _rms_fwd(name, x, g):
    s, d = x.shape
    tr = _tile(s, ROW_TILE)

    def body(x_ref, g_ref, o_ref):
        xv = x_ref[...]
        r = lax.rsqrt(jnp.mean(xv * xv, axis=-1, keepdims=True) + EPS)
        o_ref[...] = (xv * r * g_ref[...]).astype(BF16)

    return pl.pallas_call(
        body, name=name, grid=(s // tr,), in_specs=[_row_spec(tr, d), _vec_spec(d)],
        out_specs=_row_spec(tr, d), out_shape=jax.ShapeDtypeStruct((s, d), BF16),
        compiler_params=_params("parallel"),
    )(x, g)


def _rms_bwd(name, x, g, dhn, dres):
    s, d = x.shape
    tr = _tile(s, ROW_TILE)

    def body(x_ref, g_ref, dhn_ref, dres_ref, dx_ref, dxb_ref, dg_ref):
        @pl.when(pl.program_id(0) == 0)
        def _():
            dg_ref[...] = jnp.zeros_like(dg_ref)

        xv = x_ref[...]
        r = lax.rsqrt(jnp.mean(xv * xv, axis=-1, keepdims=True) + EPS)
        xh = xv * r
        dy = dhn_ref[...]
        dg_ref[...] += jnp.sum(dy * xh, axis=0, keepdims=True)
        dxh = dy * g_ref[...]
        dx = dres_ref[...] + r * (dxh - xh * jnp.mean(dxh * xh, axis=-1, keepdims=True))
        dx_ref[...] = dx
        dxb_ref[...] = dx.astype(BF16)

    return pl.pallas_call(
        body, name=name, grid=(s // tr,),
        in_specs=[_row_spec(tr, d), _vec_spec(d), _row_spec(tr, d), _row_spec(tr, d)],
        out_specs=[_row_spec(tr, d), _row_spec(tr, d), _vec_spec(d)],
        out_shape=[jax.ShapeDtypeStruct((s, d), F32), jax.ShapeDtypeStruct((s, d), BF16),
                   jax.ShapeDtypeStruct((1, d), F32)],
        compiler_params=_params("arbitrary"),
    )(x, g, dhn, dres)


def _final(name, h, g, tgt):
    s, d = h.shape
    tr = _tile(s, ROW_TILE)

    def body(h_ref, g_ref, t_ref, dh_ref, dhb_ref, dg_ref, loss_ref):
        @pl.when(pl.program_id(0) == 0)
        def _():
            dg_ref[...] = jnp.zeros_like(dg_ref)
            loss_ref[...] = jnp.zeros_like(loss_ref)

        hv = h_ref[...]
        r = lax.rsqrt(jnp.mean(hv * hv, axis=-1, keepdims=True) + EPS)
        xh = hv * r
        gv = g_ref[...]
        err = xh * gv - t_ref[...]
        part = 0.5 * jnp.sum(jnp.mean(err * err, axis=-1, keepdims=True), axis=0, keepdims=True)
        loss_ref[...] += jnp.broadcast_to(part, loss_ref.shape)
        dy = err * (1.0 / d)
        dg_ref[...] += jnp.sum(dy * xh, axis=0, keepdims=True)
        dxh = dy * gv
        dh = r * (dxh - xh * jnp.mean(dxh * xh, axis=-1, keepdims=True))
        dh_ref[...] = dh
        dhb_ref[...] = dh.astype(BF16)

    return pl.pallas_call(
        body, name=name, grid=(s // tr,),
        in_specs=[_row_spec(tr, d), _vec_spec(d), _row_spec(tr, d)],
        out_specs=[_row_spec(tr, d), _row_spec(tr, d), _vec_spec(d), _vec_spec(LANES)],
        out_shape=[jax.ShapeDtypeStruct((s, d), F32), jax.ShapeDtypeStruct((s, d), BF16),
                   jax.ShapeDtypeStruct((1, d), F32), jax.ShapeDtypeStruct((1, LANES), F32)],
        compiler_params=_params("arbitrary"),
    )(h, g, tgt)


_SQRT_HALF = 1.0 / math.sqrt(2.0)
_INV_SQRT_2PI = 1.0 / math.sqrt(2.0 * math.pi)


def _gelu(x):
    return x * (lax.erf(x * _SQRT_HALF) + 1.0) * 0.5


def _gelu_grad(x):
    return 0.5 * (lax.erf(x * _SQRT_HALF) + 1.0) + x * jnp.exp(-0.5 * x * x) * _INV_SQRT_2PI


def _causal(chunk):
    t = lax.broadcasted_iota(jnp.int32, (chunk, chunk), 0)
    sidx = lax.broadcasted_iota(jnp.int32, (chunk, chunk), 1)
    return sidx <= t


def _layernorm_parts(v, g, b):
    mu = jnp.mean(v, axis=-1, keepdims=True)
    vc = v - mu
    rs = lax.rsqrt(jnp.mean(vc * vc, axis=-1, keepdims=True) + EPS)
    vhat = vc * rs
    return vhat, rs, vhat * g + b


def _gate_fwd(name, zpre, ln_g, ln_b, w_s, b_col):
    s, aw2 = zpre.shape
    aw = aw2 // 2
    ng, chunk, _ = w_s.shape
    dh = aw // ng

    def body(z_ref, g_ref, b_ref, ws_ref, bc_ref, o_ref):
        u = _gelu(z_ref[:, :aw])
        v = _gelu(z_ref[:, aw:])
        _, _, vln = _layernorm_parts(v, g_ref[...], b_ref[...])
        mask = _causal(chunk)
        for gi in range(ng):
            sl = slice(gi * dh, (gi + 1) * dh)
            wm = jnp.where(mask, ws_ref[gi], 0.0).astype(BF16)
            sg = jnp.dot(wm, vln[:, sl].astype(BF16), preferred_element_type=F32) + bc_ref[:, gi:gi + 1]
            o_ref[:, sl] = (u[:, sl] * sg).astype(BF16)

    return pl.pallas_call(
        body, name=name, grid=(s // chunk,),
        in_specs=[_row_spec(chunk, aw2), _vec_spec(aw), _vec_spec(aw),
                  pl.BlockSpec((ng, chunk, chunk), lambda i: (0, 0, 0)),
                  pl.BlockSpec((chunk, ng), lambda i: (0, 0))],
        out_specs=_row_spec(chunk, aw), out_shape=jax.ShapeDtypeStruct((s, aw), BF16),
        compiler_params=_params("parallel"),
    )(zpre, ln_g, ln_b, w_s, b_col)


def _gate_bwd(name, zpre, dgated, ln_g, ln_b, w_s, b_col):
    s, aw2 = zpre.shape
    aw = aw2 // 2
    ng, chunk, _ = w_s.shape
    dh = aw // ng

    def body(z_ref, dgt_ref, g_ref, b_ref, ws_ref, bc_ref, dz_ref, dws_ref, dbc_ref, dlg_ref, dlb_ref,
             du_scr, dvln_scr):
        @pl.when(pl.program_id(0) == 0)
        def _():
            dws_ref[...] = jnp.zeros_like(dws_ref)
            dbc_ref[...] = jnp.zeros_like(dbc_ref)
            dlg_ref[...] = jnp.zeros_like(dlg_ref)
            dlb_ref[...] = jnp.zeros_like(dlb_ref)

        zu = z_ref[:, :aw]
        zv = z_ref[:, aw:]
        u = _gelu(zu)
        lg = g_ref[...]
        vhat, rs, vln = _layernorm_parts(_gelu(zv), lg, b_ref[...])
        mask = _causal(chunk)
        for gi in range(ng):
            sl = slice(gi * dh, (gi + 1) * dh)
            wm = jnp.where(mask, ws_ref[gi], 0.0).astype(BF16)
            vg = vln[:, sl].astype(BF16)
            sg = jnp.dot(wm, vg, preferred_element_type=F32) + bc_ref[:, gi:gi + 1]
            dgt = dgt_ref[:, sl]
            du_scr[:, sl] = dgt * sg
            ds = dgt * u[:, sl]
            dbc_ref[:, gi:gi + 1] += jnp.sum(ds, axis=-1, keepdims=True)
            dsb = ds.astype(BF16)
            dwm = lax.dot_general(dsb, vg, (((1,), (1,)), ((), ())), preferred_element_type=F32)
            dws_ref[gi] += jnp.where(mask, dwm, 0.0)
            dvln_scr[:, sl] = lax.dot_general(wm, dsb, (((0,), (0,)), ((), ())), preferred_element_type=F32)
        dvln = dvln_scr[...]
        dlb_ref[...] += jnp.sum(dvln, axis=0, keepdims=True)
        dlg_ref[...] += jnp.sum(dvln * vhat, axis=0, keepdims=True)
        dvh = dvln * lg
        dv = rs * (dvh - jnp.mean(dvh, axis=-1, keepdims=True)
                   - vhat * jnp.mean(dvh * vhat, axis=-1, keepdims=True))
        dz_ref[:, :aw] = (du_scr[...] * _gelu_grad(zu)).astype(BF16)
        dz_ref[:, aw:] = (dv * _gelu_grad(zv)).astype(BF16)

    return pl.pallas_call(
        body, name=name, grid=(s // chunk,),
        in_specs=[_row_spec(chunk, aw2), _row_spec(chunk, aw), _vec_spec(aw), _vec_spec(aw),
                  pl.BlockSpec((ng, chunk, chunk), lambda i: (0, 0, 0)),
                  pl.BlockSpec((chunk, ng), lambda i: (0, 0))],
        out_specs=[_row_spec(chunk, aw2), pl.BlockSpec((ng, chunk, chunk), lambda i: (0, 0, 0)),
                   pl.BlockSpec((chunk, ng), lambda i: (0, 0)), _vec_spec(aw), _vec_spec(aw)],
        out_shape=[jax.ShapeDtypeStruct((s, aw2), BF16), jax.ShapeDtypeStruct((ng, chunk, chunk), F32),
                   jax.ShapeDtypeStruct((chunk, ng), F32), jax.ShapeDtypeStruct((1, aw), F32),
                   jax.ShapeDtypeStruct((1, aw), F32)],
        scratch_shapes=[pltpu.VMEM((chunk, aw), F32), pltpu.VMEM((chunk, aw), F32)],
        compiler_params=_params("arbitrary"),
    )(zpre, dgated, ln_g, ln_b, w_s, b_col)


def _pool_select(g, parts):
    out = parts[-1]
    for gi in range(len(parts) - 2, -1, -1):
        out = jnp.where(g == gi, parts[gi], out)
    return out


def _pool_specs(s, bw):
    head = bw // len(B_WINDOWS)
    tc = _tile(head, 256)
    nb = head // tc
    return tc, (len(B_WINDOWS), nb), pl.BlockSpec((s, tc), lambda g, j: (0, g * nb + j))


def _pool_window(g, t):
    w = _pool_select(g, [jnp.full(t.shape, wi, jnp.int32) for wi in B_WINDOWS])
    return jnp.minimum(t + 1, w).astype(F32)


def _pool_fwd(name, vb):
    assert B_WINDOWS == (2, 4, 8, 16)
    s, bw = vb.shape
    tc, grid, spec = _pool_specs(s, bw)

    def body(v_ref, o_ref):
        g = pl.program_id(0)
        v = v_ref[...]
        t = lax.broadcasted_iota(jnp.int32, (s, tc), 0)

        def down(x, k):
            return jnp.where(t >= k, pltpu.roll(x, k, 0), 0.0)

        sums, cur, k = [], v, 1
        for _ in B_WINDOWS:
            cur = cur + down(cur, k)
            sums.append(cur)
            k *= 2
        o_ref[...] = (_pool_select(g, sums) / _pool_window(g, t) - v).astype(BF16)

    return pl.pallas_call(
        body, name=name, grid=grid, in_specs=[spec], out_specs=spec,
        out_shape=jax.ShapeDtypeStruct((s, bw), BF16), compiler_params=_params("parallel", "parallel"),
    )(vb)


def _pool_bwd(name, dpooled):
    s, bw = dpooled.shape
    tc, grid, spec = _pool_specs(s, bw)

    def body(d_ref, o_ref):
        g = pl.program_id(0)
        dp = d_ref[...]
        t = lax.broadcasted_iota(jnp.int32, (s, tc), 0)

        def up(x, k):
            return jnp.where(t < s - k, pltpu.roll(x, s - k, 0), 0.0)

        sums, cur, k = [], dp / _pool_window(g, t), 1
        for _ in B_WINDOWS:
            cur = cur + up(cur, k)
            sums.append(cur)
            k *= 2
        o_ref[...] = (_pool_select(g, sums) - dp).astype(BF16)

    return pl.pallas_call(
        body, name=name, grid=grid, in_specs=[spec], out_specs=spec,
        out_shape=jax.ShapeDtypeStruct((s, bw), BF16), compiler_params=_params("parallel", "parallel"),
    )(dpooled)


def _scale_bwd(name, dms, mixed, scale):
    s, bw = dms.shape
    tr = _tile(s, ROW_TILE)

    def body(d_ref, m_ref, sc_ref, o_ref, ds_ref):
        @pl.when(pl.program_id(0) == 0)
        def _():
            ds_ref[...] = jnp.zeros_like(ds_ref)

        dv = d_ref[...]
        ds_ref[...] += jnp.sum(dv * m_ref[...], axis=0, keepdims=True)
        o_ref[...] = (dv * sc_ref[...]).astype(BF16)

    return pl.pallas_call(
        body, name=name, grid=(s // tr,), in_specs=[_row_spec(tr, bw), _row_spec(tr, bw), _vec_spec(bw)],
        out_specs=[_row_spec(tr, bw), _vec_spec(bw)],
        out_shape=[jax.ShapeDtypeStruct((s, bw), BF16), jax.ShapeDtypeStruct((1, bw), F32)],
        compiler_params=_params("arbitrary"),
    )(dms, mixed, scale)


def _adam_update(w, g, m, v):
    m = ADAM_B1 * m + (1.0 - ADAM_B1) * g
    v = ADAM_B2 * v + (1.0 - ADAM_B2) * (g * g)
    m_hat = m / (1.0 - ADAM_B1 ** ADAM_STEP)
    v_hat = v / (1.0 - ADAM_B2 ** ADAM_STEP)
    delta = -ADAM_LR * (m_hat / (jnp.sqrt(v_hat) + ADAM_EPS) + ADAM_WD * w)
    return delta, m, v


def _adam_shard(name, w, m, v, parts, layer=0, prev=None, deps=()):
    nl, r, c = w.shape
    nj, nlp = parts.shape[:2]
    tr = _tile(r, ROW_TILE // 2)
    spec = pl.BlockSpec((None, tr, c), lambda l, i: (layer + l, i, 0))

    def body(w_ref, m_ref, v_ref, p_ref, *rest):
        g_ref, d_ref, nm_ref, nv_ref = rest[-4:]
        g = p_ref[0].astype(F32)
        for j in range(1, nj):
            g = g + p_ref[j].astype(F32)
        delta, nm, nv = _adam_update(w_ref[...], g, m_ref[...], v_ref[...])
        g_ref[...] = g
        d_ref[...] = delta
        nm_ref[...] = nm
        nv_ref[...] = nv

    prev = () if prev is None else tuple(prev)
    return pl.pallas_call(
        body, name=name, grid=(nlp, r // tr),
        in_specs=[spec, spec, spec, pl.BlockSpec((nj, None, tr, c), lambda l, i: (0, l, i, 0))]
        + [_ANY] * (len(prev) + len(deps)),
        out_specs=[spec] * 4, out_shape=[jax.ShapeDtypeStruct(w.shape, F32)] * 4,
        input_output_aliases={4 + i: i for i in range(len(prev))},
        compiler_params=_params("parallel", "parallel"),
    )(w, m, v, parts, *prev, *deps)


def _adam_packed(name, w, g, m, v):
    rows, lanes = w.shape
    tr = rows
    spec = pl.BlockSpec((tr, lanes), lambda i: (i, 0))

    def body(w_ref, g_ref, m_ref, v_ref, d_ref, nm_ref, nv_ref):
        delta, nm, nv = _adam_update(w_ref[...], g_ref[...], m_ref[...], v_ref[...])
        d_ref[...] = delta
        nm_ref[...] = nm
        nv_ref[...] = nv

    return pl.pallas_call(
        body, name=name, grid=(rows // tr,), in_specs=[spec] * 4, out_specs=[spec] * 3,
        out_shape=[jax.ShapeDtypeStruct(w.shape, F32)] * 3, compiler_params=_params("parallel"),
    )(w, g, m, v)


def _place():
    x, y, c = lax.axis_index("x"), lax.axis_index("y"), lax.axis_index("c")
    chips = [(1 - x, y), (x, 1 - y), (1 - x, 1 - y)]
    return x, y, c, chips


def _remote(src, dst, send_sem, recv_sem, device):
    return pltpu.make_async_remote_copy(src_ref=src, dst_ref=dst, send_sem=send_sem, recv_sem=recv_sem,
                                        device_id=device, device_id_type=MESH)


def _half(ref_rows, cc):
    h = ref_rows // 2
    return pl.ds(cc * h, h)


_HBM = pl.BlockSpec(memory_space=pltpu.HBM)
_SEM = pl.BlockSpec(memory_space=pltpu.SEMAPHORE)
_EFFECT = pltpu.SideEffectType.DATAFLOW_SIDE_EFFECTING


def _in_hbm(arr):
    return pltpu.with_memory_space_constraint(arr, pltpu.HBM)


def _gather_rows(land, cc, part=(0, 1)):
    k, n = part
    h = land.shape[2] // 2
    return pl.ds(cc * h + k * (h // n), h // n)


def _gather_block(land, split, j, cc, part=(0, 1)):
    if split:
        return land.at[j, :, _gather_rows(land, cc, part), :]
    return land.at[j]


def _split_start(name, srcs, lands, plan, deps=(), groups=None):
    ns = len(srcs)
    nl = len(lands) if groups is None else groups
    both = list(srcs) + list(lands)
    nb = len(both)

    def body(*refs):
        s, ld = refs[:ns], refs[ns:nb]
        outs = refs[nb + len(deps):]
        send, recv, token = outs[:nl], outs[nl:2 * nl], outs[-1]
        for a, copies in enumerate(plan(s, ld)):
            for src, dst, peer in copies:
                _remote(src, dst, send[a], recv[a], peer).start()
        token[...] = jnp.zeros_like(token)

    outs = pl.pallas_call(
        body, name=name, in_specs=[_HBM] * nb + [_ANY] * len(deps),
        out_specs=[_SEM] * (2 * nl) + [_HBM] * nb + [pl.BlockSpec(memory_space=pltpu.VMEM)],
        out_shape=[pltpu.SemaphoreType.DMA(())] * (2 * nl) + [pltpu.HBM(b.shape, b.dtype) for b in both]
        + [jax.ShapeDtypeStruct((8, LANES), F32)],
        input_output_aliases={i: 2 * nl + i for i in range(nb)},
        compiler_params=pltpu.CompilerParams(has_side_effects=_EFFECT),
    )(*[_in_hbm(b) for b in both], *deps)
    thru = outs[2 * nl:2 * nl + nb]
    return list(outs[:nl]), list(outs[nl:2 * nl]), list(thru[:ns]), list(thru[ns:]), outs[-1]


def _split_wait(name, srcs, lands, send, recv, after, whole):
    ns, nl = len(srcs), len(send)
    both = list(srcs) + list(lands)
    nb = len(both)

    def body(*refs):
        ld, snd, rcv = refs[ns:nb], refs[nb:nb + nl], refs[nb + nl:nb + 2 * nl]
        x, y, c, _ = _place()
        for a, blk in enumerate(whole(ld)):
            every = _remote(blk, blk, snd[a], rcv[a], (x, y, c))
            every.wait_send()
            every.wait_recv()

    outs = pl.pallas_call(
        body, name=name, in_specs=[_HBM] * nb + [_SEM] * (2 * nl) + [_ANY], out_specs=[_HBM] * nb,
        out_shape=[pltpu.HBM(b.shape, b.dtype) for b in both],
        input_output_aliases={i: i for i in range(nb)},
        compiler_params=pltpu.CompilerParams(has_side_effects=_EFFECT),
    )(*both, *send, *recv, after)
    return list(outs[:ns]), list(outs[ns:])


def _gather_plan(split, parts):
    def plan(srcs, lands):
        x, y, c, chips = _place()
        out = []
        for ld, sp, n in zip(lands, split, parts):
            for k in range(n):
                blk = _gather_block(ld, sp, 2 * x + y, c, (k, n))
                out.append([(blk, blk, (qx, qy, c)) for qx, qy in chips])
        return out
    return plan


def _gather_whole(split, part=(0, 1)):
    def whole(lands):
        _, _, c, _ = _place()
        return [ld.at[pl.ds(0, 3), :, _gather_rows(ld, c, part), :] if sp else ld.at[pl.ds(0, 3)]
                for ld, sp in zip(lands, split)]
    return whole


def _relay_plan(srcs, lands):
    x, y, c, _ = _place()
    out = []
    for ld in lands:
        blocks = [ld.at[j, :, _half(ld.shape[2], c), :] for j in range(N_CHIPS)]
        out.append([(blk, blk, (x, y, 1 - c)) for blk in blocks])
    return out


def _relay_whole(lands):
    _, _, c, _ = _place()
    return [ld.at[:, :, _half(ld.shape[2], c), :] for ld in lands]


def _fill_own(name, shard, dtype, place, layer=None):
    nl, r, c = shard.shape
    first = 0
    if layer is not None:
        nl, first = 1, layer
    tr = _tile(r, 512)

    def body(p_ref, s_ref, o_ref):
        o_ref[...] = s_ref[...].astype(o_ref.dtype)

    return pl.pallas_call(
        body, name=name,
        grid_spec=pltpu.PrefetchScalarGridSpec(
            num_scalar_prefetch=1, grid=(nl, r // tr),
            in_specs=[pl.BlockSpec((None, tr, c), lambda l, i, p: (first + l, i, 0))],
            out_specs=pl.BlockSpec((None, None, tr, c), lambda l, i, p: (p[0], l, i, 0))),
        out_shape=jax.ShapeDtypeStruct((N_CHIPS, nl, r, c), dtype),
        compiler_params=_params("parallel", "parallel"),
    )(place, shard)


def _gather_finish(name, lands, part=(0, 1)):
    n = len(lands)

    def body(*refs):
        outs = refs[n:2 * n]
        fsend, frecv = refs[2 * n:]
        x, y, c, chips = _place()
        sib = (x, y, 1 - c)

        def relay(a, qi, cc):
            qx, qy = chips[qi]
            blk = _gather_block(outs[a], True, 2 * qx + qy, cc, part)
            return _remote(blk, blk, fsend.at[a, qi], frecv.at[a, qi], sib)

        relays = [relay(a, qi, c) for a in range(n) for qi in range(3)]
        for cp in relays:
            cp.start()
        for a in range(n):
            for qi in range(3):
                relay(a, qi, 1 - c).wait_recv()
        for cp in relays:
            cp.wait_send()

    outs = pl.pallas_call(
        body, name=name, in_specs=[_ANY] * n, out_specs=[_ANY] * n,
        out_shape=[jax.ShapeDtypeStruct(ld.shape, ld.dtype) for ld in lands],
        input_output_aliases={i: i for i in range(n)},
        scratch_shapes=[pltpu.SemaphoreType.DMA((n, 3))] * 2,
    )(*lands)
    return list(outs)


def _pair_plan(srcs, lands):
    x, y, c, _ = _place()
    return [[(s.at[:, :, _half(s.shape[2], 1 - c), :], ld, (x, y, 1 - c))] for s, ld in zip(srcs, lands)]


def _pair_whole(lands):
    return list(lands)


def _scatter_plan(srcs, lands):
    x, y, c, chips = _place()
    return [[(s.at[2 * qx + qy], ld.at[2 * x + y, :, _half(ld.shape[2], c), :], (qx, qy, c)) for qx, qy in chips]
            for s, ld in zip(srcs, lands)]


def _scatter_whole(lands):
    _, _, c, _ = _place()
    return [ld.at[pl.ds(0, 3), :, _half(ld.shape[2], c), :] for ld in lands]


def _pair_sum(name, grad, other, place):
    nj, nl, r, c = grad.shape
    h = r // 2
    tr = _tile(h, 2 * ROW_TILE)
    nb = h // tr

    def body(p_ref, g_ref, o_ref, q_ref, d_ref):
        q = (g_ref[...].astype(F32) + o_ref[...].astype(F32)).astype(BF16)
        q_ref[...] = q

        @pl.when(pl.program_id(2) == p_ref[0])
        def _():
            d_ref[...] = q

    blk = (None, None, tr, c)
    return pl.pallas_call(
        body, name=name,
        grid_spec=pltpu.PrefetchScalarGridSpec(
            num_scalar_prefetch=1, grid=(nl, nb, nj),
            in_specs=[pl.BlockSpec(blk, lambda l, i, j, p: (j, l, p[1] * nb + i, 0)),
                      pl.BlockSpec(blk, lambda l, i, j, p: (j, l, i, 0))],
            out_specs=[pl.BlockSpec(blk, lambda l, i, j, p: (j, l, i, 0)),
                       pl.BlockSpec(blk, lambda l, i, j, p: (p[0], l, p[1] * nb + i, 0))]),
        out_shape=[jax.ShapeDtypeStruct((nj, nl, h, c), BF16), jax.ShapeDtypeStruct((nj, nl, r, c), BF16)],
        compiler_params=_params("parallel", "parallel", "arbitrary"),
    )(place, grad, other)


def _own_of_eight(name, packed, device):
    rows, lanes = packed.shape
    tr = rows

    def body(d_ref, s_ref, o_ref):
        o_ref[...] = s_ref[...]

    return pl.pallas_call(
        body, name=name,
        grid_spec=pltpu.PrefetchScalarGridSpec(
            num_scalar_prefetch=1, grid=(rows // tr,),
            in_specs=[pl.BlockSpec((tr, lanes), lambda i, d: (i, 0))],
            out_specs=pl.BlockSpec((None, tr, lanes), lambda i, d: (d[0], i, 0))),
        out_shape=jax.ShapeDtypeStruct((N_DEV, rows, lanes), packed.dtype),
        compiler_params=_params("parallel"),
    )(device, packed)


def _all_plan(srcs, lands):
    x, y, c, _ = _place()
    (ld,) = lands
    blk = ld.at[4 * x + 2 * y + c]
    flips = [(a, b, d) for a in (0, 1) for b in (0, 1) for d in (0, 1) if a + b + d]
    return [[(blk, blk, (x + a - 2 * a * x, y + b - 2 * b * y, c + d - 2 * d * c)) for a, b, d in flips]]


def _all_whole(lands):
    return [lands[0].at[pl.ds(0, N_DEV - 1)]]


def _sum_of_eight(name, slots):
    n, rows, lanes = slots.shape
    tr = rows

    def body(s_ref, o_ref):
        total = s_ref[0]
        for d in range(1, n):
            total = total + s_ref[d]
        o_ref[...] = total

    return pl.pallas_call(
        body, name=name, grid=(rows // tr,),
        in_specs=[pl.BlockSpec((n, tr, lanes), lambda i: (0, i, 0))],
        out_specs=pl.BlockSpec((tr, lanes), lambda i: (i, 0)),
        out_shape=jax.ShapeDtypeStruct((rows, lanes), F32), compiler_params=_params("parallel"),
    )(slots)


def _pack(parts):
    rows = []
    for p in parts:
        flat = p.reshape(-1)
        pad = (-flat.shape[0]) % PACK_ELEMS
        rows.append(jnp.pad(flat, (0, pad)).reshape(-1, LANES))
    return jnp.concatenate(rows, axis=0)


def _unpack(packed, shapes):
    out, row = [], 0
    for sh in shapes:
        size = math.prod(sh)
        nrows = -(-size // PACK_ELEMS) * (PACK_ELEMS // LANES)
        out.append(packed[row:row + nrows].reshape(-1)[:size].reshape(sh))
        row += nrows
    return out


def kernel(x, a_w_in, a_ln_g, a_ln_b, a_w_s, a_b_s, a_w_out, b_w_in, b_w_grp, b_scale, b_w_out, norm_mix, norm_mlp, mlp_w1, mlp_w2, final_norm, loss_target, m_a_w_in, m_a_ln_g, m_a_ln_b, m_a_w_s, m_a_b_s, m_a_w_out, m_b_w_in, m_b_w_grp, m_b_scale, m_b_w_out, m_norm_mix, m_norm_mlp, m_mlp_w1, m_mlp_w2, m_final_norm, v_a_w_in, v_a_ln_g, v_a_ln_b, v_a_w_s, v_a_b_s, v_a_w_out, v_b_w_in, v_b_w_grp, v_b_scale, v_b_w_out, v_norm_mix, v_norm_mlp, v_mlp_w1, v_mlp_w2, v_final_norm):
    xi, yi, ci = lax.axis_index("x"), lax.axis_index("y"), lax.axis_index("c")
    chip = 2 * xi + yi
    place = jnp.stack([chip, ci]).astype(jnp.int32)
    x2, tgt = x[0], loss_target[0]
    bw = b_scale.shape[1] * N_CHIPS

    units = dict(a_w_in=(a_w_in, None), a_w_out=(a_w_out, None), w1_0=(mlp_w1, 0), w2_0=(mlp_w2, 0),
                 b_scale=(b_scale.reshape(1, 1, -1), None), b_w_in=(b_w_in, None), b_w_grp=(b_w_grp[0], None),
                 b_w_out=(b_w_out, None), w1_1=(mlp_w1, 1), w2_1=(mlp_w2, 1))
    col_sharded = dict(a_w_in=True, a_w_out=False, b_w_in=False, b_w_grp=False, b_w_out=False,
                       w1_0=True, w2_0=False, w1_1=True, w2_1=False)
    in_flight, W = {}, {}

    def launch(tag, keys, deps):
        sp = [k != "b_scale" for k in keys]
        parts = [pieces.get(k, 1) for k in keys]
        zones = [_fill_own(f"gather_own_{k}", units[k][0], BF16 if s else F32, place, layer=units[k][1])
                 for k, s in zip(keys, sp)]
        send, recv, _, zones, tok = _split_start(f"gather_start_{tag}", [], zones, _gather_plan(sp, parts), deps,
                                                 groups=sum(parts))
        first = 0
        for k, z, s, n in zip(keys, zones, sp, parts):
            in_flight[k] = (send[first:first + n], recv[first:first + n], z, s)
            first += n
        return tok

    def arrive(keys, after):
        send, recv, zones, sp = zip(*[in_flight[k] for k in keys])
        n = len(send[0])
        if n > 1:
            (key,), zones = keys, list(zones)
            for k in range(n):
                _, zones = _split_wait(f"gather_wait_{key}_{k}", [], zones, [send[0][k]], [recv[0][k]], after,
                                       _gather_whole(sp, (k, n)))
                zones = _gather_finish(f"gather_finish_{key}_{k}", zones, (k, n))
            W[key] = _W4(zones[0], col_sharded[key])
            return
        _, zones = _split_wait(f"gather_wait_{keys[0]}", [], zones, [s[0] for s in send], [r[0] for r in recv],
                               after, _gather_whole(sp))
        relayed = iter(_gather_finish(f"gather_finish_{keys[0]}", [z for z, s in zip(zones, sp) if s]))
        for k, z, s in zip(keys, zones, sp):
            full = next(relayed) if s else z
            W[k] = _W4(full, col_sharded[k]) if k in col_sharded else full

    pieces = dict(a_w_in=2, w1_0=4, w2_0=4, w1_1=4, w2_1=4)

    token = launch("first", ["a_w_in", "a_w_out"], ())
    token = launch("rest", ["w1_0", "w2_0", "b_scale", "b_w_in", "b_w_grp", "b_w_out", "w1_1", "w2_1"], (token,))

    b_col = a_b_s[0].T

    def residual(acc, res):
        return (res + acc,)

    def sq_relu(acc):
        act = jnp.maximum(acc, 0.0)
        return act, act * act

    def mlp_fwd(tag, h, layer):
        hn = _rms_fwd(f"mlp{tag}_norm", h, norm_mlp[layer:layer + 1])
        arrive([f"w1_{layer}"], hn)
        act, act_sq = _mm_aw(f"mlp{tag}_up", hn, W[f"w1_{layer}"], out_dtypes=(BF16, BF16), epilogue=sq_relu)
        arrive([f"w2_{layer}"], act_sq)
        out = _mm_aw(f"mlp{tag}_down", act_sq, W[f"w2_{layer}"], extras=(h,), epilogue=residual)
        return out, (h, hn, act, act_sq)

    hn0 = _rms_fwd("mix_a_norm", x2, norm_mix[0:1])
    arrive(["a_w_in"], token)
    zpre = _mm_aw("mix_a_in", hn0, W["a_w_in"])
    gated = _gate_fwd("mix_a_gate", zpre, a_ln_g, a_ln_b, a_w_s[0], b_col)
    arrive(["a_w_out"], gated)
    h1 = _mm_aw("mix_a_out", gated, W["a_w_out"], extras=(x2,), epilogue=residual)
    h2, mlp0 = mlp_fwd("0", h1, 0)
    hn2 = _rms_fwd("mix_b_norm", h2, norm_mix[1:2])
    arrive(["b_scale", "b_w_in"], hn2)
    scale_full = W["b_scale"].reshape(1, bw)
    vb = _mm_aw("mix_b_in", hn2, W["b_w_in"])
    pooled = _pool_fwd("mix_b_pool", vb)
    arrive(["b_w_grp", "b_w_out"], pooled)
    mixed, ms = _mm_aw("mix_b_grp", pooled, W["b_w_grp"], groups=len(B_WINDOWS), extras=(scale_full,),
                       out_dtypes=(F32, BF16), epilogue=lambda acc, sc: (acc, acc * sc))
    h3 = _mm_aw("mix_b_out", ms, W["b_w_out"], extras=(h2,), epilogue=residual)
    h4, mlp1 = mlp_fwd("1", h3, 1)
    dh4, dh4_b, d_final, loss_part = _final("loss_head", h4, final_norm.reshape(1, -1), tgt)
    g1_like = _W4(None, True, shape=(N_CHIPS, 1, *W["w1_0"].arr.shape[2:]))
    g2_like = _W4(None, False, shape=(N_CHIPS, 1, *W["w2_0"].arr.shape[2:]))

    def exchange(tag, gs):
        zones = [lax.empty((g.shape[0], g.shape[1], g.shape[2] // 2, g.shape[3]), g.dtype) for g in gs]
        send, recv, srcs, zones, tok = _split_start(f"pair_start_{tag}", gs, zones, _pair_plan)
        return (tag, send, recv, srcs, zones), tok

    def reduce(state, after):
        tag, send, recv, srcs, zones = state
        srcs, zones = _split_wait(f"pair_wait_{tag}", srcs, zones, send, recv, after, _pair_whole)
        both = [_pair_sum(f"pair_sum_{tag}_{i}", g, o, place) for i, (g, o) in enumerate(zip(srcs, zones))]
        send, recv, sums, dests, tok = _split_start(f"scatter_start_{tag}", [b[0] for b in both],
                                                    [b[1] for b in both], _scatter_plan)
        return (tag, send, recv, sums, dests), tok

    def relay(state, after):
        tag, send, recv, sums, dests = state
        _, dests = _split_wait(f"scatter_wait_{tag}", sums, dests, send, recv, after, _scatter_whole)
        send, recv, _, dests, tok = _split_start(f"relay_start_{tag}", [], dests, _relay_plan)
        return (tag, send, recv, dests), tok

    def land(state, after):
        tag, send, recv, dests = state
        return _split_wait(f"relay_wait_{tag}", [], dests, send, recv, after, _relay_whole)[1]

    def mlp_bwd(tag, dh, dh_b, saved, layer, deps, pending=None):
        h, hn, act, act_sq = saved
        dpre = _mm_aw(f"mlp{tag}_down_dx", dh_b, W[f"w2_{layer}"], transpose_w=True, extras=(act,),
                      out_dtypes=(BF16,), epilogue=lambda acc, a: (acc * (2.0 * a),), deps=deps)
        scattering, dw_deps = None, ()
        if pending is not None:
            scattering, tok = reduce(pending, dpre)
            dw_deps = (tok,)
        g_w2 = _mm_dw(f"mlp{tag}_down_dw", act_sq, dh_b, g2_like, deps=dw_deps)
        pair_w2, tok = exchange(f"w2_{layer}", [g_w2])
        dhn = _mm_aw(f"mlp{tag}_up_dx", dpre, W[f"w1_{layer}"], transpose_w=True, deps=(tok,))
        g_w1 = _mm_dw(f"mlp{tag}_up_dw", hn, dpre, g1_like)
        pair_w1, tok1 = exchange(f"w1_{layer}", [g_w1])
        scat_w2, tok2 = reduce(pair_w2, dhn)
        dh_in, dh_in_b, d_norm = _rms_bwd(f"mlp{tag}_norm_bwd", h, norm_mlp[layer:layer + 1], dhn, dh)
        return dh_in, dh_in_b, d_norm, pair_w1, scat_w2, (tok1, tok2), scattering

    dh3, dh3_b, d_norm_mlp1, pair_w1_1, scat_w2_1, toks, _ = mlp_bwd("1", dh4, dh4_b, mlp1, 1, ())
    dms = _mm_aw("mix_b_out_dx", dh3_b, W["b_w_out"], transpose_w=True, deps=toks)
    g_b_out = _mm_dw("mix_b_out_dw", ms, dh3_b, W["b_w_out"])
    scat_w1_1, tok = reduce(pair_w1_1, dms)
    dmixed, d_scale = _scale_bwd("mix_b_scale_bwd", dms, mixed, scale_full)
    dpooled = _mm_aw("mix_b_grp_dx", dmixed, W["b_w_grp"], groups=len(B_WINDOWS), transpose_w=True, deps=(tok,))
    g_b_grp = _mm_dw("mix_b_grp_dw", pooled, dmixed, W["b_w_grp"], groups=len(B_WINDOWS))
    dvb = _pool_bwd("mix_b_pool_bwd", dpooled)
    dhn2 = _mm_aw("mix_b_in_dx", dvb, W["b_w_in"], transpose_w=True)
    g_b_in = _mm_dw("mix_b_in_dw", hn2, dvb, W["b_w_in"])
    pair_b, tok = exchange("b", [g_b_out, g_b_grp, g_b_in])
    dh2, dh2_b, d_norm_mix1 = _rms_bwd("mix_b_norm_bwd", h2, norm_mix[1:2], dhn2, dh3)
    dh1, dh1_b, d_norm_mlp0, pair_w1_0, scat_w2_0, toks, scat_b = mlp_bwd("0", dh2, dh2_b, mlp0, 0, (tok,),
                                                                          pending=pair_b)
    dgated = _mm_aw("mix_a_out_dx", dh1_b, W["a_w_out"], transpose_w=True, deps=toks)
    g_a_out = _mm_dw("mix_a_out_dw", gated, dh1_b, W["a_w_out"])
    pair_a_out, tok_a = exchange("a_out", [g_a_out])
    scat_w1_0, tok = reduce(pair_w1_0, dgated)
    early = [relay(state, dgated) for state in (scat_w2_1, scat_w1_1, scat_b)]
    dzpre, d_w_s, d_b_col, d_ln_g, d_ln_b = _gate_bwd("mix_a_gate_bwd", zpre, dgated, a_ln_g, a_ln_b, a_w_s[0], b_col)
    dhn0 = _mm_aw("mix_a_in_dx", dzpre, W["a_w_in"], transpose_w=True, deps=(tok, tok_a, *[t for _, t in early]))
    scat_a_out, tok = reduce(pair_a_out, dhn0)
    g_a_in = _mm_dw("mix_a_in_dw", hn0, dzpre, W["a_w_in"], deps=(tok,))
    pair_a_in, tok = exchange("a_in", [g_a_in])
    dx, _, d_norm_mix0 = _rms_bwd("mix_a_norm_bwd", x2, norm_mix[0:1], dhn0, dh1)
    scat_a_in, _ = reduce(pair_a_in, dx)

    small = dict(a_ln_g=(a_ln_g, m_a_ln_g, v_a_ln_g), a_ln_b=(a_ln_b, m_a_ln_b, v_a_ln_b),
                 a_w_s=(a_w_s, m_a_w_s, v_a_w_s), a_b_s=(a_b_s, m_a_b_s, v_a_b_s),
                 b_scale=(b_scale, m_b_scale, v_b_scale), norm_mix=(norm_mix, m_norm_mix, v_norm_mix),
                 norm_mlp=(norm_mlp, m_norm_mlp, v_norm_mlp), final_norm=(final_norm, m_final_norm, v_final_norm))
    small_names = list(small)
    local = dict(a_ln_g=d_ln_g, a_ln_b=d_ln_b, a_w_s=d_w_s[None], a_b_s=d_b_col.T[None], b_scale=d_scale,
                 norm_mix=jnp.concatenate([d_norm_mix0, d_norm_mix1], axis=0),
                 norm_mlp=jnp.concatenate([d_norm_mlp0, d_norm_mlp1], axis=0), final_norm=d_final.reshape(-1))
    device = (4 * xi + 2 * yi + ci).astype(jnp.int32).reshape(1)
    slots = _own_of_eight("small_own", _pack([local[k] for k in small_names]), device)
    small_send, small_recv, _, (slots,), small_tok = _split_start("small_start", [], [slots], _all_plan)

    moments = dict(a_w_in=(m_a_w_in, v_a_w_in), a_w_out=(m_a_w_out, v_a_w_out), b_w_in=(m_b_w_in, v_b_w_in),
                   b_w_grp=(m_b_w_grp, v_b_w_grp), b_w_out=(m_b_w_out, v_b_w_out),
                   mlp_w1=(m_mlp_w1, v_mlp_w1), mlp_w2=(m_mlp_w2, v_mlp_w2))
    weights = dict(a_w_in=a_w_in, a_w_out=a_w_out, b_w_in=b_w_in, b_w_grp=b_w_grp, b_w_out=b_w_out,
                   mlp_w1=mlp_w1, mlp_w2=mlp_w2)
    landing = [(scat_w2_1, [("mlp_w2", 1)]), (scat_w1_1, [("mlp_w1", 1)]),
               (scat_b, [("b_w_out", 0), ("b_w_grp", 0), ("b_w_in", 0)]),
               (scat_w2_0, [("mlp_w2", 0)]), (scat_w1_0, [("mlp_w1", 0)]),
               (scat_a_out, [("a_w_out", 0)]), (scat_a_in, [("a_w_in", 0)])]
    results, after = {}, dx
    relays = list(early)
    for i, (_, members) in enumerate(landing):
        deps = (small_tok,) if i == 0 else ()
        if len(relays) == i + 1 < len(landing):
            relays.append(relay(landing[i + 1][0], after))
            deps = (relays[-1][1],)
        for (k, layer), parts in zip(members, land(relays[i][0], relays[-1][1] if deps else after)):
            shard_shape = (-1, *parts.shape[2:])
            results[k] = _adam_shard(f"adam_{k}_{layer}", weights[k].reshape(shard_shape),
                                     moments[k][0].reshape(shard_shape), moments[k][1].reshape(shard_shape),
                                     parts, layer=layer, prev=results.get(k), deps=deps)
            after = results[k][1]
    grad_out, delta_out, m_out, v_out = {}, {}, {}, {}
    for k, res in results.items():
        grad_out[k], delta_out[k], m_out[k], v_out[k] = [r.reshape(weights[k].shape) for r in res]

    _, (slots,) = _split_wait("small_wait", [], [slots], small_send, small_recv, after, _all_whole)
    reduced = _sum_of_eight("small_sum", slots)
    small_grads = dict(zip(small_names, _unpack(reduced, [local[k].shape for k in small_names])))
    shard_w = b_scale.shape[1]
    small_grads["b_scale"] = lax.dynamic_slice_in_dim(small_grads["b_scale"], chip * shard_w, shard_w, axis=1)
    small_grads = {k: small_grads[k].reshape(small[k][0].shape) for k in small_names}
    packed = [_pack([small[k][i] for k in small_names]) for i in range(3)]
    res = _adam_packed("adam_small", packed[0], _pack([small_grads[k] for k in small_names]), packed[1], packed[2])
    shapes = [small[k][0].shape for k in small_names]
    for k, d, nm, nv in zip(small_names, *[_unpack(r, shapes) for r in res]):
        grad_out[k], delta_out[k], m_out[k], v_out[k] = small_grads[k], d, nm, nv

    loss = lax.psum(loss_part[0, 0], ("x", "y", "c"))
    order = ["a_w_in", "a_ln_g", "a_ln_b", "a_w_s", "a_b_s", "a_w_out", "b_w_in", "b_w_grp", "b_scale", "b_w_out",
             "norm_mix", "norm_mlp", "mlp_w1", "mlp_w2", "final_norm"]
    return (loss, dx[None], *[grad_out[k] for k in order], *[delta_out[k] for k in order],
            *[m_out[k] for k in order], *[v_out[k] for k in order])
```

```python
import math

import jax
import jax.numpy as jnp
from jax import lax
from jax.experimental import pallas as pl
from jax.experimental.pallas import tpu as pltpu

F32 = jnp.float32
BF16 = jnp.bfloat16
MESH = pl.DeviceIdType.MESH

EPS = 1e-6
B_WINDOWS = (2, 4, 8, 16)
ADAM_LR = 0.001
ADAM_B1 = 0.9
ADAM_B2 = 0.999
ADAM_EPS = 1e-08
ADAM_WD = 0.01
ADAM_STEP = 10

N_CHIPS = 4
N_DEV = 8
LANES = 128
PACK_ELEMS = 8 * LANES
VMEM_LIMIT = 56 * 1024 * 1024
ROW_TILE = 512
MM_TM, MM_TN, MM_TK = 1024, 1024, 2048
MM_DEEP = 4096


_ANY = pl.BlockSpec(memory_space=pl.ANY)


def _tile(dim, pref):
    t = min(dim, pref)
    while dim % t:
        t //= 2
    return t


def _params(*sem):
    return pltpu.CompilerParams(dimension_semantics=sem, vmem_limit_bytes=VMEM_LIMIT)


class _W4:
    def __init__(self, arr, col_sharded, shape=None):
        self.arr = arr
        self.nj, self.nl, self.r, self.c = arr.shape if shape is None else shape
        self.col = col_sharded
        self.rows = self.r if col_sharded else self.nj * self.r
        self.cols = self.nj * self.c if col_sharded else self.c

    def tile_rows(self, pref):
        return _tile(self.r, pref)

    def tile_cols(self, pref):
        return _tile(self.c, pref)

    def index(self, layer, rb, cb, tr, tc):
        if self.col:
            n = self.c // tc
            return (cb // n, layer, rb, cb % n)
        n = self.r // tr
        return (rb // n, layer, rb % n, cb)


def _mm_aw(name, a, w, *, layer=0, groups=1, transpose_w=False, extras=(), out_dtypes=(F32,), epilogue=None,
           deps=()):
    s, ka_total = a.shape
    kdim, ndim = (w.cols, w.rows) if transpose_w else (w.rows, w.cols)
    assert ka_total == groups * kdim, (name, a.shape, kdim, groups)
    span = not w.col and ((not transpose_w and kdim <= MM_TK) or (transpose_w and groups > 1))
    per_shard = w.c if transpose_w else w.r
    deep = 0
    if not span and groups == 1 and w.col == transpose_w and kdim > MM_TK and per_shard <= MM_TK:
        deep = min(w.nj, MM_DEEP // per_shard)
    if span and transpose_w:
        tm, tn, tk = _tile(s, 2048), ndim, w.tile_cols(MM_TK)
    elif deep:
        tm, tk = _tile(s, MM_TM), deep * per_shard
        tn = w.tile_rows(512) if transpose_w else w.tile_cols(512)
    else:
        tk = kdim if span else (w.tile_cols(MM_TK) if transpose_w else w.tile_rows(MM_TK))
        tm, tn_pref = (_tile(s, 2048), 512) if tk == kdim else (_tile(s, MM_TM), MM_TN)
        tn = w.tile_rows(tn_pref) if transpose_w else w.tile_cols(tn_pref)
    nk, nn = kdim // tk, ndim // tn

    def lay(g):
        return g if groups > 1 else layer

    a_spec = pl.BlockSpec((tm, tk), lambda g, i, n, k: (i, g * nk + k))
    if span and transpose_w:
        w_spec = pl.BlockSpec((w.nj, None, w.r, tk), lambda g, i, n, k: (0, lay(g), 0, k))
    elif span:
        w_spec = pl.BlockSpec((w.nj, None, w.r, tn), lambda g, i, n, k: (0, lay(g), 0, n))
    elif deep and transpose_w:
        w_spec = pl.BlockSpec((deep, None, tn, w.c), lambda g, i, n, k: (k, layer, n, 0))
    elif deep:
        w_spec = pl.BlockSpec((deep, None, w.r, tn), lambda g, i, n, k: (k, layer, 0, n))
    elif transpose_w:
        w_spec = pl.BlockSpec((None, None, tn, tk), lambda g, i, n, k: w.index(lay(g), n, k, tn, tk))
    else:
        w_spec = pl.BlockSpec((None, None, tk, tn), lambda g, i, n, k: w.index(lay(g), k, n, tk, tn))
    ex_specs = []
    for e in extras:
        assert e.shape[1] == groups * ndim and e.shape[0] in (1, s), (name, e.shape)
        if e.shape[0] == 1:
            ex_specs.append(pl.BlockSpec((1, tn), lambda g, i, n, k: (0, g * nn + n)))
        else:
            ex_specs.append(pl.BlockSpec((tm, tn), lambda g, i, n, k: (i, g * nn + n)))
    out_spec = pl.BlockSpec((tm, tn), lambda g, i, n, k: (i, g * nn + n))
    n_ex, n_out, n_dep = len(extras), len(out_dtypes), len(deps)

    def body(a_ref, w_ref, *rest):
        ex, outs = rest[:n_ex], rest[n_ex + n_dep:n_ex + n_dep + n_out]
        av = a_ref[...]
        if av.dtype != BF16:
            av = av.astype(BF16)
        nt = (((1,), (1,)), ((), ()))
        if deep and transpose_w:
            prod = lax.dot_general(av[:, :w.c], w_ref[0], nt, preferred_element_type=F32)
            for j in range(1, deep):
                prod += lax.dot_general(av[:, j * w.c:(j + 1) * w.c], w_ref[j], nt, preferred_element_type=F32)
        else:
            wv = w_ref[...]
            if span or deep:
                wv = wv.reshape((tn, tk) if transpose_w else (tk, tn))
            if transpose_w:
                prod = lax.dot_general(av, wv, nt, preferred_element_type=F32)
            else:
                prod = jnp.dot(av, wv, preferred_element_type=F32)

        def finish(total):
            vals = (total,) if epilogue is None else epilogue(total, *[e[...] for e in ex])
            for o, v in zip(outs, vals):
                o[...] = v.astype(o.dtype)

        if nk == 1:
            finish(prod)
            return
        acc, k = rest[-1], pl.program_id(3)

        @pl.when(k == 0)
        def _():
            acc[...] = prod

        @pl.when(k > 0)
        def _():
            acc[...] += prod

        @pl.when(k == nk - 1)
        def _():
            finish(acc[...])

    outs = pl.pallas_call(
        body, name=name, grid=(groups, s // tm, nn, nk),
        in_specs=[a_spec, w_spec, *ex_specs] + [_ANY] * n_dep, out_specs=[out_spec] * n_out,
        out_shape=[jax.ShapeDtypeStruct((s, groups * ndim), dt) for dt in out_dtypes],
        scratch_shapes=[pltpu.VMEM((tm, tn), F32)] if nk > 1 else [],
        compiler_params=_params("parallel", "parallel", "parallel", "arbitrary"),
    )(a, w.arr, *extras, *deps)
    return outs[0] if n_out == 1 else outs


def _mm_dw(name, a, b, like, *, layer=0, groups=1, deps=()):
    s, ka_total = a.shape
    rows, cols = ka_total // groups, b.shape[1] // groups
    assert (rows, cols) == (like.rows, like.cols) and b.shape[0] == s, (name, a.shape, b.shape)
    span = groups > 1 and not like.col
    tm, tn, tk = rows if span else like.tile_rows(MM_TM), like.tile_cols(MM_TN), _tile(s, MM_TK)
    nr, nc, nk = rows // tm, cols // tn, s // tk
    assert nk == 1 or not span

    def lay(g):
        return g if groups > 1 else layer

    in_specs = [pl.BlockSpec((tk, tm), lambda g, n, i, k: (k, g * nr + i)),
                pl.BlockSpec((tk, tn), lambda g, n, i, k: (k, g * nc + n))]
    in_specs += [_ANY] * len(deps)

    def body(a_ref, b_ref, *rest):
        av, bv = a_ref[...], b_ref[...]
        if av.dtype != BF16:
            av = av.astype(BF16)
        if bv.dtype != BF16:
            bv = bv.astype(BF16)
        prod = lax.dot_general(av, bv, (((0,), (0,)), ((), ())), preferred_element_type=F32)
        if nk == 1:
            rest[-1][...] = prod.astype(BF16).reshape(rest[-1].shape)
            return
        o_ref, acc, k = rest[-2], rest[-1], pl.program_id(3)

        @pl.when(k == 0)
        def _():
            acc[...] = prod

        @pl.when(k > 0)
        def _():
            acc[...] += prod

        @pl.when(k == nk - 1)
        def _():
            o_ref[...] = acc[...].astype(BF16)

    if span:
        out_spec = pl.BlockSpec((like.nj, None, like.r, tn), lambda g, n, i, k: (0, g, 0, n))
    else:
        out_spec = pl.BlockSpec((None, None, tm, tn), lambda g, n, i, k: like.index(lay(g), i, n, tm, tn))
    return pl.pallas_call(
        body, name=name, grid=(groups, nc, nr, nk), in_specs=in_specs, out_specs=out_spec,
        out_shape=jax.ShapeDtypeStruct((like.nj, like.nl, like.r, like.c), BF16),
        scratch_shapes=[pltpu.VMEM((tm, tn), F32)] if nk > 1 else [],
        compiler_params=_params("parallel", "parallel", "parallel", "arbitrary"),
    )(a, b, *deps)


def _row_spec(tr, d):
    return pl.BlockSpec((tr, d), lambda i: (i, 0))


def _vec_spec(d):
    return pl.BlockSpec((1, d), lambda i: (0, 0))


def _rms_fwd(name, x, g):
    s, d = x.shape
    tr = _tile(s, ROW_TILE)

    def body(x_ref, g_ref, o_ref):
        xv = x_ref[...]
        r = lax.rsqrt(jnp.mean(xv * xv, axis=-1, keepdims=True) + EPS)
        o_ref[...] = (xv * r * g_ref[...]).astype(BF16)

    return pl.pallas_call(
        body, name=name, grid=(s // tr,), in_specs=[_row_spec(tr, d), _vec_spec(d)],
        out_specs=_row_spec(tr, d), out_shape=jax.ShapeDtypeStruct((s, d), BF16),
        compiler_params=_params("parallel"),
    )(x, g)


def _rms_bwd(name, x, g, dhn, dres):
    s, d = x.shape
    tr = _tile(s, ROW_TILE)

    def body(x_ref, g_ref, dhn_ref, dres_ref, dx_ref, dxb_ref, dg_ref):
        @pl.when(pl.program_id(0) == 0)
        def _():
            dg_ref[...] = jnp.zeros_like(dg_ref)

        xv = x_ref[...]
        r = lax.rsqrt(jnp.mean(xv * xv, axis=-1, keepdims=True) + EPS)
        xh = xv * r
        dy = dhn_ref[...]
        dg_ref[...] += jnp.sum(dy * xh, axis=0, keepdims=True)
        dxh = dy * g_ref[...]
        dx = dres_ref[...] + r * (dxh - xh * jnp.mean(dxh * xh, axis=-1, keepdims=True))
        dx_ref[...] = dx
        dxb_ref[...] = dx.astype(BF16)

    return pl.pallas_call(
        body, name=name, grid=(s // tr,),
        in_specs=[_row_spec(tr, d), _vec_spec(d), _row_spec(tr, d), _row_spec(tr, d)],
        out_specs=[_row_spec(tr, d), _row_spec(tr, d), _vec_spec(d)],
        out_shape=[jax.ShapeDtypeStruct((s, d), F32), jax.ShapeDtypeStruct((s, d), BF16),
                   jax.ShapeDtypeStruct((1, d), F32)],
        compiler_params=_params("arbitrary"),
    )(x, g, dhn, dres)


def _final(name, h, g, tgt):
    s, d = h.shape
    tr = _tile(s, ROW_TILE)

    def body(h_ref, g_ref, t_ref, dh_ref, dhb_ref, dg_ref, loss_ref):
        @pl.when(pl.program_id(0) == 0)
        def _():
            dg_ref[...] = jnp.zeros_like(dg_ref)
            loss_ref[...] = jnp.zeros_like(loss_ref)

        hv = h_ref[...]
        r = lax.rsqrt(jnp.mean(hv * hv, axis=-1, keepdims=True) + EPS)
        xh = hv * r
        gv = g_ref[...]
        err = xh * gv - t_ref[...]
        part = 0.5 * jnp.sum(jnp.mean(err * err, axis=-1, keepdims=True), axis=0, keepdims=True)
        loss_ref[...] += jnp.broadcast_to(part, loss_ref.shape)
        dy = err * (1.0 / d)
        dg_ref[...] += jnp.sum(dy * xh, axis=0, keepdims=True)
        dxh = dy * gv
        dh = r * (dxh - xh * jnp.mean(dxh * xh, axis=-1, keepdims=True))
        dh_ref[...] = dh
        dhb_ref[...] = dh.astype(BF16)

    return pl.pallas_call(
        body, name=name, grid=(s // tr,),
        in_specs=[_row_spec(tr, d), _vec_spec(d), _row_spec(tr, d)],
        out_specs=[_row_spec(tr, d), _row_spec(tr, d), _vec_spec(d), _vec_spec(LANES)],
        out_shape=[jax.ShapeDtypeStruct((s, d), F32), jax.ShapeDtypeStruct((s, d), BF16),
                   jax.ShapeDtypeStruct((1, d), F32), jax.ShapeDtypeStruct((1, LANES), F32)],
        compiler_params=_params("arbitrary"),
    )(h, g, tgt)


_SQRT_HALF = 1.0 / math.sqrt(2.0)
_INV_SQRT_2PI = 1.0 / math.sqrt(2.0 * math.pi)


def _gelu(x):
    return x * (lax.erf(x * _SQRT_HALF) + 1.0) * 0.5


def _gelu_grad(x):
    return 0.5 * (lax.erf(x * _SQRT_HALF) + 1.0) + x * jnp.exp(-0.5 * x * x) * _INV_SQRT_2PI


def _causal(chunk):
    t = lax.broadcasted_iota(jnp.int32, (chunk, chunk), 0)
    sidx = lax.broadcasted_iota(jnp.int32, (chunk, chunk), 1)
    return sidx <= t


def _layernorm_parts(v, g, b):
    mu = jnp.mean(v, axis=-1, keepdims=True)
    vc = v - mu
    rs = lax.rsqrt(jnp.mean(vc * vc, axis=-1, keepdims=True) + EPS)
    vhat = vc * rs
    return vhat, rs, vhat * g + b


def _gate_fwd(name, zpre, ln_g, ln_b, w_s, b_col):
    s, aw2 = zpre.shape
    aw = aw2 // 2
    ng, chunk, _ = w_s.shape
    dh = aw // ng

    def body(z_ref, g_ref, b_ref, ws_ref, bc_ref, o_ref):
        u = _gelu(z_ref[:, :aw])
        v = _gelu(z_ref[:, aw:])
        _, _, vln = _layernorm_parts(v, g_ref[...], b_ref[...])
        mask = _causal(chunk)
        for gi in range(ng):
            sl = slice(gi * dh, (gi + 1) * dh)
            wm = jnp.where(mask, ws_ref[gi], 0.0).astype(BF16)
            sg = jnp.dot(wm, vln[:, sl].astype(BF16), preferred_element_type=F32) + bc_ref[:, gi:gi + 1]
            o_ref[:, sl] = (u[:, sl] * sg).astype(BF16)

    return pl.pallas_call(
        body, name=name, grid=(s // chunk,),
        in_specs=[_row_spec(chunk, aw2), _vec_spec(aw), _vec_spec(aw),
                  pl.BlockSpec((ng, chunk, chunk), lambda i: (0, 0, 0)),
                  pl.BlockSpec((chunk, ng), lambda i: (0, 0))],
        out_specs=_row_spec(chunk, aw), out_shape=jax.ShapeDtypeStruct((s, aw), BF16),
        compiler_params=_params("parallel"),
    )(zpre, ln_g, ln_b, w_s, b_col)


def _gate_bwd(name, zpre, dgated, ln_g, ln_b, w_s, b_col):
    s, aw2 = zpre.shape
    aw = aw2 // 2
    ng, chunk, _ = w_s.shape
    dh = aw // ng

    def body(z_ref, dgt_ref, g_ref, b_ref, ws_ref, bc_ref, dz_ref, dws_ref, dbc_ref, dlg_ref, dlb_ref,
             du_scr, dvln_scr):
        @pl.when(pl.program_id(0) == 0)
        def _():
            dws_ref[...] = jnp.zeros_like(dws_ref)
            dbc_ref[...] = jnp.zeros_like(dbc_ref)
            dlg_ref[...] = jnp.zeros_like(dlg_ref)
            dlb_ref[...] = jnp.zeros_like(dlb_ref)

        zu = z_ref[:, :aw]
        zv = z_ref[:, aw:]
        u = _gelu(zu)
        lg = g_ref[...]
        vhat, rs, vln = _layernorm_parts(_gelu(zv), lg, b_ref[...])
        mask = _causal(chunk)
        for gi in range(ng):
            sl = slice(gi * dh, (gi + 1) * dh)
            wm = jnp.where(mask, ws_ref[gi], 0.0).astype(BF16)
            vg = vln[:, sl].astype(BF16)
            sg = jnp.dot(wm, vg, preferred_element_type=F32) + bc_ref[:, gi:gi + 1]
            dgt = dgt_ref[:, sl]
            du_scr[:, sl] = dgt * sg
            ds = dgt * u[:, sl]
            dbc_ref[:, gi:gi + 1] += jnp.sum(ds, axis=-1, keepdims=True)
            dsb = ds.astype(BF16)
            dwm = lax.dot_general(dsb, vg, (((1,), (1,)), ((), ())), preferred_element_type=F32)
            dws_ref[gi] += jnp.where(mask, dwm, 0.0)
            dvln_scr[:, sl] = lax.dot_general(wm, dsb, (((0,), (0,)), ((), ())), preferred_element_type=F32)
        dvln = dvln_scr[...]
        dlb_ref[...] += jnp.sum(dvln, axis=0, keepdims=True)
        dlg_ref[...] += jnp.sum(dvln * vhat, axis=0, keepdims=True)
        dvh = dvln * lg
        dv = rs * (dvh - jnp.mean(dvh, axis=-1, keepdims=True)
                   - vhat * jnp.mean(dvh * vhat, axis=-1, keepdims=True))
        dz_ref[:, :aw] = (du_scr[...] * _gelu_grad(zu)).astype(BF16)
        dz_ref[:, aw:] = (dv * _gelu_grad(zv)).astype(BF16)

    return pl.pallas_call(
        body, name=name, grid=(s // chunk,),
        in_specs=[_row_spec(chunk, aw2), _row_spec(chunk, aw), _vec_spec(aw), _vec_spec(aw),
                  pl.BlockSpec((ng, chunk, chunk), lambda i: (0, 0, 0)),
                  pl.BlockSpec((chunk, ng), lambda i: (0, 0))],
        out_specs=[_row_spec(chunk, aw2), pl.BlockSpec((ng, chunk, chunk), lambda i: (0, 0, 0)),
                   pl.BlockSpec((chunk, ng), lambda i: (0, 0)), _vec_spec(aw), _vec_spec(aw)],
        out_shape=[jax.ShapeDtypeStruct((s, aw2), BF16), jax.ShapeDtypeStruct((ng, chunk, chunk), F32),
                   jax.ShapeDtypeStruct((chunk, ng), F32), jax.ShapeDtypeStruct((1, aw), F32),
                   jax.ShapeDtypeStruct((1, aw), F32)],
        scratch_shapes=[pltpu.VMEM((chunk, aw), F32), pltpu.VMEM((chunk, aw), F32)],
        compiler_params=_params("arbitrary"),
    )(zpre, dgated, ln_g, ln_b, w_s, b_col)


def _pool_select(g, parts):
    out = parts[-1]
    for gi in range(len(parts) - 2, -1, -1):
        out = jnp.where(g == gi, parts[gi], out)
    return out


def _pool_specs(s, bw):
    head = bw // len(B_WINDOWS)
    tc = _tile(head, 256)
    nb = head // tc
    return tc, (len(B_WINDOWS), nb), pl.BlockSpec((s, tc), lambda g, j: (0, g * nb + j))


def _pool_window(g, t):
    w = _pool_select(g, [jnp.full(t.shape, wi, jnp.int32) for wi in B_WINDOWS])
    return jnp.minimum(t + 1, w).astype(F32)


def _pool_fwd(name, vb):
    assert B_WINDOWS == (2, 4, 8, 16)
    s, bw = vb.shape
    tc, grid, spec = _pool_specs(s, bw)

    def body(v_ref, o_ref):
        g = pl.program_id(0)
        v = v_ref[...]
        t = lax.broadcasted_iota(jnp.int32, (s, tc), 0)

        def down(x, k):
            return jnp.where(t >= k, pltpu.roll(x, k, 0), 0.0)

        sums, cur, k = [], v, 1
        for _ in B_WINDOWS:
            cur = cur + down(cur, k)
            sums.append(cur)
            k *= 2
        o_ref[...] = (_pool_select(g, sums) / _pool_window(g, t) - v).astype(BF16)

    return pl.pallas_call(
        body, name=name, grid=grid, in_specs=[spec], out_specs=spec,
        out_shape=jax.ShapeDtypeStruct((s, bw), BF16), compiler_params=_params("parallel", "parallel"),
    )(vb)


def _pool_bwd(name, dpooled):
    s, bw = dpooled.shape
    tc, grid, spec = _pool_specs(s, bw)

    def body(d_ref, o_ref):
        g = pl.program_id(0)
        dp = d_ref[...]
        t = lax.broadcasted_iota(jnp.int32, (s, tc), 0)

        def up(x, k):
            return jnp.where(t < s - k, pltpu.roll(x, s - k, 0), 0.0)

        sums, cur, k = [], dp / _pool_window(g, t), 1
        for _ in B_WINDOWS:
            cur = cur + up(cur, k)
            sums.append(cur)
            k *= 2
        o_ref[...] = (_pool_select(g, sums) - dp).astype(BF16)

    return pl.pallas_call(
        body, name=name, grid=grid, in_specs=[spec], out_specs=spec,
        out_shape=jax.ShapeDtypeStruct((s, bw), BF16), compiler_params=_params("parallel", "parallel"),
    )(dpooled)


def _scale_bwd(name, dms, mixed, scale):
    s, bw = dms.shape
    tr = _tile(s, ROW_TILE)

    def body(d_ref, m_ref, sc_ref, o_ref, ds_ref):
        @pl.when(pl.program_id(0) == 0)
        def _():
            ds_ref[...] = jnp.zeros_like(ds_ref)

        dv = d_ref[...]
        ds_ref[...] += jnp.sum(dv * m_ref[...], axis=0, keepdims=True)
        o_ref[...] = (dv * sc_ref[...]).astype(BF16)

    return pl.pallas_call(
        body, name=name, grid=(s // tr,), in_specs=[_row_spec(tr, bw), _row_spec(tr, bw), _vec_spec(bw)],
        out_specs=[_row_spec(tr, bw), _vec_spec(bw)],
        out_shape=[jax.ShapeDtypeStruct((s, bw), BF16), jax.ShapeDtypeStruct((1, bw), F32)],
        compiler_params=_params("arbitrary"),
    )(dms, mixed, scale)


def _adam_update(w, g, m, v):
    m = ADAM_B1 * m + (1.0 - ADAM_B1) * g
    v = ADAM_B2 * v + (1.0 - ADAM_B2) * (g * g)
    m_hat = m / (1.0 - ADAM_B1 ** ADAM_STEP)
    v_hat = v / (1.0 - ADAM_B2 ** ADAM_STEP)
    delta = -ADAM_LR * (m_hat / (jnp.sqrt(v_hat) + ADAM_EPS) + ADAM_WD * w)
    return delta, m, v


def _adam_shard(name, w, m, v, parts, layer=0, prev=None, deps=()):
    nl, r, c = w.shape
    nj, nlp = parts.shape[:2]
    tr = _tile(r, ROW_TILE // 2)
    spec = pl.BlockSpec((None, tr, c), lambda l, i: (layer + l, i, 0))

    def body(w_ref, m_ref, v_ref, p_ref, *rest):
        g_ref, d_ref, nm_ref, nv_ref = rest[-4:]
        g = p_ref[0].astype(F32)
        for j in range(1, nj):
            g = g + p_ref[j].astype(F32)
        delta, nm, nv = _adam_update(w_ref[...], g, m_ref[...], v_ref[...])
        g_ref[...] = g
        d_ref[...] = delta
        nm_ref[...] = nm
        nv_ref[...] = nv

    prev = () if prev is None else tuple(prev)
    return pl.pallas_call(
        body, name=name, grid=(nlp, r // tr),
        in_specs=[spec, spec, spec, pl.BlockSpec((nj, None, tr, c), lambda l, i: (0, l, i, 0))]
        + [_ANY] * (len(prev) + len(deps)),
        out_specs=[spec] * 4, out_shape=[jax.ShapeDtypeStruct(w.shape, F32)] * 4,
        input_output_aliases={4 + i: i for i in range(len(prev))},
        compiler_params=_params("parallel", "parallel"),
    )(w, m, v, parts, *prev, *deps)


def _adam_packed(name, w, g, m, v):
    rows, lanes = w.shape
    tr = rows
    spec = pl.BlockSpec((tr, lanes), lambda i: (i, 0))

    def body(w_ref, g_ref, m_ref, v_ref, d_ref, nm_ref, nv_ref):
        delta, nm, nv = _adam_update(w_ref[...], g_ref[...], m_ref[...], v_ref[...])
        d_ref[...] = delta
        nm_ref[...] = nm
        nv_ref[...] = nv

    return pl.pallas_call(
        body, name=name, grid=(rows // tr,), in_specs=[spec] * 4, out_specs=[spec] * 3,
        out_shape=[jax.ShapeDtypeStruct(w.shape, F32)] * 3, compiler_params=_params("parallel"),
    )(w, g, m, v)


def _place():
    x, y, c = lax.axis_index("x"), lax.axis_index("y"), lax.axis_index("c")
    chips = [(1 - x, y), (x, 1 - y), (1 - x, 1 - y)]
    return x, y, c, chips


def _remote(src, dst, send_sem, recv_sem, device):
    return pltpu.make_async_remote_copy(src_ref=src, dst_ref=dst, send_sem=send_sem, recv_sem=recv_sem,
                                        device_id=device, device_id_type=MESH)


def _half(ref_rows, cc):
    h = ref_rows // 2
    return pl.ds(cc * h, h)


_HBM = pl.BlockSpec(memory_space=pltpu.HBM)
_SEM = pl.BlockSpec(memory_space=pltpu.SEMAPHORE)
_EFFECT = pltpu.SideEffectType.DATAFLOW_SIDE_EFFECTING


def _in_hbm(arr):
    return pltpu.with_memory_space_constraint(arr, pltpu.HBM)


def _gather_rows(land, cc, part=(0, 1)):
    k, n = part
    h = land.shape[2] // 2
    return pl.ds(cc * h + k * (h // n), h // n)


def _gather_block(land, split, j, cc, part=(0, 1)):
    if split:
        return land.at[j, :, _gather_rows(land, cc, part), :]
    return land.at[j]


def _split_start(name, srcs, lands, plan, deps=(), groups=None):
    ns = len(srcs)
    nl = len(lands) if groups is None else groups
    both = list(srcs) + list(lands)
    nb = len(both)

    def body(*refs):
        s, ld = refs[:ns], refs[ns:nb]
        outs = refs[nb + len(deps):]
        send, recv, token = outs[:nl], outs[nl:2 * nl], outs[-1]
        for a, copies in enumerate(plan(s, ld)):
            for src, dst, peer in copies:
                _remote(src, dst, send[a], recv[a], peer).start()
        token[...] = jnp.zeros_like(token)

    outs = pl.pallas_call(
        body, name=name, in_specs=[_HBM] * nb + [_ANY] * len(deps),
        out_specs=[_SEM] * (2 * nl) + [_HBM] * nb + [pl.BlockSpec(memory_space=pltpu.VMEM)],
        out_shape=[pltpu.SemaphoreType.DMA(())] * (2 * nl) + [pltpu.HBM(b.shape, b.dtype) for b in both]
        + [jax.ShapeDtypeStruct((8, LANES), F32)],
        input_output_aliases={i: 2 * nl + i for i in range(nb)},
        compiler_params=pltpu.CompilerParams(has_side_effects=_EFFECT),
    )(*[_in_hbm(b) for b in both], *deps)
    thru = outs[2 * nl:2 * nl + nb]
    return list(outs[:nl]), list(outs[nl:2 * nl]), list(thru[:ns]), list(thru[ns:]), outs[-1]


def _split_wait(name, srcs, lands, send, recv, after, whole):
    ns, nl = len(srcs), len(send)
    both = list(srcs) + list(lands)
    nb = len(both)

    def body(*refs):
        ld, snd, rcv = refs[ns:nb], refs[nb:nb + nl], refs[nb + nl:nb + 2 * nl]
        x, y, c, _ = _place()
        for a, blk in enumerate(whole(ld)):
            every = _remote(blk, blk, snd[a], rcv[a], (x, y, c))
            every.wait_send()
            every.wait_recv()

    outs = pl.pallas_call(
        body, name=name, in_specs=[_HBM] * nb + [_SEM] * (2 * nl) + [_ANY], out_specs=[_HBM] * nb,
        out_shape=[pltpu.HBM(b.shape, b.dtype) for b in both],
        input_output_aliases={i: i for i in range(nb)},
        compiler_params=pltpu.CompilerParams(has_side_effects=_EFFECT),
    )(*both, *send, *recv, after)
    return list(outs[:ns]), list(outs[ns:])


def _gather_plan(split, parts):
    def plan(srcs, lands):
        x, y, c, chips = _place()
        out = []
        for ld, sp, n in zip(lands, split, parts):
            for k in range(n):
                blk = _gather_block(ld, sp, 2 * x + y, c, (k, n))
                out.append([(blk, blk, (qx, qy, c)) for qx, qy in chips])
        return out
    return plan


def _gather_whole(split, part=(0, 1)):
    def whole(lands):
        _, _, c, _ = _place()
        return [ld.at[pl.ds(0, 3), :, _gather_rows(ld, c, part), :] if sp else ld.at[pl.ds(0, 3)]
                for ld, sp in zip(lands, split)]
    return whole


def _relay_plan(srcs, lands):
    x, y, c, _ = _place()
    out = []
    for ld in lands:
        blocks = [ld.at[j, :, _half(ld.shape[2], c), :] for j in range(N_CHIPS)]
        out.append([(blk, blk, (x, y, 1 - c)) for blk in blocks])
    return out


def _relay_whole(lands):
    _, _, c, _ = _place()
    return [ld.at[:, :, _half(ld.shape[2], c), :] for ld in lands]


def _fill_own(name, shard, dtype, place, layer=None):
    nl, r, c = shard.shape
    first = 0
    if layer is not None:
        nl, first = 1, layer
    tr = _tile(r, 512)

    def body(p_ref, s_ref, o_ref):
        o_ref[...] = s_ref[...].astype(o_ref.dtype)

    return pl.pallas_call(
        body, name=name,
        grid_spec=pltpu.PrefetchScalarGridSpec(
            num_scalar_prefetch=1, grid=(nl, r // tr),
            in_specs=[pl.BlockSpec((None, tr, c), lambda l, i, p: (first + l, i, 0))],
            out_specs=pl.BlockSpec((None, None, tr, c), lambda l, i, p: (p[0], l, i, 0))),
        out_shape=jax.ShapeDtypeStruct((N_CHIPS, nl, r, c), dtype),
        compiler_params=_params("parallel", "parallel"),
    )(place, shard)


def _gather_finish(name, lands, part=(0, 1)):
    n = len(lands)

    def body(*refs):
        outs = refs[n:2 * n]
        fsend, frecv = refs[2 * n:]
        x, y, c, chips = _place()
        sib = (x, y, 1 - c)

        def relay(a, qi, cc):
            qx, qy = chips[qi]
            blk = _gather_block(outs[a], True, 2 * qx + qy, cc, part)
            return _remote(blk, blk, fsend.at[a, qi], frecv.at[a, qi], sib)

        relays = [relay(a, qi, c) for a in range(n) for qi in range(3)]
        for cp in relays:
            cp.start()
        for a in range(n):
            for qi in range(3):
                relay(a, qi, 1 - c).wait_recv()
        for cp in relays:
            cp.wait_send()

    outs = pl.pallas_call(
        body, name=name, in_specs=[_ANY] * n, out_specs=[_ANY] * n,
        out_shape=[jax.ShapeDtypeStruct(ld.shape, ld.dtype) for ld in lands],
        input_output_aliases={i: i for i in range(n)},
        scratch_shapes=[pltpu.SemaphoreType.DMA((n, 3))] * 2,
    )(*lands)
    return list(outs)


def _pair_plan(srcs, lands):
    x, y, c, _ = _place()
    return [[(s.at[:, :, _half(s.shape[2], 1 - c), :], ld, (x, y, 1 - c))] for s, ld in zip(srcs, lands)]


def _pair_whole(lands):
    return list(lands)


def _scatter_plan(srcs, lands):
    x, y, c, chips = _place()
    return [[(s.at[2 * qx + qy], ld.at[2 * x + y, :, _half(ld.shape[2], c), :], (qx, qy, c)) for qx, qy in chips]
            for s, ld in zip(srcs, lands)]


def _scatter_whole(lands):
    _, _, c, _ = _place()
    return [ld.at[pl.ds(0, 3), :, _half(ld.shape[2], c), :] for ld in lands]


def _pair_sum(name, grad, other, place):
    nj, nl, r, c = grad.shape
    h = r // 2
    tr = _tile(h, 2 * ROW_TILE)
    nb = h // tr

    def body(p_ref, g_ref, o_ref, q_ref, d_ref):
        q = (g_ref[...].astype(F32) + o_ref[...].astype(F32)).astype(BF16)
        q_ref[...] = q

        @pl.when(pl.program_id(2) == p_ref[0])
        def _():
            d_ref[...] = q

    blk = (None, None, tr, c)
    return pl.pallas_call(
        body, name=name,
        grid_spec=pltpu.PrefetchScalarGridSpec(
            num_scalar_prefetch=1, grid=(nl, nb, nj),
            in_specs=[pl.BlockSpec(blk, lambda l, i, j, p: (j, l, p[1] * nb + i, 0)),
                      pl.BlockSpec(blk, lambda l, i, j, p: (j, l, i, 0))],
            out_specs=[pl.BlockSpec(blk, lambda l, i, j, p: (j, l, i, 0)),
                       pl.BlockSpec(blk, lambda l, i, j, p: (p[0], l, p[1] * nb + i, 0))]),
        out_shape=[jax.ShapeDtypeStruct((nj, nl, h, c), BF16), jax.ShapeDtypeStruct((nj, nl, r, c), BF16)],
        compiler_params=_params("parallel", "parallel", "arbitrary"),
    )(place, grad, other)


def _own_of_eight(name, packed, device):
    rows, lanes = packed.shape
    tr = rows

    def body(d_ref, s_ref, o_ref):
        o_ref[...] = s_ref[...]

    return pl.pallas_call(
        body, name=name,
        grid_spec=pltpu.PrefetchScalarGridSpec(
            num_scalar_prefetch=1, grid=(rows // tr,),
            in_specs=[pl.BlockSpec((tr, lanes), lambda i, d: (i, 0))],
            out_specs=pl.BlockSpec((None, tr, lanes), lambda i, d: (d[0], i, 0))),
        out_shape=jax.ShapeDtypeStruct((N_DEV, rows, lanes), packed.dtype),
        compiler_params=_params("parallel"),
    )(device, packed)


def _all_plan(srcs, lands):
    x, y, c, _ = _place()
    (ld,) = lands
    blk = ld.at[4 * x + 2 * y + c]
    flips = [(a, b, d) for a in (0, 1) for b in (0, 1) for d in (0, 1) if a + b + d]
    return [[(blk, blk, (x + a - 2 * a * x, y + b - 2 * b * y, c + d - 2 * d * c)) for a, b, d in flips]]


def _all_whole(lands):
    return [lands[0].at[pl.ds(0, N_DEV - 1)]]


def _sum_of_eight(name, slots):
    n, rows, lanes = slots.shape
    tr = rows

    def body(s_ref, o_ref):
        total = s_ref[0]
        for d in range(1, n):
            total = total + s_ref[d]
        o_ref[...] = total

    return pl.pallas_call(
        body, name=name, grid=(rows // tr,),
        in_specs=[pl.BlockSpec((n, tr, lanes), lambda i: (0, i, 0))],
        out_specs=pl.BlockSpec((tr, lanes), lambda i: (i, 0)),
        out_shape=jax.ShapeDtypeStruct((rows, lanes), F32), compiler_params=_params("parallel"),
    )(slots)


def _pack(parts):
    rows = []
    for p in parts:
        flat = p.reshape(-1)
        pad = (-flat.shape[0]) % PACK_ELEMS
        rows.append(jnp.pad(flat, (0, pad)).reshape(-1, LANES))
    return jnp.concatenate(rows, axis=0)


def _unpack(packed, shapes):
    out, row = [], 0
    for sh in shapes:
        size = math.prod(sh)
        nrows = -(-size // PACK_ELEMS) * (PACK_ELEMS // LANES)
        out.append(packed[row:row + nrows].reshape(-1)[:size].reshape(sh))
        row += nrows
    return out


def kernel(x, a_w_in, a_ln_g, a_ln_b, a_w_s, a_b_s, a_w_out, b_w_in, b_w_grp, b_scale, b_w_out, norm_mix, norm_mlp, mlp_w1, mlp_w2, final_norm, loss_target, m_a_w_in, m_a_ln_g, m_a_ln_b, m_a_w_s, m_a_b_s, m_a_w_out, m_b_w_in, m_b_w_grp, m_b_scale, m_b_w_out, m_norm_mix, m_norm_mlp, m_mlp_w1, m_mlp_w2, m_final_norm, v_a_w_in, v_a_ln_g, v_a_ln_b, v_a_w_s, v_a_b_s, v_a_w_out, v_b_w_in, v_b_w_grp, v_b_scale, v_b_w_out, v_norm_mix, v_norm_mlp, v_mlp_w1, v_mlp_w2, v_final_norm):
    xi, yi, ci = lax.axis_index("x"), lax.axis_index("y"), lax.axis_index("c")
    chip = 2 * xi + yi
    place = jnp.stack([chip, ci]).astype(jnp.int32)
    x2, tgt = x[0], loss_target[0]
    bw = b_scale.shape[1] * N_CHIPS

    units = dict(a_w_in=(a_w_in, None), a_w_out=(a_w_out, None), w1_0=(mlp_w1, 0), w2_0=(mlp_w2, 0),
                 b_scale=(b_scale.reshape(1, 1, -1), None), b_w_in=(b_w_in, None), b_w_grp=(b_w_grp[0], None),
                 b_w_out=(b_w_out, None), w1_1=(mlp_w1, 1), w2_1=(mlp_w2, 1))
    col_sharded = dict(a_w_in=True, a_w_out=False, b_w_in=False, b_w_grp=False, b_w_out=False,
                       w1_0=True, w2_0=False, w1_1=True, w2_1=False)
    in_flight, W = {}, {}

    def launch(tag, keys, deps):
        sp = [k != "b_scale" for k in keys]
        parts = [pieces.get(k, 1) for k in keys]
        zones = [_fill_own(f"gather_own_{k}", units[k][0], BF16 if s else F32, place, layer=units[k][1])
                 for k, s in zip(keys, sp)]
        send, recv, _, zones, tok = _split_start(f"gather_start_{tag}", [], zones, _gather_plan(sp, parts), deps,
                                                 groups=sum(parts))
        first = 0
        for k, z, s, n in zip(keys, zones, sp, parts):
            in_flight[k] = (send[first:first + n], recv[first:first + n], z, s)
            first += n
        return tok

    def arrive(keys, after):
        send, recv, zones, sp = zip(*[in_flight[k] for k in keys])
        n = len(send[0])
        if n > 1:
            (key,), zones = keys, list(zones)
            for k in range(n):
                _, zones = _split_wait(f"gather_wait_{key}_{k}", [], zones, [send[0][k]], [recv[0][k]], after,
                                       _gather_whole(sp, (k, n)))
                zones = _gather_finish(f"gather_finish_{key}_{k}", zones, (k, n))
            W[key] = _W4(zones[0], col_sharded[key])
            return
        _, zones = _split_wait(f"gather_wait_{keys[0]}", [], zones, [s[0] for s in send], [r[0] for r in recv],
                               after, _gather_whole(sp))
        relayed = iter(_gather_finish(f"gather_finish_{keys[0]}", [z for z, s in zip(zones, sp) if s]))
        for k, z, s in zip(keys, zones, sp):
            full = next(relayed) if s else z
            W[k] = _W4(full, col_sharded[k]) if k in col_sharded else full

    pieces = dict(w1_0=2, w2_0=2, w1_1=2, w2_1=2)

    token = launch("first", ["a_w_in", "a_w_out"], ())
    token = launch("rest", ["w1_0", "w2_0", "b_scale", "b_w_in", "b_w_grp", "b_w_out", "w1_1", "w2_1"], (token,))

    b_col = a_b_s[0].T

    def residual(acc, res):
        return (res + acc,)

    def sq_relu(acc):
        act = jnp.maximum(acc, 0.0)
        return act, act * act

    def mlp_fwd(tag, h, layer):
        hn = _rms_fwd(f"mlp{tag}_norm", h, norm_mlp[layer:layer + 1])
        arrive([f"w1_{layer}"], hn)
        act, act_sq = _mm_aw(f"mlp{tag}_up", hn, W[f"w1_{layer}"], out_dtypes=(BF16, BF16), epilogue=sq_relu)
        arrive([f"w2_{layer}"], act_sq)
        out = _mm_aw(f"mlp{tag}_down", act_sq, W[f"w2_{layer}"], extras=(h,), epilogue=residual)
        return out, (h, hn, act, act_sq)

    hn0 = _rms_fwd("mix_a_norm", x2, norm_mix[0:1])
    arrive(["a_w_in"], token)
    zpre = _mm_aw("mix_a_in", hn0, W["a_w_in"])
    gated = _gate_fwd("mix_a_gate", zpre, a_ln_g, a_ln_b, a_w_s[0], b_col)
    arrive(["a_w_out"], gated)
    h1 = _mm_aw("mix_a_out", gated, W["a_w_out"], extras=(x2,), epilogue=residual)
    h2, mlp0 = mlp_fwd("0", h1, 0)
    hn2 = _rms_fwd("mix_b_norm", h2, norm_mix[1:2])
    arrive(["b_scale", "b_w_in"], hn2)
    scale_full = W["b_scale"].reshape(1, bw)
    vb = _mm_aw("mix_b_in", hn2, W["b_w_in"])
    pooled = _pool_fwd("mix_b_pool", vb)
    arrive(["b_w_grp", "b_w_out"], pooled)
    mixed, ms = _mm_aw("mix_b_grp", pooled, W["b_w_grp"], groups=len(B_WINDOWS), extras=(scale_full,),
                       out_dtypes=(F32, BF16), epilogue=lambda acc, sc: (acc, acc * sc))
    h3 = _mm_aw("mix_b_out", ms, W["b_w_out"], extras=(h2,), epilogue=residual)
    h4, mlp1 = mlp_fwd("1", h3, 1)
    dh4, dh4_b, d_final, loss_part = _final("loss_head", h4, final_norm.reshape(1, -1), tgt)
    g1_like = _W4(None, True, shape=(N_CHIPS, 1, *W["w1_0"].arr.shape[2:]))
    g2_like = _W4(None, False, shape=(N_CHIPS, 1, *W["w2_0"].arr.shape[2:]))

    def exchange(tag, gs):
        zones = [lax.empty((g.shape[0], g.shape[1], g.shape[2] // 2, g.shape[3]), g.dtype) for g in gs]
        send, recv, srcs, zones, tok = _split_start(f"pair_start_{tag}", gs, zones, _pair_plan)
        return (tag, send, recv, srcs, zones), tok

    def reduce(state, after):
        tag, send, recv, srcs, zones = state
        srcs, zones = _split_wait(f"pair_wait_{tag}", srcs, zones, send, recv, after, _pair_whole)
        both = [_pair_sum(f"pair_sum_{tag}_{i}", g, o, place) for i, (g, o) in enumerate(zip(srcs, zones))]
        send, recv, sums, dests, tok = _split_start(f"scatter_start_{tag}", [b[0] for b in both],
                                                    [b[1] for b in both], _scatter_plan)
        return (tag, send, recv, sums, dests), tok

    def relay(state, after):
        tag, send, recv, sums, dests = state
        _, dests = _split_wait(f"scatter_wait_{tag}", sums, dests, send, recv, after, _scatter_whole)
        send, recv, _, dests, tok = _split_start(f"relay_start_{tag}", [], dests, _relay_plan)
        return (tag, send, recv, dests), tok

    def land(state, after):
        tag, send, recv, dests = state
        return _split_wait(f"relay_wait_{tag}", [], dests, send, recv, after, _relay_whole)[1]

    def mlp_bwd(tag, dh, dh_b, saved, layer, deps, pending=None):
        h, hn, act, act_sq = saved
        dpre = _mm_aw(f"mlp{tag}_down_dx", dh_b, W[f"w2_{layer}"], transpose_w=True, extras=(act,),
                      out_dtypes=(BF16,), epilogue=lambda acc, a: (acc * (2.0 * a),), deps=deps)
        scattering, dw_deps = None, ()
        if pending is not None:
            scattering, tok = reduce(pending, dpre)
            dw_deps = (tok,)
        g_w2 = _mm_dw(f"mlp{tag}_down_dw", act_sq, dh_b, g2_like, deps=dw_deps)
        pair_w2, tok = exchange(f"w2_{layer}", [g_w2])
        dhn = _mm_aw(f"mlp{tag}_up_dx", dpre, W[f"w1_{layer}"], transpose_w=True, deps=(tok,))
        g_w1 = _mm_dw(f"mlp{tag}_up_dw", hn, dpre, g1_like)
        pair_w1, tok1 = exchange(f"w1_{layer}", [g_w1])
        scat_w2, tok2 = reduce(pair_w2, dhn)
        dh_in, dh_in_b, d_norm = _rms_bwd(f"mlp{tag}_norm_bwd", h, norm_mlp[layer:layer + 1], dhn, dh)
        return dh_in, dh_in_b, d_norm, pair_w1, scat_w2, (tok1, tok2), scattering

    dh3, dh3_b, d_norm_mlp1, pair_w1_1, scat_w2_1, toks, _ = mlp_bwd("1", dh4, dh4_b, mlp1, 1, ())
    dms = _mm_aw("mix_b_out_dx", dh3_b, W["b_w_out"], transpose_w=True, deps=toks)
    g_b_out = _mm_dw("mix_b_out_dw", ms, dh3_b, W["b_w_out"])
    scat_w1_1, tok = reduce(pair_w1_1, dms)
    dmixed, d_scale = _scale_bwd("mix_b_scale_bwd", dms, mixed, scale_full)
    dpooled = _mm_aw("mix_b_grp_dx", dmixed, W["b_w_grp"], groups=len(B_WINDOWS), transpose_w=True, deps=(tok,))
    g_b_grp = _mm_dw("mix_b_grp_dw", pooled, dmixed, W["b_w_grp"], groups=len(B_WINDOWS))
    dvb = _pool_bwd("mix_b_pool_bwd", dpooled)
    dhn2 = _mm_aw("mix_b_in_dx", dvb, W["b_w_in"], transpose_w=True)
    g_b_in = _mm_dw("mix_b_in_dw", hn2, dvb, W["b_w_in"])
    pair_b, tok = exchange("b", [g_b_out, g_b_grp, g_b_in])
    dh2, dh2_b, d_norm_mix1 = _rms_bwd("mix_b_norm_bwd", h2, norm_mix[1:2], dhn2, dh3)
    dh1, dh1_b, d_norm_mlp0, pair_w1_0, scat_w2_0, toks, scat_b = mlp_bwd("0", dh2, dh2_b, mlp0, 0, (tok,),
                                                                          pending=pair_b)
    dgated = _mm_aw("mix_a_out_dx", dh1_b, W["a_w_out"], transpose_w=True, deps=toks)
    g_a_out = _mm_dw("mix_a_out_dw", gated, dh1_b, W["a_w_out"])
    pair_a_out, tok_a = exchange("a_out", [g_a_out])
    scat_w1_0, tok = reduce(pair_w1_0, dgated)
    early = [relay(state, dgated) for state in (scat_w2_1, scat_w1_1, scat_b)]
    dzpre, d_w_s, d_b_col, d_ln_g, d_ln_b = _gate_bwd("mix_a_gate_bwd", zpre, dgated, a_ln_g, a_ln_b, a_w_s[0], b_col)
    dhn0 = _mm_aw("mix_a_in_dx", dzpre, W["a_w_in"], transpose_w=True, deps=(tok, tok_a, *[t for _, t in early]))
    scat_a_out, tok = reduce(pair_a_out, dhn0)
    g_a_in = _mm_dw("mix_a_in_dw", hn0, dzpre, W["a_w_in"], deps=(tok,))
    pair_a_in, tok = exchange("a_in", [g_a_in])
    dx, _, d_norm_mix0 = _rms_bwd("mix_a_norm_bwd", x2, norm_mix[0:1], dhn0, dh1)
    scat_a_in, _ = reduce(pair_a_in, dx)

    small = dict(a_ln_g=(a_ln_g, m_a_ln_g, v_a_ln_g), a_ln_b=(a_ln_b, m_a_ln_b, v_a_ln_b),
                 a_w_s=(a_w_s, m_a_w_s, v_a_w_s), a_b_s=(a_b_s, m_a_b_s, v_a_b_s),
                 b_scale=(b_scale, m_b_scale, v_b_scale), norm_mix=(norm_mix, m_norm_mix, v_norm_mix),
                 norm_mlp=(norm_mlp, m_norm_mlp, v_norm_mlp), final_norm=(final_norm, m_final_norm, v_final_norm))
    small_names = list(small)
    local = dict(a_ln_g=d_ln_g, a_ln_b=d_ln_b, a_w_s=d_w_s[None], a_b_s=d_b_col.T[None], b_scale=d_scale,
                 norm_mix=jnp.concatenate([d_norm_mix0, d_norm_mix1], axis=0),
                 norm_mlp=jnp.concatenate([d_norm_mlp0, d_norm_mlp1], axis=0), final_norm=d_final.reshape(-1))
    device = (4 * xi + 2 * yi + ci).astype(jnp.int32).reshape(1)
    slots = _own_of_eight("small_own", _pack([local[k] for k in small_names]), device)
    small_send, small_recv, _, (slots,), small_tok = _split_start("small_start", [], [slots], _all_plan)

    moments = dict(a_w_in=(m_a_w_in, v_a_w_in), a_w_out=(m_a_w_out, v_a_w_out), b_w_in=(m_b_w_in, v_b_w_in),
                   b_w_grp=(m_b_w_grp, v_b_w_grp), b_w_out=(m_b_w_out, v_b_w_out),
                   mlp_w1=(m_mlp_w1, v_mlp_w1), mlp_w2=(m_mlp_w2, v_mlp_w2))
    weights = dict(a_w_in=a_w_in, a_w_out=a_w_out, b_w_in=b_w_in, b_w_grp=b_w_grp, b_w_out=b_w_out,
                   mlp_w1=mlp_w1, mlp_w2=mlp_w2)
    landing = [(scat_w2_1, [("mlp_w2", 1)]), (scat_w1_1, [("mlp_w1", 1)]),
               (scat_b, [("b_w_out", 0), ("b_w_grp", 0), ("b_w_in", 0)]),
               (scat_w2_0, [("mlp_w2", 0)]), (scat_w1_0, [("mlp_w1", 0)]),
               (scat_a_out, [("a_w_out", 0)]), (scat_a_in, [("a_w_in", 0)])]
    results, after = {}, dx
    relays = list(early)
    for i, (_, members) in enumerate(landing):
        deps = (small_tok,) if i == 0 else ()
        if len(relays) == i + 1 < len(landing):
            relays.append(relay(landing[i + 1][0], after))
            deps = (relays[-1][1],)
        for (k, layer), parts in zip(members, land(relays[i][0], relays[-1][1] if deps else after)):
            shard_shape = (-1, *parts.shape[2:])
            results[k] = _adam_shard(f"adam_{k}_{layer}", weights[k].reshape(shard_shape),
                                     moments[k][0].reshape(shard_shape), moments[k][1].reshape(shard_shape),
                                     parts, layer=layer, prev=results.get(k), deps=deps)
            after = results[k][1]
    grad_out, delta_out, m_out, v_out = {}, {}, {}, {}
    for k, res in results.items():
        grad_out[k], delta_out[k], m_out[k], v_out[k] = [r.reshape(weights[k].shape) for r in res]

    _, (slots,) = _split_wait("small_wait", [], [slots], small_send, small_recv, after, _all_whole)
    reduced = _sum_of_eight("small_sum", slots)
    small_grads = dict(zip(small_names, _unpack(reduced, [local[k].shape for k in small_names])))
    shard_w = b_scale.shape[1]
    small_grads["b_scale"] = lax.dynamic_slice_in_dim(small_grads["b_scale"], chip * shard_w, shard_w, axis=1)
    small_grads = {k: small_grads[k].reshape(small[k][0].shape) for k in small_names}
    packed = [_pack([small[k][i] for k in small_names]) for i in range(3)]
    res = _adam_packed("adam_small", packed[0], _pack([small_grads[k] for k in small_names]), packed[1], packed[2])
    shapes = [small[k][0].shape for k in small_names]
    for k, d, nm, nv in zip(small_names, *[_unpack(r, shapes) for r in res]):
        grad_out[k], delta_out[k], m_out[k], v_out[k] = small_grads[k], d, nm, nv

    loss = lax.psum(loss_part[0, 0], ("x", "y", "c"))
    order = ["a_w_in", "a_ln_g", "a_ln_b", "a_w_s", "a_b_s", "a_w_out", "b_w_in", "b_w_grp", "b_scale", "b_w_out",
             "norm_mix", "norm_mlp", "mlp_w1", "mlp_w2", "final_norm"]
    return (loss, dx[None], *[grad_out[k] for k in order], *[delta_out[k] for k in order],
            *[m_out[k] for k in order], *[v_out[k] for k in order])
```

```python
import math

import jax
import jax.numpy as jnp
from jax import lax
from jax.experimental import pallas as pl
from jax.experimental.pallas import tpu as pltpu

F32 = jnp.float32
BF16 = jnp.bfloat16
MESH = pl.DeviceIdType.MESH

EPS = 1e-6
B_WINDOWS = (2, 4, 8, 16)
ADAM_LR = 0.001
ADAM_B1 = 0.9
ADAM_B2 = 0.999
ADAM_EPS = 1e-08
ADAM_WD = 0.01
ADAM_STEP = 10

N_CHIPS = 4
N_DEV = 8
LANES = 128
PACK_ELEMS = 8 * LANES
VMEM_LIMIT = 56 * 1024 * 1024
ROW_TILE = 512
MM_TM, MM_TN, MM_TK = 1024, 1024, 2048


_ANY = pl.BlockSpec(memory_space=pl.ANY)


def _tile(dim, pref):
    t = min(dim, pref)
    while dim % t:
        t //= 2
    return t


def _params(*sem):
    return pltpu.CompilerParams(dimension_semantics=sem, vmem_limit_bytes=VMEM_LIMIT)


class _W4:
    def __init__(self, arr, col_sharded, shape=None):
        self.arr = arr
        self.nj, self.nl, self.r, self.c = arr.shape if shape is None else shape
        self.col = col_sharded
        self.rows = self.r if col_sharded else self.nj * self.r
        self.cols = self.nj * self.c if col_sharded else self.c

    def tile_rows(self, pref):
        return _tile(self.r, pref)

    def tile_cols(self, pref):
        return _tile(self.c, pref)

    def index(self, layer, rb, cb, tr, tc):
        if self.col:
            n = self.c // tc
            return (cb // n, layer, rb, cb % n)
        n = self.r // tr
        return (rb // n, layer, rb % n, cb)


def _mm_aw(name, a, w, *, layer=0, groups=1, transpose_w=False, extras=(), out_dtypes=(F32,), epilogue=None,
           deps=(), k_piece=None):
    s, ka_total = a.shape
    kdim, ndim = (w.cols, w.rows) if transpose_w else (w.rows, w.cols)
    assert ka_total == groups * kdim, (name, a.shape, kdim, groups)
    span = not w.col and ((not transpose_w and kdim <= MM_TK) or (transpose_w and groups > 1))
    if span and transpose_w:
        tm, tn, tk = _tile(s, 2048), ndim, w.tile_cols(MM_TK)
    else:
        tk = kdim if span else (w.tile_cols(MM_TK) if transpose_w else w.tile_rows(MM_TK))
        tm, tn_pref = (_tile(s, 2048), 512) if tk == kdim else (_tile(s, MM_TM), MM_TN)
        tn = w.tile_rows(tn_pref) if transpose_w else w.tile_cols(tn_pref)
    nk, nn = kdim // tk, ndim // tn
    if k_piece is not None:
        assert not (w.col or transpose_w or span or groups > 1), name
        piece, n_pieces = k_piece
        tm, tn, tk = _tile(s, MM_TM), w.tile_cols(MM_TN), w.r // (2 * n_pieces)
        nk, nn = 2 * w.nj, ndim // tn

        def k_block(k):
            return k // 2, (k % 2) * n_pieces + piece

    def lay(g):
        return g if groups > 1 else layer

    a_spec = pl.BlockSpec((tm, tk), lambda g, i, n, k: (i, g * nk + k))
    if k_piece is not None:
        per = w.r // tk
        a_spec = pl.BlockSpec((tm, tk), lambda g, i, n, k: (i, k_block(k)[0] * per + k_block(k)[1]))
        w_spec = pl.BlockSpec((None, None, tk, tn), lambda g, i, n, k: (k_block(k)[0], layer, k_block(k)[1], n))
    elif span and transpose_w:
        w_spec = pl.BlockSpec((w.nj, None, w.r, tk), lambda g, i, n, k: (0, lay(g), 0, k))
    elif span:
        w_spec = pl.BlockSpec((w.nj, None, w.r, tn), lambda g, i, n, k: (0, lay(g), 0, n))
    elif transpose_w:
        w_spec = pl.BlockSpec((None, None, tn, tk), lambda g, i, n, k: w.index(lay(g), n, k, tn, tk))
    else:
        w_spec = pl.BlockSpec((None, None, tk, tn), lambda g, i, n, k: w.index(lay(g), k, n, tk, tn))
    ex_specs = []
    for e in extras:
        assert e.shape[1] == groups * ndim and e.shape[0] in (1, s), (name, e.shape)
        if e.shape[0] == 1:
            ex_specs.append(pl.BlockSpec((1, tn), lambda g, i, n, k: (0, g * nn + n)))
        else:
            ex_specs.append(pl.BlockSpec((tm, tn), lambda g, i, n, k: (i, g * nn + n)))
    out_spec = pl.BlockSpec((tm, tn), lambda g, i, n, k: (i, g * nn + n))
    n_ex, n_out, n_dep = len(extras), len(out_dtypes), len(deps)

    def body(a_ref, w_ref, *rest):
        ex, outs = rest[:n_ex], rest[n_ex + n_dep:n_ex + n_dep + n_out]
        av = a_ref[...]
        if av.dtype != BF16:
            av = av.astype(BF16)
        wv = w_ref[...]
        if span:
            wv = wv.reshape((tn, tk) if transpose_w else (tk, tn))
        if transpose_w:
            prod = lax.dot_general(av, wv, (((1,), (1,)), ((), ())), preferred_element_type=F32)
        else:
            prod = jnp.dot(av, wv, preferred_element_type=F32)

        def finish(total):
            vals = (total,) if epilogue is None else epilogue(total, *[e[...] for e in ex])
            for o, v in zip(outs, vals):
                o[...] = v.astype(o.dtype)

        if nk == 1:
            finish(prod)
            return
        acc, k = rest[-1], pl.program_id(3)

        @pl.when(k == 0)
        def _():
            acc[...] = prod

        @pl.when(k > 0)
        def _():
            acc[...] += prod

        @pl.when(k == nk - 1)
        def _():
            finish(acc[...])

    outs = pl.pallas_call(
        body, name=name, grid=(groups, s // tm, nn, nk),
        in_specs=[a_spec, w_spec, *ex_specs] + [_ANY] * n_dep, out_specs=[out_spec] * n_out,
        out_shape=[jax.ShapeDtypeStruct((s, groups * ndim), dt) for dt in out_dtypes],
        scratch_shapes=[pltpu.VMEM((tm, tn), F32)] if nk > 1 else [],
        compiler_params=_params("parallel", "parallel", "parallel", "arbitrary"),
    )(a, w.arr, *extras, *deps)
    return outs[0] if n_out == 1 else outs


def _mm_dw(name, a, b, like, *, layer=0, groups=1, deps=()):
    s, ka_total = a.shape
    rows, cols = ka_total // groups, b.shape[1] // groups
    assert (rows, cols) == (like.rows, like.cols) and b.shape[0] == s, (name, a.shape, b.shape)
    span = groups > 1 and not like.col
    tm, tn, tk = rows if span else like.tile_rows(MM_TM), like.tile_cols(MM_TN), _tile(s, MM_TK)
    nr, nc, nk = rows // tm, cols // tn, s // tk
    assert nk == 1 or not span

    def lay(g):
        return g if groups > 1 else layer

    in_specs = [pl.BlockSpec((tk, tm), lambda g, n, i, k: (k, g * nr + i)),
                pl.BlockSpec((tk, tn), lambda g, n, i, k: (k, g * nc + n))]
    in_specs += [_ANY] * len(deps)

    def body(a_ref, b_ref, *rest):
        av, bv = a_ref[...], b_ref[...]
        if av.dtype != BF16:
            av = av.astype(BF16)
        if bv.dtype != BF16:
            bv = bv.astype(BF16)
        prod = lax.dot_general(av, bv, (((0,), (0,)), ((), ())), preferred_element_type=F32)
        if nk == 1:
            rest[-1][...] = prod.astype(BF16).reshape(rest[-1].shape)
            return
        o_ref, acc, k = rest[-2], rest[-1], pl.program_id(3)

        @pl.when(k == 0)
        def _():
            acc[...] = prod

        @pl.when(k > 0)
        def _():
            acc[...] += prod

        @pl.when(k == nk - 1)
        def _():
            o_ref[...] = acc[...].astype(BF16)

    if span:
        out_spec = pl.BlockSpec((like.nj, None, like.r, tn), lambda g, n, i, k: (0, g, 0, n))
    else:
        out_spec = pl.BlockSpec((None, None, tm, tn), lambda g, n, i, k: like.index(lay(g), i, n, tm, tn))
    return pl.pallas_call(
        body, name=name, grid=(groups, nc, nr, nk), in_specs=in_specs, out_specs=out_spec,
        out_shape=jax.ShapeDtypeStruct((like.nj, like.nl, like.r, like.c), BF16),
        scratch_shapes=[pltpu.VMEM((tm, tn), F32)] if nk > 1 else [],
        compiler_params=_params("parallel", "parallel", "parallel", "arbitrary"),
    )(a, b, *deps)


def _row_spec(tr, d):
    return pl.BlockSpec((tr, d), lambda i: (i, 0))


def _vec_spec(d):
    return pl.BlockSpec((1, d), lambda i: (0, 0))


def _rms_fwd(name, x, g):
    s, d = x.shape
    tr = _tile(s, ROW_TILE)

    def body(x_ref, g_ref, o_ref):
        xv = x_ref[...]
        r = lax.rsqrt(jnp.mean(xv * xv, axis=-1, keepdims=True) + EPS)
        o_ref[...] = (xv * r * g_ref[...]).astype(BF16)

    return pl.pallas_call(
        body, name=name, grid=(s // tr,), in_specs=[_row_spec(tr, d), _vec_spec(d)],
        out_specs=_row_spec(tr, d), out_shape=jax.ShapeDtypeStruct((s, d), BF16),
        compiler_params=_params("parallel"),
    )(x, g)


def _rms_bwd(name, x, g, dhn, dres):
    s, d = x.shape
    tr = _tile(s, ROW_TILE)

    def body(x_ref, g_ref, dhn_ref, dres_ref, dx_ref, dxb_ref, dg_ref):
        @pl.when(pl.program_id(0) == 0)
        def _():
            dg_ref[...] = jnp.zeros_like(dg_ref)

        xv = x_ref[...]
        r = lax.rsqrt(jnp.mean(xv * xv, axis=-1, keepdims=True) + EPS)
        xh = xv * r
        dy = dhn_ref[...]
        dg_ref[...] += jnp.sum(dy * xh, axis=0, keepdims=True)
        dxh = dy * g_ref[...]
        dx = dres_ref[...] + r * (dxh - xh * jnp.mean(dxh * xh, axis=-1, keepdims=True))
        dx_ref[...] = dx
        dxb_ref[...] = dx.astype(BF16)

    return pl.pallas_call(
        body, name=name, grid=(s // tr,),
        in_specs=[_row_spec(tr, d), _vec_spec(d), _row_spec(tr, d), _row_spec(tr, d)],
        out_specs=[_row_spec(tr, d), _row_spec(tr, d), _vec_spec(d)],
        out_shape=[jax.ShapeDtypeStruct((s, d), F32), jax.ShapeDtypeStruct((s, d), BF16),
                   jax.ShapeDtypeStruct((1, d), F32)],
        compiler_params=_params("arbitrary"),
    )(x, g, dhn, dres)


def _final(name, h, g, tgt):
    s, d = h.shape
    tr = _tile(s, ROW_TILE)

    def body(h_ref, g_ref, t_ref, dh_ref, dhb_ref, dg_ref, loss_ref):
        @pl.when(pl.program_id(0) == 0)
        def _():
            dg_ref[...] = jnp.zeros_like(dg_ref)
            loss_ref[...] = jnp.zeros_like(loss_ref)

        hv = h_ref[...]
        r = lax.rsqrt(jnp.mean(hv * hv, axis=-1, keepdims=True) + EPS)
        xh = hv * r
        gv = g_ref[...]
        err = xh * gv - t_ref[...]
        part = 0.5 * jnp.sum(jnp.mean(err * err, axis=-1, keepdims=True), axis=0, keepdims=True)
        loss_ref[...] += jnp.broadcast_to(part, loss_ref.shape)
        dy = err * (1.0 / d)
        dg_ref[...] += jnp.sum(dy * xh, axis=0, keepdims=True)
        dxh = dy * gv
        dh = r * (dxh - xh * jnp.mean(dxh * xh, axis=-1, keepdims=True))
        dh_ref[...] = dh
        dhb_ref[...] = dh.astype(BF16)

    return pl.pallas_call(
        body, name=name, grid=(s // tr,),
        in_specs=[_row_spec(tr, d), _vec_spec(d), _row_spec(tr, d)],
        out_specs=[_row_spec(tr, d), _row_spec(tr, d), _vec_spec(d), _vec_spec(LANES)],
        out_shape=[jax.ShapeDtypeStruct((s, d), F32), jax.ShapeDtypeStruct((s, d), BF16),
                   jax.ShapeDtypeStruct((1, d), F32), jax.ShapeDtypeStruct((1, LANES), F32)],
        compiler_params=_params("arbitrary"),
    )(h, g, tgt)


_SQRT_HALF = 1.0 / math.sqrt(2.0)
_INV_SQRT_2PI = 1.0 / math.sqrt(2.0 * math.pi)


def _gelu(x):
    return x * (lax.erf(x * _SQRT_HALF) + 1.0) * 0.5


def _gelu_grad(x):
    return 0.5 * (lax.erf(x * _SQRT_HALF) + 1.0) + x * jnp.exp(-0.5 * x * x) * _INV_SQRT_2PI


def _causal(chunk):
    t = lax.broadcasted_iota(jnp.int32, (chunk, chunk), 0)
    sidx = lax.broadcasted_iota(jnp.int32, (chunk, chunk), 1)
    return sidx <= t


def _layernorm_parts(v, g, b):
    mu = jnp.mean(v, axis=-1, keepdims=True)
    vc = v - mu
    rs = lax.rsqrt(jnp.mean(vc * vc, axis=-1, keepdims=True) + EPS)
    vhat = vc * rs
    return vhat, rs, vhat * g + b


def _gate_fwd(name, zpre, ln_g, ln_b, w_s, b_col):
    s, aw2 = zpre.shape
    aw = aw2 // 2
    ng, chunk, _ = w_s.shape
    dh = aw // ng

    def body(z_ref, g_ref, b_ref, ws_ref, bc_ref, o_ref):
        u = _gelu(z_ref[:, :aw])
        v = _gelu(z_ref[:, aw:])
        _, _, vln = _layernorm_parts(v, g_ref[...], b_ref[...])
        mask = _causal(chunk)
        for gi in range(ng):
            sl = slice(gi * dh, (gi + 1) * dh)
            wm = jnp.where(mask, ws_ref[gi], 0.0).astype(BF16)
            sg = jnp.dot(wm, vln[:, sl].astype(BF16), preferred_element_type=F32) + bc_ref[:, gi:gi + 1]
            o_ref[:, sl] = (u[:, sl] * sg).astype(BF16)

    return pl.pallas_call(
        body, name=name, grid=(s // chunk,),
        in_specs=[_row_spec(chunk, aw2), _vec_spec(aw), _vec_spec(aw),
                  pl.BlockSpec((ng, chunk, chunk), lambda i: (0, 0, 0)),
                  pl.BlockSpec((chunk, ng), lambda i: (0, 0))],
        out_specs=_row_spec(chunk, aw), out_shape=jax.ShapeDtypeStruct((s, aw), BF16),
        compiler_params=_params("parallel"),
    )(zpre, ln_g, ln_b, w_s, b_col)


def _gate_bwd(name, zpre, dgated, ln_g, ln_b, w_s, b_col):
    s, aw2 = zpre.shape
    aw = aw2 // 2
    ng, chunk, _ = w_s.shape
    dh = aw // ng

    def body(z_ref, dgt_ref, g_ref, b_ref, ws_ref, bc_ref, dz_ref, dws_ref, dbc_ref, dlg_ref, dlb_ref,
             du_scr, dvln_scr):
        @pl.when(pl.program_id(0) == 0)
        def _():
            dws_ref[...] = jnp.zeros_like(dws_ref)
            dbc_ref[...] = jnp.zeros_like(dbc_ref)
            dlg_ref[...] = jnp.zeros_like(dlg_ref)
            dlb_ref[...] = jnp.zeros_like(dlb_ref)

        zu = z_ref[:, :aw]
        zv = z_ref[:, aw:]
        u = _gelu(zu)
        lg = g_ref[...]
        vhat, rs, vln = _layernorm_parts(_gelu(zv), lg, b_ref[...])
        mask = _causal(chunk)
        for gi in range(ng):
            sl = slice(gi * dh, (gi + 1) * dh)
            wm = jnp.where(mask, ws_ref[gi], 0.0).astype(BF16)
            vg = vln[:, sl].astype(BF16)
            sg = jnp.dot(wm, vg, preferred_element_type=F32) + bc_ref[:, gi:gi + 1]
            dgt = dgt_ref[:, sl]
            du_scr[:, sl] = dgt * sg
            ds = dgt * u[:, sl]
            dbc_ref[:, gi:gi + 1] += jnp.sum(ds, axis=-1, keepdims=True)
            dsb = ds.astype(BF16)
            dwm = lax.dot_general(dsb, vg, (((1,), (1,)), ((), ())), preferred_element_type=F32)
            dws_ref[gi] += jnp.where(mask, dwm, 0.0)
            dvln_scr[:, sl] = lax.dot_general(wm, dsb, (((0,), (0,)), ((), ())), preferred_element_type=F32)
        dvln = dvln_scr[...]
        dlb_ref[...] += jnp.sum(dvln, axis=0, keepdims=True)
        dlg_ref[...] += jnp.sum(dvln * vhat, axis=0, keepdims=True)
        dvh = dvln * lg
        dv = rs * (dvh - jnp.mean(dvh, axis=-1, keepdims=True)
                   - vhat * jnp.mean(dvh * vhat, axis=-1, keepdims=True))
        dz_ref[:, :aw] = (du_scr[...] * _gelu_grad(zu)).astype(BF16)
        dz_ref[:, aw:] = (dv * _gelu_grad(zv)).astype(BF16)

    return pl.pallas_call(
        body, name=name, grid=(s // chunk,),
        in_specs=[_row_spec(chunk, aw2), _row_spec(chunk, aw), _vec_spec(aw), _vec_spec(aw),
                  pl.BlockSpec((ng, chunk, chunk), lambda i: (0, 0, 0)),
                  pl.BlockSpec((chunk, ng), lambda i: (0, 0))],
        out_specs=[_row_spec(chunk, aw2), pl.BlockSpec((ng, chunk, chunk), lambda i: (0, 0, 0)),
                   pl.BlockSpec((chunk, ng), lambda i: (0, 0)), _vec_spec(aw), _vec_spec(aw)],
        out_shape=[jax.ShapeDtypeStruct((s, aw2), BF16), jax.ShapeDtypeStruct((ng, chunk, chunk), F32),
                   jax.ShapeDtypeStruct((chunk, ng), F32), jax.ShapeDtypeStruct((1, aw), F32),
                   jax.ShapeDtypeStruct((1, aw), F32)],
        scratch_shapes=[pltpu.VMEM((chunk, aw), F32), pltpu.VMEM((chunk, aw), F32)],
        compiler_params=_params("arbitrary"),
    )(zpre, dgated, ln_g, ln_b, w_s, b_col)


def _pool_select(g, parts):
    out = parts[-1]
    for gi in range(len(parts) - 2, -1, -1):
        out = jnp.where(g == gi, parts[gi], out)
    return out


def _pool_specs(s, bw):
    head = bw // len(B_WINDOWS)
    tc = _tile(head, 256)
    nb = head // tc
    return tc, (len(B_WINDOWS), nb), pl.BlockSpec((s, tc), lambda g, j: (0, g * nb + j))


def _pool_window(g, t):
    w = _pool_select(g, [jnp.full(t.shape, wi, jnp.int32) for wi in B_WINDOWS])
    return jnp.minimum(t + 1, w).astype(F32)


def _pool_fwd(name, vb):
    assert B_WINDOWS == (2, 4, 8, 16)
    s, bw = vb.shape
    tc, grid, spec = _pool_specs(s, bw)

    def body(v_ref, o_ref):
        g = pl.program_id(0)
        v = v_ref[...]
        t = lax.broadcasted_iota(jnp.int32, (s, tc), 0)

        def down(x, k):
            return jnp.where(t >= k, pltpu.roll(x, k, 0), 0.0)

        sums, cur, k = [], v, 1
        for _ in B_WINDOWS:
            cur = cur + down(cur, k)
            sums.append(cur)
            k *= 2
        o_ref[...] = (_pool_select(g, sums) / _pool_window(g, t) - v).astype(BF16)

    return pl.pallas_call(
        body, name=name, grid=grid, in_specs=[spec], out_specs=spec,
        out_shape=jax.ShapeDtypeStruct((s, bw), BF16), compiler_params=_params("parallel", "parallel"),
    )(vb)


def _pool_bwd(name, dpooled):
    s, bw = dpooled.shape
    tc, grid, spec = _pool_specs(s, bw)

    def body(d_ref, o_ref):
        g = pl.program_id(0)
        dp = d_ref[...]
        t = lax.broadcasted_iota(jnp.int32, (s, tc), 0)

        def up(x, k):
            return jnp.where(t < s - k, pltpu.roll(x, s - k, 0), 0.0)

        sums, cur, k = [], dp / _pool_window(g, t), 1
        for _ in B_WINDOWS:
            cur = cur + up(cur, k)
            sums.append(cur)
            k *= 2
        o_ref[...] = (_pool_select(g, sums) - dp).astype(BF16)

    return pl.pallas_call(
        body, name=name, grid=grid, in_specs=[spec], out_specs=spec,
        out_shape=jax.ShapeDtypeStruct((s, bw), BF16), compiler_params=_params("parallel", "parallel"),
    )(dpooled)


def _scale_bwd(name, dms, mixed, scale):
    s, bw = dms.shape
    tr = _tile(s, ROW_TILE)

    def body(d_ref, m_ref, sc_ref, o_ref, ds_ref):
        @pl.when(pl.program_id(0) == 0)
        def _():
            ds_ref[...] = jnp.zeros_like(ds_ref)

        dv = d_ref[...]
        ds_ref[...] += jnp.sum(dv * m_ref[...], axis=0, keepdims=True)
        o_ref[...] = (dv * sc_ref[...]).astype(BF16)

    return pl.pallas_call(
        body, name=name, grid=(s // tr,), in_specs=[_row_spec(tr, bw), _row_spec(tr, bw), _vec_spec(bw)],
        out_specs=[_row_spec(tr, bw), _vec_spec(bw)],
        out_shape=[jax.ShapeDtypeStruct((s, bw), BF16), jax.ShapeDtypeStruct((1, bw), F32)],
        compiler_params=_params("arbitrary"),
    )(dms, mixed, scale)


def _adam_update(w, g, m, v):
    m = ADAM_B1 * m + (1.0 - ADAM_B1) * g
    v = ADAM_B2 * v + (1.0 - ADAM_B2) * (g * g)
    m_hat = m / (1.0 - ADAM_B1 ** ADAM_STEP)
    v_hat = v / (1.0 - ADAM_B2 ** ADAM_STEP)
    delta = -ADAM_LR * (m_hat / (jnp.sqrt(v_hat) + ADAM_EPS) + ADAM_WD * w)
    return delta, m, v


def _adam_shard(name, w, m, v, parts, layer=0, prev=None, deps=()):
    nl, r, c = w.shape
    nj, nlp = parts.shape[:2]
    tr = _tile(r, ROW_TILE // 2)
    spec = pl.BlockSpec((None, tr, c), lambda l, i: (layer + l, i, 0))

    def body(w_ref, m_ref, v_ref, p_ref, *rest):
        g_ref, d_ref, nm_ref, nv_ref = rest[-4:]
        g = p_ref[0].astype(F32)
        for j in range(1, nj):
            g = g + p_ref[j].astype(F32)
        delta, nm, nv = _adam_update(w_ref[...], g, m_ref[...], v_ref[...])
        g_ref[...] = g
        d_ref[...] = delta
        nm_ref[...] = nm
        nv_ref[...] = nv

    prev = () if prev is None else tuple(prev)
    return pl.pallas_call(
        body, name=name, grid=(nlp, r // tr),
        in_specs=[spec, spec, spec, pl.BlockSpec((nj, None, tr, c), lambda l, i: (0, l, i, 0))]
        + [_ANY] * (len(prev) + len(deps)),
        out_specs=[spec] * 4, out_shape=[jax.ShapeDtypeStruct(w.shape, F32)] * 4,
        input_output_aliases={4 + i: i for i in range(len(prev))},
        compiler_params=_params("parallel", "parallel"),
    )(w, m, v, parts, *prev, *deps)


def _adam_packed(name, w, g, m, v):
    rows, lanes = w.shape
    tr = rows
    spec = pl.BlockSpec((tr, lanes), lambda i: (i, 0))

    def body(w_ref, g_ref, m_ref, v_ref, d_ref, nm_ref, nv_ref):
        delta, nm, nv = _adam_update(w_ref[...], g_ref[...], m_ref[...], v_ref[...])
        d_ref[...] = delta
        nm_ref[...] = nm
        nv_ref[...] = nv

    return pl.pallas_call(
        body, name=name, grid=(rows // tr,), in_specs=[spec] * 4, out_specs=[spec] * 3,
        out_shape=[jax.ShapeDtypeStruct(w.shape, F32)] * 3, compiler_params=_params("parallel"),
    )(w, g, m, v)


def _place():
    x, y, c = lax.axis_index("x"), lax.axis_index("y"), lax.axis_index("c")
    chips = [(1 - x, y), (x, 1 - y), (1 - x, 1 - y)]
    return x, y, c, chips


def _remote(src, dst, send_sem, recv_sem, device):
    return pltpu.make_async_remote_copy(src_ref=src, dst_ref=dst, send_sem=send_sem, recv_sem=recv_sem,
                                        device_id=device, device_id_type=MESH)


def _half(ref_rows, cc):
    h = ref_rows // 2
    return pl.ds(cc * h, h)


_HBM = pl.BlockSpec(memory_space=pltpu.HBM)
_SEM = pl.BlockSpec(memory_space=pltpu.SEMAPHORE)
_EFFECT = pltpu.SideEffectType.DATAFLOW_SIDE_EFFECTING


def _in_hbm(arr):
    return pltpu.with_memory_space_constraint(arr, pltpu.HBM)


def _gather_rows(land, cc, part=(0, 1)):
    k, n = part
    h = land.shape[2] // 2
    return pl.ds(cc * h + k * (h // n), h // n)


def _gather_block(land, split, j, cc, part=(0, 1)):
    if split:
        return land.at[j, :, _gather_rows(land, cc, part), :]
    return land.at[j]


def _split_start(name, srcs, lands, plan, deps=(), groups=None):
    ns = len(srcs)
    nl = len(lands) if groups is None else groups
    both = list(srcs) + list(lands)
    nb = len(both)

    def body(*refs):
        s, ld = refs[:ns], refs[ns:nb]
        outs = refs[nb + len(deps):]
        send, recv, token = outs[:nl], outs[nl:2 * nl], outs[-1]
        for a, copies in enumerate(plan(s, ld)):
            for src, dst, peer in copies:
                _remote(src, dst, send[a], recv[a], peer).start()
        token[...] = jnp.zeros_like(token)

    outs = pl.pallas_call(
        body, name=name, in_specs=[_HBM] * nb + [_ANY] * len(deps),
        out_specs=[_SEM] * (2 * nl) + [_HBM] * nb + [pl.BlockSpec(memory_space=pltpu.VMEM)],
        out_shape=[pltpu.SemaphoreType.DMA(())] * (2 * nl) + [pltpu.HBM(b.shape, b.dtype) for b in both]
        + [jax.ShapeDtypeStruct((8, LANES), F32)],
        input_output_aliases={i: 2 * nl + i for i in range(nb)},
        compiler_params=pltpu.CompilerParams(has_side_effects=_EFFECT),
    )(*[_in_hbm(b) for b in both], *deps)
    thru = outs[2 * nl:2 * nl + nb]
    return list(outs[:nl]), list(outs[nl:2 * nl]), list(thru[:ns]), list(thru[ns:]), outs[-1]


def _split_wait(name, srcs, lands, send, recv, after, whole):
    ns, nl = len(srcs), len(send)
    both = list(srcs) + list(lands)
    nb = len(both)

    def body(*refs):
        ld, snd, rcv = refs[ns:nb], refs[nb:nb + nl], refs[nb + nl:nb + 2 * nl]
        x, y, c, _ = _place()
        for a, blk in enumerate(whole(ld)):
            every = _remote(blk, blk, snd[a], rcv[a], (x, y, c))
            every.wait_send()
            every.wait_recv()

    outs = pl.pallas_call(
        body, name=name, in_specs=[_HBM] * nb + [_SEM] * (2 * nl) + [_ANY], out_specs=[_HBM] * nb,
        out_shape=[pltpu.HBM(b.shape, b.dtype) for b in both],
        input_output_aliases={i: i for i in range(nb)},
        compiler_params=pltpu.CompilerParams(has_side_effects=_EFFECT),
    )(*both, *send, *recv, after)
    return list(outs[:ns]), list(outs[ns:])


def _gather_plan(split, parts):
    def plan(srcs, lands):
        x, y, c, chips = _place()
        out = []
        for ld, sp, n in zip(lands, split, parts):
            for k in range(n):
                blk = _gather_block(ld, sp, 2 * x + y, c, (k, n))
                out.append([(blk, blk, (qx, qy, c)) for qx, qy in chips])
        return out
    return plan


def _gather_whole(split, part=(0, 1)):
    def whole(lands):
        _, _, c, _ = _place()
        return [ld.at[pl.ds(0, 3), :, _gather_rows(ld, c, part), :] if sp else ld.at[pl.ds(0, 3)]
                for ld, sp in zip(lands, split)]
    return whole


def _relay_plan(srcs, lands):
    x, y, c, _ = _place()
    out = []
    for ld in lands:
        blocks = [ld.at[j, :, _half(ld.shape[2], c), :] for j in range(N_CHIPS)]
        out.append([(blk, blk, (x, y, 1 - c)) for blk in blocks])
    return out


def _relay_whole(lands):
    _, _, c, _ = _place()
    return [ld.at[:, :, _half(ld.shape[2], c), :] for ld in lands]


def _fill_own(name, shard, dtype, place, layer=None):
    nl, r, c = shard.shape
    first = 0
    if layer is not None:
        nl, first = 1, layer
    tr = _tile(r, 512)

    def body(p_ref, s_ref, o_ref):
        o_ref[...] = s_ref[...].astype(o_ref.dtype)

    return pl.pallas_call(
        body, name=name,
        grid_spec=pltpu.PrefetchScalarGridSpec(
            num_scalar_prefetch=1, grid=(nl, r // tr),
            in_specs=[pl.BlockSpec((None, tr, c), lambda l, i, p: (first + l, i, 0))],
            out_specs=pl.BlockSpec((None, None, tr, c), lambda l, i, p: (p[0], l, i, 0))),
        out_shape=jax.ShapeDtypeStruct((N_CHIPS, nl, r, c), dtype),
        compiler_params=_params("parallel", "parallel"),
    )(place, shard)


def _gather_finish(name, lands, part=(0, 1)):
    n = len(lands)

    def body(*refs):
        outs = refs[n:2 * n]
        fsend, frecv = refs[2 * n:]
        x, y, c, chips = _place()
        sib = (x, y, 1 - c)

        def relay(a, qi, cc):
            qx, qy = chips[qi]
            blk = _gather_block(outs[a], True, 2 * qx + qy, cc, part)
            return _remote(blk, blk, fsend.at[a, qi], frecv.at[a, qi], sib)

        relays = [relay(a, qi, c) for a in range(n) for qi in range(3)]
        for cp in relays:
            cp.start()
        for a in range(n):
            for qi in range(3):
                relay(a, qi, 1 - c).wait_recv()
        for cp in relays:
            cp.wait_send()

    outs = pl.pallas_call(
        body, name=name, in_specs=[_ANY] * n, out_specs=[_ANY] * n,
        out_shape=[jax.ShapeDtypeStruct(ld.shape, ld.dtype) for ld in lands],
        input_output_aliases={i: i for i in range(n)},
        scratch_shapes=[pltpu.SemaphoreType.DMA((n, 3))] * 2,
    )(*lands)
    return list(outs)


def _pair_plan(srcs, lands):
    x, y, c, _ = _place()
    return [[(s.at[:, :, _half(s.shape[2], 1 - c), :], ld, (x, y, 1 - c))] for s, ld in zip(srcs, lands)]


def _pair_whole(lands):
    return list(lands)


def _scatter_plan(srcs, lands):
    x, y, c, chips = _place()
    return [[(s.at[2 * qx + qy], ld.at[2 * x + y, :, _half(ld.shape[2], c), :], (qx, qy, c)) for qx, qy in chips]
            for s, ld in zip(srcs, lands)]


def _scatter_whole(lands):
    _, _, c, _ = _place()
    return [ld.at[pl.ds(0, 3), :, _half(ld.shape[2], c), :] for ld in lands]


def _pair_sum(name, grad, other, place):
    nj, nl, r, c = grad.shape
    h = r // 2
    tr = _tile(h, 2 * ROW_TILE)
    nb = h // tr

    def body(p_ref, g_ref, o_ref, q_ref, d_ref):
        q = (g_ref[...].astype(F32) + o_ref[...].astype(F32)).astype(BF16)
        q_ref[...] = q

        @pl.when(pl.program_id(2) == p_ref[0])
        def _():
            d_ref[...] = q

    blk = (None, None, tr, c)
    return pl.pallas_call(
        body, name=name,
        grid_spec=pltpu.PrefetchScalarGridSpec(
            num_scalar_prefetch=1, grid=(nl, nb, nj),
            in_specs=[pl.BlockSpec(blk, lambda l, i, j, p: (j, l, p[1] * nb + i, 0)),
                      pl.BlockSpec(blk, lambda l, i, j, p: (j, l, i, 0))],
            out_specs=[pl.BlockSpec(blk, lambda l, i, j, p: (j, l, i, 0)),
                       pl.BlockSpec(blk, lambda l, i, j, p: (p[0], l, p[1] * nb + i, 0))]),
        out_shape=[jax.ShapeDtypeStruct((nj, nl, h, c), BF16), jax.ShapeDtypeStruct((nj, nl, r, c), BF16)],
        compiler_params=_params("parallel", "parallel", "arbitrary"),
    )(place, grad, other)


def _own_of_eight(name, packed, device):
    rows, lanes = packed.shape
    tr = rows

    def body(d_ref, s_ref, o_ref):
        o_ref[...] = s_ref[...]

    return pl.pallas_call(
        body, name=name,
        grid_spec=pltpu.PrefetchScalarGridSpec(
            num_scalar_prefetch=1, grid=(rows // tr,),
            in_specs=[pl.BlockSpec((tr, lanes), lambda i, d: (i, 0))],
            out_specs=pl.BlockSpec((None, tr, lanes), lambda i, d: (d[0], i, 0))),
        out_shape=jax.ShapeDtypeStruct((N_DEV, rows, lanes), packed.dtype),
        compiler_params=_params("parallel"),
    )(device, packed)


def _all_plan(srcs, lands):
    x, y, c, _ = _place()
    (ld,) = lands
    blk = ld.at[4 * x + 2 * y + c]
    flips = [(a, b, d) for a in (0, 1) for b in (0, 1) for d in (0, 1) if a + b + d]
    return [[(blk, blk, (x + a - 2 * a * x, y + b - 2 * b * y, c + d - 2 * d * c)) for a, b, d in flips]]


def _all_whole(lands):
    return [lands[0].at[pl.ds(0, N_DEV - 1)]]


def _sum_of_eight(name, slots):
    n, rows, lanes = slots.shape
    tr = rows

    def body(s_ref, o_ref):
        total = s_ref[0]
        for d in range(1, n):
            total = total + s_ref[d]
        o_ref[...] = total

    return pl.pallas_call(
        body, name=name, grid=(rows // tr,),
        in_specs=[pl.BlockSpec((n, tr, lanes), lambda i: (0, i, 0))],
        out_specs=pl.BlockSpec((tr, lanes), lambda i: (i, 0)),
        out_shape=jax.ShapeDtypeStruct((rows, lanes), F32), compiler_params=_params("parallel"),
    )(slots)


def _pack(parts):
    rows = []
    for p in parts:
        flat = p.reshape(-1)
        pad = (-flat.shape[0]) % PACK_ELEMS
        rows.append(jnp.pad(flat, (0, pad)).reshape(-1, LANES))
    return jnp.concatenate(rows, axis=0)


def _unpack(packed, shapes):
    out, row = [], 0
    for sh in shapes:
        size = math.prod(sh)
        nrows = -(-size // PACK_ELEMS) * (PACK_ELEMS // LANES)
        out.append(packed[row:row + nrows].reshape(-1)[:size].reshape(sh))
        row += nrows
    return out


def kernel(x, a_w_in, a_ln_g, a_ln_b, a_w_s, a_b_s, a_w_out, b_w_in, b_w_grp, b_scale, b_w_out, norm_mix, norm_mlp, mlp_w1, mlp_w2, final_norm, loss_target, m_a_w_in, m_a_ln_g, m_a_ln_b, m_a_w_s, m_a_b_s, m_a_w_out, m_b_w_in, m_b_w_grp, m_b_scale, m_b_w_out, m_norm_mix, m_norm_mlp, m_mlp_w1, m_mlp_w2, m_final_norm, v_a_w_in, v_a_ln_g, v_a_ln_b, v_a_w_s, v_a_b_s, v_a_w_out, v_b_w_in, v_b_w_grp, v_b_scale, v_b_w_out, v_norm_mix, v_norm_mlp, v_mlp_w1, v_mlp_w2, v_final_norm):
    xi, yi, ci = lax.axis_index("x"), lax.axis_index("y"), lax.axis_index("c")
    chip = 2 * xi + yi
    place = jnp.stack([chip, ci]).astype(jnp.int32)
    x2, tgt = x[0], loss_target[0]
    bw = b_scale.shape[1] * N_CHIPS

    units = dict(a_w_in=(a_w_in, None), a_w_out=(a_w_out, None), w1_0=(mlp_w1, 0), w2_0=(mlp_w2, 0),
                 b_scale=(b_scale.reshape(1, 1, -1), None), b_w_in=(b_w_in, None), b_w_grp=(b_w_grp[0], None),
                 b_w_out=(b_w_out, None), w1_1=(mlp_w1, 1), w2_1=(mlp_w2, 1))
    col_sharded = dict(a_w_in=True, a_w_out=False, b_w_in=False, b_w_grp=False, b_w_out=False,
                       w1_0=True, w2_0=False, w1_1=True, w2_1=False)
    in_flight, W = {}, {}

    def launch(tag, keys, deps):
        sp = [k != "b_scale" for k in keys]
        parts = [pieces.get(k, 1) for k in keys]
        zones = [_fill_own(f"gather_own_{k}", units[k][0], BF16 if s else F32, place, layer=units[k][1])
                 for k, s in zip(keys, sp)]
        send, recv, _, zones, tok = _split_start(f"gather_start_{tag}", [], zones, _gather_plan(sp, parts), deps,
                                                 groups=sum(parts))
        first = 0
        for k, z, s, n in zip(keys, zones, sp, parts):
            in_flight[k] = (send[first:first + n], recv[first:first + n], z, s)
            first += n
        return tok

    def arrive_piece(key, k, after):
        send, recv, zone, sp = in_flight[key]
        n = len(send)
        _, zones = _split_wait(f"gather_wait_{key}_{k}", [], [zone], [send[k]], [recv[k]], after,
                               _gather_whole([sp], (k, n)))
        (zone,) = _gather_finish(f"gather_finish_{key}_{k}", zones, (k, n))
        in_flight[key] = (send, recv, zone, sp)
        W[key] = _W4(zone, col_sharded[key])

    def arrive(keys, after):
        send, recv, zones, sp = zip(*[in_flight[k] for k in keys])
        if len(send[0]) > 1:
            for k in range(len(send[0])):
                arrive_piece(keys[0], k, after)
            return
        _, zones = _split_wait(f"gather_wait_{keys[0]}", [], zones, [s[0] for s in send], [r[0] for r in recv],
                               after, _gather_whole(sp))
        relayed = iter(_gather_finish(f"gather_finish_{keys[0]}", [z for z, s in zip(zones, sp) if s]))
        for k, z, s in zip(keys, zones, sp):
            full = next(relayed) if s else z
            W[k] = _W4(full, col_sharded[k]) if k in col_sharded else full

    pieces = dict(w1_0=2, w2_0=2, w1_1=2, w2_1=2)

    token = launch("first", ["a_w_in", "a_w_out"], ())
    token = launch("rest", ["w1_0", "w2_0", "b_scale", "b_w_in", "b_w_grp", "b_w_out", "w1_1", "w2_1"], (token,))

    b_col = a_b_s[0].T

    def residual(acc, res):
        return (res + acc,)

    def sq_relu(acc):
        act = jnp.maximum(acc, 0.0)
        return act, act * act

    def mlp_fwd(tag, h, layer):
        hn = _rms_fwd(f"mlp{tag}_norm", h, norm_mlp[layer:layer + 1])
        arrive([f"w1_{layer}"], hn)
        act, act_sq = _mm_aw(f"mlp{tag}_up", hn, W[f"w1_{layer}"], out_dtypes=(BF16, BF16), epilogue=sq_relu)
        out = h
        for k in range(pieces[f"w2_{layer}"]):
            arrive_piece(f"w2_{layer}", k, act_sq if k == 0 else out)
            out = _mm_aw(f"mlp{tag}_down_{k}", act_sq, W[f"w2_{layer}"], extras=(out,), epilogue=residual,
                         k_piece=(k, pieces[f"w2_{layer}"]))
        return out, (h, hn, act, act_sq)

    hn0 = _rms_fwd("mix_a_norm", x2, norm_mix[0:1])
    arrive(["a_w_in"], token)
    zpre = _mm_aw("mix_a_in", hn0, W["a_w_in"])
    gated = _gate_fwd("mix_a_gate", zpre, a_ln_g, a_ln_b, a_w_s[0], b_col)
    arrive(["a_w_out"], gated)
    h1 = _mm_aw("mix_a_out", gated, W["a_w_out"], extras=(x2,), epilogue=residual)
    h2, mlp0 = mlp_fwd("0", h1, 0)
    hn2 = _rms_fwd("mix_b_norm", h2, norm_mix[1:2])
    arrive(["b_scale", "b_w_in"], hn2)
    scale_full = W["b_scale"].reshape(1, bw)
    vb = _mm_aw("mix_b_in", hn2, W["b_w_in"])
    pooled = _pool_fwd("mix_b_pool", vb)
    arrive(["b_w_grp", "b_w_out"], pooled)
    mixed, ms = _mm_aw("mix_b_grp", pooled, W["b_w_grp"], groups=len(B_WINDOWS), extras=(scale_full,),
                       out_dtypes=(F32, BF16), epilogue=lambda acc, sc: (acc, acc * sc))
    h3 = _mm_aw("mix_b_out", ms, W["b_w_out"], extras=(h2,), epilogue=residual)
    h4, mlp1 = mlp_fwd("1", h3, 1)
    dh4, dh4_b, d_final, loss_part = _final("loss_head", h4, final_norm.reshape(1, -1), tgt)
    g1_like = _W4(None, True, shape=(N_CHIPS, 1, *W["w1_0"].arr.shape[2:]))
    g2_like = _W4(None, False, shape=(N_CHIPS, 1, *W["w2_0"].arr.shape[2:]))

    def exchange(tag, gs):
        zones = [lax.empty((g.shape[0], g.shape[1], g.shape[2] // 2, g.shape[3]), g.dtype) for g in gs]
        send, recv, srcs, zones, tok = _split_start(f"pair_start_{tag}", gs, zones, _pair_plan)
        return (tag, send, recv, srcs, zones), tok

    def reduce(state, after):
        tag, send, recv, srcs, zones = state
        srcs, zones = _split_wait(f"pair_wait_{tag}", srcs, zones, send, recv, after, _pair_whole)
        both = [_pair_sum(f"pair_sum_{tag}_{i}", g, o, place) for i, (g, o) in enumerate(zip(srcs, zones))]
        send, recv, sums, dests, tok = _split_start(f"scatter_start_{tag}", [b[0] for b in both],
                                                    [b[1] for b in both], _scatter_plan)
        return (tag, send, recv, sums, dests), tok

    def relay(state, after):
        tag, send, recv, sums, dests = state
        _, dests = _split_wait(f"scatter_wait_{tag}", sums, dests, send, recv, after, _scatter_whole)
        send, recv, _, dests, tok = _split_start(f"relay_start_{tag}", [], dests, _relay_plan)
        return (tag, send, recv, dests), tok

    def land(state, after):
        tag, send, recv, dests = state
        return _split_wait(f"relay_wait_{tag}", [], dests, send, recv, after, _relay_whole)[1]

    def mlp_bwd(tag, dh, dh_b, saved, layer, deps, pending=None):
        h, hn, act, act_sq = saved
        dpre = _mm_aw(f"mlp{tag}_down_dx", dh_b, W[f"w2_{layer}"], transpose_w=True, extras=(act,),
                      out_dtypes=(BF16,), epilogue=lambda acc, a: (acc * (2.0 * a),), deps=deps)
        scattering, dw_deps = None, ()
        if pending is not None:
            scattering, tok = reduce(pending, dpre)
            dw_deps = (tok,)
        g_w2 = _mm_dw(f"mlp{tag}_down_dw", act_sq, dh_b, g2_like, deps=dw_deps)
        pair_w2, tok = exchange(f"w2_{layer}", [g_w2])
        dhn = _mm_aw(f"mlp{tag}_up_dx", dpre, W[f"w1_{layer}"], transpose_w=True, deps=(tok,))
        g_w1 = _mm_dw(f"mlp{tag}_up_dw", hn, dpre, g1_like)
        pair_w1, tok1 = exchange(f"w1_{layer}", [g_w1])
        scat_w2, tok2 = reduce(pair_w2, dhn)
        dh_in, dh_in_b, d_norm = _rms_bwd(f"mlp{tag}_norm_bwd", h, norm_mlp[layer:layer + 1], dhn, dh)
        return dh_in, dh_in_b, d_norm, pair_w1, scat_w2, (tok1, tok2), scattering

    dh3, dh3_b, d_norm_mlp1, pair_w1_1, scat_w2_1, toks, _ = mlp_bwd("1", dh4, dh4_b, mlp1, 1, ())
    dms = _mm_aw("mix_b_out_dx", dh3_b, W["b_w_out"], transpose_w=True, deps=toks)
    g_b_out = _mm_dw("mix_b_out_dw", ms, dh3_b, W["b_w_out"])
    scat_w1_1, tok = reduce(pair_w1_1, dms)
    dmixed, d_scale = _scale_bwd("mix_b_scale_bwd", dms, mixed, scale_full)
    dpooled = _mm_aw("mix_b_grp_dx", dmixed, W["b_w_grp"], groups=len(B_WINDOWS), transpose_w=True, deps=(tok,))
    g_b_grp = _mm_dw("mix_b_grp_dw", pooled, dmixed, W["b_w_grp"], groups=len(B_WINDOWS))
    dvb = _pool_bwd("mix_b_pool_bwd", dpooled)
    dhn2 = _mm_aw("mix_b_in_dx", dvb, W["b_w_in"], transpose_w=True)
    g_b_in = _mm_dw("mix_b_in_dw", hn2, dvb, W["b_w_in"])
    pair_b, tok = exchange("b", [g_b_out, g_b_grp, g_b_in])
    dh2, dh2_b, d_norm_mix1 = _rms_bwd("mix_b_norm_bwd", h2, norm_mix[1:2], dhn2, dh3)
    dh1, dh1_b, d_norm_mlp0, pair_w1_0, scat_w2_0, toks, scat_b = mlp_bwd("0", dh2, dh2_b, mlp0, 0, (tok,),
                                                                          pending=pair_b)
    dgated = _mm_aw("mix_a_out_dx", dh1_b, W["a_w_out"], transpose_w=True, deps=toks)
    g_a_out = _mm_dw("mix_a_out_dw", gated, dh1_b, W["a_w_out"])
    pair_a_out, tok_a = exchange("a_out", [g_a_out])
    scat_w1_0, tok = reduce(pair_w1_0, dgated)
    early = [relay(state, dgated) for state in (scat_w2_1, scat_w1_1, scat_b)]
    dzpre, d_w_s, d_b_col, d_ln_g, d_ln_b = _gate_bwd("mix_a_gate_bwd", zpre, dgated, a_ln_g, a_ln_b, a_w_s[0], b_col)
    dhn0 = _mm_aw("mix_a_in_dx", dzpre, W["a_w_in"], transpose_w=True, deps=(tok, tok_a, *[t for _, t in early]))
    scat_a_out, tok = reduce(pair_a_out, dhn0)
    g_a_in = _mm_dw("mix_a_in_dw", hn0, dzpre, W["a_w_in"], deps=(tok,))
    pair_a_in, tok = exchange("a_in", [g_a_in])
    dx, _, d_norm_mix0 = _rms_bwd("mix_a_norm_bwd", x2, norm_mix[0:1], dhn0, dh1)
    scat_a_in, _ = reduce(pair_a_in, dx)

    small = dict(a_ln_g=(a_ln_g, m_a_ln_g, v_a_ln_g), a_ln_b=(a_ln_b, m_a_ln_b, v_a_ln_b),
                 a_w_s=(a_w_s, m_a_w_s, v_a_w_s), a_b_s=(a_b_s, m_a_b_s, v_a_b_s),
                 b_scale=(b_scale, m_b_scale, v_b_scale), norm_mix=(norm_mix, m_norm_mix, v_norm_mix),
                 norm_mlp=(norm_mlp, m_norm_mlp, v_norm_mlp), final_norm=(final_norm, m_final_norm, v_final_norm))
    small_names = list(small)
    local = dict(a_ln_g=d_ln_g, a_ln_b=d_ln_b, a_w_s=d_w_s[None], a_b_s=d_b_col.T[None], b_scale=d_scale,
                 norm_mix=jnp.concatenate([d_norm_mix0, d_norm_mix1], axis=0),
                 norm_mlp=jnp.concatenate([d_norm_mlp0, d_norm_mlp1], axis=0), final_norm=d_final.reshape(-1))
    device = (4 * xi + 2 * yi + ci).astype(jnp.int32).reshape(1)
    slots = _own_of_eight("small_own", _pack([local[k] for k in small_names]), device)
    small_send, small_recv, _, (slots,), small_tok = _split_start("small_start", [], [slots], _all_plan)

    moments = dict(a_w_in=(m_a_w_in, v_a_w_in), a_w_out=(m_a_w_out, v_a_w_out), b_w_in=(m_b_w_in, v_b_w_in),
                   b_w_grp=(m_b_w_grp, v_b_w_grp), b_w_out=(m_b_w_out, v_b_w_out),
                   mlp_w1=(m_mlp_w1, v_mlp_w1), mlp_w2=(m_mlp_w2, v_mlp_w2))
    weights = dict(a_w_in=a_w_in, a_w_out=a_w_out, b_w_in=b_w_in, b_w_grp=b_w_grp, b_w_out=b_w_out,
                   mlp_w1=mlp_w1, mlp_w2=mlp_w2)
    landing = [(scat_w2_1, [("mlp_w2", 1)]), (scat_w1_1, [("mlp_w1", 1)]),
               (scat_b, [("b_w_out", 0), ("b_w_grp", 0), ("b_w_in", 0)]),
               (scat_w2_0, [("mlp_w2", 0)]), (scat_w1_0, [("mlp_w1", 0)]),
               (scat_a_out, [("a_w_out", 0)]), (scat_a_in, [("a_w_in", 0)])]
    results, after = {}, dx
    relays = list(early)
    for i, (_, members) in enumerate(landing):
        deps = (small_tok,) if i == 0 else ()
        if len(relays) == i + 1 < len(landing):
            relays.append(relay(landing[i + 1][0], after))
            deps = (relays[-1][1],)
        for (k, layer), parts in zip(members, land(relays[i][0], relays[-1][1] if deps else after)):
            shard_shape = (-1, *parts.shape[2:])
            results[k] = _adam_shard(f"adam_{k}_{layer}", weights[k].reshape(shard_shape),
                                     moments[k][0].reshape(shard_shape), moments[k][1].reshape(shard_shape),
                                     parts, layer=layer, prev=results.get(k), deps=deps)
            after = results[k][1]
    grad_out, delta_out, m_out, v_out = {}, {}, {}, {}
    for k, res in results.items():
        grad_out[k], delta_out[k], m_out[k], v_out[k] = [r.reshape(weights[k].shape) for r in res]

    _, (slots,) = _split_wait("small_wait", [], [slots], small_send, small_recv, after, _all_whole)
    reduced = _sum_of_eight("small_sum", slots)
    small_grads = dict(zip(small_names, _unpack(reduced, [local[k].shape for k in small_names])))
    shard_w = b_scale.shape[1]
    small_grads["b_scale"] = lax.dynamic_slice_in_dim(small_grads["b_scale"], chip * shard_w, shard_w, axis=1)
    small_grads = {k: small_grads[k].reshape(small[k][0].shape) for k in small_names}
    packed = [_pack([small[k][i] for k in small_names]) for i in range(3)]
    res = _adam_packed("adam_small", packed[0], _pack([small_grads[k] for k in small_names]), packed[1], packed[2])
    shapes = [small[k][0].shape for k in small_names]
    for k, d, nm, nv in zip(small_names, *[_unpack(r, shapes) for r in res]):
        grad_out[k], delta_out[k], m_out[k], v_out[k] = small_grads[k], d, nm, nv

    loss = lax.psum(loss_part[0, 0], ("x", "y", "c"))
    order = ["a_w_in", "a_ln_g", "a_ln_b", "a_w_s", "a_b_s", "a_w_out", "b_w_in", "b_w_grp", "b_scale", "b_w_out",
             "norm_mix", "norm_mlp", "mlp_w1", "mlp_w2", "final_norm"]
    return (loss, dx[None], *[grad_out[k] for k in order], *[delta_out[k] for k in order],
            *[m_out[k] for k in order], *[v_out[k] for k in order])
```

```python
import math

import jax
import jax.numpy as jnp
from jax import lax
from jax.experimental import pallas as pl
from jax.experimental.pallas import tpu as pltpu

F32 = jnp.float32
BF16 = jnp.bfloat16
MESH = pl.DeviceIdType.MESH

EPS = 1e-6
B_WINDOWS = (2, 4, 8, 16)
ADAM_LR = 0.001
ADAM_B1 = 0.9
ADAM_B2 = 0.999
ADAM_EPS = 1e-08
ADAM_WD = 0.01
ADAM_STEP = 10

N_CHIPS = 4
N_DEV = 8
LANES = 128
PACK_ELEMS = 8 * LANES
VMEM_LIMIT = 56 * 1024 * 1024
ROW_TILE = 512
MM_TM, MM_TN, MM_TK = 1024, 1024, 2048


_ANY = pl.BlockSpec(memory_space=pl.ANY)


def _tile(dim, pref):
    t = min(dim, pref)
    while dim % t:
        t //= 2
    return t


def _params(*sem):
    return pltpu.CompilerParams(dimension_semantics=sem, vmem_limit_bytes=VMEM_LIMIT)


class _W4:
    def __init__(self, arr, col_sharded, shape=None):
        self.arr = arr
        self.nj, self.nl, self.r, self.c = arr.shape if shape is None else shape
        self.col = col_sharded
        self.rows = self.r if col_sharded else self.nj * self.r
        self.cols = self.nj * self.c if col_sharded else self.c

    def tile_rows(self, pref):
        return _tile(self.r, pref)

    def tile_cols(self, pref):
        return _tile(self.c, pref)

    def index(self, layer, rb, cb, tr, tc):
        if self.col:
            n = self.c // tc
            return (cb // n, layer, rb, cb % n)
        n = self.r // tr
        return (rb // n, layer, rb % n, cb)


def _mm_aw(name, a, w, *, layer=0, groups=1, transpose_w=False, extras=(), out_dtypes=(F32,), epilogue=None,
           deps=(), k_piece=None):
    s, ka_total = a.shape
    kdim, ndim = (w.cols, w.rows) if transpose_w else (w.rows, w.cols)
    assert ka_total == groups * kdim, (name, a.shape, kdim, groups)
    span = not w.col and ((not transpose_w and kdim <= MM_TK) or (transpose_w and groups > 1))
    if span and transpose_w:
        tm, tn, tk = _tile(s, 2048), ndim, w.tile_cols(MM_TK)
    else:
        tk = kdim if span else (w.tile_cols(MM_TK) if transpose_w else w.tile_rows(MM_TK))
        tm, tn_pref = (_tile(s, 2048), 512) if tk == kdim else (_tile(s, MM_TM), MM_TN)
        tn = w.tile_rows(tn_pref) if transpose_w else w.tile_cols(tn_pref)
    nk, nn = kdim // tk, ndim // tn
    if k_piece is not None:
        assert not (w.col or transpose_w or span or groups > 1), name
        piece, count, n_pieces = k_piece
        tm, tn, tk = _tile(s, MM_TM), w.tile_cols(MM_TN), w.r // (2 * n_pieces)
        nk, nn = 2 * w.nj * count, ndim // tn

        def k_block(k):
            return k // (2 * count), ((k // count) % 2) * n_pieces + piece + k % count

    def lay(g):
        return g if groups > 1 else layer

    a_spec = pl.BlockSpec((tm, tk), lambda g, i, n, k: (i, g * nk + k))
    if k_piece is not None:
        per = w.r // tk
        a_spec = pl.BlockSpec((tm, tk), lambda g, i, n, k: (i, k_block(k)[0] * per + k_block(k)[1]))
        w_spec = pl.BlockSpec((None, None, tk, tn), lambda g, i, n, k: (k_block(k)[0], layer, k_block(k)[1], n))
    elif span and transpose_w:
        w_spec = pl.BlockSpec((w.nj, None, w.r, tk), lambda g, i, n, k: (0, lay(g), 0, k))
    elif span:
        w_spec = pl.BlockSpec((w.nj, None, w.r, tn), lambda g, i, n, k: (0, lay(g), 0, n))
    elif transpose_w:
        w_spec = pl.BlockSpec((None, None, tn, tk), lambda g, i, n, k: w.index(lay(g), n, k, tn, tk))
    else:
        w_spec = pl.BlockSpec((None, None, tk, tn), lambda g, i, n, k: w.index(lay(g), k, n, tk, tn))
    ex_specs = []
    for e in extras:
        assert e.shape[1] == groups * ndim and e.shape[0] in (1, s), (name, e.shape)
        if e.shape[0] == 1:
            ex_specs.append(pl.BlockSpec((1, tn), lambda g, i, n, k: (0, g * nn + n)))
        else:
            ex_specs.append(pl.BlockSpec((tm, tn), lambda g, i, n, k: (i, g * nn + n)))
    out_spec = pl.BlockSpec((tm, tn), lambda g, i, n, k: (i, g * nn + n))
    n_ex, n_out, n_dep = len(extras), len(out_dtypes), len(deps)

    def body(a_ref, w_ref, *rest):
        ex, outs = rest[:n_ex], rest[n_ex + n_dep:n_ex + n_dep + n_out]
        av = a_ref[...]
        if av.dtype != BF16:
            av = av.astype(BF16)
        wv = w_ref[...]
        if span:
            wv = wv.reshape((tn, tk) if transpose_w else (tk, tn))
        if transpose_w:
            prod = lax.dot_general(av, wv, (((1,), (1,)), ((), ())), preferred_element_type=F32)
        else:
            prod = jnp.dot(av, wv, preferred_element_type=F32)

        def finish(total):
            vals = (total,) if epilogue is None else epilogue(total, *[e[...] for e in ex])
            for o, v in zip(outs, vals):
                o[...] = v.astype(o.dtype)

        if nk == 1:
            finish(prod)
            return
        acc, k = rest[-1], pl.program_id(3)

        @pl.when(k == 0)
        def _():
            acc[...] = prod

        @pl.when(k > 0)
        def _():
            acc[...] += prod

        @pl.when(k == nk - 1)
        def _():
            finish(acc[...])

    outs = pl.pallas_call(
        body, name=name, grid=(groups, s // tm, nn, nk),
        in_specs=[a_spec, w_spec, *ex_specs] + [_ANY] * n_dep, out_specs=[out_spec] * n_out,
        out_shape=[jax.ShapeDtypeStruct((s, groups * ndim), dt) for dt in out_dtypes],
        scratch_shapes=[pltpu.VMEM((tm, tn), F32)] if nk > 1 else [],
        compiler_params=_params("parallel", "parallel", "parallel", "arbitrary"),
    )(a, w.arr, *extras, *deps)
    return outs[0] if n_out == 1 else outs


def _mm_dw(name, a, b, like, *, layer=0, groups=1, deps=()):
    s, ka_total = a.shape
    rows, cols = ka_total // groups, b.shape[1] // groups
    assert (rows, cols) == (like.rows, like.cols) and b.shape[0] == s, (name, a.shape, b.shape)
    span = groups > 1 and not like.col
    tm, tn, tk = rows if span else like.tile_rows(MM_TM), like.tile_cols(MM_TN), _tile(s, MM_TK)
    nr, nc, nk = rows // tm, cols // tn, s // tk
    assert nk == 1 or not span

    def lay(g):
        return g if groups > 1 else layer

    in_specs = [pl.BlockSpec((tk, tm), lambda g, n, i, k: (k, g * nr + i)),
                pl.BlockSpec((tk, tn), lambda g, n, i, k: (k, g * nc + n))]
    in_specs += [_ANY] * len(deps)

    def body(a_ref, b_ref, *rest):
        av, bv = a_ref[...], b_ref[...]
        if av.dtype != BF16:
            av = av.astype(BF16)
        if bv.dtype != BF16:
            bv = bv.astype(BF16)
        prod = lax.dot_general(av, bv, (((0,), (0,)), ((), ())), preferred_element_type=F32)
        if nk == 1:
            rest[-1][...] = prod.astype(BF16).reshape(rest[-1].shape)
            return
        o_ref, acc, k = rest[-2], rest[-1], pl.program_id(3)

        @pl.when(k == 0)
        def _():
            acc[...] = prod

        @pl.when(k > 0)
        def _():
            acc[...] += prod

        @pl.when(k == nk - 1)
        def _():
            o_ref[...] = acc[...].astype(BF16)

    if span:
        out_spec = pl.BlockSpec((like.nj, None, like.r, tn), lambda g, n, i, k: (0, g, 0, n))
    else:
        out_spec = pl.BlockSpec((None, None, tm, tn), lambda g, n, i, k: like.index(lay(g), i, n, tm, tn))
    return pl.pallas_call(
        body, name=name, grid=(groups, nc, nr, nk), in_specs=in_specs, out_specs=out_spec,
        out_shape=jax.ShapeDtypeStruct((like.nj, like.nl, like.r, like.c), BF16),
        scratch_shapes=[pltpu.VMEM((tm, tn), F32)] if nk > 1 else [],
        compiler_params=_params("parallel", "parallel", "parallel", "arbitrary"),
    )(a, b, *deps)


def _row_spec(tr, d):
    return pl.BlockSpec((tr, d), lambda i: (i, 0))


def _vec_spec(d):
    return pl.BlockSpec((1, d), lambda i: (0, 0))


def _rms_fwd(name, x, g):
    s, d = x.shape
    tr = _tile(s, ROW_TILE)

    def body(x_ref, g_ref, o_ref):
        xv = x_ref[...]
        r = lax.rsqrt(jnp.mean(xv * xv, axis=-1, keepdims=True) + EPS)
        o_ref[...] = (xv * r * g_ref[...]).astype(BF16)

    return pl.pallas_call(
        body, name=name, grid=(s // tr,), in_specs=[_row_spec(tr, d), _vec_spec(d)],
        out_specs=_row_spec(tr, d), out_shape=jax.ShapeDtypeStruct((s, d), BF16),
        compiler_params=_params("parallel"),
    )(x, g)


def _rms_bwd(name, x, g, dhn, dres):
    s, d = x.shape
    tr = _tile(s, ROW_TILE)

    def body(x_ref, g_ref, dhn_ref, dres_ref, dx_ref, dxb_ref, dg_ref):
        @pl.when(pl.program_id(0) == 0)
        def _():
            dg_ref[...] = jnp.zeros_like(dg_ref)

        xv = x_ref[...]
        r = lax.rsqrt(jnp.mean(xv * xv, axis=-1, keepdims=True) + EPS)
        xh = xv * r
        dy = dhn_ref[...]
        dg_ref[...] += jnp.sum(dy * xh, axis=0, keepdims=True)
        dxh = dy * g_ref[...]
        dx = dres_ref[...] + r * (dxh - xh * jnp.mean(dxh * xh, axis=-1, keepdims=True))
        dx_ref[...] = dx
        dxb_ref[...] = dx.astype(BF16)

    return pl.pallas_call(
        body, name=name, grid=(s // tr,),
        in_specs=[_row_spec(tr, d), _vec_spec(d), _row_spec(tr, d), _row_spec(tr, d)],
        out_specs=[_row_spec(tr, d), _row_spec(tr, d), _vec_spec(d)],
        out_shape=[jax.ShapeDtypeStruct((s, d), F32), jax.ShapeDtypeStruct((s, d), BF16),
                   jax.ShapeDtypeStruct((1, d), F32)],
        compiler_params=_params("arbitrary"),
    )(x, g, dhn, dres)


def _final(name, h, g, tgt):
    s, d = h.shape
    tr = _tile(s, ROW_TILE)

    def body(h_ref, g_ref, t_ref, dh_ref, dhb_ref, dg_ref, loss_ref):
        @pl.when(pl.program_id(0) == 0)
        def _():
            dg_ref[...] = jnp.zeros_like(dg_ref)
            loss_ref[...] = jnp.zeros_like(loss_ref)

        hv = h_ref[...]
        r = lax.rsqrt(jnp.mean(hv * hv, axis=-1, keepdims=True) + EPS)
        xh = hv * r
        gv = g_ref[...]
        err = xh * gv - t_ref[...]
        part = 0.5 * jnp.sum(jnp.mean(err * err, axis=-1, keepdims=True), axis=0, keepdims=True)
        loss_ref[...] += jnp.broadcast_to(part, loss_ref.shape)
        dy = err * (1.0 / d)
        dg_ref[...] += jnp.sum(dy * xh, axis=0, keepdims=True)
        dxh = dy * gv
        dh = r * (dxh - xh * jnp.mean(dxh * xh, axis=-1, keepdims=True))
        dh_ref[...] = dh
        dhb_ref[...] = dh.astype(BF16)

    return pl.pallas_call(
        body, name=name, grid=(s // tr,),
        in_specs=[_row_spec(tr, d), _vec_spec(d), _row_spec(tr, d)],
        out_specs=[_row_spec(tr, d), _row_spec(tr, d), _vec_spec(d), _vec_spec(LANES)],
        out_shape=[jax.ShapeDtypeStruct((s, d), F32), jax.ShapeDtypeStruct((s, d), BF16),
                   jax.ShapeDtypeStruct((1, d), F32), jax.ShapeDtypeStruct((1, LANES), F32)],
        compiler_params=_params("arbitrary"),
    )(h, g, tgt)


_SQRT_HALF = 1.0 / math.sqrt(2.0)
_INV_SQRT_2PI = 1.0 / math.sqrt(2.0 * math.pi)


def _gelu(x):
    return x * (lax.erf(x * _SQRT_HALF) + 1.0) * 0.5


def _gelu_grad(x):
    return 0.5 * (lax.erf(x * _SQRT_HALF) + 1.0) + x * jnp.exp(-0.5 * x * x) * _INV_SQRT_2PI


def _causal(chunk):
    t = lax.broadcasted_iota(jnp.int32, (chunk, chunk), 0)
    sidx = lax.broadcasted_iota(jnp.int32, (chunk, chunk), 1)
    return sidx <= t


def _layernorm_parts(v, g, b):
    mu = jnp.mean(v, axis=-1, keepdims=True)
    vc = v - mu
    rs = lax.rsqrt(jnp.mean(vc * vc, axis=-1, keepdims=True) + EPS)
    vhat = vc * rs
    return vhat, rs, vhat * g + b


def _gate_fwd(name, zpre, ln_g, ln_b, w_s, b_col):
    s, aw2 = zpre.shape
    aw = aw2 // 2
    ng, chunk, _ = w_s.shape
    dh = aw // ng

    def body(z_ref, g_ref, b_ref, ws_ref, bc_ref, o_ref):
        u = _gelu(z_ref[:, :aw])
        v = _gelu(z_ref[:, aw:])
        _, _, vln = _layernorm_parts(v, g_ref[...], b_ref[...])
        mask = _causal(chunk)
        for gi in range(ng):
            sl = slice(gi * dh, (gi + 1) * dh)
            wm = jnp.where(mask, ws_ref[gi], 0.0).astype(BF16)
            sg = jnp.dot(wm, vln[:, sl].astype(BF16), preferred_element_type=F32) + bc_ref[:, gi:gi + 1]
            o_ref[:, sl] = (u[:, sl] * sg).astype(BF16)

    return pl.pallas_call(
        body, name=name, grid=(s // chunk,),
        in_specs=[_row_spec(chunk, aw2), _vec_spec(aw), _vec_spec(aw),
                  pl.BlockSpec((ng, chunk, chunk), lambda i: (0, 0, 0)),
                  pl.BlockSpec((chunk, ng), lambda i: (0, 0))],
        out_specs=_row_spec(chunk, aw), out_shape=jax.ShapeDtypeStruct((s, aw), BF16),
        compiler_params=_params("parallel"),
    )(zpre, ln_g, ln_b, w_s, b_col)


def _gate_bwd(name, zpre, dgated, ln_g, ln_b, w_s, b_col):
    s, aw2 = zpre.shape
    aw = aw2 // 2
    ng, chunk, _ = w_s.shape
    dh = aw // ng

    def body(z_ref, dgt_ref, g_ref, b_ref, ws_ref, bc_ref, dz_ref, dws_ref, dbc_ref, dlg_ref, dlb_ref,
             du_scr, dvln_scr):
        @pl.when(pl.program_id(0) == 0)
        def _():
            dws_ref[...] = jnp.zeros_like(dws_ref)
            dbc_ref[...] = jnp.zeros_like(dbc_ref)
            dlg_ref[...] = jnp.zeros_like(dlg_ref)
            dlb_ref[...] = jnp.zeros_like(dlb_ref)

        zu = z_ref[:, :aw]
        zv = z_ref[:, aw:]
        u = _gelu(zu)
        lg = g_ref[...]
        vhat, rs, vln = _layernorm_parts(_gelu(zv), lg, b_ref[...])
        mask = _causal(chunk)
        for gi in range(ng):
            sl = slice(gi * dh, (gi + 1) * dh)
            wm = jnp.where(mask, ws_ref[gi], 0.0).astype(BF16)
            vg = vln[:, sl].astype(BF16)
            sg = jnp.dot(wm, vg, preferred_element_type=F32) + bc_ref[:, gi:gi + 1]
            dgt = dgt_ref[:, sl]
            du_scr[:, sl] = dgt * sg
            ds = dgt * u[:, sl]
            dbc_ref[:, gi:gi + 1] += jnp.sum(ds, axis=-1, keepdims=True)
            dsb = ds.astype(BF16)
            dwm = lax.dot_general(dsb, vg, (((1,), (1,)), ((), ())), preferred_element_type=F32)
            dws_ref[gi] += jnp.where(mask, dwm, 0.0)
            dvln_scr[:, sl] = lax.dot_general(wm, dsb, (((0,), (0,)), ((), ())), preferred_element_type=F32)
        dvln = dvln_scr[...]
        dlb_ref[...] += jnp.sum(dvln, axis=0, keepdims=True)
        dlg_ref[...] += jnp.sum(dvln * vhat, axis=0, keepdims=True)
        dvh = dvln * lg
        dv = rs * (dvh - jnp.mean(dvh, axis=-1, keepdims=True)
                   - vhat * jnp.mean(dvh * vhat, axis=-1, keepdims=True))
        dz_ref[:, :aw] = (du_scr[...] * _gelu_grad(zu)).astype(BF16)
        dz_ref[:, aw:] = (dv * _gelu_grad(zv)).astype(BF16)

    return pl.pallas_call(
        body, name=name, grid=(s // chunk,),
        in_specs=[_row_spec(chunk, aw2), _row_spec(chunk, aw), _vec_spec(aw), _vec_spec(aw),
                  pl.BlockSpec((ng, chunk, chunk), lambda i: (0, 0, 0)),
                  pl.BlockSpec((chunk, ng), lambda i: (0, 0))],
        out_specs=[_row_spec(chunk, aw2), pl.BlockSpec((ng, chunk, chunk), lambda i: (0, 0, 0)),
                   pl.BlockSpec((chunk, ng), lambda i: (0, 0)), _vec_spec(aw), _vec_spec(aw)],
        out_shape=[jax.ShapeDtypeStruct((s, aw2), BF16), jax.ShapeDtypeStruct((ng, chunk, chunk), F32),
                   jax.ShapeDtypeStruct((chunk, ng), F32), jax.ShapeDtypeStruct((1, aw), F32),
                   jax.ShapeDtypeStruct((1, aw), F32)],
        scratch_shapes=[pltpu.VMEM((chunk, aw), F32), pltpu.VMEM((chunk, aw), F32)],
        compiler_params=_params("arbitrary"),
    )(zpre, dgated, ln_g, ln_b, w_s, b_col)


def _pool_select(g, parts):
    out = parts[-1]
    for gi in range(len(parts) - 2, -1, -1):
        out = jnp.where(g == gi, parts[gi], out)
    return out


def _pool_specs(s, bw):
    head = bw // len(B_WINDOWS)
    tc = _tile(head, 256)
    nb = head // tc
    return tc, (len(B_WINDOWS), nb), pl.BlockSpec((s, tc), lambda g, j: (0, g * nb + j))


def _pool_window(g, t):
    w = _pool_select(g, [jnp.full(t.shape, wi, jnp.int32) for wi in B_WINDOWS])
    return jnp.minimum(t + 1, w).astype(F32)


def _pool_fwd(name, vb):
    assert B_WINDOWS == (2, 4, 8, 16)
    s, bw = vb.shape
    tc, grid, spec = _pool_specs(s, bw)

    def body(v_ref, o_ref):
        g = pl.program_id(0)
        v = v_ref[...]
        t = lax.broadcasted_iota(jnp.int32, (s, tc), 0)

        def down(x, k):
            return jnp.where(t >= k, pltpu.roll(x, k, 0), 0.0)

        sums, cur, k = [], v, 1
        for _ in B_WINDOWS:
            cur = cur + down(cur, k)
            sums.append(cur)
            k *= 2
        o_ref[...] = (_pool_select(g, sums) / _pool_window(g, t) - v).astype(BF16)

    return pl.pallas_call(
        body, name=name, grid=grid, in_specs=[spec], out_specs=spec,
        out_shape=jax.ShapeDtypeStruct((s, bw), BF16), compiler_params=_params("parallel", "parallel"),
    )(vb)


def _pool_bwd(name, dpooled):
    s, bw = dpooled.shape
    tc, grid, spec = _pool_specs(s, bw)

    def body(d_ref, o_ref):
        g = pl.program_id(0)
        dp = d_ref[...]
        t = lax.broadcasted_iota(jnp.int32, (s, tc), 0)

        def up(x, k):
            return jnp.where(t < s - k, pltpu.roll(x, s - k, 0), 0.0)

        sums, cur, k = [], dp / _pool_window(g, t), 1
        for _ in B_WINDOWS:
            cur = cur + up(cur, k)
            sums.append(cur)
            k *= 2
        o_ref[...] = (_pool_select(g, sums) - dp).astype(BF16)

    return pl.pallas_call(
        body, name=name, grid=grid, in_specs=[spec], out_specs=spec,
        out_shape=jax.ShapeDtypeStruct((s, bw), BF16), compiler_params=_params("parallel", "parallel"),
    )(dpooled)


def _scale_bwd(name, dms, mixed, scale):
    s, bw = dms.shape
    tr = _tile(s, ROW_TILE)

    def body(d_ref, m_ref, sc_ref, o_ref, ds_ref):
        @pl.when(pl.program_id(0) == 0)
        def _():
            ds_ref[...] = jnp.zeros_like(ds_ref)

        dv = d_ref[...]
        ds_ref[...] += jnp.sum(dv * m_ref[...], axis=0, keepdims=True)
        o_ref[...] = (dv * sc_ref[...]).astype(BF16)

    return pl.pallas_call(
        body, name=name, grid=(s // tr,), in_specs=[_row_spec(tr, bw), _row_spec(tr, bw), _vec_spec(bw)],
        out_specs=[_row_spec(tr, bw), _vec_spec(bw)],
        out_shape=[jax.ShapeDtypeStruct((s, bw), BF16), jax.ShapeDtypeStruct((1, bw), F32)],
        compiler_params=_params("arbitrary"),
    )(dms, mixed, scale)


def _adam_update(w, g, m, v):
    m = ADAM_B1 * m + (1.0 - ADAM_B1) * g
    v = ADAM_B2 * v + (1.0 - ADAM_B2) * (g * g)
    m_hat = m / (1.0 - ADAM_B1 ** ADAM_STEP)
    v_hat = v / (1.0 - ADAM_B2 ** ADAM_STEP)
    delta = -ADAM_LR * (m_hat / (jnp.sqrt(v_hat) + ADAM_EPS) + ADAM_WD * w)
    return delta, m, v


def _adam_shard(name, w, m, v, parts, layer=0, prev=None, deps=()):
    nl, r, c = w.shape
    nj, nlp = parts.shape[:2]
    tr = _tile(r, ROW_TILE // 2)
    spec = pl.BlockSpec((None, tr, c), lambda l, i: (layer + l, i, 0))

    def body(w_ref, m_ref, v_ref, p_ref, *rest):
        g_ref, d_ref, nm_ref, nv_ref = rest[-4:]
        g = p_ref[0].astype(F32)
        for j in range(1, nj):
            g = g + p_ref[j].astype(F32)
        delta, nm, nv = _adam_update(w_ref[...], g, m_ref[...], v_ref[...])
        g_ref[...] = g
        d_ref[...] = delta
        nm_ref[...] = nm
        nv_ref[...] = nv

    prev = () if prev is None else tuple(prev)
    return pl.pallas_call(
        body, name=name, grid=(nlp, r // tr),
        in_specs=[spec, spec, spec, pl.BlockSpec((nj, None, tr, c), lambda l, i: (0, l, i, 0))]
        + [_ANY] * (len(prev) + len(deps)),
        out_specs=[spec] * 4, out_shape=[jax.ShapeDtypeStruct(w.shape, F32)] * 4,
        input_output_aliases={4 + i: i for i in range(len(prev))},
        compiler_params=_params("parallel", "parallel"),
    )(w, m, v, parts, *prev, *deps)


def _adam_packed(name, w, g, m, v):
    rows, lanes = w.shape
    tr = rows
    spec = pl.BlockSpec((tr, lanes), lambda i: (i, 0))

    def body(w_ref, g_ref, m_ref, v_ref, d_ref, nm_ref, nv_ref):
        delta, nm, nv = _adam_update(w_ref[...], g_ref[...], m_ref[...], v_ref[...])
        d_ref[...] = delta
        nm_ref[...] = nm
        nv_ref[...] = nv

    return pl.pallas_call(
        body, name=name, grid=(rows // tr,), in_specs=[spec] * 4, out_specs=[spec] * 3,
        out_shape=[jax.ShapeDtypeStruct(w.shape, F32)] * 3, compiler_params=_params("parallel"),
    )(w, g, m, v)


def _place():
    x, y, c = lax.axis_index("x"), lax.axis_index("y"), lax.axis_index("c")
    chips = [(1 - x, y), (x, 1 - y), (1 - x, 1 - y)]
    return x, y, c, chips


def _remote(src, dst, send_sem, recv_sem, device):
    return pltpu.make_async_remote_copy(src_ref=src, dst_ref=dst, send_sem=send_sem, recv_sem=recv_sem,
                                        device_id=device, device_id_type=MESH)


def _half(ref_rows, cc):
    h = ref_rows // 2
    return pl.ds(cc * h, h)


_HBM = pl.BlockSpec(memory_space=pltpu.HBM)
_SEM = pl.BlockSpec(memory_space=pltpu.SEMAPHORE)
_EFFECT = pltpu.SideEffectType.DATAFLOW_SIDE_EFFECTING


def _in_hbm(arr):
    return pltpu.with_memory_space_constraint(arr, pltpu.HBM)


def _gather_rows(land, cc, part=(0, 1)):
    k, n = part
    h = land.shape[2] // 2
    return pl.ds(cc * h + k * (h // n), h // n)


def _gather_block(land, split, j, cc, part=(0, 1)):
    if split:
        return land.at[j, :, _gather_rows(land, cc, part), :]
    return land.at[j]


def _split_start(name, srcs, lands, plan, deps=(), groups=None):
    ns = len(srcs)
    nl = len(lands) if groups is None else groups
    both = list(srcs) + list(lands)
    nb = len(both)

    def body(*refs):
        s, ld = refs[:ns], refs[ns:nb]
        outs = refs[nb + len(deps):]
        send, recv, token = outs[:nl], outs[nl:2 * nl], outs[-1]
        for a, copies in enumerate(plan(s, ld)):
            for src, dst, peer in copies:
                _remote(src, dst, send[a], recv[a], peer).start()
        token[...] = jnp.zeros_like(token)

    outs = pl.pallas_call(
        body, name=name, in_specs=[_HBM] * nb + [_ANY] * len(deps),
        out_specs=[_SEM] * (2 * nl) + [_HBM] * nb + [pl.BlockSpec(memory_space=pltpu.VMEM)],
        out_shape=[pltpu.SemaphoreType.DMA(())] * (2 * nl) + [pltpu.HBM(b.shape, b.dtype) for b in both]
        + [jax.ShapeDtypeStruct((8, LANES), F32)],
        input_output_aliases={i: 2 * nl + i for i in range(nb)},
        compiler_params=pltpu.CompilerParams(has_side_effects=_EFFECT),
    )(*[_in_hbm(b) for b in both], *deps)
    thru = outs[2 * nl:2 * nl + nb]
    return list(outs[:nl]), list(outs[nl:2 * nl]), list(thru[:ns]), list(thru[ns:]), outs[-1]


def _split_wait(name, srcs, lands, send, recv, after, whole):
    ns, nl = len(srcs), len(send)
    both = list(srcs) + list(lands)
    nb = len(both)

    def body(*refs):
        ld, snd, rcv = refs[ns:nb], refs[nb:nb + nl], refs[nb + nl:nb + 2 * nl]
        x, y, c, _ = _place()
        for a, blk in enumerate(whole(ld)):
            every = _remote(blk, blk, snd[a], rcv[a], (x, y, c))
            every.wait_send()
            every.wait_recv()

    outs = pl.pallas_call(
        body, name=name, in_specs=[_HBM] * nb + [_SEM] * (2 * nl) + [_ANY], out_specs=[_HBM] * nb,
        out_shape=[pltpu.HBM(b.shape, b.dtype) for b in both],
        input_output_aliases={i: i for i in range(nb)},
        compiler_params=pltpu.CompilerParams(has_side_effects=_EFFECT),
    )(*both, *send, *recv, after)
    return list(outs[:ns]), list(outs[ns:])


def _gather_plan(split, parts):
    def plan(srcs, lands):
        x, y, c, chips = _place()
        out = []
        for ld, sp, n in zip(lands, split, parts):
            for k in range(n):
                blk = _gather_block(ld, sp, 2 * x + y, c, (k, n))
                out.append([(blk, blk, (qx, qy, c)) for qx, qy in chips])
        return out
    return plan


def _gather_whole(split, part=(0, 1)):
    def whole(lands):
        _, _, c, _ = _place()
        return [ld.at[pl.ds(0, 3), :, _gather_rows(ld, c, part), :] if sp else ld.at[pl.ds(0, 3)]
                for ld, sp in zip(lands, split)]
    return whole


def _relay_plan(srcs, lands):
    x, y, c, _ = _place()
    out = []
    for ld in lands:
        blocks = [ld.at[j, :, _half(ld.shape[2], c), :] for j in range(N_CHIPS)]
        out.append([(blk, blk, (x, y, 1 - c)) for blk in blocks])
    return out


def _relay_whole(lands):
    _, _, c, _ = _place()
    return [ld.at[:, :, _half(ld.shape[2], c), :] for ld in lands]


def _fill_own(name, shard, dtype, place, layer=None):
    nl, r, c = shard.shape
    first = 0
    if layer is not None:
        nl, first = 1, layer
    tr = _tile(r, 512)

    def body(p_ref, s_ref, o_ref):
        o_ref[...] = s_ref[...].astype(o_ref.dtype)

    return pl.pallas_call(
        body, name=name,
        grid_spec=pltpu.PrefetchScalarGridSpec(
            num_scalar_prefetch=1, grid=(nl, r // tr),
            in_specs=[pl.BlockSpec((None, tr, c), lambda l, i, p: (first + l, i, 0))],
            out_specs=pl.BlockSpec((None, None, tr, c), lambda l, i, p: (p[0], l, i, 0))),
        out_shape=jax.ShapeDtypeStruct((N_CHIPS, nl, r, c), dtype),
        compiler_params=_params("parallel", "parallel"),
    )(place, shard)


def _gather_finish(name, lands, part=(0, 1)):
    n = len(lands)

    def body(*refs):
        outs = refs[n:2 * n]
        fsend, frecv = refs[2 * n:]
        x, y, c, chips = _place()
        sib = (x, y, 1 - c)

        def relay(a, qi, cc):
            qx, qy = chips[qi]
            blk = _gather_block(outs[a], True, 2 * qx + qy, cc, part)
            return _remote(blk, blk, fsend.at[a, qi], frecv.at[a, qi], sib)

        relays = [relay(a, qi, c) for a in range(n) for qi in range(3)]
        for cp in relays:
            cp.start()
        for a in range(n):
            for qi in range(3):
                relay(a, qi, 1 - c).wait_recv()
        for cp in relays:
            cp.wait_send()

    outs = pl.pallas_call(
        body, name=name, in_specs=[_ANY] * n, out_specs=[_ANY] * n,
        out_shape=[jax.ShapeDtypeStruct(ld.shape, ld.dtype) for ld in lands],
        input_output_aliases={i: i for i in range(n)},
        scratch_shapes=[pltpu.SemaphoreType.DMA((n, 3))] * 2,
    )(*lands)
    return list(outs)


def _pair_plan(srcs, lands):
    x, y, c, _ = _place()
    return [[(s.at[:, :, _half(s.shape[2], 1 - c), :], ld, (x, y, 1 - c))] for s, ld in zip(srcs, lands)]


def _pair_whole(lands):
    return list(lands)


def _scatter_plan(srcs, lands):
    x, y, c, chips = _place()
    return [[(s.at[2 * qx + qy], ld.at[2 * x + y, :, _half(ld.shape[2], c), :], (qx, qy, c)) for qx, qy in chips]
            for s, ld in zip(srcs, lands)]


def _scatter_whole(lands):
    _, _, c, _ = _place()
    return [ld.at[pl.ds(0, 3), :, _half(ld.shape[2], c), :] for ld in lands]


def _pair_sum(name, grad, other, place):
    nj, nl, r, c = grad.shape
    h = r // 2
    tr = _tile(h, 2 * ROW_TILE)
    nb = h // tr

    def body(p_ref, g_ref, o_ref, q_ref, d_ref):
        q = (g_ref[...].astype(F32) + o_ref[...].astype(F32)).astype(BF16)
        q_ref[...] = q

        @pl.when(pl.program_id(2) == p_ref[0])
        def _():
            d_ref[...] = q

    blk = (None, None, tr, c)
    return pl.pallas_call(
        body, name=name,
        grid_spec=pltpu.PrefetchScalarGridSpec(
            num_scalar_prefetch=1, grid=(nl, nb, nj),
            in_specs=[pl.BlockSpec(blk, lambda l, i, j, p: (j, l, p[1] * nb + i, 0)),
                      pl.BlockSpec(blk, lambda l, i, j, p: (j, l, i, 0))],
            out_specs=[pl.BlockSpec(blk, lambda l, i, j, p: (j, l, i, 0)),
                       pl.BlockSpec(blk, lambda l, i, j, p: (p[0], l, p[1] * nb + i, 0))]),
        out_shape=[jax.ShapeDtypeStruct((nj, nl, h, c), BF16), jax.ShapeDtypeStruct((nj, nl, r, c), BF16)],
        compiler_params=_params("parallel", "parallel", "arbitrary"),
    )(place, grad, other)


def _own_of_eight(name, packed, device):
    rows, lanes = packed.shape
    tr = rows

    def body(d_ref, s_ref, o_ref):
        o_ref[...] = s_ref[...]

    return pl.pallas_call(
        body, name=name,
        grid_spec=pltpu.PrefetchScalarGridSpec(
            num_scalar_prefetch=1, grid=(rows // tr,),
            in_specs=[pl.BlockSpec((tr, lanes), lambda i, d: (i, 0))],
            out_specs=pl.BlockSpec((None, tr, lanes), lambda i, d: (d[0], i, 0))),
        out_shape=jax.ShapeDtypeStruct((N_DEV, rows, lanes), packed.dtype),
        compiler_params=_params("parallel"),
    )(device, packed)


def _all_plan(srcs, lands):
    x, y, c, _ = _place()
    (ld,) = lands
    blk = ld.at[4 * x + 2 * y + c]
    flips = [(a, b, d) for a in (0, 1) for b in (0, 1) for d in (0, 1) if a + b + d]
    return [[(blk, blk, (x + a - 2 * a * x, y + b - 2 * b * y, c + d - 2 * d * c)) for a, b, d in flips]]


def _all_whole(lands):
    return [lands[0].at[pl.ds(0, N_DEV - 1)]]


def _sum_of_eight(name, slots):
    n, rows, lanes = slots.shape
    tr = rows

    def body(s_ref, o_ref):
        total = s_ref[0]
        for d in range(1, n):
            total = total + s_ref[d]
        o_ref[...] = total

    return pl.pallas_call(
        body, name=name, grid=(rows // tr,),
        in_specs=[pl.BlockSpec((n, tr, lanes), lambda i: (0, i, 0))],
        out_specs=pl.BlockSpec((tr, lanes), lambda i: (i, 0)),
        out_shape=jax.ShapeDtypeStruct((rows, lanes), F32), compiler_params=_params("parallel"),
    )(slots)


def _pack(parts):
    rows = []
    for p in parts:
        flat = p.reshape(-1)
        pad = (-flat.shape[0]) % PACK_ELEMS
        rows.append(jnp.pad(flat, (0, pad)).reshape(-1, LANES))
    return jnp.concatenate(rows, axis=0)


def _unpack(packed, shapes):
    out, row = [], 0
    for sh in shapes:
        size = math.prod(sh)
        nrows = -(-size // PACK_ELEMS) * (PACK_ELEMS // LANES)
        out.append(packed[row:row + nrows].reshape(-1)[:size].reshape(sh))
        row += nrows
    return out


def kernel(x, a_w_in, a_ln_g, a_ln_b, a_w_s, a_b_s, a_w_out, b_w_in, b_w_grp, b_scale, b_w_out, norm_mix, norm_mlp, mlp_w1, mlp_w2, final_norm, loss_target, m_a_w_in, m_a_ln_g, m_a_ln_b, m_a_w_s, m_a_b_s, m_a_w_out, m_b_w_in, m_b_w_grp, m_b_scale, m_b_w_out, m_norm_mix, m_norm_mlp, m_mlp_w1, m_mlp_w2, m_final_norm, v_a_w_in, v_a_ln_g, v_a_ln_b, v_a_w_s, v_a_b_s, v_a_w_out, v_b_w_in, v_b_w_grp, v_b_scale, v_b_w_out, v_norm_mix, v_norm_mlp, v_mlp_w1, v_mlp_w2, v_final_norm):
    xi, yi, ci = lax.axis_index("x"), lax.axis_index("y"), lax.axis_index("c")
    chip = 2 * xi + yi
    place = jnp.stack([chip, ci]).astype(jnp.int32)
    x2, tgt = x[0], loss_target[0]
    bw = b_scale.shape[1] * N_CHIPS

    units = dict(a_w_in=(a_w_in, None), a_w_out=(a_w_out, None), w1_0=(mlp_w1, 0), w2_0=(mlp_w2, 0),
                 b_scale=(b_scale.reshape(1, 1, -1), None), b_w_in=(b_w_in, None), b_w_grp=(b_w_grp[0], None),
                 b_w_out=(b_w_out, None), w1_1=(mlp_w1, 1), w2_1=(mlp_w2, 1))
    col_sharded = dict(a_w_in=True, a_w_out=False, b_w_in=False, b_w_grp=False, b_w_out=False,
                       w1_0=True, w2_0=False, w1_1=True, w2_1=False)
    in_flight, W = {}, {}

    def launch(tag, keys, deps):
        sp = [k != "b_scale" for k in keys]
        parts = [pieces.get(k, 1) for k in keys]
        zones = [_fill_own(f"gather_own_{k}", units[k][0], BF16 if s else F32, place, layer=units[k][1])
                 for k, s in zip(keys, sp)]
        send, recv, _, zones, tok = _split_start(f"gather_start_{tag}", [], zones, _gather_plan(sp, parts), deps,
                                                 groups=sum(parts))
        first = 0
        for k, z, s, n in zip(keys, zones, sp, parts):
            in_flight[k] = (send[first:first + n], recv[first:first + n], z, s)
            first += n
        return tok

    def arrive_piece(key, k, after):
        send, recv, zone, sp = in_flight[key]
        n = len(send)
        _, zones = _split_wait(f"gather_wait_{key}_{k}", [], [zone], [send[k]], [recv[k]], after,
                               _gather_whole([sp], (k, n)))
        (zone,) = _gather_finish(f"gather_finish_{key}_{k}", zones, (k, n))
        in_flight[key] = (send, recv, zone, sp)
        W[key] = _W4(zone, col_sharded[key])

    def arrive(keys, after):
        send, recv, zones, sp = zip(*[in_flight[k] for k in keys])
        if len(send[0]) > 1:
            for k in range(len(send[0])):
                arrive_piece(keys[0], k, after)
            return
        _, zones = _split_wait(f"gather_wait_{keys[0]}", [], zones, [s[0] for s in send], [r[0] for r in recv],
                               after, _gather_whole(sp))
        relayed = iter(_gather_finish(f"gather_finish_{keys[0]}", [z for z, s in zip(zones, sp) if s]))
        for k, z, s in zip(keys, zones, sp):
            full = next(relayed) if s else z
            W[k] = _W4(full, col_sharded[k]) if k in col_sharded else full

    pieces = dict(w1_0=2, w2_0=2, w1_1=2, w2_1=4)

    token = launch("first", ["a_w_in", "a_w_out"], ())
    token = launch("rest", ["w1_0", "w2_0", "b_scale", "b_w_in", "b_w_grp", "b_w_out", "w1_1", "w2_1"], (token,))

    b_col = a_b_s[0].T

    def residual(acc, res):
        return (res + acc,)

    def sq_relu(acc):
        act = jnp.maximum(acc, 0.0)
        return act, act * act

    def mlp_fwd(tag, h, layer):
        hn = _rms_fwd(f"mlp{tag}_norm", h, norm_mlp[layer:layer + 1])
        arrive([f"w1_{layer}"], hn)
        act, act_sq = _mm_aw(f"mlp{tag}_up", hn, W[f"w1_{layer}"], out_dtypes=(BF16, BF16), epilogue=sq_relu)
        key, n = f"w2_{layer}", pieces[f"w2_{layer}"]
        for k in range(n - 1):
            arrive_piece(key, k, act_sq)
        out = _mm_aw(f"mlp{tag}_down_head", act_sq, W[key], extras=(h,), epilogue=residual, k_piece=(0, n - 1, n))
        arrive_piece(key, n - 1, out)
        out = _mm_aw(f"mlp{tag}_down_tail", act_sq, W[key], extras=(out,), epilogue=residual, k_piece=(n - 1, 1, n))
        return out, (h, hn, act, act_sq)

    hn0 = _rms_fwd("mix_a_norm", x2, norm_mix[0:1])
    arrive(["a_w_in"], token)
    zpre = _mm_aw("mix_a_in", hn0, W["a_w_in"])
    gated = _gate_fwd("mix_a_gate", zpre, a_ln_g, a_ln_b, a_w_s[0], b_col)
    arrive(["a_w_out"], gated)
    h1 = _mm_aw("mix_a_out", gated, W["a_w_out"], extras=(x2,), epilogue=residual)
    h2, mlp0 = mlp_fwd("0", h1, 0)
    hn2 = _rms_fwd("mix_b_norm", h2, norm_mix[1:2])
    arrive(["b_scale", "b_w_in"], hn2)
    scale_full = W["b_scale"].reshape(1, bw)
    vb = _mm_aw("mix_b_in", hn2, W["b_w_in"])
    pooled = _pool_fwd("mix_b_pool", vb)
    arrive(["b_w_grp", "b_w_out"], pooled)
    mixed, ms = _mm_aw("mix_b_grp", pooled, W["b_w_grp"], groups=len(B_WINDOWS), extras=(scale_full,),
                       out_dtypes=(F32, BF16), epilogue=lambda acc, sc: (acc, acc * sc))
    h3 = _mm_aw("mix_b_out", ms, W["b_w_out"], extras=(h2,), epilogue=residual)
    h4, mlp1 = mlp_fwd("1", h3, 1)
    dh4, dh4_b, d_final, loss_part = _final("loss_head", h4, final_norm.reshape(1, -1), tgt)
    g1_like = _W4(None, True, shape=(N_CHIPS, 1, *W["w1_0"].arr.shape[2:]))
    g2_like = _W4(None, False, shape=(N_CHIPS, 1, *W["w2_0"].arr.shape[2:]))

    def exchange(tag, gs):
        zones = [lax.empty((g.shape[0], g.shape[1], g.shape[2] // 2, g.shape[3]), g.dtype) for g in gs]
        send, recv, srcs, zones, tok = _split_start(f"pair_start_{tag}", gs, zones, _pair_plan)
        return (tag, send, recv, srcs, zones), tok

    def reduce(state, after):
        tag, send, recv, srcs, zones = state
        srcs, zones = _split_wait(f"pair_wait_{tag}", srcs, zones, send, recv, after, _pair_whole)
        both = [_pair_sum(f"pair_sum_{tag}_{i}", g, o, place) for i, (g, o) in enumerate(zip(srcs, zones))]
        send, recv, sums, dests, tok = _split_start(f"scatter_start_{tag}", [b[0] for b in both],
                                                    [b[1] for b in both], _scatter_plan)
        return (tag, send, recv, sums, dests), tok

    def relay(state, after):
        tag, send, recv, sums, dests = state
        _, dests = _split_wait(f"scatter_wait_{tag}", sums, dests, send, recv, after, _scatter_whole)
        send, recv, _, dests, tok = _split_start(f"relay_start_{tag}", [], dests, _relay_plan)
        return (tag, send, recv, dests), tok

    def land(state, after):
        tag, send, recv, dests = state
        return _split_wait(f"relay_wait_{tag}", [], dests, send, recv, after, _relay_whole)[1]

    def mlp_bwd(tag, dh, dh_b, saved, layer, deps, pending=None):
        h, hn, act, act_sq = saved
        dpre = _mm_aw(f"mlp{tag}_down_dx", dh_b, W[f"w2_{layer}"], transpose_w=True, extras=(act,),
                      out_dtypes=(BF16,), epilogue=lambda acc, a: (acc * (2.0 * a),), deps=deps)
        scattering, dw_deps = None, ()
        if pending is not None:
            scattering, tok = reduce(pending, dpre)
            dw_deps = (tok,)
        g_w2 = _mm_dw(f"mlp{tag}_down_dw", act_sq, dh_b, g2_like, deps=dw_deps)
        pair_w2, tok = exchange(f"w2_{layer}", [g_w2])
        dhn = _mm_aw(f"mlp{tag}_up_dx", dpre, W[f"w1_{layer}"], transpose_w=True, deps=(tok,))
        g_w1 = _mm_dw(f"mlp{tag}_up_dw", hn, dpre, g1_like)
        pair_w1, tok1 = exchange(f"w1_{layer}", [g_w1])
        scat_w2, tok2 = reduce(pair_w2, dhn)
        dh_in, dh_in_b, d_norm = _rms_bwd(f"mlp{tag}_norm_bwd", h, norm_mlp[layer:layer + 1], dhn, dh)
        return dh_in, dh_in_b, d_norm, pair_w1, scat_w2, (tok1, tok2), scattering

    dh3, dh3_b, d_norm_mlp1, pair_w1_1, scat_w2_1, toks, _ = mlp_bwd("1", dh4, dh4_b, mlp1, 1, ())
    dms = _mm_aw("mix_b_out_dx", dh3_b, W["b_w_out"], transpose_w=True, deps=toks)
    g_b_out = _mm_dw("mix_b_out_dw", ms, dh3_b, W["b_w_out"])
    scat_w1_1, tok = reduce(pair_w1_1, dms)
    dmixed, d_scale = _scale_bwd("mix_b_scale_bwd", dms, mixed, scale_full)
    dpooled = _mm_aw("mix_b_grp_dx", dmixed, W["b_w_grp"], groups=len(B_WINDOWS), transpose_w=True, deps=(tok,))
    g_b_grp = _mm_dw("mix_b_grp_dw", pooled, dmixed, W["b_w_grp"], groups=len(B_WINDOWS))
    dvb = _pool_bwd("mix_b_pool_bwd", dpooled)
    dhn2 = _mm_aw("mix_b_in_dx", dvb, W["b_w_in"], transpose_w=True)
    g_b_in = _mm_dw("mix_b_in_dw", hn2, dvb, W["b_w_in"])
    pair_b, tok = exchange("b", [g_b_out, g_b_grp, g_b_in])
    dh2, dh2_b, d_norm_mix1 = _rms_bwd("mix_b_norm_bwd", h2, norm_mix[1:2], dhn2, dh3)
    dh1, dh1_b, d_norm_mlp0, pair_w1_0, scat_w2_0, toks, scat_b = mlp_bwd("0", dh2, dh2_b, mlp0, 0, (tok,),
                                                                          pending=pair_b)
    dgated = _mm_aw("mix_a_out_dx", dh1_b, W["a_w_out"], transpose_w=True, deps=toks)
    g_a_out = _mm_dw("mix_a_out_dw", gated, dh1_b, W["a_w_out"])
    pair_a_out, tok_a = exchange("a_out", [g_a_out])
    scat_w1_0, tok = reduce(pair_w1_0, dgated)
    early = [relay(state, dgated) for state in (scat_w2_1, scat_w1_1, scat_b)]
    dzpre, d_w_s, d_b_col, d_ln_g, d_ln_b = _gate_bwd("mix_a_gate_bwd", zpre, dgated, a_ln_g, a_ln_b, a_w_s[0], b_col)
    dhn0 = _mm_aw("mix_a_in_dx", dzpre, W["a_w_in"], transpose_w=True, deps=(tok, tok_a, *[t for _, t in early]))
    scat_a_out, tok = reduce(pair_a_out, dhn0)
    g_a_in = _mm_dw("mix_a_in_dw", hn0, dzpre, W["a_w_in"], deps=(tok,))
    pair_a_in, tok = exchange("a_in", [g_a_in])
    dx, _, d_norm_mix0 = _rms_bwd("mix_a_norm_bwd", x2, norm_mix[0:1], dhn0, dh1)
    scat_a_in, _ = reduce(pair_a_in, dx)

    small = dict(a_ln_g=(a_ln_g, m_a_ln_g, v_a_ln_g), a_ln_b=(a_ln_b, m_a_ln_b, v_a_ln_b),
                 a_w_s=(a_w_s, m_a_w_s, v_a_w_s), a_b_s=(a_b_s, m_a_b_s, v_a_b_s),
                 b_scale=(b_scale, m_b_scale, v_b_scale), norm_mix=(norm_mix, m_norm_mix, v_norm_mix),
                 norm_mlp=(norm_mlp, m_norm_mlp, v_norm_mlp), final_norm=(final_norm, m_final_norm, v_final_norm))
    small_names = list(small)
    local = dict(a_ln_g=d_ln_g, a_ln_b=d_ln_b, a_w_s=d_w_s[None], a_b_s=d_b_col.T[None], b_scale=d_scale,
                 norm_mix=jnp.concatenate([d_norm_mix0, d_norm_mix1], axis=0),
                 norm_mlp=jnp.concatenate([d_norm_mlp0, d_norm_mlp1], axis=0), final_norm=d_final.reshape(-1))
    device = (4 * xi + 2 * yi + ci).astype(jnp.int32).reshape(1)
    slots = _own_of_eight("small_own", _pack([local[k] for k in small_names]), device)
    small_send, small_recv, _, (slots,), small_tok = _split_start("small_start", [], [slots], _all_plan)

    moments = dict(a_w_in=(m_a_w_in, v_a_w_in), a_w_out=(m_a_w_out, v_a_w_out), b_w_in=(m_b_w_in, v_b_w_in),
                   b_w_grp=(m_b_w_grp, v_b_w_grp), b_w_out=(m_b_w_out, v_b_w_out),
                   mlp_w1=(m_mlp_w1, v_mlp_w1), mlp_w2=(m_mlp_w2, v_mlp_w2))
    weights = dict(a_w_in=a_w_in, a_w_out=a_w_out, b_w_in=b_w_in, b_w_grp=b_w_grp, b_w_out=b_w_out,
                   mlp_w1=mlp_w1, mlp_w2=mlp_w2)
    landing = [(scat_w2_1, [("mlp_w2", 1)]), (scat_w1_1, [("mlp_w1", 1)]),
               (scat_b, [("b_w_out", 0), ("b_w_grp", 0), ("b_w_in", 0)]),
               (scat_w2_0, [("mlp_w2", 0)]), (scat_w1_0, [("mlp_w1", 0)]),
               (scat_a_out, [("a_w_out", 0)]), (scat_a_in, [("a_w_in", 0)])]
    results, after = {}, dx
    relays = list(early)
    for i, (_, members) in enumerate(landing):
        deps = (small_tok,) if i == 0 else ()
        if len(relays) == i + 1 < len(landing):
            relays.append(relay(landing[i + 1][0], after))
            deps = (relays[-1][1],)
        for (k, layer), parts in zip(members, land(relays[i][0], relays[-1][1] if deps else after)):
            shard_shape = (-1, *parts.shape[2:])
            results[k] = _adam_shard(f"adam_{k}_{layer}", weights[k].reshape(shard_shape),
                                     moments[k][0].reshape(shard_shape), moments[k][1].reshape(shard_shape),
                                     parts, layer=layer, prev=results.get(k), deps=deps)
            after = results[k][1]
    grad_out, delta_out, m_out, v_out = {}, {}, {}, {}
    for k, res in results.items():
        grad_out[k], delta_out[k], m_out[k], v_out[k] = [r.reshape(weights[k].shape) for r in res]

    _, (slots,) = _split_wait("small_wait", [], [slots], small_send, small_recv, after, _all_whole)
    reduced = _sum_of_eight("small_sum", slots)
    small_grads = dict(zip(small_names, _unpack(reduced, [local[k].shape for k in small_names])))
    shard_w = b_scale.shape[1]
    small_grads["b_scale"] = lax.dynamic_slice_in_dim(small_grads["b_scale"], chip * shard_w, shard_w, axis=1)
    small_grads = {k: small_grads[k].reshape(small[k][0].shape) for k in small_names}
    packed = [_pack([small[k][i] for k in small_names]) for i in range(3)]
    res = _adam_packed("adam_small", packed[0], _pack([small_grads[k] for k in small_names]), packed[1], packed[2])
    shapes = [small[k][0].shape for k in small_names]
    for k, d, nm, nv in zip(small_names, *[_unpack(r, shapes) for r in res]):
        grad_out[k], delta_out[k], m_out[k], v_out[k] = small_grads[k], d, nm, nv

    loss = lax.psum(loss_part[0, 0], ("x", "y", "c"))
    order = ["a_w_in", "a_ln_g", "a_ln_b", "a_w_s", "a_b_s", "a_w_out", "b_w_in", "b_w_grp", "b_scale", "b_w_out",
             "norm_mix", "norm_mlp", "mlp_w1", "mlp_w2", "final_norm"]
    return (loss, dx[None], *[grad_out[k] for k in order], *[delta_out[k] for k in order],
            *[m_out[k] for k in order], *[v_out[k] for k in order])
```

```python
import math

import jax
import jax.numpy as jnp
from jax import lax
from jax.experimental import pallas as pl
from jax.experimental.pallas import tpu as pltpu

F32 = jnp.float32
BF16 = jnp.bfloat16
MESH = pl.DeviceIdType.MESH

EPS = 1e-6
B_WINDOWS = (2, 4, 8, 16)
ADAM_LR = 0.001
ADAM_B1 = 0.9
ADAM_B2 = 0.999
ADAM_EPS = 1e-08
ADAM_WD = 0.01
ADAM_STEP = 10

N_CHIPS = 4
N_DEV = 8
LANES = 128
PACK_ELEMS = 8 * LANES
VMEM_LIMIT = 56 * 1024 * 1024
ROW_TILE = 512
MM_TM, MM_TN, MM_TK = 1024, 1024, 2048


_ANY = pl.BlockSpec(memory_space=pl.ANY)


def _tile(dim, pref):
    t = min(dim, pref)
    while dim % t:
        t //= 2
    return t


def _params(*sem):
    return pltpu.CompilerParams(dimension_semantics=sem, vmem_limit_bytes=VMEM_LIMIT)


class _W4:
    def __init__(self, arr, col_sharded, shape=None):
        self.arr = arr
        self.nj, self.nl, self.r, self.c = arr.shape if shape is None else shape
        self.col = col_sharded
        self.rows = self.r if col_sharded else self.nj * self.r
        self.cols = self.nj * self.c if col_sharded else self.c

    def tile_rows(self, pref):
        return _tile(self.r, pref)

    def tile_cols(self, pref):
        return _tile(self.c, pref)

    def index(self, layer, rb, cb, tr, tc):
        if self.col:
            n = self.c // tc
            return (cb // n, layer, rb, cb % n)
        n = self.r // tr
        return (rb // n, layer, rb % n, cb)


def _mm_aw(name, a, w, *, layer=0, groups=1, transpose_w=False, extras=(), out_dtypes=(F32,), epilogue=None,
           deps=(), k_piece=None):
    s, ka_total = a.shape
    kdim, ndim = (w.cols, w.rows) if transpose_w else (w.rows, w.cols)
    assert ka_total == groups * kdim, (name, a.shape, kdim, groups)
    span = not w.col and ((not transpose_w and kdim <= MM_TK) or (transpose_w and groups > 1))
    if span and transpose_w:
        tm, tn, tk = _tile(s, 2048), ndim, w.tile_cols(MM_TK)
    else:
        tk = kdim if span else (w.tile_cols(MM_TK) if transpose_w else w.tile_rows(MM_TK))
        tm, tn_pref = (_tile(s, 2048), 512) if tk == kdim else (_tile(s, MM_TM), MM_TN)
        tn = w.tile_rows(tn_pref) if transpose_w else w.tile_cols(tn_pref)
    nk, nn = kdim // tk, ndim // tn
    if k_piece is not None:
        assert not (w.col or transpose_w or span or groups > 1), name
        piece, n_pieces = k_piece
        tm, tn, tk = _tile(s, MM_TM), w.tile_cols(MM_TN), w.r // (2 * n_pieces)
        nk, nn = 2 * w.nj, ndim // tn

        def k_block(k):
            return k // 2, (k % 2) * n_pieces + piece

    def lay(g):
        return g if groups > 1 else layer

    a_spec = pl.BlockSpec((tm, tk), lambda g, i, n, k: (i, g * nk + k))
    if k_piece is not None:
        per = w.r // tk
        a_spec = pl.BlockSpec((tm, tk), lambda g, i, n, k: (i, k_block(k)[0] * per + k_block(k)[1]))
        w_spec = pl.BlockSpec((None, None, tk, tn), lambda g, i, n, k: (k_block(k)[0], layer, k_block(k)[1], n))
    elif span and transpose_w:
        w_spec = pl.BlockSpec((w.nj, None, w.r, tk), lambda g, i, n, k: (0, lay(g), 0, k))
    elif span:
        w_spec = pl.BlockSpec((w.nj, None, w.r, tn), lambda g, i, n, k: (0, lay(g), 0, n))
    elif transpose_w:
        w_spec = pl.BlockSpec((None, None, tn, tk), lambda g, i, n, k: w.index(lay(g), n, k, tn, tk))
    else:
        w_spec = pl.BlockSpec((None, None, tk, tn), lambda g, i, n, k: w.index(lay(g), k, n, tk, tn))
    ex_specs = []
    for e in extras:
        assert e.shape[1] == groups * ndim and e.shape[0] in (1, s), (name, e.shape)
        if e.shape[0] == 1:
            ex_specs.append(pl.BlockSpec((1, tn), lambda g, i, n, k: (0, g * nn + n)))
        else:
            ex_specs.append(pl.BlockSpec((tm, tn), lambda g, i, n, k: (i, g * nn + n)))
    out_spec = pl.BlockSpec((tm, tn), lambda g, i, n, k: (i, g * nn + n))
    n_ex, n_out, n_dep = len(extras), len(out_dtypes), len(deps)

    def body(a_ref, w_ref, *rest):
        ex, outs = rest[:n_ex], rest[n_ex + n_dep:n_ex + n_dep + n_out]
        av = a_ref[...]
        if av.dtype != BF16:
            av = av.astype(BF16)
        wv = w_ref[...]
        if span:
            wv = wv.reshape((tn, tk) if transpose_w else (tk, tn))
        if transpose_w:
            prod = lax.dot_general(av, wv, (((1,), (1,)), ((), ())), preferred_element_type=F32)
        else:
            prod = jnp.dot(av, wv, preferred_element_type=F32)

        def finish(total):
            vals = (total,) if epilogue is None else epilogue(total, *[e[...] for e in ex])
            for o, v in zip(outs, vals):
                o[...] = v.astype(o.dtype)

        if nk == 1:
            finish(prod)
            return
        acc, k = rest[-1], pl.program_id(3)

        @pl.when(k == 0)
        def _():
            acc[...] = prod

        @pl.when(k > 0)
        def _():
            acc[...] += prod

        @pl.when(k == nk - 1)
        def _():
            finish(acc[...])

    outs = pl.pallas_call(
        body, name=name, grid=(groups, s // tm, nn, nk),
        in_specs=[a_spec, w_spec, *ex_specs] + [_ANY] * n_dep, out_specs=[out_spec] * n_out,
        out_shape=[jax.ShapeDtypeStruct((s, groups * ndim), dt) for dt in out_dtypes],
        scratch_shapes=[pltpu.VMEM((tm, tn), F32)] if nk > 1 else [],
        compiler_params=_params("parallel", "parallel", "parallel", "arbitrary"),
    )(a, w.arr, *extras, *deps)
    return outs[0] if n_out == 1 else outs


def _mm_dw(name, a, b, like, *, layer=0, groups=1, deps=()):
    s, ka_total = a.shape
    rows, cols = ka_total // groups, b.shape[1] // groups
    assert (rows, cols) == (like.rows, like.cols) and b.shape[0] == s, (name, a.shape, b.shape)
    span = groups > 1 and not like.col
    tm, tn, tk = rows if span else like.tile_rows(MM_TM), like.tile_cols(MM_TN), _tile(s, MM_TK)
    nr, nc, nk = rows // tm, cols // tn, s // tk
    assert nk == 1 or not span

    def lay(g):
        return g if groups > 1 else layer

    in_specs = [pl.BlockSpec((tk, tm), lambda g, n, i, k: (k, g * nr + i)),
                pl.BlockSpec((tk, tn), lambda g, n, i, k: (k, g * nc + n))]
    in_specs += [_ANY] * len(deps)

    def body(a_ref, b_ref, *rest):
        av, bv = a_ref[...], b_ref[...]
        if av.dtype != BF16:
            av = av.astype(BF16)
        if bv.dtype != BF16:
            bv = bv.astype(BF16)
        prod = lax.dot_general(av, bv, (((0,), (0,)), ((), ())), preferred_element_type=F32)
        if nk == 1:
            rest[-1][...] = prod.astype(BF16).reshape(rest[-1].shape)
            return
        o_ref, acc, k = rest[-2], rest[-1], pl.program_id(3)

        @pl.when(k == 0)
        def _():
            acc[...] = prod

        @pl.when(k > 0)
        def _():
            acc[...] += prod

        @pl.when(k == nk - 1)
        def _():
            o_ref[...] = acc[...].astype(BF16)

    if span:
        out_spec = pl.BlockSpec((like.nj, None, like.r, tn), lambda g, n, i, k: (0, g, 0, n))
    else:
        out_spec = pl.BlockSpec((None, None, tm, tn), lambda g, n, i, k: like.index(lay(g), i, n, tm, tn))
    return pl.pallas_call(
        body, name=name, grid=(groups, nc, nr, nk), in_specs=in_specs, out_specs=out_spec,
        out_shape=jax.ShapeDtypeStruct((like.nj, like.nl, like.r, like.c), BF16),
        scratch_shapes=[pltpu.VMEM((tm, tn), F32)] if nk > 1 else [],
        compiler_params=_params("parallel", "parallel", "parallel", "arbitrary"),
    )(a, b, *deps)


def _mm_bwd(name, dy, w, a, *, deps=()):
    s, c = dy.shape
    assert not w.col and w.nl == 1 and w.c == c and a.shape == (s, w.rows), (name, dy.shape, a.shape)
    r = w.r

    def body(dy_ref, w_ref, a_ref, *rest):
        dx_ref, dw_ref = rest[-2:]
        dyv = dy_ref[...]
        dx_ref[...] = lax.dot_general(dyv, w_ref[...], (((1,), (1,)), ((), ())), preferred_element_type=F32)
        dw_ref[...] = lax.dot_general(a_ref[...], dyv, (((0,), (0,)), ((), ())),
                                      preferred_element_type=F32).astype(BF16)

    return pl.pallas_call(
        body, name=name, grid=(w.nj,),
        in_specs=[pl.BlockSpec((s, c), lambda j: (0, 0)), pl.BlockSpec((None, None, r, c), lambda j: (j, 0, 0, 0)),
                  pl.BlockSpec((s, r), lambda j: (0, j))] + [_ANY] * len(deps),
        out_specs=[pl.BlockSpec((s, r), lambda j: (0, j)), pl.BlockSpec((None, None, r, c), lambda j: (j, 0, 0, 0))],
        out_shape=[jax.ShapeDtypeStruct((s, w.rows), F32), jax.ShapeDtypeStruct((w.nj, 1, r, c), BF16)],
        compiler_params=_params("parallel"),
    )(dy, w.arr, a, *deps)


def _row_spec(tr, d):
    return pl.BlockSpec((tr, d), lambda i: (i, 0))


def _vec_spec(d):
    return pl.BlockSpec((1, d), lambda i: (0, 0))


def _rms_fwd(name, x, g):
    s, d = x.shape
    tr = _tile(s, ROW_TILE)

    def body(x_ref, g_ref, o_ref):
        xv = x_ref[...]
        r = lax.rsqrt(jnp.mean(xv * xv, axis=-1, keepdims=True) + EPS)
        o_ref[...] = (xv * r * g_ref[...]).astype(BF16)

    return pl.pallas_call(
        body, name=name, grid=(s // tr,), in_specs=[_row_spec(tr, d), _vec_spec(d)],
        out_specs=_row_spec(tr, d), out_shape=jax.ShapeDtypeStruct((s, d), BF16),
        compiler_params=_params("parallel"),
    )(x, g)


def _rms_bwd(name, x, g, dhn, dres):
    s, d = x.shape
    tr = _tile(s, ROW_TILE)

    def body(x_ref, g_ref, dhn_ref, dres_ref, dx_ref, dxb_ref, dg_ref):
        @pl.when(pl.program_id(0) == 0)
        def _():
            dg_ref[...] = jnp.zeros_like(dg_ref)

        xv = x_ref[...]
        r = lax.rsqrt(jnp.mean(xv * xv, axis=-1, keepdims=True) + EPS)
        xh = xv * r
        dy = dhn_ref[...]
        dg_ref[...] += jnp.sum(dy * xh, axis=0, keepdims=True)
        dxh = dy * g_ref[...]
        dx = dres_ref[...] + r * (dxh - xh * jnp.mean(dxh * xh, axis=-1, keepdims=True))
        dx_ref[...] = dx
        dxb_ref[...] = dx.astype(BF16)

    return pl.pallas_call(
        body, name=name, grid=(s // tr,),
        in_specs=[_row_spec(tr, d), _vec_spec(d), _row_spec(tr, d), _row_spec(tr, d)],
        out_specs=[_row_spec(tr, d), _row_spec(tr, d), _vec_spec(d)],
        out_shape=[jax.ShapeDtypeStruct((s, d), F32), jax.ShapeDtypeStruct((s, d), BF16),
                   jax.ShapeDtypeStruct((1, d), F32)],
        compiler_params=_params("arbitrary"),
    )(x, g, dhn, dres)


def _final(name, h, g, tgt):
    s, d = h.shape
    tr = _tile(s, ROW_TILE)

    def body(h_ref, g_ref, t_ref, dh_ref, dhb_ref, dg_ref, loss_ref):
        @pl.when(pl.program_id(0) == 0)
        def _():
            dg_ref[...] = jnp.zeros_like(dg_ref)
            loss_ref[...] = jnp.zeros_like(loss_ref)

        hv = h_ref[...]
        r = lax.rsqrt(jnp.mean(hv * hv, axis=-1, keepdims=True) + EPS)
        xh = hv * r
        gv = g_ref[...]
        err = xh * gv - t_ref[...]
        part = 0.5 * jnp.sum(jnp.mean(err * err, axis=-1, keepdims=True), axis=0, keepdims=True)
        loss_ref[...] += jnp.broadcast_to(part, loss_ref.shape)
        dy = err * (1.0 / d)
        dg_ref[...] += jnp.sum(dy * xh, axis=0, keepdims=True)
        dxh = dy * gv
        dh = r * (dxh - xh * jnp.mean(dxh * xh, axis=-1, keepdims=True))
        dh_ref[...] = dh
        dhb_ref[...] = dh.astype(BF16)

    return pl.pallas_call(
        body, name=name, grid=(s // tr,),
        in_specs=[_row_spec(tr, d), _vec_spec(d), _row_spec(tr, d)],
        out_specs=[_row_spec(tr, d), _row_spec(tr, d), _vec_spec(d), _vec_spec(LANES)],
        out_shape=[jax.ShapeDtypeStruct((s, d), F32), jax.ShapeDtypeStruct((s, d), BF16),
                   jax.ShapeDtypeStruct((1, d), F32), jax.ShapeDtypeStruct((1, LANES), F32)],
        compiler_params=_params("arbitrary"),
    )(h, g, tgt)


_SQRT_HALF = 1.0 / math.sqrt(2.0)
_INV_SQRT_2PI = 1.0 / math.sqrt(2.0 * math.pi)


def _gelu(x):
    return x * (lax.erf(x * _SQRT_HALF) + 1.0) * 0.5


def _gelu_grad(x):
    return 0.5 * (lax.erf(x * _SQRT_HALF) + 1.0) + x * jnp.exp(-0.5 * x * x) * _INV_SQRT_2PI


def _causal(chunk):
    t = lax.broadcasted_iota(jnp.int32, (chunk, chunk), 0)
    sidx = lax.broadcasted_iota(jnp.int32, (chunk, chunk), 1)
    return sidx <= t


def _layernorm_parts(v, g, b):
    mu = jnp.mean(v, axis=-1, keepdims=True)
    vc = v - mu
    rs = lax.rsqrt(jnp.mean(vc * vc, axis=-1, keepdims=True) + EPS)
    vhat = vc * rs
    return vhat, rs, vhat * g + b


def _gate_fwd(name, zpre, ln_g, ln_b, w_s, b_col):
    s, aw2 = zpre.shape
    aw = aw2 // 2
    ng, chunk, _ = w_s.shape
    dh = aw // ng

    def body(z_ref, g_ref, b_ref, ws_ref, bc_ref, o_ref):
        u = _gelu(z_ref[:, :aw])
        v = _gelu(z_ref[:, aw:])
        _, _, vln = _layernorm_parts(v, g_ref[...], b_ref[...])
        mask = _causal(chunk)
        for gi in range(ng):
            sl = slice(gi * dh, (gi + 1) * dh)
            wm = jnp.where(mask, ws_ref[gi], 0.0).astype(BF16)
            sg = jnp.dot(wm, vln[:, sl].astype(BF16), preferred_element_type=F32) + bc_ref[:, gi:gi + 1]
            o_ref[:, sl] = (u[:, sl] * sg).astype(BF16)

    return pl.pallas_call(
        body, name=name, grid=(s // chunk,),
        in_specs=[_row_spec(chunk, aw2), _vec_spec(aw), _vec_spec(aw),
                  pl.BlockSpec((ng, chunk, chunk), lambda i: (0, 0, 0)),
                  pl.BlockSpec((chunk, ng), lambda i: (0, 0))],
        out_specs=_row_spec(chunk, aw), out_shape=jax.ShapeDtypeStruct((s, aw), BF16),
        compiler_params=_params("parallel"),
    )(zpre, ln_g, ln_b, w_s, b_col)


def _gate_bwd(name, zpre, dgated, ln_g, ln_b, w_s, b_col):
    s, aw2 = zpre.shape
    aw = aw2 // 2
    ng, chunk, _ = w_s.shape
    dh = aw // ng

    def body(z_ref, dgt_ref, g_ref, b_ref, ws_ref, bc_ref, dz_ref, dws_ref, dbc_ref, dlg_ref, dlb_ref,
             du_scr, dvln_scr):
        @pl.when(pl.program_id(0) == 0)
        def _():
            dws_ref[...] = jnp.zeros_like(dws_ref)
            dbc_ref[...] = jnp.zeros_like(dbc_ref)
            dlg_ref[...] = jnp.zeros_like(dlg_ref)
            dlb_ref[...] = jnp.zeros_like(dlb_ref)

        zu = z_ref[:, :aw]
        zv = z_ref[:, aw:]
        u = _gelu(zu)
        lg = g_ref[...]
        vhat, rs, vln = _layernorm_parts(_gelu(zv), lg, b_ref[...])
        mask = _causal(chunk)
        for gi in range(ng):
            sl = slice(gi * dh, (gi + 1) * dh)
            wm = jnp.where(mask, ws_ref[gi], 0.0).astype(BF16)
            vg = vln[:, sl].astype(BF16)
            sg = jnp.dot(wm, vg, preferred_element_type=F32) + bc_ref[:, gi:gi + 1]
            dgt = dgt_ref[:, sl]
            du_scr[:, sl] = dgt * sg
            ds = dgt * u[:, sl]
            dbc_ref[:, gi:gi + 1] += jnp.sum(ds, axis=-1, keepdims=True)
            dsb = ds.astype(BF16)
            dwm = lax.dot_general(dsb, vg, (((1,), (1,)), ((), ())), preferred_element_type=F32)
            dws_ref[gi] += jnp.where(mask, dwm, 0.0)
            dvln_scr[:, sl] = lax.dot_general(wm, dsb, (((0,), (0,)), ((), ())), preferred_element_type=F32)
        dvln = dvln_scr[...]
        dlb_ref[...] += jnp.sum(dvln, axis=0, keepdims=True)
        dlg_ref[...] += jnp.sum(dvln * vhat, axis=0, keepdims=True)
        dvh = dvln * lg
        dv = rs * (dvh - jnp.mean(dvh, axis=-1, keepdims=True)
                   - vhat * jnp.mean(dvh * vhat, axis=-1, keepdims=True))
        dz_ref[:, :aw] = (du_scr[...] * _gelu_grad(zu)).astype(BF16)
        dz_ref[:, aw:] = (dv * _gelu_grad(zv)).astype(BF16)

    return pl.pallas_call(
        body, name=name, grid=(s // chunk,),
        in_specs=[_row_spec(chunk, aw2), _row_spec(chunk, aw), _vec_spec(aw), _vec_spec(aw),
                  pl.BlockSpec((ng, chunk, chunk), lambda i: (0, 0, 0)),
                  pl.BlockSpec((chunk, ng), lambda i: (0, 0))],
        out_specs=[_row_spec(chunk, aw2), pl.BlockSpec((ng, chunk, chunk), lambda i: (0, 0, 0)),
                   pl.BlockSpec((chunk, ng), lambda i: (0, 0)), _vec_spec(aw), _vec_spec(aw)],
        out_shape=[jax.ShapeDtypeStruct((s, aw2), BF16), jax.ShapeDtypeStruct((ng, chunk, chunk), F32),
                   jax.ShapeDtypeStruct((chunk, ng), F32), jax.ShapeDtypeStruct((1, aw), F32),
                   jax.ShapeDtypeStruct((1, aw), F32)],
        scratch_shapes=[pltpu.VMEM((chunk, aw), F32), pltpu.VMEM((chunk, aw), F32)],
        compiler_params=_params("arbitrary"),
    )(zpre, dgated, ln_g, ln_b, w_s, b_col)


def _pool_select(g, parts):
    out = parts[-1]
    for gi in range(len(parts) - 2, -1, -1):
        out = jnp.where(g == gi, parts[gi], out)
    return out


def _pool_specs(s, bw):
    head = bw // len(B_WINDOWS)
    tc = _tile(head, 256)
    nb = head // tc
    return tc, (len(B_WINDOWS), nb), pl.BlockSpec((s, tc), lambda g, j: (0, g * nb + j))


def _pool_window(g, t):
    w = _pool_select(g, [jnp.full(t.shape, wi, jnp.int32) for wi in B_WINDOWS])
    return jnp.minimum(t + 1, w).astype(F32)


def _pool_fwd(name, vb):
    assert B_WINDOWS == (2, 4, 8, 16)
    s, bw = vb.shape
    tc, grid, spec = _pool_specs(s, bw)

    def body(v_ref, o_ref):
        g = pl.program_id(0)
        v = v_ref[...]
        t = lax.broadcasted_iota(jnp.int32, (s, tc), 0)

        def down(x, k):
            return jnp.where(t >= k, pltpu.roll(x, k, 0), 0.0)

        sums, cur, k = [], v, 1
        for _ in B_WINDOWS:
            cur = cur + down(cur, k)
            sums.append(cur)
            k *= 2
        o_ref[...] = (_pool_select(g, sums) / _pool_window(g, t) - v).astype(BF16)

    return pl.pallas_call(
        body, name=name, grid=grid, in_specs=[spec], out_specs=spec,
        out_shape=jax.ShapeDtypeStruct((s, bw), BF16), compiler_params=_params("parallel", "parallel"),
    )(vb)


def _pool_bwd(name, dpooled):
    s, bw = dpooled.shape
    tc, grid, spec = _pool_specs(s, bw)

    def body(d_ref, o_ref):
        g = pl.program_id(0)
        dp = d_ref[...]
        t = lax.broadcasted_iota(jnp.int32, (s, tc), 0)

        def up(x, k):
            return jnp.where(t < s - k, pltpu.roll(x, s - k, 0), 0.0)

        sums, cur, k = [], dp / _pool_window(g, t), 1
        for _ in B_WINDOWS:
            cur = cur + up(cur, k)
            sums.append(cur)
            k *= 2
        o_ref[...] = (_pool_select(g, sums) - dp).astype(BF16)

    return pl.pallas_call(
        body, name=name, grid=grid, in_specs=[spec], out_specs=spec,
        out_shape=jax.ShapeDtypeStruct((s, bw), BF16), compiler_params=_params("parallel", "parallel"),
    )(dpooled)


def _scale_bwd(name, dms, mixed, scale):
    s, bw = dms.shape
    tr = _tile(s, ROW_TILE)

    def body(d_ref, m_ref, sc_ref, o_ref, ds_ref):
        @pl.when(pl.program_id(0) == 0)
        def _():
            ds_ref[...] = jnp.zeros_like(ds_ref)

        dv = d_ref[...]
        ds_ref[...] += jnp.sum(dv * m_ref[...], axis=0, keepdims=True)
        o_ref[...] = (dv * sc_ref[...]).astype(BF16)

    return pl.pallas_call(
        body, name=name, grid=(s // tr,), in_specs=[_row_spec(tr, bw), _row_spec(tr, bw), _vec_spec(bw)],
        out_specs=[_row_spec(tr, bw), _vec_spec(bw)],
        out_shape=[jax.ShapeDtypeStruct((s, bw), BF16), jax.ShapeDtypeStruct((1, bw), F32)],
        compiler_params=_params("arbitrary"),
    )(dms, mixed, scale)


def _adam_update(w, g, m, v):
    m = ADAM_B1 * m + (1.0 - ADAM_B1) * g
    v = ADAM_B2 * v + (1.0 - ADAM_B2) * (g * g)
    m_hat = m / (1.0 - ADAM_B1 ** ADAM_STEP)
    v_hat = v / (1.0 - ADAM_B2 ** ADAM_STEP)
    delta = -ADAM_LR * (m_hat / (jnp.sqrt(v_hat) + ADAM_EPS) + ADAM_WD * w)
    return delta, m, v


def _adam_shard(name, w, m, v, parts, layer=0, prev=None, deps=()):
    nl, r, c = w.shape
    nj, nlp = parts.shape[:2]
    tr = _tile(r, ROW_TILE // 2)
    spec = pl.BlockSpec((None, tr, c), lambda l, i: (layer + l, i, 0))

    def body(w_ref, m_ref, v_ref, p_ref, *rest):
        g_ref, d_ref, nm_ref, nv_ref = rest[-4:]
        g = p_ref[0].astype(F32)
        for j in range(1, nj):
            g = g + p_ref[j].astype(F32)
        delta, nm, nv = _adam_update(w_ref[...], g, m_ref[...], v_ref[...])
        g_ref[...] = g
        d_ref[...] = delta
        nm_ref[...] = nm
        nv_ref[...] = nv

    prev = () if prev is None else tuple(prev)
    return pl.pallas_call(
        body, name=name, grid=(nlp, r // tr),
        in_specs=[spec, spec, spec, pl.BlockSpec((nj, None, tr, c), lambda l, i: (0, l, i, 0))]
        + [_ANY] * (len(prev) + len(deps)),
        out_specs=[spec] * 4, out_shape=[jax.ShapeDtypeStruct(w.shape, F32)] * 4,
        input_output_aliases={4 + i: i for i in range(len(prev))},
        compiler_params=_params("parallel", "parallel"),
    )(w, m, v, parts, *prev, *deps)


def _adam_packed(name, w, g, m, v):
    rows, lanes = w.shape
    tr = rows
    spec = pl.BlockSpec((tr, lanes), lambda i: (i, 0))

    def body(w_ref, g_ref, m_ref, v_ref, d_ref, nm_ref, nv_ref):
        delta, nm, nv = _adam_update(w_ref[...], g_ref[...], m_ref[...], v_ref[...])
        d_ref[...] = delta
        nm_ref[...] = nm
        nv_ref[...] = nv

    return pl.pallas_call(
        body, name=name, grid=(rows // tr,), in_specs=[spec] * 4, out_specs=[spec] * 3,
        out_shape=[jax.ShapeDtypeStruct(w.shape, F32)] * 3, compiler_params=_params("parallel"),
    )(w, g, m, v)


def _place():
    x, y, c = lax.axis_index("x"), lax.axis_index("y"), lax.axis_index("c")
    chips = [(1 - x, y), (x, 1 - y), (1 - x, 1 - y)]
    return x, y, c, chips


def _remote(src, dst, send_sem, recv_sem, device):
    return pltpu.make_async_remote_copy(src_ref=src, dst_ref=dst, send_sem=send_sem, recv_sem=recv_sem,
                                        device_id=device, device_id_type=MESH)


def _half(ref_rows, cc):
    h = ref_rows // 2
    return pl.ds(cc * h, h)


_HBM = pl.BlockSpec(memory_space=pltpu.HBM)
_SEM = pl.BlockSpec(memory_space=pltpu.SEMAPHORE)
_EFFECT = pltpu.SideEffectType.DATAFLOW_SIDE_EFFECTING


def _in_hbm(arr):
    return pltpu.with_memory_space_constraint(arr, pltpu.HBM)


def _gather_rows(land, cc, part=(0, 1)):
    k, n = part
    h = land.shape[2] // 2
    return pl.ds(cc * h + k * (h // n), h // n)


def _gather_block(land, split, j, cc, part=(0, 1)):
    if split:
        return land.at[j, :, _gather_rows(land, cc, part), :]
    return land.at[j]


def _split_start(name, srcs, lands, plan, deps=(), groups=None):
    ns = len(srcs)
    nl = len(lands) if groups is None else groups
    both = list(srcs) + list(lands)
    nb = len(both)

    def body(*refs):
        s, ld = refs[:ns], refs[ns:nb]
        outs = refs[nb + len(deps):]
        send, recv, token = outs[:nl], outs[nl:2 * nl], outs[-1]
        for a, copies in enumerate(plan(s, ld)):
            for src, dst, peer in copies:
                _remote(src, dst, send[a], recv[a], peer).start()
        token[...] = jnp.zeros_like(token)

    outs = pl.pallas_call(
        body, name=name, in_specs=[_HBM] * nb + [_ANY] * len(deps),
        out_specs=[_SEM] * (2 * nl) + [_HBM] * nb + [pl.BlockSpec(memory_space=pltpu.VMEM)],
        out_shape=[pltpu.SemaphoreType.DMA(())] * (2 * nl) + [pltpu.HBM(b.shape, b.dtype) for b in both]
        + [jax.ShapeDtypeStruct((8, LANES), F32)],
        input_output_aliases={i: 2 * nl + i for i in range(nb)},
        compiler_params=pltpu.CompilerParams(has_side_effects=_EFFECT),
    )(*[_in_hbm(b) for b in both], *deps)
    thru = outs[2 * nl:2 * nl + nb]
    return list(outs[:nl]), list(outs[nl:2 * nl]), list(thru[:ns]), list(thru[ns:]), outs[-1]


def _split_wait(name, srcs, lands, send, recv, after, whole):
    ns, nl = len(srcs), len(send)
    both = list(srcs) + list(lands)
    nb = len(both)

    def body(*refs):
        ld, snd, rcv = refs[ns:nb], refs[nb:nb + nl], refs[nb + nl:nb + 2 * nl]
        x, y, c, _ = _place()
        for a, blk in enumerate(whole(ld)):
            every = _remote(blk, blk, snd[a], rcv[a], (x, y, c))
            every.wait_send()
            every.wait_recv()

    outs = pl.pallas_call(
        body, name=name, in_specs=[_HBM] * nb + [_SEM] * (2 * nl) + [_ANY], out_specs=[_HBM] * nb,
        out_shape=[pltpu.HBM(b.shape, b.dtype) for b in both],
        input_output_aliases={i: i for i in range(nb)},
        compiler_params=pltpu.CompilerParams(has_side_effects=_EFFECT),
    )(*both, *send, *recv, after)
    return list(outs[:ns]), list(outs[ns:])


def _gather_plan(split, parts):
    def plan(srcs, lands):
        x, y, c, chips = _place()
        out = []
        for ld, sp, n in zip(lands, split, parts):
            for k in range(n):
                blk = _gather_block(ld, sp, 2 * x + y, c, (k, n))
                out.append([(blk, blk, (qx, qy, c)) for qx, qy in chips])
        return out
    return plan


def _gather_whole(split, part=(0, 1)):
    def whole(lands):
        _, _, c, _ = _place()
        return [ld.at[pl.ds(0, 3), :, _gather_rows(ld, c, part), :] if sp else ld.at[pl.ds(0, 3)]
                for ld, sp in zip(lands, split)]
    return whole


def _relay_plan(srcs, lands):
    x, y, c, _ = _place()
    out = []
    for ld in lands:
        blocks = [ld.at[j, :, _half(ld.shape[2], c), :] for j in range(N_CHIPS)]
        out.append([(blk, blk, (x, y, 1 - c)) for blk in blocks])
    return out


def _relay_whole(lands):
    _, _, c, _ = _place()
    return [ld.at[:, :, _half(ld.shape[2], c), :] for ld in lands]


def _fill_own(name, shard, dtype, place, layer=None):
    nl, r, c = shard.shape
    first = 0
    if layer is not None:
        nl, first = 1, layer
    tr = _tile(r, 512)

    def body(p_ref, s_ref, o_ref):
        o_ref[...] = s_ref[...].astype(o_ref.dtype)

    return pl.pallas_call(
        body, name=name,
        grid_spec=pltpu.PrefetchScalarGridSpec(
            num_scalar_prefetch=1, grid=(nl, r // tr),
            in_specs=[pl.BlockSpec((None, tr, c), lambda l, i, p: (first + l, i, 0))],
            out_specs=pl.BlockSpec((None, None, tr, c), lambda l, i, p: (p[0], l, i, 0))),
        out_shape=jax.ShapeDtypeStruct((N_CHIPS, nl, r, c), dtype),
        compiler_params=_params("parallel", "parallel"),
    )(place, shard)


def _gather_finish(name, lands, part=(0, 1)):
    n = len(lands)

    def body(*refs):
        outs = refs[n:2 * n]
        fsend, frecv = refs[2 * n:]
        x, y, c, chips = _place()
        sib = (x, y, 1 - c)

        def relay(a, qi, cc):
            qx, qy = chips[qi]
            blk = _gather_block(outs[a], True, 2 * qx + qy, cc, part)
            return _remote(blk, blk, fsend.at[a, qi], frecv.at[a, qi], sib)

        relays = [relay(a, qi, c) for a in range(n) for qi in range(3)]
        for cp in relays:
            cp.start()
        for a in range(n):
            for qi in range(3):
                relay(a, qi, 1 - c).wait_recv()
        for cp in relays:
            cp.wait_send()

    outs = pl.pallas_call(
        body, name=name, in_specs=[_ANY] * n, out_specs=[_ANY] * n,
        out_shape=[jax.ShapeDtypeStruct(ld.shape, ld.dtype) for ld in lands],
        input_output_aliases={i: i for i in range(n)},
        scratch_shapes=[pltpu.SemaphoreType.DMA((n, 3))] * 2,
    )(*lands)
    return list(outs)


def _pair_plan(srcs, lands):
    x, y, c, _ = _place()
    return [[(s.at[:, :, _half(s.shape[2], 1 - c), :], ld, (x, y, 1 - c))] for s, ld in zip(srcs, lands)]


def _pair_whole(lands):
    return list(lands)


def _scatter_plan(srcs, lands):
    x, y, c, chips = _place()
    return [[(s.at[2 * qx + qy], ld.at[2 * x + y, :, _half(ld.shape[2], c), :], (qx, qy, c)) for qx, qy in chips]
            for s, ld in zip(srcs, lands)]


def _scatter_whole(lands):
    _, _, c, _ = _place()
    return [ld.at[pl.ds(0, 3), :, _half(ld.shape[2], c), :] for ld in lands]


def _pair_sum(name, grad, other, place):
    nj, nl, r, c = grad.shape
    h = r // 2
    tr = _tile(h, 2 * ROW_TILE)
    nb = h // tr

    def body(p_ref, g_ref, o_ref, q_ref, d_ref):
        q = (g_ref[...].astype(F32) + o_ref[...].astype(F32)).astype(BF16)
        q_ref[...] = q

        @pl.when(pl.program_id(2) == p_ref[0])
        def _():
            d_ref[...] = q

    blk = (None, None, tr, c)
    return pl.pallas_call(
        body, name=name,
        grid_spec=pltpu.PrefetchScalarGridSpec(
            num_scalar_prefetch=1, grid=(nl, nb, nj),
            in_specs=[pl.BlockSpec(blk, lambda l, i, j, p: (j, l, p[1] * nb + i, 0)),
                      pl.BlockSpec(blk, lambda l, i, j, p: (j, l, i, 0))],
            out_specs=[pl.BlockSpec(blk, lambda l, i, j, p: (j, l, i, 0)),
                       pl.BlockSpec(blk, lambda l, i, j, p: (p[0], l, p[1] * nb + i, 0))]),
        out_shape=[jax.ShapeDtypeStruct((nj, nl, h, c), BF16), jax.ShapeDtypeStruct((nj, nl, r, c), BF16)],
        compiler_params=_params("parallel", "parallel", "arbitrary"),
    )(place, grad, other)


def _own_of_eight(name, packed, device):
    rows, lanes = packed.shape
    tr = rows

    def body(d_ref, s_ref, o_ref):
        o_ref[...] = s_ref[...]

    return pl.pallas_call(
        body, name=name,
        grid_spec=pltpu.PrefetchScalarGridSpec(
            num_scalar_prefetch=1, grid=(rows // tr,),
            in_specs=[pl.BlockSpec((tr, lanes), lambda i, d: (i, 0))],
            out_specs=pl.BlockSpec((None, tr, lanes), lambda i, d: (d[0], i, 0))),
        out_shape=jax.ShapeDtypeStruct((N_DEV, rows, lanes), packed.dtype),
        compiler_params=_params("parallel"),
    )(device, packed)


def _all_plan(srcs, lands):
    x, y, c, _ = _place()
    (ld,) = lands
    blk = ld.at[4 * x + 2 * y + c]
    flips = [(a, b, d) for a in (0, 1) for b in (0, 1) for d in (0, 1) if a + b + d]
    return [[(blk, blk, (x + a - 2 * a * x, y + b - 2 * b * y, c + d - 2 * d * c)) for a, b, d in flips]]


def _all_whole(lands):
    return [lands[0].at[pl.ds(0, N_DEV - 1)]]


def _sum_of_eight(name, slots):
    n, rows, lanes = slots.shape
    tr = rows

    def body(s_ref, o_ref):
        total = s_ref[0]
        for d in range(1, n):
            total = total + s_ref[d]
        o_ref[...] = total

    return pl.pallas_call(
        body, name=name, grid=(rows // tr,),
        in_specs=[pl.BlockSpec((n, tr, lanes), lambda i: (0, i, 0))],
        out_specs=pl.BlockSpec((tr, lanes), lambda i: (i, 0)),
        out_shape=jax.ShapeDtypeStruct((rows, lanes), F32), compiler_params=_params("parallel"),
    )(slots)


def _pack(parts):
    rows = []
    for p in parts:
        flat = p.reshape(-1)
        pad = (-flat.shape[0]) % PACK_ELEMS
        rows.append(jnp.pad(flat, (0, pad)).reshape(-1, LANES))
    return jnp.concatenate(rows, axis=0)


def _unpack(packed, shapes):
    out, row = [], 0
    for sh in shapes:
        size = math.prod(sh)
        nrows = -(-size // PACK_ELEMS) * (PACK_ELEMS // LANES)
        out.append(packed[row:row + nrows].reshape(-1)[:size].reshape(sh))
        row += nrows
    return out


def kernel(x, a_w_in, a_ln_g, a_ln_b, a_w_s, a_b_s, a_w_out, b_w_in, b_w_grp, b_scale, b_w_out, norm_mix, norm_mlp, mlp_w1, mlp_w2, final_norm, loss_target, m_a_w_in, m_a_ln_g, m_a_ln_b, m_a_w_s, m_a_b_s, m_a_w_out, m_b_w_in, m_b_w_grp, m_b_scale, m_b_w_out, m_norm_mix, m_norm_mlp, m_mlp_w1, m_mlp_w2, m_final_norm, v_a_w_in, v_a_ln_g, v_a_ln_b, v_a_w_s, v_a_b_s, v_a_w_out, v_b_w_in, v_b_w_grp, v_b_scale, v_b_w_out, v_norm_mix, v_norm_mlp, v_mlp_w1, v_mlp_w2, v_final_norm):
    xi, yi, ci = lax.axis_index("x"), lax.axis_index("y"), lax.axis_index("c")
    chip = 2 * xi + yi
    place = jnp.stack([chip, ci]).astype(jnp.int32)
    x2, tgt = x[0], loss_target[0]
    bw = b_scale.shape[1] * N_CHIPS

    units = dict(a_w_in=(a_w_in, None), a_w_out=(a_w_out, None), w1_0=(mlp_w1, 0), w2_0=(mlp_w2, 0),
                 b_scale=(b_scale.reshape(1, 1, -1), None), b_w_in=(b_w_in, None), b_w_grp=(b_w_grp[0], None),
                 b_w_out=(b_w_out, None), w1_1=(mlp_w1, 1), w2_1=(mlp_w2, 1))
    col_sharded = dict(a_w_in=True, a_w_out=False, b_w_in=False, b_w_grp=False, b_w_out=False,
                       w1_0=True, w2_0=False, w1_1=True, w2_1=False)
    in_flight, W = {}, {}

    def launch(tag, keys, deps):
        sp = [k != "b_scale" for k in keys]
        parts = [pieces.get(k, 1) for k in keys]
        zones = [_fill_own(f"gather_own_{k}", units[k][0], BF16 if s else F32, place, layer=units[k][1])
                 for k, s in zip(keys, sp)]
        send, recv, _, zones, tok = _split_start(f"gather_start_{tag}", [], zones, _gather_plan(sp, parts), deps,
                                                 groups=sum(parts))
        first = 0
        for k, z, s, n in zip(keys, zones, sp, parts):
            in_flight[k] = (send[first:first + n], recv[first:first + n], z, s)
            first += n
        return tok

    def arrive_piece(key, k, after):
        send, recv, zone, sp = in_flight[key]
        n = len(send)
        _, zones = _split_wait(f"gather_wait_{key}_{k}", [], [zone], [send[k]], [recv[k]], after,
                               _gather_whole([sp], (k, n)))
        (zone,) = _gather_finish(f"gather_finish_{key}_{k}", zones, (k, n))
        in_flight[key] = (send, recv, zone, sp)
        W[key] = _W4(zone, col_sharded[key])

    def arrive(keys, after):
        send, recv, zones, sp = zip(*[in_flight[k] for k in keys])
        if len(send[0]) > 1:
            for k in range(len(send[0])):
                arrive_piece(keys[0], k, after)
            return
        _, zones = _split_wait(f"gather_wait_{keys[0]}", [], zones, [s[0] for s in send], [r[0] for r in recv],
                               after, _gather_whole(sp))
        relayed = iter(_gather_finish(f"gather_finish_{keys[0]}", [z for z, s in zip(zones, sp) if s]))
        for k, z, s in zip(keys, zones, sp):
            full = next(relayed) if s else z
            W[k] = _W4(full, col_sharded[k]) if k in col_sharded else full

    pieces = dict(w1_0=2, w2_0=2, w1_1=2, w2_1=2)

    token = launch("first", ["a_w_in", "a_w_out"], ())
    token = launch("rest", ["w1_0", "w2_0", "b_scale", "b_w_in", "b_w_grp", "b_w_out", "w1_1", "w2_1"], (token,))

    b_col = a_b_s[0].T

    def residual(acc, res):
        return (res + acc,)

    def sq_relu(acc):
        act = jnp.maximum(acc, 0.0)
        return act, act * act

    def mlp_fwd(tag, h, layer):
        hn = _rms_fwd(f"mlp{tag}_norm", h, norm_mlp[layer:layer + 1])
        arrive([f"w1_{layer}"], hn)
        act, act_sq = _mm_aw(f"mlp{tag}_up", hn, W[f"w1_{layer}"], out_dtypes=(BF16, BF16), epilogue=sq_relu)
        out = h
        for k in range(pieces[f"w2_{layer}"]):
            arrive_piece(f"w2_{layer}", k, act_sq if k == 0 else out)
            out = _mm_aw(f"mlp{tag}_down_{k}", act_sq, W[f"w2_{layer}"], extras=(out,), epilogue=residual,
                         k_piece=(k, pieces[f"w2_{layer}"]))
        return out, (h, hn, act, act_sq)

    hn0 = _rms_fwd("mix_a_norm", x2, norm_mix[0:1])
    arrive(["a_w_in"], token)
    zpre = _mm_aw("mix_a_in", hn0, W["a_w_in"])
    gated = _gate_fwd("mix_a_gate", zpre, a_ln_g, a_ln_b, a_w_s[0], b_col)
    arrive(["a_w_out"], gated)
    h1 = _mm_aw("mix_a_out", gated, W["a_w_out"], extras=(x2,), epilogue=residual)
    h2, mlp0 = mlp_fwd("0", h1, 0)
    hn2 = _rms_fwd("mix_b_norm", h2, norm_mix[1:2])
    arrive(["b_scale", "b_w_in"], hn2)
    scale_full = W["b_scale"].reshape(1, bw)
    vb = _mm_aw("mix_b_in", hn2, W["b_w_in"])
    pooled = _pool_fwd("mix_b_pool", vb)
    arrive(["b_w_grp", "b_w_out"], pooled)
    mixed, ms = _mm_aw("mix_b_grp", pooled, W["b_w_grp"], groups=len(B_WINDOWS), extras=(scale_full,),
                       out_dtypes=(F32, BF16), epilogue=lambda acc, sc: (acc, acc * sc))
    h3 = _mm_aw("mix_b_out", ms, W["b_w_out"], extras=(h2,), epilogue=residual)
    h4, mlp1 = mlp_fwd("1", h3, 1)
    dh4, dh4_b, d_final, loss_part = _final("loss_head", h4, final_norm.reshape(1, -1), tgt)
    g1_like = _W4(None, True, shape=(N_CHIPS, 1, *W["w1_0"].arr.shape[2:]))
    g2_like = _W4(None, False, shape=(N_CHIPS, 1, *W["w2_0"].arr.shape[2:]))

    def exchange(tag, gs):
        zones = [lax.empty((g.shape[0], g.shape[1], g.shape[2] // 2, g.shape[3]), g.dtype) for g in gs]
        send, recv, srcs, zones, tok = _split_start(f"pair_start_{tag}", gs, zones, _pair_plan)
        return (tag, send, recv, srcs, zones), tok

    def reduce(state, after):
        tag, send, recv, srcs, zones = state
        srcs, zones = _split_wait(f"pair_wait_{tag}", srcs, zones, send, recv, after, _pair_whole)
        both = [_pair_sum(f"pair_sum_{tag}_{i}", g, o, place) for i, (g, o) in enumerate(zip(srcs, zones))]
        send, recv, sums, dests, tok = _split_start(f"scatter_start_{tag}", [b[0] for b in both],
                                                    [b[1] for b in both], _scatter_plan)
        return (tag, send, recv, sums, dests), tok

    def relay(state, after):
        tag, send, recv, sums, dests = state
        _, dests = _split_wait(f"scatter_wait_{tag}", sums, dests, send, recv, after, _scatter_whole)
        send, recv, _, dests, tok = _split_start(f"relay_start_{tag}", [], dests, _relay_plan)
        return (tag, send, recv, dests), tok

    def land(state, after):
        tag, send, recv, dests = state
        return _split_wait(f"relay_wait_{tag}", [], dests, send, recv, after, _relay_whole)[1]

    def mlp_bwd(tag, dh, dh_b, saved, layer, deps, pending=None):
        h, hn, act, act_sq = saved
        dpre = _mm_aw(f"mlp{tag}_down_dx", dh_b, W[f"w2_{layer}"], transpose_w=True, extras=(act,),
                      out_dtypes=(BF16,), epilogue=lambda acc, a: (acc * (2.0 * a),), deps=deps)
        scattering, dw_deps = None, ()
        if pending is not None:
            scattering, tok = reduce(pending, dpre)
            dw_deps = (tok,)
        g_w2 = _mm_dw(f"mlp{tag}_down_dw", act_sq, dh_b, g2_like, deps=dw_deps)
        pair_w2, tok = exchange(f"w2_{layer}", [g_w2])
        dhn = _mm_aw(f"mlp{tag}_up_dx", dpre, W[f"w1_{layer}"], transpose_w=True, deps=(tok,))
        g_w1 = _mm_dw(f"mlp{tag}_up_dw", hn, dpre, g1_like)
        pair_w1, tok1 = exchange(f"w1_{layer}", [g_w1])
        scat_w2, tok2 = reduce(pair_w2, dhn)
        dh_in, dh_in_b, d_norm = _rms_bwd(f"mlp{tag}_norm_bwd", h, norm_mlp[layer:layer + 1], dhn, dh)
        return dh_in, dh_in_b, d_norm, pair_w1, scat_w2, (tok1, tok2), scattering

    dh3, dh3_b, d_norm_mlp1, pair_w1_1, scat_w2_1, toks, _ = mlp_bwd("1", dh4, dh4_b, mlp1, 1, ())
    dms, g_b_out = _mm_bwd("mix_b_out_bwd", dh3_b, W["b_w_out"], ms, deps=toks)
    scat_w1_1, tok = reduce(pair_w1_1, dms)
    dmixed, d_scale = _scale_bwd("mix_b_scale_bwd", dms, mixed, scale_full)
    dpooled = _mm_aw("mix_b_grp_dx", dmixed, W["b_w_grp"], groups=len(B_WINDOWS), transpose_w=True, deps=(tok,))
    g_b_grp = _mm_dw("mix_b_grp_dw", pooled, dmixed, W["b_w_grp"], groups=len(B_WINDOWS))
    dvb = _pool_bwd("mix_b_pool_bwd", dpooled)
    dhn2, g_b_in = _mm_bwd("mix_b_in_bwd", dvb, W["b_w_in"], hn2)
    pair_b, tok = exchange("b", [g_b_out, g_b_grp, g_b_in])
    dh2, dh2_b, d_norm_mix1 = _rms_bwd("mix_b_norm_bwd", h2, norm_mix[1:2], dhn2, dh3)
    dh1, dh1_b, d_norm_mlp0, pair_w1_0, scat_w2_0, toks, scat_b = mlp_bwd("0", dh2, dh2_b, mlp0, 0, (tok,),
                                                                          pending=pair_b)
    dgated, g_a_out = _mm_bwd("mix_a_out_bwd", dh1_b, W["a_w_out"], gated, deps=toks)
    pair_a_out, tok_a = exchange("a_out", [g_a_out])
    scat_w1_0, tok = reduce(pair_w1_0, dgated)
    early = [relay(state, dgated) for state in (scat_w2_1, scat_w1_1, scat_b)]
    dzpre, d_w_s, d_b_col, d_ln_g, d_ln_b = _gate_bwd("mix_a_gate_bwd", zpre, dgated, a_ln_g, a_ln_b, a_w_s[0], b_col)
    dhn0 = _mm_aw("mix_a_in_dx", dzpre, W["a_w_in"], transpose_w=True, deps=(tok, tok_a, *[t for _, t in early]))
    scat_a_out, tok = reduce(pair_a_out, dhn0)
    g_a_in = _mm_dw("mix_a_in_dw", hn0, dzpre, W["a_w_in"], deps=(tok,))
    pair_a_in, tok = exchange("a_in", [g_a_in])
    dx, _, d_norm_mix0 = _rms_bwd("mix_a_norm_bwd", x2, norm_mix[0:1], dhn0, dh1)
    scat_a_in, _ = reduce(pair_a_in, dx)

    small = dict(a_ln_g=(a_ln_g, m_a_ln_g, v_a_ln_g), a_ln_b=(a_ln_b, m_a_ln_b, v_a_ln_b),
                 a_w_s=(a_w_s, m_a_w_s, v_a_w_s), a_b_s=(a_b_s, m_a_b_s, v_a_b_s),
                 b_scale=(b_scale, m_b_scale, v_b_scale), norm_mix=(norm_mix, m_norm_mix, v_norm_mix),
                 norm_mlp=(norm_mlp, m_norm_mlp, v_norm_mlp), final_norm=(final_norm, m_final_norm, v_final_norm))
    small_names = list(small)
    local = dict(a_ln_g=d_ln_g, a_ln_b=d_ln_b, a_w_s=d_w_s[None], a_b_s=d_b_col.T[None], b_scale=d_scale,
                 norm_mix=jnp.concatenate([d_norm_mix0, d_norm_mix1], axis=0),
                 norm_mlp=jnp.concatenate([d_norm_mlp0, d_norm_mlp1], axis=0), final_norm=d_final.reshape(-1))
    device = (4 * xi + 2 * yi + ci).astype(jnp.int32).reshape(1)
    slots = _own_of_eight("small_own", _pack([local[k] for k in small_names]), device)
    small_send, small_recv, _, (slots,), small_tok = _split_start("small_start", [], [slots], _all_plan)

    moments = dict(a_w_in=(m_a_w_in, v_a_w_in), a_w_out=(m_a_w_out, v_a_w_out), b_w_in=(m_b_w_in, v_b_w_in),
                   b_w_grp=(m_b_w_grp, v_b_w_grp), b_w_out=(m_b_w_out, v_b_w_out),
                   mlp_w1=(m_mlp_w1, v_mlp_w1), mlp_w2=(m_mlp_w2, v_mlp_w2))
    weights = dict(a_w_in=a_w_in, a_w_out=a_w_out, b_w_in=b_w_in, b_w_grp=b_w_grp, b_w_out=b_w_out,
                   mlp_w1=mlp_w1, mlp_w2=mlp_w2)
    landing = [(scat_w2_1, [("mlp_w2", 1)]), (scat_w1_1, [("mlp_w1", 1)]),
               (scat_b, [("b_w_out", 0), ("b_w_grp", 0), ("b_w_in", 0)]),
               (scat_w2_0, [("mlp_w2", 0)]), (scat_w1_0, [("mlp_w1", 0)]),
               (scat_a_out, [("a_w_out", 0)]), (scat_a_in, [("a_w_in", 0)])]
    results, after = {}, dx
    relays = list(early)
    for i, (_, members) in enumerate(landing):
        deps = (small_tok,) if i == 0 else ()
        if len(relays) == i + 1 < len(landing):
            relays.append(relay(landing[i + 1][0], after))
            deps = (relays[-1][1],)
        for (k, layer), parts in zip(members, land(relays[i][0], relays[-1][1] if deps else after)):
            shard_shape = (-1, *parts.shape[2:])
            results[k] = _adam_shard(f"adam_{k}_{layer}", weights[k].reshape(shard_shape),
                                     moments[k][0].reshape(shard_shape), moments[k][1].reshape(shard_shape),
                                     parts, layer=layer, prev=results.get(k), deps=deps)
            after = results[k][1]
    grad_out, delta_out, m_out, v_out = {}, {}, {}, {}
    for k, res in results.items():
        grad_out[k], delta_out[k], m_out[k], v_out[k] = [r.reshape(weights[k].shape) for r in res]

    _, (slots,) = _split_wait("small_wait", [], [slots], small_send, small_recv, after, _all_whole)
    reduced = _sum_of_eight("small_sum", slots)
    small_grads = dict(zip(small_names, _unpack(reduced, [local[k].shape for k in small_names])))
    shard_w = b_scale.shape[1]
    small_grads["b_scale"] = lax.dynamic_slice_in_dim(small_grads["b_scale"], chip * shard_w, shard_w, axis=1)
    small_grads = {k: small_grads[k].reshape(small[k][0].shape) for k in small_names}
    packed = [_pack([small[k][i] for k in small_names]) for i in range(3)]
    res = _adam_packed("adam_small", packed[0], _pack([small_grads[k] for k in small_names]), packed[1], packed[2])
    shapes = [small[k][0].shape for k in small_names]
    for k, d, nm, nv in zip(small_names, *[_unpack(r, shapes) for r in res]):
        grad_out[k], delta_out[k], m_out[k], v_out[k] = small_grads[k], d, nm, nv

    loss = lax.psum(loss_part[0, 0], ("x", "y", "c"))
    order = ["a_w_in", "a_ln_g", "a_ln_b", "a_w_s", "a_b_s", "a_w_out", "b_w_in", "b_w_grp", "b_scale", "b_w_out",
             "norm_mix", "norm_mlp", "mlp_w1", "mlp_w2", "final_norm"]
    return (loss, dx[None], *[grad_out[k] for k in order], *[delta_out[k] for k in order],
            *[m_out[k] for k in order], *[v_out[k] for k in order])
```

```python
import math

import jax
import jax.numpy as jnp
from jax import lax
from jax.experimental import pallas as pl
from jax.experimental.pallas import tpu as pltpu

F32 = jnp.float32
BF16 = jnp.bfloat16
MESH = pl.DeviceIdType.MESH

EPS = 1e-6
B_WINDOWS = (2, 4, 8, 16)
ADAM_LR = 0.001
ADAM_B1 = 0.9
ADAM_B2 = 0.999
ADAM_EPS = 1e-08
ADAM_WD = 0.01
ADAM_STEP = 10

N_CHIPS = 4
N_DEV = 8
LANES = 128
PACK_ELEMS = 8 * LANES
VMEM_LIMIT = 56 * 1024 * 1024
ROW_TILE = 512
MM_TM, MM_TN, MM_TK = 1024, 1024, 2048


_ANY = pl.BlockSpec(memory_space=pl.ANY)


def _tile(dim, pref):
    t = min(dim, pref)
    while dim % t:
        t //= 2
    return t


def _params(*sem):
    return pltpu.CompilerParams(dimension_semantics=sem, vmem_limit_bytes=VMEM_LIMIT)


class _W4:
    def __init__(self, arr, col_sharded, shape=None):
        self.arr = arr
        self.nj, self.nl, self.r, self.c = arr.shape if shape is None else shape
        self.col = col_sharded
        self.rows = self.r if col_sharded else self.nj * self.r
        self.cols = self.nj * self.c if col_sharded else self.c

    def tile_rows(self, pref):
        return _tile(self.r, pref)

    def tile_cols(self, pref):
        return _tile(self.c, pref)

    def index(self, layer, rb, cb, tr, tc):
        if self.col:
            n = self.c // tc
            return (cb // n, layer, rb, cb % n)
        n = self.r // tr
        return (rb // n, layer, rb % n, cb)


def _mm_aw(name, a, w, *, layer=0, groups=1, transpose_w=False, extras=(), out_dtypes=(F32,), epilogue=None,
           deps=(), k_piece=None):
    s, ka_total = a.shape
    kdim, ndim = (w.cols, w.rows) if transpose_w else (w.rows, w.cols)
    assert ka_total == groups * kdim, (name, a.shape, kdim, groups)
    span = not w.col and ((not transpose_w and kdim <= MM_TK) or (transpose_w and groups > 1))
    if span and transpose_w:
        tm, tn, tk = _tile(s, 2048), ndim, w.tile_cols(MM_TK)
    else:
        tk = kdim if span else (w.tile_cols(MM_TK) if transpose_w else w.tile_rows(MM_TK))
        tm, tn_pref = (_tile(s, 2048), 512) if tk == kdim else (_tile(s, MM_TM), MM_TN)
        tn = w.tile_rows(tn_pref) if transpose_w else w.tile_cols(tn_pref)
    nk, nn = kdim // tk, ndim // tn
    if k_piece is not None:
        assert not (w.col or transpose_w or span or groups > 1), name
        piece, n_pieces = k_piece
        tm, tn, tk = _tile(s, MM_TM), w.tile_cols(MM_TN), w.r // (2 * n_pieces)
        nk, nn = 2 * w.nj, ndim // tn

        def k_block(k):
            return k // 2, (k % 2) * n_pieces + piece

    def lay(g):
        return g if groups > 1 else layer

    a_spec = pl.BlockSpec((tm, tk), lambda g, i, n, k: (i, g * nk + k))
    if k_piece is not None:
        per = w.r // tk
        a_spec = pl.BlockSpec((tm, tk), lambda g, i, n, k: (i, k_block(k)[0] * per + k_block(k)[1]))
        w_spec = pl.BlockSpec((None, None, tk, tn), lambda g, i, n, k: (k_block(k)[0], layer, k_block(k)[1], n))
    elif span and transpose_w:
        w_spec = pl.BlockSpec((w.nj, None, w.r, tk), lambda g, i, n, k: (0, lay(g), 0, k))
    elif span:
        w_spec = pl.BlockSpec((w.nj, None, w.r, tn), lambda g, i, n, k: (0, lay(g), 0, n))
    elif transpose_w:
        w_spec = pl.BlockSpec((None, None, tn, tk), lambda g, i, n, k: w.index(lay(g), n, k, tn, tk))
    else:
        w_spec = pl.BlockSpec((None, None, tk, tn), lambda g, i, n, k: w.index(lay(g), k, n, tk, tn))
    ex_specs = []
    for e in extras:
        assert e.shape[1] == groups * ndim and e.shape[0] in (1, s), (name, e.shape)
        if e.shape[0] == 1:
            ex_specs.append(pl.BlockSpec((1, tn), lambda g, i, n, k: (0, g * nn + n)))
        else:
            ex_specs.append(pl.BlockSpec((tm, tn), lambda g, i, n, k: (i, g * nn + n)))
    out_spec = pl.BlockSpec((tm, tn), lambda g, i, n, k: (i, g * nn + n))
    n_ex, n_out, n_dep = len(extras), len(out_dtypes), len(deps)

    def body(a_ref, w_ref, *rest):
        ex, outs = rest[:n_ex], rest[n_ex + n_dep:n_ex + n_dep + n_out]
        av = a_ref[...]
        if av.dtype != BF16:
            av = av.astype(BF16)
        wv = w_ref[...]
        if span:
            wv = wv.reshape((tn, tk) if transpose_w else (tk, tn))
        if transpose_w:
            prod = lax.dot_general(av, wv, (((1,), (1,)), ((), ())), preferred_element_type=F32)
        else:
            prod = jnp.dot(av, wv, preferred_element_type=F32)

        def finish(total):
            vals = (total,) if epilogue is None else epilogue(total, *[e[...] for e in ex])
            for o, v in zip(outs, vals):
                o[...] = v.astype(o.dtype)

        if nk == 1:
            finish(prod)
            return
        acc, k = rest[-1], pl.program_id(3)

        @pl.when(k == 0)
        def _():
            acc[...] = prod

        @pl.when(k > 0)
        def _():
            acc[...] += prod

        @pl.when(k == nk - 1)
        def _():
            finish(acc[...])

    outs = pl.pallas_call(
        body, name=name, grid=(groups, s // tm, nn, nk),
        in_specs=[a_spec, w_spec, *ex_specs] + [_ANY] * n_dep, out_specs=[out_spec] * n_out,
        out_shape=[jax.ShapeDtypeStruct((s, groups * ndim), dt) for dt in out_dtypes],
        scratch_shapes=[pltpu.VMEM((tm, tn), F32)] if nk > 1 else [],
        compiler_params=_params("parallel", "parallel", "parallel", "arbitrary"),
    )(a, w.arr, *extras, *deps)
    return outs[0] if n_out == 1 else outs


def _mm_dw(name, a, b, like, *, layer=0, groups=1, deps=()):
    s, ka_total = a.shape
    rows, cols = ka_total // groups, b.shape[1] // groups
    assert (rows, cols) == (like.rows, like.cols) and b.shape[0] == s, (name, a.shape, b.shape)
    span = groups > 1 and not like.col
    tm, tn, tk = rows if span else like.tile_rows(MM_TM), like.tile_cols(MM_TN), _tile(s, MM_TK)
    nr, nc, nk = rows // tm, cols // tn, s // tk
    assert nk == 1 or not span

    def lay(g):
        return g if groups > 1 else layer

    in_specs = [pl.BlockSpec((tk, tm), lambda g, n, i, k: (k, g * nr + i)),
                pl.BlockSpec((tk, tn), lambda g, n, i, k: (k, g * nc + n))]
    in_specs += [_ANY] * len(deps)

    def body(a_ref, b_ref, *rest):
        av, bv = a_ref[...], b_ref[...]
        if av.dtype != BF16:
            av = av.astype(BF16)
        if bv.dtype != BF16:
            bv = bv.astype(BF16)
        prod = lax.dot_general(av, bv, (((0,), (0,)), ((), ())), preferred_element_type=F32)
        if nk == 1:
            rest[-1][...] = prod.astype(BF16).reshape(rest[-1].shape)
            return
        o_ref, acc, k = rest[-2], rest[-1], pl.program_id(3)

        @pl.when(k == 0)
        def _():
            acc[...] = prod

        @pl.when(k > 0)
        def _():
            acc[...] += prod

        @pl.when(k == nk - 1)
        def _():
            o_ref[...] = acc[...].astype(BF16)

    if span:
        out_spec = pl.BlockSpec((like.nj, None, like.r, tn), lambda g, n, i, k: (0, g, 0, n))
    else:
        out_spec = pl.BlockSpec((None, None, tm, tn), lambda g, n, i, k: like.index(lay(g), i, n, tm, tn))
    return pl.pallas_call(
        body, name=name, grid=(groups, nc, nr, nk), in_specs=in_specs, out_specs=out_spec,
        out_shape=jax.ShapeDtypeStruct((like.nj, like.nl, like.r, like.c), BF16),
        scratch_shapes=[pltpu.VMEM((tm, tn), F32)] if nk > 1 else [],
        compiler_params=_params("parallel", "parallel", "parallel", "arbitrary"),
    )(a, b, *deps)


def _mm_bwd(name, dy, w, a, *, deps=()):
    s, c = dy.shape
    assert not w.col and w.nl == 1 and w.c == c and a.shape == (s, w.rows), (name, dy.shape, a.shape)
    r = w.r

    def body(dy_ref, w_ref, a_ref, *rest):
        dx_ref, dw_ref = rest[-2:]
        dyv = dy_ref[...]
        dx_ref[...] = lax.dot_general(dyv, w_ref[...], (((1,), (1,)), ((), ())), preferred_element_type=F32)
        dw_ref[...] = lax.dot_general(a_ref[...], dyv, (((0,), (0,)), ((), ())),
                                      preferred_element_type=F32).astype(BF16)

    return pl.pallas_call(
        body, name=name, grid=(w.nj,),
        in_specs=[pl.BlockSpec((s, c), lambda j: (0, 0)), pl.BlockSpec((None, None, r, c), lambda j: (j, 0, 0, 0)),
                  pl.BlockSpec((s, r), lambda j: (0, j))] + [_ANY] * len(deps),
        out_specs=[pl.BlockSpec((s, r), lambda j: (0, j)), pl.BlockSpec((None, None, r, c), lambda j: (j, 0, 0, 0))],
        out_shape=[jax.ShapeDtypeStruct((s, w.rows), F32), jax.ShapeDtypeStruct((w.nj, 1, r, c), BF16)],
        compiler_params=_params("parallel"),
    )(dy, w.arr, a, *deps)


def _row_spec(tr, d):
    return pl.BlockSpec((tr, d), lambda i: (i, 0))


def _vec_spec(d):
    return pl.BlockSpec((1, d), lambda i: (0, 0))


def _rms_fwd(name, x, g):
    s, d = x.shape
    tr = _tile(s, ROW_TILE)

    def body(x_ref, g_ref, o_ref):
        xv = x_ref[...]
        r = lax.rsqrt(jnp.mean(xv * xv, axis=-1, keepdims=True) + EPS)
        o_ref[...] = (xv * r * g_ref[...]).astype(BF16)

    return pl.pallas_call(
        body, name=name, grid=(s // tr,), in_specs=[_row_spec(tr, d), _vec_spec(d)],
        out_specs=_row_spec(tr, d), out_shape=jax.ShapeDtypeStruct((s, d), BF16),
        compiler_params=_params("parallel"),
    )(x, g)


def _rms_bwd(name, x, g, dhn, dres):
    s, d = x.shape
    tr = _tile(s, ROW_TILE)

    def body(x_ref, g_ref, dhn_ref, dres_ref, dx_ref, dxb_ref, dg_ref):
        @pl.when(pl.program_id(0) == 0)
        def _():
            dg_ref[...] = jnp.zeros_like(dg_ref)

        xv = x_ref[...]
        r = lax.rsqrt(jnp.mean(xv * xv, axis=-1, keepdims=True) + EPS)
        xh = xv * r
        dy = dhn_ref[...]
        dg_ref[...] += jnp.sum(dy * xh, axis=0, keepdims=True)
        dxh = dy * g_ref[...]
        dx = dres_ref[...] + r * (dxh - xh * jnp.mean(dxh * xh, axis=-1, keepdims=True))
        dx_ref[...] = dx
        dxb_ref[...] = dx.astype(BF16)

    return pl.pallas_call(
        body, name=name, grid=(s // tr,),
        in_specs=[_row_spec(tr, d), _vec_spec(d), _row_spec(tr, d), _row_spec(tr, d)],
        out_specs=[_row_spec(tr, d), _row_spec(tr, d), _vec_spec(d)],
        out_shape=[jax.ShapeDtypeStruct((s, d), F32), jax.ShapeDtypeStruct((s, d), BF16),
                   jax.ShapeDtypeStruct((1, d), F32)],
        compiler_params=_params("arbitrary"),
    )(x, g, dhn, dres)


def _final(name, h, g, tgt):
    s, d = h.shape
    tr = _tile(s, ROW_TILE)

    def body(h_ref, g_ref, t_ref, dh_ref, dhb_ref, dg_ref, loss_ref):
        @pl.when(pl.program_id(0) == 0)
        def _():
            dg_ref[...] = jnp.zeros_like(dg_ref)
            loss_ref[...] = jnp.zeros_like(loss_ref)

        hv = h_ref[...]
        r = lax.rsqrt(jnp.mean(hv * hv, axis=-1, keepdims=True) + EPS)
        xh = hv * r
        gv = g_ref[...]
        err = xh * gv - t_ref[...]
        part = 0.5 * jnp.sum(jnp.mean(err * err, axis=-1, keepdims=True), axis=0, keepdims=True)
        loss_ref[...] += jnp.broadcast_to(part, loss_ref.shape)
        dy = err * (1.0 / d)
        dg_ref[...] += jnp.sum(dy * xh, axis=0, keepdims=True)
        dxh = dy * gv
        dh = r * (dxh - xh * jnp.mean(dxh * xh, axis=-1, keepdims=True))
        dh_ref[...] = dh
        dhb_ref[...] = dh.astype(BF16)

    return pl.pallas_call(
        body, name=name, grid=(s // tr,),
        in_specs=[_row_spec(tr, d), _vec_spec(d), _row_spec(tr, d)],
        out_specs=[_row_spec(tr, d), _row_spec(tr, d), _vec_spec(d), _vec_spec(LANES)],
        out_shape=[jax.ShapeDtypeStruct((s, d), F32), jax.ShapeDtypeStruct((s, d), BF16),
                   jax.ShapeDtypeStruct((1, d), F32), jax.ShapeDtypeStruct((1, LANES), F32)],
        compiler_params=_params("arbitrary"),
    )(h, g, tgt)


_SQRT_HALF = 1.0 / math.sqrt(2.0)
_INV_SQRT_2PI = 1.0 / math.sqrt(2.0 * math.pi)


def _gelu(x):
    return x * (lax.erf(x * _SQRT_HALF) + 1.0) * 0.5


def _gelu_grad(x):
    return 0.5 * (lax.erf(x * _SQRT_HALF) + 1.0) + x * jnp.exp(-0.5 * x * x) * _INV_SQRT_2PI


def _causal(chunk):
    t = lax.broadcasted_iota(jnp.int32, (chunk, chunk), 0)
    sidx = lax.broadcasted_iota(jnp.int32, (chunk, chunk), 1)
    return sidx <= t


def _layernorm_parts(v, g, b):
    mu = jnp.mean(v, axis=-1, keepdims=True)
    vc = v - mu
    rs = lax.rsqrt(jnp.mean(vc * vc, axis=-1, keepdims=True) + EPS)
    vhat = vc * rs
    return vhat, rs, vhat * g + b


def _gate_fwd(name, zpre, ln_g, ln_b, w_s, b_col):
    s, aw2 = zpre.shape
    aw = aw2 // 2
    ng, chunk, _ = w_s.shape
    dh = aw // ng

    def body(z_ref, g_ref, b_ref, ws_ref, bc_ref, o_ref):
        u = _gelu(z_ref[:, :aw])
        v = _gelu(z_ref[:, aw:])
        _, _, vln = _layernorm_parts(v, g_ref[...], b_ref[...])
        mask = _causal(chunk)
        for gi in range(ng):
            sl = slice(gi * dh, (gi + 1) * dh)
            wm = jnp.where(mask, ws_ref[gi], 0.0).astype(BF16)
            sg = jnp.dot(wm, vln[:, sl].astype(BF16), preferred_element_type=F32) + bc_ref[:, gi:gi + 1]
            o_ref[:, sl] = (u[:, sl] * sg).astype(BF16)

    return pl.pallas_call(
        body, name=name, grid=(s // chunk,),
        in_specs=[_row_spec(chunk, aw2), _vec_spec(aw), _vec_spec(aw),
                  pl.BlockSpec((ng, chunk, chunk), lambda i: (0, 0, 0)),
                  pl.BlockSpec((chunk, ng), lambda i: (0, 0))],
        out_specs=_row_spec(chunk, aw), out_shape=jax.ShapeDtypeStruct((s, aw), BF16),
        compiler_params=_params("parallel"),
    )(zpre, ln_g, ln_b, w_s, b_col)


def _gate_bwd(name, zpre, dgated, ln_g, ln_b, w_s, b_col):
    s, aw2 = zpre.shape
    aw = aw2 // 2
    ng, chunk, _ = w_s.shape
    dh = aw // ng

    def body(z_ref, dgt_ref, g_ref, b_ref, ws_ref, bc_ref, dz_ref, dws_ref, dbc_ref, dlg_ref, dlb_ref,
             du_scr, dvln_scr):
        @pl.when(pl.program_id(0) == 0)
        def _():
            dws_ref[...] = jnp.zeros_like(dws_ref)
            dbc_ref[...] = jnp.zeros_like(dbc_ref)
            dlg_ref[...] = jnp.zeros_like(dlg_ref)
            dlb_ref[...] = jnp.zeros_like(dlb_ref)

        zu = z_ref[:, :aw]
        zv = z_ref[:, aw:]
        u = _gelu(zu)
        lg = g_ref[...]
        vhat, rs, vln = _layernorm_parts(_gelu(zv), lg, b_ref[...])
        mask = _causal(chunk)
        for gi in range(ng):
            sl = slice(gi * dh, (gi + 1) * dh)
            wm = jnp.where(mask, ws_ref[gi], 0.0).astype(BF16)
            vg = vln[:, sl].astype(BF16)
            sg = jnp.dot(wm, vg, preferred_element_type=F32) + bc_ref[:, gi:gi + 1]
            dgt = dgt_ref[:, sl]
            du_scr[:, sl] = dgt * sg
            ds = dgt * u[:, sl]
            dbc_ref[:, gi:gi + 1] += jnp.sum(ds, axis=-1, keepdims=True)
            dsb = ds.astype(BF16)
            dwm = lax.dot_general(dsb, vg, (((1,), (1,)), ((), ())), preferred_element_type=F32)
            dws_ref[gi] += jnp.where(mask, dwm, 0.0)
            dvln_scr[:, sl] = lax.dot_general(wm, dsb, (((0,), (0,)), ((), ())), preferred_element_type=F32)
        dvln = dvln_scr[...]
        dlb_ref[...] += jnp.sum(dvln, axis=0, keepdims=True)
        dlg_ref[...] += jnp.sum(dvln * vhat, axis=0, keepdims=True)
        dvh = dvln * lg
        dv = rs * (dvh - jnp.mean(dvh, axis=-1, keepdims=True)
                   - vhat * jnp.mean(dvh * vhat, axis=-1, keepdims=True))
        dz_ref[:, :aw] = (du_scr[...] * _gelu_grad(zu)).astype(BF16)
        dz_ref[:, aw:] = (dv * _gelu_grad(zv)).astype(BF16)

    return pl.pallas_call(
        body, name=name, grid=(s // chunk,),
        in_specs=[_row_spec(chunk, aw2), _row_spec(chunk, aw), _vec_spec(aw), _vec_spec(aw),
                  pl.BlockSpec((ng, chunk, chunk), lambda i: (0, 0, 0)),
                  pl.BlockSpec((chunk, ng), lambda i: (0, 0))],
        out_specs=[_row_spec(chunk, aw2), pl.BlockSpec((ng, chunk, chunk), lambda i: (0, 0, 0)),
                   pl.BlockSpec((chunk, ng), lambda i: (0, 0)), _vec_spec(aw), _vec_spec(aw)],
        out_shape=[jax.ShapeDtypeStruct((s, aw2), BF16), jax.ShapeDtypeStruct((ng, chunk, chunk), F32),
                   jax.ShapeDtypeStruct((chunk, ng), F32), jax.ShapeDtypeStruct((1, aw), F32),
                   jax.ShapeDtypeStruct((1, aw), F32)],
        scratch_shapes=[pltpu.VMEM((chunk, aw), F32), pltpu.VMEM((chunk, aw), F32)],
        compiler_params=_params("arbitrary"),
    )(zpre, dgated, ln_g, ln_b, w_s, b_col)


def _pool_select(g, parts):
    out = parts[-1]
    for gi in range(len(parts) - 2, -1, -1):
        out = jnp.where(g == gi, parts[gi], out)
    return out


def _pool_specs(s, bw):
    head = bw // len(B_WINDOWS)
    tc = _tile(head, 256)
    nb = head // tc
    return tc, (len(B_WINDOWS), nb), pl.BlockSpec((s, tc), lambda g, j: (0, g * nb + j))


def _pool_window(g, t):
    w = _pool_select(g, [jnp.full(t.shape, wi, jnp.int32) for wi in B_WINDOWS])
    return jnp.minimum(t + 1, w).astype(F32)


def _pool_fwd(name, vb):
    assert B_WINDOWS == (2, 4, 8, 16)
    s, bw = vb.shape
    tc, grid, spec = _pool_specs(s, bw)

    def body(v_ref, o_ref):
        g = pl.program_id(0)
        v = v_ref[...]
        t = lax.broadcasted_iota(jnp.int32, (s, tc), 0)

        def down(x, k):
            return jnp.where(t >= k, pltpu.roll(x, k, 0), 0.0)

        sums, cur, k = [], v, 1
        for _ in B_WINDOWS:
            cur = cur + down(cur, k)
            sums.append(cur)
            k *= 2
        o_ref[...] = (_pool_select(g, sums) / _pool_window(g, t) - v).astype(BF16)

    return pl.pallas_call(
        body, name=name, grid=grid, in_specs=[spec], out_specs=spec,
        out_shape=jax.ShapeDtypeStruct((s, bw), BF16), compiler_params=_params("parallel", "parallel"),
    )(vb)


def _pool_bwd(name, dpooled):
    s, bw = dpooled.shape
    tc, grid, spec = _pool_specs(s, bw)

    def body(d_ref, o_ref):
        g = pl.program_id(0)
        dp = d_ref[...]
        t = lax.broadcasted_iota(jnp.int32, (s, tc), 0)

        def up(x, k):
            return jnp.where(t < s - k, pltpu.roll(x, s - k, 0), 0.0)

        sums, cur, k = [], dp / _pool_window(g, t), 1
        for _ in B_WINDOWS:
            cur = cur + up(cur, k)
            sums.append(cur)
            k *= 2
        o_ref[...] = (_pool_select(g, sums) - dp).astype(BF16)

    return pl.pallas_call(
        body, name=name, grid=grid, in_specs=[spec], out_specs=spec,
        out_shape=jax.ShapeDtypeStruct((s, bw), BF16), compiler_params=_params("parallel", "parallel"),
    )(dpooled)


def _scale_bwd(name, dms, mixed, scale):
    s, bw = dms.shape
    tr = _tile(s, ROW_TILE)

    def body(d_ref, m_ref, sc_ref, o_ref, ds_ref):
        @pl.when(pl.program_id(0) == 0)
        def _():
            ds_ref[...] = jnp.zeros_like(ds_ref)

        dv = d_ref[...]
        ds_ref[...] += jnp.sum(dv * m_ref[...], axis=0, keepdims=True)
        o_ref[...] = (dv * sc_ref[...]).astype(BF16)

    return pl.pallas_call(
        body, name=name, grid=(s // tr,), in_specs=[_row_spec(tr, bw), _row_spec(tr, bw), _vec_spec(bw)],
        out_specs=[_row_spec(tr, bw), _vec_spec(bw)],
        out_shape=[jax.ShapeDtypeStruct((s, bw), BF16), jax.ShapeDtypeStruct((1, bw), F32)],
        compiler_params=_params("arbitrary"),
    )(dms, mixed, scale)


def _adam_update(w, g, m, v):
    m = ADAM_B1 * m + (1.0 - ADAM_B1) * g
    v = ADAM_B2 * v + (1.0 - ADAM_B2) * (g * g)
    m_hat = m / (1.0 - ADAM_B1 ** ADAM_STEP)
    v_hat = v / (1.0 - ADAM_B2 ** ADAM_STEP)
    delta = -ADAM_LR * (m_hat / (jnp.sqrt(v_hat) + ADAM_EPS) + ADAM_WD * w)
    return delta, m, v


def _adam_shard(name, w, m, v, parts, layer=0, prev=None, deps=()):
    nl, r, c = w.shape
    nj, nlp = parts.shape[:2]
    tr = _tile(r, ROW_TILE // 2)
    spec = pl.BlockSpec((None, tr, c), lambda l, i: (layer + l, i, 0))

    def body(w_ref, m_ref, v_ref, p_ref, *rest):
        g_ref, d_ref, nm_ref, nv_ref = rest[-4:]
        g = p_ref[0].astype(F32)
        for j in range(1, nj):
            g = g + p_ref[j].astype(F32)
        delta, nm, nv = _adam_update(w_ref[...], g, m_ref[...], v_ref[...])
        g_ref[...] = g
        d_ref[...] = delta
        nm_ref[...] = nm
        nv_ref[...] = nv

    prev = () if prev is None else tuple(prev)
    return pl.pallas_call(
        body, name=name, grid=(nlp, r // tr),
        in_specs=[spec, spec, spec, pl.BlockSpec((nj, None, tr, c), lambda l, i: (0, l, i, 0))]
        + [_ANY] * (len(prev) + len(deps)),
        out_specs=[spec] * 4, out_shape=[jax.ShapeDtypeStruct(w.shape, F32)] * 4,
        input_output_aliases={4 + i: i for i in range(len(prev))},
        compiler_params=_params("parallel", "parallel"),
    )(w, m, v, parts, *prev, *deps)


def _adam_packed(name, w, g, m, v):
    rows, lanes = w.shape
    tr = rows
    spec = pl.BlockSpec((tr, lanes), lambda i: (i, 0))

    def body(w_ref, g_ref, m_ref, v_ref, d_ref, nm_ref, nv_ref):
        delta, nm, nv = _adam_update(w_ref[...], g_ref[...], m_ref[...], v_ref[...])
        d_ref[...] = delta
        nm_ref[...] = nm
        nv_ref[...] = nv

    return pl.pallas_call(
        body, name=name, grid=(rows // tr,), in_specs=[spec] * 4, out_specs=[spec] * 3,
        out_shape=[jax.ShapeDtypeStruct(w.shape, F32)] * 3, compiler_params=_params("parallel"),
    )(w, g, m, v)


def _place():
    x, y, c = lax.axis_index("x"), lax.axis_index("y"), lax.axis_index("c")
    chips = [(1 - x, y), (x, 1 - y), (1 - x, 1 - y)]
    return x, y, c, chips


def _remote(src, dst, send_sem, recv_sem, device):
    return pltpu.make_async_remote_copy(src_ref=src, dst_ref=dst, send_sem=send_sem, recv_sem=recv_sem,
                                        device_id=device, device_id_type=MESH)


def _half(ref_rows, cc):
    h = ref_rows // 2
    return pl.ds(cc * h, h)


_HBM = pl.BlockSpec(memory_space=pltpu.HBM)
_SEM = pl.BlockSpec(memory_space=pltpu.SEMAPHORE)
_EFFECT = pltpu.SideEffectType.DATAFLOW_SIDE_EFFECTING


def _in_hbm(arr):
    return pltpu.with_memory_space_constraint(arr, pltpu.HBM)


def _gather_rows(land, cc, part=(0, 1)):
    k, n = part
    h = land.shape[2] // 2
    return pl.ds(cc * h + k * (h // n), h // n)


def _gather_block(land, split, j, cc, part=(0, 1)):
    if split:
        return land.at[j, :, _gather_rows(land, cc, part), :]
    return land.at[j]


def _split_start(name, srcs, lands, plan, deps=(), groups=None):
    ns = len(srcs)
    nl = len(lands) if groups is None else groups
    both = list(srcs) + list(lands)
    nb = len(both)

    def body(*refs):
        s, ld = refs[:ns], refs[ns:nb]
        outs = refs[nb + len(deps):]
        send, recv, token = outs[:nl], outs[nl:2 * nl], outs[-1]
        for a, copies in enumerate(plan(s, ld)):
            for src, dst, peer in copies:
                _remote(src, dst, send[a], recv[a], peer).start()
        token[...] = jnp.zeros_like(token)

    outs = pl.pallas_call(
        body, name=name, in_specs=[_HBM] * nb + [_ANY] * len(deps),
        out_specs=[_SEM] * (2 * nl) + [_HBM] * nb + [pl.BlockSpec(memory_space=pltpu.VMEM)],
        out_shape=[pltpu.SemaphoreType.DMA(())] * (2 * nl) + [pltpu.HBM(b.shape, b.dtype) for b in both]
        + [jax.ShapeDtypeStruct((8, LANES), F32)],
        input_output_aliases={i: 2 * nl + i for i in range(nb)},
        compiler_params=pltpu.CompilerParams(has_side_effects=_EFFECT),
    )(*[_in_hbm(b) for b in both], *deps)
    thru = outs[2 * nl:2 * nl + nb]
    return list(outs[:nl]), list(outs[nl:2 * nl]), list(thru[:ns]), list(thru[ns:]), outs[-1]


def _split_wait(name, srcs, lands, send, recv, after, whole):
    ns, nl = len(srcs), len(send)
    both = list(srcs) + list(lands)
    nb = len(both)

    def body(*refs):
        ld, snd, rcv = refs[ns:nb], refs[nb:nb + nl], refs[nb + nl:nb + 2 * nl]
        x, y, c, _ = _place()
        for a, blk in enumerate(whole(ld)):
            every = _remote(blk, blk, snd[a], rcv[a], (x, y, c))
            every.wait_send()
            every.wait_recv()

    outs = pl.pallas_call(
        body, name=name, in_specs=[_HBM] * nb + [_SEM] * (2 * nl) + [_ANY], out_specs=[_HBM] * nb,
        out_shape=[pltpu.HBM(b.shape, b.dtype) for b in both],
        input_output_aliases={i: i for i in range(nb)},
        compiler_params=pltpu.CompilerParams(has_side_effects=_EFFECT),
    )(*both, *send, *recv, after)
    return list(outs[:ns]), list(outs[ns:])


def _gather_plan(split, parts):
    def plan(srcs, lands):
        x, y, c, chips = _place()
        out = []
        for ld, sp, n in zip(lands, split, parts):
            for k in range(n):
                blk = _gather_block(ld, sp, 2 * x + y, c, (k, n))
                out.append([(blk, blk, (qx, qy, c)) for qx, qy in chips])
        return out
    return plan


def _gather_whole(split, part=(0, 1)):
    def whole(lands):
        _, _, c, _ = _place()
        return [ld.at[pl.ds(0, 3), :, _gather_rows(ld, c, part), :] if sp else ld.at[pl.ds(0, 3)]
                for ld, sp in zip(lands, split)]
    return whole


def _relay_plan(srcs, lands):
    x, y, c, _ = _place()
    out = []
    for ld in lands:
        blocks = [ld.at[j, :, _half(ld.shape[2], c), :] for j in range(N_CHIPS)]
        out.append([(blk, blk, (x, y, 1 - c)) for blk in blocks])
    return out


def _relay_whole(lands):
    _, _, c, _ = _place()
    return [ld.at[:, :, _half(ld.shape[2], c), :] for ld in lands]


def _fill_own(name, shard, dtype, place, layer=None):
    nl, r, c = shard.shape
    first = 0
    if layer is not None:
        nl, first = 1, layer
    tr = _tile(r, 512)

    def body(p_ref, s_ref, o_ref):
        o_ref[...] = s_ref[...].astype(o_ref.dtype)

    return pl.pallas_call(
        body, name=name,
        grid_spec=pltpu.PrefetchScalarGridSpec(
            num_scalar_prefetch=1, grid=(nl, r // tr),
            in_specs=[pl.BlockSpec((None, tr, c), lambda l, i, p: (first + l, i, 0))],
            out_specs=pl.BlockSpec((None, None, tr, c), lambda l, i, p: (p[0], l, i, 0))),
        out_shape=jax.ShapeDtypeStruct((N_CHIPS, nl, r, c), dtype),
        compiler_params=_params("parallel", "parallel"),
    )(place, shard)


def _gather_finish(name, lands, part=(0, 1)):
    n = len(lands)

    def body(*refs):
        outs = refs[n:2 * n]
        fsend, frecv = refs[2 * n:]
        x, y, c, chips = _place()
        sib = (x, y, 1 - c)

        def relay(a, qi, cc):
            qx, qy = chips[qi]
            blk = _gather_block(outs[a], True, 2 * qx + qy, cc, part)
            return _remote(blk, blk, fsend.at[a, qi], frecv.at[a, qi], sib)

        relays = [relay(a, qi, c) for a in range(n) for qi in range(3)]
        for cp in relays:
            cp.start()
        for a in range(n):
            for qi in range(3):
                relay(a, qi, 1 - c).wait_recv()
        for cp in relays:
            cp.wait_send()

    outs = pl.pallas_call(
        body, name=name, in_specs=[_ANY] * n, out_specs=[_ANY] * n,
        out_shape=[jax.ShapeDtypeStruct(ld.shape, ld.dtype) for ld in lands],
        input_output_aliases={i: i for i in range(n)},
        scratch_shapes=[pltpu.SemaphoreType.DMA((n, 3))] * 2,
    )(*lands)
    return list(outs)


def _pair_plan(srcs, lands):
    x, y, c, _ = _place()
    return [[(s.at[:, :, _half(s.shape[2], 1 - c), :], ld, (x, y, 1 - c))] for s, ld in zip(srcs, lands)]


def _pair_whole(lands):
    return list(lands)


def _scatter_plan(srcs, lands):
    x, y, c, chips = _place()
    return [[(s.at[2 * qx + qy], ld.at[2 * x + y, :, _half(ld.shape[2], c), :], (qx, qy, c)) for qx, qy in chips]
            for s, ld in zip(srcs, lands)]


def _scatter_whole(lands):
    _, _, c, _ = _place()
    return [ld.at[pl.ds(0, 3), :, _half(ld.shape[2], c), :] for ld in lands]


def _pair_sum(name, grad, other, place):
    nj, nl, r, c = grad.shape
    h = r // 2
    tr = _tile(h, 2 * ROW_TILE)
    nb = h // tr

    def body(p_ref, g_ref, o_ref, q_ref, d_ref):
        q = (g_ref[...].astype(F32) + o_ref[...].astype(F32)).astype(BF16)
        q_ref[...] = q

        @pl.when(pl.program_id(2) == p_ref[0])
        def _():
            d_ref[...] = q

    blk = (None, None, tr, c)
    return pl.pallas_call(
        body, name=name,
        grid_spec=pltpu.PrefetchScalarGridSpec(
            num_scalar_prefetch=1, grid=(nl, nb, nj),
            in_specs=[pl.BlockSpec(blk, lambda l, i, j, p: (j, l, p[1] * nb + i, 0)),
                      pl.BlockSpec(blk, lambda l, i, j, p: (j, l, i, 0))],
            out_specs=[pl.BlockSpec(blk, lambda l, i, j, p: (j, l, i, 0)),
                       pl.BlockSpec(blk, lambda l, i, j, p: (p[0], l, p[1] * nb + i, 0))]),
        out_shape=[jax.ShapeDtypeStruct((nj, nl, h, c), BF16), jax.ShapeDtypeStruct((nj, nl, r, c), BF16)],
        compiler_params=_params("parallel", "parallel", "arbitrary"),
    )(place, grad, other)


def _own_of_eight(name, packed, device):
    rows, lanes = packed.shape
    tr = rows

    def body(d_ref, s_ref, o_ref):
        o_ref[...] = s_ref[...]

    return pl.pallas_call(
        body, name=name,
        grid_spec=pltpu.PrefetchScalarGridSpec(
            num_scalar_prefetch=1, grid=(rows // tr,),
            in_specs=[pl.BlockSpec((tr, lanes), lambda i, d: (i, 0))],
            out_specs=pl.BlockSpec((None, tr, lanes), lambda i, d: (d[0], i, 0))),
        out_shape=jax.ShapeDtypeStruct((N_DEV, rows, lanes), packed.dtype),
        compiler_params=_params("parallel"),
    )(device, packed)


def _all_plan(srcs, lands):
    x, y, c, _ = _place()
    (ld,) = lands
    blk = ld.at[4 * x + 2 * y + c]
    flips = [(a, b, d) for a in (0, 1) for b in (0, 1) for d in (0, 1) if a + b + d]
    return [[(blk, blk, (x + a - 2 * a * x, y + b - 2 * b * y, c + d - 2 * d * c)) for a, b, d in flips]]


def _all_whole(lands):
    return [lands[0].at[pl.ds(0, N_DEV - 1)]]


def _sum_of_eight(name, slots):
    n, rows, lanes = slots.shape
    tr = rows

    def body(s_ref, o_ref):
        total = s_ref[0]
        for d in range(1, n):
            total = total + s_ref[d]
        o_ref[...] = total

    return pl.pallas_call(
        body, name=name, grid=(rows // tr,),
        in_specs=[pl.BlockSpec((n, tr, lanes), lambda i: (0, i, 0))],
        out_specs=pl.BlockSpec((tr, lanes), lambda i: (i, 0)),
        out_shape=jax.ShapeDtypeStruct((rows, lanes), F32), compiler_params=_params("parallel"),
    )(slots)


def _pack(parts):
    rows = []
    for p in parts:
        flat = p.reshape(-1)
        pad = (-flat.shape[0]) % PACK_ELEMS
        rows.append(jnp.pad(flat, (0, pad)).reshape(-1, LANES))
    return jnp.concatenate(rows, axis=0)


def _unpack(packed, shapes):
    out, row = [], 0
    for sh in shapes:
        size = math.prod(sh)
        nrows = -(-size // PACK_ELEMS) * (PACK_ELEMS // LANES)
        out.append(packed[row:row + nrows].reshape(-1)[:size].reshape(sh))
        row += nrows
    return out


def kernel(x, a_w_in, a_ln_g, a_ln_b, a_w_s, a_b_s, a_w_out, b_w_in, b_w_grp, b_scale, b_w_out, norm_mix, norm_mlp, mlp_w1, mlp_w2, final_norm, loss_target, m_a_w_in, m_a_ln_g, m_a_ln_b, m_a_w_s, m_a_b_s, m_a_w_out, m_b_w_in, m_b_w_grp, m_b_scale, m_b_w_out, m_norm_mix, m_norm_mlp, m_mlp_w1, m_mlp_w2, m_final_norm, v_a_w_in, v_a_ln_g, v_a_ln_b, v_a_w_s, v_a_b_s, v_a_w_out, v_b_w_in, v_b_w_grp, v_b_scale, v_b_w_out, v_norm_mix, v_norm_mlp, v_mlp_w1, v_mlp_w2, v_final_norm):
    xi, yi, ci = lax.axis_index("x"), lax.axis_index("y"), lax.axis_index("c")
    chip = 2 * xi + yi
    place = jnp.stack([chip, ci]).astype(jnp.int32)
    x2, tgt = x[0], loss_target[0]
    bw = b_scale.shape[1] * N_CHIPS

    units = dict(a_w_in=(a_w_in, None), a_w_out=(a_w_out, None), w1_0=(mlp_w1, 0), w2_0=(mlp_w2, 0),
                 b_scale=(b_scale.reshape(1, 1, -1), None), b_w_in=(b_w_in, None), b_w_grp=(b_w_grp[0], None),
                 b_w_out=(b_w_out, None), w1_1=(mlp_w1, 1), w2_1=(mlp_w2, 1))
    col_sharded = dict(a_w_in=True, a_w_out=False, b_w_in=False, b_w_grp=False, b_w_out=False,
                       w1_0=True, w2_0=False, w1_1=True, w2_1=False)
    in_flight, W = {}, {}

    def launch(tag, keys, deps):
        sp = [k != "b_scale" for k in keys]
        parts = [pieces.get(k, 1) for k in keys]
        zones = [_fill_own(f"gather_own_{k}", units[k][0], BF16 if s else F32, place, layer=units[k][1])
                 for k, s in zip(keys, sp)]
        send, recv, _, zones, tok = _split_start(f"gather_start_{tag}", [], zones, _gather_plan(sp, parts), deps,
                                                 groups=sum(parts))
        first = 0
        for k, z, s, n in zip(keys, zones, sp, parts):
            in_flight[k] = (send[first:first + n], recv[first:first + n], z, s)
            first += n
        return tok

    def arrive_piece(key, k, after):
        send, recv, zone, sp = in_flight[key]
        n = len(send)
        _, zones = _split_wait(f"gather_wait_{key}_{k}", [], [zone], [send[k]], [recv[k]], after,
                               _gather_whole([sp], (k, n)))
        (zone,) = _gather_finish(f"gather_finish_{key}_{k}", zones, (k, n))
        in_flight[key] = (send, recv, zone, sp)
        W[key] = _W4(zone, col_sharded[key])

    def arrive(keys, after):
        send, recv, zones, sp = zip(*[in_flight[k] for k in keys])
        if len(send[0]) > 1:
            for k in range(len(send[0])):
                arrive_piece(keys[0], k, after)
            return
        _, zones = _split_wait(f"gather_wait_{keys[0]}", [], zones, [s[0] for s in send], [r[0] for r in recv],
                               after, _gather_whole(sp))
        relayed = iter(_gather_finish(f"gather_finish_{keys[0]}", [z for z, s in zip(zones, sp) if s]))
        for k, z, s in zip(keys, zones, sp):
            full = next(relayed) if s else z
            W[k] = _W4(full, col_sharded[k]) if k in col_sharded else full

    pieces = dict(w1_0=2, w2_0=2, w1_1=2, w2_1=2)

    token = launch("first", ["a_w_in", "a_w_out"], ())
    token = launch("rest", ["w1_0", "w2_0", "b_scale", "b_w_in", "b_w_grp", "b_w_out", "w1_1", "w2_1"], (token,))

    b_col = a_b_s[0].T

    def residual(acc, res):
        return (res + acc,)

    def sq_relu(acc):
        act = jnp.maximum(acc, 0.0)
        return act, act * act

    def mlp_fwd(tag, h, layer):
        hn = _rms_fwd(f"mlp{tag}_norm", h, norm_mlp[layer:layer + 1])
        arrive([f"w1_{layer}"], hn)
        act, act_sq = _mm_aw(f"mlp{tag}_up", hn, W[f"w1_{layer}"], out_dtypes=(BF16, BF16), epilogue=sq_relu)
        out = h
        for k in range(pieces[f"w2_{layer}"]):
            arrive_piece(f"w2_{layer}", k, act_sq if k == 0 else out)
            out = _mm_aw(f"mlp{tag}_down_{k}", act_sq, W[f"w2_{layer}"], extras=(out,), epilogue=residual,
                         k_piece=(k, pieces[f"w2_{layer}"]))
        return out, (h, hn, act, act_sq)

    hn0 = _rms_fwd("mix_a_norm", x2, norm_mix[0:1])
    arrive(["a_w_in"], token)
    zpre = _mm_aw("mix_a_in", hn0, W["a_w_in"])
    gated = _gate_fwd("mix_a_gate", zpre, a_ln_g, a_ln_b, a_w_s[0], b_col)
    arrive(["a_w_out"], gated)
    h1 = _mm_aw("mix_a_out", gated, W["a_w_out"], extras=(x2,), epilogue=residual)
    h2, mlp0 = mlp_fwd("0", h1, 0)
    hn2 = _rms_fwd("mix_b_norm", h2, norm_mix[1:2])
    arrive(["b_scale", "b_w_in"], hn2)
    scale_full = W["b_scale"].reshape(1, bw)
    vb = _mm_aw("mix_b_in", hn2, W["b_w_in"])
    pooled = _pool_fwd("mix_b_pool", vb)
    arrive(["b_w_grp", "b_w_out"], pooled)
    mixed, ms = _mm_aw("mix_b_grp", pooled, W["b_w_grp"], groups=len(B_WINDOWS), extras=(scale_full,),
                       out_dtypes=(F32, BF16), epilogue=lambda acc, sc: (acc, acc * sc))
    h3 = _mm_aw("mix_b_out", ms, W["b_w_out"], extras=(h2,), epilogue=residual)
    h4, mlp1 = mlp_fwd("1", h3, 1)
    dh4, dh4_b, d_final, loss_part = _final("loss_head", h4, final_norm.reshape(1, -1), tgt)
    g1_like = _W4(None, True, shape=(N_CHIPS, 1, *W["w1_0"].arr.shape[2:]))
    g2_like = _W4(None, False, shape=(N_CHIPS, 1, *W["w2_0"].arr.shape[2:]))

    def exchange(tag, gs):
        zones = [lax.empty((g.shape[0], g.shape[1], g.shape[2] // 2, g.shape[3]), g.dtype) for g in gs]
        send, recv, srcs, zones, tok = _split_start(f"pair_start_{tag}", gs, zones, _pair_plan)
        return (tag, send, recv, srcs, zones), tok

    def reduce(state, after, then_exchange=None):
        tag, send, recv, srcs, zones = state
        srcs, zones = _split_wait(f"pair_wait_{tag}", srcs, zones, send, recv, after, _pair_whole)
        both = [_pair_sum(f"pair_sum_{tag}_{i}", g, o, place) for i, (g, o) in enumerate(zip(srcs, zones))]
        sums, dests = [b[0] for b in both], [b[1] for b in both]
        if then_exchange is None:
            send, recv, sums, dests, tok = _split_start(f"scatter_start_{tag}", sums, dests, _scatter_plan)
            return (tag, send, recv, sums, dests), tok
        other, gs = then_exchange
        n = len(sums)
        zones = [lax.empty((g.shape[0], g.shape[1], g.shape[2] // 2, g.shape[3]), g.dtype) for g in gs]
        send, recv, srcs, lands, tok = _split_start(
            f"scatter_start_{tag}", sums + list(gs), dests + zones,
            lambda s, ld: _scatter_plan(s[:n], ld[:n]) + _pair_plan(s[n:], ld[n:]))
        return ((tag, send[:n], recv[:n], srcs[:n], lands[:n]), tok,
                (other, send[n:], recv[n:], srcs[n:], lands[n:]))

    def relay_all(states, after):
        waited = []
        for tag, send, recv, sums, dests in states:
            waited.append((tag, _split_wait(f"scatter_wait_{tag}", sums, dests, send, recv, after, _scatter_whole)[1]))
        flat = [d for _, ds in waited for d in ds]
        send, recv, _, flat, tok = _split_start(f"relay_start_{waited[0][0]}", [], flat, _relay_plan)
        out, first = [], 0
        for tag, ds in waited:
            last = first + len(ds)
            out.append(((tag, send[first:last], recv[first:last], flat[first:last]), tok))
            first = last
        return out

    def relay(state, after):
        return relay_all([state], after)[0]

    def land(state, after):
        tag, send, recv, dests = state
        return _split_wait(f"relay_wait_{tag}", [], dests, send, recv, after, _relay_whole)[1]

    def mlp_bwd(tag, dh, dh_b, saved, layer, deps, pending=None):
        h, hn, act, act_sq = saved
        dpre = _mm_aw(f"mlp{tag}_down_dx", dh_b, W[f"w2_{layer}"], transpose_w=True, extras=(act,),
                      out_dtypes=(BF16,), epilogue=lambda acc, a: (acc * (2.0 * a),), deps=deps)
        scattering, dw_deps = None, ()
        if pending is not None:
            scattering, tok = reduce(pending, dpre)
            dw_deps = (tok,)
        g_w2 = _mm_dw(f"mlp{tag}_down_dw", act_sq, dh_b, g2_like, deps=dw_deps)
        pair_w2, tok = exchange(f"w2_{layer}", [g_w2])
        dhn = _mm_aw(f"mlp{tag}_up_dx", dpre, W[f"w1_{layer}"], transpose_w=True, deps=(tok,))
        g_w1 = _mm_dw(f"mlp{tag}_up_dw", hn, dpre, g1_like)
        scat_w2, tok2, pair_w1 = reduce(pair_w2, dhn, then_exchange=(f"w1_{layer}", [g_w1]))
        dh_in, dh_in_b, d_norm = _rms_bwd(f"mlp{tag}_norm_bwd", h, norm_mlp[layer:layer + 1], dhn, dh)
        return dh_in, dh_in_b, d_norm, pair_w1, scat_w2, (tok2,), scattering

    dh3, dh3_b, d_norm_mlp1, pair_w1_1, scat_w2_1, toks, _ = mlp_bwd("1", dh4, dh4_b, mlp1, 1, ())
    dms, g_b_out = _mm_bwd("mix_b_out_bwd", dh3_b, W["b_w_out"], ms, deps=toks)
    scat_w1_1, tok = reduce(pair_w1_1, dms)
    dmixed, d_scale = _scale_bwd("mix_b_scale_bwd", dms, mixed, scale_full)
    dpooled = _mm_aw("mix_b_grp_dx", dmixed, W["b_w_grp"], groups=len(B_WINDOWS), transpose_w=True, deps=(tok,))
    g_b_grp = _mm_dw("mix_b_grp_dw", pooled, dmixed, W["b_w_grp"], groups=len(B_WINDOWS))
    dvb = _pool_bwd("mix_b_pool_bwd", dpooled)
    dhn2, g_b_in = _mm_bwd("mix_b_in_bwd", dvb, W["b_w_in"], hn2)
    pair_b, tok = exchange("b", [g_b_out, g_b_grp, g_b_in])
    dh2, dh2_b, d_norm_mix1 = _rms_bwd("mix_b_norm_bwd", h2, norm_mix[1:2], dhn2, dh3)
    dh1, dh1_b, d_norm_mlp0, pair_w1_0, scat_w2_0, toks, scat_b = mlp_bwd("0", dh2, dh2_b, mlp0, 0, (tok,),
                                                                          pending=pair_b)
    dgated, g_a_out = _mm_bwd("mix_a_out_bwd", dh1_b, W["a_w_out"], gated, deps=toks)
    scat_w1_0, tok, pair_a_out = reduce(pair_w1_0, dgated, then_exchange=("a_out", [g_a_out]))
    early = relay_all([scat_w2_1, scat_w1_1, scat_b], dgated)
    dzpre, d_w_s, d_b_col, d_ln_g, d_ln_b = _gate_bwd("mix_a_gate_bwd", zpre, dgated, a_ln_g, a_ln_b, a_w_s[0], b_col)
    dhn0 = _mm_aw("mix_a_in_dx", dzpre, W["a_w_in"], transpose_w=True, deps=(tok, early[0][1]))
    scat_a_out, tok = reduce(pair_a_out, dhn0)
    g_a_in = _mm_dw("mix_a_in_dw", hn0, dzpre, W["a_w_in"], deps=(tok,))
    pair_a_in, tok = exchange("a_in", [g_a_in])
    dx, _, d_norm_mix0 = _rms_bwd("mix_a_norm_bwd", x2, norm_mix[0:1], dhn0, dh1)
    scat_a_in, _ = reduce(pair_a_in, dx)

    small = dict(a_ln_g=(a_ln_g, m_a_ln_g, v_a_ln_g), a_ln_b=(a_ln_b, m_a_ln_b, v_a_ln_b),
                 a_w_s=(a_w_s, m_a_w_s, v_a_w_s), a_b_s=(a_b_s, m_a_b_s, v_a_b_s),
                 b_scale=(b_scale, m_b_scale, v_b_scale), norm_mix=(norm_mix, m_norm_mix, v_norm_mix),
                 norm_mlp=(norm_mlp, m_norm_mlp, v_norm_mlp), final_norm=(final_norm, m_final_norm, v_final_norm))
    small_names = list(small)
    local = dict(a_ln_g=d_ln_g, a_ln_b=d_ln_b, a_w_s=d_w_s[None], a_b_s=d_b_col.T[None], b_scale=d_scale,
                 norm_mix=jnp.concatenate([d_norm_mix0, d_norm_mix1], axis=0),
                 norm_mlp=jnp.concatenate([d_norm_mlp0, d_norm_mlp1], axis=0), final_norm=d_final.reshape(-1))
    device = (4 * xi + 2 * yi + ci).astype(jnp.int32).reshape(1)
    slots = _own_of_eight("small_own", _pack([local[k] for k in small_names]), device)
    small_send, small_recv, _, (slots,), small_tok = _split_start("small_start", [], [slots], _all_plan)

    moments = dict(a_w_in=(m_a_w_in, v_a_w_in), a_w_out=(m_a_w_out, v_a_w_out), b_w_in=(m_b_w_in, v_b_w_in),
                   b_w_grp=(m_b_w_grp, v_b_w_grp), b_w_out=(m_b_w_out, v_b_w_out),
                   mlp_w1=(m_mlp_w1, v_mlp_w1), mlp_w2=(m_mlp_w2, v_mlp_w2))
    weights = dict(a_w_in=a_w_in, a_w_out=a_w_out, b_w_in=b_w_in, b_w_grp=b_w_grp, b_w_out=b_w_out,
                   mlp_w1=mlp_w1, mlp_w2=mlp_w2)
    landing = [(scat_w2_1, [("mlp_w2", 1)]), (scat_w1_1, [("mlp_w1", 1)]),
               (scat_b, [("b_w_out", 0), ("b_w_grp", 0), ("b_w_in", 0)]),
               (scat_w2_0, [("mlp_w2", 0)]), (scat_w1_0, [("mlp_w1", 0)]),
               (scat_a_out, [("a_w_out", 0)]), (scat_a_in, [("a_w_in", 0)])]
    results, after = {}, dx
    relays = list(early)
    for i, (_, members) in enumerate(landing):
        deps = (small_tok,) if i == 0 else ()
        if len(relays) == i + 1 < len(landing):
            relays.append(relay(landing[i + 1][0], after))
            deps = (relays[-1][1],)
        for (k, layer), parts in zip(members, land(relays[i][0], relays[-1][1] if deps else after)):
            shard_shape = (-1, *parts.shape[2:])
            results[k] = _adam_shard(f"adam_{k}_{layer}", weights[k].reshape(shard_shape),
                                     moments[k][0].reshape(shard_shape), moments[k][1].reshape(shard_shape),
                                     parts, layer=layer, prev=results.get(k), deps=deps)
            after = results[k][1]
    grad_out, delta_out, m_out, v_out = {}, {}, {}, {}
    for k, res in results.items():
        grad_out[k], delta_out[k], m_out[k], v_out[k] = [r.reshape(weights[k].shape) for r in res]

    _, (slots,) = _split_wait("small_wait", [], [slots], small_send, small_recv, after, _all_whole)
    reduced = _sum_of_eight("small_sum", slots)
    small_grads = dict(zip(small_names, _unpack(reduced, [local[k].shape for k in small_names])))
    shard_w = b_scale.shape[1]
    small_grads["b_scale"] = lax.dynamic_slice_in_dim(small_grads["b_scale"], chip * shard_w, shard_w, axis=1)
    small_grads = {k: small_grads[k].reshape(small[k][0].shape) for k in small_names}
    packed = [_pack([small[k][i] for k in small_names]) for i in range(3)]
    res = _adam_packed("adam_small", packed[0], _pack([small_grads[k] for k in small_names]), packed[1], packed[2])
    shapes = [small[k][0].shape for k in small_names]
    for k, d, nm, nv in zip(small_names, *[_unpack(r, shapes) for r in res]):
        grad_out[k], delta_out[k], m_out[k], v_out[k] = small_grads[k], d, nm, nv

    loss = lax.psum(loss_part[0, 0], ("x", "y", "c"))
    order = ["a_w_in", "a_ln_g", "a_ln_b", "a_w_s", "a_b_s", "a_w_out", "b_w_in", "b_w_grp", "b_scale", "b_w_out",
             "norm_mix", "norm_mlp", "mlp_w1", "mlp_w2", "final_norm"]
    return (loss, dx[None], *[grad_out[k] for k in order], *[delta_out[k] for k in order],
            *[m_out[k] for k in order], *[v_out[k] for k in order])
```

```python
import math

import jax
import jax.numpy as jnp
from jax import lax
from jax.experimental import pallas as pl
from jax.experimental.pallas import tpu as pltpu

F32 = jnp.float32
BF16 = jnp.bfloat16
MESH = pl.DeviceIdType.MESH

EPS = 1e-6
B_WINDOWS = (2, 4, 8, 16)
ADAM_LR = 0.001
ADAM_B1 = 0.9
ADAM_B2 = 0.999
ADAM_EPS = 1e-08
ADAM_WD = 0.01
ADAM_STEP = 10

N_CHIPS = 4
N_DEV = 8
LANES = 128
PACK_ELEMS = 8 * LANES
VMEM_LIMIT = 56 * 1024 * 1024
ROW_TILE = 512
MM_TM, MM_TN, MM_TK = 1024, 1024, 2048


_ANY = pl.BlockSpec(memory_space=pl.ANY)


def _tile(dim, pref):
    t = min(dim, pref)
    while dim % t:
        t //= 2
    return t


def _params(*sem):
    return pltpu.CompilerParams(dimension_semantics=sem, vmem_limit_bytes=VMEM_LIMIT)


class _W4:
    def __init__(self, arr, col_sharded, shape=None):
        self.arr = arr
        self.nj, self.nl, self.r, self.c = arr.shape if shape is None else shape
        self.col = col_sharded
        self.rows = self.r if col_sharded else self.nj * self.r
        self.cols = self.nj * self.c if col_sharded else self.c

    def tile_rows(self, pref):
        return _tile(self.r, pref)

    def tile_cols(self, pref):
        return _tile(self.c, pref)

    def index(self, layer, rb, cb, tr, tc):
        if self.col:
            n = self.c // tc
            return (cb // n, layer, rb, cb % n)
        n = self.r // tr
        return (rb // n, layer, rb % n, cb)


def _mm_aw(name, a, w, *, layer=0, groups=1, transpose_w=False, extras=(), out_dtypes=(F32,), epilogue=None,
           deps=(), k_piece=None):
    s, ka_total = a.shape
    kdim, ndim = (w.cols, w.rows) if transpose_w else (w.rows, w.cols)
    assert ka_total == groups * kdim, (name, a.shape, kdim, groups)
    span = not w.col and ((not transpose_w and kdim <= MM_TK) or (transpose_w and groups > 1))
    across = transpose_w and w.col and groups == 1 and MM_TK < kdim <= 2 * MM_TK
    if span and transpose_w:
        tm, tn, tk = _tile(s, 2048), ndim, w.tile_cols(MM_TK)
    elif across:
        tm, tn, tk = _tile(s, MM_TM), w.tile_rows(512), kdim
    else:
        tk = kdim if span else (w.tile_cols(MM_TK) if transpose_w else w.tile_rows(MM_TK))
        tm, tn_pref = (_tile(s, 2048), 512) if tk == kdim else (_tile(s, MM_TM), MM_TN)
        tn = w.tile_rows(tn_pref) if transpose_w else w.tile_cols(tn_pref)
    nk, nn = kdim // tk, ndim // tn
    if k_piece is not None:
        assert not (w.col or transpose_w or span or groups > 1), name
        piece, n_pieces = k_piece
        tm, tn, tk = _tile(s, MM_TM), w.tile_cols(MM_TN), w.r // (2 * n_pieces)
        nk, nn = 2 * w.nj, ndim // tn

        def k_block(k):
            return k // 2, (k % 2) * n_pieces + piece

    def lay(g):
        return g if groups > 1 else layer

    a_spec = pl.BlockSpec((tm, tk), lambda g, i, n, k: (i, g * nk + k))
    if k_piece is not None:
        per = w.r // tk
        a_spec = pl.BlockSpec((tm, tk), lambda g, i, n, k: (i, k_block(k)[0] * per + k_block(k)[1]))
        w_spec = pl.BlockSpec((None, None, tk, tn), lambda g, i, n, k: (k_block(k)[0], layer, k_block(k)[1], n))
    elif span and transpose_w:
        w_spec = pl.BlockSpec((w.nj, None, w.r, tk), lambda g, i, n, k: (0, lay(g), 0, k))
    elif span:
        w_spec = pl.BlockSpec((w.nj, None, w.r, tn), lambda g, i, n, k: (0, lay(g), 0, n))
    elif across:
        w_spec = pl.BlockSpec((w.nj, None, tn, w.c), lambda g, i, n, k: (0, layer, n, 0))
    elif transpose_w:
        w_spec = pl.BlockSpec((None, None, tn, tk), lambda g, i, n, k: w.index(lay(g), n, k, tn, tk))
    else:
        w_spec = pl.BlockSpec((None, None, tk, tn), lambda g, i, n, k: w.index(lay(g), k, n, tk, tn))
    ex_specs = []
    for e in extras:
        assert e.shape[1] == groups * ndim and e.shape[0] in (1, s), (name, e.shape)
        if e.shape[0] == 1:
            ex_specs.append(pl.BlockSpec((1, tn), lambda g, i, n, k: (0, g * nn + n)))
        else:
            ex_specs.append(pl.BlockSpec((tm, tn), lambda g, i, n, k: (i, g * nn + n)))
    out_spec = pl.BlockSpec((tm, tn), lambda g, i, n, k: (i, g * nn + n))
    n_ex, n_out, n_dep = len(extras), len(out_dtypes), len(deps)

    def body(a_ref, w_ref, *rest):
        ex, outs = rest[:n_ex], rest[n_ex + n_dep:n_ex + n_dep + n_out]
        av = a_ref[...]
        if av.dtype != BF16:
            av = av.astype(BF16)
        nt = (((1,), (1,)), ((), ()))
        if across:
            prod = lax.dot_general(av[:, :w.c], w_ref[0], nt, preferred_element_type=F32)
            for j in range(1, w.nj):
                prod += lax.dot_general(av[:, j * w.c:(j + 1) * w.c], w_ref[j], nt, preferred_element_type=F32)
        else:
            wv = w_ref[...]
            if span:
                wv = wv.reshape((tn, tk) if transpose_w else (tk, tn))
            prod = lax.dot_general(av, wv, nt, preferred_element_type=F32) if transpose_w else jnp.dot(
                av, wv, preferred_element_type=F32)

        def finish(total):
            vals = (total,) if epilogue is None else epilogue(total, *[e[...] for e in ex])
            for o, v in zip(outs, vals):
                o[...] = v.astype(o.dtype)

        if nk == 1:
            finish(prod)
            return
        acc, k = rest[-1], pl.program_id(3)

        @pl.when(k == 0)
        def _():
            acc[...] = prod

        @pl.when(k > 0)
        def _():
            acc[...] += prod

        @pl.when(k == nk - 1)
        def _():
            finish(acc[...])

    outs = pl.pallas_call(
        body, name=name, grid=(groups, s // tm, nn, nk),
        in_specs=[a_spec, w_spec, *ex_specs] + [_ANY] * n_dep, out_specs=[out_spec] * n_out,
        out_shape=[jax.ShapeDtypeStruct((s, groups * ndim), dt) for dt in out_dtypes],
        scratch_shapes=[pltpu.VMEM((tm, tn), F32)] if nk > 1 else [],
        compiler_params=_params("parallel", "parallel", "parallel", "arbitrary"),
    )(a, w.arr, *extras, *deps)
    return outs[0] if n_out == 1 else outs


def _mm_dw(name, a, b, like, *, layer=0, groups=1, deps=()):
    s, ka_total = a.shape
    rows, cols = ka_total // groups, b.shape[1] // groups
    assert (rows, cols) == (like.rows, like.cols) and b.shape[0] == s, (name, a.shape, b.shape)
    span = groups > 1 and not like.col
    tm, tn, tk = rows if span else like.tile_rows(MM_TM), like.tile_cols(2 * MM_TN), _tile(s, MM_TK)
    nr, nc, nk = rows // tm, cols // tn, s // tk
    assert nk == 1 or not span

    def lay(g):
        return g if groups > 1 else layer

    in_specs = [pl.BlockSpec((tk, tm), lambda g, n, i, k: (k, g * nr + i)),
                pl.BlockSpec((tk, tn), lambda g, n, i, k: (k, g * nc + n))]
    in_specs += [_ANY] * len(deps)

    def body(a_ref, b_ref, *rest):
        av, bv = a_ref[...], b_ref[...]
        if av.dtype != BF16:
            av = av.astype(BF16)
        if bv.dtype != BF16:
            bv = bv.astype(BF16)
        prod = lax.dot_general(av, bv, (((0,), (0,)), ((), ())), preferred_element_type=F32)
        if nk == 1:
            rest[-1][...] = prod.astype(BF16).reshape(rest[-1].shape)
            return
        o_ref, acc, k = rest[-2], rest[-1], pl.program_id(3)

        @pl.when(k == 0)
        def _():
            acc[...] = prod

        @pl.when(k > 0)
        def _():
            acc[...] += prod

        @pl.when(k == nk - 1)
        def _():
            o_ref[...] = acc[...].astype(BF16)

    if span:
        out_spec = pl.BlockSpec((like.nj, None, like.r, tn), lambda g, n, i, k: (0, g, 0, n))
    else:
        out_spec = pl.BlockSpec((None, None, tm, tn), lambda g, n, i, k: like.index(lay(g), i, n, tm, tn))
    return pl.pallas_call(
        body, name=name, grid=(groups, nc, nr, nk), in_specs=in_specs, out_specs=out_spec,
        out_shape=jax.ShapeDtypeStruct((like.nj, like.nl, like.r, like.c), BF16),
        scratch_shapes=[pltpu.VMEM((tm, tn), F32)] if nk > 1 else [],
        compiler_params=_params("parallel", "parallel", "parallel", "arbitrary"),
    )(a, b, *deps)


def _mm_bwd(name, dy, w, a, *, deps=()):
    s, c = dy.shape
    assert not w.col and w.nl == 1 and w.c == c and a.shape == (s, w.rows), (name, dy.shape, a.shape)
    r = w.r

    def body(dy_ref, w_ref, a_ref, *rest):
        dx_ref, dw_ref = rest[-2:]
        dyv = dy_ref[...]
        dx_ref[...] = lax.dot_general(dyv, w_ref[...], (((1,), (1,)), ((), ())), preferred_element_type=F32)
        dw_ref[...] = lax.dot_general(a_ref[...], dyv, (((0,), (0,)), ((), ())),
                                      preferred_element_type=F32).astype(BF16)

    return pl.pallas_call(
        body, name=name, grid=(w.nj,),
        in_specs=[pl.BlockSpec((s, c), lambda j: (0, 0)), pl.BlockSpec((None, None, r, c), lambda j: (j, 0, 0, 0)),
                  pl.BlockSpec((s, r), lambda j: (0, j))] + [_ANY] * len(deps),
        out_specs=[pl.BlockSpec((s, r), lambda j: (0, j)), pl.BlockSpec((None, None, r, c), lambda j: (j, 0, 0, 0))],
        out_shape=[jax.ShapeDtypeStruct((s, w.rows), F32), jax.ShapeDtypeStruct((w.nj, 1, r, c), BF16)],
        compiler_params=_params("parallel"),
    )(dy, w.arr, a, *deps)


def _row_spec(tr, d):
    return pl.BlockSpec((tr, d), lambda i: (i, 0))


def _vec_spec(d):
    return pl.BlockSpec((1, d), lambda i: (0, 0))


def _rms_fwd(name, x, g):
    s, d = x.shape
    tr = _tile(s, ROW_TILE)

    def body(x_ref, g_ref, o_ref):
        xv = x_ref[...]
        r = lax.rsqrt(jnp.mean(xv * xv, axis=-1, keepdims=True) + EPS)
        o_ref[...] = (xv * r * g_ref[...]).astype(BF16)

    return pl.pallas_call(
        body, name=name, grid=(s // tr,), in_specs=[_row_spec(tr, d), _vec_spec(d)],
        out_specs=_row_spec(tr, d), out_shape=jax.ShapeDtypeStruct((s, d), BF16),
        compiler_params=_params("parallel"),
    )(x, g)


def _rms_bwd(name, x, g, dhn, dres):
    s, d = x.shape
    tr = _tile(s, ROW_TILE)

    def body(x_ref, g_ref, dhn_ref, dres_ref, dx_ref, dxb_ref, dg_ref):
        @pl.when(pl.program_id(0) == 0)
        def _():
            dg_ref[...] = jnp.zeros_like(dg_ref)

        xv = x_ref[...]
        r = lax.rsqrt(jnp.mean(xv * xv, axis=-1, keepdims=True) + EPS)
        xh = xv * r
        dy = dhn_ref[...]
        dg_ref[...] += jnp.sum(dy * xh, axis=0, keepdims=True)
        dxh = dy * g_ref[...]
        dx = dres_ref[...] + r * (dxh - xh * jnp.mean(dxh * xh, axis=-1, keepdims=True))
        dx_ref[...] = dx
        dxb_ref[...] = dx.astype(BF16)

    return pl.pallas_call(
        body, name=name, grid=(s // tr,),
        in_specs=[_row_spec(tr, d), _vec_spec(d), _row_spec(tr, d), _row_spec(tr, d)],
        out_specs=[_row_spec(tr, d), _row_spec(tr, d), _vec_spec(d)],
        out_shape=[jax.ShapeDtypeStruct((s, d), F32), jax.ShapeDtypeStruct((s, d), BF16),
                   jax.ShapeDtypeStruct((1, d), F32)],
        compiler_params=_params("arbitrary"),
    )(x, g, dhn, dres)


def _final(name, h, g, tgt):
    s, d = h.shape
    tr = _tile(s, ROW_TILE)

    def body(h_ref, g_ref, t_ref, dh_ref, dhb_ref, dg_ref, loss_ref):
        @pl.when(pl.program_id(0) == 0)
        def _():
            dg_ref[...] = jnp.zeros_like(dg_ref)
            loss_ref[...] = jnp.zeros_like(loss_ref)

        hv = h_ref[...]
        r = lax.rsqrt(jnp.mean(hv * hv, axis=-1, keepdims=True) + EPS)
        xh = hv * r
        gv = g_ref[...]
        err = xh * gv - t_ref[...]
        part = 0.5 * jnp.sum(jnp.mean(err * err, axis=-1, keepdims=True), axis=0, keepdims=True)
        loss_ref[...] += jnp.broadcast_to(part, loss_ref.shape)
        dy = err * (1.0 / d)
        dg_ref[...] += jnp.sum(dy * xh, axis=0, keepdims=True)
        dxh = dy * gv
        dh = r * (dxh - xh * jnp.mean(dxh * xh, axis=-1, keepdims=True))
        dh_ref[...] = dh
        dhb_ref[...] = dh.astype(BF16)

    return pl.pallas_call(
        body, name=name, grid=(s // tr,),
        in_specs=[_row_spec(tr, d), _vec_spec(d), _row_spec(tr, d)],
        out_specs=[_row_spec(tr, d), _row_spec(tr, d), _vec_spec(d), _vec_spec(LANES)],
        out_shape=[jax.ShapeDtypeStruct((s, d), F32), jax.ShapeDtypeStruct((s, d), BF16),
                   jax.ShapeDtypeStruct((1, d), F32), jax.ShapeDtypeStruct((1, LANES), F32)],
        compiler_params=_params("arbitrary"),
    )(h, g, tgt)


_SQRT_HALF = 1.0 / math.sqrt(2.0)
_INV_SQRT_2PI = 1.0 / math.sqrt(2.0 * math.pi)


def _gelu(x):
    return x * (lax.erf(x * _SQRT_HALF) + 1.0) * 0.5


def _gelu_grad(x):
    return 0.5 * (lax.erf(x * _SQRT_HALF) + 1.0) + x * jnp.exp(-0.5 * x * x) * _INV_SQRT_2PI


def _causal(chunk):
    t = lax.broadcasted_iota(jnp.int32, (chunk, chunk), 0)
    sidx = lax.broadcasted_iota(jnp.int32, (chunk, chunk), 1)
    return sidx <= t


def _layernorm_parts(v, g, b):
    mu = jnp.mean(v, axis=-1, keepdims=True)
    vc = v - mu
    rs = lax.rsqrt(jnp.mean(vc * vc, axis=-1, keepdims=True) + EPS)
    vhat = vc * rs
    return vhat, rs, vhat * g + b


def _gate_fwd(name, zpre, ln_g, ln_b, w_s, b_col):
    s, aw2 = zpre.shape
    aw = aw2 // 2
    ng, chunk, _ = w_s.shape
    dh = aw // ng

    def body(z_ref, g_ref, b_ref, ws_ref, bc_ref, o_ref):
        u = _gelu(z_ref[:, :aw])
        v = _gelu(z_ref[:, aw:])
        _, _, vln = _layernorm_parts(v, g_ref[...], b_ref[...])
        mask = _causal(chunk)
        for gi in range(ng):
            sl = slice(gi * dh, (gi + 1) * dh)
            wm = jnp.where(mask, ws_ref[gi], 0.0).astype(BF16)
            sg = jnp.dot(wm, vln[:, sl].astype(BF16), preferred_element_type=F32) + bc_ref[:, gi:gi + 1]
            o_ref[:, sl] = (u[:, sl] * sg).astype(BF16)

    return pl.pallas_call(
        body, name=name, grid=(s // chunk,),
        in_specs=[_row_spec(chunk, aw2), _vec_spec(aw), _vec_spec(aw),
                  pl.BlockSpec((ng, chunk, chunk), lambda i: (0, 0, 0)),
                  pl.BlockSpec((chunk, ng), lambda i: (0, 0))],
        out_specs=_row_spec(chunk, aw), out_shape=jax.ShapeDtypeStruct((s, aw), BF16),
        compiler_params=_params("parallel"),
    )(zpre, ln_g, ln_b, w_s, b_col)


def _gate_bwd(name, zpre, dgated, ln_g, ln_b, w_s, b_col):
    s, aw2 = zpre.shape
    aw = aw2 // 2
    ng, chunk, _ = w_s.shape
    dh = aw // ng

    def body(z_ref, dgt_ref, g_ref, b_ref, ws_ref, bc_ref, dz_ref, dws_ref, dbc_ref, dlg_ref, dlb_ref,
             du_scr, dvln_scr):
        @pl.when(pl.program_id(0) == 0)
        def _():
            dws_ref[...] = jnp.zeros_like(dws_ref)
            dbc_ref[...] = jnp.zeros_like(dbc_ref)
            dlg_ref[...] = jnp.zeros_like(dlg_ref)
            dlb_ref[...] = jnp.zeros_like(dlb_ref)

        zu = z_ref[:, :aw]
        zv = z_ref[:, aw:]
        u = _gelu(zu)
        lg = g_ref[...]
        vhat, rs, vln = _layernorm_parts(_gelu(zv), lg, b_ref[...])
        mask = _causal(chunk)
        for gi in range(ng):
            sl = slice(gi * dh, (gi + 1) * dh)
            wm = jnp.where(mask, ws_ref[gi], 0.0).astype(BF16)
            vg = vln[:, sl].astype(BF16)
            sg = jnp.dot(wm, vg, preferred_element_type=F32) + bc_ref[:, gi:gi + 1]
            dgt = dgt_ref[:, sl]
            du_scr[:, sl] = dgt * sg
            ds = dgt * u[:, sl]
            dbc_ref[:, gi:gi + 1] += jnp.sum(ds, axis=-1, keepdims=True)
            dsb = ds.astype(BF16)
            dwm = lax.dot_general(dsb, vg, (((1,), (1,)), ((), ())), preferred_element_type=F32)
            dws_ref[gi] += jnp.where(mask, dwm, 0.0)
            dvln_scr[:, sl] = lax.dot_general(wm, dsb, (((0,), (0,)), ((), ())), preferred_element_type=F32)
        dvln = dvln_scr[...]
        dlb_ref[...] += jnp.sum(dvln, axis=0, keepdims=True)
        dlg_ref[...] += jnp.sum(dvln * vhat, axis=0, keepdims=True)
        dvh = dvln * lg
        dv = rs * (dvh - jnp.mean(dvh, axis=-1, keepdims=True)
                   - vhat * jnp.mean(dvh * vhat, axis=-1, keepdims=True))
        dz_ref[:, :aw] = (du_scr[...] * _gelu_grad(zu)).astype(BF16)
        dz_ref[:, aw:] = (dv * _gelu_grad(zv)).astype(BF16)

    return pl.pallas_call(
        body, name=name, grid=(s // chunk,),
        in_specs=[_row_spec(chunk, aw2), _row_spec(chunk, aw), _vec_spec(aw), _vec_spec(aw),
                  pl.BlockSpec((ng, chunk, chunk), lambda i: (0, 0, 0)),
                  pl.BlockSpec((chunk, ng), lambda i: (0, 0))],
        out_specs=[_row_spec(chunk, aw2), pl.BlockSpec((ng, chunk, chunk), lambda i: (0, 0, 0)),
                   pl.BlockSpec((chunk, ng), lambda i: (0, 0)), _vec_spec(aw), _vec_spec(aw)],
        out_shape=[jax.ShapeDtypeStruct((s, aw2), BF16), jax.ShapeDtypeStruct((ng, chunk, chunk), F32),
                   jax.ShapeDtypeStruct((chunk, ng), F32), jax.ShapeDtypeStruct((1, aw), F32),
                   jax.ShapeDtypeStruct((1, aw), F32)],
        scratch_shapes=[pltpu.VMEM((chunk, aw), F32), pltpu.VMEM((chunk, aw), F32)],
        compiler_params=_params("arbitrary"),
    )(zpre, dgated, ln_g, ln_b, w_s, b_col)


def _pool_select(g, parts):
    out = parts[-1]
    for gi in range(len(parts) - 2, -1, -1):
        out = jnp.where(g == gi, parts[gi], out)
    return out


def _pool_specs(s, bw):
    head = bw // len(B_WINDOWS)
    tc = _tile(head, 256)
    nb = head // tc
    return tc, (len(B_WINDOWS), nb), pl.BlockSpec((s, tc), lambda g, j: (0, g * nb + j))


def _pool_window(g, t):
    w = _pool_select(g, [jnp.full(t.shape, wi, jnp.int32) for wi in B_WINDOWS])
    return jnp.minimum(t + 1, w).astype(F32)


def _pool_fwd(name, vb):
    assert B_WINDOWS == (2, 4, 8, 16)
    s, bw = vb.shape
    tc, grid, spec = _pool_specs(s, bw)

    def body(v_ref, o_ref):
        g = pl.program_id(0)
        v = v_ref[...]
        t = lax.broadcasted_iota(jnp.int32, (s, tc), 0)

        def down(x, k):
            return jnp.where(t >= k, pltpu.roll(x, k, 0), 0.0)

        sums, cur, k = [], v, 1
        for _ in B_WINDOWS:
            cur = cur + down(cur, k)
            sums.append(cur)
            k *= 2
        o_ref[...] = (_pool_select(g, sums) / _pool_window(g, t) - v).astype(BF16)

    return pl.pallas_call(
        body, name=name, grid=grid, in_specs=[spec], out_specs=spec,
        out_shape=jax.ShapeDtypeStruct((s, bw), BF16), compiler_params=_params("parallel", "parallel"),
    )(vb)


def _pool_bwd(name, dpooled):
    s, bw = dpooled.shape
    tc, grid, spec = _pool_specs(s, bw)

    def body(d_ref, o_ref):
        g = pl.program_id(0)
        dp = d_ref[...]
        t = lax.broadcasted_iota(jnp.int32, (s, tc), 0)

        def up(x, k):
            return jnp.where(t < s - k, pltpu.roll(x, s - k, 0), 0.0)

        sums, cur, k = [], dp / _pool_window(g, t), 1
        for _ in B_WINDOWS:
            cur = cur + up(cur, k)
            sums.append(cur)
            k *= 2
        o_ref[...] = (_pool_select(g, sums) - dp).astype(BF16)

    return pl.pallas_call(
        body, name=name, grid=grid, in_specs=[spec], out_specs=spec,
        out_shape=jax.ShapeDtypeStruct((s, bw), BF16), compiler_params=_params("parallel", "parallel"),
    )(dpooled)


def _scale_bwd(name, dms, mixed, scale):
    s, bw = dms.shape
    tr = _tile(s, ROW_TILE)

    def body(d_ref, m_ref, sc_ref, o_ref, ds_ref):
        @pl.when(pl.program_id(0) == 0)
        def _():
            ds_ref[...] = jnp.zeros_like(ds_ref)

        dv = d_ref[...]
        ds_ref[...] += jnp.sum(dv * m_ref[...], axis=0, keepdims=True)
        o_ref[...] = (dv * sc_ref[...]).astype(BF16)

    return pl.pallas_call(
        body, name=name, grid=(s // tr,), in_specs=[_row_spec(tr, bw), _row_spec(tr, bw), _vec_spec(bw)],
        out_specs=[_row_spec(tr, bw), _vec_spec(bw)],
        out_shape=[jax.ShapeDtypeStruct((s, bw), BF16), jax.ShapeDtypeStruct((1, bw), F32)],
        compiler_params=_params("arbitrary"),
    )(dms, mixed, scale)


def _adam_update(w, g, m, v):
    m = ADAM_B1 * m + (1.0 - ADAM_B1) * g
    v = ADAM_B2 * v + (1.0 - ADAM_B2) * (g * g)
    m_hat = m / (1.0 - ADAM_B1 ** ADAM_STEP)
    v_hat = v / (1.0 - ADAM_B2 ** ADAM_STEP)
    delta = -ADAM_LR * (m_hat / (jnp.sqrt(v_hat) + ADAM_EPS) + ADAM_WD * w)
    return delta, m, v


def _adam_shard(name, w, m, v, parts, layer=0, prev=None, deps=()):
    nl, r, c = w.shape
    nj, nlp = parts.shape[:2]
    tr = _tile(r, ROW_TILE // 2)
    spec = pl.BlockSpec((None, tr, c), lambda l, i: (layer + l, i, 0))

    def body(w_ref, m_ref, v_ref, p_ref, *rest):
        g_ref, d_ref, nm_ref, nv_ref = rest[-4:]
        g = p_ref[0].astype(F32)
        for j in range(1, nj):
            g = g + p_ref[j].astype(F32)
        delta, nm, nv = _adam_update(w_ref[...], g, m_ref[...], v_ref[...])
        g_ref[...] = g
        d_ref[...] = delta
        nm_ref[...] = nm
        nv_ref[...] = nv

    prev = () if prev is None else tuple(prev)
    return pl.pallas_call(
        body, name=name, grid=(nlp, r // tr),
        in_specs=[spec, spec, spec, pl.BlockSpec((nj, None, tr, c), lambda l, i: (0, l, i, 0))]
        + [_ANY] * (len(prev) + len(deps)),
        out_specs=[spec] * 4, out_shape=[jax.ShapeDtypeStruct(w.shape, F32)] * 4,
        input_output_aliases={4 + i: i for i in range(len(prev))},
        compiler_params=_params("parallel", "parallel"),
    )(w, m, v, parts, *prev, *deps)


def _adam_packed(name, w, g, m, v):
    rows, lanes = w.shape
    tr = rows
    spec = pl.BlockSpec((tr, lanes), lambda i: (i, 0))

    def body(w_ref, g_ref, m_ref, v_ref, d_ref, nm_ref, nv_ref):
        delta, nm, nv = _adam_update(w_ref[...], g_ref[...], m_ref[...], v_ref[...])
        d_ref[...] = delta
        nm_ref[...] = nm
        nv_ref[...] = nv

    return pl.pallas_call(
        body, name=name, grid=(rows // tr,), in_specs=[spec] * 4, out_specs=[spec] * 3,
        out_shape=[jax.ShapeDtypeStruct(w.shape, F32)] * 3, compiler_params=_params("parallel"),
    )(w, g, m, v)


def _place():
    x, y, c = lax.axis_index("x"), lax.axis_index("y"), lax.axis_index("c")
    chips = [(1 - x, y), (x, 1 - y), (1 - x, 1 - y)]
    return x, y, c, chips


def _remote(src, dst, send_sem, recv_sem, device):
    return pltpu.make_async_remote_copy(src_ref=src, dst_ref=dst, send_sem=send_sem, recv_sem=recv_sem,
                                        device_id=device, device_id_type=MESH)


def _half(ref_rows, cc):
    h = ref_rows // 2
    return pl.ds(cc * h, h)


_HBM = pl.BlockSpec(memory_space=pltpu.HBM)
_SEM = pl.BlockSpec(memory_space=pltpu.SEMAPHORE)
_EFFECT = pltpu.SideEffectType.DATAFLOW_SIDE_EFFECTING


def _in_hbm(arr):
    return pltpu.with_memory_space_constraint(arr, pltpu.HBM)


def _gather_rows(land, cc, part=(0, 1)):
    k, n = part
    h = land.shape[2] // 2
    return pl.ds(cc * h + k * (h // n), h // n)


def _gather_block(land, split, j, cc, part=(0, 1)):
    if split:
        return land.at[j, :, _gather_rows(land, cc, part), :]
    return land.at[j]


def _split_start(name, srcs, lands, plan, deps=(), groups=None):
    ns = len(srcs)
    nl = len(lands) if groups is None else groups
    both = list(srcs) + list(lands)
    nb = len(both)

    def body(*refs):
        s, ld = refs[:ns], refs[ns:nb]
        outs = refs[nb + len(deps):]
        send, recv, token = outs[:nl], outs[nl:2 * nl], outs[-1]
        for a, copies in enumerate(plan(s, ld)):
            for src, dst, peer in copies:
                _remote(src, dst, send[a], recv[a], peer).start()
        token[...] = jnp.zeros_like(token)

    outs = pl.pallas_call(
        body, name=name, in_specs=[_HBM] * nb + [_ANY] * len(deps),
        out_specs=[_SEM] * (2 * nl) + [_HBM] * nb + [pl.BlockSpec(memory_space=pltpu.VMEM)],
        out_shape=[pltpu.SemaphoreType.DMA(())] * (2 * nl) + [pltpu.HBM(b.shape, b.dtype) for b in both]
        + [jax.ShapeDtypeStruct((8, LANES), F32)],
        input_output_aliases={i: 2 * nl + i for i in range(nb)},
        compiler_params=pltpu.CompilerParams(has_side_effects=_EFFECT),
    )(*[_in_hbm(b) for b in both], *deps)
    thru = outs[2 * nl:2 * nl + nb]
    return list(outs[:nl]), list(outs[nl:2 * nl]), list(thru[:ns]), list(thru[ns:]), outs[-1]


def _split_wait(name, srcs, lands, send, recv, after, whole):
    ns, nl = len(srcs), len(send)
    both = list(srcs) + list(lands)
    nb = len(both)

    def body(*refs):
        ld, snd, rcv = refs[ns:nb], refs[nb:nb + nl], refs[nb + nl:nb + 2 * nl]
        x, y, c, _ = _place()
        for a, blk in enumerate(whole(ld)):
            every = _remote(blk, blk, snd[a], rcv[a], (x, y, c))
            every.wait_send()
            every.wait_recv()

    outs = pl.pallas_call(
        body, name=name, in_specs=[_HBM] * nb + [_SEM] * (2 * nl) + [_ANY], out_specs=[_HBM] * nb,
        out_shape=[pltpu.HBM(b.shape, b.dtype) for b in both],
        input_output_aliases={i: i for i in range(nb)},
        compiler_params=pltpu.CompilerParams(has_side_effects=_EFFECT),
    )(*both, *send, *recv, after)
    return list(outs[:ns]), list(outs[ns:])


def _gather_plan(split, parts):
    def plan(srcs, lands):
        x, y, c, chips = _place()
        out = []
        for ld, sp, n in zip(lands, split, parts):
            for k in range(n):
                blk = _gather_block(ld, sp, 2 * x + y, c, (k, n))
                out.append([(blk, blk, (qx, qy, c)) for qx, qy in chips])
        return out
    return plan


def _gather_whole(split, part=(0, 1)):
    def whole(lands):
        _, _, c, _ = _place()
        return [ld.at[pl.ds(0, 3), :, _gather_rows(ld, c, part), :] if sp else ld.at[pl.ds(0, 3)]
                for ld, sp in zip(lands, split)]
    return whole


def _relay_plan(srcs, lands):
    x, y, c, _ = _place()
    out = []
    for ld in lands:
        blocks = [ld.at[j, :, _half(ld.shape[2], c), :] for j in range(N_CHIPS)]
        out.append([(blk, blk, (x, y, 1 - c)) for blk in blocks])
    return out


def _relay_whole(lands):
    _, _, c, _ = _place()
    return [ld.at[:, :, _half(ld.shape[2], c), :] for ld in lands]


def _fill_own(name, shard, dtype, place, layer=None):
    nl, r, c = shard.shape
    first = 0
    if layer is not None:
        nl, first = 1, layer
    tr = _tile(r, 512)

    def body(p_ref, s_ref, o_ref):
        o_ref[...] = s_ref[...].astype(o_ref.dtype)

    return pl.pallas_call(
        body, name=name,
        grid_spec=pltpu.PrefetchScalarGridSpec(
            num_scalar_prefetch=1, grid=(nl, r // tr),
            in_specs=[pl.BlockSpec((None, tr, c), lambda l, i, p: (first + l, i, 0))],
            out_specs=pl.BlockSpec((None, None, tr, c), lambda l, i, p: (p[0], l, i, 0))),
        out_shape=jax.ShapeDtypeStruct((N_CHIPS, nl, r, c), dtype),
        compiler_params=_params("parallel", "parallel"),
    )(place, shard)


def _gather_finish(name, lands, part=(0, 1)):
    n = len(lands)

    def body(*refs):
        outs = refs[n:2 * n]
        fsend, frecv = refs[2 * n:]
        x, y, c, chips = _place()
        sib = (x, y, 1 - c)

        def relay(a, qi, cc):
            qx, qy = chips[qi]
            blk = _gather_block(outs[a], True, 2 * qx + qy, cc, part)
            return _remote(blk, blk, fsend.at[a, qi], frecv.at[a, qi], sib)

        relays = [relay(a, qi, c) for a in range(n) for qi in range(3)]
        for cp in relays:
            cp.start()
        for a in range(n):
            for qi in range(3):
                relay(a, qi, 1 - c).wait_recv()
        for cp in relays:
            cp.wait_send()

    outs = pl.pallas_call(
        body, name=name, in_specs=[_ANY] * n, out_specs=[_ANY] * n,
        out_shape=[jax.ShapeDtypeStruct(ld.shape, ld.dtype) for ld in lands],
        input_output_aliases={i: i for i in range(n)},
        scratch_shapes=[pltpu.SemaphoreType.DMA((n, 3))] * 2,
    )(*lands)
    return list(outs)


def _pair_plan(srcs, lands):
    x, y, c, _ = _place()
    return [[(s.at[:, :, _half(s.shape[2], 1 - c), :], ld, (x, y, 1 - c))] for s, ld in zip(srcs, lands)]


def _pair_whole(lands):
    return list(lands)


def _scatter_plan(srcs, lands):
    x, y, c, chips = _place()
    return [[(s.at[2 * qx + qy], ld.at[2 * x + y, :, _half(ld.shape[2], c), :], (qx, qy, c)) for qx, qy in chips]
            for s, ld in zip(srcs, lands)]


def _scatter_whole(lands):
    _, _, c, _ = _place()
    return [ld.at[pl.ds(0, 3), :, _half(ld.shape[2], c), :] for ld in lands]


def _pair_sum(name, grad, other, place):
    nj, nl, r, c = grad.shape
    h = r // 2
    tr = _tile(h, 2 * ROW_TILE)
    nb = h // tr

    def body(p_ref, g_ref, o_ref, q_ref, d_ref):
        q = (g_ref[...].astype(F32) + o_ref[...].astype(F32)).astype(BF16)
        q_ref[...] = q

        @pl.when(pl.program_id(2) == p_ref[0])
        def _():
            d_ref[...] = q

    blk = (None, None, tr, c)
    return pl.pallas_call(
        body, name=name,
        grid_spec=pltpu.PrefetchScalarGridSpec(
            num_scalar_prefetch=1, grid=(nl, nb, nj),
            in_specs=[pl.BlockSpec(blk, lambda l, i, j, p: (j, l, p[1] * nb + i, 0)),
                      pl.BlockSpec(blk, lambda l, i, j, p: (j, l, i, 0))],
            out_specs=[pl.BlockSpec(blk, lambda l, i, j, p: (j, l, i, 0)),
                       pl.BlockSpec(blk, lambda l, i, j, p: (p[0], l, p[1] * nb + i, 0))]),
        out_shape=[jax.ShapeDtypeStruct((nj, nl, h, c), BF16), jax.ShapeDtypeStruct((nj, nl, r, c), BF16)],
        compiler_params=_params("parallel", "parallel", "arbitrary"),
    )(place, grad, other)


def _own_of_eight(name, packed, device):
    rows, lanes = packed.shape
    tr = rows

    def body(d_ref, s_ref, o_ref):
        o_ref[...] = s_ref[...]

    return pl.pallas_call(
        body, name=name,
        grid_spec=pltpu.PrefetchScalarGridSpec(
            num_scalar_prefetch=1, grid=(rows // tr,),
            in_specs=[pl.BlockSpec((tr, lanes), lambda i, d: (i, 0))],
            out_specs=pl.BlockSpec((None, tr, lanes), lambda i, d: (d[0], i, 0))),
        out_shape=jax.ShapeDtypeStruct((N_DEV, rows, lanes), packed.dtype),
        compiler_params=_params("parallel"),
    )(device, packed)


def _all_plan(srcs, lands):
    x, y, c, _ = _place()
    (ld,) = lands
    blk = ld.at[4 * x + 2 * y + c]
    flips = [(a, b, d) for a in (0, 1) for b in (0, 1) for d in (0, 1) if a + b + d]
    return [[(blk, blk, (x + a - 2 * a * x, y + b - 2 * b * y, c + d - 2 * d * c)) for a, b, d in flips]]


def _all_whole(lands):
    return [lands[0].at[pl.ds(0, N_DEV - 1)]]


def _sum_of_eight(name, slots):
    n, rows, lanes = slots.shape
    tr = rows

    def body(s_ref, o_ref):
        total = s_ref[0]
        for d in range(1, n):
            total = total + s_ref[d]
        o_ref[...] = total

    return pl.pallas_call(
        body, name=name, grid=(rows // tr,),
        in_specs=[pl.BlockSpec((n, tr, lanes), lambda i: (0, i, 0))],
        out_specs=pl.BlockSpec((tr, lanes), lambda i: (i, 0)),
        out_shape=jax.ShapeDtypeStruct((rows, lanes), F32), compiler_params=_params("parallel"),
    )(slots)


def _pack(parts):
    rows = []
    for p in parts:
        flat = p.reshape(-1)
        pad = (-flat.shape[0]) % PACK_ELEMS
        rows.append(jnp.pad(flat, (0, pad)).reshape(-1, LANES))
    return jnp.concatenate(rows, axis=0)


def _unpack(packed, shapes):
    out, row = [], 0
    for sh in shapes:
        size = math.prod(sh)
        nrows = -(-size // PACK_ELEMS) * (PACK_ELEMS // LANES)
        out.append(packed[row:row + nrows].reshape(-1)[:size].reshape(sh))
        row += nrows
    return out


def kernel(x, a_w_in, a_ln_g, a_ln_b, a_w_s, a_b_s, a_w_out, b_w_in, b_w_grp, b_scale, b_w_out, norm_mix, norm_mlp, mlp_w1, mlp_w2, final_norm, loss_target, m_a_w_in, m_a_ln_g, m_a_ln_b, m_a_w_s, m_a_b_s, m_a_w_out, m_b_w_in, m_b_w_grp, m_b_scale, m_b_w_out, m_norm_mix, m_norm_mlp, m_mlp_w1, m_mlp_w2, m_final_norm, v_a_w_in, v_a_ln_g, v_a_ln_b, v_a_w_s, v_a_b_s, v_a_w_out, v_b_w_in, v_b_w_grp, v_b_scale, v_b_w_out, v_norm_mix, v_norm_mlp, v_mlp_w1, v_mlp_w2, v_final_norm):
    xi, yi, ci = lax.axis_index("x"), lax.axis_index("y"), lax.axis_index("c")
    chip = 2 * xi + yi
    place = jnp.stack([chip, ci]).astype(jnp.int32)
    x2, tgt = x[0], loss_target[0]
    bw = b_scale.shape[1] * N_CHIPS

    units = dict(a_w_in=(a_w_in, None), a_w_out=(a_w_out, None), w1_0=(mlp_w1, 0), w2_0=(mlp_w2, 0),
                 b_scale=(b_scale.reshape(1, 1, -1), None), b_w_in=(b_w_in, None), b_w_grp=(b_w_grp[0], None),
                 b_w_out=(b_w_out, None), w1_1=(mlp_w1, 1), w2_1=(mlp_w2, 1))
    col_sharded = dict(a_w_in=True, a_w_out=False, b_w_in=False, b_w_grp=False, b_w_out=False,
                       w1_0=True, w2_0=False, w1_1=True, w2_1=False)
    in_flight, W = {}, {}

    def launch(tag, keys, deps):
        sp = [k != "b_scale" for k in keys]
        parts = [pieces.get(k, 1) for k in keys]
        zones = [_fill_own(f"gather_own_{k}", units[k][0], BF16 if s else F32, place, layer=units[k][1])
                 for k, s in zip(keys, sp)]
        send, recv, _, zones, tok = _split_start(f"gather_start_{tag}", [], zones, _gather_plan(sp, parts), deps,
                                                 groups=sum(parts))
        first = 0
        for k, z, s, n in zip(keys, zones, sp, parts):
            in_flight[k] = (send[first:first + n], recv[first:first + n], z, s)
            first += n
        return tok

    def arrive_piece(key, k, after):
        send, recv, zone, sp = in_flight[key]
        n = len(send)
        _, zones = _split_wait(f"gather_wait_{key}_{k}", [], [zone], [send[k]], [recv[k]], after,
                               _gather_whole([sp], (k, n)))
        (zone,) = _gather_finish(f"gather_finish_{key}_{k}", zones, (k, n))
        in_flight[key] = (send, recv, zone, sp)
        W[key] = _W4(zone, col_sharded[key])

    def arrive(keys, after):
        send, recv, zones, sp = zip(*[in_flight[k] for k in keys])
        if len(send[0]) > 1:
            for k in range(len(send[0])):
                arrive_piece(keys[0], k, after)
            return
        _, zones = _split_wait(f"gather_wait_{keys[0]}", [], zones, [s[0] for s in send], [r[0] for r in recv],
                               after, _gather_whole(sp))
        relayed = iter(_gather_finish(f"gather_finish_{keys[0]}", [z for z, s in zip(zones, sp) if s]))
        for k, z, s in zip(keys, zones, sp):
            full = next(relayed) if s else z
            W[k] = _W4(full, col_sharded[k]) if k in col_sharded else full

    pieces = dict(w1_0=2, w2_0=2, w1_1=2, w2_1=2)

    token = launch("first", ["a_w_in", "a_w_out"], ())
    token = launch("rest", ["w1_0", "w2_0", "b_scale", "b_w_in", "b_w_grp", "b_w_out", "w1_1", "w2_1"], (token,))

    b_col = a_b_s[0].T

    def residual(acc, res):
        return (res + acc,)

    def sq_relu(acc):
        act = jnp.maximum(acc, 0.0)
        return act, act * act

    def mlp_fwd(tag, h, layer):
        hn = _rms_fwd(f"mlp{tag}_norm", h, norm_mlp[layer:layer + 1])
        arrive([f"w1_{layer}"], hn)
        act, act_sq = _mm_aw(f"mlp{tag}_up", hn, W[f"w1_{layer}"], out_dtypes=(BF16, BF16), epilogue=sq_relu)
        out = h
        for k in range(pieces[f"w2_{layer}"]):
            arrive_piece(f"w2_{layer}", k, act_sq if k == 0 else out)
            out = _mm_aw(f"mlp{tag}_down_{k}", act_sq, W[f"w2_{layer}"], extras=(out,), epilogue=residual,
                         k_piece=(k, pieces[f"w2_{layer}"]))
        return out, (h, hn, act, act_sq)

    hn0 = _rms_fwd("mix_a_norm", x2, norm_mix[0:1])
    arrive(["a_w_in"], token)
    zpre = _mm_aw("mix_a_in", hn0, W["a_w_in"])
    gated = _gate_fwd("mix_a_gate", zpre, a_ln_g, a_ln_b, a_w_s[0], b_col)
    arrive(["a_w_out"], gated)
    h1 = _mm_aw("mix_a_out", gated, W["a_w_out"], extras=(x2,), epilogue=residual)
    h2, mlp0 = mlp_fwd("0", h1, 0)
    hn2 = _rms_fwd("mix_b_norm", h2, norm_mix[1:2])
    arrive(["b_scale", "b_w_in"], hn2)
    scale_full = W["b_scale"].reshape(1, bw)
    vb = _mm_aw("mix_b_in", hn2, W["b_w_in"])
    pooled = _pool_fwd("mix_b_pool", vb)
    arrive(["b_w_grp", "b_w_out"], pooled)
    mixed, ms = _mm_aw("mix_b_grp", pooled, W["b_w_grp"], groups=len(B_WINDOWS), extras=(scale_full,),
                       out_dtypes=(F32, BF16), epilogue=lambda acc, sc: (acc, acc * sc))
    h3 = _mm_aw("mix_b_out", ms, W["b_w_out"], extras=(h2,), epilogue=residual)
    h4, mlp1 = mlp_fwd("1", h3, 1)
    dh4, dh4_b, d_final, loss_part = _final("loss_head", h4, final_norm.reshape(1, -1), tgt)
    g1_like = _W4(None, True, shape=(N_CHIPS, 1, *W["w1_0"].arr.shape[2:]))
    g2_like = _W4(None, False, shape=(N_CHIPS, 1, *W["w2_0"].arr.shape[2:]))

    def exchange(tag, gs):
        zones = [lax.empty((g.shape[0], g.shape[1], g.shape[2] // 2, g.shape[3]), g.dtype) for g in gs]
        send, recv, srcs, zones, tok = _split_start(f"pair_start_{tag}", gs, zones, _pair_plan)
        return (tag, send, recv, srcs, zones), tok

    def reduce(state, after, then_exchange=None):
        tag, send, recv, srcs, zones = state
        srcs, zones = _split_wait(f"pair_wait_{tag}", srcs, zones, send, recv, after, _pair_whole)
        both = [_pair_sum(f"pair_sum_{tag}_{i}", g, o, place) for i, (g, o) in enumerate(zip(srcs, zones))]
        sums, dests = [b[0] for b in both], [b[1] for b in both]
        if then_exchange is None:
            send, recv, sums, dests, tok = _split_start(f"scatter_start_{tag}", sums, dests, _scatter_plan)
            return (tag, send, recv, sums, dests), tok
        other, gs = then_exchange
        n = len(sums)
        zones = [lax.empty((g.shape[0], g.shape[1], g.shape[2] // 2, g.shape[3]), g.dtype) for g in gs]
        send, recv, srcs, lands, tok = _split_start(
            f"scatter_start_{tag}", sums + list(gs), dests + zones,
            lambda s, ld: _scatter_plan(s[:n], ld[:n]) + _pair_plan(s[n:], ld[n:]))
        return ((tag, send[:n], recv[:n], srcs[:n], lands[:n]), tok,
                (other, send[n:], recv[n:], srcs[n:], lands[n:]))

    def relay_all(states, after):
        waited = []
        for tag, send, recv, sums, dests in states:
            waited.append((tag, _split_wait(f"scatter_wait_{tag}", sums, dests, send, recv, after, _scatter_whole)[1]))
        flat = [d for _, ds in waited for d in ds]
        send, recv, _, flat, tok = _split_start(f"relay_start_{waited[0][0]}", [], flat, _relay_plan)
        out, first = [], 0
        for tag, ds in waited:
            last = first + len(ds)
            out.append(((tag, send[first:last], recv[first:last], flat[first:last]), tok))
            first = last
        return out

    def relay(state, after):
        return relay_all([state], after)[0]

    def land(state, after):
        tag, send, recv, dests = state
        return _split_wait(f"relay_wait_{tag}", [], dests, send, recv, after, _relay_whole)[1]

    def mlp_bwd(tag, dh, dh_b, saved, layer, deps, pending=None):
        h, hn, act, act_sq = saved
        dpre = _mm_aw(f"mlp{tag}_down_dx", dh_b, W[f"w2_{layer}"], transpose_w=True, extras=(act,),
                      out_dtypes=(BF16,), epilogue=lambda acc, a: (acc * (2.0 * a),), deps=deps)
        scattering, dw_deps = None, ()
        if pending is not None:
            scattering, tok = reduce(pending, dpre)
            dw_deps = (tok,)
        g_w2 = _mm_dw(f"mlp{tag}_down_dw", act_sq, dh_b, g2_like, deps=dw_deps)
        pair_w2, tok = exchange(f"w2_{layer}", [g_w2])
        dhn = _mm_aw(f"mlp{tag}_up_dx", dpre, W[f"w1_{layer}"], transpose_w=True, deps=(tok,))
        g_w1 = _mm_dw(f"mlp{tag}_up_dw", hn, dpre, g1_like)
        scat_w2, tok2, pair_w1 = reduce(pair_w2, dhn, then_exchange=(f"w1_{layer}", [g_w1]))
        dh_in, dh_in_b, d_norm = _rms_bwd(f"mlp{tag}_norm_bwd", h, norm_mlp[layer:layer + 1], dhn, dh)
        return dh_in, dh_in_b, d_norm, pair_w1, scat_w2, (tok2,), scattering

    dh3, dh3_b, d_norm_mlp1, pair_w1_1, scat_w2_1, toks, _ = mlp_bwd("1", dh4, dh4_b, mlp1, 1, ())
    dms, g_b_out = _mm_bwd("mix_b_out_bwd", dh3_b, W["b_w_out"], ms, deps=toks)
    scat_w1_1, tok = reduce(pair_w1_1, dms)
    dmixed, d_scale = _scale_bwd("mix_b_scale_bwd", dms, mixed, scale_full)
    dpooled = _mm_aw("mix_b_grp_dx", dmixed, W["b_w_grp"], groups=len(B_WINDOWS), transpose_w=True, deps=(tok,))
    g_b_grp = _mm_dw("mix_b_grp_dw", pooled, dmixed, W["b_w_grp"], groups=len(B_WINDOWS))
    dvb = _pool_bwd("mix_b_pool_bwd", dpooled)
    dhn2, g_b_in = _mm_bwd("mix_b_in_bwd", dvb, W["b_w_in"], hn2)
    pair_b, tok = exchange("b", [g_b_out, g_b_grp, g_b_in])
    dh2, dh2_b, d_norm_mix1 = _rms_bwd("mix_b_norm_bwd", h2, norm_mix[1:2], dhn2, dh3)
    dh1, dh1_b, d_norm_mlp0, pair_w1_0, scat_w2_0, toks, scat_b = mlp_bwd("0", dh2, dh2_b, mlp0, 0, (tok,),
                                                                          pending=pair_b)
    dgated, g_a_out = _mm_bwd("mix_a_out_bwd", dh1_b, W["a_w_out"], gated, deps=toks)
    scat_w1_0, tok, pair_a_out = reduce(pair_w1_0, dgated, then_exchange=("a_out", [g_a_out]))
    early = relay_all([scat_w2_1, scat_w1_1, scat_b], dgated)
    dzpre, d_w_s, d_b_col, d_ln_g, d_ln_b = _gate_bwd("mix_a_gate_bwd", zpre, dgated, a_ln_g, a_ln_b, a_w_s[0], b_col)
    dhn0 = _mm_aw("mix_a_in_dx", dzpre, W["a_w_in"], transpose_w=True, deps=(tok, early[0][1]))
    scat_a_out, tok = reduce(pair_a_out, dhn0)
    g_a_in = _mm_dw("mix_a_in_dw", hn0, dzpre, W["a_w_in"], deps=(tok,))
    pair_a_in, tok = exchange("a_in", [g_a_in])
    dx, _, d_norm_mix0 = _rms_bwd("mix_a_norm_bwd", x2, norm_mix[0:1], dhn0, dh1)
    scat_a_in, _ = reduce(pair_a_in, dx)

    small = dict(a_ln_g=(a_ln_g, m_a_ln_g, v_a_ln_g), a_ln_b=(a_ln_b, m_a_ln_b, v_a_ln_b),
                 a_w_s=(a_w_s, m_a_w_s, v_a_w_s), a_b_s=(a_b_s, m_a_b_s, v_a_b_s),
                 b_scale=(b_scale, m_b_scale, v_b_scale), norm_mix=(norm_mix, m_norm_mix, v_norm_mix),
                 norm_mlp=(norm_mlp, m_norm_mlp, v_norm_mlp), final_norm=(final_norm, m_final_norm, v_final_norm))
    small_names = list(small)
    local = dict(a_ln_g=d_ln_g, a_ln_b=d_ln_b, a_w_s=d_w_s[None], a_b_s=d_b_col.T[None], b_scale=d_scale,
                 norm_mix=jnp.concatenate([d_norm_mix0, d_norm_mix1], axis=0),
                 norm_mlp=jnp.concatenate([d_norm_mlp0, d_norm_mlp1], axis=0), final_norm=d_final.reshape(-1))
    device = (4 * xi + 2 * yi + ci).astype(jnp.int32).reshape(1)
    slots = _own_of_eight("small_own", _pack([local[k] for k in small_names]), device)
    small_send, small_recv, _, (slots,), small_tok = _split_start("small_start", [], [slots], _all_plan)

    moments = dict(a_w_in=(m_a_w_in, v_a_w_in), a_w_out=(m_a_w_out, v_a_w_out), b_w_in=(m_b_w_in, v_b_w_in),
                   b_w_grp=(m_b_w_grp, v_b_w_grp), b_w_out=(m_b_w_out, v_b_w_out),
                   mlp_w1=(m_mlp_w1, v_mlp_w1), mlp_w2=(m_mlp_w2, v_mlp_w2))
    weights = dict(a_w_in=a_w_in, a_w_out=a_w_out, b_w_in=b_w_in, b_w_grp=b_w_grp, b_w_out=b_w_out,
                   mlp_w1=mlp_w1, mlp_w2=mlp_w2)
    landing = [(scat_w2_1, [("mlp_w2", 1)]), (scat_w1_1, [("mlp_w1", 1)]),
               (scat_b, [("b_w_out", 0), ("b_w_grp", 0), ("b_w_in", 0)]),
               (scat_w2_0, [("mlp_w2", 0)]), (scat_w1_0, [("mlp_w1", 0)]),
               (scat_a_out, [("a_w_out", 0)]), (scat_a_in, [("a_w_in", 0)])]
    results, after = {}, dx
    relays = list(early)
    for i, (_, members) in enumerate(landing):
        deps = (small_tok,) if i == 0 else ()
        if len(relays) == i + 1 < len(landing):
            relays.append(relay(landing[i + 1][0], after))
            deps = (relays[-1][1],)
        for (k, layer), parts in zip(members, land(relays[i][0], relays[-1][1] if deps else after)):
            shard_shape = (-1, *parts.shape[2:])
            results[k] = _adam_shard(f"adam_{k}_{layer}", weights[k].reshape(shard_shape),
                                     moments[k][0].reshape(shard_shape), moments[k][1].reshape(shard_shape),
                                     parts, layer=layer, prev=results.get(k), deps=deps)
            after = results[k][1]
    grad_out, delta_out, m_out, v_out = {}, {}, {}, {}
    for k, res in results.items():
        grad_out[k], delta_out[k], m_out[k], v_out[k] = [r.reshape(weights[k].shape) for r in res]

    _, (slots,) = _split_wait("small_wait", [], [slots], small_send, small_recv, after, _all_whole)
    reduced = _sum_of_eight("small_sum", slots)
    small_grads = dict(zip(small_names, _unpack(reduced, [local[k].shape for k in small_names])))
    shard_w = b_scale.shape[1]
    small_grads["b_scale"] = lax.dynamic_slice_in_dim(small_grads["b_scale"], chip * shard_w, shard_w, axis=1)
    small_grads = {k: small_grads[k].reshape(small[k][0].shape) for k in small_names}
    packed = [_pack([small[k][i] for k in small_names]) for i in range(3)]
    res = _adam_packed("adam_small", packed[0], _pack([small_grads[k] for k in small_names]), packed[1], packed[2])
    shapes = [small[k][0].shape for k in small_names]
    for k, d, nm, nv in zip(small_names, *[_unpack(r, shapes) for r in res]):
        grad_out[k], delta_out[k], m_out[k], v_out[k] = small_grads[k], d, nm, nv

    loss = lax.psum(loss_part[0, 0], ("x", "y", "c"))
    order = ["a_w_in", "a_ln_g", "a_ln_b", "a_w_s", "a_b_s", "a_w_out", "b_w_in", "b_w_grp", "b_scale", "b_w_out",
             "norm_mix", "norm_mlp", "mlp_w1", "mlp_w2", "final_norm"]
    return (loss, dx[None], *[grad_out[k] for k in order], *[delta_out[k] for k in order],
            *[m_out[k] for k in order], *[v_out[k] for k in order])
```

```python
import math

import jax
import jax.numpy as jnp
from jax import lax
from jax.experimental import pallas as pl
from jax.experimental.pallas import tpu as pltpu

F32 = jnp.float32
BF16 = jnp.bfloat16
MESH = pl.DeviceIdType.MESH

EPS = 1e-6
B_WINDOWS = (2, 4, 8, 16)
ADAM_LR = 0.001
ADAM_B1 = 0.9
ADAM_B2 = 0.999
ADAM_EPS = 1e-08
ADAM_WD = 0.01
ADAM_STEP = 10

N_CHIPS = 4
N_DEV = 8
LANES = 128
PACK_ELEMS = 8 * LANES
VMEM_LIMIT = 56 * 1024 * 1024
ROW_TILE = 512
MM_TM, MM_TN, MM_TK = 1024, 1024, 2048


_ANY = pl.BlockSpec(memory_space=pl.ANY)


def _tile(dim, pref):
    t = min(dim, pref)
    while dim % t:
        t //= 2
    return t


def _params(*sem):
    return pltpu.CompilerParams(dimension_semantics=sem, vmem_limit_bytes=VMEM_LIMIT)


class _W4:
    def __init__(self, arr, col_sharded, shape=None):
        self.arr = arr
        self.nj, self.nl, self.r, self.c = arr.shape if shape is None else shape
        self.col = col_sharded
        self.rows = self.r if col_sharded else self.nj * self.r
        self.cols = self.nj * self.c if col_sharded else self.c

    def tile_rows(self, pref):
        return _tile(self.r, pref)

    def tile_cols(self, pref):
        return _tile(self.c, pref)

    def index(self, layer, rb, cb, tr, tc):
        if self.col:
            n = self.c // tc
            return (cb // n, layer, rb, cb % n)
        n = self.r // tr
        return (rb // n, layer, rb % n, cb)


def _mm_aw(name, a, w, *, layer=0, groups=1, transpose_w=False, extras=(), out_dtypes=(F32,), epilogue=None,
           deps=(), k_piece=None):
    s, ka_total = a.shape
    kdim, ndim = (w.cols, w.rows) if transpose_w else (w.rows, w.cols)
    assert ka_total == groups * kdim, (name, a.shape, kdim, groups)
    span = not w.col and ((not transpose_w and kdim <= MM_TK) or (transpose_w and groups > 1))
    across = transpose_w and w.col and groups == 1 and MM_TK < kdim <= 2 * MM_TK
    if span and transpose_w:
        tm, tn, tk = _tile(s, 2048), ndim, w.tile_cols(MM_TK)
    elif across:
        tm, tn, tk = _tile(s, MM_TM), w.tile_rows(512), kdim
    else:
        tk = kdim if span else (w.tile_cols(MM_TK) if transpose_w else w.tile_rows(MM_TK))
        tm, tn_pref = (_tile(s, 2048), 512) if tk == kdim else (_tile(s, MM_TM), MM_TN)
        tn = w.tile_rows(tn_pref) if transpose_w else w.tile_cols(tn_pref)
    nk, nn = kdim // tk, ndim // tn
    if k_piece is not None:
        assert not (w.col or transpose_w or span or groups > 1), name
        piece, n_pieces = k_piece
        tm, tn, tk = _tile(s, MM_TM), w.tile_cols(MM_TN), w.r // (2 * n_pieces)
        nk, nn = 2 * w.nj, ndim // tn

        def k_block(k):
            return k // 2, (k % 2) * n_pieces + piece

    def lay(g):
        return g if groups > 1 else layer

    a_spec = pl.BlockSpec((tm, tk), lambda g, i, n, k: (i, g * nk + k))
    if k_piece is not None:
        per = w.r // tk
        a_spec = pl.BlockSpec((tm, tk), lambda g, i, n, k: (i, k_block(k)[0] * per + k_block(k)[1]))
        w_spec = pl.BlockSpec((None, None, tk, tn), lambda g, i, n, k: (k_block(k)[0], layer, k_block(k)[1], n))
    elif span and transpose_w:
        w_spec = pl.BlockSpec((w.nj, None, w.r, tk), lambda g, i, n, k: (0, lay(g), 0, k))
    elif span:
        w_spec = pl.BlockSpec((w.nj, None, w.r, tn), lambda g, i, n, k: (0, lay(g), 0, n))
    elif across:
        w_spec = pl.BlockSpec((w.nj, None, tn, w.c), lambda g, i, n, k: (0, layer, n, 0))
    elif transpose_w:
        w_spec = pl.BlockSpec((None, None, tn, tk), lambda g, i, n, k: w.index(lay(g), n, k, tn, tk))
    else:
        w_spec = pl.BlockSpec((None, None, tk, tn), lambda g, i, n, k: w.index(lay(g), k, n, tk, tn))
    ex_specs = []
    for e in extras:
        assert e.shape[1] == groups * ndim and e.shape[0] in (1, s), (name, e.shape)
        if e.shape[0] == 1:
            ex_specs.append(pl.BlockSpec((1, tn), lambda g, i, n, k: (0, g * nn + n)))
        else:
            ex_specs.append(pl.BlockSpec((tm, tn), lambda g, i, n, k: (i, g * nn + n)))
    out_spec = pl.BlockSpec((tm, tn), lambda g, i, n, k: (i, g * nn + n))
    n_ex, n_out, n_dep = len(extras), len(out_dtypes), len(deps)

    def body(a_ref, w_ref, *rest):
        ex, outs = rest[:n_ex], rest[n_ex + n_dep:n_ex + n_dep + n_out]
        av = a_ref[...]
        if av.dtype != BF16:
            av = av.astype(BF16)
        nt = (((1,), (1,)), ((), ()))
        if across:
            prod = lax.dot_general(av[:, :w.c], w_ref[0], nt, preferred_element_type=F32)
            for j in range(1, w.nj):
                prod += lax.dot_general(av[:, j * w.c:(j + 1) * w.c], w_ref[j], nt, preferred_element_type=F32)
        else:
            wv = w_ref[...]
            if span:
                wv = wv.reshape((tn, tk) if transpose_w else (tk, tn))
            prod = lax.dot_general(av, wv, nt, preferred_element_type=F32) if transpose_w else jnp.dot(
                av, wv, preferred_element_type=F32)

        def finish(total):
            vals = (total,) if epilogue is None else epilogue(total, *[e[...] for e in ex])
            for o, v in zip(outs, vals):
                o[...] = v.astype(o.dtype)

        if nk == 1:
            finish(prod)
            return
        acc, k = rest[-1], pl.program_id(3)

        @pl.when(k == 0)
        def _():
            acc[...] = prod

        @pl.when(k > 0)
        def _():
            acc[...] += prod

        @pl.when(k == nk - 1)
        def _():
            finish(acc[...])

    outs = pl.pallas_call(
        body, name=name, grid=(groups, s // tm, nn, nk),
        in_specs=[a_spec, w_spec, *ex_specs] + [_ANY] * n_dep, out_specs=[out_spec] * n_out,
        out_shape=[jax.ShapeDtypeStruct((s, groups * ndim), dt) for dt in out_dtypes],
        scratch_shapes=[pltpu.VMEM((tm, tn), F32)] if nk > 1 else [],
        compiler_params=_params("parallel", "parallel", "parallel", "arbitrary"),
    )(a, w.arr, *extras, *deps)
    return outs[0] if n_out == 1 else outs


def _mm_dw(name, a, b, like, *, layer=0, groups=1, deps=()):
    s, ka_total = a.shape
    rows, cols = ka_total // groups, b.shape[1] // groups
    assert (rows, cols) == (like.rows, like.cols) and b.shape[0] == s, (name, a.shape, b.shape)
    span = groups > 1 and not like.col
    tm, tn, tk = rows if span else like.tile_rows(MM_TM), like.tile_cols(2 * MM_TN), _tile(s, MM_TK)
    nr, nc, nk = rows // tm, cols // tn, s // tk
    assert nk == 1 or not span

    def lay(g):
        return g if groups > 1 else layer

    in_specs = [pl.BlockSpec((tk, tm), lambda g, n, i, k: (k, g * nr + i)),
                pl.BlockSpec((tk, tn), lambda g, n, i, k: (k, g * nc + n))]
    in_specs += [_ANY] * len(deps)

    def body(a_ref, b_ref, *rest):
        av, bv = a_ref[...], b_ref[...]
        if av.dtype != BF16:
            av = av.astype(BF16)
        if bv.dtype != BF16:
            bv = bv.astype(BF16)
        prod = lax.dot_general(av, bv, (((0,), (0,)), ((), ())), preferred_element_type=F32)
        if nk == 1:
            rest[-1][...] = prod.astype(BF16).reshape(rest[-1].shape)
            return
        o_ref, acc, k = rest[-2], rest[-1], pl.program_id(3)

        @pl.when(k == 0)
        def _():
            acc[...] = prod

        @pl.when(k > 0)
        def _():
            acc[...] += prod

        @pl.when(k == nk - 1)
        def _():
            o_ref[...] = acc[...].astype(BF16)

    if span:
        out_spec = pl.BlockSpec((like.nj, None, like.r, tn), lambda g, n, i, k: (0, g, 0, n))
    else:
        out_spec = pl.BlockSpec((None, None, tm, tn), lambda g, n, i, k: like.index(lay(g), i, n, tm, tn))
    return pl.pallas_call(
        body, name=name, grid=(groups, nc, nr, nk), in_specs=in_specs, out_specs=out_spec,
        out_shape=jax.ShapeDtypeStruct((like.nj, like.nl, like.r, like.c), BF16),
        scratch_shapes=[pltpu.VMEM((tm, tn), F32)] if nk > 1 else [],
        compiler_params=_params("parallel", "parallel", "parallel", "arbitrary"),
    )(a, b, *deps)


def _mm_bwd(name, dy, w, a, *, deps=()):
    s, c = dy.shape
    assert not w.col and w.nl == 1 and w.c == c and a.shape == (s, w.rows), (name, dy.shape, a.shape)
    r = w.r

    def body(dy_ref, w_ref, a_ref, *rest):
        dx_ref, dw_ref = rest[-2:]
        dyv = dy_ref[...]
        dx_ref[...] = lax.dot_general(dyv, w_ref[...], (((1,), (1,)), ((), ())), preferred_element_type=F32)
        dw_ref[...] = lax.dot_general(a_ref[...], dyv, (((0,), (0,)), ((), ())),
                                      preferred_element_type=F32).astype(BF16)

    return pl.pallas_call(
        body, name=name, grid=(w.nj,),
        in_specs=[pl.BlockSpec((s, c), lambda j: (0, 0)), pl.BlockSpec((None, None, r, c), lambda j: (j, 0, 0, 0)),
                  pl.BlockSpec((s, r), lambda j: (0, j))] + [_ANY] * len(deps),
        out_specs=[pl.BlockSpec((s, r), lambda j: (0, j)), pl.BlockSpec((None, None, r, c), lambda j: (j, 0, 0, 0))],
        out_shape=[jax.ShapeDtypeStruct((s, w.rows), F32), jax.ShapeDtypeStruct((w.nj, 1, r, c), BF16)],
        compiler_params=_params("parallel"),
    )(dy, w.arr, a, *deps)


def _row_spec(tr, d):
    return pl.BlockSpec((tr, d), lambda i: (i, 0))


def _vec_spec(d):
    return pl.BlockSpec((1, d), lambda i: (0, 0))


def _rms_fwd(name, x, g):
    s, d = x.shape
    tr = _tile(s, ROW_TILE)

    def body(x_ref, g_ref, o_ref):
        xv = x_ref[...]
        r = lax.rsqrt(jnp.mean(xv * xv, axis=-1, keepdims=True) + EPS)
        o_ref[...] = (xv * r * g_ref[...]).astype(BF16)

    return pl.pallas_call(
        body, name=name, grid=(s // tr,), in_specs=[_row_spec(tr, d), _vec_spec(d)],
        out_specs=_row_spec(tr, d), out_shape=jax.ShapeDtypeStruct((s, d), BF16),
        compiler_params=_params("parallel"),
    )(x, g)


def _rms_bwd(name, x, g, dhn, dres):
    s, d = x.shape
    tr = _tile(s, ROW_TILE)

    def body(x_ref, g_ref, dhn_ref, dres_ref, dx_ref, dxb_ref, dg_ref):
        @pl.when(pl.program_id(0) == 0)
        def _():
            dg_ref[...] = jnp.zeros_like(dg_ref)

        xv = x_ref[...]
        r = lax.rsqrt(jnp.mean(xv * xv, axis=-1, keepdims=True) + EPS)
        xh = xv * r
        dy = dhn_ref[...]
        dg_ref[...] += jnp.sum(dy * xh, axis=0, keepdims=True)
        dxh = dy * g_ref[...]
        dx = dres_ref[...] + r * (dxh - xh * jnp.mean(dxh * xh, axis=-1, keepdims=True))
        dx_ref[...] = dx
        dxb_ref[...] = dx.astype(BF16)

    return pl.pallas_call(
        body, name=name, grid=(s // tr,),
        in_specs=[_row_spec(tr, d), _vec_spec(d), _row_spec(tr, d), _row_spec(tr, d)],
        out_specs=[_row_spec(tr, d), _row_spec(tr, d), _vec_spec(d)],
        out_shape=[jax.ShapeDtypeStruct((s, d), F32), jax.ShapeDtypeStruct((s, d), BF16),
                   jax.ShapeDtypeStruct((1, d), F32)],
        compiler_params=_params("arbitrary"),
    )(x, g, dhn, dres)


def _loss_head_tile(hv, gv, tv):
    r = lax.rsqrt(jnp.mean(hv * hv, axis=-1, keepdims=True) + EPS)
    xh = hv * r
    err = xh * gv - tv
    part = 0.5 * jnp.sum(jnp.mean(err * err, axis=-1, keepdims=True), axis=0, keepdims=True)
    dy = err * (1.0 / hv.shape[-1])
    dxh = dy * gv
    dh = r * (dxh - xh * jnp.mean(dxh * xh, axis=-1, keepdims=True))
    return part, jnp.sum(dy * xh, axis=0, keepdims=True), dh


def _down_tail_loss(name, a, w, partial, g, tgt, k_piece):
    s, d = partial.shape
    piece, n_pieces = k_piece
    assert not w.col and w.nl == 1 and w.cols == d and a.shape == (s, w.rows), (name, a.shape)
    tm, tk = _tile(s, 512), w.r // (2 * n_pieces)
    per, nk = w.r // tk, 2 * w.nj

    def k_block(k):
        return k // 2, (k % 2) * n_pieces + piece

    def body(a_ref, w_ref, p_ref, g_ref, t_ref, dh_ref, dhb_ref, dg_ref, loss_ref, acc):
        i, k = pl.program_id(0), pl.program_id(1)

        @pl.when(jnp.logical_and(i == 0, k == 0))
        def _():
            dg_ref[...] = jnp.zeros_like(dg_ref)
            loss_ref[...] = jnp.zeros_like(loss_ref)

        prod = jnp.dot(a_ref[...], w_ref[...], preferred_element_type=F32)

        @pl.when(k == 0)
        def _():
            acc[...] = prod

        @pl.when(k > 0)
        def _():
            acc[...] += prod

        @pl.when(k == nk - 1)
        def _():
            part, dg, dh = _loss_head_tile(p_ref[...] + acc[...], g_ref[...], t_ref[...])
            loss_ref[...] += jnp.broadcast_to(part, loss_ref.shape)
            dg_ref[...] += dg
            dh_ref[...] = dh
            dhb_ref[...] = dh.astype(BF16)

    rows = pl.BlockSpec((tm, d), lambda i, k: (i, 0))
    return pl.pallas_call(
        body, name=name, grid=(s // tm, nk),
        in_specs=[pl.BlockSpec((tm, tk), lambda i, k: (i, k_block(k)[0] * per + k_block(k)[1])),
                  pl.BlockSpec((None, None, tk, d), lambda i, k: (k_block(k)[0], 0, k_block(k)[1], 0)),
                  rows, pl.BlockSpec((1, d), lambda i, k: (0, 0)), rows],
        out_specs=[rows, rows, pl.BlockSpec((1, d), lambda i, k: (0, 0)), pl.BlockSpec((1, LANES), lambda i, k: (0, 0))],
        out_shape=[jax.ShapeDtypeStruct((s, d), F32), jax.ShapeDtypeStruct((s, d), BF16),
                   jax.ShapeDtypeStruct((1, d), F32), jax.ShapeDtypeStruct((1, LANES), F32)],
        scratch_shapes=[pltpu.VMEM((tm, d), F32)],
        compiler_params=_params("arbitrary", "arbitrary"),
    )(a, w.arr, partial, g, tgt)


_SQRT_HALF = 1.0 / math.sqrt(2.0)
_INV_SQRT_2PI = 1.0 / math.sqrt(2.0 * math.pi)


def _gelu(x):
    return x * (lax.erf(x * _SQRT_HALF) + 1.0) * 0.5


def _gelu_grad(x):
    return 0.5 * (lax.erf(x * _SQRT_HALF) + 1.0) + x * jnp.exp(-0.5 * x * x) * _INV_SQRT_2PI


def _causal(chunk):
    t = lax.broadcasted_iota(jnp.int32, (chunk, chunk), 0)
    sidx = lax.broadcasted_iota(jnp.int32, (chunk, chunk), 1)
    return sidx <= t


def _layernorm_parts(v, g, b):
    mu = jnp.mean(v, axis=-1, keepdims=True)
    vc = v - mu
    rs = lax.rsqrt(jnp.mean(vc * vc, axis=-1, keepdims=True) + EPS)
    vhat = vc * rs
    return vhat, rs, vhat * g + b


def _gate_fwd(name, zpre, ln_g, ln_b, w_s, b_col):
    s, aw2 = zpre.shape
    aw = aw2 // 2
    ng, chunk, _ = w_s.shape
    dh = aw // ng

    def body(z_ref, g_ref, b_ref, ws_ref, bc_ref, o_ref):
        u = _gelu(z_ref[:, :aw])
        v = _gelu(z_ref[:, aw:])
        _, _, vln = _layernorm_parts(v, g_ref[...], b_ref[...])
        mask = _causal(chunk)
        for gi in range(ng):
            sl = slice(gi * dh, (gi + 1) * dh)
            wm = jnp.where(mask, ws_ref[gi], 0.0).astype(BF16)
            sg = jnp.dot(wm, vln[:, sl].astype(BF16), preferred_element_type=F32) + bc_ref[:, gi:gi + 1]
            o_ref[:, sl] = (u[:, sl] * sg).astype(BF16)

    return pl.pallas_call(
        body, name=name, grid=(s // chunk,),
        in_specs=[_row_spec(chunk, aw2), _vec_spec(aw), _vec_spec(aw),
                  pl.BlockSpec((ng, chunk, chunk), lambda i: (0, 0, 0)),
                  pl.BlockSpec((chunk, ng), lambda i: (0, 0))],
        out_specs=_row_spec(chunk, aw), out_shape=jax.ShapeDtypeStruct((s, aw), BF16),
        compiler_params=_params("parallel"),
    )(zpre, ln_g, ln_b, w_s, b_col)


def _gate_bwd(name, zpre, dgated, ln_g, ln_b, w_s, b_col):
    s, aw2 = zpre.shape
    aw = aw2 // 2
    ng, chunk, _ = w_s.shape
    dh = aw // ng

    def body(z_ref, dgt_ref, g_ref, b_ref, ws_ref, bc_ref, dz_ref, dws_ref, dbc_ref, dlg_ref, dlb_ref,
             du_scr, dvln_scr):
        @pl.when(pl.program_id(0) == 0)
        def _():
            dws_ref[...] = jnp.zeros_like(dws_ref)
            dbc_ref[...] = jnp.zeros_like(dbc_ref)
            dlg_ref[...] = jnp.zeros_like(dlg_ref)
            dlb_ref[...] = jnp.zeros_like(dlb_ref)

        zu = z_ref[:, :aw]
        zv = z_ref[:, aw:]
        u = _gelu(zu)
        lg = g_ref[...]
        vhat, rs, vln = _layernorm_parts(_gelu(zv), lg, b_ref[...])
        mask = _causal(chunk)
        for gi in range(ng):
            sl = slice(gi * dh, (gi + 1) * dh)
            wm = jnp.where(mask, ws_ref[gi], 0.0).astype(BF16)
            vg = vln[:, sl].astype(BF16)
            sg = jnp.dot(wm, vg, preferred_element_type=F32) + bc_ref[:, gi:gi + 1]
            dgt = dgt_ref[:, sl]
            du_scr[:, sl] = dgt * sg
            ds = dgt * u[:, sl]
            dbc_ref[:, gi:gi + 1] += jnp.sum(ds, axis=-1, keepdims=True)
            dsb = ds.astype(BF16)
            dwm = lax.dot_general(dsb, vg, (((1,), (1,)), ((), ())), preferred_element_type=F32)
            dws_ref[gi] += jnp.where(mask, dwm, 0.0)
            dvln_scr[:, sl] = lax.dot_general(wm, dsb, (((0,), (0,)), ((), ())), preferred_element_type=F32)
        dvln = dvln_scr[...]
        dlb_ref[...] += jnp.sum(dvln, axis=0, keepdims=True)
        dlg_ref[...] += jnp.sum(dvln * vhat, axis=0, keepdims=True)
        dvh = dvln * lg
        dv = rs * (dvh - jnp.mean(dvh, axis=-1, keepdims=True)
                   - vhat * jnp.mean(dvh * vhat, axis=-1, keepdims=True))
        dz_ref[:, :aw] = (du_scr[...] * _gelu_grad(zu)).astype(BF16)
        dz_ref[:, aw:] = (dv * _gelu_grad(zv)).astype(BF16)

    return pl.pallas_call(
        body, name=name, grid=(s // chunk,),
        in_specs=[_row_spec(chunk, aw2), _row_spec(chunk, aw), _vec_spec(aw), _vec_spec(aw),
                  pl.BlockSpec((ng, chunk, chunk), lambda i: (0, 0, 0)),
                  pl.BlockSpec((chunk, ng), lambda i: (0, 0))],
        out_specs=[_row_spec(chunk, aw2), pl.BlockSpec((ng, chunk, chunk), lambda i: (0, 0, 0)),
                   pl.BlockSpec((chunk, ng), lambda i: (0, 0)), _vec_spec(aw), _vec_spec(aw)],
        out_shape=[jax.ShapeDtypeStruct((s, aw2), BF16), jax.ShapeDtypeStruct((ng, chunk, chunk), F32),
                   jax.ShapeDtypeStruct((chunk, ng), F32), jax.ShapeDtypeStruct((1, aw), F32),
                   jax.ShapeDtypeStruct((1, aw), F32)],
        scratch_shapes=[pltpu.VMEM((chunk, aw), F32), pltpu.VMEM((chunk, aw), F32)],
        compiler_params=_params("arbitrary"),
    )(zpre, dgated, ln_g, ln_b, w_s, b_col)


def _pool_select(g, parts):
    out = parts[-1]
    for gi in range(len(parts) - 2, -1, -1):
        out = jnp.where(g == gi, parts[gi], out)
    return out


def _pool_specs(s, bw):
    head = bw // len(B_WINDOWS)
    tc = _tile(head, 256)
    nb = head // tc
    return tc, (len(B_WINDOWS), nb), pl.BlockSpec((s, tc), lambda g, j: (0, g * nb + j))


def _pool_window(g, t):
    w = _pool_select(g, [jnp.full(t.shape, wi, jnp.int32) for wi in B_WINDOWS])
    return jnp.minimum(t + 1, w).astype(F32)


def _pool_fwd(name, vb):
    assert B_WINDOWS == (2, 4, 8, 16)
    s, bw = vb.shape
    tc, grid, spec = _pool_specs(s, bw)

    def body(v_ref, o_ref):
        g = pl.program_id(0)
        v = v_ref[...]
        t = lax.broadcasted_iota(jnp.int32, (s, tc), 0)

        def down(x, k):
            return jnp.where(t >= k, pltpu.roll(x, k, 0), 0.0)

        sums, cur, k = [], v, 1
        for _ in B_WINDOWS:
            cur = cur + down(cur, k)
            sums.append(cur)
            k *= 2
        o_ref[...] = (_pool_select(g, sums) / _pool_window(g, t) - v).astype(BF16)

    return pl.pallas_call(
        body, name=name, grid=grid, in_specs=[spec], out_specs=spec,
        out_shape=jax.ShapeDtypeStruct((s, bw), BF16), compiler_params=_params("parallel", "parallel"),
    )(vb)


def _pool_bwd(name, dpooled):
    s, bw = dpooled.shape
    tc, grid, spec = _pool_specs(s, bw)

    def body(d_ref, o_ref):
        g = pl.program_id(0)
        dp = d_ref[...]
        t = lax.broadcasted_iota(jnp.int32, (s, tc), 0)

        def up(x, k):
            return jnp.where(t < s - k, pltpu.roll(x, s - k, 0), 0.0)

        sums, cur, k = [], dp / _pool_window(g, t), 1
        for _ in B_WINDOWS:
            cur = cur + up(cur, k)
            sums.append(cur)
            k *= 2
        o_ref[...] = (_pool_select(g, sums) - dp).astype(BF16)

    return pl.pallas_call(
        body, name=name, grid=grid, in_specs=[spec], out_specs=spec,
        out_shape=jax.ShapeDtypeStruct((s, bw), BF16), compiler_params=_params("parallel", "parallel"),
    )(dpooled)


def _scale_bwd(name, dms, mixed, scale):
    s, bw = dms.shape
    tr = _tile(s, ROW_TILE)

    def body(d_ref, m_ref, sc_ref, o_ref, ds_ref):
        @pl.when(pl.program_id(0) == 0)
        def _():
            ds_ref[...] = jnp.zeros_like(ds_ref)

        dv = d_ref[...]
        ds_ref[...] += jnp.sum(dv * m_ref[...], axis=0, keepdims=True)
        o_ref[...] = (dv * sc_ref[...]).astype(BF16)

    return pl.pallas_call(
        body, name=name, grid=(s // tr,), in_specs=[_row_spec(tr, bw), _row_spec(tr, bw), _vec_spec(bw)],
        out_specs=[_row_spec(tr, bw), _vec_spec(bw)],
        out_shape=[jax.ShapeDtypeStruct((s, bw), BF16), jax.ShapeDtypeStruct((1, bw), F32)],
        compiler_params=_params("arbitrary"),
    )(dms, mixed, scale)


def _adam_update(w, g, m, v):
    m = ADAM_B1 * m + (1.0 - ADAM_B1) * g
    v = ADAM_B2 * v + (1.0 - ADAM_B2) * (g * g)
    m_hat = m / (1.0 - ADAM_B1 ** ADAM_STEP)
    v_hat = v / (1.0 - ADAM_B2 ** ADAM_STEP)
    delta = -ADAM_LR * (m_hat / (jnp.sqrt(v_hat) + ADAM_EPS) + ADAM_WD * w)
    return delta, m, v


def _adam_shard(name, w, m, v, parts, layer=0, prev=None, deps=()):
    nl, r, c = w.shape
    nj, nlp = parts.shape[:2]
    tr = _tile(r, ROW_TILE // 2)
    spec = pl.BlockSpec((None, tr, c), lambda l, i: (layer + l, i, 0))

    def body(w_ref, m_ref, v_ref, p_ref, *rest):
        g_ref, d_ref, nm_ref, nv_ref = rest[-4:]
        g = p_ref[0].astype(F32)
        for j in range(1, nj):
            g = g + p_ref[j].astype(F32)
        delta, nm, nv = _adam_update(w_ref[...], g, m_ref[...], v_ref[...])
        g_ref[...] = g
        d_ref[...] = delta
        nm_ref[...] = nm
        nv_ref[...] = nv

    prev = () if prev is None else tuple(prev)
    return pl.pallas_call(
        body, name=name, grid=(nlp, r // tr),
        in_specs=[spec, spec, spec, pl.BlockSpec((nj, None, tr, c), lambda l, i: (0, l, i, 0))]
        + [_ANY] * (len(prev) + len(deps)),
        out_specs=[spec] * 4, out_shape=[jax.ShapeDtypeStruct(w.shape, F32)] * 4,
        input_output_aliases={4 + i: i for i in range(len(prev))},
        compiler_params=_params("parallel", "parallel"),
    )(w, m, v, parts, *prev, *deps)


def _adam_packed(name, w, g, m, v):
    rows, lanes = w.shape
    tr = rows
    spec = pl.BlockSpec((tr, lanes), lambda i: (i, 0))

    def body(w_ref, g_ref, m_ref, v_ref, d_ref, nm_ref, nv_ref):
        delta, nm, nv = _adam_update(w_ref[...], g_ref[...], m_ref[...], v_ref[...])
        d_ref[...] = delta
        nm_ref[...] = nm
        nv_ref[...] = nv

    return pl.pallas_call(
        body, name=name, grid=(rows // tr,), in_specs=[spec] * 4, out_specs=[spec] * 3,
        out_shape=[jax.ShapeDtypeStruct(w.shape, F32)] * 3, compiler_params=_params("parallel"),
    )(w, g, m, v)


def _place():
    x, y, c = lax.axis_index("x"), lax.axis_index("y"), lax.axis_index("c")
    chips = [(1 - x, y), (x, 1 - y), (1 - x, 1 - y)]
    return x, y, c, chips


def _remote(src, dst, send_sem, recv_sem, device):
    return pltpu.make_async_remote_copy(src_ref=src, dst_ref=dst, send_sem=send_sem, recv_sem=recv_sem,
                                        device_id=device, device_id_type=MESH)


def _half(ref_rows, cc):
    h = ref_rows // 2
    return pl.ds(cc * h, h)


_HBM = pl.BlockSpec(memory_space=pltpu.HBM)
_SEM = pl.BlockSpec(memory_space=pltpu.SEMAPHORE)
_EFFECT = pltpu.SideEffectType.DATAFLOW_SIDE_EFFECTING


def _in_hbm(arr):
    return pltpu.with_memory_space_constraint(arr, pltpu.HBM)


def _gather_rows(land, cc, part=(0, 1)):
    k, n = part
    h = land.shape[2] // 2
    return pl.ds(cc * h + k * (h // n), h // n)


def _gather_block(land, split, j, cc, part=(0, 1)):
    if split:
        return land.at[j, :, _gather_rows(land, cc, part), :]
    return land.at[j]


def _split_start(name, srcs, lands, plan, deps=(), groups=None):
    ns = len(srcs)
    nl = len(lands) if groups is None else groups
    both = list(srcs) + list(lands)
    nb = len(both)

    def body(*refs):
        s, ld = refs[:ns], refs[ns:nb]
        outs = refs[nb + len(deps):]
        send, recv, token = outs[:nl], outs[nl:2 * nl], outs[-1]
        for a, copies in enumerate(plan(s, ld)):
            for src, dst, peer in copies:
                _remote(src, dst, send[a], recv[a], peer).start()
        token[...] = jnp.zeros_like(token)

    outs = pl.pallas_call(
        body, name=name, in_specs=[_HBM] * nb + [_ANY] * len(deps),
        out_specs=[_SEM] * (2 * nl) + [_HBM] * nb + [pl.BlockSpec(memory_space=pltpu.VMEM)],
        out_shape=[pltpu.SemaphoreType.DMA(())] * (2 * nl) + [pltpu.HBM(b.shape, b.dtype) for b in both]
        + [jax.ShapeDtypeStruct((8, LANES), F32)],
        input_output_aliases={i: 2 * nl + i for i in range(nb)},
        compiler_params=pltpu.CompilerParams(has_side_effects=_EFFECT),
    )(*[_in_hbm(b) for b in both], *deps)
    thru = outs[2 * nl:2 * nl + nb]
    return list(outs[:nl]), list(outs[nl:2 * nl]), list(thru[:ns]), list(thru[ns:]), outs[-1]


def _split_wait(name, srcs, lands, send, recv, after, whole):
    ns, nl = len(srcs), len(send)
    both = list(srcs) + list(lands)
    nb = len(both)

    def body(*refs):
        ld, snd, rcv = refs[ns:nb], refs[nb:nb + nl], refs[nb + nl:nb + 2 * nl]
        x, y, c, _ = _place()
        for a, blk in enumerate(whole(ld)):
            every = _remote(blk, blk, snd[a], rcv[a], (x, y, c))
            every.wait_send()
            every.wait_recv()

    outs = pl.pallas_call(
        body, name=name, in_specs=[_HBM] * nb + [_SEM] * (2 * nl) + [_ANY], out_specs=[_HBM] * nb,
        out_shape=[pltpu.HBM(b.shape, b.dtype) for b in both],
        input_output_aliases={i: i for i in range(nb)},
        compiler_params=pltpu.CompilerParams(has_side_effects=_EFFECT),
    )(*both, *send, *recv, after)
    return list(outs[:ns]), list(outs[ns:])


def _gather_plan(split, parts):
    def plan(srcs, lands):
        x, y, c, chips = _place()
        out = []
        for ld, sp, n in zip(lands, split, parts):
            for k in range(n):
                blk = _gather_block(ld, sp, 2 * x + y, c, (k, n))
                out.append([(blk, blk, (qx, qy, c)) for qx, qy in chips])
        return out
    return plan


def _gather_whole(split, part=(0, 1)):
    def whole(lands):
        _, _, c, _ = _place()
        return [ld.at[pl.ds(0, 3), :, _gather_rows(ld, c, part), :] if sp else ld.at[pl.ds(0, 3)]
                for ld, sp in zip(lands, split)]
    return whole


def _relay_plan(srcs, lands):
    x, y, c, _ = _place()
    out = []
    for ld in lands:
        blocks = [ld.at[j, :, _half(ld.shape[2], c), :] for j in range(N_CHIPS)]
        out.append([(blk, blk, (x, y, 1 - c)) for blk in blocks])
    return out


def _relay_whole(lands):
    _, _, c, _ = _place()
    return [ld.at[:, :, _half(ld.shape[2], c), :] for ld in lands]


def _fill_own(name, shard, dtype, place, layer=None):
    nl, r, c = shard.shape
    first = 0
    if layer is not None:
        nl, first = 1, layer
    tr = _tile(r, 512)

    def body(p_ref, s_ref, o_ref):
        o_ref[...] = s_ref[...].astype(o_ref.dtype)

    return pl.pallas_call(
        body, name=name,
        grid_spec=pltpu.PrefetchScalarGridSpec(
            num_scalar_prefetch=1, grid=(nl, r // tr),
            in_specs=[pl.BlockSpec((None, tr, c), lambda l, i, p: (first + l, i, 0))],
            out_specs=pl.BlockSpec((None, None, tr, c), lambda l, i, p: (p[0], l, i, 0))),
        out_shape=jax.ShapeDtypeStruct((N_CHIPS, nl, r, c), dtype),
        compiler_params=_params("parallel", "parallel"),
    )(place, shard)


def _gather_finish(name, lands, part=(0, 1)):
    n = len(lands)

    def body(*refs):
        outs = refs[n:2 * n]
        fsend, frecv = refs[2 * n:]
        x, y, c, chips = _place()
        sib = (x, y, 1 - c)

        def relay(a, qi, cc):
            qx, qy = chips[qi]
            blk = _gather_block(outs[a], True, 2 * qx + qy, cc, part)
            return _remote(blk, blk, fsend.at[a, qi], frecv.at[a, qi], sib)

        relays = [relay(a, qi, c) for a in range(n) for qi in range(3)]
        for cp in relays:
            cp.start()
        for a in range(n):
            for qi in range(3):
                relay(a, qi, 1 - c).wait_recv()
        for cp in relays:
            cp.wait_send()

    outs = pl.pallas_call(
        body, name=name, in_specs=[_ANY] * n, out_specs=[_ANY] * n,
        out_shape=[jax.ShapeDtypeStruct(ld.shape, ld.dtype) for ld in lands],
        input_output_aliases={i: i for i in range(n)},
        scratch_shapes=[pltpu.SemaphoreType.DMA((n, 3))] * 2,
    )(*lands)
    return list(outs)


def _pair_plan(srcs, lands):
    x, y, c, _ = _place()
    return [[(s.at[:, :, _half(s.shape[2], 1 - c), :], ld, (x, y, 1 - c))] for s, ld in zip(srcs, lands)]


def _pair_whole(lands):
    return list(lands)


def _scatter_plan(srcs, lands):
    x, y, c, chips = _place()
    return [[(s.at[2 * qx + qy], ld.at[2 * x + y, :, _half(ld.shape[2], c), :], (qx, qy, c)) for qx, qy in chips]
            for s, ld in zip(srcs, lands)]


def _scatter_whole(lands):
    _, _, c, _ = _place()
    return [ld.at[pl.ds(0, 3), :, _half(ld.shape[2], c), :] for ld in lands]


def _pair_sum(name, grad, other, place):
    nj, nl, r, c = grad.shape
    h = r // 2
    tr = _tile(h, 2 * ROW_TILE)
    nb = h // tr

    def body(p_ref, g_ref, o_ref, q_ref, d_ref):
        q = (g_ref[...].astype(F32) + o_ref[...].astype(F32)).astype(BF16)
        q_ref[...] = q

        @pl.when(pl.program_id(2) == p_ref[0])
        def _():
            d_ref[...] = q

    blk = (None, None, tr, c)
    return pl.pallas_call(
        body, name=name,
        grid_spec=pltpu.PrefetchScalarGridSpec(
            num_scalar_prefetch=1, grid=(nl, nb, nj),
            in_specs=[pl.BlockSpec(blk, lambda l, i, j, p: (j, l, p[1] * nb + i, 0)),
                      pl.BlockSpec(blk, lambda l, i, j, p: (j, l, i, 0))],
            out_specs=[pl.BlockSpec(blk, lambda l, i, j, p: (j, l, i, 0)),
                       pl.BlockSpec(blk, lambda l, i, j, p: (p[0], l, p[1] * nb + i, 0))]),
        out_shape=[jax.ShapeDtypeStruct((nj, nl, h, c), BF16), jax.ShapeDtypeStruct((nj, nl, r, c), BF16)],
        compiler_params=_params("parallel", "parallel", "arbitrary"),
    )(place, grad, other)


def _own_of_eight(name, packed, device):
    rows, lanes = packed.shape
    tr = rows

    def body(d_ref, s_ref, o_ref):
        o_ref[...] = s_ref[...]

    return pl.pallas_call(
        body, name=name,
        grid_spec=pltpu.PrefetchScalarGridSpec(
            num_scalar_prefetch=1, grid=(rows // tr,),
            in_specs=[pl.BlockSpec((tr, lanes), lambda i, d: (i, 0))],
            out_specs=pl.BlockSpec((None, tr, lanes), lambda i, d: (d[0], i, 0))),
        out_shape=jax.ShapeDtypeStruct((N_DEV, rows, lanes), packed.dtype),
        compiler_params=_params("parallel"),
    )(device, packed)


def _all_plan(srcs, lands):
    x, y, c, _ = _place()
    (ld,) = lands
    blk = ld.at[4 * x + 2 * y + c]
    flips = [(a, b, d) for a in (0, 1) for b in (0, 1) for d in (0, 1) if a + b + d]
    return [[(blk, blk, (x + a - 2 * a * x, y + b - 2 * b * y, c + d - 2 * d * c)) for a, b, d in flips]]


def _all_whole(lands):
    return [lands[0].at[pl.ds(0, N_DEV - 1)]]


def _sum_of_eight(name, slots):
    n, rows, lanes = slots.shape
    tr = rows

    def body(s_ref, o_ref):
        total = s_ref[0]
        for d in range(1, n):
            total = total + s_ref[d]
        o_ref[...] = total

    return pl.pallas_call(
        body, name=name, grid=(rows // tr,),
        in_specs=[pl.BlockSpec((n, tr, lanes), lambda i: (0, i, 0))],
        out_specs=pl.BlockSpec((tr, lanes), lambda i: (i, 0)),
        out_shape=jax.ShapeDtypeStruct((rows, lanes), F32), compiler_params=_params("parallel"),
    )(slots)


def _pack(parts):
    rows = []
    for p in parts:
        flat = p.reshape(-1)
        pad = (-flat.shape[0]) % PACK_ELEMS
        rows.append(jnp.pad(flat, (0, pad)).reshape(-1, LANES))
    return jnp.concatenate(rows, axis=0)


def _unpack(packed, shapes):
    out, row = [], 0
    for sh in shapes:
        size = math.prod(sh)
        nrows = -(-size // PACK_ELEMS) * (PACK_ELEMS // LANES)
        out.append(packed[row:row + nrows].reshape(-1)[:size].reshape(sh))
        row += nrows
    return out


def kernel(x, a_w_in, a_ln_g, a_ln_b, a_w_s, a_b_s, a_w_out, b_w_in, b_w_grp, b_scale, b_w_out, norm_mix, norm_mlp, mlp_w1, mlp_w2, final_norm, loss_target, m_a_w_in, m_a_ln_g, m_a_ln_b, m_a_w_s, m_a_b_s, m_a_w_out, m_b_w_in, m_b_w_grp, m_b_scale, m_b_w_out, m_norm_mix, m_norm_mlp, m_mlp_w1, m_mlp_w2, m_final_norm, v_a_w_in, v_a_ln_g, v_a_ln_b, v_a_w_s, v_a_b_s, v_a_w_out, v_b_w_in, v_b_w_grp, v_b_scale, v_b_w_out, v_norm_mix, v_norm_mlp, v_mlp_w1, v_mlp_w2, v_final_norm):
    xi, yi, ci = lax.axis_index("x"), lax.axis_index("y"), lax.axis_index("c")
    chip = 2 * xi + yi
    place = jnp.stack([chip, ci]).astype(jnp.int32)
    x2, tgt = x[0], loss_target[0]
    bw = b_scale.shape[1] * N_CHIPS

    units = dict(a_w_in=(a_w_in, None), a_w_out=(a_w_out, None), w1_0=(mlp_w1, 0), w2_0=(mlp_w2, 0),
                 b_scale=(b_scale.reshape(1, 1, -1), None), b_w_in=(b_w_in, None), b_w_grp=(b_w_grp[0], None),
                 b_w_out=(b_w_out, None), w1_1=(mlp_w1, 1), w2_1=(mlp_w2, 1))
    col_sharded = dict(a_w_in=True, a_w_out=False, b_w_in=False, b_w_grp=False, b_w_out=False,
                       w1_0=True, w2_0=False, w1_1=True, w2_1=False)
    in_flight, W = {}, {}

    def launch(tag, keys, deps):
        sp = [k != "b_scale" for k in keys]
        parts = [pieces.get(k, 1) for k in keys]
        zones = [_fill_own(f"gather_own_{k}", units[k][0], BF16 if s else F32, place, layer=units[k][1])
                 for k, s in zip(keys, sp)]
        send, recv, _, zones, tok = _split_start(f"gather_start_{tag}", [], zones, _gather_plan(sp, parts), deps,
                                                 groups=sum(parts))
        first = 0
        for k, z, s, n in zip(keys, zones, sp, parts):
            in_flight[k] = (send[first:first + n], recv[first:first + n], z, s)
            first += n
        return tok

    def arrive_piece(key, k, after):
        send, recv, zone, sp = in_flight[key]
        n = len(send)
        _, zones = _split_wait(f"gather_wait_{key}_{k}", [], [zone], [send[k]], [recv[k]], after,
                               _gather_whole([sp], (k, n)))
        (zone,) = _gather_finish(f"gather_finish_{key}_{k}", zones, (k, n))
        in_flight[key] = (send, recv, zone, sp)
        W[key] = _W4(zone, col_sharded[key])

    def arrive(keys, after):
        send, recv, zones, sp = zip(*[in_flight[k] for k in keys])
        if len(send[0]) > 1:
            for k in range(len(send[0])):
                arrive_piece(keys[0], k, after)
            return
        _, zones = _split_wait(f"gather_wait_{keys[0]}", [], zones, [s[0] for s in send], [r[0] for r in recv],
                               after, _gather_whole(sp))
        relayed = iter(_gather_finish(f"gather_finish_{keys[0]}", [z for z, s in zip(zones, sp) if s]))
        for k, z, s in zip(keys, zones, sp):
            full = next(relayed) if s else z
            W[k] = _W4(full, col_sharded[k]) if k in col_sharded else full

    pieces = dict(w1_0=2, w2_0=2, w1_1=2, w2_1=2)

    token = launch("first", ["a_w_in", "a_w_out"], ())
    token = launch("rest", ["w1_0", "w2_0", "b_scale", "b_w_in", "b_w_grp", "b_w_out", "w1_1", "w2_1"], (token,))

    b_col = a_b_s[0].T

    def residual(acc, res):
        return (res + acc,)

    def sq_relu(acc):
        act = jnp.maximum(acc, 0.0)
        return act, act * act

    def mlp_fwd(tag, h, layer, tail=None):
        hn = _rms_fwd(f"mlp{tag}_norm", h, norm_mlp[layer:layer + 1])
        arrive([f"w1_{layer}"], hn)
        act, act_sq = _mm_aw(f"mlp{tag}_up", hn, W[f"w1_{layer}"], out_dtypes=(BF16, BF16), epilogue=sq_relu)
        out, n = h, pieces[f"w2_{layer}"]
        for k in range(n):
            arrive_piece(f"w2_{layer}", k, act_sq if k == 0 else out)
            if tail is not None and k == n - 1:
                out = tail(act_sq, W[f"w2_{layer}"], out, (k, n))
            else:
                out = _mm_aw(f"mlp{tag}_down_{k}", act_sq, W[f"w2_{layer}"], extras=(out,), epilogue=residual,
                             k_piece=(k, n))
        return out, (h, hn, act, act_sq)

    hn0 = _rms_fwd("mix_a_norm", x2, norm_mix[0:1])
    arrive(["a_w_in"], token)
    zpre = _mm_aw("mix_a_in", hn0, W["a_w_in"])
    gated = _gate_fwd("mix_a_gate", zpre, a_ln_g, a_ln_b, a_w_s[0], b_col)
    arrive(["a_w_out"], gated)
    h1 = _mm_aw("mix_a_out", gated, W["a_w_out"], extras=(x2,), epilogue=residual)
    h2, mlp0 = mlp_fwd("0", h1, 0)
    hn2 = _rms_fwd("mix_b_norm", h2, norm_mix[1:2])
    arrive(["b_scale", "b_w_in"], hn2)
    scale_full = W["b_scale"].reshape(1, bw)
    vb = _mm_aw("mix_b_in", hn2, W["b_w_in"])
    pooled = _pool_fwd("mix_b_pool", vb)
    arrive(["b_w_grp", "b_w_out"], pooled)
    mixed, ms = _mm_aw("mix_b_grp", pooled, W["b_w_grp"], groups=len(B_WINDOWS), extras=(scale_full,),
                       out_dtypes=(F32, BF16), epilogue=lambda acc, sc: (acc, acc * sc))
    h3 = _mm_aw("mix_b_out", ms, W["b_w_out"], extras=(h2,), epilogue=residual)
    (dh4, dh4_b, d_final, loss_part), mlp1 = mlp_fwd(
        "1", h3, 1, tail=lambda act_sq, w2, partial, piece: _down_tail_loss(
            "mlp1_down_loss_head", act_sq, w2, partial, final_norm.reshape(1, -1), tgt, piece))
    g1_like = _W4(None, True, shape=(N_CHIPS, 1, *W["w1_0"].arr.shape[2:]))
    g2_like = _W4(None, False, shape=(N_CHIPS, 1, *W["w2_0"].arr.shape[2:]))

    def exchange(tag, gs):
        zones = [lax.empty((g.shape[0], g.shape[1], g.shape[2] // 2, g.shape[3]), g.dtype) for g in gs]
        send, recv, srcs, zones, tok = _split_start(f"pair_start_{tag}", gs, zones, _pair_plan)
        return (tag, send, recv, srcs, zones), tok

    def reduce(state, after, then_exchange=None):
        tag, send, recv, srcs, zones = state
        srcs, zones = _split_wait(f"pair_wait_{tag}", srcs, zones, send, recv, after, _pair_whole)
        both = [_pair_sum(f"pair_sum_{tag}_{i}", g, o, place) for i, (g, o) in enumerate(zip(srcs, zones))]
        sums, dests = [b[0] for b in both], [b[1] for b in both]
        if then_exchange is None:
            send, recv, sums, dests, tok = _split_start(f"scatter_start_{tag}", sums, dests, _scatter_plan)
            return (tag, send, recv, sums, dests), tok
        other, gs = then_exchange
        n = len(sums)
        zones = [lax.empty((g.shape[0], g.shape[1], g.shape[2] // 2, g.shape[3]), g.dtype) for g in gs]
        send, recv, srcs, lands, tok = _split_start(
            f"scatter_start_{tag}", sums + list(gs), dests + zones,
            lambda s, ld: _scatter_plan(s[:n], ld[:n]) + _pair_plan(s[n:], ld[n:]))
        return ((tag, send[:n], recv[:n], srcs[:n], lands[:n]), tok,
                (other, send[n:], recv[n:], srcs[n:], lands[n:]))

    def relay_all(states, after):
        waited = []
        for tag, send, recv, sums, dests in states:
            waited.append((tag, _split_wait(f"scatter_wait_{tag}", sums, dests, send, recv, after, _scatter_whole)[1]))
        flat = [d for _, ds in waited for d in ds]
        send, recv, _, flat, tok = _split_start(f"relay_start_{waited[0][0]}", [], flat, _relay_plan)
        out, first = [], 0
        for tag, ds in waited:
            last = first + len(ds)
            out.append(((tag, send[first:last], recv[first:last], flat[first:last]), tok))
            first = last
        return out

    def relay(state, after):
        return relay_all([state], after)[0]

    def land(state, after):
        tag, send, recv, dests = state
        return _split_wait(f"relay_wait_{tag}", [], dests, send, recv, after, _relay_whole)[1]

    def mlp_bwd(tag, dh, dh_b, saved, layer, deps, pending=None):
        h, hn, act, act_sq = saved
        dpre = _mm_aw(f"mlp{tag}_down_dx", dh_b, W[f"w2_{layer}"], transpose_w=True, extras=(act,),
                      out_dtypes=(BF16,), epilogue=lambda acc, a: (acc * (2.0 * a),), deps=deps)
        scattering, dw_deps = None, ()
        if pending is not None:
            scattering, tok = reduce(pending, dpre)
            dw_deps = (tok,)
        g_w2 = _mm_dw(f"mlp{tag}_down_dw", act_sq, dh_b, g2_like, deps=dw_deps)
        pair_w2, tok = exchange(f"w2_{layer}", [g_w2])
        dhn = _mm_aw(f"mlp{tag}_up_dx", dpre, W[f"w1_{layer}"], transpose_w=True, deps=(tok,))
        g_w1 = _mm_dw(f"mlp{tag}_up_dw", hn, dpre, g1_like)
        scat_w2, tok2, pair_w1 = reduce(pair_w2, dhn, then_exchange=(f"w1_{layer}", [g_w1]))
        dh_in, dh_in_b, d_norm = _rms_bwd(f"mlp{tag}_norm_bwd", h, norm_mlp[layer:layer + 1], dhn, dh)
        return dh_in, dh_in_b, d_norm, pair_w1, scat_w2, (tok2,), scattering

    dh3, dh3_b, d_norm_mlp1, pair_w1_1, scat_w2_1, toks, _ = mlp_bwd("1", dh4, dh4_b, mlp1, 1, ())
    dms, g_b_out = _mm_bwd("mix_b_out_bwd", dh3_b, W["b_w_out"], ms, deps=toks)
    scat_w1_1, tok = reduce(pair_w1_1, dms)
    dmixed, d_scale = _scale_bwd("mix_b_scale_bwd", dms, mixed, scale_full)
    dpooled = _mm_aw("mix_b_grp_dx", dmixed, W["b_w_grp"], groups=len(B_WINDOWS), transpose_w=True, deps=(tok,))
    g_b_grp = _mm_dw("mix_b_grp_dw", pooled, dmixed, W["b_w_grp"], groups=len(B_WINDOWS))
    dvb = _pool_bwd("mix_b_pool_bwd", dpooled)
    dhn2, g_b_in = _mm_bwd("mix_b_in_bwd", dvb, W["b_w_in"], hn2)
    pair_b, tok = exchange("b", [g_b_out, g_b_grp, g_b_in])
    dh2, dh2_b, d_norm_mix1 = _rms_bwd("mix_b_norm_bwd", h2, norm_mix[1:2], dhn2, dh3)
    dh1, dh1_b, d_norm_mlp0, pair_w1_0, scat_w2_0, toks, scat_b = mlp_bwd("0", dh2, dh2_b, mlp0, 0, (tok,),
                                                                          pending=pair_b)
    dgated, g_a_out = _mm_bwd("mix_a_out_bwd", dh1_b, W["a_w_out"], gated, deps=toks)
    scat_w1_0, tok, pair_a_out = reduce(pair_w1_0, dgated, then_exchange=("a_out", [g_a_out]))
    early = relay_all([scat_w2_1, scat_w1_1, scat_b], dgated)
    dzpre, d_w_s, d_b_col, d_ln_g, d_ln_b = _gate_bwd("mix_a_gate_bwd", zpre, dgated, a_ln_g, a_ln_b, a_w_s[0], b_col)
    dhn0 = _mm_aw("mix_a_in_dx", dzpre, W["a_w_in"], transpose_w=True, deps=(tok, early[0][1]))
    scat_a_out, tok = reduce(pair_a_out, dhn0)
    g_a_in = _mm_dw("mix_a_in_dw", hn0, dzpre, W["a_w_in"], deps=(tok,))
    pair_a_in, tok = exchange("a_in", [g_a_in])
    dx, _, d_norm_mix0 = _rms_bwd("mix_a_norm_bwd", x2, norm_mix[0:1], dhn0, dh1)
    scat_a_in, _ = reduce(pair_a_in, dx)

    small = dict(a_ln_g=(a_ln_g, m_a_ln_g, v_a_ln_g), a_ln_b=(a_ln_b, m_a_ln_b, v_a_ln_b),
                 a_w_s=(a_w_s, m_a_w_s, v_a_w_s), a_b_s=(a_b_s, m_a_b_s, v_a_b_s),
                 b_scale=(b_scale, m_b_scale, v_b_scale), norm_mix=(norm_mix, m_norm_mix, v_norm_mix),
                 norm_mlp=(norm_mlp, m_norm_mlp, v_norm_mlp), final_norm=(final_norm, m_final_norm, v_final_norm))
    small_names = list(small)
    local = dict(a_ln_g=d_ln_g, a_ln_b=d_ln_b, a_w_s=d_w_s[None], a_b_s=d_b_col.T[None], b_scale=d_scale,
                 norm_mix=jnp.concatenate([d_norm_mix0, d_norm_mix1], axis=0),
                 norm_mlp=jnp.concatenate([d_norm_mlp0, d_norm_mlp1], axis=0), final_norm=d_final.reshape(-1))
    device = (4 * xi + 2 * yi + ci).astype(jnp.int32).reshape(1)
    slots = _own_of_eight("small_own", _pack([local[k] for k in small_names]), device)
    small_send, small_recv, _, (slots,), small_tok = _split_start("small_start", [], [slots], _all_plan)

    moments = dict(a_w_in=(m_a_w_in, v_a_w_in), a_w_out=(m_a_w_out, v_a_w_out), b_w_in=(m_b_w_in, v_b_w_in),
                   b_w_grp=(m_b_w_grp, v_b_w_grp), b_w_out=(m_b_w_out, v_b_w_out),
                   mlp_w1=(m_mlp_w1, v_mlp_w1), mlp_w2=(m_mlp_w2, v_mlp_w2))
    weights = dict(a_w_in=a_w_in, a_w_out=a_w_out, b_w_in=b_w_in, b_w_grp=b_w_grp, b_w_out=b_w_out,
                   mlp_w1=mlp_w1, mlp_w2=mlp_w2)
    landing = [(scat_w2_1, [("mlp_w2", 1)]), (scat_w1_1, [("mlp_w1", 1)]),
               (scat_b, [("b_w_out", 0), ("b_w_grp", 0), ("b_w_in", 0)]),
               (scat_w2_0, [("mlp_w2", 0)]), (scat_w1_0, [("mlp_w1", 0)]),
               (scat_a_out, [("a_w_out", 0)]), (scat_a_in, [("a_w_in", 0)])]
    results, after = {}, dx
    relays = list(early)
    for i, (_, members) in enumerate(landing):
        deps = (small_tok,) if i == 0 else ()
        if len(relays) == i + 1 < len(landing):
            relays.append(relay(landing[i + 1][0], after))
            deps = (relays[-1][1],)
        for (k, layer), parts in zip(members, land(relays[i][0], relays[-1][1] if deps else after)):
            shard_shape = (-1, *parts.shape[2:])
            results[k] = _adam_shard(f"adam_{k}_{layer}", weights[k].reshape(shard_shape),
                                     moments[k][0].reshape(shard_shape), moments[k][1].reshape(shard_shape),
                                     parts, layer=layer, prev=results.get(k), deps=deps)
            after = results[k][1]
    grad_out, delta_out, m_out, v_out = {}, {}, {}, {}
    for k, res in results.items():
        grad_out[k], delta_out[k], m_out[k], v_out[k] = [r.reshape(weights[k].shape) for r in res]

    _, (slots,) = _split_wait("small_wait", [], [slots], small_send, small_recv, after, _all_whole)
    reduced = _sum_of_eight("small_sum", slots)
    small_grads = dict(zip(small_names, _unpack(reduced, [local[k].shape for k in small_names])))
    shard_w = b_scale.shape[1]
    small_grads["b_scale"] = lax.dynamic_slice_in_dim(small_grads["b_scale"], chip * shard_w, shard_w, axis=1)
    small_grads = {k: small_grads[k].reshape(small[k][0].shape) for k in small_names}
    packed = [_pack([small[k][i] for k in small_names]) for i in range(3)]
    res = _adam_packed("adam_small", packed[0], _pack([small_grads[k] for k in small_names]), packed[1], packed[2])
    shapes = [small[k][0].shape for k in small_names]
    for k, d, nm, nv in zip(small_names, *[_unpack(r, shapes) for r in res]):
        grad_out[k], delta_out[k], m_out[k], v_out[k] = small_grads[k], d, nm, nv

    loss = lax.psum(loss_part[0, 0], ("x", "y", "c"))
    order = ["a_w_in", "a_ln_g", "a_ln_b", "a_w_s", "a_b_s", "a_w_out", "b_w_in", "b_w_grp", "b_scale", "b_w_out",
             "norm_mix", "norm_mlp", "mlp_w1", "mlp_w2", "final_norm"]
    return (loss, dx[None], *[grad_out[k] for k in order], *[delta_out[k] for k in order],
            *[m_out[k] for k in order], *[v_out[k] for k in order])
```

```python
import math

import jax
import jax.numpy as jnp
from jax import lax
from jax.experimental import pallas as pl
from jax.experimental.pallas import tpu as pltpu

F32 = jnp.float32
BF16 = jnp.bfloat16
MESH = pl.DeviceIdType.MESH

EPS = 1e-6
B_WINDOWS = (2, 4, 8, 16)
ADAM_LR = 0.001
ADAM_B1 = 0.9
ADAM_B2 = 0.999
ADAM_EPS = 1e-08
ADAM_WD = 0.01
ADAM_STEP = 10

N_CHIPS = 4
N_DEV = 8
LANES = 128
PACK_ELEMS = 8 * LANES
VMEM_LIMIT = 56 * 1024 * 1024
ROW_TILE = 512
MM_TM, MM_TN, MM_TK = 1024, 1024, 2048


_ANY = pl.BlockSpec(memory_space=pl.ANY)


def _tile(dim, pref):
    t = min(dim, pref)
    while dim % t:
        t //= 2
    return t


def _params(*sem):
    return pltpu.CompilerParams(dimension_semantics=sem, vmem_limit_bytes=VMEM_LIMIT)


class _W4:
    def __init__(self, arr, col_sharded, shape=None):
        self.arr = arr
        self.nj, self.nl, self.r, self.c = arr.shape if shape is None else shape
        self.col = col_sharded
        self.rows = self.r if col_sharded else self.nj * self.r
        self.cols = self.nj * self.c if col_sharded else self.c

    def tile_rows(self, pref):
        return _tile(self.r, pref)

    def tile_cols(self, pref):
        return _tile(self.c, pref)

    def index(self, layer, rb, cb, tr, tc):
        if self.col:
            n = self.c // tc
            return (cb // n, layer, rb, cb % n)
        n = self.r // tr
        return (rb // n, layer, rb % n, cb)


def _mm_aw(name, a, w, *, layer=0, groups=1, transpose_w=False, extras=(), out_dtypes=(F32,), epilogue=None,
           deps=(), k_piece=None):
    s, ka_total = a.shape
    kdim, ndim = (w.cols, w.rows) if transpose_w else (w.rows, w.cols)
    assert ka_total == groups * kdim, (name, a.shape, kdim, groups)
    span = not w.col and ((not transpose_w and kdim <= MM_TK) or (transpose_w and groups > 1))
    across = transpose_w and w.col and groups == 1 and MM_TK < kdim <= 2 * MM_TK
    if span and transpose_w:
        tm, tn, tk = _tile(s, 2048), ndim, w.tile_cols(MM_TK)
    elif across:
        tm, tn, tk = _tile(s, MM_TM), w.tile_rows(512), kdim
    else:
        tk = kdim if span else (w.tile_cols(MM_TK) if transpose_w else w.tile_rows(MM_TK))
        tm, tn_pref = (_tile(s, 2048), 512) if tk == kdim else (_tile(s, MM_TM), MM_TN)
        tn = w.tile_rows(tn_pref) if transpose_w else w.tile_cols(tn_pref)
    nk, nn = kdim // tk, ndim // tn
    if k_piece is not None:
        assert not (w.col or transpose_w or span or groups > 1), name
        piece, n_pieces = k_piece
        tm, tn, tk = _tile(s, MM_TM), w.tile_cols(MM_TN), w.r // (2 * n_pieces)
        nk, nn = 2 * w.nj, ndim // tn

        def k_block(k):
            return k // 2, (k % 2) * n_pieces + piece

    def lay(g):
        return g if groups > 1 else layer

    a_spec = pl.BlockSpec((tm, tk), lambda g, i, n, k: (i, g * nk + k))
    if k_piece is not None:
        per = w.r // tk
        a_spec = pl.BlockSpec((tm, tk), lambda g, i, n, k: (i, k_block(k)[0] * per + k_block(k)[1]))
        w_spec = pl.BlockSpec((None, None, tk, tn), lambda g, i, n, k: (k_block(k)[0], layer, k_block(k)[1], n))
    elif span and transpose_w:
        w_spec = pl.BlockSpec((w.nj, None, w.r, tk), lambda g, i, n, k: (0, lay(g), 0, k))
    elif span:
        w_spec = pl.BlockSpec((w.nj, None, w.r, tn), lambda g, i, n, k: (0, lay(g), 0, n))
    elif across:
        w_spec = pl.BlockSpec((w.nj, None, tn, w.c), lambda g, i, n, k: (0, layer, n, 0))
    elif transpose_w:
        w_spec = pl.BlockSpec((None, None, tn, tk), lambda g, i, n, k: w.index(lay(g), n, k, tn, tk))
    else:
        w_spec = pl.BlockSpec((None, None, tk, tn), lambda g, i, n, k: w.index(lay(g), k, n, tk, tn))
    ex_specs = []
    for e in extras:
        assert e.shape[1] == groups * ndim and e.shape[0] in (1, s), (name, e.shape)
        if e.shape[0] == 1:
            ex_specs.append(pl.BlockSpec((1, tn), lambda g, i, n, k: (0, g * nn + n)))
        else:
            ex_specs.append(pl.BlockSpec((tm, tn), lambda g, i, n, k: (i, g * nn + n)))
    out_spec = pl.BlockSpec((tm, tn), lambda g, i, n, k: (i, g * nn + n))
    n_ex, n_out, n_dep = len(extras), len(out_dtypes), len(deps)

    def body(a_ref, w_ref, *rest):
        ex, outs = rest[:n_ex], rest[n_ex + n_dep:n_ex + n_dep + n_out]
        av = a_ref[...]
        if av.dtype != BF16:
            av = av.astype(BF16)
        nt = (((1,), (1,)), ((), ()))
        if across:
            prod = lax.dot_general(av[:, :w.c], w_ref[0], nt, preferred_element_type=F32)
            for j in range(1, w.nj):
                prod += lax.dot_general(av[:, j * w.c:(j + 1) * w.c], w_ref[j], nt, preferred_element_type=F32)
        else:
            wv = w_ref[...]
            if span:
                wv = wv.reshape((tn, tk) if transpose_w else (tk, tn))
            prod = lax.dot_general(av, wv, nt, preferred_element_type=F32) if transpose_w else jnp.dot(
                av, wv, preferred_element_type=F32)

        def finish(total):
            vals = (total,) if epilogue is None else epilogue(total, *[e[...] for e in ex])
            for o, v in zip(outs, vals):
                o[...] = v.astype(o.dtype)

        if nk == 1:
            finish(prod)
            return
        acc, k = rest[-1], pl.program_id(3)

        @pl.when(k == 0)
        def _():
            acc[...] = prod

        @pl.when(k > 0)
        def _():
            acc[...] += prod

        @pl.when(k == nk - 1)
        def _():
            finish(acc[...])

    outs = pl.pallas_call(
        body, name=name, grid=(groups, s // tm, nn, nk),
        in_specs=[a_spec, w_spec, *ex_specs] + [_ANY] * n_dep, out_specs=[out_spec] * n_out,
        out_shape=[jax.ShapeDtypeStruct((s, groups * ndim), dt) for dt in out_dtypes],
        scratch_shapes=[pltpu.VMEM((tm, tn), F32)] if nk > 1 else [],
        compiler_params=_params("parallel", "parallel", "parallel", "arbitrary"),
    )(a, w.arr, *extras, *deps)
    return outs[0] if n_out == 1 else outs


def _mm_dw(name, a, b, like, *, layer=0, groups=1, deps=()):
    s, ka_total = a.shape
    rows, cols = ka_total // groups, b.shape[1] // groups
    assert (rows, cols) == (like.rows, like.cols) and b.shape[0] == s, (name, a.shape, b.shape)
    span = groups > 1 and not like.col
    tm, tn, tk = rows if span else like.tile_rows(MM_TM), like.tile_cols(2 * MM_TN), _tile(s, MM_TK)
    nr, nc, nk = rows // tm, cols // tn, s // tk
    assert nk == 1 or not span

    def lay(g):
        return g if groups > 1 else layer

    in_specs = [pl.BlockSpec((tk, tm), lambda g, n, i, k: (k, g * nr + i)),
                pl.BlockSpec((tk, tn), lambda g, n, i, k: (k, g * nc + n))]
    in_specs += [_ANY] * len(deps)

    def body(a_ref, b_ref, *rest):
        av, bv = a_ref[...], b_ref[...]
        if av.dtype != BF16:
            av = av.astype(BF16)
        if bv.dtype != BF16:
            bv = bv.astype(BF16)
        prod = lax.dot_general(av, bv, (((0,), (0,)), ((), ())), preferred_element_type=F32)
        if nk == 1:
            rest[-1][...] = prod.astype(BF16).reshape(rest[-1].shape)
            return
        o_ref, acc, k = rest[-2], rest[-1], pl.program_id(3)

        @pl.when(k == 0)
        def _():
            acc[...] = prod

        @pl.when(k > 0)
        def _():
            acc[...] += prod

        @pl.when(k == nk - 1)
        def _():
            o_ref[...] = acc[...].astype(BF16)

    if span:
        out_spec = pl.BlockSpec((like.nj, None, like.r, tn), lambda g, n, i, k: (0, g, 0, n))
    else:
        out_spec = pl.BlockSpec((None, None, tm, tn), lambda g, n, i, k: like.index(lay(g), i, n, tm, tn))
    return pl.pallas_call(
        body, name=name, grid=(groups, nc, nr, nk), in_specs=in_specs, out_specs=out_spec,
        out_shape=jax.ShapeDtypeStruct((like.nj, like.nl, like.r, like.c), BF16),
        scratch_shapes=[pltpu.VMEM((tm, tn), F32)] if nk > 1 else [],
        compiler_params=_params("parallel", "parallel", "parallel", "arbitrary"),
    )(a, b, *deps)


def _mm_bwd(name, dy, w, a, *, deps=()):
    s, c = dy.shape
    assert not w.col and w.nl == 1 and w.c == c and a.shape == (s, w.rows), (name, dy.shape, a.shape)
    r = w.r

    def body(dy_ref, w_ref, a_ref, *rest):
        dx_ref, dw_ref = rest[-2:]
        dyv = dy_ref[...]
        dx_ref[...] = lax.dot_general(dyv, w_ref[...], (((1,), (1,)), ((), ())), preferred_element_type=F32)
        dw_ref[...] = lax.dot_general(a_ref[...], dyv, (((0,), (0,)), ((), ())),
                                      preferred_element_type=F32).astype(BF16)

    return pl.pallas_call(
        body, name=name, grid=(w.nj,),
        in_specs=[pl.BlockSpec((s, c), lambda j: (0, 0)), pl.BlockSpec((None, None, r, c), lambda j: (j, 0, 0, 0)),
                  pl.BlockSpec((s, r), lambda j: (0, j))] + [_ANY] * len(deps),
        out_specs=[pl.BlockSpec((s, r), lambda j: (0, j)), pl.BlockSpec((None, None, r, c), lambda j: (j, 0, 0, 0))],
        out_shape=[jax.ShapeDtypeStruct((s, w.rows), F32), jax.ShapeDtypeStruct((w.nj, 1, r, c), BF16)],
        compiler_params=_params("parallel"),
    )(dy, w.arr, a, *deps)


def _row_spec(tr, d):
    return pl.BlockSpec((tr, d), lambda i: (i, 0))


def _vec_spec(d):
    return pl.BlockSpec((1, d), lambda i: (0, 0))


def _rms_fwd(name, x, g):
    s, d = x.shape
    tr = _tile(s, ROW_TILE)

    def body(x_ref, g_ref, o_ref):
        xv = x_ref[...]
        r = lax.rsqrt(jnp.mean(xv * xv, axis=-1, keepdims=True) + EPS)
        o_ref[...] = (xv * r * g_ref[...]).astype(BF16)

    return pl.pallas_call(
        body, name=name, grid=(s // tr,), in_specs=[_row_spec(tr, d), _vec_spec(d)],
        out_specs=_row_spec(tr, d), out_shape=jax.ShapeDtypeStruct((s, d), BF16),
        compiler_params=_params("parallel"),
    )(x, g)


def _rms_bwd(name, x, g, dhn, dres):
    s, d = x.shape
    tr = _tile(s, ROW_TILE)

    def body(x_ref, g_ref, dhn_ref, dres_ref, dx_ref, dxb_ref, dg_ref):
        @pl.when(pl.program_id(0) == 0)
        def _():
            dg_ref[...] = jnp.zeros_like(dg_ref)

        xv = x_ref[...]
        r = lax.rsqrt(jnp.mean(xv * xv, axis=-1, keepdims=True) + EPS)
        xh = xv * r
        dy = dhn_ref[...]
        dg_ref[...] += jnp.sum(dy * xh, axis=0, keepdims=True)
        dxh = dy * g_ref[...]
        dx = dres_ref[...] + r * (dxh - xh * jnp.mean(dxh * xh, axis=-1, keepdims=True))
        dx_ref[...] = dx
        dxb_ref[...] = dx.astype(BF16)

    return pl.pallas_call(
        body, name=name, grid=(s // tr,),
        in_specs=[_row_spec(tr, d), _vec_spec(d), _row_spec(tr, d), _row_spec(tr, d)],
        out_specs=[_row_spec(tr, d), _row_spec(tr, d), _vec_spec(d)],
        out_shape=[jax.ShapeDtypeStruct((s, d), F32), jax.ShapeDtypeStruct((s, d), BF16),
                   jax.ShapeDtypeStruct((1, d), F32)],
        compiler_params=_params("arbitrary"),
    )(x, g, dhn, dres)


def _loss_head_tile(hv, gv, tv):
    r = lax.rsqrt(jnp.mean(hv * hv, axis=-1, keepdims=True) + EPS)
    xh = hv * r
    err = xh * gv - tv
    part = 0.5 * jnp.sum(jnp.mean(err * err, axis=-1, keepdims=True), axis=0, keepdims=True)
    dy = err * (1.0 / hv.shape[-1])
    dxh = dy * gv
    dh = r * (dxh - xh * jnp.mean(dxh * xh, axis=-1, keepdims=True))
    return part, jnp.sum(dy * xh, axis=0, keepdims=True), dh


def _down_tail_loss(name, a, w, partial, g, tgt, k_piece):
    s, d = partial.shape
    piece, n_pieces = k_piece
    assert not w.col and w.nl == 1 and w.cols == d and a.shape == (s, w.rows), (name, a.shape)
    tm, tk = _tile(s, 512), w.r // (2 * n_pieces)
    per, nk = w.r // tk, w.nj

    def body(a0_ref, a1_ref, w0_ref, w1_ref, p_ref, g_ref, t_ref, dh_ref, dhb_ref, dg_ref, loss_ref, acc):
        i, k = pl.program_id(0), pl.program_id(1)

        @pl.when(jnp.logical_and(i == 0, k == 0))
        def _():
            dg_ref[...] = jnp.zeros_like(dg_ref)
            loss_ref[...] = jnp.zeros_like(loss_ref)

        prod = jnp.dot(a0_ref[...], w0_ref[...], preferred_element_type=F32)
        prod += jnp.dot(a1_ref[...], w1_ref[...], preferred_element_type=F32)

        @pl.when(k == 0)
        def _():
            acc[...] = prod

        @pl.when(k > 0)
        def _():
            acc[...] += prod

        @pl.when(k == nk - 1)
        def _():
            part, dg, dh = _loss_head_tile(p_ref[...] + acc[...], g_ref[...], t_ref[...])
            loss_ref[...] += jnp.broadcast_to(part, loss_ref.shape)
            dg_ref[...] += dg
            dh_ref[...] = dh
            dhb_ref[...] = dh.astype(BF16)

    rows = pl.BlockSpec((tm, d), lambda i, k: (i, 0))
    return pl.pallas_call(
        body, name=name, grid=(s // tm, nk),
        in_specs=[pl.BlockSpec((tm, tk), lambda i, k: (i, k * per + piece)),
                  pl.BlockSpec((tm, tk), lambda i, k: (i, k * per + n_pieces + piece)),
                  pl.BlockSpec((None, None, tk, d), lambda i, k: (k, 0, piece, 0)),
                  pl.BlockSpec((None, None, tk, d), lambda i, k: (k, 0, n_pieces + piece, 0)),
                  rows, pl.BlockSpec((1, d), lambda i, k: (0, 0)), rows],
        out_specs=[rows, rows, pl.BlockSpec((1, d), lambda i, k: (0, 0)), pl.BlockSpec((1, LANES), lambda i, k: (0, 0))],
        out_shape=[jax.ShapeDtypeStruct((s, d), F32), jax.ShapeDtypeStruct((s, d), BF16),
                   jax.ShapeDtypeStruct((1, d), F32), jax.ShapeDtypeStruct((1, LANES), F32)],
        scratch_shapes=[pltpu.VMEM((tm, d), F32)],
        compiler_params=_params("arbitrary", "arbitrary"),
    )(a, a, w.arr, w.arr, partial, g, tgt)


_SQRT_HALF = 1.0 / math.sqrt(2.0)
_INV_SQRT_2PI = 1.0 / math.sqrt(2.0 * math.pi)


def _gelu(x):
    return x * (lax.erf(x * _SQRT_HALF) + 1.0) * 0.5


def _gelu_grad(x):
    return 0.5 * (lax.erf(x * _SQRT_HALF) + 1.0) + x * jnp.exp(-0.5 * x * x) * _INV_SQRT_2PI


def _causal(chunk):
    t = lax.broadcasted_iota(jnp.int32, (chunk, chunk), 0)
    sidx = lax.broadcasted_iota(jnp.int32, (chunk, chunk), 1)
    return sidx <= t


def _layernorm_parts(v, g, b):
    mu = jnp.mean(v, axis=-1, keepdims=True)
    vc = v - mu
    rs = lax.rsqrt(jnp.mean(vc * vc, axis=-1, keepdims=True) + EPS)
    vhat = vc * rs
    return vhat, rs, vhat * g + b


def _gate_fwd(name, zpre, ln_g, ln_b, w_s, b_col):
    s, aw2 = zpre.shape
    aw = aw2 // 2
    ng, chunk, _ = w_s.shape
    dh = aw // ng

    def body(z_ref, g_ref, b_ref, ws_ref, bc_ref, o_ref):
        u = _gelu(z_ref[:, :aw])
        v = _gelu(z_ref[:, aw:])
        _, _, vln = _layernorm_parts(v, g_ref[...], b_ref[...])
        mask = _causal(chunk)
        for gi in range(ng):
            sl = slice(gi * dh, (gi + 1) * dh)
            wm = jnp.where(mask, ws_ref[gi], 0.0).astype(BF16)
            sg = jnp.dot(wm, vln[:, sl].astype(BF16), preferred_element_type=F32) + bc_ref[:, gi:gi + 1]
            o_ref[:, sl] = (u[:, sl] * sg).astype(BF16)

    return pl.pallas_call(
        body, name=name, grid=(s // chunk,),
        in_specs=[_row_spec(chunk, aw2), _vec_spec(aw), _vec_spec(aw),
                  pl.BlockSpec((ng, chunk, chunk), lambda i: (0, 0, 0)),
                  pl.BlockSpec((chunk, ng), lambda i: (0, 0))],
        out_specs=_row_spec(chunk, aw), out_shape=jax.ShapeDtypeStruct((s, aw), BF16),
        compiler_params=_params("parallel"),
    )(zpre, ln_g, ln_b, w_s, b_col)


def _gate_bwd(name, zpre, dgated, ln_g, ln_b, w_s, b_col):
    s, aw2 = zpre.shape
    aw = aw2 // 2
    ng, chunk, _ = w_s.shape
    dh = aw // ng

    def body(z_ref, dgt_ref, g_ref, b_ref, ws_ref, bc_ref, dz_ref, dws_ref, dbc_ref, dlg_ref, dlb_ref,
             du_scr, dvln_scr):
        @pl.when(pl.program_id(0) == 0)
        def _():
            dws_ref[...] = jnp.zeros_like(dws_ref)
            dbc_ref[...] = jnp.zeros_like(dbc_ref)
            dlg_ref[...] = jnp.zeros_like(dlg_ref)
            dlb_ref[...] = jnp.zeros_like(dlb_ref)

        zu = z_ref[:, :aw]
        zv = z_ref[:, aw:]
        u = _gelu(zu)
        lg = g_ref[...]
        vhat, rs, vln = _layernorm_parts(_gelu(zv), lg, b_ref[...])
        mask = _causal(chunk)
        for gi in range(ng):
            sl = slice(gi * dh, (gi + 1) * dh)
            wm = jnp.where(mask, ws_ref[gi], 0.0).astype(BF16)
            vg = vln[:, sl].astype(BF16)
            sg = jnp.dot(wm, vg, preferred_element_type=F32) + bc_ref[:, gi:gi + 1]
            dgt = dgt_ref[:, sl]
            du_scr[:, sl] = dgt * sg
            ds = dgt * u[:, sl]
            dbc_ref[:, gi:gi + 1] += jnp.sum(ds, axis=-1, keepdims=True)
            dsb = ds.astype(BF16)
            dwm = lax.dot_general(dsb, vg, (((1,), (1,)), ((), ())), preferred_element_type=F32)
            dws_ref[gi] += jnp.where(mask, dwm, 0.0)
            dvln_scr[:, sl] = lax.dot_general(wm, dsb, (((0,), (0,)), ((), ())), preferred_element_type=F32)
        dvln = dvln_scr[...]
        dlb_ref[...] += jnp.sum(dvln, axis=0, keepdims=True)
        dlg_ref[...] += jnp.sum(dvln * vhat, axis=0, keepdims=True)
        dvh = dvln * lg
        dv = rs * (dvh - jnp.mean(dvh, axis=-1, keepdims=True)
                   - vhat * jnp.mean(dvh * vhat, axis=-1, keepdims=True))
        dz_ref[:, :aw] = (du_scr[...] * _gelu_grad(zu)).astype(BF16)
        dz_ref[:, aw:] = (dv * _gelu_grad(zv)).astype(BF16)

    return pl.pallas_call(
        body, name=name, grid=(s // chunk,),
        in_specs=[_row_spec(chunk, aw2), _row_spec(chunk, aw), _vec_spec(aw), _vec_spec(aw),
                  pl.BlockSpec((ng, chunk, chunk), lambda i: (0, 0, 0)),
                  pl.BlockSpec((chunk, ng), lambda i: (0, 0))],
        out_specs=[_row_spec(chunk, aw2), pl.BlockSpec((ng, chunk, chunk), lambda i: (0, 0, 0)),
                   pl.BlockSpec((chunk, ng), lambda i: (0, 0)), _vec_spec(aw), _vec_spec(aw)],
        out_shape=[jax.ShapeDtypeStruct((s, aw2), BF16), jax.ShapeDtypeStruct((ng, chunk, chunk), F32),
                   jax.ShapeDtypeStruct((chunk, ng), F32), jax.ShapeDtypeStruct((1, aw), F32),
                   jax.ShapeDtypeStruct((1, aw), F32)],
        scratch_shapes=[pltpu.VMEM((chunk, aw), F32), pltpu.VMEM((chunk, aw), F32)],
        compiler_params=_params("arbitrary"),
    )(zpre, dgated, ln_g, ln_b, w_s, b_col)


def _pool_select(g, parts):
    out = parts[-1]
    for gi in range(len(parts) - 2, -1, -1):
        out = jnp.where(g == gi, parts[gi], out)
    return out


def _pool_specs(s, bw):
    head = bw // len(B_WINDOWS)
    tc = _tile(head, 256)
    nb = head // tc
    return tc, (len(B_WINDOWS), nb), pl.BlockSpec((s, tc), lambda g, j: (0, g * nb + j))


def _pool_window(g, t):
    w = _pool_select(g, [jnp.full(t.shape, wi, jnp.int32) for wi in B_WINDOWS])
    return jnp.minimum(t + 1, w).astype(F32)


def _pool_fwd(name, vb):
    assert B_WINDOWS == (2, 4, 8, 16)
    s, bw = vb.shape
    tc, grid, spec = _pool_specs(s, bw)

    def body(v_ref, o_ref):
        g = pl.program_id(0)
        v = v_ref[...]
        t = lax.broadcasted_iota(jnp.int32, (s, tc), 0)

        def down(x, k):
            return jnp.where(t >= k, pltpu.roll(x, k, 0), 0.0)

        sums, cur, k = [], v, 1
        for _ in B_WINDOWS:
            cur = cur + down(cur, k)
            sums.append(cur)
            k *= 2
        o_ref[...] = (_pool_select(g, sums) / _pool_window(g, t) - v).astype(BF16)

    return pl.pallas_call(
        body, name=name, grid=grid, in_specs=[spec], out_specs=spec,
        out_shape=jax.ShapeDtypeStruct((s, bw), BF16), compiler_params=_params("parallel", "parallel"),
    )(vb)


def _pool_bwd(name, dpooled):
    s, bw = dpooled.shape
    tc, grid, spec = _pool_specs(s, bw)

    def body(d_ref, o_ref):
        g = pl.program_id(0)
        dp = d_ref[...]
        t = lax.broadcasted_iota(jnp.int32, (s, tc), 0)

        def up(x, k):
            return jnp.where(t < s - k, pltpu.roll(x, s - k, 0), 0.0)

        sums, cur, k = [], dp / _pool_window(g, t), 1
        for _ in B_WINDOWS:
            cur = cur + up(cur, k)
            sums.append(cur)
            k *= 2
        o_ref[...] = (_pool_select(g, sums) - dp).astype(BF16)

    return pl.pallas_call(
        body, name=name, grid=grid, in_specs=[spec], out_specs=spec,
        out_shape=jax.ShapeDtypeStruct((s, bw), BF16), compiler_params=_params("parallel", "parallel"),
    )(dpooled)


def _scale_bwd(name, dms, mixed, scale):
    s, bw = dms.shape
    tr = _tile(s, ROW_TILE)

    def body(d_ref, m_ref, sc_ref, o_ref, ds_ref):
        @pl.when(pl.program_id(0) == 0)
        def _():
            ds_ref[...] = jnp.zeros_like(ds_ref)

        dv = d_ref[...]
        ds_ref[...] += jnp.sum(dv * m_ref[...], axis=0, keepdims=True)
        o_ref[...] = (dv * sc_ref[...]).astype(BF16)

    return pl.pallas_call(
        body, name=name, grid=(s // tr,), in_specs=[_row_spec(tr, bw), _row_spec(tr, bw), _vec_spec(bw)],
        out_specs=[_row_spec(tr, bw), _vec_spec(bw)],
        out_shape=[jax.ShapeDtypeStruct((s, bw), BF16), jax.ShapeDtypeStruct((1, bw), F32)],
        compiler_params=_params("arbitrary"),
    )(dms, mixed, scale)


def _adam_update(w, g, m, v):
    m = ADAM_B1 * m + (1.0 - ADAM_B1) * g
    v = ADAM_B2 * v + (1.0 - ADAM_B2) * (g * g)
    m_hat = m / (1.0 - ADAM_B1 ** ADAM_STEP)
    v_hat = v / (1.0 - ADAM_B2 ** ADAM_STEP)
    delta = -ADAM_LR * (m_hat / (jnp.sqrt(v_hat) + ADAM_EPS) + ADAM_WD * w)
    return delta, m, v


def _adam_shard(name, w, m, v, parts, layer=0, prev=None, deps=()):
    nl, r, c = w.shape
    nj, nlp = parts.shape[:2]
    tr = _tile(r, ROW_TILE // 2)
    spec = pl.BlockSpec((None, tr, c), lambda l, i: (layer + l, i, 0))

    def body(w_ref, m_ref, v_ref, p_ref, *rest):
        g_ref, d_ref, nm_ref, nv_ref = rest[-4:]
        g = p_ref[0].astype(F32)
        for j in range(1, nj):
            g = g + p_ref[j].astype(F32)
        delta, nm, nv = _adam_update(w_ref[...], g, m_ref[...], v_ref[...])
        g_ref[...] = g
        d_ref[...] = delta
        nm_ref[...] = nm
        nv_ref[...] = nv

    prev = () if prev is None else tuple(prev)
    return pl.pallas_call(
        body, name=name, grid=(nlp, r // tr),
        in_specs=[spec, spec, spec, pl.BlockSpec((nj, None, tr, c), lambda l, i: (0, l, i, 0))]
        + [_ANY] * (len(prev) + len(deps)),
        out_specs=[spec] * 4, out_shape=[jax.ShapeDtypeStruct(w.shape, F32)] * 4,
        input_output_aliases={4 + i: i for i in range(len(prev))},
        compiler_params=_params("parallel", "parallel"),
    )(w, m, v, parts, *prev, *deps)


def _adam_packed(name, w, g, m, v):
    rows, lanes = w.shape
    tr = rows
    spec = pl.BlockSpec((tr, lanes), lambda i: (i, 0))

    def body(w_ref, g_ref, m_ref, v_ref, d_ref, nm_ref, nv_ref):
        delta, nm, nv = _adam_update(w_ref[...], g_ref[...], m_ref[...], v_ref[...])
        d_ref[...] = delta
        nm_ref[...] = nm
        nv_ref[...] = nv

    return pl.pallas_call(
        body, name=name, grid=(rows // tr,), in_specs=[spec] * 4, out_specs=[spec] * 3,
        out_shape=[jax.ShapeDtypeStruct(w.shape, F32)] * 3, compiler_params=_params("parallel"),
    )(w, g, m, v)


def _place():
    x, y, c = lax.axis_index("x"), lax.axis_index("y"), lax.axis_index("c")
    chips = [(1 - x, y), (x, 1 - y), (1 - x, 1 - y)]
    return x, y, c, chips


def _remote(src, dst, send_sem, recv_sem, device):
    return pltpu.make_async_remote_copy(src_ref=src, dst_ref=dst, send_sem=send_sem, recv_sem=recv_sem,
                                        device_id=device, device_id_type=MESH)


def _half(ref_rows, cc):
    h = ref_rows // 2
    return pl.ds(cc * h, h)


_HBM = pl.BlockSpec(memory_space=pltpu.HBM)
_SEM = pl.BlockSpec(memory_space=pltpu.SEMAPHORE)
_EFFECT = pltpu.SideEffectType.DATAFLOW_SIDE_EFFECTING


def _in_hbm(arr):
    return pltpu.with_memory_space_constraint(arr, pltpu.HBM)


def _gather_rows(land, cc, part=(0, 1)):
    k, n = part
    h = land.shape[2] // 2
    return pl.ds(cc * h + k * (h // n), h // n)


def _gather_block(land, split, j, cc, part=(0, 1)):
    if split:
        return land.at[j, :, _gather_rows(land, cc, part), :]
    return land.at[j]


def _split_start(name, srcs, lands, plan, deps=(), groups=None):
    ns = len(srcs)
    nl = len(lands) if groups is None else groups
    both = list(srcs) + list(lands)
    nb = len(both)

    def body(*refs):
        s, ld = refs[:ns], refs[ns:nb]
        outs = refs[nb + len(deps):]
        send, recv, token = outs[:nl], outs[nl:2 * nl], outs[-1]
        for a, copies in enumerate(plan(s, ld)):
            for src, dst, peer in copies:
                _remote(src, dst, send[a], recv[a], peer).start()
        token[...] = jnp.zeros_like(token)

    outs = pl.pallas_call(
        body, name=name, in_specs=[_HBM] * nb + [_ANY] * len(deps),
        out_specs=[_SEM] * (2 * nl) + [_HBM] * nb + [pl.BlockSpec(memory_space=pltpu.VMEM)],
        out_shape=[pltpu.SemaphoreType.DMA(())] * (2 * nl) + [pltpu.HBM(b.shape, b.dtype) for b in both]
        + [jax.ShapeDtypeStruct((8, LANES), F32)],
        input_output_aliases={i: 2 * nl + i for i in range(nb)},
        compiler_params=pltpu.CompilerParams(has_side_effects=_EFFECT),
    )(*[_in_hbm(b) for b in both], *deps)
    thru = outs[2 * nl:2 * nl + nb]
    return list(outs[:nl]), list(outs[nl:2 * nl]), list(thru[:ns]), list(thru[ns:]), outs[-1]


def _split_wait(name, srcs, lands, send, recv, after, whole):
    ns, nl = len(srcs), len(send)
    both = list(srcs) + list(lands)
    nb = len(both)

    def body(*refs):
        ld, snd, rcv = refs[ns:nb], refs[nb:nb + nl], refs[nb + nl:nb + 2 * nl]
        x, y, c, _ = _place()
        for a, blk in enumerate(whole(ld)):
            every = _remote(blk, blk, snd[a], rcv[a], (x, y, c))
            every.wait_send()
            every.wait_recv()

    outs = pl.pallas_call(
        body, name=name, in_specs=[_HBM] * nb + [_SEM] * (2 * nl) + [_ANY], out_specs=[_HBM] * nb,
        out_shape=[pltpu.HBM(b.shape, b.dtype) for b in both],
        input_output_aliases={i: i for i in range(nb)},
        compiler_params=pltpu.CompilerParams(has_side_effects=_EFFECT),
    )(*both, *send, *recv, after)
    return list(outs[:ns]), list(outs[ns:])


def _gather_plan(split, parts):
    def plan(srcs, lands):
        x, y, c, chips = _place()
        out = []
        for ld, sp, n in zip(lands, split, parts):
            for k in range(n):
                blk = _gather_block(ld, sp, 2 * x + y, c, (k, n))
                out.append([(blk, blk, (qx, qy, c)) for qx, qy in chips])
        return out
    return plan


def _gather_whole(split, part=(0, 1)):
    def whole(lands):
        _, _, c, _ = _place()
        return [ld.at[pl.ds(0, 3), :, _gather_rows(ld, c, part), :] if sp else ld.at[pl.ds(0, 3)]
                for ld, sp in zip(lands, split)]
    return whole


def _relay_plan(srcs, lands):
    x, y, c, _ = _place()
    out = []
    for ld in lands:
        blocks = [ld.at[j, :, _half(ld.shape[2], c), :] for j in range(N_CHIPS)]
        out.append([(blk, blk, (x, y, 1 - c)) for blk in blocks])
    return out


def _relay_whole(lands):
    _, _, c, _ = _place()
    return [ld.at[:, :, _half(ld.shape[2], c), :] for ld in lands]


def _fill_own(name, shard, dtype, place, layer=None):
    nl, r, c = shard.shape
    first = 0
    if layer is not None:
        nl, first = 1, layer
    tr = _tile(r, 512)

    def body(p_ref, s_ref, o_ref):
        o_ref[...] = s_ref[...].astype(o_ref.dtype)

    return pl.pallas_call(
        body, name=name,
        grid_spec=pltpu.PrefetchScalarGridSpec(
            num_scalar_prefetch=1, grid=(nl, r // tr),
            in_specs=[pl.BlockSpec((None, tr, c), lambda l, i, p: (first + l, i, 0))],
            out_specs=pl.BlockSpec((None, None, tr, c), lambda l, i, p: (p[0], l, i, 0))),
        out_shape=jax.ShapeDtypeStruct((N_CHIPS, nl, r, c), dtype),
        compiler_params=_params("parallel", "parallel"),
    )(place, shard)


def _gather_finish(name, lands, part=(0, 1)):
    n = len(lands)

    def body(*refs):
        outs = refs[n:2 * n]
        fsend, frecv = refs[2 * n:]
        x, y, c, chips = _place()
        sib = (x, y, 1 - c)

        def relay(a, qi, cc):
            qx, qy = chips[qi]
            blk = _gather_block(outs[a], True, 2 * qx + qy, cc, part)
            return _remote(blk, blk, fsend.at[a, qi], frecv.at[a, qi], sib)

        relays = [relay(a, qi, c) for a in range(n) for qi in range(3)]
        for cp in relays:
            cp.start()
        for a in range(n):
            for qi in range(3):
                relay(a, qi, 1 - c).wait_recv()
        for cp in relays:
            cp.wait_send()

    outs = pl.pallas_call(
        body, name=name, in_specs=[_ANY] * n, out_specs=[_ANY] * n,
        out_shape=[jax.ShapeDtypeStruct(ld.shape, ld.dtype) for ld in lands],
        input_output_aliases={i: i for i in range(n)},
        scratch_shapes=[pltpu.SemaphoreType.DMA((n, 3))] * 2,
    )(*lands)
    return list(outs)


def _pair_plan(srcs, lands):
    x, y, c, _ = _place()
    return [[(s.at[:, :, _half(s.shape[2], 1 - c), :], ld, (x, y, 1 - c))] for s, ld in zip(srcs, lands)]


def _pair_whole(lands):
    return list(lands)


def _scatter_plan(srcs, lands):
    x, y, c, chips = _place()
    return [[(s.at[2 * qx + qy], ld.at[2 * x + y, :, _half(ld.shape[2], c), :], (qx, qy, c)) for qx, qy in chips]
            for s, ld in zip(srcs, lands)]


def _scatter_whole(lands):
    _, _, c, _ = _place()
    return [ld.at[pl.ds(0, 3), :, _half(ld.shape[2], c), :] for ld in lands]


def _pair_sum(name, grad, other, place):
    nj, nl, r, c = grad.shape
    h = r // 2
    tr = _tile(h, 2 * ROW_TILE)
    nb = h // tr

    def body(p_ref, g_ref, o_ref, q_ref, d_ref):
        q = (g_ref[...].astype(F32) + o_ref[...].astype(F32)).astype(BF16)
        q_ref[...] = q

        @pl.when(pl.program_id(2) == p_ref[0])
        def _():
            d_ref[...] = q

    blk = (None, None, tr, c)
    return pl.pallas_call(
        body, name=name,
        grid_spec=pltpu.PrefetchScalarGridSpec(
            num_scalar_prefetch=1, grid=(nl, nb, nj),
            in_specs=[pl.BlockSpec(blk, lambda l, i, j, p: (j, l, p[1] * nb + i, 0)),
                      pl.BlockSpec(blk, lambda l, i, j, p: (j, l, i, 0))],
            out_specs=[pl.BlockSpec(blk, lambda l, i, j, p: (j, l, i, 0)),
                       pl.BlockSpec(blk, lambda l, i, j, p: (p[0], l, p[1] * nb + i, 0))]),
        out_shape=[jax.ShapeDtypeStruct((nj, nl, h, c), BF16), jax.ShapeDtypeStruct((nj, nl, r, c), BF16)],
        compiler_params=_params("parallel", "parallel", "arbitrary"),
    )(place, grad, other)


def _own_of_eight(name, packed, device):
    rows, lanes = packed.shape
    tr = rows

    def body(d_ref, s_ref, o_ref):
        o_ref[...] = s_ref[...]

    return pl.pallas_call(
        body, name=name,
        grid_spec=pltpu.PrefetchScalarGridSpec(
            num_scalar_prefetch=1, grid=(rows // tr,),
            in_specs=[pl.BlockSpec((tr, lanes), lambda i, d: (i, 0))],
            out_specs=pl.BlockSpec((None, tr, lanes), lambda i, d: (d[0], i, 0))),
        out_shape=jax.ShapeDtypeStruct((N_DEV, rows, lanes), packed.dtype),
        compiler_params=_params("parallel"),
    )(device, packed)


def _all_plan(srcs, lands):
    x, y, c, _ = _place()
    (ld,) = lands
    blk = ld.at[4 * x + 2 * y + c]
    flips = [(a, b, d) for a in (0, 1) for b in (0, 1) for d in (0, 1) if a + b + d]
    return [[(blk, blk, (x + a - 2 * a * x, y + b - 2 * b * y, c + d - 2 * d * c)) for a, b, d in flips]]


def _all_whole(lands):
    return [lands[0].at[pl.ds(0, N_DEV - 1)]]


def _sum_of_eight(name, slots):
    n, rows, lanes = slots.shape
    tr = rows

    def body(s_ref, o_ref):
        total = s_ref[0]
        for d in range(1, n):
            total = total + s_ref[d]
        o_ref[...] = total

    return pl.pallas_call(
        body, name=name, grid=(rows // tr,),
        in_specs=[pl.BlockSpec((n, tr, lanes), lambda i: (0, i, 0))],
        out_specs=pl.BlockSpec((tr, lanes), lambda i: (i, 0)),
        out_shape=jax.ShapeDtypeStruct((rows, lanes), F32), compiler_params=_params("parallel"),
    )(slots)


def _pack(parts):
    rows = []
    for p in parts:
        flat = p.reshape(-1)
        pad = (-flat.shape[0]) % PACK_ELEMS
        rows.append(jnp.pad(flat, (0, pad)).reshape(-1, LANES))
    return jnp.concatenate(rows, axis=0)


def _unpack(packed, shapes):
    out, row = [], 0
    for sh in shapes:
        size = math.prod(sh)
        nrows = -(-size // PACK_ELEMS) * (PACK_ELEMS // LANES)
        out.append(packed[row:row + nrows].reshape(-1)[:size].reshape(sh))
        row += nrows
    return out


def kernel(x, a_w_in, a_ln_g, a_ln_b, a_w_s, a_b_s, a_w_out, b_w_in, b_w_grp, b_scale, b_w_out, norm_mix, norm_mlp, mlp_w1, mlp_w2, final_norm, loss_target, m_a_w_in, m_a_ln_g, m_a_ln_b, m_a_w_s, m_a_b_s, m_a_w_out, m_b_w_in, m_b_w_grp, m_b_scale, m_b_w_out, m_norm_mix, m_norm_mlp, m_mlp_w1, m_mlp_w2, m_final_norm, v_a_w_in, v_a_ln_g, v_a_ln_b, v_a_w_s, v_a_b_s, v_a_w_out, v_b_w_in, v_b_w_grp, v_b_scale, v_b_w_out, v_norm_mix, v_norm_mlp, v_mlp_w1, v_mlp_w2, v_final_norm):
    xi, yi, ci = lax.axis_index("x"), lax.axis_index("y"), lax.axis_index("c")
    chip = 2 * xi + yi
    place = jnp.stack([chip, ci]).astype(jnp.int32)
    x2, tgt = x[0], loss_target[0]
    bw = b_scale.shape[1] * N_CHIPS

    units = dict(a_w_in=(a_w_in, None), a_w_out=(a_w_out, None), w1_0=(mlp_w1, 0), w2_0=(mlp_w2, 0),
                 b_scale=(b_scale.reshape(1, 1, -1), None), b_w_in=(b_w_in, None), b_w_grp=(b_w_grp[0], None),
                 b_w_out=(b_w_out, None), w1_1=(mlp_w1, 1), w2_1=(mlp_w2, 1))
    col_sharded = dict(a_w_in=True, a_w_out=False, b_w_in=False, b_w_grp=False, b_w_out=False,
                       w1_0=True, w2_0=False, w1_1=True, w2_1=False)
    in_flight, W = {}, {}

    def launch(tag, keys, deps):
        sp = [k != "b_scale" for k in keys]
        parts = [pieces.get(k, 1) for k in keys]
        zones = [_fill_own(f"gather_own_{k}", units[k][0], BF16 if s else F32, place, layer=units[k][1])
                 for k, s in zip(keys, sp)]
        send, recv, _, zones, tok = _split_start(f"gather_start_{tag}", [], zones, _gather_plan(sp, parts), deps,
                                                 groups=sum(parts))
        first = 0
        for k, z, s, n in zip(keys, zones, sp, parts):
            in_flight[k] = (send[first:first + n], recv[first:first + n], z, s)
            first += n
        return tok

    def arrive_piece(key, k, after):
        send, recv, zone, sp = in_flight[key]
        n = len(send)
        _, zones = _split_wait(f"gather_wait_{key}_{k}", [], [zone], [send[k]], [recv[k]], after,
                               _gather_whole([sp], (k, n)))
        (zone,) = _gather_finish(f"gather_finish_{key}_{k}", zones, (k, n))
        in_flight[key] = (send, recv, zone, sp)
        W[key] = _W4(zone, col_sharded[key])

    def arrive(keys, after):
        send, recv, zones, sp = zip(*[in_flight[k] for k in keys])
        if len(send[0]) > 1:
            for k in range(len(send[0])):
                arrive_piece(keys[0], k, after)
            return
        _, zones = _split_wait(f"gather_wait_{keys[0]}", [], zones, [s[0] for s in send], [r[0] for r in recv],
                               after, _gather_whole(sp))
        relayed = iter(_gather_finish(f"gather_finish_{keys[0]}", [z for z, s in zip(zones, sp) if s]))
        for k, z, s in zip(keys, zones, sp):
            full = next(relayed) if s else z
            W[k] = _W4(full, col_sharded[k]) if k in col_sharded else full

    pieces = dict(w1_0=2, w2_0=2, w1_1=2, w2_1=2)

    token = launch("first", ["a_w_in", "a_w_out"], ())
    token = launch("rest", ["w1_0", "w2_0", "b_scale", "b_w_in", "b_w_grp", "b_w_out", "w1_1", "w2_1"], (token,))

    b_col = a_b_s[0].T

    def residual(acc, res):
        return (res + acc,)

    def sq_relu(acc):
        act = jnp.maximum(acc, 0.0)
        return act, act * act

    def mlp_fwd(tag, h, layer, tail=None):
        hn = _rms_fwd(f"mlp{tag}_norm", h, norm_mlp[layer:layer + 1])
        arrive([f"w1_{layer}"], hn)
        act, act_sq = _mm_aw(f"mlp{tag}_up", hn, W[f"w1_{layer}"], out_dtypes=(BF16, BF16), epilogue=sq_relu)
        out, n = h, pieces[f"w2_{layer}"]
        for k in range(n):
            arrive_piece(f"w2_{layer}", k, act_sq if k == 0 else out)
            if tail is not None and k == n - 1:
                out = tail(act_sq, W[f"w2_{layer}"], out, (k, n))
            else:
                out = _mm_aw(f"mlp{tag}_down_{k}", act_sq, W[f"w2_{layer}"], extras=(out,), epilogue=residual,
                             k_piece=(k, n))
        return out, (h, hn, act, act_sq)

    hn0 = _rms_fwd("mix_a_norm", x2, norm_mix[0:1])
    arrive(["a_w_in"], token)
    zpre = _mm_aw("mix_a_in", hn0, W["a_w_in"])
    gated = _gate_fwd("mix_a_gate", zpre, a_ln_g, a_ln_b, a_w_s[0], b_col)
    arrive(["a_w_out"], gated)
    h1 = _mm_aw("mix_a_out", gated, W["a_w_out"], extras=(x2,), epilogue=residual)
    h2, mlp0 = mlp_fwd("0", h1, 0)
    hn2 = _rms_fwd("mix_b_norm", h2, norm_mix[1:2])
    arrive(["b_scale", "b_w_in"], hn2)
    scale_full = W["b_scale"].reshape(1, bw)
    vb = _mm_aw("mix_b_in", hn2, W["b_w_in"])
    pooled = _pool_fwd("mix_b_pool", vb)
    arrive(["b_w_grp", "b_w_out"], pooled)
    mixed, ms = _mm_aw("mix_b_grp", pooled, W["b_w_grp"], groups=len(B_WINDOWS), extras=(scale_full,),
                       out_dtypes=(F32, BF16), epilogue=lambda acc, sc: (acc, acc * sc))
    h3 = _mm_aw("mix_b_out", ms, W["b_w_out"], extras=(h2,), epilogue=residual)
    (dh4, dh4_b, d_final, loss_part), mlp1 = mlp_fwd(
        "1", h3, 1, tail=lambda act_sq, w2, partial, piece: _down_tail_loss(
            "mlp1_down_loss_head", act_sq, w2, partial, final_norm.reshape(1, -1), tgt, piece))
    g1_like = _W4(None, True, shape=(N_CHIPS, 1, *W["w1_0"].arr.shape[2:]))
    g2_like = _W4(None, False, shape=(N_CHIPS, 1, *W["w2_0"].arr.shape[2:]))

    def exchange(tag, gs):
        zones = [lax.empty((g.shape[0], g.shape[1], g.shape[2] // 2, g.shape[3]), g.dtype) for g in gs]
        send, recv, srcs, zones, tok = _split_start(f"pair_start_{tag}", gs, zones, _pair_plan)
        return (tag, send, recv, srcs, zones), tok

    def reduce(state, after, then_exchange=None):
        tag, send, recv, srcs, zones = state
        srcs, zones = _split_wait(f"pair_wait_{tag}", srcs, zones, send, recv, after, _pair_whole)
        both = [_pair_sum(f"pair_sum_{tag}_{i}", g, o, place) for i, (g, o) in enumerate(zip(srcs, zones))]
        sums, dests = [b[0] for b in both], [b[1] for b in both]
        if then_exchange is None:
            send, recv, sums, dests, tok = _split_start(f"scatter_start_{tag}", sums, dests, _scatter_plan)
            return (tag, send, recv, sums, dests), tok
        other, gs = then_exchange
        n = len(sums)
        zones = [lax.empty((g.shape[0], g.shape[1], g.shape[2] // 2, g.shape[3]), g.dtype) for g in gs]
        send, recv, srcs, lands, tok = _split_start(
            f"scatter_start_{tag}", sums + list(gs), dests + zones,
            lambda s, ld: _scatter_plan(s[:n], ld[:n]) + _pair_plan(s[n:], ld[n:]))
        return ((tag, send[:n], recv[:n], srcs[:n], lands[:n]), tok,
                (other, send[n:], recv[n:], srcs[n:], lands[n:]))

    def relay_all(states, after):
        waited = []
        for tag, send, recv, sums, dests in states:
            waited.append((tag, _split_wait(f"scatter_wait_{tag}", sums, dests, send, recv, after, _scatter_whole)[1]))
        flat = [d for _, ds in waited for d in ds]
        send, recv, _, flat, tok = _split_start(f"relay_start_{waited[0][0]}", [], flat, _relay_plan)
        out, first = [], 0
        for tag, ds in waited:
            last = first + len(ds)
            out.append(((tag, send[first:last], recv[first:last], flat[first:last]), tok))
            first = last
        return out

    def relay(state, after):
        return relay_all([state], after)[0]

    def land(state, after):
        tag, send, recv, dests = state
        return _split_wait(f"relay_wait_{tag}", [], dests, send, recv, after, _relay_whole)[1]

    def mlp_bwd(tag, dh, dh_b, saved, layer, deps, pending=None):
        h, hn, act, act_sq = saved
        dpre = _mm_aw(f"mlp{tag}_down_dx", dh_b, W[f"w2_{layer}"], transpose_w=True, extras=(act,),
                      out_dtypes=(BF16,), epilogue=lambda acc, a: (acc * (2.0 * a),), deps=deps)
        scattering, dw_deps = None, ()
        if pending is not None:
            scattering, tok = reduce(pending, dpre)
            dw_deps = (tok,)
        g_w2 = _mm_dw(f"mlp{tag}_down_dw", act_sq, dh_b, g2_like, deps=dw_deps)
        pair_w2, tok = exchange(f"w2_{layer}", [g_w2])
        dhn = _mm_aw(f"mlp{tag}_up_dx", dpre, W[f"w1_{layer}"], transpose_w=True, deps=(tok,))
        g_w1 = _mm_dw(f"mlp{tag}_up_dw", hn, dpre, g1_like)
        scat_w2, tok2, pair_w1 = reduce(pair_w2, dhn, then_exchange=(f"w1_{layer}", [g_w1]))
        dh_in, dh_in_b, d_norm = _rms_bwd(f"mlp{tag}_norm_bwd", h, norm_mlp[layer:layer + 1], dhn, dh)
        return dh_in, dh_in_b, d_norm, pair_w1, scat_w2, (tok2,), scattering

    dh3, dh3_b, d_norm_mlp1, pair_w1_1, scat_w2_1, toks, _ = mlp_bwd("1", dh4, dh4_b, mlp1, 1, ())
    dms, g_b_out = _mm_bwd("mix_b_out_bwd", dh3_b, W["b_w_out"], ms, deps=toks)
    scat_w1_1, tok = reduce(pair_w1_1, dms)
    dmixed, d_scale = _scale_bwd("mix_b_scale_bwd", dms, mixed, scale_full)
    dpooled = _mm_aw("mix_b_grp_dx", dmixed, W["b_w_grp"], groups=len(B_WINDOWS), transpose_w=True, deps=(tok,))
    g_b_grp = _mm_dw("mix_b_grp_dw", pooled, dmixed, W["b_w_grp"], groups=len(B_WINDOWS))
    dvb = _pool_bwd("mix_b_pool_bwd", dpooled)
    dhn2, g_b_in = _mm_bwd("mix_b_in_bwd", dvb, W["b_w_in"], hn2)
    pair_b, tok = exchange("b", [g_b_out, g_b_grp, g_b_in])
    dh2, dh2_b, d_norm_mix1 = _rms_bwd("mix_b_norm_bwd", h2, norm_mix[1:2], dhn2, dh3)
    dh1, dh1_b, d_norm_mlp0, pair_w1_0, scat_w2_0, toks, scat_b = mlp_bwd("0", dh2, dh2_b, mlp0, 0, (tok,),
                                                                          pending=pair_b)
    dgated, g_a_out = _mm_bwd("mix_a_out_bwd", dh1_b, W["a_w_out"], gated, deps=toks)
    scat_w1_0, tok, pair_a_out = reduce(pair_w1_0, dgated, then_exchange=("a_out", [g_a_out]))
    early = relay_all([scat_w2_1, scat_w1_1, scat_b], dgated)
    dzpre, d_w_s, d_b_col, d_ln_g, d_ln_b = _gate_bwd("mix_a_gate_bwd", zpre, dgated, a_ln_g, a_ln_b, a_w_s[0], b_col)
    dhn0 = _mm_aw("mix_a_in_dx", dzpre, W["a_w_in"], transpose_w=True, deps=(tok, early[0][1]))
    scat_a_out, tok = reduce(pair_a_out, dhn0)
    g_a_in = _mm_dw("mix_a_in_dw", hn0, dzpre, W["a_w_in"], deps=(tok,))
    pair_a_in, tok = exchange("a_in", [g_a_in])
    dx, _, d_norm_mix0 = _rms_bwd("mix_a_norm_bwd", x2, norm_mix[0:1], dhn0, dh1)
    scat_a_in, _ = reduce(pair_a_in, dx)

    small = dict(a_ln_g=(a_ln_g, m_a_ln_g, v_a_ln_g), a_ln_b=(a_ln_b, m_a_ln_b, v_a_ln_b),
                 a_w_s=(a_w_s, m_a_w_s, v_a_w_s), a_b_s=(a_b_s, m_a_b_s, v_a_b_s),
                 b_scale=(b_scale, m_b_scale, v_b_scale), norm_mix=(norm_mix, m_norm_mix, v_norm_mix),
                 norm_mlp=(norm_mlp, m_norm_mlp, v_norm_mlp), final_norm=(final_norm, m_final_norm, v_final_norm))
    small_names = list(small)
    local = dict(a_ln_g=d_ln_g, a_ln_b=d_ln_b, a_w_s=d_w_s[None], a_b_s=d_b_col.T[None], b_scale=d_scale,
                 norm_mix=jnp.concatenate([d_norm_mix0, d_norm_mix1], axis=0),
                 norm_mlp=jnp.concatenate([d_norm_mlp0, d_norm_mlp1], axis=0), final_norm=d_final.reshape(-1))
    device = (4 * xi + 2 * yi + ci).astype(jnp.int32).reshape(1)
    slots = _own_of_eight("small_own", _pack([local[k] for k in small_names]), device)
    small_send, small_recv, _, (slots,), small_tok = _split_start("small_start", [], [slots], _all_plan)

    moments = dict(a_w_in=(m_a_w_in, v_a_w_in), a_w_out=(m_a_w_out, v_a_w_out), b_w_in=(m_b_w_in, v_b_w_in),
                   b_w_grp=(m_b_w_grp, v_b_w_grp), b_w_out=(m_b_w_out, v_b_w_out),
                   mlp_w1=(m_mlp_w1, v_mlp_w1), mlp_w2=(m_mlp_w2, v_mlp_w2))
    weights = dict(a_w_in=a_w_in, a_w_out=a_w_out, b_w_in=b_w_in, b_w_grp=b_w_grp, b_w_out=b_w_out,
                   mlp_w1=mlp_w1, mlp_w2=mlp_w2)
    landing = [(scat_w2_1, [("mlp_w2", 1)]), (scat_w1_1, [("mlp_w1", 1)]),
               (scat_b, [("b_w_out", 0), ("b_w_grp", 0), ("b_w_in", 0)]),
               (scat_w2_0, [("mlp_w2", 0)]), (scat_w1_0, [("mlp_w1", 0)]),
               (scat_a_out, [("a_w_out", 0)]), (scat_a_in, [("a_w_in", 0)])]
    results, after = {}, dx
    relays = list(early)
    for i, (_, members) in enumerate(landing):
        deps = (small_tok,) if i == 0 else ()
        if len(relays) == i + 1 < len(landing):
            relays.append(relay(landing[i + 1][0], after))
            deps = (relays[-1][1],)
        for (k, layer), parts in zip(members, land(relays[i][0], relays[-1][1] if deps else after)):
            shard_shape = (-1, *parts.shape[2:])
            results[k] = _adam_shard(f"adam_{k}_{layer}", weights[k].reshape(shard_shape),
                                     moments[k][0].reshape(shard_shape), moments[k][1].reshape(shard_shape),
                                     parts, layer=layer, prev=results.get(k), deps=deps)
            after = results[k][1]
    grad_out, delta_out, m_out, v_out = {}, {}, {}, {}
    for k, res in results.items():
        grad_out[k], delta_out[k], m_out[k], v_out[k] = [r.reshape(weights[k].shape) for r in res]

    _, (slots,) = _split_wait("small_wait", [], [slots], small_send, small_recv, after, _all_whole)
    reduced = _sum_of_eight("small_sum", slots)
    small_grads = dict(zip(small_names, _unpack(reduced, [local[k].shape for k in small_names])))
    shard_w = b_scale.shape[1]
    small_grads["b_scale"] = lax.dynamic_slice_in_dim(small_grads["b_scale"], chip * shard_w, shard_w, axis=1)
    small_grads = {k: small_grads[k].reshape(small[k][0].shape) for k in small_names}
    packed = [_pack([small[k][i] for k in small_names]) for i in range(3)]
    res = _adam_packed("adam_small", packed[0], _pack([small_grads[k] for k in small_names]), packed[1], packed[2])
    shapes = [small[k][0].shape for k in small_names]
    for k, d, nm, nv in zip(small_names, *[_unpack(r, shapes) for r in res]):
        grad_out[k], delta_out[k], m_out[k], v_out[k] = small_grads[k], d, nm, nv

    loss = lax.psum(loss_part[0, 0], ("x", "y", "c"))
    order = ["a_w_in", "a_ln_g", "a_ln_b", "a_w_s", "a_b_s", "a_w_out", "b_w_in", "b_w_grp", "b_scale", "b_w_out",
             "norm_mix", "norm_mlp", "mlp_w1", "mlp_w2", "final_norm"]
    return (loss, dx[None], *[grad_out[k] for k in order], *[delta_out[k] for k in order],
            *[m_out[k] for k in order], *[v_out[k] for k in order])
```

```python
import math

import jax
import jax.numpy as jnp
from jax import lax
from jax.experimental import pallas as pl
from jax.experimental.pallas import tpu as pltpu

F32 = jnp.float32
BF16 = jnp.bfloat16
MESH = pl.DeviceIdType.MESH

EPS = 1e-6
B_WINDOWS = (2, 4, 8, 16)
ADAM_LR = 0.001
ADAM_B1 = 0.9
ADAM_B2 = 0.999
ADAM_EPS = 1e-08
ADAM_WD = 0.01
ADAM_STEP = 10

N_CHIPS = 4
N_DEV = 8
LANES = 128
PACK_ELEMS = 8 * LANES
VMEM_LIMIT = 56 * 1024 * 1024
ROW_TILE = 512
MM_TM, MM_TN, MM_TK = 1024, 1024, 2048


_ANY = pl.BlockSpec(memory_space=pl.ANY)


def _tile(dim, pref):
    t = min(dim, pref)
    while dim % t:
        t //= 2
    return t


def _params(*sem):
    return pltpu.CompilerParams(dimension_semantics=sem, vmem_limit_bytes=VMEM_LIMIT)


class _W4:
    def __init__(self, arr, col_sharded, shape=None):
        self.arr = arr
        self.nj, self.nl, self.r, self.c = arr.shape if shape is None else shape
        self.col = col_sharded
        self.rows = self.r if col_sharded else self.nj * self.r
        self.cols = self.nj * self.c if col_sharded else self.c

    def tile_rows(self, pref):
        return _tile(self.r, pref)

    def tile_cols(self, pref):
        return _tile(self.c, pref)

    def index(self, layer, rb, cb, tr, tc):
        if self.col:
            n = self.c // tc
            return (cb // n, layer, rb, cb % n)
        n = self.r // tr
        return (rb // n, layer, rb % n, cb)


def _mm_aw(name, a, w, *, layer=0, groups=1, transpose_w=False, extras=(), out_dtypes=(F32,), epilogue=None,
           deps=(), k_piece=None):
    s, ka_total = a.shape
    kdim, ndim = (w.cols, w.rows) if transpose_w else (w.rows, w.cols)
    assert ka_total == groups * kdim, (name, a.shape, kdim, groups)
    span = not w.col and ((not transpose_w and kdim <= MM_TK) or (transpose_w and groups > 1))
    across = transpose_w and w.col and groups == 1 and MM_TK < kdim <= 2 * MM_TK
    if span and transpose_w:
        tm, tn, tk = _tile(s, 2048), ndim, w.tile_cols(MM_TK)
    elif across:
        tm, tn, tk = _tile(s, MM_TM), w.tile_rows(512), kdim
    else:
        tk = kdim if span else (w.tile_cols(MM_TK) if transpose_w else w.tile_rows(MM_TK))
        wide = len(out_dtypes) == 1 and not any(e.shape[0] == s for e in extras) and groups == 1 and s <= 2048
        tm, tn_pref = (_tile(s, 2048), 1024 if wide else 512) if tk == kdim else (_tile(s, MM_TM), MM_TN)
        tn = w.tile_rows(tn_pref) if transpose_w else w.tile_cols(tn_pref)
    nk, nn = kdim // tk, ndim // tn
    if k_piece is not None:
        assert not (w.col or transpose_w or span or groups > 1), name
        piece, n_pieces = k_piece
        tm, tn, tk = _tile(s, MM_TM), w.tile_cols(MM_TN), w.r // (2 * n_pieces)
        nk, nn = 2 * w.nj, ndim // tn

        def k_block(k):
            return k // 2, (k % 2) * n_pieces + piece

    def lay(g):
        return g if groups > 1 else layer

    resident = dict(pipeline_mode=pl.Buffered(1)) if (groups, s // tm, nk) == (1, 1, 1) and k_piece is None else {}
    a_spec = pl.BlockSpec((tm, tk), lambda g, i, n, k: (i, g * nk + k), **resident)
    if k_piece is not None:
        per = w.r // tk
        a_spec = pl.BlockSpec((tm, tk), lambda g, i, n, k: (i, k_block(k)[0] * per + k_block(k)[1]))
        w_spec = pl.BlockSpec((None, None, tk, tn), lambda g, i, n, k: (k_block(k)[0], layer, k_block(k)[1], n))
    elif span and transpose_w:
        w_spec = pl.BlockSpec((w.nj, None, w.r, tk), lambda g, i, n, k: (0, lay(g), 0, k))
    elif span:
        w_spec = pl.BlockSpec((w.nj, None, w.r, tn), lambda g, i, n, k: (0, lay(g), 0, n))
    elif across:
        w_spec = pl.BlockSpec((w.nj, None, tn, w.c), lambda g, i, n, k: (0, layer, n, 0))
    elif transpose_w:
        w_spec = pl.BlockSpec((None, None, tn, tk), lambda g, i, n, k: w.index(lay(g), n, k, tn, tk))
    else:
        w_spec = pl.BlockSpec((None, None, tk, tn), lambda g, i, n, k: w.index(lay(g), k, n, tk, tn))
    ex_specs = []
    for e in extras:
        assert e.shape[1] == groups * ndim and e.shape[0] in (1, s), (name, e.shape)
        if e.shape[0] == 1:
            ex_specs.append(pl.BlockSpec((1, tn), lambda g, i, n, k: (0, g * nn + n)))
        else:
            ex_specs.append(pl.BlockSpec((tm, tn), lambda g, i, n, k: (i, g * nn + n)))
    out_spec = pl.BlockSpec((tm, tn), lambda g, i, n, k: (i, g * nn + n))
    n_ex, n_out, n_dep = len(extras), len(out_dtypes), len(deps)

    def body(a_ref, w_ref, *rest):
        ex, outs = rest[:n_ex], rest[n_ex + n_dep:n_ex + n_dep + n_out]
        av = a_ref[...]
        if av.dtype != BF16:
            av = av.astype(BF16)
        nt = (((1,), (1,)), ((), ()))
        if across:
            prod = lax.dot_general(av[:, :w.c], w_ref[0], nt, preferred_element_type=F32)
            for j in range(1, w.nj):
                prod += lax.dot_general(av[:, j * w.c:(j + 1) * w.c], w_ref[j], nt, preferred_element_type=F32)
        else:
            wv = w_ref[...]
            if span:
                wv = wv.reshape((tn, tk) if transpose_w else (tk, tn))
            prod = lax.dot_general(av, wv, nt, preferred_element_type=F32) if transpose_w else jnp.dot(
                av, wv, preferred_element_type=F32)

        def finish(total):
            vals = (total,) if epilogue is None else epilogue(total, *[e[...] for e in ex])
            for o, v in zip(outs, vals):
                o[...] = v.astype(o.dtype)

        if nk == 1:
            finish(prod)
            return
        acc, k = rest[-1], pl.program_id(3)

        @pl.when(k == 0)
        def _():
            acc[...] = prod

        @pl.when(k > 0)
        def _():
            acc[...] += prod

        @pl.when(k == nk - 1)
        def _():
            finish(acc[...])

    outs = pl.pallas_call(
        body, name=name, grid=(groups, s // tm, nn, nk),
        in_specs=[a_spec, w_spec, *ex_specs] + [_ANY] * n_dep, out_specs=[out_spec] * n_out,
        out_shape=[jax.ShapeDtypeStruct((s, groups * ndim), dt) for dt in out_dtypes],
        scratch_shapes=[pltpu.VMEM((tm, tn), F32)] if nk > 1 else [],
        compiler_params=_params("parallel", "parallel", "parallel", "arbitrary"),
    )(a, w.arr, *extras, *deps)
    return outs[0] if n_out == 1 else outs


def _mm_dw(name, a, b, like, *, layer=0, groups=1, deps=()):
    s, ka_total = a.shape
    rows, cols = ka_total // groups, b.shape[1] // groups
    assert (rows, cols) == (like.rows, like.cols) and b.shape[0] == s, (name, a.shape, b.shape)
    span = groups > 1 and not like.col
    tm, tn, tk = rows if span else like.tile_rows(MM_TM), like.tile_cols(2 * MM_TN), _tile(s, MM_TK)
    nr, nc, nk = rows // tm, cols // tn, s // tk
    assert nk == 1 or not span

    def lay(g):
        return g if groups > 1 else layer

    in_specs = [pl.BlockSpec((tk, tm), lambda g, n, i, k: (k, g * nr + i)),
                pl.BlockSpec((tk, tn), lambda g, n, i, k: (k, g * nc + n))]
    in_specs += [_ANY] * len(deps)

    def body(a_ref, b_ref, *rest):
        av, bv = a_ref[...], b_ref[...]
        if av.dtype != BF16:
            av = av.astype(BF16)
        if bv.dtype != BF16:
            bv = bv.astype(BF16)
        prod = lax.dot_general(av, bv, (((0,), (0,)), ((), ())), preferred_element_type=F32)
        if nk == 1:
            rest[-1][...] = prod.astype(BF16).reshape(rest[-1].shape)
            return
        o_ref, acc, k = rest[-2], rest[-1], pl.program_id(3)

        @pl.when(k == 0)
        def _():
            acc[...] = prod

        @pl.when(k > 0)
        def _():
            acc[...] += prod

        @pl.when(k == nk - 1)
        def _():
            o_ref[...] = acc[...].astype(BF16)

    if span:
        out_spec = pl.BlockSpec((like.nj, None, like.r, tn), lambda g, n, i, k: (0, g, 0, n))
    else:
        out_spec = pl.BlockSpec((None, None, tm, tn), lambda g, n, i, k: like.index(lay(g), i, n, tm, tn))
    return pl.pallas_call(
        body, name=name, grid=(groups, nc, nr, nk), in_specs=in_specs, out_specs=out_spec,
        out_shape=jax.ShapeDtypeStruct((like.nj, like.nl, like.r, like.c), BF16),
        scratch_shapes=[pltpu.VMEM((tm, tn), F32)] if nk > 1 else [],
        compiler_params=_params("parallel", "parallel", "parallel", "arbitrary"),
    )(a, b, *deps)


def _mm_bwd(name, dy, w, a, *, deps=()):
    s, c = dy.shape
    assert not w.col and w.nl == 1 and w.c == c and a.shape == (s, w.rows), (name, dy.shape, a.shape)
    r = w.r

    def body(dy_ref, w_ref, a_ref, *rest):
        dx_ref, dw_ref = rest[-2:]
        dyv = dy_ref[...]
        dx_ref[...] = lax.dot_general(dyv, w_ref[...], (((1,), (1,)), ((), ())), preferred_element_type=F32)
        dw_ref[...] = lax.dot_general(a_ref[...], dyv, (((0,), (0,)), ((), ())),
                                      preferred_element_type=F32).astype(BF16)

    return pl.pallas_call(
        body, name=name, grid=(w.nj,),
        in_specs=[pl.BlockSpec((s, c), lambda j: (0, 0)), pl.BlockSpec((None, None, r, c), lambda j: (j, 0, 0, 0)),
                  pl.BlockSpec((s, r), lambda j: (0, j))] + [_ANY] * len(deps),
        out_specs=[pl.BlockSpec((s, r), lambda j: (0, j)), pl.BlockSpec((None, None, r, c), lambda j: (j, 0, 0, 0))],
        out_shape=[jax.ShapeDtypeStruct((s, w.rows), F32), jax.ShapeDtypeStruct((w.nj, 1, r, c), BF16)],
        compiler_params=_params("parallel"),
    )(dy, w.arr, a, *deps)


def _row_spec(tr, d):
    return pl.BlockSpec((tr, d), lambda i: (i, 0))


def _vec_spec(d):
    return pl.BlockSpec((1, d), lambda i: (0, 0))


def _rms_fwd(name, x, g):
    s, d = x.shape
    tr = _tile(s, ROW_TILE)

    def body(x_ref, g_ref, o_ref):
        xv = x_ref[...]
        r = lax.rsqrt(jnp.mean(xv * xv, axis=-1, keepdims=True) + EPS)
        o_ref[...] = (xv * r * g_ref[...]).astype(BF16)

    return pl.pallas_call(
        body, name=name, grid=(s // tr,), in_specs=[_row_spec(tr, d), _vec_spec(d)],
        out_specs=_row_spec(tr, d), out_shape=jax.ShapeDtypeStruct((s, d), BF16),
        compiler_params=_params("parallel"),
    )(x, g)


def _rms_bwd(name, x, g, dhn, dres):
    s, d = x.shape
    tr = _tile(s, ROW_TILE)

    def body(x_ref, g_ref, dhn_ref, dres_ref, dx_ref, dxb_ref, dg_ref):
        @pl.when(pl.program_id(0) == 0)
        def _():
            dg_ref[...] = jnp.zeros_like(dg_ref)

        xv = x_ref[...]
        r = lax.rsqrt(jnp.mean(xv * xv, axis=-1, keepdims=True) + EPS)
        xh = xv * r
        dy = dhn_ref[...]
        dg_ref[...] += jnp.sum(dy * xh, axis=0, keepdims=True)
        dxh = dy * g_ref[...]
        dx = dres_ref[...] + r * (dxh - xh * jnp.mean(dxh * xh, axis=-1, keepdims=True))
        dx_ref[...] = dx
        dxb_ref[...] = dx.astype(BF16)

    return pl.pallas_call(
        body, name=name, grid=(s // tr,),
        in_specs=[_row_spec(tr, d), _vec_spec(d), _row_spec(tr, d), _row_spec(tr, d)],
        out_specs=[_row_spec(tr, d), _row_spec(tr, d), _vec_spec(d)],
        out_shape=[jax.ShapeDtypeStruct((s, d), F32), jax.ShapeDtypeStruct((s, d), BF16),
                   jax.ShapeDtypeStruct((1, d), F32)],
        compiler_params=_params("arbitrary"),
    )(x, g, dhn, dres)


def _loss_head_tile(hv, gv, tv):
    r = lax.rsqrt(jnp.mean(hv * hv, axis=-1, keepdims=True) + EPS)
    xh = hv * r
    err = xh * gv - tv
    part = 0.5 * jnp.sum(jnp.mean(err * err, axis=-1, keepdims=True), axis=0, keepdims=True)
    dy = err * (1.0 / hv.shape[-1])
    dxh = dy * gv
    dh = r * (dxh - xh * jnp.mean(dxh * xh, axis=-1, keepdims=True))
    return part, jnp.sum(dy * xh, axis=0, keepdims=True), dh


def _down_tail_loss(name, a, w, partial, g, tgt, k_piece):
    s, d = partial.shape
    piece, n_pieces = k_piece
    assert not w.col and w.nl == 1 and w.cols == d and a.shape == (s, w.rows), (name, a.shape)
    tm, tk = _tile(s, 512), w.r // (2 * n_pieces)
    per, nk = w.r // tk, w.nj

    def body(a0_ref, a1_ref, w0_ref, w1_ref, p_ref, g_ref, t_ref, dh_ref, dhb_ref, dg_ref, loss_ref, acc):
        i, k = pl.program_id(0), pl.program_id(1)

        @pl.when(jnp.logical_and(i == 0, k == 0))
        def _():
            dg_ref[...] = jnp.zeros_like(dg_ref)
            loss_ref[...] = jnp.zeros_like(loss_ref)

        prod = jnp.dot(a0_ref[...], w0_ref[...], preferred_element_type=F32)
        prod += jnp.dot(a1_ref[...], w1_ref[...], preferred_element_type=F32)

        @pl.when(k == 0)
        def _():
            acc[...] = prod

        @pl.when(k > 0)
        def _():
            acc[...] += prod

        @pl.when(k == nk - 1)
        def _():
            part, dg, dh = _loss_head_tile(p_ref[...] + acc[...], g_ref[...], t_ref[...])
            loss_ref[...] += jnp.broadcast_to(part, loss_ref.shape)
            dg_ref[...] += dg
            dh_ref[...] = dh
            dhb_ref[...] = dh.astype(BF16)

    rows = pl.BlockSpec((tm, d), lambda i, k: (i, 0))
    return pl.pallas_call(
        body, name=name, grid=(s // tm, nk),
        in_specs=[pl.BlockSpec((tm, tk), lambda i, k: (i, k * per + piece)),
                  pl.BlockSpec((tm, tk), lambda i, k: (i, k * per + n_pieces + piece)),
                  pl.BlockSpec((None, None, tk, d), lambda i, k: (k, 0, piece, 0)),
                  pl.BlockSpec((None, None, tk, d), lambda i, k: (k, 0, n_pieces + piece, 0)),
                  rows, pl.BlockSpec((1, d), lambda i, k: (0, 0)), rows],
        out_specs=[rows, rows, pl.BlockSpec((1, d), lambda i, k: (0, 0)), pl.BlockSpec((1, LANES), lambda i, k: (0, 0))],
        out_shape=[jax.ShapeDtypeStruct((s, d), F32), jax.ShapeDtypeStruct((s, d), BF16),
                   jax.ShapeDtypeStruct((1, d), F32), jax.ShapeDtypeStruct((1, LANES), F32)],
        scratch_shapes=[pltpu.VMEM((tm, d), F32)],
        compiler_params=_params("arbitrary", "arbitrary"),
    )(a, a, w.arr, w.arr, partial, g, tgt)


_SQRT_HALF = 1.0 / math.sqrt(2.0)
_INV_SQRT_2PI = 1.0 / math.sqrt(2.0 * math.pi)


def _gelu(x):
    return x * (lax.erf(x * _SQRT_HALF) + 1.0) * 0.5


def _gelu_grad(x):
    return 0.5 * (lax.erf(x * _SQRT_HALF) + 1.0) + x * jnp.exp(-0.5 * x * x) * _INV_SQRT_2PI


def _causal(chunk):
    t = lax.broadcasted_iota(jnp.int32, (chunk, chunk), 0)
    sidx = lax.broadcasted_iota(jnp.int32, (chunk, chunk), 1)
    return sidx <= t


def _layernorm_parts(v, g, b):
    mu = jnp.mean(v, axis=-1, keepdims=True)
    vc = v - mu
    rs = lax.rsqrt(jnp.mean(vc * vc, axis=-1, keepdims=True) + EPS)
    vhat = vc * rs
    return vhat, rs, vhat * g + b


def _gate_fwd(name, zpre, ln_g, ln_b, w_s, b_col):
    s, aw2 = zpre.shape
    aw = aw2 // 2
    ng, chunk, _ = w_s.shape
    dh = aw // ng

    def body(z_ref, g_ref, b_ref, ws_ref, bc_ref, o_ref):
        u = _gelu(z_ref[:, :aw])
        v = _gelu(z_ref[:, aw:])
        _, _, vln = _layernorm_parts(v, g_ref[...], b_ref[...])
        mask = _causal(chunk)
        for gi in range(ng):
            sl = slice(gi * dh, (gi + 1) * dh)
            wm = jnp.where(mask, ws_ref[gi], 0.0).astype(BF16)
            sg = jnp.dot(wm, vln[:, sl].astype(BF16), preferred_element_type=F32) + bc_ref[:, gi:gi + 1]
            o_ref[:, sl] = (u[:, sl] * sg).astype(BF16)

    return pl.pallas_call(
        body, name=name, grid=(s // chunk,),
        in_specs=[_row_spec(chunk, aw2), _vec_spec(aw), _vec_spec(aw),
                  pl.BlockSpec((ng, chunk, chunk), lambda i: (0, 0, 0)),
                  pl.BlockSpec((chunk, ng), lambda i: (0, 0))],
        out_specs=_row_spec(chunk, aw), out_shape=jax.ShapeDtypeStruct((s, aw), BF16),
        compiler_params=_params("parallel"),
    )(zpre, ln_g, ln_b, w_s, b_col)


def _gate_bwd(name, zpre, dgated, ln_g, ln_b, w_s, b_col):
    s, aw2 = zpre.shape
    aw = aw2 // 2
    ng, chunk, _ = w_s.shape
    dh = aw // ng

    def body(z_ref, dgt_ref, g_ref, b_ref, ws_ref, bc_ref, dz_ref, dws_ref, dbc_ref, dlg_ref, dlb_ref,
             du_scr, dvln_scr):
        @pl.when(pl.program_id(0) == 0)
        def _():
            dws_ref[...] = jnp.zeros_like(dws_ref)
            dbc_ref[...] = jnp.zeros_like(dbc_ref)
            dlg_ref[...] = jnp.zeros_like(dlg_ref)
            dlb_ref[...] = jnp.zeros_like(dlb_ref)

        zu = z_ref[:, :aw]
        zv = z_ref[:, aw:]
        u = _gelu(zu)
        lg = g_ref[...]
        vhat, rs, vln = _layernorm_parts(_gelu(zv), lg, b_ref[...])
        mask = _causal(chunk)
        for gi in range(ng):
            sl = slice(gi * dh, (gi + 1) * dh)
            wm = jnp.where(mask, ws_ref[gi], 0.0).astype(BF16)
            vg = vln[:, sl].astype(BF16)
            sg = jnp.dot(wm, vg, preferred_element_type=F32) + bc_ref[:, gi:gi + 1]
            dgt = dgt_ref[:, sl]
            du_scr[:, sl] = dgt * sg
            ds = dgt * u[:, sl]
            dbc_ref[:, gi:gi + 1] += jnp.sum(ds, axis=-1, keepdims=True)
            dsb = ds.astype(BF16)
            dwm = lax.dot_general(dsb, vg, (((1,), (1,)), ((), ())), preferred_element_type=F32)
            dws_ref[gi] += jnp.where(mask, dwm, 0.0)
            dvln_scr[:, sl] = lax.dot_general(wm, dsb, (((0,), (0,)), ((), ())), preferred_element_type=F32)
        dvln = dvln_scr[...]
        dlb_ref[...] += jnp.sum(dvln, axis=0, keepdims=True)
        dlg_ref[...] += jnp.sum(dvln * vhat, axis=0, keepdims=True)
        dvh = dvln * lg
        dv = rs * (dvh - jnp.mean(dvh, axis=-1, keepdims=True)
                   - vhat * jnp.mean(dvh * vhat, axis=-1, keepdims=True))
        dz_ref[:, :aw] = (du_scr[...] * _gelu_grad(zu)).astype(BF16)
        dz_ref[:, aw:] = (dv * _gelu_grad(zv)).astype(BF16)

    return pl.pallas_call(
        body, name=name, grid=(s // chunk,),
        in_specs=[_row_spec(chunk, aw2), _row_spec(chunk, aw), _vec_spec(aw), _vec_spec(aw),
                  pl.BlockSpec((ng, chunk, chunk), lambda i: (0, 0, 0)),
                  pl.BlockSpec((chunk, ng), lambda i: (0, 0))],
        out_specs=[_row_spec(chunk, aw2), pl.BlockSpec((ng, chunk, chunk), lambda i: (0, 0, 0)),
                   pl.BlockSpec((chunk, ng), lambda i: (0, 0)), _vec_spec(aw), _vec_spec(aw)],
        out_shape=[jax.ShapeDtypeStruct((s, aw2), BF16), jax.ShapeDtypeStruct((ng, chunk, chunk), F32),
                   jax.ShapeDtypeStruct((chunk, ng), F32), jax.ShapeDtypeStruct((1, aw), F32),
                   jax.ShapeDtypeStruct((1, aw), F32)],
        scratch_shapes=[pltpu.VMEM((chunk, aw), F32), pltpu.VMEM((chunk, aw), F32)],
        compiler_params=_params("arbitrary"),
    )(zpre, dgated, ln_g, ln_b, w_s, b_col)


def _pool_select(g, parts):
    out = parts[-1]
    for gi in range(len(parts) - 2, -1, -1):
        out = jnp.where(g == gi, parts[gi], out)
    return out


def _pool_specs(s, bw):
    head = bw // len(B_WINDOWS)
    tc = _tile(head, 256)
    nb = head // tc
    return tc, (len(B_WINDOWS), nb), pl.BlockSpec((s, tc), lambda g, j: (0, g * nb + j))


def _pool_window(g, t):
    w = _pool_select(g, [jnp.full(t.shape, wi, jnp.int32) for wi in B_WINDOWS])
    return jnp.minimum(t + 1, w).astype(F32)


def _pool_fwd(name, vb):
    assert B_WINDOWS == (2, 4, 8, 16)
    s, bw = vb.shape
    tc, grid, spec = _pool_specs(s, bw)

    def body(v_ref, o_ref):
        g = pl.program_id(0)
        v = v_ref[...]
        t = lax.broadcasted_iota(jnp.int32, (s, tc), 0)

        def down(x, k):
            return jnp.where(t >= k, pltpu.roll(x, k, 0), 0.0)

        sums, cur, k = [], v, 1
        for _ in B_WINDOWS:
            cur = cur + down(cur, k)
            sums.append(cur)
            k *= 2
        o_ref[...] = (_pool_select(g, sums) / _pool_window(g, t) - v).astype(BF16)

    return pl.pallas_call(
        body, name=name, grid=grid, in_specs=[spec], out_specs=spec,
        out_shape=jax.ShapeDtypeStruct((s, bw), BF16), compiler_params=_params("parallel", "parallel"),
    )(vb)


def _pool_bwd(name, dpooled):
    s, bw = dpooled.shape
    tc, grid, spec = _pool_specs(s, bw)

    def body(d_ref, o_ref):
        g = pl.program_id(0)
        dp = d_ref[...]
        t = lax.broadcasted_iota(jnp.int32, (s, tc), 0)

        def up(x, k):
            return jnp.where(t < s - k, pltpu.roll(x, s - k, 0), 0.0)

        sums, cur, k = [], dp / _pool_window(g, t), 1
        for _ in B_WINDOWS:
            cur = cur + up(cur, k)
            sums.append(cur)
            k *= 2
        o_ref[...] = (_pool_select(g, sums) - dp).astype(BF16)

    return pl.pallas_call(
        body, name=name, grid=grid, in_specs=[spec], out_specs=spec,
        out_shape=jax.ShapeDtypeStruct((s, bw), BF16), compiler_params=_params("parallel", "parallel"),
    )(dpooled)


def _scale_bwd(name, dms, mixed, scale):
    s, bw = dms.shape
    tr = _tile(s, ROW_TILE)

    def body(d_ref, m_ref, sc_ref, o_ref, ds_ref):
        @pl.when(pl.program_id(0) == 0)
        def _():
            ds_ref[...] = jnp.zeros_like(ds_ref)

        dv = d_ref[...]
        ds_ref[...] += jnp.sum(dv * m_ref[...], axis=0, keepdims=True)
        o_ref[...] = (dv * sc_ref[...]).astype(BF16)

    return pl.pallas_call(
        body, name=name, grid=(s // tr,), in_specs=[_row_spec(tr, bw), _row_spec(tr, bw), _vec_spec(bw)],
        out_specs=[_row_spec(tr, bw), _vec_spec(bw)],
        out_shape=[jax.ShapeDtypeStruct((s, bw), BF16), jax.ShapeDtypeStruct((1, bw), F32)],
        compiler_params=_params("arbitrary"),
    )(dms, mixed, scale)


def _adam_update(w, g, m, v):
    m = ADAM_B1 * m + (1.0 - ADAM_B1) * g
    v = ADAM_B2 * v + (1.0 - ADAM_B2) * (g * g)
    m_hat = m / (1.0 - ADAM_B1 ** ADAM_STEP)
    v_hat = v / (1.0 - ADAM_B2 ** ADAM_STEP)
    delta = -ADAM_LR * (m_hat / (jnp.sqrt(v_hat) + ADAM_EPS) + ADAM_WD * w)
    return delta, m, v


def _adam_shard(name, w, m, v, parts, layer=0, prev=None, deps=()):
    nl, r, c = w.shape
    nj, nlp = parts.shape[:2]
    tr = _tile(r, ROW_TILE // 2)
    spec = pl.BlockSpec((None, tr, c), lambda l, i: (layer + l, i, 0))

    def body(w_ref, m_ref, v_ref, p_ref, *rest):
        g_ref, d_ref, nm_ref, nv_ref = rest[-4:]
        g = p_ref[0].astype(F32)
        for j in range(1, nj):
            g = g + p_ref[j].astype(F32)
        delta, nm, nv = _adam_update(w_ref[...], g, m_ref[...], v_ref[...])
        g_ref[...] = g
        d_ref[...] = delta
        nm_ref[...] = nm
        nv_ref[...] = nv

    prev = () if prev is None else tuple(prev)
    return pl.pallas_call(
        body, name=name, grid=(nlp, r // tr),
        in_specs=[spec, spec, spec, pl.BlockSpec((nj, None, tr, c), lambda l, i: (0, l, i, 0))]
        + [_ANY] * (len(prev) + len(deps)),
        out_specs=[spec] * 4, out_shape=[jax.ShapeDtypeStruct(w.shape, F32)] * 4,
        input_output_aliases={4 + i: i for i in range(len(prev))},
        compiler_params=_params("parallel", "parallel"),
    )(w, m, v, parts, *prev, *deps)


def _adam_packed(name, w, g, m, v):
    rows, lanes = w.shape
    tr = rows
    spec = pl.BlockSpec((tr, lanes), lambda i: (i, 0))

    def body(w_ref, g_ref, m_ref, v_ref, d_ref, nm_ref, nv_ref):
        delta, nm, nv = _adam_update(w_ref[...], g_ref[...], m_ref[...], v_ref[...])
        d_ref[...] = delta
        nm_ref[...] = nm
        nv_ref[...] = nv

    return pl.pallas_call(
        body, name=name, grid=(rows // tr,), in_specs=[spec] * 4, out_specs=[spec] * 3,
        out_shape=[jax.ShapeDtypeStruct(w.shape, F32)] * 3, compiler_params=_params("parallel"),
    )(w, g, m, v)


def _place():
    x, y, c = lax.axis_index("x"), lax.axis_index("y"), lax.axis_index("c")
    chips = [(1 - x, y), (x, 1 - y), (1 - x, 1 - y)]
    return x, y, c, chips


def _remote(src, dst, send_sem, recv_sem, device):
    return pltpu.make_async_remote_copy(src_ref=src, dst_ref=dst, send_sem=send_sem, recv_sem=recv_sem,
                                        device_id=device, device_id_type=MESH)


def _half(ref_rows, cc):
    h = ref_rows // 2
    return pl.ds(cc * h, h)


_HBM = pl.BlockSpec(memory_space=pltpu.HBM)
_SEM = pl.BlockSpec(memory_space=pltpu.SEMAPHORE)
_EFFECT = pltpu.SideEffectType.DATAFLOW_SIDE_EFFECTING


def _in_hbm(arr):
    return pltpu.with_memory_space_constraint(arr, pltpu.HBM)


def _gather_rows(land, cc, part=(0, 1)):
    k, n = part
    h = land.shape[2] // 2
    return pl.ds(cc * h + k * (h // n), h // n)


def _gather_block(land, split, j, cc, part=(0, 1)):
    if split:
        return land.at[j, :, _gather_rows(land, cc, part), :]
    return land.at[j]


def _split_start(name, srcs, lands, plan, deps=(), groups=None):
    ns = len(srcs)
    nl = len(lands) if groups is None else groups
    both = list(srcs) + list(lands)
    nb = len(both)

    def body(*refs):
        s, ld = refs[:ns], refs[ns:nb]
        outs = refs[nb + len(deps):]
        send, recv, token = outs[:nl], outs[nl:2 * nl], outs[-1]
        for a, copies in enumerate(plan(s, ld)):
            for src, dst, peer in copies:
                _remote(src, dst, send[a], recv[a], peer).start()
        token[...] = jnp.zeros_like(token)

    outs = pl.pallas_call(
        body, name=name, in_specs=[_HBM] * nb + [_ANY] * len(deps),
        out_specs=[_SEM] * (2 * nl) + [_HBM] * nb + [pl.BlockSpec(memory_space=pltpu.VMEM)],
        out_shape=[pltpu.SemaphoreType.DMA(())] * (2 * nl) + [pltpu.HBM(b.shape, b.dtype) for b in both]
        + [jax.ShapeDtypeStruct((8, LANES), F32)],
        input_output_aliases={i: 2 * nl + i for i in range(nb)},
        compiler_params=pltpu.CompilerParams(has_side_effects=_EFFECT),
    )(*[_in_hbm(b) for b in both], *deps)
    thru = outs[2 * nl:2 * nl + nb]
    return list(outs[:nl]), list(outs[nl:2 * nl]), list(thru[:ns]), list(thru[ns:]), outs[-1]


def _split_wait(name, srcs, lands, send, recv, after, whole):
    ns, nl = len(srcs), len(send)
    both = list(srcs) + list(lands)
    nb = len(both)

    def body(*refs):
        ld, snd, rcv = refs[ns:nb], refs[nb:nb + nl], refs[nb + nl:nb + 2 * nl]
        x, y, c, _ = _place()
        for a, blk in enumerate(whole(ld)):
            every = _remote(blk, blk, snd[a], rcv[a], (x, y, c))
            every.wait_send()
            every.wait_recv()

    outs = pl.pallas_call(
        body, name=name, in_specs=[_HBM] * nb + [_SEM] * (2 * nl) + [_ANY], out_specs=[_HBM] * nb,
        out_shape=[pltpu.HBM(b.shape, b.dtype) for b in both],
        input_output_aliases={i: i for i in range(nb)},
        compiler_params=pltpu.CompilerParams(has_side_effects=_EFFECT),
    )(*both, *send, *recv, after)
    return list(outs[:ns]), list(outs[ns:])


def _gather_plan(split, parts):
    def plan(srcs, lands):
        x, y, c, chips = _place()
        out = []
        for ld, sp, n in zip(lands, split, parts):
            for k in range(n):
                blk = _gather_block(ld, sp, 2 * x + y, c, (k, n))
                out.append([(blk, blk, (qx, qy, c)) for qx, qy in chips])
        return out
    return plan


def _gather_whole(split, part=(0, 1)):
    def whole(lands):
        _, _, c, _ = _place()
        return [ld.at[pl.ds(0, 3), :, _gather_rows(ld, c, part), :] if sp else ld.at[pl.ds(0, 3)]
                for ld, sp in zip(lands, split)]
    return whole


def _relay_plan(srcs, lands):
    x, y, c, _ = _place()
    out = []
    for ld in lands:
        blocks = [ld.at[j, :, _half(ld.shape[2], c), :] for j in range(N_CHIPS)]
        out.append([(blk, blk, (x, y, 1 - c)) for blk in blocks])
    return out


def _relay_whole(lands):
    _, _, c, _ = _place()
    return [ld.at[:, :, _half(ld.shape[2], c), :] for ld in lands]


def _fill_own(name, shard, dtype, place, layer=None):
    nl, r, c = shard.shape
    first = 0
    if layer is not None:
        nl, first = 1, layer
    tr = _tile(r, 512)

    def body(p_ref, s_ref, o_ref):
        o_ref[...] = s_ref[...].astype(o_ref.dtype)

    return pl.pallas_call(
        body, name=name,
        grid_spec=pltpu.PrefetchScalarGridSpec(
            num_scalar_prefetch=1, grid=(nl, r // tr),
            in_specs=[pl.BlockSpec((None, tr, c), lambda l, i, p: (first + l, i, 0))],
            out_specs=pl.BlockSpec((None, None, tr, c), lambda l, i, p: (p[0], l, i, 0))),
        out_shape=jax.ShapeDtypeStruct((N_CHIPS, nl, r, c), dtype),
        compiler_params=_params("parallel", "parallel"),
    )(place, shard)


def _gather_finish(name, lands, part=(0, 1)):
    n = len(lands)

    def body(*refs):
        outs = refs[n:2 * n]
        fsend, frecv = refs[2 * n:]
        x, y, c, chips = _place()
        sib = (x, y, 1 - c)

        def relay(a, qi, cc):
            qx, qy = chips[qi]
            blk = _gather_block(outs[a], True, 2 * qx + qy, cc, part)
            return _remote(blk, blk, fsend.at[a, qi], frecv.at[a, qi], sib)

        relays = [relay(a, qi, c) for a in range(n) for qi in range(3)]
        for cp in relays:
            cp.start()
        for a in range(n):
            for qi in range(3):
                relay(a, qi, 1 - c).wait_recv()
        for cp in relays:
            cp.wait_send()

    outs = pl.pallas_call(
        body, name=name, in_specs=[_ANY] * n, out_specs=[_ANY] * n,
        out_shape=[jax.ShapeDtypeStruct(ld.shape, ld.dtype) for ld in lands],
        input_output_aliases={i: i for i in range(n)},
        scratch_shapes=[pltpu.SemaphoreType.DMA((n, 3))] * 2,
    )(*lands)
    return list(outs)


def _pair_plan(srcs, lands):
    x, y, c, _ = _place()
    return [[(s.at[:, :, _half(s.shape[2], 1 - c), :], ld, (x, y, 1 - c))] for s, ld in zip(srcs, lands)]


def _pair_whole(lands):
    return list(lands)


def _scatter_plan(srcs, lands):
    x, y, c, chips = _place()
    return [[(s.at[2 * qx + qy], ld.at[2 * x + y, :, _half(ld.shape[2], c), :], (qx, qy, c)) for qx, qy in chips]
            for s, ld in zip(srcs, lands)]


def _scatter_whole(lands):
    _, _, c, _ = _place()
    return [ld.at[pl.ds(0, 3), :, _half(ld.shape[2], c), :] for ld in lands]


def _pair_sum(name, grad, other, place):
    nj, nl, r, c = grad.shape
    h = r // 2
    tr = _tile(h, 2 * ROW_TILE)
    nb = h // tr

    def body(p_ref, g_ref, o_ref, q_ref, d_ref):
        q = (g_ref[...].astype(F32) + o_ref[...].astype(F32)).astype(BF16)
        q_ref[...] = q

        @pl.when(pl.program_id(2) == p_ref[0])
        def _():
            d_ref[...] = q

    blk = (None, None, tr, c)
    return pl.pallas_call(
        body, name=name,
        grid_spec=pltpu.PrefetchScalarGridSpec(
            num_scalar_prefetch=1, grid=(nl, nb, nj),
            in_specs=[pl.BlockSpec(blk, lambda l, i, j, p: (j, l, p[1] * nb + i, 0)),
                      pl.BlockSpec(blk, lambda l, i, j, p: (j, l, i, 0))],
            out_specs=[pl.BlockSpec(blk, lambda l, i, j, p: (j, l, i, 0)),
                       pl.BlockSpec(blk, lambda l, i, j, p: (p[0], l, p[1] * nb + i, 0))]),
        out_shape=[jax.ShapeDtypeStruct((nj, nl, h, c), BF16), jax.ShapeDtypeStruct((nj, nl, r, c), BF16)],
        compiler_params=_params("parallel", "parallel", "arbitrary"),
    )(place, grad, other)


def _own_of_eight(name, packed, device):
    rows, lanes = packed.shape
    tr = rows

    def body(d_ref, s_ref, o_ref):
        o_ref[...] = s_ref[...]

    return pl.pallas_call(
        body, name=name,
        grid_spec=pltpu.PrefetchScalarGridSpec(
            num_scalar_prefetch=1, grid=(rows // tr,),
            in_specs=[pl.BlockSpec((tr, lanes), lambda i, d: (i, 0))],
            out_specs=pl.BlockSpec((None, tr, lanes), lambda i, d: (d[0], i, 0))),
        out_shape=jax.ShapeDtypeStruct((N_DEV, rows, lanes), packed.dtype),
        compiler_params=_params("parallel"),
    )(device, packed)


def _all_plan(srcs, lands):
    x, y, c, _ = _place()
    (ld,) = lands
    blk = ld.at[4 * x + 2 * y + c]
    flips = [(a, b, d) for a in (0, 1) for b in (0, 1) for d in (0, 1) if a + b + d]
    return [[(blk, blk, (x + a - 2 * a * x, y + b - 2 * b * y, c + d - 2 * d * c)) for a, b, d in flips]]


def _all_whole(lands):
    return [lands[0].at[pl.ds(0, N_DEV - 1)]]


def _sum_of_eight(name, slots):
    n, rows, lanes = slots.shape
    tr = rows

    def body(s_ref, o_ref):
        total = s_ref[0]
        for d in range(1, n):
            total = total + s_ref[d]
        o_ref[...] = total

    return pl.pallas_call(
        body, name=name, grid=(rows // tr,),
        in_specs=[pl.BlockSpec((n, tr, lanes), lambda i: (0, i, 0))],
        out_specs=pl.BlockSpec((tr, lanes), lambda i: (i, 0)),
        out_shape=jax.ShapeDtypeStruct((rows, lanes), F32), compiler_params=_params("parallel"),
    )(slots)


def _pack(parts):
    rows = []
    for p in parts:
        flat = p.reshape(-1)
        pad = (-flat.shape[0]) % PACK_ELEMS
        rows.append(jnp.pad(flat, (0, pad)).reshape(-1, LANES))
    return jnp.concatenate(rows, axis=0)


def _unpack(packed, shapes):
    out, row = [], 0
    for sh in shapes:
        size = math.prod(sh)
        nrows = -(-size // PACK_ELEMS) * (PACK_ELEMS // LANES)
        out.append(packed[row:row + nrows].reshape(-1)[:size].reshape(sh))
        row += nrows
    return out


def kernel(x, a_w_in, a_ln_g, a_ln_b, a_w_s, a_b_s, a_w_out, b_w_in, b_w_grp, b_scale, b_w_out, norm_mix, norm_mlp, mlp_w1, mlp_w2, final_norm, loss_target, m_a_w_in, m_a_ln_g, m_a_ln_b, m_a_w_s, m_a_b_s, m_a_w_out, m_b_w_in, m_b_w_grp, m_b_scale, m_b_w_out, m_norm_mix, m_norm_mlp, m_mlp_w1, m_mlp_w2, m_final_norm, v_a_w_in, v_a_ln_g, v_a_ln_b, v_a_w_s, v_a_b_s, v_a_w_out, v_b_w_in, v_b_w_grp, v_b_scale, v_b_w_out, v_norm_mix, v_norm_mlp, v_mlp_w1, v_mlp_w2, v_final_norm):
    xi, yi, ci = lax.axis_index("x"), lax.axis_index("y"), lax.axis_index("c")
    chip = 2 * xi + yi
    place = jnp.stack([chip, ci]).astype(jnp.int32)
    x2, tgt = x[0], loss_target[0]
    bw = b_scale.shape[1] * N_CHIPS

    units = dict(a_w_in=(a_w_in, None), a_w_out=(a_w_out, None), w1_0=(mlp_w1, 0), w2_0=(mlp_w2, 0),
                 b_scale=(b_scale.reshape(1, 1, -1), None), b_w_in=(b_w_in, None), b_w_grp=(b_w_grp[0], None),
                 b_w_out=(b_w_out, None), w1_1=(mlp_w1, 1), w2_1=(mlp_w2, 1))
    col_sharded = dict(a_w_in=True, a_w_out=False, b_w_in=False, b_w_grp=False, b_w_out=False,
                       w1_0=True, w2_0=False, w1_1=True, w2_1=False)
    in_flight, W = {}, {}

    def launch(tag, keys, deps):
        sp = [k != "b_scale" for k in keys]
        parts = [pieces.get(k, 1) for k in keys]
        zones = [_fill_own(f"gather_own_{k}", units[k][0], BF16 if s else F32, place, layer=units[k][1])
                 for k, s in zip(keys, sp)]
        send, recv, _, zones, tok = _split_start(f"gather_start_{tag}", [], zones, _gather_plan(sp, parts), deps,
                                                 groups=sum(parts))
        first = 0
        for k, z, s, n in zip(keys, zones, sp, parts):
            in_flight[k] = (send[first:first + n], recv[first:first + n], z, s)
            first += n
        return tok

    def arrive_piece(key, k, after):
        send, recv, zone, sp = in_flight[key]
        n = len(send)
        _, zones = _split_wait(f"gather_wait_{key}_{k}", [], [zone], [send[k]], [recv[k]], after,
                               _gather_whole([sp], (k, n)))
        (zone,) = _gather_finish(f"gather_finish_{key}_{k}", zones, (k, n))
        in_flight[key] = (send, recv, zone, sp)
        W[key] = _W4(zone, col_sharded[key])

    def arrive(keys, after):
        send, recv, zones, sp = zip(*[in_flight[k] for k in keys])
        if len(send[0]) > 1:
            for k in range(len(send[0])):
                arrive_piece(keys[0], k, after)
            return
        _, zones = _split_wait(f"gather_wait_{keys[0]}", [], zones, [s[0] for s in send], [r[0] for r in recv],
                               after, _gather_whole(sp))
        relayed = iter(_gather_finish(f"gather_finish_{keys[0]}", [z for z, s in zip(zones, sp) if s]))
        for k, z, s in zip(keys, zones, sp):
            full = next(relayed) if s else z
            W[k] = _W4(full, col_sharded[k]) if k in col_sharded else full

    pieces = dict(w1_0=2, w2_0=2, w1_1=2, w2_1=2)

    token = launch("first", ["a_w_in", "a_w_out"], ())
    token = launch("rest", ["w1_0", "w2_0", "b_scale", "b_w_in", "b_w_grp", "b_w_out", "w1_1", "w2_1"], (token,))

    b_col = a_b_s[0].T

    def residual(acc, res):
        return (res + acc,)

    def sq_relu(acc):
        act = jnp.maximum(acc, 0.0)
        return act, act * act

    def mlp_fwd(tag, h, layer, tail=None):
        hn = _rms_fwd(f"mlp{tag}_norm", h, norm_mlp[layer:layer + 1])
        arrive([f"w1_{layer}"], hn)
        act, act_sq = _mm_aw(f"mlp{tag}_up", hn, W[f"w1_{layer}"], out_dtypes=(BF16, BF16), epilogue=sq_relu)
        out, n = h, pieces[f"w2_{layer}"]
        for k in range(n):
            arrive_piece(f"w2_{layer}", k, act_sq if k == 0 else out)
            if tail is not None and k == n - 1:
                out = tail(act_sq, W[f"w2_{layer}"], out, (k, n))
            else:
                out = _mm_aw(f"mlp{tag}_down_{k}", act_sq, W[f"w2_{layer}"], extras=(out,), epilogue=residual,
                             k_piece=(k, n))
        return out, (h, hn, act, act_sq)

    hn0 = _rms_fwd("mix_a_norm", x2, norm_mix[0:1])
    arrive(["a_w_in"], token)
    zpre = _mm_aw("mix_a_in", hn0, W["a_w_in"])
    gated = _gate_fwd("mix_a_gate", zpre, a_ln_g, a_ln_b, a_w_s[0], b_col)
    arrive(["a_w_out"], gated)
    h1 = _mm_aw("mix_a_out", gated, W["a_w_out"], extras=(x2,), epilogue=residual)
    h2, mlp0 = mlp_fwd("0", h1, 0)
    hn2 = _rms_fwd("mix_b_norm", h2, norm_mix[1:2])
    arrive(["b_scale", "b_w_in"], hn2)
    scale_full = W["b_scale"].reshape(1, bw)
    vb = _mm_aw("mix_b_in", hn2, W["b_w_in"])
    pooled = _pool_fwd("mix_b_pool", vb)
    arrive(["b_w_grp", "b_w_out"], pooled)
    mixed, ms = _mm_aw("mix_b_grp", pooled, W["b_w_grp"], groups=len(B_WINDOWS), extras=(scale_full,),
                       out_dtypes=(F32, BF16), epilogue=lambda acc, sc: (acc, acc * sc))
    h3 = _mm_aw("mix_b_out", ms, W["b_w_out"], extras=(h2,), epilogue=residual)
    (dh4, dh4_b, d_final, loss_part), mlp1 = mlp_fwd(
        "1", h3, 1, tail=lambda act_sq, w2, partial, piece: _down_tail_loss(
            "mlp1_down_loss_head", act_sq, w2, partial, final_norm.reshape(1, -1), tgt, piece))
    g1_like = _W4(None, True, shape=(N_CHIPS, 1, *W["w1_0"].arr.shape[2:]))
    g2_like = _W4(None, False, shape=(N_CHIPS, 1, *W["w2_0"].arr.shape[2:]))

    def exchange(tag, gs):
        zones = [lax.empty((g.shape[0], g.shape[1], g.shape[2] // 2, g.shape[3]), g.dtype) for g in gs]
        send, recv, srcs, zones, tok = _split_start(f"pair_start_{tag}", gs, zones, _pair_plan)
        return (tag, send, recv, srcs, zones), tok

    def reduce(state, after, then_exchange=None):
        tag, send, recv, srcs, zones = state
        srcs, zones = _split_wait(f"pair_wait_{tag}", srcs, zones, send, recv, after, _pair_whole)
        both = [_pair_sum(f"pair_sum_{tag}_{i}", g, o, place) for i, (g, o) in enumerate(zip(srcs, zones))]
        sums, dests = [b[0] for b in both], [b[1] for b in both]
        if then_exchange is None:
            send, recv, sums, dests, tok = _split_start(f"scatter_start_{tag}", sums, dests, _scatter_plan)
            return (tag, send, recv, sums, dests), tok
        other, gs = then_exchange
        n = len(sums)
        zones = [lax.empty((g.shape[0], g.shape[1], g.shape[2] // 2, g.shape[3]), g.dtype) for g in gs]
        send, recv, srcs, lands, tok = _split_start(
            f"scatter_start_{tag}", sums + list(gs), dests + zones,
            lambda s, ld: _scatter_plan(s[:n], ld[:n]) + _pair_plan(s[n:], ld[n:]))
        return ((tag, send[:n], recv[:n], srcs[:n], lands[:n]), tok,
                (other, send[n:], recv[n:], srcs[n:], lands[n:]))

    def relay_all(states, after):
        waited = []
        for tag, send, recv, sums, dests in states:
            waited.append((tag, _split_wait(f"scatter_wait_{tag}", sums, dests, send, recv, after, _scatter_whole)[1]))
        flat = [d for _, ds in waited for d in ds]
        send, recv, _, flat, tok = _split_start(f"relay_start_{waited[0][0]}", [], flat, _relay_plan)
        out, first = [], 0
        for tag, ds in waited:
            last = first + len(ds)
            out.append(((tag, send[first:last], recv[first:last], flat[first:last]), tok))
            first = last
        return out

    def relay(state, after):
        return relay_all([state], after)[0]

    def land(state, after):
        tag, send, recv, dests = state
        return _split_wait(f"relay_wait_{tag}", [], dests, send, recv, after, _relay_whole)[1]

    def mlp_bwd(tag, dh, dh_b, saved, layer, deps, pending=None):
        h, hn, act, act_sq = saved
        dpre = _mm_aw(f"mlp{tag}_down_dx", dh_b, W[f"w2_{layer}"], transpose_w=True, extras=(act,),
                      out_dtypes=(BF16,), epilogue=lambda acc, a: (acc * (2.0 * a),), deps=deps)
        scattering, dw_deps = None, ()
        if pending is not None:
            scattering, tok = reduce(pending, dpre)
            dw_deps = (tok,)
        g_w2 = _mm_dw(f"mlp{tag}_down_dw", act_sq, dh_b, g2_like, deps=dw_deps)
        pair_w2, tok = exchange(f"w2_{layer}", [g_w2])
        dhn = _mm_aw(f"mlp{tag}_up_dx", dpre, W[f"w1_{layer}"], transpose_w=True, deps=(tok,))
        g_w1 = _mm_dw(f"mlp{tag}_up_dw", hn, dpre, g1_like)
        scat_w2, tok2, pair_w1 = reduce(pair_w2, dhn, then_exchange=(f"w1_{layer}", [g_w1]))
        dh_in, dh_in_b, d_norm = _rms_bwd(f"mlp{tag}_norm_bwd", h, norm_mlp[layer:layer + 1], dhn, dh)
        return dh_in, dh_in_b, d_norm, pair_w1, scat_w2, (tok2,), scattering

    dh3, dh3_b, d_norm_mlp1, pair_w1_1, scat_w2_1, toks, _ = mlp_bwd("1", dh4, dh4_b, mlp1, 1, ())
    dms, g_b_out = _mm_bwd("mix_b_out_bwd", dh3_b, W["b_w_out"], ms, deps=toks)
    scat_w1_1, tok = reduce(pair_w1_1, dms)
    dmixed, d_scale = _scale_bwd("mix_b_scale_bwd", dms, mixed, scale_full)
    dpooled = _mm_aw("mix_b_grp_dx", dmixed, W["b_w_grp"], groups=len(B_WINDOWS), transpose_w=True, deps=(tok,))
    g_b_grp = _mm_dw("mix_b_grp_dw", pooled, dmixed, W["b_w_grp"], groups=len(B_WINDOWS))
    dvb = _pool_bwd("mix_b_pool_bwd", dpooled)
    dhn2, g_b_in = _mm_bwd("mix_b_in_bwd", dvb, W["b_w_in"], hn2)
    pair_b, tok = exchange("b", [g_b_out, g_b_grp, g_b_in])
    dh2, dh2_b, d_norm_mix1 = _rms_bwd("mix_b_norm_bwd", h2, norm_mix[1:2], dhn2, dh3)
    dh1, dh1_b, d_norm_mlp0, pair_w1_0, scat_w2_0, toks, scat_b = mlp_bwd("0", dh2, dh2_b, mlp0, 0, (tok,),
                                                                          pending=pair_b)
    dgated, g_a_out = _mm_bwd("mix_a_out_bwd", dh1_b, W["a_w_out"], gated, deps=toks)
    scat_w1_0, tok, pair_a_out = reduce(pair_w1_0, dgated, then_exchange=("a_out", [g_a_out]))
    early = relay_all([scat_w2_1, scat_w1_1, scat_b], dgated)
    dzpre, d_w_s, d_b_col, d_ln_g, d_ln_b = _gate_bwd("mix_a_gate_bwd", zpre, dgated, a_ln_g, a_ln_b, a_w_s[0], b_col)
    dhn0 = _mm_aw("mix_a_in_dx", dzpre, W["a_w_in"], transpose_w=True, deps=(tok, early[0][1]))
    scat_a_out, tok = reduce(pair_a_out, dhn0)
    g_a_in = _mm_dw("mix_a_in_dw", hn0, dzpre, W["a_w_in"], deps=(tok,))
    pair_a_in, tok = exchange("a_in", [g_a_in])
    dx, _, d_norm_mix0 = _rms_bwd("mix_a_norm_bwd", x2, norm_mix[0:1], dhn0, dh1)
    scat_a_in, _ = reduce(pair_a_in, dx)

    small = dict(a_ln_g=(a_ln_g, m_a_ln_g, v_a_ln_g), a_ln_b=(a_ln_b, m_a_ln_b, v_a_ln_b),
                 a_w_s=(a_w_s, m_a_w_s, v_a_w_s), a_b_s=(a_b_s, m_a_b_s, v_a_b_s),
                 b_scale=(b_scale, m_b_scale, v_b_scale), norm_mix=(norm_mix, m_norm_mix, v_norm_mix),
                 norm_mlp=(norm_mlp, m_norm_mlp, v_norm_mlp), final_norm=(final_norm, m_final_norm, v_final_norm))
    small_names = list(small)
    local = dict(a_ln_g=d_ln_g, a_ln_b=d_ln_b, a_w_s=d_w_s[None], a_b_s=d_b_col.T[None], b_scale=d_scale,
                 norm_mix=jnp.concatenate([d_norm_mix0, d_norm_mix1], axis=0),
                 norm_mlp=jnp.concatenate([d_norm_mlp0, d_norm_mlp1], axis=0), final_norm=d_final.reshape(-1))
    device = (4 * xi + 2 * yi + ci).astype(jnp.int32).reshape(1)
    slots = _own_of_eight("small_own", _pack([local[k] for k in small_names]), device)
    small_send, small_recv, _, (slots,), small_tok = _split_start("small_start", [], [slots], _all_plan)

    moments = dict(a_w_in=(m_a_w_in, v_a_w_in), a_w_out=(m_a_w_out, v_a_w_out), b_w_in=(m_b_w_in, v_b_w_in),
                   b_w_grp=(m_b_w_grp, v_b_w_grp), b_w_out=(m_b_w_out, v_b_w_out),
                   mlp_w1=(m_mlp_w1, v_mlp_w1), mlp_w2=(m_mlp_w2, v_mlp_w2))
    weights = dict(a_w_in=a_w_in, a_w_out=a_w_out, b_w_in=b_w_in, b_w_grp=b_w_grp, b_w_out=b_w_out,
                   mlp_w1=mlp_w1, mlp_w2=mlp_w2)
    landing = [(scat_w2_1, [("mlp_w2", 1)]), (scat_w1_1, [("mlp_w1", 1)]),
               (scat_b, [("b_w_out", 0), ("b_w_grp", 0), ("b_w_in", 0)]),
               (scat_w2_0, [("mlp_w2", 0)]), (scat_w1_0, [("mlp_w1", 0)]),
               (scat_a_out, [("a_w_out", 0)]), (scat_a_in, [("a_w_in", 0)])]
    results, after = {}, dx
    relays = list(early)
    for i, (_, members) in enumerate(landing):
        deps = (small_tok,) if i == 0 else ()
        if len(relays) == i + 1 < len(landing):
            relays.append(relay(landing[i + 1][0], after))
            deps = (relays[-1][1],)
        for (k, layer), parts in zip(members, land(relays[i][0], relays[-1][1] if deps else after)):
            shard_shape = (-1, *parts.shape[2:])
            results[k] = _adam_shard(f"adam_{k}_{layer}", weights[k].reshape(shard_shape),
                                     moments[k][0].reshape(shard_shape), moments[k][1].reshape(shard_shape),
                                     parts, layer=layer, prev=results.get(k), deps=deps)
            after = results[k][1]
    grad_out, delta_out, m_out, v_out = {}, {}, {}, {}
    for k, res in results.items():
        grad_out[k], delta_out[k], m_out[k], v_out[k] = [r.reshape(weights[k].shape) for r in res]

    _, (slots,) = _split_wait("small_wait", [], [slots], small_send, small_recv, after, _all_whole)
    reduced = _sum_of_eight("small_sum", slots)
    small_grads = dict(zip(small_names, _unpack(reduced, [local[k].shape for k in small_names])))
    shard_w = b_scale.shape[1]
    small_grads["b_scale"] = lax.dynamic_slice_in_dim(small_grads["b_scale"], chip * shard_w, shard_w, axis=1)
    small_grads = {k: small_grads[k].reshape(small[k][0].shape) for k in small_names}
    packed = [_pack([small[k][i] for k in small_names]) for i in range(3)]
    res = _adam_packed("adam_small", packed[0], _pack([small_grads[k] for k in small_names]), packed[1], packed[2])
    shapes = [small[k][0].shape for k in small_names]
    for k, d, nm, nv in zip(small_names, *[_unpack(r, shapes) for r in res]):
        grad_out[k], delta_out[k], m_out[k], v_out[k] = small_grads[k], d, nm, nv

    loss = lax.psum(loss_part[0, 0], ("x", "y", "c"))
    order = ["a_w_in", "a_ln_g", "a_ln_b", "a_w_s", "a_b_s", "a_w_out", "b_w_in", "b_w_grp", "b_scale", "b_w_out",
             "norm_mix", "norm_mlp", "mlp_w1", "mlp_w2", "final_norm"]
    return (loss, dx[None], *[grad_out[k] for k in order], *[delta_out[k] for k in order],
            *[m_out[k] for k in order], *[v_out[k] for k in order])
```
